```python
import jax, jax.numpy as jnp
from jax import lax
import numpy as np

D_MODEL = 1024
BATCH = 8
SEQ = 2048
DEPTH = 4

CHUNK = 64
EPS = 1e-6
N_HEADS = 4
HEAD_DIM = 64
MIX_W = N_HEADS * HEAD_DIM
N_BRANCH = 4
ATT_LEFT_CHUNKS = 8
ATT_BAND = (ATT_LEFT_CHUNKS + 1) * CHUNK
REL_MAX = 256
REL_SIZE = REL_MAX + CHUNK
NEG_BIG = -1e30
HG_BLOCK = 16
LOG_FLOOR = 1e-30
GM_BLOCK = 128
GM_GROUPS = N_HEADS
GM_GROUP_W = MIX_W // GM_GROUPS
CONV_W = 4
LRU_C = 8.0
FFN_HIDDEN = ((8 * D_MODEL // 3 + 255) // 256) * 256
IN_SIZES = [MIX_W] * 11 + [N_BRANCH * D_MODEL]
IN_COLS = sum(IN_SIZES)

kernel_name = 'hybrid_chunk_causal_parallel_mixer'


def rms_norm(x, g):
    xf = x.astype(jnp.float32)
    y = xf * lax.rsqrt(jnp.mean(xf * xf, axis=-1, keepdims=True) + EPS)
    return (y * g.astype(jnp.float32)).astype(x.dtype)


def split_in(z):
    idx = np.cumsum(IN_SIZES)[:-1].tolist()
    return jnp.split(z, idx, axis=-1)


def chunk_band_attention(q, k, v, rel_bias):
    B, S, _ = q.shape
    nc = S // CHUNK
    L = ATT_LEFT_CHUNKS
    pad = L * CHUNK
    qc = q.reshape(B, nc, CHUNK, N_HEADS, HEAD_DIM)

    def band(t):
        tp = jnp.pad(t, ((0, 0), (pad, 0), (0, 0))).reshape(B, nc + L, CHUNK, N_HEADS, HEAD_DIM)
        return jnp.concatenate([tp[:, j:j + nc] for j in range(L + 1)], axis=2)

    kb, vb = band(k), band(v)
    s = jnp.einsum('bcqhd,bckhd->bchqk', qc, kb).astype(jnp.float32) * (HEAD_DIM ** -0.5)
    dist = pad + jnp.arange(CHUNK)[:, None] - jnp.arange(ATT_BAND)[None, :]
    idx = jnp.clip(dist, -(CHUNK - 1), REL_MAX) + (CHUNK - 1)
    bias = rel_bias.astype(jnp.float32)[:, idx]
    key_pos = jnp.arange(nc)[:, None] * CHUNK + jnp.arange(ATT_BAND)[None, :] - pad
    valid = key_pos >= 0
    s = jnp.where(valid[None, :, None, None, :], s + bias[None, None], NEG_BIG)
    p = jax.nn.softmax(s, axis=-1).astype(v.dtype)
    o = jnp.einsum('bchqk,bckhd->bcqhd', p, vb)
    return o.reshape(B, S, MIX_W)


def hgrn2(q, fz, i, g, lb, norm_g):
    B, S, _ = q.shape
    n = S // HG_BLOCK
    f32 = jnp.float32
    fz = fz.astype(f32)
    lb = lb.astype(f32)
    qf = jax.nn.silu(q.astype(f32))
    f = lb + (1.0 - lb) * jax.nn.sigmoid(fz)
    log_f = jnp.log(jnp.maximum(f, LOG_FLOOR))
    kf = (1.0 - lb) * jax.nn.sigmoid(-fz)
    shp = (B, n, HG_BLOCK, N_HEADS, HEAD_DIM)
    qf, kf, log_f = qf.reshape(shp), kf.reshape(shp), log_f.reshape(shp)
    vf = i.astype(f32).reshape(shp)
    b = jnp.cumsum(log_f, axis=2)
    causal = jnp.tril(jnp.ones((HG_BLOCK, HG_BLOCK), bool))[:, :, None, None]
    diff = b[:, :, :, None] - b[:, :, None, :]
    decay = jnp.where(causal, jnp.exp(jnp.where(causal, diff, 0.0)), 0.0)
    scores = jnp.einsum('bntshk,bnthk,bnshk->bnhts', decay, qf, kf)
    intra = jnp.einsum('bnhts,bnshv->bnthv', scores, vf)
    b_last = b[:, :, -1:]
    qd = qf * jnp.exp(b)
    kd = kf * jnp.exp(b_last - b)
    dec = jnp.exp(b_last[:, :, 0])

    def step(state, xs):
        qd_c, kd_c, v_c, dec_c = xs
        inter_c = jnp.einsum('bthk,bhkv->bthv', qd_c, state)
        state = dec_c[..., None] * state + jnp.einsum('bshk,bshv->bhkv', kd_c, v_c)
        return state, inter_c

    s0 = jnp.zeros((B, N_HEADS, HEAD_DIM, HEAD_DIM), f32)
    xs = (jnp.moveaxis(qd, 1, 0), jnp.moveaxis(kd, 1, 0), jnp.moveaxis(vf, 1, 0), jnp.moveaxis(dec, 1, 0))
    _, inter = lax.scan(step, s0, xs)
    o = (intra + jnp.moveaxis(inter, 0, 1)).reshape(B, S, N_HEADS, HEAD_DIM)
    o = o * lax.rsqrt(jnp.mean(o * o, axis=-1, keepdims=True) + EPS)
    o = o * norm_g.astype(f32).reshape(N_HEADS, HEAD_DIM)
    o = o.reshape(B, S, MIX_W) * jax.nn.silu(g.astype(f32))
    return o.astype(q.dtype)


def spatial_gating(u, v, norm_g, ws, bs):
    B, S, _ = u.shape
    n = S // GM_BLOCK
    vn = rms_norm(v, norm_g).reshape(B, n, GM_BLOCK, GM_GROUPS, GM_GROUP_W)
    w = ws * jnp.tril(jnp.ones((GM_BLOCK, GM_BLOCK), ws.dtype))
    mixed = jnp.einsum('gpq,bnqgc->bnpgc', w, vn) + bs.T[:, :, None]
    return u * mixed.reshape(B, S, MIX_W)


def rg_lru_branch(xin, gate, conv_w, conv_b, wa, ba, wx, bx, lam):
    B, S, _ = xin.shape
    f32 = jnp.float32
    xp = jnp.pad(xin, ((0, 0), (CONV_W - 1, 0), (0, 0)))
    xc = conv_b + xp[:, 0:S] * conv_w[0]
    for j in range(1, CONV_W):
        xc = xc + xp[:, j:j + S] * conv_w[j]
    xh = xc.reshape(B, S, N_HEADS, HEAD_DIM)
    r = jax.nn.sigmoid(jnp.einsum('bshi,hij->bshj', xh, wa).reshape(B, S, MIX_W) + ba)
    ig = jax.nn.sigmoid(jnp.einsum('bshi,hij->bshj', xh, wx).reshape(B, S, MIX_W) + bx)
    log_a = -LRU_C * r.astype(f32) * jax.nn.softplus(-lam.astype(f32))
    a = jnp.exp(log_a)
    mult = jnp.sqrt(jnp.maximum(-jnp.expm1(2.0 * log_a), 0.0))
    mult = jnp.where(jnp.arange(S)[None, :, None] == 0, 1.0, mult)
    bt = mult * (ig * xc).astype(f32)

    def combine(left, right):
        a1, b1 = left
        a2, b2 = right
        return a1 * a2, a2 * b1 + b2

    _, h = lax.associative_scan(combine, (a, bt), axis=1)
    return (h * jax.nn.gelu(gate.astype(f32))).astype(xin.dtype)


def hybrid_mixer(h, w_in, rel_bias, lb, hg_norm_g, gm_norm_g, gm_ws, gm_bs,
                 conv_w, conv_b, wa, ba, wx, bx, lam, w_branch, w_out):
    B, S, _ = h.shape
    z = h @ w_in
    aq, ak, av, bq, bf, bi, bg, cu, cv, dx, dg, gates = split_in(z)
    o_a = chunk_band_attention(aq, ak, av, rel_bias)
    o_b = hgrn2(bq, bf, bi, bg, lb, hg_norm_g)
    o_c = spatial_gating(jax.nn.gelu(cu), jax.nn.gelu(cv), gm_norm_g, gm_ws, gm_bs)
    o_d = rg_lru_branch(dx, dg, conv_w, conv_b, wa, ba, wx, bx, lam)
    outs = jnp.stack([o_a, o_b, o_c, o_d], axis=2)
    proj = jnp.einsum('bsnw,nwd->bsnd', outs, w_branch)
    g = jax.nn.sigmoid(gates.reshape(B, S, N_BRANCH, D_MODEL))
    merged = jnp.sum(g * proj, axis=2)
    return merged @ w_out


def swiglu(h, w1, w2):
    gt, up = jnp.split(h @ w1, 2, axis=-1)
    return (jax.nn.silu(gt) * up) @ w2


def _fwd_setup_inputs(seed: int = 0) -> dict:
    key = jax.random.key(seed)
    ks = jax.random.split(key, 24)
    f32 = jnp.float32

    def nrm(k, shape, scale):
        return jax.random.normal(k, shape, f32) * scale

    u = jax.random.uniform(ks[18], (DEPTH, MIX_W), f32, 0.9, 0.999)
    sa = u ** (1.0 / LRU_C)
    return {
        'x': nrm(ks[0], (BATCH, SEQ, D_MODEL), 1.0),
        'norm_mix_pre': 1.0 + nrm(ks[1], (DEPTH, D_MODEL), 0.02),
        'norm_mix_post': 1.0 + nrm(ks[2], (DEPTH, D_MODEL), 0.02),
        'norm_ffn_pre': 1.0 + nrm(ks[3], (DEPTH, D_MODEL), 0.02),
        'norm_ffn_post': 1.0 + nrm(ks[4], (DEPTH, D_MODEL), 0.02),
        'w_in': nrm(ks[5], (DEPTH, D_MODEL, IN_COLS), D_MODEL ** -0.5),
        'attn_rel_bias': nrm(ks[6], (DEPTH, N_HEADS, REL_SIZE), 0.1),
        'hgrn_lb_logits': nrm(ks[7], (DEPTH, MIX_W), 1.0),
        'hgrn_norm_g': 1.0 + nrm(ks[8], (DEPTH, MIX_W), 0.02),
        'gmlp_norm_g': 1.0 + nrm(ks[9], (DEPTH, MIX_W), 0.02),
        'gmlp_ws': nrm(ks[10], (DEPTH, GM_GROUPS, GM_BLOCK, GM_BLOCK), 0.5 * GM_BLOCK ** -0.5),
        'gmlp_bs': 1.0 + nrm(ks[11], (DEPTH, GM_GROUPS, GM_BLOCK), 0.01),
        'lru_conv_w': nrm(ks[12], (DEPTH, CONV_W, MIX_W), CONV_W ** -0.5),
        'lru_conv_b': nrm(ks[13], (DEPTH, MIX_W), 0.01),
        'lru_wa': nrm(ks[14], (DEPTH, N_HEADS, HEAD_DIM, HEAD_DIM), HEAD_DIM ** -0.5),
        'lru_ba': nrm(ks[15], (DEPTH, MIX_W), 0.01),
        'lru_wx': nrm(ks[16], (DEPTH, N_HEADS, HEAD_DIM, HEAD_DIM), HEAD_DIM ** -0.5),
        'lru_bx': nrm(ks[17], (DEPTH, MIX_W), 0.01),
        'lru_lambda': jnp.log(sa) - jnp.log1p(-sa),
        'w_branch': nrm(ks[19], (DEPTH, N_BRANCH, MIX_W, D_MODEL), MIX_W ** -0.5),
        'w_out': nrm(ks[20], (DEPTH, D_MODEL, D_MODEL), D_MODEL ** -0.5),
        'w_ffn_in': nrm(ks[21], (DEPTH, D_MODEL, 2 * FFN_HIDDEN), D_MODEL ** -0.5),
        'w_ffn_out': nrm(ks[22], (DEPTH, FFN_HIDDEN, D_MODEL), FFN_HIDDEN ** -0.5),
    }


def _fwd_reference(x, norm_mix_pre, norm_mix_post, norm_ffn_pre, norm_ffn_post, w_in,
              attn_rel_bias, hgrn_lb_logits, hgrn_norm_g, gmlp_norm_g, gmlp_ws, gmlp_bs,
              lru_conv_w, lru_conv_b, lru_wa, lru_ba, lru_wx, lru_bx, lru_lambda,
              w_branch, w_out, w_ffn_in, w_ffn_out):
    p = jax.nn.softmax(hgrn_lb_logits.astype(jnp.float32), axis=0)
    lbs = jnp.cumsum(p, axis=0) - p[0]
    for l in range(DEPTH):
        h = rms_norm(x, norm_mix_pre[l])
        y = hybrid_mixer(h, w_in[l], attn_rel_bias[l], lbs[l], hgrn_norm_g[l], gmlp_norm_g[l],
                         gmlp_ws[l], gmlp_bs[l], lru_conv_w[l], lru_conv_b[l], lru_wa[l],
                         lru_ba[l], lru_wx[l], lru_bx[l], lru_lambda[l], w_branch[l], w_out[l])
        x = x + rms_norm(y, norm_mix_post[l])
        h = rms_norm(x, norm_ffn_pre[l])
        x = x + rms_norm(swiglu(h, w_ffn_in[l], w_ffn_out[l]), norm_ffn_post[l])
    return x


import jax as _jax
import jax.numpy as _jnp

TWIN_FORMAT = 'train_step'
FWD_PARAMS = ['x', 'norm_mix_pre', 'norm_mix_post', 'norm_ffn_pre', 'norm_ffn_post', 'w_in', 'attn_rel_bias', 'hgrn_lb_logits', 'hgrn_norm_g', 'gmlp_norm_g', 'gmlp_ws', 'gmlp_bs', 'lru_conv_w', 'lru_conv_b', 'lru_wa', 'lru_ba', 'lru_wx', 'lru_bx', 'lru_lambda', 'w_branch', 'w_out', 'w_ffn_in', 'w_ffn_out']
TWIN_WEIGHTS = ['norm_mix_pre', 'norm_mix_post', 'norm_ffn_pre', 'norm_ffn_post', 'w_in', 'attn_rel_bias', 'hgrn_lb_logits', 'hgrn_norm_g', 'gmlp_norm_g', 'gmlp_ws', 'gmlp_bs', 'lru_conv_w', 'lru_conv_b', 'lru_wa', 'lru_ba', 'lru_wx', 'lru_bx', 'lru_lambda', 'w_branch', 'w_out', 'w_ffn_in', 'w_ffn_out']
TWIN_DIFF_INPUT = 'x'
TWIN_INPUTS = ['x', 'norm_mix_pre', 'norm_mix_post', 'norm_ffn_pre', 'norm_ffn_post', 'w_in', 'attn_rel_bias', 'hgrn_lb_logits', 'hgrn_norm_g', 'gmlp_norm_g', 'gmlp_ws', 'gmlp_bs', 'lru_conv_w', 'lru_conv_b', 'lru_wa', 'lru_ba', 'lru_wx', 'lru_bx', 'lru_lambda', 'w_branch', 'w_out', 'w_ffn_in', 'w_ffn_out', 'loss_target', 'm_norm_mix_pre', 'm_norm_mix_post', 'm_norm_ffn_pre', 'm_norm_ffn_post', 'm_w_in', 'm_attn_rel_bias', 'm_hgrn_lb_logits', 'm_hgrn_norm_g', 'm_gmlp_norm_g', 'm_gmlp_ws', 'm_gmlp_bs', 'm_lru_conv_w', 'm_lru_conv_b', 'm_lru_wa', 'm_lru_ba', 'm_lru_wx', 'm_lru_bx', 'm_lru_lambda', 'm_w_branch', 'm_w_out', 'm_w_ffn_in', 'm_w_ffn_out', 'v_norm_mix_pre', 'v_norm_mix_post', 'v_norm_ffn_pre', 'v_norm_ffn_post', 'v_w_in', 'v_attn_rel_bias', 'v_hgrn_lb_logits', 'v_hgrn_norm_g', 'v_gmlp_norm_g', 'v_gmlp_ws', 'v_gmlp_bs', 'v_lru_conv_w', 'v_lru_conv_b', 'v_lru_wa', 'v_lru_ba', 'v_lru_wx', 'v_lru_bx', 'v_lru_lambda', 'v_w_branch', 'v_w_out', 'v_w_ffn_in', 'v_w_ffn_out']
TWIN_OUTPUTS = ['loss', 'grad_x', 'grad_norm_mix_pre', 'grad_norm_mix_post', 'grad_norm_ffn_pre', 'grad_norm_ffn_post', 'grad_w_in', 'grad_attn_rel_bias', 'grad_hgrn_lb_logits', 'grad_hgrn_norm_g', 'grad_gmlp_norm_g', 'grad_gmlp_ws', 'grad_gmlp_bs', 'grad_lru_conv_w', 'grad_lru_conv_b', 'grad_lru_wa', 'grad_lru_ba', 'grad_lru_wx', 'grad_lru_bx', 'grad_lru_lambda', 'grad_w_branch', 'grad_w_out', 'grad_w_ffn_in', 'grad_w_ffn_out', 'delta_norm_mix_pre', 'delta_norm_mix_post', 'delta_norm_ffn_pre', 'delta_norm_ffn_post', 'delta_w_in', 'delta_attn_rel_bias', 'delta_hgrn_lb_logits', 'delta_hgrn_norm_g', 'delta_gmlp_norm_g', 'delta_gmlp_ws', 'delta_gmlp_bs', 'delta_lru_conv_w', 'delta_lru_conv_b', 'delta_lru_wa', 'delta_lru_ba', 'delta_lru_wx', 'delta_lru_bx', 'delta_lru_lambda', 'delta_w_branch', 'delta_w_out', 'delta_w_ffn_in', 'delta_w_ffn_out', 'new_m_norm_mix_pre', 'new_m_norm_mix_post', 'new_m_norm_ffn_pre', 'new_m_norm_ffn_post', 'new_m_w_in', 'new_m_attn_rel_bias', 'new_m_hgrn_lb_logits', 'new_m_hgrn_norm_g', 'new_m_gmlp_norm_g', 'new_m_gmlp_ws', 'new_m_gmlp_bs', 'new_m_lru_conv_w', 'new_m_lru_conv_b', 'new_m_lru_wa', 'new_m_lru_ba', 'new_m_lru_wx', 'new_m_lru_bx', 'new_m_lru_lambda', 'new_m_w_branch', 'new_m_w_out', 'new_m_w_ffn_in', 'new_m_w_ffn_out', 'new_v_norm_mix_pre', 'new_v_norm_mix_post', 'new_v_norm_ffn_pre', 'new_v_norm_ffn_post', 'new_v_w_in', 'new_v_attn_rel_bias', 'new_v_hgrn_lb_logits', 'new_v_hgrn_norm_g', 'new_v_gmlp_norm_g', 'new_v_gmlp_ws', 'new_v_gmlp_bs', 'new_v_lru_conv_w', 'new_v_lru_conv_b', 'new_v_lru_wa', 'new_v_lru_ba', 'new_v_lru_wx', 'new_v_lru_bx', 'new_v_lru_lambda', 'new_v_w_branch', 'new_v_w_out', 'new_v_w_ffn_in', 'new_v_w_ffn_out']
TWIN_LEAF_KINDS = {'loss': 'loss', 'grad_x': 'grad_x', 'grad_norm_mix_pre': 'grad_w', 'grad_norm_mix_post': 'grad_w', 'grad_norm_ffn_pre': 'grad_w', 'grad_norm_ffn_post': 'grad_w', 'grad_w_in': 'grad_w', 'grad_attn_rel_bias': 'grad_w', 'grad_hgrn_lb_logits': 'grad_w', 'grad_hgrn_norm_g': 'grad_w', 'grad_gmlp_norm_g': 'grad_w', 'grad_gmlp_ws': 'grad_w', 'grad_gmlp_bs': 'grad_w', 'grad_lru_conv_w': 'grad_w', 'grad_lru_conv_b': 'grad_w', 'grad_lru_wa': 'grad_w', 'grad_lru_ba': 'grad_w', 'grad_lru_wx': 'grad_w', 'grad_lru_bx': 'grad_w', 'grad_lru_lambda': 'grad_w', 'grad_w_branch': 'grad_w', 'grad_w_out': 'grad_w', 'grad_w_ffn_in': 'grad_w', 'grad_w_ffn_out': 'grad_w', 'delta_norm_mix_pre': 'delta_w', 'delta_norm_mix_post': 'delta_w', 'delta_norm_ffn_pre': 'delta_w', 'delta_norm_ffn_post': 'delta_w', 'delta_w_in': 'delta_w', 'delta_attn_rel_bias': 'delta_w', 'delta_hgrn_lb_logits': 'delta_w', 'delta_hgrn_norm_g': 'delta_w', 'delta_gmlp_norm_g': 'delta_w', 'delta_gmlp_ws': 'delta_w', 'delta_gmlp_bs': 'delta_w', 'delta_lru_conv_w': 'delta_w', 'delta_lru_conv_b': 'delta_w', 'delta_lru_wa': 'delta_w', 'delta_lru_ba': 'delta_w', 'delta_lru_wx': 'delta_w', 'delta_lru_bx': 'delta_w', 'delta_lru_lambda': 'delta_w', 'delta_w_branch': 'delta_w', 'delta_w_out': 'delta_w', 'delta_w_ffn_in': 'delta_w', 'delta_w_ffn_out': 'delta_w', 'new_m_norm_mix_pre': 'new_m', 'new_m_norm_mix_post': 'new_m', 'new_m_norm_ffn_pre': 'new_m', 'new_m_norm_ffn_post': 'new_m', 'new_m_w_in': 'new_m', 'new_m_attn_rel_bias': 'new_m', 'new_m_hgrn_lb_logits': 'new_m', 'new_m_hgrn_norm_g': 'new_m', 'new_m_gmlp_norm_g': 'new_m', 'new_m_gmlp_ws': 'new_m', 'new_m_gmlp_bs': 'new_m', 'new_m_lru_conv_w': 'new_m', 'new_m_lru_conv_b': 'new_m', 'new_m_lru_wa': 'new_m', 'new_m_lru_ba': 'new_m', 'new_m_lru_wx': 'new_m', 'new_m_lru_bx': 'new_m', 'new_m_lru_lambda': 'new_m', 'new_m_w_branch': 'new_m', 'new_m_w_out': 'new_m', 'new_m_w_ffn_in': 'new_m', 'new_m_w_ffn_out': 'new_m', 'new_v_norm_mix_pre': 'new_v', 'new_v_norm_mix_post': 'new_v', 'new_v_norm_ffn_pre': 'new_v', 'new_v_norm_ffn_post': 'new_v', 'new_v_w_in': 'new_v', 'new_v_attn_rel_bias': 'new_v', 'new_v_hgrn_lb_logits': 'new_v', 'new_v_hgrn_norm_g': 'new_v', 'new_v_gmlp_norm_g': 'new_v', 'new_v_gmlp_ws': 'new_v', 'new_v_gmlp_bs': 'new_v', 'new_v_lru_conv_w': 'new_v', 'new_v_lru_conv_b': 'new_v', 'new_v_lru_wa': 'new_v', 'new_v_lru_ba': 'new_v', 'new_v_lru_wx': 'new_v', 'new_v_lru_bx': 'new_v', 'new_v_lru_lambda': 'new_v', 'new_v_w_branch': 'new_v', 'new_v_w_out': 'new_v', 'new_v_w_ffn_in': 'new_v', 'new_v_w_ffn_out': 'new_v'}


def _forward(args):
    return _fwd_reference(*[args[k] for k in FWD_PARAMS])


def _output_shape():
    out = _jax.eval_shape(lambda: _forward(_fwd_setup_inputs(0)))
    return out.shape, out.dtype

N_MICROBATCH = 1
ADAM_LR = 0.001
ADAM_B1 = 0.9
ADAM_B2 = 0.999
ADAM_EPS = 1e-08
ADAM_WD = 0.01
ADAM_STEP = 10
PER_EXAMPLE_BATCH_AXIS = {'x': 0, 'loss_target': 0}
SHARED_INPUTS = []
_WEIGHT_DTYPES = {'norm_mix_pre': _jnp.float32, 'norm_mix_post': _jnp.float32, 'norm_ffn_pre': _jnp.float32, 'norm_ffn_post': _jnp.float32, 'w_in': _jnp.float32, 'attn_rel_bias': _jnp.float32, 'hgrn_lb_logits': _jnp.float32, 'hgrn_norm_g': _jnp.float32, 'gmlp_norm_g': _jnp.float32, 'gmlp_ws': _jnp.float32, 'gmlp_bs': _jnp.float32, 'lru_conv_w': _jnp.float32, 'lru_conv_b': _jnp.float32, 'lru_wa': _jnp.float32, 'lru_ba': _jnp.float32, 'lru_wx': _jnp.float32, 'lru_bx': _jnp.float32, 'lru_lambda': _jnp.float32, 'w_branch': _jnp.float32, 'w_out': _jnp.float32, 'w_ffn_in': _jnp.float32, 'w_ffn_out': _jnp.float32}
MOMENT_SCALE = {'norm_mix_pre': 3.157828e+00, 'norm_mix_post': 1.718342e+01, 'norm_ffn_pre': 1.888080e+00, 'norm_ffn_post': 1.602234e+01, 'w_in': 1.220795e+00, 'attn_rel_bias': 1.019986e-01, 'hgrn_lb_logits': 7.021076e-02, 'hgrn_norm_g': 1.763407e+00, 'gmlp_norm_g': 3.864412e-01, 'gmlp_ws': 5.318077e-01, 'gmlp_bs': 7.545919e-01, 'lru_conv_w': 5.316815e+00, 'lru_conv_b': 3.268491e+01, 'lru_wa': 1.092201e+00, 'lru_ba': 8.207258e-01, 'lru_wx': 2.067947e+00, 'lru_bx': 1.945038e+00, 'lru_lambda': 1.700129e+00, 'w_branch': 2.439272e+00, 'w_out': 4.996281e+00, 'w_ffn_in': 7.587487e-01, 'w_ffn_out': 1.493086e+00}


def _to_microbatches(a, axis):
    t = _jnp.moveaxis(a, axis, 0)
    t = t.reshape((N_MICROBATCH, t.shape[0] // N_MICROBATCH) + t.shape[1:])
    return _jnp.moveaxis(t, 1, axis + 1)


def setup_inputs(seed: int = 0) -> dict:
    inp = _fwd_setup_inputs(seed)
    key = _jax.random.fold_in(_jax.random.key(seed), 7919)
    shape, _ = _output_shape()
    out = dict(inp)
    out["loss_target"] = _jax.random.normal(_jax.random.fold_in(key, 0), shape, _jnp.float32)
    for i, name in enumerate(TWIN_WEIGHTS):
        w = inp[name].astype(_jnp.float32)
        if MOMENT_SCALE is None:
            s = _jnp.sqrt(_jnp.mean(_jnp.square(w)) + 1e-30)
        else:
            s = MOMENT_SCALE[name]
        km, kv = _jax.random.split(_jax.random.fold_in(key, i + 1))
        out[name] = w
        out["m_" + name] = s * _jax.random.normal(km, w.shape, _jnp.float32)
        out["v_" + name] = (s * s) * _jax.random.uniform(kv, w.shape, _jnp.float32, 0.5, 1.5)
    if N_MICROBATCH > 1:
        for name, axis in PER_EXAMPLE_BATCH_AXIS.items():
            out[name] = _to_microbatches(out[name], axis)
    return {'x': out['x'], 'norm_mix_pre': out['norm_mix_pre'], 'norm_mix_post': out['norm_mix_post'], 'norm_ffn_pre': out['norm_ffn_pre'], 'norm_ffn_post': out['norm_ffn_post'], 'w_in': out['w_in'], 'attn_rel_bias': out['attn_rel_bias'], 'hgrn_lb_logits': out['hgrn_lb_logits'], 'hgrn_norm_g': out['hgrn_norm_g'], 'gmlp_norm_g': out['gmlp_norm_g'], 'gmlp_ws': out['gmlp_ws'], 'gmlp_bs': out['gmlp_bs'], 'lru_conv_w': out['lru_conv_w'], 'lru_conv_b': out['lru_conv_b'], 'lru_wa': out['lru_wa'], 'lru_ba': out['lru_ba'], 'lru_wx': out['lru_wx'], 'lru_bx': out['lru_bx'], 'lru_lambda': out['lru_lambda'], 'w_branch': out['w_branch'], 'w_out': out['w_out'], 'w_ffn_in': out['w_ffn_in'], 'w_ffn_out': out['w_ffn_out'], 'loss_target': out['loss_target'], 'm_norm_mix_pre': out['m_norm_mix_pre'], 'm_norm_mix_post': out['m_norm_mix_post'], 'm_norm_ffn_pre': out['m_norm_ffn_pre'], 'm_norm_ffn_post': out['m_norm_ffn_post'], 'm_w_in': out['m_w_in'], 'm_attn_rel_bias': out['m_attn_rel_bias'], 'm_hgrn_lb_logits': out['m_hgrn_lb_logits'], 'm_hgrn_norm_g': out['m_hgrn_norm_g'], 'm_gmlp_norm_g': out['m_gmlp_norm_g'], 'm_gmlp_ws': out['m_gmlp_ws'], 'm_gmlp_bs': out['m_gmlp_bs'], 'm_lru_conv_w': out['m_lru_conv_w'], 'm_lru_conv_b': out['m_lru_conv_b'], 'm_lru_wa': out['m_lru_wa'], 'm_lru_ba': out['m_lru_ba'], 'm_lru_wx': out['m_lru_wx'], 'm_lru_bx': out['m_lru_bx'], 'm_lru_lambda': out['m_lru_lambda'], 'm_w_branch': out['m_w_branch'], 'm_w_out': out['m_w_out'], 'm_w_ffn_in': out['m_w_ffn_in'], 'm_w_ffn_out': out['m_w_ffn_out'], 'v_norm_mix_pre': out['v_norm_mix_pre'], 'v_norm_mix_post': out['v_norm_mix_post'], 'v_norm_ffn_pre': out['v_norm_ffn_pre'], 'v_norm_ffn_post': out['v_norm_ffn_post'], 'v_w_in': out['v_w_in'], 'v_attn_rel_bias': out['v_attn_rel_bias'], 'v_hgrn_lb_logits': out['v_hgrn_lb_logits'], 'v_hgrn_norm_g': out['v_hgrn_norm_g'], 'v_gmlp_norm_g': out['v_gmlp_norm_g'], 'v_gmlp_ws': out['v_gmlp_ws'], 'v_gmlp_bs': out['v_gmlp_bs'], 'v_lru_conv_w': out['v_lru_conv_w'], 'v_lru_conv_b': out['v_lru_conv_b'], 'v_lru_wa': out['v_lru_wa'], 'v_lru_ba': out['v_lru_ba'], 'v_lru_wx': out['v_lru_wx'], 'v_lru_bx': out['v_lru_bx'], 'v_lru_lambda': out['v_lru_lambda'], 'v_w_branch': out['v_w_branch'], 'v_w_out': out['v_w_out'], 'v_w_ffn_in': out['v_w_ffn_in'], 'v_w_ffn_out': out['v_w_ffn_out']}


def _loss(weights, diff, rest, loss_target):
    with _jax.named_scope("forward"):
        args = {**rest, TWIN_DIFF_INPUT: diff, **{k: w.astype(_WEIGHT_DTYPES[k]) for k, w in weights.items()}}
        y = _forward(args)
    with _jax.named_scope("loss_head"):
        err = _jnp.square(y.astype(_jnp.float32) - loss_target)
        return 0.5 * _jnp.sum(_jnp.mean(err, axis=-1)) if err.ndim else 0.5 * err


def _adamw(w, g, m, v):
    m = ADAM_B1 * m + (1.0 - ADAM_B1) * g
    v = ADAM_B2 * v + (1.0 - ADAM_B2) * _jnp.square(g)
    m_hat = m / (1.0 - ADAM_B1 ** ADAM_STEP)
    v_hat = v / (1.0 - ADAM_B2 ** ADAM_STEP)
    delta = -ADAM_LR * (m_hat / (_jnp.sqrt(v_hat) + ADAM_EPS) + ADAM_WD * w)
    return delta, m, v


def reference(x, norm_mix_pre, norm_mix_post, norm_ffn_pre, norm_ffn_post, w_in, attn_rel_bias, hgrn_lb_logits, hgrn_norm_g, gmlp_norm_g, gmlp_ws, gmlp_bs, lru_conv_w, lru_conv_b, lru_wa, lru_ba, lru_wx, lru_bx, lru_lambda, w_branch, w_out, w_ffn_in, w_ffn_out, loss_target, m_norm_mix_pre, m_norm_mix_post, m_norm_ffn_pre, m_norm_ffn_post, m_w_in, m_attn_rel_bias, m_hgrn_lb_logits, m_hgrn_norm_g, m_gmlp_norm_g, m_gmlp_ws, m_gmlp_bs, m_lru_conv_w, m_lru_conv_b, m_lru_wa, m_lru_ba, m_lru_wx, m_lru_bx, m_lru_lambda, m_w_branch, m_w_out, m_w_ffn_in, m_w_ffn_out, v_norm_mix_pre, v_norm_mix_post, v_norm_ffn_pre, v_norm_ffn_post, v_w_in, v_attn_rel_bias, v_hgrn_lb_logits, v_hgrn_norm_g, v_gmlp_norm_g, v_gmlp_ws, v_gmlp_bs, v_lru_conv_w, v_lru_conv_b, v_lru_wa, v_lru_ba, v_lru_wx, v_lru_bx, v_lru_lambda, v_w_branch, v_w_out, v_w_ffn_in, v_w_ffn_out):
    given = dict(x=x, norm_mix_pre=norm_mix_pre, norm_mix_post=norm_mix_post, norm_ffn_pre=norm_ffn_pre, norm_ffn_post=norm_ffn_post, w_in=w_in, attn_rel_bias=attn_rel_bias, hgrn_lb_logits=hgrn_lb_logits, hgrn_norm_g=hgrn_norm_g, gmlp_norm_g=gmlp_norm_g, gmlp_ws=gmlp_ws, gmlp_bs=gmlp_bs, lru_conv_w=lru_conv_w, lru_conv_b=lru_conv_b, lru_wa=lru_wa, lru_ba=lru_ba, lru_wx=lru_wx, lru_bx=lru_bx, lru_lambda=lru_lambda, w_branch=w_branch, w_out=w_out, w_ffn_in=w_ffn_in, w_ffn_out=w_ffn_out, loss_target=loss_target, m_norm_mix_pre=m_norm_mix_pre, m_norm_mix_post=m_norm_mix_post, m_norm_ffn_pre=m_norm_ffn_pre, m_norm_ffn_post=m_norm_ffn_post, m_w_in=m_w_in, m_attn_rel_bias=m_attn_rel_bias, m_hgrn_lb_logits=m_hgrn_lb_logits, m_hgrn_norm_g=m_hgrn_norm_g, m_gmlp_norm_g=m_gmlp_norm_g, m_gmlp_ws=m_gmlp_ws, m_gmlp_bs=m_gmlp_bs, m_lru_conv_w=m_lru_conv_w, m_lru_conv_b=m_lru_conv_b, m_lru_wa=m_lru_wa, m_lru_ba=m_lru_ba, m_lru_wx=m_lru_wx, m_lru_bx=m_lru_bx, m_lru_lambda=m_lru_lambda, m_w_branch=m_w_branch, m_w_out=m_w_out, m_w_ffn_in=m_w_ffn_in, m_w_ffn_out=m_w_ffn_out, v_norm_mix_pre=v_norm_mix_pre, v_norm_mix_post=v_norm_mix_post, v_norm_ffn_pre=v_norm_ffn_pre, v_norm_ffn_post=v_norm_ffn_post, v_w_in=v_w_in, v_attn_rel_bias=v_attn_rel_bias, v_hgrn_lb_logits=v_hgrn_lb_logits, v_hgrn_norm_g=v_hgrn_norm_g, v_gmlp_norm_g=v_gmlp_norm_g, v_gmlp_ws=v_gmlp_ws, v_gmlp_bs=v_gmlp_bs, v_lru_conv_w=v_lru_conv_w, v_lru_conv_b=v_lru_conv_b, v_lru_wa=v_lru_wa, v_lru_ba=v_lru_ba, v_lru_wx=v_lru_wx, v_lru_bx=v_lru_bx, v_lru_lambda=v_lru_lambda, v_w_branch=v_w_branch, v_w_out=v_w_out, v_w_ffn_in=v_w_ffn_in, v_w_ffn_out=v_w_ffn_out)
    weights = {n: given[n] for n in TWIN_WEIGHTS}
    shared = {n: given[n] for n in SHARED_INPUTS}
    per_example = {n: given[n] for n in ['x']}
    grad_fn = _jax.value_and_grad(_loss, argnums=(0, 1))

    def one_microbatch(ex, loss_target):
        ex = dict(ex)
        diff = ex.pop(TWIN_DIFF_INPUT)
        return grad_fn(weights, diff, {**shared, **ex}, loss_target)

    if N_MICROBATCH == 1:
        loss, (grad_w, grad_x) = one_microbatch(per_example, given["loss_target"])
    else:
        def body(carry, xs):
            loss_sum, grad_sum = carry
            l_k, (gw_k, gx_k) = one_microbatch(xs[0], xs[1])
            with _jax.named_scope("update"):
                return (loss_sum + l_k, _jax.tree.map(_jnp.add, grad_sum, gw_k)), gx_k

        init = (_jnp.zeros((), _jnp.float32), _jax.tree.map(_jnp.zeros_like, weights))
        (loss, grad_w), grad_x = _jax.lax.scan(body, init, (per_example, given["loss_target"]))
    with _jax.named_scope("update"):
        delta_w, new_m, new_v = {}, {}, {}
        for n in TWIN_WEIGHTS:
            delta_w[n], new_m[n], new_v[n] = _adamw(weights[n], grad_w[n], given["m_" + n], given["v_" + n])
    return (loss, grad_x, *[grad_w[n] for n in TWIN_WEIGHTS], *[delta_w[n] for n in TWIN_WEIGHTS],
            *[new_m[n] for n in TWIN_WEIGHTS], *[new_v[n] for n in TWIN_WEIGHTS])
```

```python
import math

import jax
import jax.numpy as jnp
from jax import lax
from jax.experimental import pallas as pl
from jax.experimental.pallas import tpu as pltpu

f32 = jnp.float32
bf16 = jnp.bfloat16

SEQ = 2048
DM = 1024
DEPTH = 4
NDEV = 8
MIXW = 256
NHEAD = 4
HDIM = 64
NMIX = 11 * MIXW
NGATE = 4 * DM
FFH = 2816
EPS = 1e-6
NEG_BIG = -1e30
LOG_FLOOR = 1e-30
LRU_C = 8.0
REL_SIZE = 320
ATT_PAIR = 128
ATT_BAND = 640
ATT_PAD = 512
ATT_WV = 768
HG_T = 16
HG_N = SEQ // HG_T
GM_T = 128
LRU_T = 128
ADAM_LR, ADAM_B1, ADAM_B2, ADAM_EPS, ADAM_WD, ADAM_STEP = 0.001, 0.9, 0.999, 1e-8, 0.01, 10
V7X_VMEM_LIMIT = 56 * 1024 * 1024
GELU_C0 = math.sqrt(2.0 / math.pi)
GELU_C1 = 0.044715
MESH_ID = pl.DeviceIdType.MESH


def _params(sem=None):
    if sem is None:
        return pltpu.CompilerParams(vmem_limit_bytes=V7X_VMEM_LIMIT)
    return pltpu.CompilerParams(dimension_semantics=sem, vmem_limit_bytes=V7X_VMEM_LIMIT)


def _sds(shape, dtype):
    return jax.ShapeDtypeStruct(shape, dtype)


def _dot(a, b):
    return jnp.dot(a.astype(bf16), b.astype(bf16), preferred_element_type=f32)


def _dot_nt(a, b):
    return lax.dot_general(a.astype(bf16), b.astype(bf16), (((1,), (1,)), ((), ())), preferred_element_type=f32)


def _dot_tn(a, b):
    return lax.dot_general(a.astype(bf16), b.astype(bf16), (((0,), (0,)), ((), ())), preferred_element_type=f32)


def _split(a):
    hi = a.astype(bf16)
    lo = (a - hi.astype(f32)).astype(bf16)
    return hi, lo


def _dot_hl(a, m):
    hi, lo = _split(a)
    return jnp.dot(hi, m, preferred_element_type=f32) + jnp.dot(lo, m, preferred_element_type=f32)


def _dot_nt_hl(m, a):
    hi, lo = _split(a)
    dn = (((1,), (1,)), ((), ()))
    return lax.dot_general(m, hi, dn, preferred_element_type=f32) + lax.dot_general(m, lo, dn, preferred_element_type=f32)


def _sigmoid(x):
    return jax.nn.sigmoid(x)


def _silu(x):
    return x * _sigmoid(x)


def _dsilu(x):
    s = _sigmoid(x)
    return s * (1.0 + x * (1.0 - s))


def _gelu(x):
    return 0.5 * x * (1.0 + jnp.tanh(GELU_C0 * (x + GELU_C1 * x * x * x)))


def _dgelu(x):
    t = jnp.tanh(GELU_C0 * (x + GELU_C1 * x * x * x))
    return 0.5 * (1.0 + t) + 0.5 * x * (1.0 - t * t) * GELU_C0 * (1.0 + 3.0 * GELU_C1 * x * x)


def _rms(x, g):
    r = lax.rsqrt(jnp.mean(x * x, axis=-1, keepdims=True) + EPS)
    return x * r * g


def _rms_bwd(x, g, dy):
    r = lax.rsqrt(jnp.mean(x * x, axis=-1, keepdims=True) + EPS)
    xh = x * r
    dxh = dy * g
    dx = r * (dxh - xh * jnp.mean(dxh * xh, axis=-1, keepdims=True))
    return dx, jnp.sum(dy * xh, axis=0, keepdims=True)


def _same_head(n, width, dtype):
    r = lax.broadcasted_iota(jnp.int32, (n, n), 0) // width
    c = lax.broadcasted_iota(jnp.int32, (n, n), 1) // width
    return (r == c).astype(dtype)


def _head_masks(rows=1):
    lane = lax.broadcasted_iota(jnp.int32, (rows, MIXW), 1) // HDIM
    return [lane == h for h in range(NHEAD)]


def _norm_matmul(x, g, w, tn):
    n = w.shape[1]
    tm = 1024

    def body(x_ref, g_ref, w_ref, z_ref, h_ref):
        @pl.when(pl.program_id(1) == 0)
        def _():
            h_ref[...] = _rms(x_ref[...], g_ref[...]).astype(bf16)

        z_ref[...] = jnp.dot(h_ref[...], w_ref[...], preferred_element_type=f32)

    return pl.pallas_call(
        body, name="norm_matmul", grid=(SEQ // tm, n // tn),
        in_specs=[pl.BlockSpec((tm, DM), lambda i, j: (i, 0)), pl.BlockSpec((1, DM), lambda i, j: (0, 0)),
                  pl.BlockSpec((DM, tn), lambda i, j: (0, j))],
        out_specs=[pl.BlockSpec((tm, tn), lambda i, j: (i, j)), pl.BlockSpec((tm, DM), lambda i, j: (i, 0))],
        out_shape=[_sds((SEQ, n), f32), _sds((SEQ, DM), bf16)],
        compiler_params=_params(("parallel", "arbitrary")),
    )(x, g, w)


def _matmul(a, w, tn):
    k, n = w.shape
    tm = 1024

    def body(a_ref, w_ref, z_ref):
        z_ref[...] = jnp.dot(a_ref[...], w_ref[...], preferred_element_type=f32)

    return pl.pallas_call(
        body, name="matmul", grid=(SEQ // tm, n // tn),
        in_specs=[pl.BlockSpec((tm, k), lambda i, j: (i, 0)), pl.BlockSpec((k, tn), lambda i, j: (0, j))],
        out_specs=pl.BlockSpec((tm, tn), lambda i, j: (i, j)),
        out_shape=_sds((SEQ, n), f32),
        compiler_params=_params(("parallel", "arbitrary")),
    )(a, w)


def _att_offset_map():
    i = lax.broadcasted_iota(jnp.int32, (REL_SIZE, ATT_WV), 0)
    t = lax.broadcasted_iota(jnp.int32, (REL_SIZE, ATT_WV), 1)
    e = jnp.where(t <= ATT_BAND, t, t - ATT_WV)
    idx = jnp.clip(ATT_PAD - e, -(HDIM - 1), 256) + (HDIM - 1)
    return (idx == i).astype(bf16)


def _att_band_valid():
    qc = lax.broadcasted_iota(jnp.int32, (ATT_PAIR, ATT_BAND), 0) // HDIM
    kc = lax.broadcasted_iota(jnp.int32, (ATT_PAIR, ATT_BAND), 1) // HDIM
    return (kc >= qc) & (kc <= qc + 8)


def _att_bias_tiles(rb_ref, bm_ref):
    wv = _dot_hl(rb_ref[...], _att_offset_map())
    valid = _att_band_valid()
    for h in range(NHEAD):
        rows = jnp.broadcast_to(wv[h:h + 1, :], (ATT_PAIR, ATT_WV))
        tile = pltpu.roll(rows, 0, 1, stride=1, stride_axis=0)[:, :ATT_BAND]
        bm_ref[h] = jnp.where(valid, tile, NEG_BIG)


def _att_pad_kv(k_ref, v_ref, kp_ref, vp_ref):
    kp_ref[pl.ds(0, ATT_PAD), :] = jnp.zeros((ATT_PAD, MIXW), bf16)
    vp_ref[pl.ds(0, ATT_PAD), :] = jnp.zeros((ATT_PAD, MIXW), bf16)
    kp_ref[pl.ds(ATT_PAD, SEQ), :] = k_ref[...].astype(bf16)
    vp_ref[pl.ds(ATT_PAD, SEQ), :] = v_ref[...].astype(bf16)


def _att_probs(qm, kb, bm, key_ok):
    s = _dot_nt(qm, kb) + bm
    s = jnp.where(key_ok, s, NEG_BIG)
    m = jnp.max(s, axis=-1, keepdims=True)
    e = jnp.exp(s - m)
    return e / jnp.sum(e, axis=-1, keepdims=True)


def _attn_fwd(zm, rb8):
    def body(q_ref, k_ref, v_ref, rb_ref, o_ref, kp_ref, vp_ref, bm_ref):
        _att_pad_kv(k_ref, v_ref, kp_ref, vp_ref)
        _att_bias_tiles(rb_ref, bm_ref)
        hm = _head_masks()

        def pair(p, carry):
            r0 = pl.multiple_of(p * ATT_PAIR, ATT_PAIR)
            q = q_ref[pl.ds(r0, ATT_PAIR), :] * (HDIM ** -0.5)
            kb = kp_ref[pl.ds(r0, ATT_BAND), :]
            vb = vp_ref[pl.ds(r0, ATT_BAND), :]
            key_ok = (lax.broadcasted_iota(jnp.int32, (1, ATT_BAND), 1) + (r0 - ATT_PAD)) >= 0
            o = jnp.zeros((ATT_PAIR, MIXW), f32)
            for h in range(NHEAD):
                qm = jnp.where(hm[h], q, 0.0)
                p_h = _att_probs(qm, kb, bm_ref[h], key_ok)
                o = o + jnp.where(hm[h], _dot(p_h, vb), 0.0)
            o_ref[pl.ds(r0, ATT_PAIR), :] = o.astype(bf16)
            return carry

        lax.fori_loop(0, SEQ // ATT_PAIR, pair, 0)

    col = lambda j: pl.BlockSpec((SEQ, MIXW), lambda i: (0, j))
    return pl.pallas_call(
        body, name="attn_fwd", grid=(1,),
        in_specs=[col(0), col(1), col(2), pl.BlockSpec((8, REL_SIZE), lambda i: (0, 0))],
        out_specs=pl.BlockSpec((SEQ, MIXW), lambda i: (0, 0)),
        out_shape=_sds((SEQ, MIXW), bf16),
        scratch_shapes=[pltpu.VMEM((SEQ + ATT_PAD, MIXW), bf16), pltpu.VMEM((SEQ + ATT_PAD, MIXW), bf16),
                        pltpu.VMEM((NHEAD, ATT_PAIR, ATT_BAND), f32)],
        compiler_params=_params(("arbitrary",)),
    )(zm, zm, zm, rb8)


def _hg_gates(q, fz, lb):
    sq = _sigmoid(q)
    sg = _sigmoid(fz)
    f = lb + (1.0 - lb) * sg
    return q * sq, (1.0 - lb) * (1.0 - sg), jnp.log(jnp.maximum(f, LOG_FLOOR)), sq, sg, f


def _hg_prepare(q_ref, f_ref, lb, qf_s, kf_s, b_s, qd_s, kd_s, dec_s):
    b = None
    for t in range(HG_T):
        qf, kf, lf, _, _, _ = _hg_gates(q_ref[:, t, :], f_ref[:, t, :], lb)
        b = lf if b is None else b + lf
        qf_s[:, t, :] = qf
        kf_s[:, t, :] = kf
        b_s[:, t, :] = b
    b_last = b
    dec_s[...] = jnp.broadcast_to(jnp.exp(b_last)[:, None, :], (HG_N, 8, MIXW))
    for t in range(HG_T):
        bt = b_s[:, t, :]
        qd_s[:, t, :] = qf_s[:, t, :] * jnp.exp(bt)
        kd_s[:, t, :] = kf_s[:, t, :] * jnp.exp(b_last - bt)


def _hg_scores(t, qf_s, kf_s, b_s, w_s, hm):
    qt = qf_s[:, t, :]
    bt = b_s[:, t, :]
    for s in range(t + 1):
        w = qt * kf_s[:, s, :]
        if s < t:
            w = w * jnp.exp(bt - b_s[:, s, :])
        w_s[pl.ds(s * HG_N, HG_N), :] = w.astype(bf16)
    return jnp.dot(w_s[pl.ds(0, (t + 1) * HG_N), :], hm, preferred_element_type=f32)


def _hgrn_fwd(zm3, lb, ng):
    def body(q_ref, f_ref, i_ref, g_ref, lb_ref, ng_ref, o_ref, oraw_ref,
             qf_s, kf_s, b_s, qd_s, kd_s, dec_s, w_s, st_s):
        lb = lb_ref[...]
        hm = _same_head(MIXW, HDIM, bf16)
        hmf = _same_head(MIXW, HDIM, f32)
        _hg_prepare(q_ref, f_ref, lb, qf_s, kf_s, b_s, qd_s, kd_s, dec_s)
        for t in range(HG_T):
            p = _hg_scores(t, qf_s, kf_s, b_s, w_s, hm)
            acc = jnp.zeros((HG_N, MIXW), f32)
            for s in range(t + 1):
                acc = acc + p[s * HG_N:(s + 1) * HG_N] * i_ref[:, s, :]
            oraw_ref[:, t, :] = acc
        st_s[...] = jnp.zeros((MIXW, MIXW), f32)

        def step(n, carry):
            st = st_s[...]
            oraw_ref[n] = oraw_ref[n] + _dot_nt(qd_s[n], st)
            st_s[...] = st * dec_s[n][0:1] + _dot_tn(i_ref[n], kd_s[n]) * hmf
            return carry

        lax.fori_loop(0, HG_N, step, 0)
        ngv = ng_ref[...]
        for t in range(HG_T):
            o = oraw_ref[:, t, :]
            ms = _dot_hl(o * o, hm) * (1.0 / HDIM)
            o_ref[:, t, :] = (o * lax.rsqrt(ms + EPS) * ngv * _silu(g_ref[:, t, :])).astype(bf16)

    col = lambda j: pl.BlockSpec((HG_N, HG_T, MIXW), lambda i: (0, 0, j))
    vec = pl.BlockSpec((1, MIXW), lambda i: (0, 0))
    blk = pl.BlockSpec((HG_N, HG_T, MIXW), lambda i: (0, 0, 0))
    s3 = pltpu.VMEM((HG_N, HG_T, MIXW), f32)
    return pl.pallas_call(
        body, name="hgrn_fwd", grid=(1,),
        in_specs=[col(3), col(4), col(5), col(6), vec, vec],
        out_specs=[blk, blk],
        out_shape=[_sds((HG_N, HG_T, MIXW), bf16), _sds((HG_N, HG_T, MIXW), f32)],
        scratch_shapes=[s3, s3, s3, s3, s3, pltpu.VMEM((HG_N, 8, MIXW), f32),
                        pltpu.VMEM((HG_T * HG_N, MIXW), bf16), pltpu.VMEM((MIXW, MIXW), f32)],
        compiler_params=_params(("arbitrary",)),
    )(zm3, zm3, zm3, zm3, lb, ng)


def _gm_weights(ws_ref):
    tril = lax.broadcasted_iota(jnp.int32, (GM_T, GM_T), 0) >= lax.broadcasted_iota(jnp.int32, (GM_T, GM_T), 1)
    return tril, [jnp.where(tril, ws_ref[g], 0.0).astype(bf16) for g in range(NHEAD)]


def _gm_expand():
    r = lax.broadcasted_iota(jnp.int32, (8, MIXW), 0)
    c = lax.broadcasted_iota(jnp.int32, (8, MIXW), 1) // HDIM
    return (r == c).astype(bf16)


def _gm_mixed(vn, wts, bias, hm):
    vb = vn.astype(bf16)
    mixed = bias
    for g in range(NHEAD):
        mixed = mixed + jnp.where(hm[g], jnp.dot(wts[g], vb, preferred_element_type=f32), 0.0)
    return mixed


def _gm_bias(bs_ref):
    hi, lo = _split(bs_ref[...])
    et = _gm_expand()
    dn = (((0,), (0,)), ((), ()))
    return lax.dot_general(hi, et, dn, preferred_element_type=f32) + lax.dot_general(lo, et, dn, preferred_element_type=f32)


def _gmlp_fwd(zm, ng, ws, bs8):
    def body(u_ref, v_ref, ng_ref, ws_ref, bs_ref, o_ref):
        hm = _head_masks()
        _, wts = _gm_weights(ws_ref)
        bias = _gm_bias(bs_ref)
        ngv = ng_ref[...]

        def blk(n, carry):
            rows = pl.ds(pl.multiple_of(n * GM_T, GM_T), GM_T)
            vn = _rms(_gelu(v_ref[rows, :]), ngv)
            o_ref[rows, :] = (_gelu(u_ref[rows, :]) * _gm_mixed(vn, wts, bias, hm)).astype(bf16)
            return carry

        lax.fori_loop(0, SEQ // GM_T, blk, 0)

    col = lambda j: pl.BlockSpec((SEQ, MIXW), lambda i: (0, j))
    return pl.pallas_call(
        body, name="gmlp_fwd", grid=(1,),
        in_specs=[col(7), col(8), pl.BlockSpec((1, MIXW), lambda i: (0, 0)),
                  pl.BlockSpec((NHEAD, GM_T, GM_T), lambda i: (0, 0, 0)), pl.BlockSpec((8, GM_T), lambda i: (0, 0))],
        out_specs=pl.BlockSpec((SEQ, MIXW), lambda i: (0, 0)),
        out_shape=_sds((SEQ, MIXW), bf16),
        compiler_params=_params(("arbitrary",)),
    )(zm, zm, ng, ws, bs8)


def _lru_conv(x_ref, cw_ref, cb_ref, xp_s, xc_s):
    xp_s[pl.ds(0, 8), :] = jnp.zeros((8, MIXW), f32)
    xp_s[pl.ds(8, SEQ), :] = x_ref[...]
    cw = cw_ref[...]
    xc = cb_ref[...] + x_ref[...] * cw[3:4]
    for k in range(1, 4):
        xc = xc + xp_s[pl.ds(8 - k, SEQ), :] * cw[3 - k:4 - k]
    xc_s[...] = xc


def _lru_gates(xc, wa, ba, wx, bx, sp, first_row):
    r = _sigmoid(_dot(xc, wa) + ba)
    ig = _sigmoid(_dot(xc, wx) + bx)
    la = (-LRU_C) * r * sp
    a = jnp.exp(la)
    th = jnp.tanh(la)
    m2 = -2.0 * th / (1.0 - th)
    mult = jnp.where(first_row, 1.0, jnp.sqrt(jnp.maximum(m2, 0.0)))
    return a, mult, r, ig, m2


def _lru_scan(a, b, rev):
    row = lax.broadcasted_iota(jnp.int32, (LRU_T, 1), 0)
    k = 1
    while k < LRU_T:
        ok = (row < LRU_T - k) if rev else (row >= k)
        sh = (LRU_T - k) if rev else k
        a_sh = jnp.where(ok, pltpu.roll(a, sh, 0), 1.0)
        b_sh = jnp.where(ok, pltpu.roll(b, sh, 0), 0.0)
        b = b + a * b_sh
        a = a * a_sh
        k *= 2
    return a, b


def _lru_fwd(zm, cw8, cb, wa, ba, wx, bx, lam):
    def body(x_ref, g_ref, cw_ref, cb_ref, wa_ref, ba_ref, wx_ref, bx_ref, lam_ref, o_ref, h_ref, xp_s, xc_s):
        _lru_conv(x_ref, cw_ref, cb_ref, xp_s, xc_s)
        sp = jax.nn.softplus(-lam_ref[...])
        wa_v, wx_v, ba_v, bx_v = wa_ref[...], wx_ref[...], ba_ref[...], bx_ref[...]

        def chunk(c, h_prev):
            rows = pl.ds(pl.multiple_of(c * LRU_T, LRU_T), LRU_T)
            first = (lax.broadcasted_iota(jnp.int32, (LRU_T, 1), 0) + c * LRU_T) == 0
            xc = xc_s[rows, :]
            a, mult, _, ig, _ = _lru_gates(xc, wa_v, ba_v, wx_v, bx_v, sp, first)
            acum, hloc = _lru_scan(a, mult * (ig * xc), False)
            h = hloc + acum * h_prev
            h_ref[rows, :] = h
            o_ref[rows, :] = (h * _gelu(g_ref[rows, :])).astype(bf16)
            return h[LRU_T - 1:LRU_T, :]

        lax.fori_loop(0, SEQ // LRU_T, chunk, jnp.zeros((1, MIXW), f32))

    col = lambda j: pl.BlockSpec((SEQ, MIXW), lambda i: (0, j))
    vec = pl.BlockSpec((1, MIXW), lambda i: (0, 0))
    mat = pl.BlockSpec((MIXW, MIXW), lambda i: (0, 0))
    out = pl.BlockSpec((SEQ, MIXW), lambda i: (0, 0))
    return pl.pallas_call(
        body, name="lru_fwd", grid=(1,),
        in_specs=[col(9), col(10), pl.BlockSpec((8, MIXW), lambda i: (0, 0)), vec, mat, vec, mat, vec, vec],
        out_specs=[out, out],
        out_shape=[_sds((SEQ, MIXW), bf16), _sds((SEQ, MIXW), f32)],
        scratch_shapes=[pltpu.VMEM((SEQ + 8, MIXW), f32), pltpu.VMEM((SEQ, MIXW), f32)],
        compiler_params=_params(("arbitrary",)),
    )(zm, zm, cw8, cb, wa, ba, wx, bx, lam)


def _block_diag(w):
    out = jnp.zeros((MIXW, MIXW), w.dtype)
    for h in range(NHEAD):
        out = lax.dynamic_update_slice(out, w[h], (h * HDIM, h * HDIM))
    return out


def _diag_blocks(w):
    return jnp.stack([w[h * HDIM:(h + 1) * HDIM, h * HDIM:(h + 1) * HDIM] for h in range(NHEAD)])


ROW_TILE = 256


def _merge_fwd(outs, zg, wb, wo, x, g2):
    def body(oa_ref, ob_ref, oc_ref, od_ref, zg_ref, wb_ref, wo_ref, x_ref, g_ref, xo_ref, mg_ref, y_ref):
        merged = jnp.zeros((ROW_TILE, DM), f32)
        for n, o_ref in enumerate((oa_ref, ob_ref, oc_ref, od_ref)):
            proj = jnp.dot(o_ref[...], wb_ref[n], preferred_element_type=f32)
            merged = merged + _sigmoid(zg_ref[:, n * DM:(n + 1) * DM]) * proj
        mb = merged.astype(bf16)
        y = jnp.dot(mb, wo_ref[...], preferred_element_type=f32)
        mg_ref[...] = mb
        y_ref[...] = y
        xo_ref[...] = x_ref[...] + _rms(y, g_ref[...])

    row = lambda w: pl.BlockSpec((ROW_TILE, w), lambda i: (i, 0))
    return pl.pallas_call(
        body, name="merge_fwd", grid=(SEQ // ROW_TILE,),
        in_specs=[row(MIXW)] * 4 + [row(NGATE), pl.BlockSpec((NHEAD, MIXW, DM), lambda i: (0, 0, 0)),
                                    pl.BlockSpec((DM, DM), lambda i: (0, 0)), row(DM), pl.BlockSpec((1, DM), lambda i: (0, 0))],
        out_specs=[row(DM), row(DM), row(DM)],
        out_shape=[_sds((SEQ, DM), f32), _sds((SEQ, DM), bf16), _sds((SEQ, DM), f32)],
        compiler_params=_params(("parallel",)),
    )(*outs, zg, wb, wo, x, g2)


def _ffn_out(u, w2, x, g4):
    def body(u_ref, w_ref, x_ref, g_ref, xo_ref, f_ref):
        a = _silu(u_ref[:, :FFH]) * u_ref[:, FFH:]
        f = jnp.dot(a.astype(bf16), w_ref[...], preferred_element_type=f32)
        f_ref[...] = f
        xo_ref[...] = x_ref[...] + _rms(f, g_ref[...])

    row = lambda w: pl.BlockSpec((ROW_TILE, w), lambda i: (i, 0))
    return pl.pallas_call(
        body, name="ffn_out", grid=(SEQ // ROW_TILE,),
        in_specs=[row(2 * FFH), pl.BlockSpec((FFH, DM), lambda i: (0, 0)), row(DM), pl.BlockSpec((1, DM), lambda i: (0, 0))],
        out_specs=[row(DM), row(DM)],
        out_shape=[_sds((SEQ, DM), f32), _sds((SEQ, DM), f32)],
        compiler_params=_params(("parallel",)),
    )(u, w2, x, g4)


def _loss_head(x, tgt):
    tm = 512

    def body(x_ref, t_ref, l_ref, dx_ref):
        @pl.when(pl.program_id(0) == 0)
        def _():
            l_ref[...] = jnp.zeros((1, 1), f32)

        d = x_ref[...] - t_ref[...]
        dx_ref[...] = d * (1.0 / DM)
        l_ref[...] += (0.5 / DM) * jnp.sum(d * d).reshape(1, 1)

    row = pl.BlockSpec((tm, DM), lambda i: (i, 0))
    return pl.pallas_call(
        body, name="loss_head", grid=(SEQ // tm,),
        in_specs=[row, row], out_specs=[pl.BlockSpec((1, 1), lambda i: (0, 0)), row],
        out_shape=[_sds((1, 1), f32), _sds((SEQ, DM), f32)],
        compiler_params=_params(("arbitrary",)),
    )(x, tgt)


def _lb_fwd(logits):
    def body(lg_ref, o_ref):
        lg = lg_ref[...]
        e = jnp.exp(lg - jnp.max(lg, axis=0, keepdims=True))
        p = e / jnp.sum(e, axis=0, keepdims=True)
        acc = jnp.zeros((1, MIXW), f32)
        o_ref[0:1, :] = acc
        for l in range(1, DEPTH):
            acc = acc + p[l:l + 1]
            o_ref[l:l + 1, :] = acc

    return pl.pallas_call(body, name="lb_fwd", out_shape=_sds((DEPTH, MIXW), f32))(logits)


def _lb_bwd(logits, dlbs):
    def body(lg_ref, d_ref, o_ref):
        lg = lg_ref[...]
        e = jnp.exp(lg - jnp.max(lg, axis=0, keepdims=True))
        p = e / jnp.sum(e, axis=0, keepdims=True)
        d = d_ref[...]
        dp = [jnp.zeros((1, MIXW), f32)] * DEPTH
        acc = jnp.zeros((1, MIXW), f32)
        for j in range(DEPTH - 1, 0, -1):
            acc = acc + d[j:j + 1]
            dp[j] = acc
        inner = sum(p[j:j + 1] * dp[j] for j in range(DEPTH))
        for j in range(DEPTH):
            o_ref[j:j + 1, :] = p[j:j + 1] * (dp[j] - inner)

    return pl.pallas_call(body, name="lb_bwd", out_shape=_sds((DEPTH, MIXW), f32))(logits, dlbs)


def _pad_rows(a, rows=8):
    return jnp.concatenate([a, jnp.zeros((rows - a.shape[0], a.shape[1]), a.dtype)], axis=0)


def _layer_params(l, full, small, lbs):
    row = lambda name: small[name][l][None]
    return dict(
        wm=full["w_in"][l][:, :NMIX], wgt=full["w_in"][l][:, NMIX:], wb=full["w_branch"][l], wo=full["w_out"][l],
        w1=full["w_ffn_in"][l], w2=full["w_ffn_out"][l],
        g1=row("norm_mix_pre"), g2=row("norm_mix_post"), g3=row("norm_ffn_pre"), g4=row("norm_ffn_post"),
        rb8=_pad_rows(small["attn_rel_bias"][l]), lb=lbs[l][None], hng=row("hgrn_norm_g"),
        gng=row("gmlp_norm_g"), gws=small["gmlp_ws"][l], gbs8=_pad_rows(small["gmlp_bs"][l]),
        cw8=_pad_rows(small["lru_conv_w"][l]), cb=row("lru_conv_b"),
        wa=_block_diag(small["lru_wa"][l]).astype(bf16), ba=row("lru_ba"),
        wx=_block_diag(small["lru_wx"][l]).astype(bf16), bx=row("lru_bx"), lam=row("lru_lambda"),
    )


def _layer_fwd(x, p):
    zm, h = _norm_matmul(x, p["g1"], p["wm"], 1408)
    zg = _matmul(h, p["wgt"], 1024)
    oa = _attn_fwd(zm, p["rb8"])
    ob3, obraw3 = _hgrn_fwd(zm.reshape(HG_N, HG_T, NMIX), p["lb"], p["hng"])
    oc = _gmlp_fwd(zm, p["gng"], p["gws"], p["gbs8"])
    od, hd = _lru_fwd(zm, p["cw8"], p["cb"], p["wa"], p["ba"], p["wx"], p["bx"], p["lam"])
    outs = (oa, ob3.reshape(SEQ, MIXW), oc, od)
    x1, merged, y = _merge_fwd(outs, zg, p["wb"], p["wo"], x, p["g2"])
    u, h2 = _norm_matmul(x1, p["g3"], p["w1"], 1408)
    x2, f = _ffn_out(u, p["w2"], x1, p["g4"])
    saved = dict(x=x, h=h, zm=zm, zg=zg, outs=outs, obraw3=obraw3, hd=hd, x1=x1, merged=merged, y=y, u=u, h2=h2, f=f)
    return x2, saved


def _att_bias_grad(db_ref, o_ref):
    r = lax.broadcasted_iota(jnp.int32, (ATT_PAIR, ATT_PAIR), 0)
    c = lax.broadcasted_iota(jnp.int32, (ATT_PAIR, ATT_PAIR), 1)
    flip = (r + c == ATT_PAIR - 1).astype(bf16)
    rows = []
    for h in range(NHEAD):
        d = jnp.concatenate([db_ref[h], jnp.zeros((ATT_PAIR, ATT_WV - ATT_BAND), f32)], axis=1)
        hi, lo = _split(d)
        rev = jnp.dot(flip, hi, preferred_element_type=f32) + jnp.dot(flip, lo, preferred_element_type=f32)
        lined = pltpu.roll(rev, ATT_WV - (ATT_PAIR - 1), 1, stride=1, stride_axis=0)
        rows.append(jnp.sum(lined, axis=0, keepdims=True))
    dwv = jnp.concatenate(rows + [jnp.zeros((8 - NHEAD, ATT_WV), f32)], axis=0)
    hi, lo = _split(dwv)
    m = _att_offset_map()
    dn = (((1,), (1,)), ((), ()))
    o_ref[...] = lax.dot_general(hi, m, dn, preferred_element_type=f32) + lax.dot_general(lo, m, dn, preferred_element_type=f32)


def _attn_bwd(zm, rb8, do):
    def body(q_ref, k_ref, v_ref, rb_ref, do_ref, dz_ref, drb_ref, kp_ref, vp_ref, bm_ref, dk_s, dv_s, db_s):
        _att_pad_kv(k_ref, v_ref, kp_ref, vp_ref)
        _att_bias_tiles(rb_ref, bm_ref)
        dk_s[...] = jnp.zeros_like(dk_s)
        dv_s[...] = jnp.zeros_like(dv_s)
        db_s[...] = jnp.zeros_like(db_s)
        hm = _head_masks()
        scale = HDIM ** -0.5

        def pair(p, carry):
            r0 = pl.multiple_of(p * ATT_PAIR, ATT_PAIR)
            q = q_ref[pl.ds(r0, ATT_PAIR), :] * scale
            dout = do_ref[pl.ds(r0, ATT_PAIR), :]
            kb = kp_ref[pl.ds(r0, ATT_BAND), :]
            vb = vp_ref[pl.ds(r0, ATT_BAND), :]
            key_ok = (lax.broadcasted_iota(jnp.int32, (1, ATT_BAND), 1) + (r0 - ATT_PAD)) >= 0
            dq = jnp.zeros((ATT_PAIR, MIXW), f32)
            dkb = jnp.zeros((ATT_BAND, MIXW), f32)
            dvb = jnp.zeros((ATT_BAND, MIXW), f32)
            for h in range(NHEAD):
                qm = jnp.where(hm[h], q, 0.0).astype(bf16)
                dom = jnp.where(hm[h], dout, 0.0).astype(bf16)
                p_h = _att_probs(qm, kb, bm_ref[h], key_ok)
                dp = _dot_nt(dom, vb)
                ds = p_h * (dp - jnp.sum(dp * p_h, axis=-1, keepdims=True))
                dsb = ds.astype(bf16)
                dq = dq + jnp.where(hm[h], _dot(dsb, kb), 0.0)
                dkb = dkb + _dot_tn(dsb, qm)
                dvb = dvb + _dot_tn(p_h, dom)
                db_s[h] = db_s[h] + ds
            dz_ref[pl.ds(r0, ATT_PAIR), 0:MIXW] = (dq * scale).astype(bf16)
            dk_s[pl.ds(r0, ATT_BAND), :] = dk_s[pl.ds(r0, ATT_BAND), :] + dkb
            dv_s[pl.ds(r0, ATT_BAND), :] = dv_s[pl.ds(r0, ATT_BAND), :] + dvb
            return carry

        lax.fori_loop(0, SEQ // ATT_PAIR, pair, 0)
        dz_ref[:, MIXW:2 * MIXW] = dk_s[pl.ds(ATT_PAD, SEQ), :].astype(bf16)
        dz_ref[:, 2 * MIXW:3 * MIXW] = dv_s[pl.ds(ATT_PAD, SEQ), :].astype(bf16)
        _att_bias_grad(db_s, drb_ref)

    col = lambda j: pl.BlockSpec((SEQ, MIXW), lambda i: (0, j))
    return pl.pallas_call(
        body, name="attn_bwd", grid=(1,),
        in_specs=[col(0), col(1), col(2), pl.BlockSpec((8, REL_SIZE), lambda i: (0, 0)), pl.BlockSpec((SEQ, MIXW), lambda i: (0, 0))],
        out_specs=[pl.BlockSpec((SEQ, 3 * MIXW), lambda i: (0, 0)), pl.BlockSpec((8, REL_SIZE), lambda i: (0, 0))],
        out_shape=[_sds((SEQ, 3 * MIXW), bf16), _sds((8, REL_SIZE), f32)],
        scratch_shapes=[pltpu.VMEM((SEQ + ATT_PAD, MIXW), bf16), pltpu.VMEM((SEQ + ATT_PAD, MIXW), bf16),
                        pltpu.VMEM((NHEAD, ATT_PAIR, ATT_BAND), f32),
                        pltpu.VMEM((SEQ + ATT_PAD, MIXW), f32), pltpu.VMEM((SEQ + ATT_PAD, MIXW), f32),
                        pltpu.VMEM((NHEAD, ATT_PAIR, ATT_BAND), f32)],
        compiler_params=_params(("arbitrary",)),
    )(zm, zm, zm, rb8, do)


def _hgrn_out_bwd(zm3, ng, oraw3, do3):
    def body(g_ref, ng_ref, o_ref, do_ref, dor_ref, dg_ref, dng_ref):
        hm = _same_head(MIXW, HDIM, bf16)
        ngv = ng_ref[...]
        dng = jnp.zeros((1, MIXW), f32)
        for t in range(HG_T):
            o, g, d = o_ref[:, t, :], g_ref[:, t, :], do_ref[:, t, :]
            rs = lax.rsqrt(_dot_hl(o * o, hm) * (1.0 / HDIM) + EPS)
            y1 = o * rs
            dy2 = d * _silu(g)
            dg_ref[:, t, :] = (d * y1 * ngv * _dsilu(g)).astype(bf16)
            dng = dng + jnp.sum(dy2 * y1, axis=0, keepdims=True)
            dy1 = dy2 * ngv
            dor_ref[:, t, :] = rs * (dy1 - y1 * (_dot_hl(dy1 * y1, hm) * (1.0 / HDIM)))
        dng_ref[...] = jnp.broadcast_to(dng, (8, MIXW))

    blk = pl.BlockSpec((HG_N, HG_T, MIXW), lambda i: (0, 0, 0))
    return pl.pallas_call(
        body, name="hgrn_out_bwd", grid=(1,),
        in_specs=[pl.BlockSpec((HG_N, HG_T, MIXW), lambda i: (0, 0, 6)), pl.BlockSpec((1, MIXW), lambda i: (0, 0)), blk, blk],
        out_specs=[blk, blk, pl.BlockSpec((8, MIXW), lambda i: (0, 0))],
        out_shape=[_sds((HG_N, HG_T, MIXW), f32), _sds((HG_N, HG_T, MIXW), bf16), _sds((8, MIXW), f32)],
        compiler_params=_params(("arbitrary",)),
    )(zm3, ng, oraw3, do3)


def _hgrn_bwd(zm3, lb, dor3):
    def body(q_ref, f_ref, i_ref, lb_ref, dor_ref, dz_ref, dlb_ref,
             qf_s, kf_s, b_s, dq_s, dk_s, db_s, dv_s, w_s, x_s, st_s, cur_s):
        lb = lb_ref[...]
        hm = _same_head(MIXW, HDIM, bf16)
        hmf = _same_head(MIXW, HDIM, f32)
        b = None
        for t in range(HG_T):
            qf, kf, lf, _, _, _ = _hg_gates(q_ref[:, t, :], f_ref[:, t, :], lb)
            b = lf if b is None else b + lf
            qf_s[:, t, :] = qf
            kf_s[:, t, :] = kf
            b_s[:, t, :] = b

        def block_terms(n):
            bn = b_s[n]
            bl = bn[HG_T - 1:HG_T]
            eb = jnp.exp(bn)
            ek = jnp.exp(bl - bn)
            return qf_s[n] * eb, kf_s[n] * ek, jnp.exp(bl), eb, ek

        cur_s[...] = jnp.zeros((MIXW, MIXW), f32)

        def fwd_step(n, carry):
            _, kd, dec, _, _ = block_terms(n)
            st = cur_s[...]
            st_s[n] = st.astype(bf16)
            cur_s[...] = st * dec + _dot_tn(i_ref[n], kd) * hmf
            return carry

        lax.fori_loop(0, HG_N, fwd_step, 0)
        cur_s[...] = jnp.zeros((MIXW, MIXW), f32)
        last = lax.broadcasted_iota(jnp.int32, (HG_T, 1), 0) == HG_T - 1

        def bwd_step(j, carry):
            n = HG_N - 1 - j
            qd, kd, dec, eb, ek = block_terms(n)
            v, do_n = i_ref[n], dor_ref[n]
            dst = cur_s[...]
            st = st_s[n]
            dqd = _dot(do_n, st)
            dkd = _dot(v, dst)
            ddec = jnp.sum(dst * st.astype(f32), axis=0, keepdims=True)
            cur_s[...] = dst * dec + _dot_tn(do_n, qd) * hmf
            dq_s[n] = dqd * eb
            dk_s[n] = dkd * ek
            dv_s[n] = _dot_nt(kd, dst)
            dbl = jnp.sum(dkd * kd, axis=0, keepdims=True) + ddec * dec
            db_s[n] = dqd * qd - dkd * kd + jnp.where(last, dbl, 0.0)
            return carry

        lax.fori_loop(0, HG_N, bwd_step, 0)
        for t in range(HG_T):
            qt, bt, dot_t = qf_s[:, t, :], b_s[:, t, :], dor_ref[:, t, :]
            for s in range(t + 1):
                w = qt * kf_s[:, s, :]
                if s < t:
                    w = w * jnp.exp(bt - b_s[:, s, :])
                w_s[pl.ds(s * HG_N, HG_N), :] = w.astype(bf16)
                x_s[pl.ds(s * HG_N, HG_N), :] = (dot_t * i_ref[:, s, :]).astype(bf16)
            p = jnp.dot(w_s[pl.ds(0, (t + 1) * HG_N), :], hm, preferred_element_type=f32)
            dp = jnp.dot(x_s[pl.ds(0, (t + 1) * HG_N), :], hm, preferred_element_type=f32)
            dq_t = jnp.zeros((HG_N, MIXW), f32)
            db_t = jnp.zeros((HG_N, MIXW), f32)
            for s in range(t + 1):
                ps = p[s * HG_N:(s + 1) * HG_N]
                dps = dp[s * HG_N:(s + 1) * HG_N]
                ks = kf_s[:, s, :]
                dv_s[:, s, :] = dv_s[:, s, :] + ps * dot_t
                if s < t:
                    dec_ts = jnp.exp(bt - b_s[:, s, :])
                    g1 = dps * ks * dec_ts
                    dk_s[:, s, :] = dk_s[:, s, :] + dps * qt * dec_ts
                    gw = g1 * qt
                    db_t = db_t + gw
                    db_s[:, s, :] = db_s[:, s, :] - gw
                else:
                    g1 = dps * ks
                    dk_s[:, s, :] = dk_s[:, s, :] + dps * qt
                dq_t = dq_t + g1
            dq_s[:, t, :] = dq_s[:, t, :] + dq_t
            db_s[:, t, :] = db_s[:, t, :] + db_t
        run = jnp.zeros((HG_N, MIXW), f32)
        dlb = jnp.zeros((1, MIXW), f32)
        oml = 1.0 - lb
        for t in range(HG_T - 1, -1, -1):
            run = run + db_s[:, t, :]
            q = q_ref[:, t, :]
            _, _, _, sq, sg, f = _hg_gates(q, f_ref[:, t, :], lb)
            dkf = dk_s[:, t, :]
            df = jnp.where(f > LOG_FLOOR, run / f, 0.0)
            dsg = (df - dkf) * oml
            dlb = dlb + jnp.sum((df - dkf) * (1.0 - sg), axis=0, keepdims=True)
            dz_ref[:, t, 0:MIXW] = (dq_s[:, t, :] * sq * (1.0 + q * (1.0 - sq))).astype(bf16)
            dz_ref[:, t, MIXW:2 * MIXW] = (dsg * sg * (1.0 - sg)).astype(bf16)
            dz_ref[:, t, 2 * MIXW:3 * MIXW] = dv_s[:, t, :].astype(bf16)
        dlb_ref[...] = jnp.broadcast_to(dlb, (8, MIXW))

    one = pl.Buffered(1)
    col = lambda j: pl.BlockSpec((HG_N, HG_T, MIXW), lambda i: (0, 0, j), pipeline_mode=one)
    s3 = pltpu.VMEM((HG_N, HG_T, MIXW), f32)
    return pl.pallas_call(
        body, name="hgrn_bwd", grid=(1,),
        in_specs=[col(3), col(4), col(5), pl.BlockSpec((1, MIXW), lambda i: (0, 0)),
                  pl.BlockSpec((HG_N, HG_T, MIXW), lambda i: (0, 0, 0), pipeline_mode=one)],
        out_specs=[pl.BlockSpec((HG_N, HG_T, 3 * MIXW), lambda i: (0, 0, 0)), pl.BlockSpec((8, MIXW), lambda i: (0, 0))],
        out_shape=[_sds((HG_N, HG_T, 3 * MIXW), bf16), _sds((8, MIXW), f32)],
        scratch_shapes=[s3, s3, s3, s3, s3, s3, s3,
                        pltpu.VMEM((HG_T * HG_N, MIXW), bf16), pltpu.VMEM((HG_T * HG_N, MIXW), bf16),
                        pltpu.VMEM((HG_N, MIXW, MIXW), bf16), pltpu.VMEM((MIXW, MIXW), f32)],
        compiler_params=_params(("arbitrary",)),
    )(zm3, zm3, zm3, lb, dor3)


def _gmlp_bwd(zm, ng, ws, bs8, do):
    def body(u_ref, v_ref, ng_ref, ws_ref, bs_ref, do_ref, dz_ref, dws_ref, dng_ref, dbs_ref, dm_s):
        hm = _head_masks()
        tril, wts = _gm_weights(ws_ref)
        bias = _gm_bias(bs_ref)
        ngv = ng_ref[...]
        dws_ref[...] = jnp.zeros_like(dws_ref)
        dm_s[...] = jnp.zeros_like(dm_s)

        def blk(n, dng):
            rows = pl.ds(pl.multiple_of(n * GM_T, GM_T), GM_T)
            cu, cv, d = u_ref[rows, :], v_ref[rows, :], do_ref[rows, :]
            v = _gelu(cv)
            r = lax.rsqrt(jnp.mean(v * v, axis=-1, keepdims=True) + EPS)
            vh = v * r
            vn = vh * ngv
            u = _gelu(cu)
            dm = d * u
            dmb, vnb = dm.astype(bf16), vn.astype(bf16)
            dvn = jnp.zeros((GM_T, MIXW), f32)
            for g in range(NHEAD):
                dws_ref[g] = dws_ref[g] + _dot_nt(jnp.where(hm[g], dm, 0.0), vnb)
                dvn = dvn + jnp.where(hm[g], _dot_tn(wts[g], dmb), 0.0)
            dm_s[...] = dm_s[...] + dm
            dvh = dvn * ngv
            dv = r * (dvh - vh * jnp.mean(dvh * vh, axis=-1, keepdims=True))
            dz_ref[rows, 0:MIXW] = (d * _gm_mixed(vn, wts, bias, hm) * _dgelu(cu)).astype(bf16)
            dz_ref[rows, MIXW:2 * MIXW] = (dv * _dgelu(cv)).astype(bf16)
            return dng + jnp.sum(dvn * vh, axis=0, keepdims=True)

        dng = lax.fori_loop(0, SEQ // GM_T, blk, jnp.zeros((1, MIXW), f32))
        dng_ref[...] = jnp.broadcast_to(dng, (8, MIXW))
        for g in range(NHEAD):
            dws_ref[g] = jnp.where(tril, dws_ref[g], 0.0)
        dbs_ref[...] = _dot_nt_hl(_gm_expand(), dm_s[...])

    col = lambda j: pl.BlockSpec((SEQ, MIXW), lambda i: (0, j))
    return pl.pallas_call(
        body, name="gmlp_bwd", grid=(1,),
        in_specs=[col(7), col(8), pl.BlockSpec((1, MIXW), lambda i: (0, 0)),
                  pl.BlockSpec((NHEAD, GM_T, GM_T), lambda i: (0, 0, 0)), pl.BlockSpec((8, GM_T), lambda i: (0, 0)),
                  pl.BlockSpec((SEQ, MIXW), lambda i: (0, 0))],
        out_specs=[pl.BlockSpec((SEQ, 2 * MIXW), lambda i: (0, 0)), pl.BlockSpec((NHEAD, GM_T, GM_T), lambda i: (0, 0, 0)),
                   pl.BlockSpec((8, MIXW), lambda i: (0, 0)), pl.BlockSpec((8, GM_T), lambda i: (0, 0))],
        out_shape=[_sds((SEQ, 2 * MIXW), bf16), _sds((NHEAD, GM_T, GM_T), f32), _sds((8, MIXW), f32), _sds((8, GM_T), f32)],
        scratch_shapes=[pltpu.VMEM((GM_T, MIXW), f32)],
        compiler_params=_params(("arbitrary",)),
    )(zm, zm, ng, ws, bs8, do)


def _lru_bwd(zm, cw8, cb, wa, ba, wx, bx, lam, hd, do):
    nchunk = SEQ // LRU_T

    def body(x_ref, g_ref, cw_ref, cb_ref, wa_ref, ba_ref, wx_ref, bx_ref, lam_ref, h_ref, do_ref,
             dz_ref, dwa_ref, dwx_ref, dcw_ref, dvec_ref, xp_s, xc_s, dxc_s):
        _lru_conv(x_ref, cw_ref, cb_ref, xp_s, xc_s)
        lam_v = lam_ref[...]
        sp = jax.nn.softplus(-lam_v)
        sgl = _sigmoid(-lam_v)
        wa_v, wx_v, ba_v, bx_v = wa_ref[...], wx_ref[...], ba_ref[...], bx_ref[...]
        dwa_ref[...] = jnp.zeros_like(dwa_ref)
        dwx_ref[...] = jnp.zeros_like(dwx_ref)
        dxc_s[pl.ds(SEQ, 8), :] = jnp.zeros((8, MIXW), f32)
        row = lax.broadcasted_iota(jnp.int32, (LRU_T, 1), 0)
        zero = jnp.zeros((1, MIXW), f32)

        def chunk(j, carry):
            dh_next, a_next, dba, dbx, dlam = carry
            c = nchunk - 1 - j
            rows = pl.ds(pl.multiple_of(c * LRU_T, LRU_T), LRU_T)
            prev = pl.ds(pl.multiple_of(jnp.maximum(c - 1, 0) * LRU_T, LRU_T), LRU_T)
            first = (row + c * LRU_T) == 0
            xc, gate, d, h = xc_s[rows, :], g_ref[rows, :], do_ref[rows, :], h_ref[rows, :]
            a, mult, r, ig, m2 = _lru_gates(xc, wa_v, ba_v, wx_v, bx_v, sp, first)
            h_last = jnp.where(c > 0, h_ref[prev, :][LRU_T - 1:LRU_T, :], 0.0)
            h_m1 = jnp.where(row == 0, h_last, pltpu.roll(h, 1, 0))
            a_up = jnp.where(row == LRU_T - 1, a_next, pltpu.roll(a, LRU_T - 1, 0))
            acum, dh_loc = _lru_scan(a_up, d * _gelu(gate), True)
            dh = dh_loc + acum * dh_next
            dmult = jnp.where(first, 0.0, dh * (ig * xc))
            msq = jnp.sqrt(jnp.maximum(m2, 0.0))
            dla = dh * h_m1 * a + jnp.where(m2 > 0.0, -dmult * (1.0 - m2) / msq, 0.0)
            dpr = dla * (-LRU_C) * sp * r * (1.0 - r)
            dpi = dh * mult * xc * ig * (1.0 - ig)
            dxc_s[rows, :] = dh * mult * ig + _dot_nt(dpr, wa_v) + _dot_nt(dpi, wx_v)
            dwa_ref[...] = dwa_ref[...] + _dot_tn(xc, dpr)
            dwx_ref[...] = dwx_ref[...] + _dot_tn(xc, dpi)
            dz_ref[rows, MIXW:2 * MIXW] = (d * h * _dgelu(gate)).astype(bf16)
            return (dh[0:1], a[0:1], dba + jnp.sum(dpr, axis=0, keepdims=True), dbx + jnp.sum(dpi, axis=0, keepdims=True),
                    dlam + jnp.sum(dla * r, axis=0, keepdims=True) * (LRU_C * sgl))

        _, _, dba, dbx, dlam = lax.fori_loop(0, nchunk, chunk, (zero, zero, zero, zero, zero))
        cw = cw_ref[...]
        dxc = dxc_s[pl.ds(0, SEQ), :]
        dx = dxc * cw[3:4]
        dcw = [None] * 4
        dcw[3] = jnp.sum(dxc * x_ref[...], axis=0, keepdims=True)
        for k in range(1, 4):
            dx = dx + dxc_s[pl.ds(k, SEQ), :] * cw[3 - k:4 - k]
            dcw[3 - k] = jnp.sum(dxc * xp_s[pl.ds(8 - k, SEQ), :], axis=0, keepdims=True)
        dz_ref[:, 0:MIXW] = dx.astype(bf16)
        dcw_ref[...] = jnp.concatenate(dcw + [jnp.zeros((4, MIXW), f32)], axis=0)
        dvec_ref[...] = jnp.concatenate([jnp.sum(dxc, axis=0, keepdims=True), dba, dbx, dlam, jnp.zeros((4, MIXW), f32)], axis=0)

    col = lambda j: pl.BlockSpec((SEQ, MIXW), lambda i: (0, j))
    vec = pl.BlockSpec((1, MIXW), lambda i: (0, 0))
    vec8 = pl.BlockSpec((8, MIXW), lambda i: (0, 0))
    mat = pl.BlockSpec((MIXW, MIXW), lambda i: (0, 0))
    full = pl.BlockSpec((SEQ, MIXW), lambda i: (0, 0))
    return pl.pallas_call(
        body, name="lru_bwd", grid=(1,),
        in_specs=[col(9), col(10), vec8, vec, mat, vec, mat, vec, vec, full, full],
        out_specs=[pl.BlockSpec((SEQ, 2 * MIXW), lambda i: (0, 0)), mat, mat, vec8, vec8],
        out_shape=[_sds((SEQ, 2 * MIXW), bf16), _sds((MIXW, MIXW), f32), _sds((MIXW, MIXW), f32),
                   _sds((8, MIXW), f32), _sds((8, MIXW), f32)],
        scratch_shapes=[pltpu.VMEM((SEQ + 8, MIXW), f32), pltpu.VMEM((SEQ, MIXW), f32), pltpu.VMEM((SEQ + 8, MIXW), f32)],
        compiler_params=_params(("arbitrary",)),
    )(zm, zm, cw8, cb, wa, ba, wx, bx, lam, hd, do)


def _matmul_tn(a, b, tm, tn, b_col0=0):
    m = a.shape[1]
    n = tn if b_col0 else b.shape[1]
    off = b_col0 // tn

    def body(a_ref, b_ref, o_ref):
        o_ref[...] = _dot_tn(a_ref[...], b_ref[...]).astype(bf16)

    return pl.pallas_call(
        body, name="matmul_tn", grid=(m // tm, n // tn),
        in_specs=[pl.BlockSpec((SEQ, tm), lambda i, j: (0, i)), pl.BlockSpec((SEQ, tn), lambda i, j: (0, j + off))],
        out_specs=pl.BlockSpec((tm, tn), lambda i, j: (i, j)),
        out_shape=_sds((m, n), bf16),
        compiler_params=_params(("parallel", "arbitrary")),
    )(a, b)


def _matmul_nt_norm(pairs, x, g, dres):
    tm = 512
    steps = [a.shape[1] // t for a, _, t in pairs]
    starts = [sum(steps[:i]) for i in range(len(pairs))]
    total = sum(steps)
    npair = len(pairs)

    def body(*refs):
        a_refs, w_refs = refs[0:2 * npair:2], refs[1:2 * npair:2]
        x_ref, g_ref, dres_ref, dx_ref, dg_ref, acc_s = refs[2 * npair:]
        i, k = pl.program_id(0), pl.program_id(1)

        @pl.when(k == 0)
        def _():
            acc_s[...] = jnp.zeros_like(acc_s)

        @pl.when((i == 0) & (k == 0))
        def _():
            dg_ref[...] = jnp.zeros_like(dg_ref)

        for q in range(npair):
            @pl.when((k >= starts[q]) & (k < starts[q] + steps[q]))
            def _(q=q):
                acc_s[...] += _dot_nt(a_refs[q][...], w_refs[q][...])

        @pl.when(k == total - 1)
        def _():
            dx, dg = _rms_bwd(x_ref[...], g_ref[...], acc_s[...])
            dx_ref[...] = dres_ref[...] + dx
            dg_ref[...] += dg

    in_specs, args = [], []
    for q, (a, w, t) in enumerate(pairs):
        kmap = lambda k, q=q: jnp.clip(k - starts[q], 0, steps[q] - 1)
        in_specs += [pl.BlockSpec((tm, t), lambda i, k, kmap=kmap: (i, kmap(k))),
                     pl.BlockSpec((DM, t), lambda i, k, kmap=kmap: (0, kmap(k)))]
        args += [a, w]
    row = pl.BlockSpec((tm, DM), lambda i, k: (i, 0))
    vec = pl.BlockSpec((1, DM), lambda i, k: (0, 0))
    return pl.pallas_call(
        body, name="matmul_nt_norm", grid=(SEQ // tm, total),
        in_specs=in_specs + [row, vec, row], out_specs=[row, vec],
        out_shape=[_sds((SEQ, DM), f32), _sds((1, DM), f32)],
        scratch_shapes=[pltpu.VMEM((tm, DM), f32)],
        compiler_params=_params(("arbitrary", "arbitrary")),
    )(*args, x, g, dres)


def _merge_bwd(dx1, y, g2, outs, zg, wb, wo):
    def body(dx_ref, y_ref, g_ref, oa_ref, ob_ref, oc_ref, od_ref, zg_ref, wb_ref, wo_ref,
             da_ref, db_ref, dc_ref, dd_ref, dzg_ref, dpj_ref, dy_ref, dg_ref):
        @pl.when(pl.program_id(0) == 0)
        def _():
            dg_ref[...] = jnp.zeros_like(dg_ref)

        dy, dg = _rms_bwd(y_ref[...], g_ref[...], dx_ref[...])
        dg_ref[...] += dg
        dyb = dy.astype(bf16)
        dy_ref[...] = dyb
        dmerged = _dot_nt(dyb, wo_ref[...])
        for n, (o_ref, do_ref) in enumerate(((oa_ref, da_ref), (ob_ref, db_ref), (oc_ref, dc_ref), (od_ref, dd_ref))):
            cols = slice(n * DM, (n + 1) * DM)
            gate = _sigmoid(zg_ref[:, cols])
            proj = jnp.dot(o_ref[...], wb_ref[n], preferred_element_type=f32)
            dproj = (dmerged * gate).astype(bf16)
            dpj_ref[:, cols] = dproj
            dzg_ref[:, cols] = (dmerged * proj * gate * (1.0 - gate)).astype(bf16)
            do_ref[...] = _dot_nt(dproj, wb_ref[n])

    row = lambda w: pl.BlockSpec((ROW_TILE, w), lambda i: (i, 0))
    vec = pl.BlockSpec((1, DM), lambda i: (0, 0))
    return pl.pallas_call(
        body, name="merge_bwd", grid=(SEQ // ROW_TILE,),
        in_specs=[row(DM), row(DM), vec] + [row(MIXW)] * 4 + [row(NGATE), pl.BlockSpec((NHEAD, MIXW, DM), lambda i: (0, 0, 0)),
                                                              pl.BlockSpec((DM, DM), lambda i: (0, 0))],
        out_specs=[row(MIXW)] * 4 + [row(NGATE), row(NGATE), row(DM), vec],
        out_shape=[_sds((SEQ, MIXW), f32)] * 4 + [_sds((SEQ, NGATE), bf16), _sds((SEQ, NGATE), bf16), _sds((SEQ, DM), bf16),
                                                  _sds((1, DM), f32)],
        compiler_params=_params(("arbitrary",)),
    )(dx1, y, g2, *outs, zg, wb, wo)


def _ffn_bwd(dx2, f, g4, u, w2):
    def body(dx_ref, f_ref, g_ref, u_ref, w_ref, du_ref, a_ref, df_ref, dg_ref):
        @pl.when(pl.program_id(0) == 0)
        def _():
            dg_ref[...] = jnp.zeros_like(dg_ref)

        df, dg = _rms_bwd(f_ref[...], g_ref[...], dx_ref[...])
        dg_ref[...] += dg
        dfb = df.astype(bf16)
        df_ref[...] = dfb
        da = _dot_nt(dfb, w_ref[...])
        gt, up = u_ref[:, :FFH], u_ref[:, FFH:]
        a_ref[...] = (_silu(gt) * up).astype(bf16)
        du_ref[:, :FFH] = (da * up * _dsilu(gt)).astype(bf16)
        du_ref[:, FFH:] = (da * _silu(gt)).astype(bf16)

    row = lambda w: pl.BlockSpec((ROW_TILE, w), lambda i: (i, 0))
    vec = pl.BlockSpec((1, DM), lambda i: (0, 0))
    return pl.pallas_call(
        body, name="ffn_bwd", grid=(SEQ // ROW_TILE,),
        in_specs=[row(DM), row(DM), vec, row(2 * FFH), pl.BlockSpec((FFH, DM), lambda i: (0, 0))],
        out_specs=[row(2 * FFH), row(FFH), row(DM), vec],
        out_shape=[_sds((SEQ, 2 * FFH), bf16), _sds((SEQ, FFH), bf16), _sds((SEQ, DM), bf16), _sds((1, DM), f32)],
        compiler_params=_params(("arbitrary",)),
    )(dx2, f, g4, u, w2)


def _layer_bwd(dx2, p, sv):
    du, act, df, dg4 = _ffn_bwd(dx2, sv["f"], p["g4"], sv["u"], p["w2"])
    dw2 = _matmul_tn(act, df, 1408, DM)
    dx1, dg3 = _matmul_nt_norm([(du, p["w1"], 1408)], sv["x1"], p["g3"], dx2)
    dw1 = _matmul_tn(sv["h2"], du, DM, 1408)
    *dos, dzg, dproj, dy, dg2 = _merge_bwd(dx1, sv["y"], p["g2"], sv["outs"], sv["zg"], p["wb"], p["wo"])
    dwo = _matmul_tn(sv["merged"], dy, DM, DM)
    dwb = jnp.stack([_matmul_tn(sv["outs"][n], dproj, MIXW, DM, b_col0=n * DM) if n else
                     _matmul_tn(sv["outs"][0], dproj[:, :DM], MIXW, DM) for n in range(NHEAD)])
    zm = sv["zm"]
    zm3 = zm.reshape(HG_N, HG_T, NMIX)
    dza, drb = _attn_bwd(zm, p["rb8"], dos[0])
    dor, dgb, dhng = _hgrn_out_bwd(zm3, p["hng"], sv["obraw3"], dos[1].reshape(HG_N, HG_T, MIXW))
    dzb, dlb = _hgrn_bwd(zm3, p["lb"], dor)
    dzc, dws, dgng, dbs = _gmlp_bwd(zm, p["gng"], p["gws"], p["gbs8"], dos[2])
    dzd, dwa, dwx, dcw, dvec = _lru_bwd(zm, p["cw8"], p["cb"], p["wa"], p["ba"], p["wx"], p["bx"], p["lam"], sv["hd"], dos[3])
    dzm = jnp.concatenate([dza, dzb.reshape(SEQ, 3 * MIXW), dgb.reshape(SEQ, MIXW), dzc, dzd], axis=1)
    dx0, dg1 = _matmul_nt_norm([(dzm, p["wm"], 1408), (dzg, p["wgt"], 1024)], sv["x"], p["g1"], dx1)
    dwin = jnp.concatenate([_matmul_tn(sv["h"], dzm, DM, 1408), _matmul_tn(sv["h"], dzg, DM, 1024)], axis=1)
    big = dict(w_in=dwin, w_branch=dwb, w_out=dwo, w_ffn_in=dw1, w_ffn_out=dw2)
    small = dict(
        norm_mix_pre=dg1[0], norm_mix_post=dg2[0], norm_ffn_pre=dg3[0], norm_ffn_post=dg4[0],
        attn_rel_bias=drb[:NHEAD], lb=dlb[0], hgrn_norm_g=dhng[0], gmlp_norm_g=dgng[0], gmlp_ws=dws, gmlp_bs=dbs[:NHEAD],
        lru_conv_w=dcw[:NHEAD], lru_conv_b=dvec[0], lru_wa=_diag_blocks(dwa), lru_ba=dvec[1], lru_wx=_diag_blocks(dwx),
        lru_bx=dvec[2], lru_lambda=dvec[3],
    )
    return dx0, big, small


BIG = ("w_in", "w_branch", "w_out", "w_ffn_in", "w_ffn_out")
SMALL = ("norm_mix_pre", "norm_mix_post", "norm_ffn_pre", "norm_ffn_post", "attn_rel_bias", "hgrn_lb_logits", "hgrn_norm_g",
         "gmlp_norm_g", "gmlp_ws", "gmlp_bs", "lru_conv_w", "lru_conv_b", "lru_wa", "lru_ba", "lru_wx", "lru_bx", "lru_lambda")


def _local_step(x, tgt, full, small):
    lbs = _lb_fwd(small["hgrn_lb_logits"])
    params, saved = [], []
    for l in range(DEPTH):
        p = _layer_params(l, full, small, lbs)
        x, sv = _layer_fwd(x, p)
        params.append(p)
        saved.append(sv)
    loss, dx = _loss_head(x, tgt)
    bigs, smalls = [None] * DEPTH, [None] * DEPTH
    for l in range(DEPTH - 1, -1, -1):
        dx, bigs[l], smalls[l] = _layer_bwd(dx, params[l], saved[l])
    gbig = {k: jnp.stack([bigs[l][k] for l in range(DEPTH)]) for k in BIG}
    gsmall = {k: jnp.stack([smalls[l][k] for l in range(DEPTH)]) for k in smalls[0]}
    gsmall["hgrn_lb_logits"] = _lb_bwd(small["hgrn_lb_logits"], gsmall.pop("lb"))
    return loss, dx, gbig, gsmall


HBM_ANY = pl.BlockSpec(memory_space=pl.ANY)


def _mesh_pos():
    return lax.axis_index("x"), lax.axis_index("y"), lax.axis_index("c")


def _all_gather(x, name):
    def body(x_ref, out_ref, send_sems, recv_sems, local_sem):
        ax, ay, ac = _mesh_pos()
        me, sibling = (ax, ay, ac), (ax, ay, 1 - ac)
        chips = [(1 - ax, ay), (ax, 1 - ay), (1 - ax, 1 - ay)]

        def slot(px, py, pc):
            return out_ref.at[4 * px + 2 * py + pc]

        def copy(k, block, to, src=None):
            return pltpu.make_async_remote_copy(
                src_ref=slot(*block) if src is None else src, dst_ref=slot(*block),
                send_sem=send_sems.at[k], recv_sem=recv_sems.at[k], device_id=to, device_id_type=MESH_ID)

        mine = pltpu.make_async_copy(x_ref, slot(*me), local_sem)
        mine.start()
        first = [copy(0, me, sibling, src=x_ref)]
        first += [copy(1 + j, me, (*chip, ac), src=x_ref) for j, chip in enumerate(chips)]
        for cp in first:
            cp.start()
        passed = [copy(4 + j, (*chip, ac), sibling) for j, chip in enumerate(chips)]
        for j, chip in enumerate(chips):
            copy(1 + j, (*chip, ac), me).wait_recv()
            passed[j].start()
        copy(0, sibling, me).wait_recv()
        for j, chip in enumerate(chips):
            copy(4 + j, (*chip, 1 - ac), me).wait_recv()
        for cp in first + passed:
            cp.wait_send()
        mine.wait()

    return pl.pallas_call(
        body, name=name, out_shape=_sds((NDEV,) + x.shape, x.dtype),
        in_specs=[HBM_ANY], out_specs=HBM_ANY,
        scratch_shapes=[pltpu.SemaphoreType.DMA((7,)), pltpu.SemaphoreType.DMA((7,)), pltpu.SemaphoreType.DMA],
    )(x)


def _exchange(g, name):
    def body(g_ref, out_ref, send_sems, recv_sems, local_sem):
        ax, ay, ac = _mesh_pos()
        me = 4 * ax + 2 * ay + ac
        mine = pltpu.make_async_copy(g_ref.at[me], out_ref.at[me], local_sem)
        mine.start()
        copies = []
        for k in range(1, NDEV):
            px = 1 - ax if k & 4 else ax
            py = 1 - ay if k & 2 else ay
            pc = 1 - ac if k & 1 else ac
            copies.append(pltpu.make_async_remote_copy(
                src_ref=g_ref.at[4 * px + 2 * py + pc], dst_ref=out_ref.at[me],
                send_sem=send_sems.at[k - 1], recv_sem=recv_sems.at[k - 1], device_id=(px, py, pc), device_id_type=MESH_ID))
        for cp in copies:
            cp.start()
        for cp in copies:
            cp.wait()
        mine.wait()

    return pl.pallas_call(
        body, name=name, out_shape=_sds(g.shape, g.dtype),
        in_specs=[HBM_ANY], out_specs=HBM_ANY,
        scratch_shapes=[pltpu.SemaphoreType.DMA((7,)), pltpu.SemaphoreType.DMA((7,)), pltpu.SemaphoreType.DMA],
    )(g)


def _row_tile(rows, cols):
    cap = max(8, (1 << 18) // cols)
    if rows <= cap:
        return rows
    best = None
    for t in range(8, cap + 1, 8):
        if rows % t == 0:
            best = t
    assert best is not None, (rows, cols)
    return best


def _sum_parts(parts, name):
    npart, rows, cols = parts.shape
    tr = _row_tile(rows, cols)

    def body(p_ref, o_ref):
        g = p_ref[0].astype(f32)
        for j in range(1, npart):
            g = g + p_ref[j].astype(f32)
        o_ref[...] = g

    return pl.pallas_call(
        body, name=name, grid=(rows // tr,),
        in_specs=[pl.BlockSpec((npart, tr, cols), lambda i: (0, i, 0))], out_specs=pl.BlockSpec((tr, cols), lambda i: (i, 0)),
        out_shape=_sds((rows, cols), f32), compiler_params=_params(("parallel",)),
    )(parts)


def _adamw(parts, w, m, v, name):
    npart, rows, cols = parts.shape
    tr = _row_tile(rows, cols)
    c1 = 1.0 / (1.0 - ADAM_B1 ** ADAM_STEP)
    c2 = 1.0 / (1.0 - ADAM_B2 ** ADAM_STEP)

    def body(p_ref, w_ref, m_ref, v_ref, g_ref, d_ref, mo_ref, vo_ref):
        g = p_ref[0].astype(f32)
        for j in range(1, npart):
            g = g + p_ref[j].astype(f32)
        mn = ADAM_B1 * m_ref[...] + (1.0 - ADAM_B1) * g
        vn = ADAM_B2 * v_ref[...] + (1.0 - ADAM_B2) * (g * g)
        g_ref[...] = g
        mo_ref[...] = mn
        vo_ref[...] = vn
        d_ref[...] = (-ADAM_LR) * ((mn * c1) / (jnp.sqrt(vn * c2) + ADAM_EPS) + ADAM_WD * w_ref[...])

    blk = pl.BlockSpec((tr, cols), lambda i: (i, 0))
    return pl.pallas_call(
        body, name=name, grid=(rows // tr,),
        in_specs=[pl.BlockSpec((npart, tr, cols), lambda i: (0, i, 0)), blk, blk, blk], out_specs=[blk] * 4,
        out_shape=[_sds((rows, cols), f32)] * 4, compiler_params=_params(("parallel",)),
    )(parts, w, m, v)


def _pack(arrays):
    rows = []
    for a in arrays:
        flat = a.reshape(-1)
        pad = (-flat.shape[0]) % 1024
        rows.append(jnp.concatenate([flat, jnp.zeros((pad,), flat.dtype)]).reshape(-1, 128))
    return jnp.concatenate(rows, axis=0)


def _unpack(flat, shapes):
    out, r = [], 0
    for s in shapes:
        n = math.prod(s)
        nr = (n + 1023) // 1024 * 8
        out.append(flat[r:r + nr].reshape(-1)[:n].reshape(s))
        r += nr
    return out


BIG_SHARD_AXIS = dict(w_in=2, w_branch=3, w_out=1, w_ffn_in=2, w_ffn_out=1)
SHARDED_SMALL = ("attn_rel_bias", "lru_conv_w")


def _to_blocks(full, axis):
    s = full.shape
    cut = full.reshape(s[:axis] + (NDEV, s[axis] // NDEV) + s[axis + 1:])
    return jnp.moveaxis(cut, axis, 0)


def _from_blocks(blocks, axis):
    moved = jnp.moveaxis(blocks, 0, axis)
    s = moved.shape
    return moved.reshape(s[:axis] + (s[axis] * s[axis + 1],) + s[axis + 2:])


def _flat2(a):
    return a.reshape(-1, a.shape[-1])


def _my_slice(a, n):
    ax, ay, ac = _mesh_pos()
    return lax.dynamic_slice_in_dim(a, (4 * ax + 2 * ay + ac) * n, n, axis=a.ndim - 1)


_WEIGHTS = ("norm_mix_pre", "norm_mix_post", "norm_ffn_pre", "norm_ffn_post", "w_in", "attn_rel_bias", "hgrn_lb_logits",
            "hgrn_norm_g", "gmlp_norm_g", "gmlp_ws", "gmlp_bs", "lru_conv_w", "lru_conv_b", "lru_wa", "lru_ba", "lru_wx",
            "lru_bx", "lru_lambda", "w_branch", "w_out", "w_ffn_in", "w_ffn_out")


def _step(x, loss_target, w, m, v):
    full = {k: _from_blocks(_all_gather(w[k].astype(bf16), "gather_" + k), BIG_SHARD_AXIS[k]) for k in BIG}
    cut = jnp.concatenate([w[k] for k in SHARDED_SMALL], axis=-1)
    parts = _all_gather(_pack([cut]), "gather_small").reshape(NDEV, -1)[:, :math.prod(cut.shape)].reshape((NDEV,) + cut.shape)
    small = {k: w[k] for k in SMALL if k not in SHARDED_SMALL}
    at = 0
    for k in SHARDED_SMALL:
        n = w[k].shape[-1]
        small[k] = _from_blocks(parts[..., at:at + n], 2)
        at += n
    loss, dx, gbig, gsmall = _step_local(x, loss_target, full, small)
    grads, deltas, new_m, new_v = {}, {}, {}, {}
    for k in BIG:
        got = _exchange(_to_blocks(gbig[k], BIG_SHARD_AXIS[k]), "exchange_" + k)
        shape = w[k].shape
        outs = _adamw(got.reshape((NDEV,) + _flat2(w[k]).shape), _flat2(w[k]), _flat2(m[k]), _flat2(v[k]), "adamw_" + k)
        grads[k], deltas[k], new_m[k], new_v[k] = (o.reshape(shape) for o in outs)
    shapes = [gsmall[k].shape for k in SMALL]
    sums = _unpack(_sum_parts(_all_gather(_pack([gsmall[k] for k in SMALL]), "gather_small_grads"), "sum_small_grads"), shapes)
    gs = dict(zip(SMALL, sums))
    for k in SHARDED_SMALL:
        gs[k] = _my_slice(gs[k], w[k].shape[-1])
    packed = [_pack([d[k] for k in SMALL]) for d in (gs, w, m, v)]
    outs = _adamw(packed[0][None], packed[1], packed[2], packed[3], "adamw_small")
    shapes = [w[k].shape for k in SMALL]
    for d, o in zip((grads, deltas, new_m, new_v), outs):
        d.update(zip(SMALL, _unpack(o, shapes)))
    total = lax.psum(loss[0, 0], ("x", "y", "c"))
    return total, dx, grads, deltas, new_m, new_v


def _step_local(x, loss_target, full, small):
    loss, dx, gbig, gsmall = _local_step(x[0], loss_target[0], full, small)
    return loss, dx[None], gbig, gsmall


def kernel(x, norm_mix_pre, norm_mix_post, norm_ffn_pre, norm_ffn_post, w_in, attn_rel_bias, hgrn_lb_logits, hgrn_norm_g, gmlp_norm_g, gmlp_ws, gmlp_bs, lru_conv_w, lru_conv_b, lru_wa, lru_ba, lru_wx, lru_bx, lru_lambda, w_branch, w_out, w_ffn_in, w_ffn_out, loss_target, m_norm_mix_pre, m_norm_mix_post, m_norm_ffn_pre, m_norm_ffn_post, m_w_in, m_attn_rel_bias, m_hgrn_lb_logits, m_hgrn_norm_g, m_gmlp_norm_g, m_gmlp_ws, m_gmlp_bs, m_lru_conv_w, m_lru_conv_b, m_lru_wa, m_lru_ba, m_lru_wx, m_lru_bx, m_lru_lambda, m_w_branch, m_w_out, m_w_ffn_in, m_w_ffn_out, v_norm_mix_pre, v_norm_mix_post, v_norm_ffn_pre, v_norm_ffn_post, v_w_in, v_attn_rel_bias, v_hgrn_lb_logits, v_hgrn_norm_g, v_gmlp_norm_g, v_gmlp_ws, v_gmlp_bs, v_lru_conv_w, v_lru_conv_b, v_lru_wa, v_lru_ba, v_lru_wx, v_lru_bx, v_lru_lambda, v_w_branch, v_w_out, v_w_ffn_in, v_w_ffn_out):
    w = dict(zip(_WEIGHTS, (norm_mix_pre, norm_mix_post, norm_ffn_pre, norm_ffn_post, w_in, attn_rel_bias, hgrn_lb_logits, hgrn_norm_g, gmlp_norm_g, gmlp_ws, gmlp_bs, lru_conv_w, lru_conv_b, lru_wa, lru_ba, lru_wx, lru_bx, lru_lambda, w_branch, w_out, w_ffn_in, w_ffn_out)))
    m = dict(zip(_WEIGHTS, (m_norm_mix_pre, m_norm_mix_post, m_norm_ffn_pre, m_norm_ffn_post, m_w_in, m_attn_rel_bias, m_hgrn_lb_logits, m_hgrn_norm_g, m_gmlp_norm_g, m_gmlp_ws, m_gmlp_bs, m_lru_conv_w, m_lru_conv_b, m_lru_wa, m_lru_ba, m_lru_wx, m_lru_bx, m_lru_lambda, m_w_branch, m_w_out, m_w_ffn_in, m_w_ffn_out)))
    v = dict(zip(_WEIGHTS, (v_norm_mix_pre, v_norm_mix_post, v_norm_ffn_pre, v_norm_ffn_post, v_w_in, v_attn_rel_bias, v_hgrn_lb_logits, v_hgrn_norm_g, v_gmlp_norm_g, v_gmlp_ws, v_gmlp_bs, v_lru_conv_w, v_lru_conv_b, v_lru_wa, v_lru_ba, v_lru_wx, v_lru_bx, v_lru_lambda, v_w_branch, v_w_out, v_w_ffn_in, v_w_ffn_out)))
    loss, grad_x, grads, deltas, new_m, new_v = _step(x, loss_target, w, m, v)
    return (loss, grad_x, *[grads[k] for k in _WEIGHTS], *[deltas[k] for k in _WEIGHTS],
            *[new_m[k] for k in _WEIGHTS], *[new_v[k] for k in _WEIGHTS])
```

```python
import math

import jax
import jax.numpy as jnp
from jax import lax
from jax.experimental import pallas as pl
from jax.experimental.pallas import tpu as pltpu
from jax.experimental.pallas import tpu_sc as plsc

f32 = jnp.float32
bf16 = jnp.bfloat16

SEQ = 2048
DM = 1024
DEPTH = 4
NDEV = 8
MIXW = 256
NHEAD = 4
HDIM = 64
NMIX = 11 * MIXW
NGATE = 4 * DM
FFH = 2816
EPS = 1e-6
NEG_BIG = -1e30
LOG_FLOOR = 1e-30
LRU_C = 8.0
REL_SIZE = 320
ATT_PAIR = 128
ATT_BAND = 640
ATT_PAD = 512
ATT_WV = 768
HG_T = 16
HG_N = SEQ // HG_T
GM_T = 128
LRU_T = 128
ADAM_LR, ADAM_B1, ADAM_B2, ADAM_EPS, ADAM_WD, ADAM_STEP = 0.001, 0.9, 0.999, 1e-8, 0.01, 10
V7X_VMEM_LIMIT = 56 * 1024 * 1024
GELU_C0 = math.sqrt(2.0 / math.pi)
GELU_C1 = 0.044715
MESH_ID = pl.DeviceIdType.MESH


def _params(sem=None):
    if sem is None:
        return pltpu.CompilerParams(vmem_limit_bytes=V7X_VMEM_LIMIT)
    return pltpu.CompilerParams(dimension_semantics=sem, vmem_limit_bytes=V7X_VMEM_LIMIT)


def _sds(shape, dtype):
    return jax.ShapeDtypeStruct(shape, dtype)


def _dot(a, b):
    return jnp.dot(a.astype(bf16), b.astype(bf16), preferred_element_type=f32)


def _dot_nt(a, b):
    return lax.dot_general(a.astype(bf16), b.astype(bf16), (((1,), (1,)), ((), ())), preferred_element_type=f32)


def _dot_tn(a, b):
    return lax.dot_general(a.astype(bf16), b.astype(bf16), (((0,), (0,)), ((), ())), preferred_element_type=f32)


def _split(a):
    hi = a.astype(bf16)
    lo = (a - hi.astype(f32)).astype(bf16)
    return hi, lo


def _dot_hl(a, m):
    hi, lo = _split(a)
    return jnp.dot(hi, m, preferred_element_type=f32) + jnp.dot(lo, m, preferred_element_type=f32)


def _dot_nt_hl(m, a):
    hi, lo = _split(a)
    dn = (((1,), (1,)), ((), ()))
    return lax.dot_general(m, hi, dn, preferred_element_type=f32) + lax.dot_general(m, lo, dn, preferred_element_type=f32)


def _sigmoid(x):
    return jax.nn.sigmoid(x)


def _silu(x):
    return x * _sigmoid(x)


def _dsilu(x):
    s = _sigmoid(x)
    return s * (1.0 + x * (1.0 - s))


def _gelu(x):
    return 0.5 * x * (1.0 + jnp.tanh(GELU_C0 * (x + GELU_C1 * x * x * x)))


def _dgelu(x):
    t = jnp.tanh(GELU_C0 * (x + GELU_C1 * x * x * x))
    return 0.5 * (1.0 + t) + 0.5 * x * (1.0 - t * t) * GELU_C0 * (1.0 + 3.0 * GELU_C1 * x * x)


def _rms(x, g):
    r = lax.rsqrt(jnp.mean(x * x, axis=-1, keepdims=True) + EPS)
    return x * r * g


def _rms_bwd(x, g, dy):
    r = lax.rsqrt(jnp.mean(x * x, axis=-1, keepdims=True) + EPS)
    xh = x * r
    dxh = dy * g
    dx = r * (dxh - xh * jnp.mean(dxh * xh, axis=-1, keepdims=True))
    return dx, jnp.sum(dy * xh, axis=0, keepdims=True)


def _same_head(n, width, dtype):
    r = lax.broadcasted_iota(jnp.int32, (n, n), 0) // width
    c = lax.broadcasted_iota(jnp.int32, (n, n), 1) // width
    return (r == c).astype(dtype)


def _head_masks(rows=1):
    lane = lax.broadcasted_iota(jnp.int32, (rows, MIXW), 1) // HDIM
    return [lane == h for h in range(NHEAD)]


def _norm_matmul(x, g, w, tn):
    n = w.shape[1]
    tm = 1024

    def body(x_ref, g_ref, w_ref, z_ref, h_ref):
        @pl.when(pl.program_id(1) == 0)
        def _():
            h_ref[...] = _rms(x_ref[...], g_ref[...]).astype(bf16)

        z_ref[...] = jnp.dot(h_ref[...], w_ref[...], preferred_element_type=f32)

    return pl.pallas_call(
        body, name="norm_matmul", grid=(SEQ // tm, n // tn),
        in_specs=[pl.BlockSpec((tm, DM), lambda i, j: (i, 0)), pl.BlockSpec((1, DM), lambda i, j: (0, 0)),
                  pl.BlockSpec((DM, tn), lambda i, j: (0, j))],
        out_specs=[pl.BlockSpec((tm, tn), lambda i, j: (i, j)), pl.BlockSpec((tm, DM), lambda i, j: (i, 0))],
        out_shape=[_sds((SEQ, n), f32), _sds((SEQ, DM), bf16)],
        compiler_params=_params(("parallel", "arbitrary")),
    )(x, g, w)


def _matmul(a, w, tn):
    k, n = w.shape
    tm = 1024

    def body(a_ref, w_ref, z_ref):
        z_ref[...] = jnp.dot(a_ref[...], w_ref[...], preferred_element_type=f32)

    return pl.pallas_call(
        body, name="matmul", grid=(SEQ // tm, n // tn),
        in_specs=[pl.BlockSpec((tm, k), lambda i, j: (i, 0)), pl.BlockSpec((k, tn), lambda i, j: (0, j))],
        out_specs=pl.BlockSpec((tm, tn), lambda i, j: (i, j)),
        out_shape=_sds((SEQ, n), f32),
        compiler_params=_params(("parallel", "arbitrary")),
    )(a, w)


def _att_offset_map():
    i = lax.broadcasted_iota(jnp.int32, (REL_SIZE, ATT_WV), 0)
    t = lax.broadcasted_iota(jnp.int32, (REL_SIZE, ATT_WV), 1)
    e = jnp.where(t <= ATT_BAND, t, t - ATT_WV)
    idx = jnp.clip(ATT_PAD - e, -(HDIM - 1), 256) + (HDIM - 1)
    return (idx == i).astype(bf16)


def _att_band_valid():
    qc = lax.broadcasted_iota(jnp.int32, (ATT_PAIR, ATT_BAND), 0) // HDIM
    kc = lax.broadcasted_iota(jnp.int32, (ATT_PAIR, ATT_BAND), 1) // HDIM
    return (kc >= qc) & (kc <= qc + 8)


def _att_bias_tiles(rb_ref, bm_ref):
    wv = _dot_hl(rb_ref[...], _att_offset_map())
    valid = _att_band_valid()
    for h in range(NHEAD):
        rows = jnp.broadcast_to(wv[h:h + 1, :], (ATT_PAIR, ATT_WV))
        tile = pltpu.roll(rows, 0, 1, stride=1, stride_axis=0)[:, :ATT_BAND]
        bm_ref[h] = jnp.where(valid, tile, NEG_BIG)


def _att_pad_kv(k_ref, v_ref, kp_ref, vp_ref):
    kp_ref[pl.ds(0, ATT_PAD), :] = jnp.zeros((ATT_PAD, MIXW), bf16)
    vp_ref[pl.ds(0, ATT_PAD), :] = jnp.zeros((ATT_PAD, MIXW), bf16)
    kp_ref[pl.ds(ATT_PAD, SEQ), :] = k_ref[...].astype(bf16)
    vp_ref[pl.ds(ATT_PAD, SEQ), :] = v_ref[...].astype(bf16)


def _att_probs(qm, kb, bm, key_ok):
    s = _dot_nt(qm, kb) + bm
    s = jnp.where(key_ok, s, NEG_BIG)
    m = jnp.max(s, axis=-1, keepdims=True)
    e = jnp.exp(s - m)
    return e / jnp.sum(e, axis=-1, keepdims=True)


def _attn_fwd(zm, rb8):
    def body(q_ref, k_ref, v_ref, rb_ref, o_ref, kp_ref, vp_ref, bm_ref):
        _att_pad_kv(k_ref, v_ref, kp_ref, vp_ref)
        _att_bias_tiles(rb_ref, bm_ref)
        hm = _head_masks()

        def pair(p, carry):
            r0 = pl.multiple_of(p * ATT_PAIR, ATT_PAIR)
            q = q_ref[pl.ds(r0, ATT_PAIR), :] * (HDIM ** -0.5)
            kb = kp_ref[pl.ds(r0, ATT_BAND), :]
            vb = vp_ref[pl.ds(r0, ATT_BAND), :]
            key_ok = (lax.broadcasted_iota(jnp.int32, (1, ATT_BAND), 1) + (r0 - ATT_PAD)) >= 0
            o = jnp.zeros((ATT_PAIR, MIXW), f32)
            for h in range(NHEAD):
                qm = jnp.where(hm[h], q, 0.0)
                p_h = _att_probs(qm, kb, bm_ref[h], key_ok)
                o = o + jnp.where(hm[h], _dot(p_h, vb), 0.0)
            o_ref[pl.ds(r0, ATT_PAIR), :] = o.astype(bf16)
            return carry

        lax.fori_loop(0, SEQ // ATT_PAIR, pair, 0)

    col = lambda j: pl.BlockSpec((SEQ, MIXW), lambda i: (0, j))
    return pl.pallas_call(
        body, name="attn_fwd", grid=(1,),
        in_specs=[col(0), col(1), col(2), pl.BlockSpec((8, REL_SIZE), lambda i: (0, 0))],
        out_specs=pl.BlockSpec((SEQ, MIXW), lambda i: (0, 0)),
        out_shape=_sds((SEQ, MIXW), bf16),
        scratch_shapes=[pltpu.VMEM((SEQ + ATT_PAD, MIXW), bf16), pltpu.VMEM((SEQ + ATT_PAD, MIXW), bf16),
                        pltpu.VMEM((NHEAD, ATT_PAIR, ATT_BAND), f32)],
        compiler_params=_params(("arbitrary",)),
    )(zm, zm, zm, rb8)


def _hg_gates(q, fz, lb):
    sq = _sigmoid(q)
    sg = _sigmoid(fz)
    f = lb + (1.0 - lb) * sg
    return q * sq, (1.0 - lb) * (1.0 - sg), jnp.log(jnp.maximum(f, LOG_FLOOR)), sq, sg, f


def _hg_prepare(q_ref, f_ref, lb, qf_s, kf_s, b_s, qd_s, kd_s, dec_s):
    b = None
    for t in range(HG_T):
        qf, kf, lf, _, _, _ = _hg_gates(q_ref[:, t, :], f_ref[:, t, :], lb)
        b = lf if b is None else b + lf
        qf_s[:, t, :] = qf
        kf_s[:, t, :] = kf
        b_s[:, t, :] = b
    b_last = b
    dec_s[...] = jnp.broadcast_to(jnp.exp(b_last)[:, None, :], (HG_N, 8, MIXW))
    for t in range(HG_T):
        bt = b_s[:, t, :]
        qd_s[:, t, :] = qf_s[:, t, :] * jnp.exp(bt)
        kd_s[:, t, :] = kf_s[:, t, :] * jnp.exp(b_last - bt)


def _hg_scores(t, qf_s, kf_s, b_s, w_s, hm):
    qt = qf_s[:, t, :]
    bt = b_s[:, t, :]
    for s in range(t + 1):
        w = qt * kf_s[:, s, :]
        if s < t:
            w = w * jnp.exp(bt - b_s[:, s, :])
        w_s[pl.ds(s * HG_N, HG_N), :] = w.astype(bf16)
    return jnp.dot(w_s[pl.ds(0, (t + 1) * HG_N), :], hm, preferred_element_type=f32)


def _hgrn_fwd(zm3, lb, ng):
    def body(q_ref, f_ref, i_ref, g_ref, lb_ref, ng_ref, o_ref, oraw_ref,
             qf_s, kf_s, b_s, qd_s, kd_s, dec_s, w_s, st_s):
        lb = lb_ref[...]
        hm = _same_head(MIXW, HDIM, bf16)
        hmf = _same_head(MIXW, HDIM, f32)
        _hg_prepare(q_ref, f_ref, lb, qf_s, kf_s, b_s, qd_s, kd_s, dec_s)
        for t in range(HG_T):
            p = _hg_scores(t, qf_s, kf_s, b_s, w_s, hm)
            acc = jnp.zeros((HG_N, MIXW), f32)
            for s in range(t + 1):
                acc = acc + p[s * HG_N:(s + 1) * HG_N] * i_ref[:, s, :]
            oraw_ref[:, t, :] = acc
        st_s[...] = jnp.zeros((MIXW, MIXW), f32)

        def step(n, carry):
            st = st_s[...]
            oraw_ref[n] = oraw_ref[n] + _dot_nt(qd_s[n], st)
            st_s[...] = st * dec_s[n][0:1] + _dot_tn(i_ref[n], kd_s[n]) * hmf
            return carry

        lax.fori_loop(0, HG_N, step, 0)
        ngv = ng_ref[...]
        for t in range(HG_T):
            o = oraw_ref[:, t, :]
            ms = _dot_hl(o * o, hm) * (1.0 / HDIM)
            o_ref[:, t, :] = (o * lax.rsqrt(ms + EPS) * ngv * _silu(g_ref[:, t, :])).astype(bf16)

    col = lambda j: pl.BlockSpec((HG_N, HG_T, MIXW), lambda i: (0, 0, j))
    vec = pl.BlockSpec((1, MIXW), lambda i: (0, 0))
    blk = pl.BlockSpec((HG_N, HG_T, MIXW), lambda i: (0, 0, 0))
    s3 = pltpu.VMEM((HG_N, HG_T, MIXW), f32)
    return pl.pallas_call(
        body, name="hgrn_fwd", grid=(1,),
        in_specs=[col(3), col(4), col(5), col(6), vec, vec],
        out_specs=[blk, blk],
        out_shape=[_sds((HG_N, HG_T, MIXW), bf16), _sds((HG_N, HG_T, MIXW), f32)],
        scratch_shapes=[s3, s3, s3, s3, s3, pltpu.VMEM((HG_N, 8, MIXW), f32),
                        pltpu.VMEM((HG_T * HG_N, MIXW), bf16), pltpu.VMEM((MIXW, MIXW), f32)],
        compiler_params=_params(("arbitrary",)),
    )(zm3, zm3, zm3, zm3, lb, ng)


def _gm_weights(ws_ref):
    tril = lax.broadcasted_iota(jnp.int32, (GM_T, GM_T), 0) >= lax.broadcasted_iota(jnp.int32, (GM_T, GM_T), 1)
    return tril, [jnp.where(tril, ws_ref[g], 0.0).astype(bf16) for g in range(NHEAD)]


def _gm_expand():
    r = lax.broadcasted_iota(jnp.int32, (8, MIXW), 0)
    c = lax.broadcasted_iota(jnp.int32, (8, MIXW), 1) // HDIM
    return (r == c).astype(bf16)


def _gm_mixed(vn, wts, bias, hm):
    vb = vn.astype(bf16)
    mixed = bias
    for g in range(NHEAD):
        mixed = mixed + jnp.where(hm[g], jnp.dot(wts[g], vb, preferred_element_type=f32), 0.0)
    return mixed


def _gm_bias(bs_ref):
    hi, lo = _split(bs_ref[...])
    et = _gm_expand()
    dn = (((0,), (0,)), ((), ()))
    return lax.dot_general(hi, et, dn, preferred_element_type=f32) + lax.dot_general(lo, et, dn, preferred_element_type=f32)


def _gmlp_fwd(zm, ng, ws, bs8):
    def body(u_ref, v_ref, ng_ref, ws_ref, bs_ref, o_ref):
        hm = _head_masks()
        _, wts = _gm_weights(ws_ref)
        bias = _gm_bias(bs_ref)
        ngv = ng_ref[...]

        def blk(n, carry):
            rows = pl.ds(pl.multiple_of(n * GM_T, GM_T), GM_T)
            vn = _rms(_gelu(v_ref[rows, :]), ngv)
            o_ref[rows, :] = (_gelu(u_ref[rows, :]) * _gm_mixed(vn, wts, bias, hm)).astype(bf16)
            return carry

        lax.fori_loop(0, SEQ // GM_T, blk, 0)

    col = lambda j: pl.BlockSpec((SEQ, MIXW), lambda i: (0, j))
    return pl.pallas_call(
        body, name="gmlp_fwd", grid=(1,),
        in_specs=[col(7), col(8), pl.BlockSpec((1, MIXW), lambda i: (0, 0)),
                  pl.BlockSpec((NHEAD, GM_T, GM_T), lambda i: (0, 0, 0)), pl.BlockSpec((8, GM_T), lambda i: (0, 0))],
        out_specs=pl.BlockSpec((SEQ, MIXW), lambda i: (0, 0)),
        out_shape=_sds((SEQ, MIXW), bf16),
        compiler_params=_params(("arbitrary",)),
    )(zm, zm, ng, ws, bs8)


def _lru_conv(x_ref, cw_ref, cb_ref, xp_s, xc_s):
    xp_s[pl.ds(0, 8), :] = jnp.zeros((8, MIXW), f32)
    xp_s[pl.ds(8, SEQ), :] = x_ref[...]
    cw = cw_ref[...]
    xc = cb_ref[...] + x_ref[...] * cw[3:4]
    for k in range(1, 4):
        xc = xc + xp_s[pl.ds(8 - k, SEQ), :] * cw[3 - k:4 - k]
    xc_s[...] = xc


def _lru_gates(xc, wa, ba, wx, bx, sp, first_row):
    r = _sigmoid(_dot(xc, wa) + ba)
    ig = _sigmoid(_dot(xc, wx) + bx)
    la = (-LRU_C) * r * sp
    a = jnp.exp(la)
    th = jnp.tanh(la)
    m2 = -2.0 * th / (1.0 - th)
    mult = jnp.where(first_row, 1.0, jnp.sqrt(jnp.maximum(m2, 0.0)))
    return a, mult, r, ig, m2


def _lru_scan(a, b, rev):
    row = lax.broadcasted_iota(jnp.int32, (LRU_T, 1), 0)
    k = 1
    while k < LRU_T:
        ok = (row < LRU_T - k) if rev else (row >= k)
        sh = (LRU_T - k) if rev else k
        a_sh = jnp.where(ok, pltpu.roll(a, sh, 0), 1.0)
        b_sh = jnp.where(ok, pltpu.roll(b, sh, 0), 0.0)
        b = b + a * b_sh
        a = a * a_sh
        k *= 2
    return a, b


def _lru_fwd(zm, cw8, cb, wa, ba, wx, bx, lam):
    def body(x_ref, g_ref, cw_ref, cb_ref, wa_ref, ba_ref, wx_ref, bx_ref, lam_ref, o_ref, h_ref, xp_s, xc_s):
        _lru_conv(x_ref, cw_ref, cb_ref, xp_s, xc_s)
        sp = jax.nn.softplus(-lam_ref[...])
        wa_v, wx_v, ba_v, bx_v = wa_ref[...], wx_ref[...], ba_ref[...], bx_ref[...]

        def chunk(c, h_prev):
            rows = pl.ds(pl.multiple_of(c * LRU_T, LRU_T), LRU_T)
            first = (lax.broadcasted_iota(jnp.int32, (LRU_T, 1), 0) + c * LRU_T) == 0
            xc = xc_s[rows, :]
            a, mult, _, ig, _ = _lru_gates(xc, wa_v, ba_v, wx_v, bx_v, sp, first)
            acum, hloc = _lru_scan(a, mult * (ig * xc), False)
            h = hloc + acum * h_prev
            h_ref[rows, :] = h
            o_ref[rows, :] = (h * _gelu(g_ref[rows, :])).astype(bf16)
            return h[LRU_T - 1:LRU_T, :]

        lax.fori_loop(0, SEQ // LRU_T, chunk, jnp.zeros((1, MIXW), f32))

    col = lambda j: pl.BlockSpec((SEQ, MIXW), lambda i: (0, j))
    vec = pl.BlockSpec((1, MIXW), lambda i: (0, 0))
    mat = pl.BlockSpec((MIXW, MIXW), lambda i: (0, 0))
    out = pl.BlockSpec((SEQ, MIXW), lambda i: (0, 0))
    return pl.pallas_call(
        body, name="lru_fwd", grid=(1,),
        in_specs=[col(9), col(10), pl.BlockSpec((8, MIXW), lambda i: (0, 0)), vec, mat, vec, mat, vec, vec],
        out_specs=[out, out],
        out_shape=[_sds((SEQ, MIXW), bf16), _sds((SEQ, MIXW), f32)],
        scratch_shapes=[pltpu.VMEM((SEQ + 8, MIXW), f32), pltpu.VMEM((SEQ, MIXW), f32)],
        compiler_params=_params(("arbitrary",)),
    )(zm, zm, cw8, cb, wa, ba, wx, bx, lam)


def _block_diag(w):
    out = jnp.zeros((MIXW, MIXW), w.dtype)
    for h in range(NHEAD):
        out = lax.dynamic_update_slice(out, w[h], (h * HDIM, h * HDIM))
    return out


def _diag_blocks(w):
    return jnp.stack([w[h * HDIM:(h + 1) * HDIM, h * HDIM:(h + 1) * HDIM] for h in range(NHEAD)])


ROW_TILE = 256


def _merge_fwd(outs, zg, wb, wo, x, g2):
    def body(oa_ref, ob_ref, oc_ref, od_ref, zg_ref, wb_ref, wo_ref, x_ref, g_ref, xo_ref, mg_ref, y_ref):
        merged = jnp.zeros((ROW_TILE, DM), f32)
        for n, o_ref in enumerate((oa_ref, ob_ref, oc_ref, od_ref)):
            proj = jnp.dot(o_ref[...], wb_ref[n], preferred_element_type=f32)
            merged = merged + _sigmoid(zg_ref[:, n * DM:(n + 1) * DM]) * proj
        mb = merged.astype(bf16)
        y = jnp.dot(mb, wo_ref[...], preferred_element_type=f32)
        mg_ref[...] = mb
        y_ref[...] = y
        xo_ref[...] = x_ref[...] + _rms(y, g_ref[...])

    row = lambda w: pl.BlockSpec((ROW_TILE, w), lambda i: (i, 0))
    return pl.pallas_call(
        body, name="merge_fwd", grid=(SEQ // ROW_TILE,),
        in_specs=[row(MIXW)] * 4 + [row(NGATE), pl.BlockSpec((NHEAD, MIXW, DM), lambda i: (0, 0, 0)),
                                    pl.BlockSpec((DM, DM), lambda i: (0, 0)), row(DM), pl.BlockSpec((1, DM), lambda i: (0, 0))],
        out_specs=[row(DM), row(DM), row(DM)],
        out_shape=[_sds((SEQ, DM), f32), _sds((SEQ, DM), bf16), _sds((SEQ, DM), f32)],
        compiler_params=_params(("parallel",)),
    )(*outs, zg, wb, wo, x, g2)


def _ffn_out(u, w2, x, g4):
    def body(u_ref, w_ref, x_ref, g_ref, xo_ref, f_ref):
        a = _silu(u_ref[:, :FFH]) * u_ref[:, FFH:]
        f = jnp.dot(a.astype(bf16), w_ref[...], preferred_element_type=f32)
        f_ref[...] = f
        xo_ref[...] = x_ref[...] + _rms(f, g_ref[...])

    row = lambda w: pl.BlockSpec((ROW_TILE, w), lambda i: (i, 0))
    return pl.pallas_call(
        body, name="ffn_out", grid=(SEQ // ROW_TILE,),
        in_specs=[row(2 * FFH), pl.BlockSpec((FFH, DM), lambda i: (0, 0)), row(DM), pl.BlockSpec((1, DM), lambda i: (0, 0))],
        out_specs=[row(DM), row(DM)],
        out_shape=[_sds((SEQ, DM), f32), _sds((SEQ, DM), f32)],
        compiler_params=_params(("parallel",)),
    )(u, w2, x, g4)


def _loss_head(x, tgt):
    tm = 512

    def body(x_ref, t_ref, l_ref, dx_ref):
        @pl.when(pl.program_id(0) == 0)
        def _():
            l_ref[...] = jnp.zeros((1, 1), f32)

        d = x_ref[...] - t_ref[...]
        dx_ref[...] = d * (1.0 / DM)
        l_ref[...] += (0.5 / DM) * jnp.sum(d * d).reshape(1, 1)

    row = pl.BlockSpec((tm, DM), lambda i: (i, 0))
    return pl.pallas_call(
        body, name="loss_head", grid=(SEQ // tm,),
        in_specs=[row, row], out_specs=[pl.BlockSpec((1, 1), lambda i: (0, 0)), row],
        out_shape=[_sds((1, 1), f32), _sds((SEQ, DM), f32)],
        compiler_params=_params(("arbitrary",)),
    )(x, tgt)


def _lb_fwd(logits):
    def body(lg_ref, o_ref):
        lg = lg_ref[...]
        e = jnp.exp(lg - jnp.max(lg, axis=0, keepdims=True))
        p = e / jnp.sum(e, axis=0, keepdims=True)
        acc = jnp.zeros((1, MIXW), f32)
        o_ref[0:1, :] = acc
        for l in range(1, DEPTH):
            acc = acc + p[l:l + 1]
            o_ref[l:l + 1, :] = acc

    return pl.pallas_call(body, name="lb_fwd", out_shape=_sds((DEPTH, MIXW), f32))(logits)


def _lb_bwd(logits, dlbs):
    def body(lg_ref, d_ref, o_ref):
        lg = lg_ref[...]
        e = jnp.exp(lg - jnp.max(lg, axis=0, keepdims=True))
        p = e / jnp.sum(e, axis=0, keepdims=True)
        d = d_ref[...]
        dp = [jnp.zeros((1, MIXW), f32)] * DEPTH
        acc = jnp.zeros((1, MIXW), f32)
        for j in range(DEPTH - 1, 0, -1):
            acc = acc + d[j:j + 1]
            dp[j] = acc
        inner = sum(p[j:j + 1] * dp[j] for j in range(DEPTH))
        for j in range(DEPTH):
            o_ref[j:j + 1, :] = p[j:j + 1] * (dp[j] - inner)

    return pl.pallas_call(body, name="lb_bwd", out_shape=_sds((DEPTH, MIXW), f32))(logits, dlbs)


def _pad_rows(a, rows=8):
    return jnp.concatenate([a, jnp.zeros((rows - a.shape[0], a.shape[1]), a.dtype)], axis=0)


def _layer_params(l, full, small, lbs):
    row = lambda name: small[name][l][None]
    return dict(
        wm=full["w_in"][:, :NMIX], wgt=full["w_in"][:, NMIX:], wb=full["w_branch"], wo=full["w_out"],
        w1=full["w_ffn_in"], w2=full["w_ffn_out"],
        g1=row("norm_mix_pre"), g2=row("norm_mix_post"), g3=row("norm_ffn_pre"), g4=row("norm_ffn_post"),
        rb8=_pad_rows(small["attn_rel_bias"][l]), lb=lbs[l][None], hng=row("hgrn_norm_g"),
        gng=row("gmlp_norm_g"), gws=small["gmlp_ws"][l], gbs8=_pad_rows(small["gmlp_bs"][l]),
        cw8=_pad_rows(small["lru_conv_w"][l]), cb=row("lru_conv_b"),
        wa=_block_diag(small["lru_wa"][l]).astype(bf16), ba=row("lru_ba"),
        wx=_block_diag(small["lru_wx"][l]).astype(bf16), bx=row("lru_bx"), lam=row("lru_lambda"),
    )


def _layer_fwd(x, p):
    zm, h = _norm_matmul(x, p["g1"], p["wm"], 1408)
    zg = _matmul(h, p["wgt"], 1024)
    oa = _attn_fwd(zm, p["rb8"])
    ob3, obraw3 = _hgrn_fwd(zm.reshape(HG_N, HG_T, NMIX), p["lb"], p["hng"])
    oc = _gmlp_fwd(zm, p["gng"], p["gws"], p["gbs8"])
    od, hd = _lru_fwd(zm, p["cw8"], p["cb"], p["wa"], p["ba"], p["wx"], p["bx"], p["lam"])
    outs = (oa, ob3.reshape(SEQ, MIXW), oc, od)
    x1, merged, y = _merge_fwd(outs, zg, p["wb"], p["wo"], x, p["g2"])
    u, h2 = _norm_matmul(x1, p["g3"], p["w1"], 1408)
    x2, f = _ffn_out(u, p["w2"], x1, p["g4"])
    saved = dict(x=x, h=h, zm=zm, zg=zg, outs=outs, obraw3=obraw3, hd=hd, x1=x1, merged=merged, y=y, u=u, h2=h2, f=f)
    return x2, saved


def _att_bias_grad(db_ref, o_ref):
    r = lax.broadcasted_iota(jnp.int32, (ATT_PAIR, ATT_PAIR), 0)
    c = lax.broadcasted_iota(jnp.int32, (ATT_PAIR, ATT_PAIR), 1)
    flip = (r + c == ATT_PAIR - 1).astype(bf16)
    rows = []
    for h in range(NHEAD):
        d = jnp.concatenate([db_ref[h], jnp.zeros((ATT_PAIR, ATT_WV - ATT_BAND), f32)], axis=1)
        hi, lo = _split(d)
        rev = jnp.dot(flip, hi, preferred_element_type=f32) + jnp.dot(flip, lo, preferred_element_type=f32)
        lined = pltpu.roll(rev, ATT_WV - (ATT_PAIR - 1), 1, stride=1, stride_axis=0)
        rows.append(jnp.sum(lined, axis=0, keepdims=True))
    dwv = jnp.concatenate(rows + [jnp.zeros((8 - NHEAD, ATT_WV), f32)], axis=0)
    hi, lo = _split(dwv)
    m = _att_offset_map()
    dn = (((1,), (1,)), ((), ()))
    o_ref[...] = lax.dot_general(hi, m, dn, preferred_element_type=f32) + lax.dot_general(lo, m, dn, preferred_element_type=f32)


def _attn_bwd(zm, rb8, do):
    def body(q_ref, k_ref, v_ref, rb_ref, do_ref, dz_ref, drb_ref, kp_ref, vp_ref, bm_ref, dk_s, dv_s, db_s):
        _att_pad_kv(k_ref, v_ref, kp_ref, vp_ref)
        _att_bias_tiles(rb_ref, bm_ref)
        dk_s[...] = jnp.zeros_like(dk_s)
        dv_s[...] = jnp.zeros_like(dv_s)
        db_s[...] = jnp.zeros_like(db_s)
        hm = _head_masks()
        scale = HDIM ** -0.5

        def pair(p, carry):
            r0 = pl.multiple_of(p * ATT_PAIR, ATT_PAIR)
            q = q_ref[pl.ds(r0, ATT_PAIR), :] * scale
            dout = do_ref[pl.ds(r0, ATT_PAIR), :]
            kb = kp_ref[pl.ds(r0, ATT_BAND), :]
            vb = vp_ref[pl.ds(r0, ATT_BAND), :]
            key_ok = (lax.broadcasted_iota(jnp.int32, (1, ATT_BAND), 1) + (r0 - ATT_PAD)) >= 0
            dq = jnp.zeros((ATT_PAIR, MIXW), f32)
            dkb = jnp.zeros((ATT_BAND, MIXW), f32)
            dvb = jnp.zeros((ATT_BAND, MIXW), f32)
            for h in range(NHEAD):
                qm = jnp.where(hm[h], q, 0.0).astype(bf16)
                dom = jnp.where(hm[h], dout, 0.0).astype(bf16)
                p_h = _att_probs(qm, kb, bm_ref[h], key_ok)
                dp = _dot_nt(dom, vb)
                ds = p_h * (dp - jnp.sum(dp * p_h, axis=-1, keepdims=True))
                dsb = ds.astype(bf16)
                dq = dq + jnp.where(hm[h], _dot(dsb, kb), 0.0)
                dkb = dkb + _dot_tn(dsb, qm)
                dvb = dvb + _dot_tn(p_h, dom)
                db_s[h] = db_s[h] + ds
            dz_ref[pl.ds(r0, ATT_PAIR), 0:MIXW] = (dq * scale).astype(bf16)
            dk_s[pl.ds(r0, ATT_BAND), :] = dk_s[pl.ds(r0, ATT_BAND), :] + dkb
            dv_s[pl.ds(r0, ATT_BAND), :] = dv_s[pl.ds(r0, ATT_BAND), :] + dvb
            return carry

        lax.fori_loop(0, SEQ // ATT_PAIR, pair, 0)
        dz_ref[:, MIXW:2 * MIXW] = dk_s[pl.ds(ATT_PAD, SEQ), :].astype(bf16)
        dz_ref[:, 2 * MIXW:3 * MIXW] = dv_s[pl.ds(ATT_PAD, SEQ), :].astype(bf16)
        _att_bias_grad(db_s, drb_ref)

    col = lambda j: pl.BlockSpec((SEQ, MIXW), lambda i: (0, j))
    return pl.pallas_call(
        body, name="attn_bwd", grid=(1,),
        in_specs=[col(0), col(1), col(2), pl.BlockSpec((8, REL_SIZE), lambda i: (0, 0)), pl.BlockSpec((SEQ, MIXW), lambda i: (0, 0))],
        out_specs=[pl.BlockSpec((SEQ, 3 * MIXW), lambda i: (0, 0)), pl.BlockSpec((8, REL_SIZE), lambda i: (0, 0))],
        out_shape=[_sds((SEQ, 3 * MIXW), bf16), _sds((8, REL_SIZE), f32)],
        scratch_shapes=[pltpu.VMEM((SEQ + ATT_PAD, MIXW), bf16), pltpu.VMEM((SEQ + ATT_PAD, MIXW), bf16),
                        pltpu.VMEM((NHEAD, ATT_PAIR, ATT_BAND), f32),
                        pltpu.VMEM((SEQ + ATT_PAD, MIXW), f32), pltpu.VMEM((SEQ + ATT_PAD, MIXW), f32),
                        pltpu.VMEM((NHEAD, ATT_PAIR, ATT_BAND), f32)],
        compiler_params=_params(("arbitrary",)),
    )(zm, zm, zm, rb8, do)


def _hgrn_out_bwd(zm3, ng, oraw3, do3):
    def body(g_ref, ng_ref, o_ref, do_ref, dor_ref, dg_ref, dng_ref):
        hm = _same_head(MIXW, HDIM, bf16)
        ngv = ng_ref[...]
        dng = jnp.zeros((1, MIXW), f32)
        for t in range(HG_T):
            o, g, d = o_ref[:, t, :], g_ref[:, t, :], do_ref[:, t, :]
            rs = lax.rsqrt(_dot_hl(o * o, hm) * (1.0 / HDIM) + EPS)
            y1 = o * rs
            dy2 = d * _silu(g)
            dg_ref[:, t, :] = (d * y1 * ngv * _dsilu(g)).astype(bf16)
            dng = dng + jnp.sum(dy2 * y1, axis=0, keepdims=True)
            dy1 = dy2 * ngv
            dor_ref[:, t, :] = rs * (dy1 - y1 * (_dot_hl(dy1 * y1, hm) * (1.0 / HDIM)))
        dng_ref[...] = jnp.broadcast_to(dng, (8, MIXW))

    blk = pl.BlockSpec((HG_N, HG_T, MIXW), lambda i: (0, 0, 0))
    return pl.pallas_call(
        body, name="hgrn_out_bwd", grid=(1,),
        in_specs=[pl.BlockSpec((HG_N, HG_T, MIXW), lambda i: (0, 0, 6)), pl.BlockSpec((1, MIXW), lambda i: (0, 0)), blk, blk],
        out_specs=[blk, blk, pl.BlockSpec((8, MIXW), lambda i: (0, 0))],
        out_shape=[_sds((HG_N, HG_T, MIXW), f32), _sds((HG_N, HG_T, MIXW), bf16), _sds((8, MIXW), f32)],
        compiler_params=_params(("arbitrary",)),
    )(zm3, ng, oraw3, do3)


def _hgrn_bwd(zm3, lb, dor3):
    def body(q_ref, f_ref, i_ref, lb_ref, dor_ref, dz_ref, dlb_ref,
             qf_s, kf_s, b_s, dq_s, dk_s, db_s, dv_s, w_s, x_s, st_s, cur_s):
        lb = lb_ref[...]
        hm = _same_head(MIXW, HDIM, bf16)
        hmf = _same_head(MIXW, HDIM, f32)
        b = None
        for t in range(HG_T):
            qf, kf, lf, _, _, _ = _hg_gates(q_ref[:, t, :], f_ref[:, t, :], lb)
            b = lf if b is None else b + lf
            qf_s[:, t, :] = qf
            kf_s[:, t, :] = kf
            b_s[:, t, :] = b

        def block_terms(n):
            bn = b_s[n]
            bl = bn[HG_T - 1:HG_T]
            eb = jnp.exp(bn)
            ek = jnp.exp(bl - bn)
            return qf_s[n] * eb, kf_s[n] * ek, jnp.exp(bl), eb, ek

        cur_s[...] = jnp.zeros((MIXW, MIXW), f32)

        def fwd_step(n, carry):
            _, kd, dec, _, _ = block_terms(n)
            st = cur_s[...]
            st_s[n] = st.astype(bf16)
            cur_s[...] = st * dec + _dot_tn(i_ref[n], kd) * hmf
            return carry

        lax.fori_loop(0, HG_N, fwd_step, 0)
        cur_s[...] = jnp.zeros((MIXW, MIXW), f32)
        last = lax.broadcasted_iota(jnp.int32, (HG_T, 1), 0) == HG_T - 1

        def bwd_step(j, carry):
            n = HG_N - 1 - j
            qd, kd, dec, eb, ek = block_terms(n)
            v, do_n = i_ref[n], dor_ref[n]
            dst = cur_s[...]
            st = st_s[n]
            dqd = _dot(do_n, st)
            dkd = _dot(v, dst)
            ddec = jnp.sum(dst * st.astype(f32), axis=0, keepdims=True)
            cur_s[...] = dst * dec + _dot_tn(do_n, qd) * hmf
            dq_s[n] = dqd * eb
            dk_s[n] = dkd * ek
            dv_s[n] = _dot_nt(kd, dst)
            dbl = jnp.sum(dkd * kd, axis=0, keepdims=True) + ddec * dec
            db_s[n] = dqd * qd - dkd * kd + jnp.where(last, dbl, 0.0)
            return carry

        lax.fori_loop(0, HG_N, bwd_step, 0)
        for t in range(HG_T):
            qt, bt, dot_t = qf_s[:, t, :], b_s[:, t, :], dor_ref[:, t, :]
            for s in range(t + 1):
                w = qt * kf_s[:, s, :]
                if s < t:
                    w = w * jnp.exp(bt - b_s[:, s, :])
                w_s[pl.ds(s * HG_N, HG_N), :] = w.astype(bf16)
                x_s[pl.ds(s * HG_N, HG_N), :] = (dot_t * i_ref[:, s, :]).astype(bf16)
            p = jnp.dot(w_s[pl.ds(0, (t + 1) * HG_N), :], hm, preferred_element_type=f32)
            dp = jnp.dot(x_s[pl.ds(0, (t + 1) * HG_N), :], hm, preferred_element_type=f32)
            dq_t = jnp.zeros((HG_N, MIXW), f32)
            db_t = jnp.zeros((HG_N, MIXW), f32)
            for s in range(t + 1):
                ps = p[s * HG_N:(s + 1) * HG_N]
                dps = dp[s * HG_N:(s + 1) * HG_N]
                ks = kf_s[:, s, :]
                dv_s[:, s, :] = dv_s[:, s, :] + ps * dot_t
                if s < t:
                    dec_ts = jnp.exp(bt - b_s[:, s, :])
                    g1 = dps * ks * dec_ts
                    dk_s[:, s, :] = dk_s[:, s, :] + dps * qt * dec_ts
                    gw = g1 * qt
                    db_t = db_t + gw
                    db_s[:, s, :] = db_s[:, s, :] - gw
                else:
                    g1 = dps * ks
                    dk_s[:, s, :] = dk_s[:, s, :] + dps * qt
                dq_t = dq_t + g1
            dq_s[:, t, :] = dq_s[:, t, :] + dq_t
            db_s[:, t, :] = db_s[:, t, :] + db_t
        run = jnp.zeros((HG_N, MIXW), f32)
        dlb = jnp.zeros((1, MIXW), f32)
        oml = 1.0 - lb
        for t in range(HG_T - 1, -1, -1):
            run = run + db_s[:, t, :]
            q = q_ref[:, t, :]
            _, _, _, sq, sg, f = _hg_gates(q, f_ref[:, t, :], lb)
            dkf = dk_s[:, t, :]
            df = jnp.where(f > LOG_FLOOR, run / f, 0.0)
            dsg = (df - dkf) * oml
            dlb = dlb + jnp.sum((df - dkf) * (1.0 - sg), axis=0, keepdims=True)
            dz_ref[:, t, 0:MIXW] = (dq_s[:, t, :] * sq * (1.0 + q * (1.0 - sq))).astype(bf16)
            dz_ref[:, t, MIXW:2 * MIXW] = (dsg * sg * (1.0 - sg)).astype(bf16)
            dz_ref[:, t, 2 * MIXW:3 * MIXW] = dv_s[:, t, :].astype(bf16)
        dlb_ref[...] = jnp.broadcast_to(dlb, (8, MIXW))

    one = pl.Buffered(1)
    col = lambda j: pl.BlockSpec((HG_N, HG_T, MIXW), lambda i: (0, 0, j), pipeline_mode=one)
    s3 = pltpu.VMEM((HG_N, HG_T, MIXW), f32)
    return pl.pallas_call(
        body, name="hgrn_bwd", grid=(1,),
        in_specs=[col(3), col(4), col(5), pl.BlockSpec((1, MIXW), lambda i: (0, 0)),
                  pl.BlockSpec((HG_N, HG_T, MIXW), lambda i: (0, 0, 0), pipeline_mode=one)],
        out_specs=[pl.BlockSpec((HG_N, HG_T, 3 * MIXW), lambda i: (0, 0, 0)), pl.BlockSpec((8, MIXW), lambda i: (0, 0))],
        out_shape=[_sds((HG_N, HG_T, 3 * MIXW), bf16), _sds((8, MIXW), f32)],
        scratch_shapes=[s3, s3, s3, s3, s3, s3, s3,
                        pltpu.VMEM((HG_T * HG_N, MIXW), bf16), pltpu.VMEM((HG_T * HG_N, MIXW), bf16),
                        pltpu.VMEM((HG_N, MIXW, MIXW), bf16), pltpu.VMEM((MIXW, MIXW), f32)],
        compiler_params=_params(("arbitrary",)),
    )(zm3, zm3, zm3, lb, dor3)


def _gmlp_bwd(zm, ng, ws, bs8, do):
    def body(u_ref, v_ref, ng_ref, ws_ref, bs_ref, do_ref, dz_ref, dws_ref, dng_ref, dbs_ref, dm_s):
        hm = _head_masks()
        tril, wts = _gm_weights(ws_ref)
        bias = _gm_bias(bs_ref)
        ngv = ng_ref[...]
        dws_ref[...] = jnp.zeros_like(dws_ref)
        dm_s[...] = jnp.zeros_like(dm_s)

        def blk(n, dng):
            rows = pl.ds(pl.multiple_of(n * GM_T, GM_T), GM_T)
            cu, cv, d = u_ref[rows, :], v_ref[rows, :], do_ref[rows, :]
            v = _gelu(cv)
            r = lax.rsqrt(jnp.mean(v * v, axis=-1, keepdims=True) + EPS)
            vh = v * r
            vn = vh * ngv
            u = _gelu(cu)
            dm = d * u
            dmb, vnb = dm.astype(bf16), vn.astype(bf16)
            dvn = jnp.zeros((GM_T, MIXW), f32)
            for g in range(NHEAD):
                dws_ref[g] = dws_ref[g] + _dot_nt(jnp.where(hm[g], dm, 0.0), vnb)
                dvn = dvn + jnp.where(hm[g], _dot_tn(wts[g], dmb), 0.0)
            dm_s[...] = dm_s[...] + dm
            dvh = dvn * ngv
            dv = r * (dvh - vh * jnp.mean(dvh * vh, axis=-1, keepdims=True))
            dz_ref[rows, 0:MIXW] = (d * _gm_mixed(vn, wts, bias, hm) * _dgelu(cu)).astype(bf16)
            dz_ref[rows, MIXW:2 * MIXW] = (dv * _dgelu(cv)).astype(bf16)
            return dng + jnp.sum(dvn * vh, axis=0, keepdims=True)

        dng = lax.fori_loop(0, SEQ // GM_T, blk, jnp.zeros((1, MIXW), f32))
        dng_ref[...] = jnp.broadcast_to(dng, (8, MIXW))
        for g in range(NHEAD):
            dws_ref[g] = jnp.where(tril, dws_ref[g], 0.0)
        dbs_ref[...] = _dot_nt_hl(_gm_expand(), dm_s[...])

    col = lambda j: pl.BlockSpec((SEQ, MIXW), lambda i: (0, j))
    return pl.pallas_call(
        body, name="gmlp_bwd", grid=(1,),
        in_specs=[col(7), col(8), pl.BlockSpec((1, MIXW), lambda i: (0, 0)),
                  pl.BlockSpec((NHEAD, GM_T, GM_T), lambda i: (0, 0, 0)), pl.BlockSpec((8, GM_T), lambda i: (0, 0)),
                  pl.BlockSpec((SEQ, MIXW), lambda i: (0, 0))],
        out_specs=[pl.BlockSpec((SEQ, 2 * MIXW), lambda i: (0, 0)), pl.BlockSpec((NHEAD, GM_T, GM_T), lambda i: (0, 0, 0)),
                   pl.BlockSpec((8, MIXW), lambda i: (0, 0)), pl.BlockSpec((8, GM_T), lambda i: (0, 0))],
        out_shape=[_sds((SEQ, 2 * MIXW), bf16), _sds((NHEAD, GM_T, GM_T), f32), _sds((8, MIXW), f32), _sds((8, GM_T), f32)],
        scratch_shapes=[pltpu.VMEM((GM_T, MIXW), f32)],
        compiler_params=_params(("arbitrary",)),
    )(zm, zm, ng, ws, bs8, do)


def _lru_bwd(zm, cw8, cb, wa, ba, wx, bx, lam, hd, do):
    nchunk = SEQ // LRU_T

    def body(x_ref, g_ref, cw_ref, cb_ref, wa_ref, ba_ref, wx_ref, bx_ref, lam_ref, h_ref, do_ref,
             dz_ref, dwa_ref, dwx_ref, dcw_ref, dvec_ref, xp_s, xc_s, dxc_s):
        _lru_conv(x_ref, cw_ref, cb_ref, xp_s, xc_s)
        lam_v = lam_ref[...]
        sp = jax.nn.softplus(-lam_v)
        sgl = _sigmoid(-lam_v)
        wa_v, wx_v, ba_v, bx_v = wa_ref[...], wx_ref[...], ba_ref[...], bx_ref[...]
        dwa_ref[...] = jnp.zeros_like(dwa_ref)
        dwx_ref[...] = jnp.zeros_like(dwx_ref)
        dxc_s[pl.ds(SEQ, 8), :] = jnp.zeros((8, MIXW), f32)
        row = lax.broadcasted_iota(jnp.int32, (LRU_T, 1), 0)
        zero = jnp.zeros((1, MIXW), f32)

        def chunk(j, carry):
            dh_next, a_next, dba, dbx, dlam = carry
            c = nchunk - 1 - j
            rows = pl.ds(pl.multiple_of(c * LRU_T, LRU_T), LRU_T)
            prev = pl.ds(pl.multiple_of(jnp.maximum(c - 1, 0) * LRU_T, LRU_T), LRU_T)
            first = (row + c * LRU_T) == 0
            xc, gate, d, h = xc_s[rows, :], g_ref[rows, :], do_ref[rows, :], h_ref[rows, :]
            a, mult, r, ig, m2 = _lru_gates(xc, wa_v, ba_v, wx_v, bx_v, sp, first)
            h_last = jnp.where(c > 0, h_ref[prev, :][LRU_T - 1:LRU_T, :], 0.0)
            h_m1 = jnp.where(row == 0, h_last, pltpu.roll(h, 1, 0))
            a_up = jnp.where(row == LRU_T - 1, a_next, pltpu.roll(a, LRU_T - 1, 0))
            acum, dh_loc = _lru_scan(a_up, d * _gelu(gate), True)
            dh = dh_loc + acum * dh_next
            dmult = jnp.where(first, 0.0, dh * (ig * xc))
            msq = jnp.sqrt(jnp.maximum(m2, 0.0))
            dla = dh * h_m1 * a + jnp.where(m2 > 0.0, -dmult * (1.0 - m2) / msq, 0.0)
            dpr = dla * (-LRU_C) * sp * r * (1.0 - r)
            dpi = dh * mult * xc * ig * (1.0 - ig)
            dxc_s[rows, :] = dh * mult * ig + _dot_nt(dpr, wa_v) + _dot_nt(dpi, wx_v)
            dwa_ref[...] = dwa_ref[...] + _dot_tn(xc, dpr)
            dwx_ref[...] = dwx_ref[...] + _dot_tn(xc, dpi)
            dz_ref[rows, MIXW:2 * MIXW] = (d * h * _dgelu(gate)).astype(bf16)
            return (dh[0:1], a[0:1], dba + jnp.sum(dpr, axis=0, keepdims=True), dbx + jnp.sum(dpi, axis=0, keepdims=True),
                    dlam + jnp.sum(dla * r, axis=0, keepdims=True) * (LRU_C * sgl))

        _, _, dba, dbx, dlam = lax.fori_loop(0, nchunk, chunk, (zero, zero, zero, zero, zero))
        cw = cw_ref[...]
        dxc = dxc_s[pl.ds(0, SEQ), :]
        dx = dxc * cw[3:4]
        dcw = [None] * 4
        dcw[3] = jnp.sum(dxc * x_ref[...], axis=0, keepdims=True)
        for k in range(1, 4):
            dx = dx + dxc_s[pl.ds(k, SEQ), :] * cw[3 - k:4 - k]
            dcw[3 - k] = jnp.sum(dxc * xp_s[pl.ds(8 - k, SEQ), :], axis=0, keepdims=True)
        dz_ref[:, 0:MIXW] = dx.astype(bf16)
        dcw_ref[...] = jnp.concatenate(dcw + [jnp.zeros((4, MIXW), f32)], axis=0)
        dvec_ref[...] = jnp.concatenate([jnp.sum(dxc, axis=0, keepdims=True), dba, dbx, dlam, jnp.zeros((4, MIXW), f32)], axis=0)

    col = lambda j: pl.BlockSpec((SEQ, MIXW), lambda i: (0, j))
    vec = pl.BlockSpec((1, MIXW), lambda i: (0, 0))
    vec8 = pl.BlockSpec((8, MIXW), lambda i: (0, 0))
    mat = pl.BlockSpec((MIXW, MIXW), lambda i: (0, 0))
    full = pl.BlockSpec((SEQ, MIXW), lambda i: (0, 0))
    return pl.pallas_call(
        body, name="lru_bwd", grid=(1,),
        in_specs=[col(9), col(10), vec8, vec, mat, vec, mat, vec, vec, full, full],
        out_specs=[pl.BlockSpec((SEQ, 2 * MIXW), lambda i: (0, 0)), mat, mat, vec8, vec8],
        out_shape=[_sds((SEQ, 2 * MIXW), bf16), _sds((MIXW, MIXW), f32), _sds((MIXW, MIXW), f32),
                   _sds((8, MIXW), f32), _sds((8, MIXW), f32)],
        scratch_shapes=[pltpu.VMEM((SEQ + 8, MIXW), f32), pltpu.VMEM((SEQ, MIXW), f32), pltpu.VMEM((SEQ + 8, MIXW), f32)],
        compiler_params=_params(("arbitrary",)),
    )(zm, zm, cw8, cb, wa, ba, wx, bx, lam, hd, do)


def _matmul_tn(a, b, tm, tn, b_col0=0):
    m = a.shape[1]
    n = tn if b_col0 else b.shape[1]
    off = b_col0 // tn

    def body(a_ref, b_ref, o_ref):
        o_ref[...] = _dot_tn(a_ref[...], b_ref[...]).astype(bf16)

    return pl.pallas_call(
        body, name="matmul_tn", grid=(m // tm, n // tn),
        in_specs=[pl.BlockSpec((SEQ, tm), lambda i, j: (0, i)), pl.BlockSpec((SEQ, tn), lambda i, j: (0, j + off))],
        out_specs=pl.BlockSpec((tm, tn), lambda i, j: (i, j)),
        out_shape=_sds((m, n), bf16),
        compiler_params=_params(("parallel", "arbitrary")),
    )(a, b)


def _matmul_nt_norm(pairs, x, g, dres):
    tm = 512
    steps = [a.shape[1] // t for a, _, t in pairs]
    starts = [sum(steps[:i]) for i in range(len(pairs))]
    total = sum(steps)
    npair = len(pairs)

    def body(*refs):
        a_refs, w_refs = refs[0:2 * npair:2], refs[1:2 * npair:2]
        x_ref, g_ref, dres_ref, dx_ref, dg_ref, acc_s = refs[2 * npair:]
        i, k = pl.program_id(0), pl.program_id(1)

        @pl.when(k == 0)
        def _():
            acc_s[...] = jnp.zeros_like(acc_s)

        @pl.when((i == 0) & (k == 0))
        def _():
            dg_ref[...] = jnp.zeros_like(dg_ref)

        for q in range(npair):
            @pl.when((k >= starts[q]) & (k < starts[q] + steps[q]))
            def _(q=q):
                acc_s[...] += _dot_nt(a_refs[q][...], w_refs[q][...])

        @pl.when(k == total - 1)
        def _():
            dx, dg = _rms_bwd(x_ref[...], g_ref[...], acc_s[...])
            dx_ref[...] = dres_ref[...] + dx
            dg_ref[...] += dg

    in_specs, args = [], []
    for q, (a, w, t) in enumerate(pairs):
        kmap = lambda k, q=q: jnp.clip(k - starts[q], 0, steps[q] - 1)
        in_specs += [pl.BlockSpec((tm, t), lambda i, k, kmap=kmap: (i, kmap(k))),
                     pl.BlockSpec((DM, t), lambda i, k, kmap=kmap: (0, kmap(k)))]
        args += [a, w]
    row = pl.BlockSpec((tm, DM), lambda i, k: (i, 0))
    vec = pl.BlockSpec((1, DM), lambda i, k: (0, 0))
    return pl.pallas_call(
        body, name="matmul_nt_norm", grid=(SEQ // tm, total),
        in_specs=in_specs + [row, vec, row], out_specs=[row, vec],
        out_shape=[_sds((SEQ, DM), f32), _sds((1, DM), f32)],
        scratch_shapes=[pltpu.VMEM((tm, DM), f32)],
        compiler_params=_params(("arbitrary", "arbitrary")),
    )(*args, x, g, dres)


def _merge_bwd(dx1, y, g2, outs, zg, wb, wo):
    def body(dx_ref, y_ref, g_ref, oa_ref, ob_ref, oc_ref, od_ref, zg_ref, wb_ref, wo_ref,
             da_ref, db_ref, dc_ref, dd_ref, dzg_ref, dpj_ref, dy_ref, dg_ref):
        @pl.when(pl.program_id(0) == 0)
        def _():
            dg_ref[...] = jnp.zeros_like(dg_ref)

        dy, dg = _rms_bwd(y_ref[...], g_ref[...], dx_ref[...])
        dg_ref[...] += dg
        dyb = dy.astype(bf16)
        dy_ref[...] = dyb
        dmerged = _dot_nt(dyb, wo_ref[...])
        for n, (o_ref, do_ref) in enumerate(((oa_ref, da_ref), (ob_ref, db_ref), (oc_ref, dc_ref), (od_ref, dd_ref))):
            cols = slice(n * DM, (n + 1) * DM)
            gate = _sigmoid(zg_ref[:, cols])
            proj = jnp.dot(o_ref[...], wb_ref[n], preferred_element_type=f32)
            dproj = (dmerged * gate).astype(bf16)
            dpj_ref[:, cols] = dproj
            dzg_ref[:, cols] = (dmerged * proj * gate * (1.0 - gate)).astype(bf16)
            do_ref[...] = _dot_nt(dproj, wb_ref[n])

    row = lambda w: pl.BlockSpec((ROW_TILE, w), lambda i: (i, 0))
    vec = pl.BlockSpec((1, DM), lambda i: (0, 0))
    return pl.pallas_call(
        body, name="merge_bwd", grid=(SEQ // ROW_TILE,),
        in_specs=[row(DM), row(DM), vec] + [row(MIXW)] * 4 + [row(NGATE), pl.BlockSpec((NHEAD, MIXW, DM), lambda i: (0, 0, 0)),
                                                              pl.BlockSpec((DM, DM), lambda i: (0, 0))],
        out_specs=[row(MIXW)] * 4 + [row(NGATE), row(NGATE), row(DM), vec],
        out_shape=[_sds((SEQ, MIXW), f32)] * 4 + [_sds((SEQ, NGATE), bf16), _sds((SEQ, NGATE), bf16), _sds((SEQ, DM), bf16),
                                                  _sds((1, DM), f32)],
        compiler_params=_params(("arbitrary",)),
    )(dx1, y, g2, *outs, zg, wb, wo)


def _ffn_bwd(dx2, f, g4, u, w2):
    def body(dx_ref, f_ref, g_ref, u_ref, w_ref, du_ref, a_ref, df_ref, dg_ref):
        @pl.when(pl.program_id(0) == 0)
        def _():
            dg_ref[...] = jnp.zeros_like(dg_ref)

        df, dg = _rms_bwd(f_ref[...], g_ref[...], dx_ref[...])
        dg_ref[...] += dg
        dfb = df.astype(bf16)
        df_ref[...] = dfb
        da = _dot_nt(dfb, w_ref[...])
        gt, up = u_ref[:, :FFH], u_ref[:, FFH:]
        a_ref[...] = (_silu(gt) * up).astype(bf16)
        du_ref[:, :FFH] = (da * up * _dsilu(gt)).astype(bf16)
        du_ref[:, FFH:] = (da * _silu(gt)).astype(bf16)

    row = lambda w: pl.BlockSpec((ROW_TILE, w), lambda i: (i, 0))
    vec = pl.BlockSpec((1, DM), lambda i: (0, 0))
    return pl.pallas_call(
        body, name="ffn_bwd", grid=(SEQ // ROW_TILE,),
        in_specs=[row(DM), row(DM), vec, row(2 * FFH), pl.BlockSpec((FFH, DM), lambda i: (0, 0))],
        out_specs=[row(2 * FFH), row(FFH), row(DM), vec],
        out_shape=[_sds((SEQ, 2 * FFH), bf16), _sds((SEQ, FFH), bf16), _sds((SEQ, DM), bf16), _sds((1, DM), f32)],
        compiler_params=_params(("arbitrary",)),
    )(dx2, f, g4, u, w2)


def _layer_bwd(dx2, p, sv):
    du, act, df, dg4 = _ffn_bwd(dx2, sv["f"], p["g4"], sv["u"], p["w2"])
    dw2 = _matmul_tn(act, df, 1408, DM)
    dx1, dg3 = _matmul_nt_norm([(du, p["w1"], 1408)], sv["x1"], p["g3"], dx2)
    dw1 = _matmul_tn(sv["h2"], du, DM, 1408)
    *dos, dzg, dproj, dy, dg2 = _merge_bwd(dx1, sv["y"], p["g2"], sv["outs"], sv["zg"], p["wb"], p["wo"])
    dwo = _matmul_tn(sv["merged"], dy, DM, DM)
    dwb = jnp.stack([_matmul_tn(sv["outs"][n], dproj, MIXW, DM, b_col0=n * DM) if n else
                     _matmul_tn(sv["outs"][0], dproj[:, :DM], MIXW, DM) for n in range(NHEAD)])
    zm = sv["zm"]
    zm3 = zm.reshape(HG_N, HG_T, NMIX)
    dza, drb = _attn_bwd(zm, p["rb8"], dos[0])
    dor, dgb, dhng = _hgrn_out_bwd(zm3, p["hng"], sv["obraw3"], dos[1].reshape(HG_N, HG_T, MIXW))
    dzb, dlb = _hgrn_bwd(zm3, p["lb"], dor)
    dzc, dws, dgng, dbs = _gmlp_bwd(zm, p["gng"], p["gws"], p["gbs8"], dos[2])
    dzd, dwa, dwx, dcw, dvec = _lru_bwd(zm, p["cw8"], p["cb"], p["wa"], p["ba"], p["wx"], p["bx"], p["lam"], sv["hd"], dos[3])
    dzm = jnp.concatenate([dza, dzb.reshape(SEQ, 3 * MIXW), dgb.reshape(SEQ, MIXW), dzc, dzd], axis=1)
    dx0, dg1 = _matmul_nt_norm([(dzm, p["wm"], 1408), (dzg, p["wgt"], 1024)], sv["x"], p["g1"], dx1)
    dwin = jnp.concatenate([_matmul_tn(sv["h"], dzm, DM, 1408), _matmul_tn(sv["h"], dzg, DM, 1024)], axis=1)
    big = dict(w_in=dwin, w_branch=dwb, w_out=dwo, w_ffn_in=dw1, w_ffn_out=dw2)
    small = dict(
        norm_mix_pre=dg1[0], norm_mix_post=dg2[0], norm_ffn_pre=dg3[0], norm_ffn_post=dg4[0],
        attn_rel_bias=drb[:NHEAD], lb=dlb[0], hgrn_norm_g=dhng[0], gmlp_norm_g=dgng[0], gmlp_ws=dws, gmlp_bs=dbs[:NHEAD],
        lru_conv_w=dcw[:NHEAD], lru_conv_b=dvec[0], lru_wa=_diag_blocks(dwa), lru_ba=dvec[1], lru_wx=_diag_blocks(dwx),
        lru_bx=dvec[2], lru_lambda=dvec[3],
    )
    return dx0, big, small


BIG = ("w_in", "w_branch", "w_out", "w_ffn_in", "w_ffn_out")
SMALL = ("norm_mix_pre", "norm_mix_post", "norm_ffn_pre", "norm_ffn_post", "attn_rel_bias", "hgrn_lb_logits", "hgrn_norm_g",
         "gmlp_norm_g", "gmlp_ws", "gmlp_bs", "lru_conv_w", "lru_conv_b", "lru_wa", "lru_ba", "lru_wx", "lru_bx", "lru_lambda")


def _local_step(x, tgt, full, small):
    lbs = _lb_fwd(small["hgrn_lb_logits"])
    params, saved = [], []
    for l in range(DEPTH):
        p = _layer_params(l, {k: full[k][l] for k in BIG}, small, lbs)
        x, sv = _layer_fwd(x, p)
        params.append(p)
        saved.append(sv)
    loss, dx = _loss_head(x, tgt)
    bigs, smalls = [None] * DEPTH, [None] * DEPTH
    for l in range(DEPTH - 1, -1, -1):
        dx, bigs[l], smalls[l] = _layer_bwd(dx, params[l], saved[l])
    gbig = {k: jnp.stack([bigs[l][k] for l in range(DEPTH)]) for k in BIG}
    gsmall = {k: jnp.stack([smalls[l][k] for l in range(DEPTH)]) for k in smalls[0]}
    gsmall["hgrn_lb_logits"] = _lb_bwd(small["hgrn_lb_logits"], gsmall.pop("lb"))
    return loss, dx, gbig, gsmall


HBM_ANY = pl.BlockSpec(memory_space=pl.ANY)


def _mesh_pos():
    return lax.axis_index("x"), lax.axis_index("y"), lax.axis_index("c")


def _all_gather(x, name):
    def body(x_ref, out_ref, send_sems, recv_sems, local_sem):
        ax, ay, ac = _mesh_pos()
        me, sibling = (ax, ay, ac), (ax, ay, 1 - ac)
        chips = [(1 - ax, ay), (ax, 1 - ay), (1 - ax, 1 - ay)]

        def slot(px, py, pc):
            return out_ref.at[4 * px + 2 * py + pc]

        def copy(k, block, to, src=None):
            return pltpu.make_async_remote_copy(
                src_ref=slot(*block) if src is None else src, dst_ref=slot(*block),
                send_sem=send_sems.at[k], recv_sem=recv_sems.at[k], device_id=to, device_id_type=MESH_ID)

        mine = pltpu.make_async_copy(x_ref, slot(*me), local_sem)
        mine.start()
        first = [copy(0, me, sibling, src=x_ref)]
        first += [copy(1 + j, me, (*chip, ac), src=x_ref) for j, chip in enumerate(chips)]
        for cp in first:
            cp.start()
        passed = [copy(4 + j, (*chip, ac), sibling) for j, chip in enumerate(chips)]
        for j, chip in enumerate(chips):
            copy(1 + j, (*chip, ac), me).wait_recv()
            passed[j].start()
        copy(0, sibling, me).wait_recv()
        for j, chip in enumerate(chips):
            copy(4 + j, (*chip, 1 - ac), me).wait_recv()
        for cp in first + passed:
            cp.wait_send()
        mine.wait()

    return pl.pallas_call(
        body, name=name, out_shape=_sds((NDEV,) + x.shape, x.dtype),
        in_specs=[HBM_ANY], out_specs=HBM_ANY,
        scratch_shapes=[pltpu.SemaphoreType.DMA((7,)), pltpu.SemaphoreType.DMA((7,)), pltpu.SemaphoreType.DMA],
    )(x)


def _exchange(g, name):
    def body(g_ref, out_ref, send_sems, recv_sems, local_sem):
        ax, ay, ac = _mesh_pos()
        me = 4 * ax + 2 * ay + ac
        mine = pltpu.make_async_copy(g_ref.at[me], out_ref.at[me], local_sem)
        mine.start()
        copies = []
        for k in range(1, NDEV):
            px = 1 - ax if k & 4 else ax
            py = 1 - ay if k & 2 else ay
            pc = 1 - ac if k & 1 else ac
            copies.append(pltpu.make_async_remote_copy(
                src_ref=g_ref.at[4 * px + 2 * py + pc], dst_ref=out_ref.at[me],
                send_sem=send_sems.at[k - 1], recv_sem=recv_sems.at[k - 1], device_id=(px, py, pc), device_id_type=MESH_ID))
        for cp in copies:
            cp.start()
        for cp in copies:
            cp.wait()
        mine.wait()

    return pl.pallas_call(
        body, name=name, out_shape=_sds(g.shape, g.dtype),
        in_specs=[HBM_ANY], out_specs=HBM_ANY,
        scratch_shapes=[pltpu.SemaphoreType.DMA((7,)), pltpu.SemaphoreType.DMA((7,)), pltpu.SemaphoreType.DMA],
    )(g)


def _peer(ax, ay, ac, k):
    return (1 - ax if k & 4 else ax, 1 - ay if k & 2 else ay, 1 - ac if k & 1 else ac)


def _handshake(peers):
    barrier = pltpu.get_barrier_semaphore()
    for peer in peers:
        pl.semaphore_signal(barrier, inc=1, device_id=peer, device_id_type=MESH_ID)
    pl.semaphore_wait(barrier, len(peers))


SEQUENCER = dict(axis_name="seq", num_cores=1)
GATHER_ID = 1
EXCHANGE_ID = 2


def _gather_sc(xs, name):
    n = len(xs)

    def body(*refs):
        srcs, outs = refs[:n], refs[n:2 * n]
        send_sems, recv_sems, local_sems = refs[2 * n:]
        ax, ay, ac = _mesh_pos()
        me, sibling = (ax, ay, ac), (ax, ay, 1 - ac)
        chips = [(1 - ax, ay), (ax, 1 - ay), (1 - ax, 1 - ay)]
        _handshake([sibling] + [(*chip, ac) for chip in chips])

        def slot(i, px, py, pc):
            return outs[i].at[4 * px + 2 * py + pc]

        def copy(i, k, block, to, src=None):
            return pltpu.make_async_remote_copy(
                src_ref=slot(i, *block) if src is None else src, dst_ref=slot(i, *block),
                send_sem=send_sems.at[7 * i + k], recv_sem=recv_sems.at[7 * i + k], device_id=to, device_id_type=MESH_ID)

        mine = [pltpu.make_async_copy(srcs[i], slot(i, *me), local_sems.at[i]) for i in range(n)]
        first = []
        for i in range(n):
            first += [copy(i, 1 + j, me, (*chip, ac), src=srcs[i]) for j, chip in enumerate(chips)]
        for i in range(n):
            first += [copy(i, 0, me, sibling, src=srcs[i])]
        for cp in first + mine:
            cp.start()
        passed = []
        for i in range(n):
            for j, chip in enumerate(chips):
                copy(i, 1 + j, (*chip, ac), me).wait_recv()
                passed.append(copy(i, 4 + j, (*chip, ac), sibling))
                passed[-1].start()
        for i in range(n):
            copy(i, 0, sibling, me).wait_recv()
            for j, chip in enumerate(chips):
                copy(i, 4 + j, (*chip, 1 - ac), me).wait_recv()
        for cp in first + passed:
            cp.wait_send()
        for cp in mine:
            cp.wait()

    return pl.kernel(
        body, name=name, out_type=[_sds((NDEV,) + x.shape, x.dtype) for x in xs],
        mesh=plsc.ScalarSubcoreMesh(**SEQUENCER),
        scratch_types=[pltpu.SemaphoreType.DMA((7 * n,)), pltpu.SemaphoreType.DMA((7 * n,)), pltpu.SemaphoreType.DMA((n,))],
        compiler_params=pltpu.CompilerParams(collective_id=GATHER_ID),
    )(*xs)


def _exchange_sc(gs, name):
    n = len(gs)

    def body(*refs):
        srcs, outs = refs[:n], refs[n:2 * n]
        send_sems, recv_sems, local_sems = refs[2 * n:]
        ax, ay, ac = _mesh_pos()
        me = 4 * ax + 2 * ay + ac
        peers = [_peer(ax, ay, ac, k) for k in range(1, NDEV)]
        _handshake(peers)
        mine = [pltpu.make_async_copy(srcs[i].at[me], outs[i].at[me], local_sems.at[i]) for i in range(n)]
        copies = []
        for i in range(n):
            for k, (px, py, pc) in enumerate(peers):
                copies.append(pltpu.make_async_remote_copy(
                    src_ref=srcs[i].at[4 * px + 2 * py + pc], dst_ref=outs[i].at[me],
                    send_sem=send_sems.at[7 * i + k], recv_sem=recv_sems.at[7 * i + k],
                    device_id=(px, py, pc), device_id_type=MESH_ID))
        for cp in copies + mine:
            cp.start()
        for cp in copies + mine:
            cp.wait()

    return pl.kernel(
        body, name=name, out_type=[_sds(g.shape, g.dtype) for g in gs],
        mesh=plsc.ScalarSubcoreMesh(**SEQUENCER),
        scratch_types=[pltpu.SemaphoreType.DMA((7 * n,)), pltpu.SemaphoreType.DMA((7 * n,)), pltpu.SemaphoreType.DMA((n,))],
        compiler_params=pltpu.CompilerParams(collective_id=EXCHANGE_ID),
    )(*gs)


def _row_tile(rows, cols):
    cap = max(8, (1 << 18) // cols)
    if rows <= cap:
        return rows
    best = None
    for t in range(8, cap + 1, 8):
        if rows % t == 0:
            best = t
    assert best is not None, (rows, cols)
    return best


def _sum_parts(parts, name):
    npart, rows, cols = parts.shape
    tr = _row_tile(rows, cols)

    def body(p_ref, o_ref):
        g = p_ref[0].astype(f32)
        for j in range(1, npart):
            g = g + p_ref[j].astype(f32)
        o_ref[...] = g

    return pl.pallas_call(
        body, name=name, grid=(rows // tr,),
        in_specs=[pl.BlockSpec((npart, tr, cols), lambda i: (0, i, 0))], out_specs=pl.BlockSpec((tr, cols), lambda i: (i, 0)),
        out_shape=_sds((rows, cols), f32), compiler_params=_params(("parallel",)),
    )(parts)


def _adamw(parts, w, m, v, name):
    npart, rows, cols = parts.shape
    tr = _row_tile(rows, cols)
    c1 = 1.0 / (1.0 - ADAM_B1 ** ADAM_STEP)
    c2 = 1.0 / (1.0 - ADAM_B2 ** ADAM_STEP)

    def body(p_ref, w_ref, m_ref, v_ref, g_ref, d_ref, mo_ref, vo_ref):
        g = p_ref[0].astype(f32)
        for j in range(1, npart):
            g = g + p_ref[j].astype(f32)
        mn = ADAM_B1 * m_ref[...] + (1.0 - ADAM_B1) * g
        vn = ADAM_B2 * v_ref[...] + (1.0 - ADAM_B2) * (g * g)
        g_ref[...] = g
        mo_ref[...] = mn
        vo_ref[...] = vn
        d_ref[...] = (-ADAM_LR) * ((mn * c1) / (jnp.sqrt(vn * c2) + ADAM_EPS) + ADAM_WD * w_ref[...])

    blk = pl.BlockSpec((tr, cols), lambda i: (i, 0))
    return pl.pallas_call(
        body, name=name, grid=(rows // tr,),
        in_specs=[pl.BlockSpec((npart, tr, cols), lambda i: (0, i, 0)), blk, blk, blk], out_specs=[blk] * 4,
        out_shape=[_sds((rows, cols), f32)] * 4, compiler_params=_params(("parallel",)),
    )(parts, w, m, v)


def _adamw_layer(parts, w, m, v, acc, l, name):
    npart, rows, cols = parts.shape
    tr = _row_tile(rows, cols)
    c1 = 1.0 / (1.0 - ADAM_B1 ** ADAM_STEP)
    c2 = 1.0 / (1.0 - ADAM_B2 ** ADAM_STEP)

    def body(p_ref, w_ref, m_ref, v_ref, *refs):
        g_ref, d_ref, mo_ref, vo_ref = refs[-4:]
        g = p_ref[0].astype(f32)
        for j in range(1, npart):
            g = g + p_ref[j].astype(f32)
        mn = ADAM_B1 * m_ref[...] + (1.0 - ADAM_B1) * g
        vn = ADAM_B2 * v_ref[...] + (1.0 - ADAM_B2) * (g * g)
        g_ref[...] = g
        mo_ref[...] = mn
        vo_ref[...] = vn
        d_ref[...] = (-ADAM_LR) * ((mn * c1) / (jnp.sqrt(vn * c2) + ADAM_EPS) + ADAM_WD * w_ref[...])

    blk = pl.BlockSpec((None, tr, cols), lambda i: (l, i, 0))
    prev = [] if acc is None else list(acc)
    return pl.pallas_call(
        body, name=name, grid=(rows // tr,),
        in_specs=[pl.BlockSpec((npart, tr, cols), lambda i: (0, i, 0)), blk, blk, blk] + [HBM_ANY] * len(prev),
        out_specs=[blk] * 4, out_shape=[_sds(w.shape, f32)] * 4,
        input_output_aliases={4 + j: j for j in range(len(prev))},
        compiler_params=_params(("parallel",)),
    )(parts, w, m, v, *prev)


def _pack(arrays):
    rows = []
    for a in arrays:
        flat = a.reshape(-1)
        pad = (-flat.shape[0]) % 1024
        rows.append(jnp.concatenate([flat, jnp.zeros((pad,), flat.dtype)]).reshape(-1, 128))
    return jnp.concatenate(rows, axis=0)


def _unpack(flat, shapes):
    out, r = [], 0
    for s in shapes:
        n = math.prod(s)
        nr = (n + 1023) // 1024 * 8
        out.append(flat[r:r + nr].reshape(-1)[:n].reshape(s))
        r += nr
    return out


BIG_SHARD_AXIS = dict(w_in=2, w_branch=3, w_out=1, w_ffn_in=2, w_ffn_out=1)
SHARDED_SMALL = ("attn_rel_bias", "lru_conv_w")


def _to_blocks(full, axis):
    s = full.shape
    cut = full.reshape(s[:axis] + (NDEV, s[axis] // NDEV) + s[axis + 1:])
    return jnp.moveaxis(cut, axis, 0)


def _from_blocks(blocks, axis):
    moved = jnp.moveaxis(blocks, 0, axis)
    s = moved.shape
    return moved.reshape(s[:axis] + (s[axis] * s[axis + 1],) + s[axis + 2:])


def _flat2(a):
    return a.reshape(-1, a.shape[-1])


def _my_slice(a, n):
    ax, ay, ac = _mesh_pos()
    return lax.dynamic_slice_in_dim(a, (4 * ax + 2 * ay + ac) * n, n, axis=a.ndim - 1)


_WEIGHTS = ("norm_mix_pre", "norm_mix_post", "norm_ffn_pre", "norm_ffn_post", "w_in", "attn_rel_bias", "hgrn_lb_logits",
            "hgrn_norm_g", "gmlp_norm_g", "gmlp_ws", "gmlp_bs", "lru_conv_w", "lru_conv_b", "lru_wa", "lru_ba", "lru_wx",
            "lru_bx", "lru_lambda", "w_branch", "w_out", "w_ffn_in", "w_ffn_out")


def _step(x, loss_target, w, m, v):
    full = []
    for l in range(DEPTH):
        got = _gather_sc([w[k][l].astype(bf16) for k in BIG], "gather_layer%d" % l)
        full.append({k: _from_blocks(g, BIG_SHARD_AXIS[k] - 1) for k, g in zip(BIG, got)})
    cut = jnp.concatenate([w[k] for k in SHARDED_SMALL], axis=-1)
    parts = _all_gather(_pack([cut]), "gather_small").reshape(NDEV, -1)[:, :math.prod(cut.shape)].reshape((NDEV,) + cut.shape)
    small = {k: w[k] for k in SMALL if k not in SHARDED_SMALL}
    at = 0
    for k in SHARDED_SMALL:
        n = w[k].shape[-1]
        small[k] = _from_blocks(parts[..., at:at + n], 2)
        at += n
    loss, dx, layers = _step_forward(x, loss_target, full, small)
    flat3 = lambda a: a.reshape((DEPTH, -1, a.shape[-1]))
    acc = {k: None for k in BIG}
    smalls = [None] * DEPTH
    for l in range(DEPTH - 1, -1, -1):
        dx, gbig, smalls[l] = _step_backward(dx, layers[l])
        got = _exchange_sc([_to_blocks(gbig[k], BIG_SHARD_AXIS[k] - 1) for k in BIG], "exchange_layer%d" % l)
        for k, g in zip(BIG, got):
            w3 = flat3(w[k])
            acc[k] = _adamw_layer(g.reshape((NDEV,) + w3.shape[1:]), w3, flat3(m[k]), flat3(v[k]), acc[k], l,
                                  "adamw_%s_%d" % (k, l))
    grads, deltas, new_m, new_v = {}, {}, {}, {}
    for k in BIG:
        grads[k], deltas[k], new_m[k], new_v[k] = (o.reshape(w[k].shape) for o in acc[k])
    gsmall = {k: jnp.stack([smalls[l][k] for l in range(DEPTH)]) for k in smalls[0]}
    gsmall["hgrn_lb_logits"] = _lb_bwd(small["hgrn_lb_logits"], gsmall.pop("lb"))
    shapes = [gsmall[k].shape for k in SMALL]
    sums = _unpack(_sum_parts(_all_gather(_pack([gsmall[k] for k in SMALL]), "gather_small_grads"), "sum_small_grads"), shapes)
    gs = dict(zip(SMALL, sums))
    for k in SHARDED_SMALL:
        gs[k] = _my_slice(gs[k], w[k].shape[-1])
    packed = [_pack([d[k] for k in SMALL]) for d in (gs, w, m, v)]
    outs = _adamw(packed[0][None], packed[1], packed[2], packed[3], "adamw_small")
    shapes = [w[k].shape for k in SMALL]
    for d, o in zip((grads, deltas, new_m, new_v), outs):
        d.update(zip(SMALL, _unpack(o, shapes)))
    total = lax.psum(loss[0, 0], ("x", "y", "c"))
    return total, dx[None], grads, deltas, new_m, new_v


def _step_forward(x, loss_target, full, small):
    lbs = _lb_fwd(small["hgrn_lb_logits"])
    x = x[0]
    layers = []
    for l in range(DEPTH):
        p = _layer_params(l, full[l], small, lbs)
        x, sv = _layer_fwd(x, p)
        layers.append((p, sv))
    loss, dx = _loss_head(x, loss_target[0])
    return loss, dx, layers


def _step_backward(dx, layer):
    return _layer_bwd(dx, *layer)


def kernel(x, norm_mix_pre, norm_mix_post, norm_ffn_pre, norm_ffn_post, w_in, attn_rel_bias, hgrn_lb_logits, hgrn_norm_g, gmlp_norm_g, gmlp_ws, gmlp_bs, lru_conv_w, lru_conv_b, lru_wa, lru_ba, lru_wx, lru_bx, lru_lambda, w_branch, w_out, w_ffn_in, w_ffn_out, loss_target, m_norm_mix_pre, m_norm_mix_post, m_norm_ffn_pre, m_norm_ffn_post, m_w_in, m_attn_rel_bias, m_hgrn_lb_logits, m_hgrn_norm_g, m_gmlp_norm_g, m_gmlp_ws, m_gmlp_bs, m_lru_conv_w, m_lru_conv_b, m_lru_wa, m_lru_ba, m_lru_wx, m_lru_bx, m_lru_lambda, m_w_branch, m_w_out, m_w_ffn_in, m_w_ffn_out, v_norm_mix_pre, v_norm_mix_post, v_norm_ffn_pre, v_norm_ffn_post, v_w_in, v_attn_rel_bias, v_hgrn_lb_logits, v_hgrn_norm_g, v_gmlp_norm_g, v_gmlp_ws, v_gmlp_bs, v_lru_conv_w, v_lru_conv_b, v_lru_wa, v_lru_ba, v_lru_wx, v_lru_bx, v_lru_lambda, v_w_branch, v_w_out, v_w_ffn_in, v_w_ffn_out):
    w = dict(zip(_WEIGHTS, (norm_mix_pre, norm_mix_post, norm_ffn_pre, norm_ffn_post, w_in, attn_rel_bias, hgrn_lb_logits, hgrn_norm_g, gmlp_norm_g, gmlp_ws, gmlp_bs, lru_conv_w, lru_conv_b, lru_wa, lru_ba, lru_wx, lru_bx, lru_lambda, w_branch, w_out, w_ffn_in, w_ffn_out)))
    m = dict(zip(_WEIGHTS, (m_norm_mix_pre, m_norm_mix_post, m_norm_ffn_pre, m_norm_ffn_post, m_w_in, m_attn_rel_bias, m_hgrn_lb_logits, m_hgrn_norm_g, m_gmlp_norm_g, m_gmlp_ws, m_gmlp_bs, m_lru_conv_w, m_lru_conv_b, m_lru_wa, m_lru_ba, m_lru_wx, m_lru_bx, m_lru_lambda, m_w_branch, m_w_out, m_w_ffn_in, m_w_ffn_out)))
    v = dict(zip(_WEIGHTS, (v_norm_mix_pre, v_norm_mix_post, v_norm_ffn_pre, v_norm_ffn_post, v_w_in, v_attn_rel_bias, v_hgrn_lb_logits, v_hgrn_norm_g, v_gmlp_norm_g, v_gmlp_ws, v_gmlp_bs, v_lru_conv_w, v_lru_conv_b, v_lru_wa, v_lru_ba, v_lru_wx, v_lru_bx, v_lru_lambda, v_w_branch, v_w_out, v_w_ffn_in, v_w_ffn_out)))
    loss, grad_x, grads, deltas, new_m, new_v = _step(x, loss_target, w, m, v)
    return (loss, grad_x, *[grads[k] for k in _WEIGHTS], *[deltas[k] for k in _WEIGHTS],
            *[new_m[k] for k in _WEIGHTS], *[new_v[k] for k in _WEIGHTS])
```

```python
import math

import jax
import jax.numpy as jnp
from jax import lax
from jax.experimental import pallas as pl
from jax.experimental.pallas import tpu as pltpu
from jax.experimental.pallas import tpu_sc as plsc

f32 = jnp.float32
bf16 = jnp.bfloat16

SEQ = 2048
DM = 1024
DEPTH = 4
NDEV = 8
MIXW = 256
NHEAD = 4
HDIM = 64
NMIX = 11 * MIXW
NGATE = 4 * DM
FFH = 2816
EPS = 1e-6
NEG_BIG = -1e30
LOG_FLOOR = 1e-30
LRU_C = 8.0
REL_SIZE = 320
ATT_PAIR = 128
ATT_BAND = 640
ATT_PAD = 512
ATT_WV = 768
HG_T = 16
HG_N = SEQ // HG_T
GM_T = 128
LRU_T = 128
ADAM_LR, ADAM_B1, ADAM_B2, ADAM_EPS, ADAM_WD, ADAM_STEP = 0.001, 0.9, 0.999, 1e-8, 0.01, 10
V7X_VMEM_LIMIT = 56 * 1024 * 1024
GELU_C0 = math.sqrt(2.0 / math.pi)
GELU_C1 = 0.044715
MESH_ID = pl.DeviceIdType.MESH


def _params(sem=None):
    if sem is None:
        return pltpu.CompilerParams(vmem_limit_bytes=V7X_VMEM_LIMIT)
    return pltpu.CompilerParams(dimension_semantics=sem, vmem_limit_bytes=V7X_VMEM_LIMIT)


def _sds(shape, dtype):
    return jax.ShapeDtypeStruct(shape, dtype)


def _dot(a, b):
    return jnp.dot(a.astype(bf16), b.astype(bf16), preferred_element_type=f32)


def _dot_nt(a, b):
    return lax.dot_general(a.astype(bf16), b.astype(bf16), (((1,), (1,)), ((), ())), preferred_element_type=f32)


def _dot_tn(a, b):
    return lax.dot_general(a.astype(bf16), b.astype(bf16), (((0,), (0,)), ((), ())), preferred_element_type=f32)


def _split(a):
    hi = a.astype(bf16)
    lo = (a - hi.astype(f32)).astype(bf16)
    return hi, lo


def _dot_hl(a, m):
    hi, lo = _split(a)
    return jnp.dot(hi, m, preferred_element_type=f32) + jnp.dot(lo, m, preferred_element_type=f32)


def _dot_nt_hl(m, a):
    hi, lo = _split(a)
    dn = (((1,), (1,)), ((), ()))
    return lax.dot_general(m, hi, dn, preferred_element_type=f32) + lax.dot_general(m, lo, dn, preferred_element_type=f32)


def _sigmoid(x):
    return jax.nn.sigmoid(x)


def _silu(x):
    return x * _sigmoid(x)


def _dsilu(x):
    s = _sigmoid(x)
    return s * (1.0 + x * (1.0 - s))


def _gelu(x):
    return 0.5 * x * (1.0 + jnp.tanh(GELU_C0 * (x + GELU_C1 * x * x * x)))


def _dgelu(x):
    t = jnp.tanh(GELU_C0 * (x + GELU_C1 * x * x * x))
    return 0.5 * (1.0 + t) + 0.5 * x * (1.0 - t * t) * GELU_C0 * (1.0 + 3.0 * GELU_C1 * x * x)


def _rms(x, g):
    r = lax.rsqrt(jnp.mean(x * x, axis=-1, keepdims=True) + EPS)
    return x * r * g


def _rms_bwd(x, g, dy):
    r = lax.rsqrt(jnp.mean(x * x, axis=-1, keepdims=True) + EPS)
    xh = x * r
    dxh = dy * g
    dx = r * (dxh - xh * jnp.mean(dxh * xh, axis=-1, keepdims=True))
    return dx, jnp.sum(dy * xh, axis=0, keepdims=True)


def _same_head(n, width, dtype):
    r = lax.broadcasted_iota(jnp.int32, (n, n), 0) // width
    c = lax.broadcasted_iota(jnp.int32, (n, n), 1) // width
    return (r == c).astype(dtype)


def _head_masks(rows=1):
    lane = lax.broadcasted_iota(jnp.int32, (rows, MIXW), 1) // HDIM
    return [lane == h for h in range(NHEAD)]


def _norm_matmul(x, g, w, tn):
    n = w.shape[1]
    tm = 1024

    def body(x_ref, g_ref, w_ref, z_ref, h_ref):
        @pl.when(pl.program_id(1) == 0)
        def _():
            h_ref[...] = _rms(x_ref[...], g_ref[...]).astype(bf16)

        z_ref[...] = jnp.dot(h_ref[...], w_ref[...], preferred_element_type=f32)

    return pl.pallas_call(
        body, name="norm_matmul", grid=(SEQ // tm, n // tn),
        in_specs=[pl.BlockSpec((tm, DM), lambda i, j: (i, 0)), pl.BlockSpec((1, DM), lambda i, j: (0, 0)),
                  pl.BlockSpec((DM, tn), lambda i, j: (0, j))],
        out_specs=[pl.BlockSpec((tm, tn), lambda i, j: (i, j)), pl.BlockSpec((tm, DM), lambda i, j: (i, 0))],
        out_shape=[_sds((SEQ, n), f32), _sds((SEQ, DM), bf16)],
        compiler_params=_params(("parallel", "arbitrary")),
    )(x, g, w)


def _matmul(a, w, tn):
    k, n = w.shape
    tm = 1024

    def body(a_ref, w_ref, z_ref):
        z_ref[...] = jnp.dot(a_ref[...], w_ref[...], preferred_element_type=f32)

    return pl.pallas_call(
        body, name="matmul", grid=(SEQ // tm, n // tn),
        in_specs=[pl.BlockSpec((tm, k), lambda i, j: (i, 0)), pl.BlockSpec((k, tn), lambda i, j: (0, j))],
        out_specs=pl.BlockSpec((tm, tn), lambda i, j: (i, j)),
        out_shape=_sds((SEQ, n), f32),
        compiler_params=_params(("parallel", "arbitrary")),
    )(a, w)


def _att_offset_map():
    i = lax.broadcasted_iota(jnp.int32, (REL_SIZE, ATT_WV), 0)
    t = lax.broadcasted_iota(jnp.int32, (REL_SIZE, ATT_WV), 1)
    e = jnp.where(t <= ATT_BAND, t, t - ATT_WV)
    idx = jnp.clip(ATT_PAD - e, -(HDIM - 1), 256) + (HDIM - 1)
    return (idx == i).astype(bf16)


def _att_band_valid():
    qc = lax.broadcasted_iota(jnp.int32, (ATT_PAIR, ATT_BAND), 0) // HDIM
    kc = lax.broadcasted_iota(jnp.int32, (ATT_PAIR, ATT_BAND), 1) // HDIM
    return (kc >= qc) & (kc <= qc + 8)


def _att_bias_tiles(rb_ref, bm_ref):
    wv = _dot_hl(rb_ref[...], _att_offset_map())
    valid = _att_band_valid()
    for h in range(NHEAD):
        rows = jnp.broadcast_to(wv[h:h + 1, :], (ATT_PAIR, ATT_WV))
        tile = pltpu.roll(rows, 0, 1, stride=1, stride_axis=0)[:, :ATT_BAND]
        bm_ref[h] = jnp.where(valid, tile, NEG_BIG)


def _att_pad_kv(k_ref, v_ref, kp_ref, vp_ref):
    kp_ref[pl.ds(0, ATT_PAD), :] = jnp.zeros((ATT_PAD, MIXW), bf16)
    vp_ref[pl.ds(0, ATT_PAD), :] = jnp.zeros((ATT_PAD, MIXW), bf16)
    kp_ref[pl.ds(ATT_PAD, SEQ), :] = k_ref[...].astype(bf16)
    vp_ref[pl.ds(ATT_PAD, SEQ), :] = v_ref[...].astype(bf16)


def _att_probs(qm, kb, bm, key_ok):
    s = _dot_nt(qm, kb) + bm
    s = jnp.where(key_ok, s, NEG_BIG)
    m = jnp.max(s, axis=-1, keepdims=True)
    e = jnp.exp(s - m)
    return e / jnp.sum(e, axis=-1, keepdims=True)


def _attn_fwd(zm, rb8):
    def body(q_ref, k_ref, v_ref, rb_ref, o_ref, kp_ref, vp_ref, bm_ref):
        _att_pad_kv(k_ref, v_ref, kp_ref, vp_ref)
        _att_bias_tiles(rb_ref, bm_ref)
        hm = _head_masks()

        def pair(p, carry):
            r0 = pl.multiple_of(p * ATT_PAIR, ATT_PAIR)
            q = q_ref[pl.ds(r0, ATT_PAIR), :] * (HDIM ** -0.5)
            kb = kp_ref[pl.ds(r0, ATT_BAND), :]
            vb = vp_ref[pl.ds(r0, ATT_BAND), :]
            key_ok = (lax.broadcasted_iota(jnp.int32, (1, ATT_BAND), 1) + (r0 - ATT_PAD)) >= 0
            o = jnp.zeros((ATT_PAIR, MIXW), f32)
            for h in range(NHEAD):
                qm = jnp.where(hm[h], q, 0.0)
                p_h = _att_probs(qm, kb, bm_ref[h], key_ok)
                o = o + jnp.where(hm[h], _dot(p_h, vb), 0.0)
            o_ref[pl.ds(r0, ATT_PAIR), :] = o.astype(bf16)
            return carry

        lax.fori_loop(0, SEQ // ATT_PAIR, pair, 0)

    col = lambda j: pl.BlockSpec((SEQ, MIXW), lambda i: (0, j))
    return pl.pallas_call(
        body, name="attn_fwd", grid=(1,),
        in_specs=[col(0), col(1), col(2), pl.BlockSpec((8, REL_SIZE), lambda i: (0, 0))],
        out_specs=pl.BlockSpec((SEQ, MIXW), lambda i: (0, 0)),
        out_shape=_sds((SEQ, MIXW), bf16),
        scratch_shapes=[pltpu.VMEM((SEQ + ATT_PAD, MIXW), bf16), pltpu.VMEM((SEQ + ATT_PAD, MIXW), bf16),
                        pltpu.VMEM((NHEAD, ATT_PAIR, ATT_BAND), f32)],
        compiler_params=_params(("arbitrary",)),
    )(zm, zm, zm, rb8)


def _hg_gates(q, fz, lb):
    sq = _sigmoid(q)
    sg = _sigmoid(fz)
    f = lb + (1.0 - lb) * sg
    return q * sq, (1.0 - lb) * (1.0 - sg), jnp.log(jnp.maximum(f, LOG_FLOOR)), sq, sg, f


def _hg_prepare(q_ref, f_ref, lb, qf_s, kf_s, b_s, qd_s, kd_s, dec_s):
    b = None
    for t in range(HG_T):
        qf, kf, lf, _, _, _ = _hg_gates(q_ref[:, t, :], f_ref[:, t, :], lb)
        b = lf if b is None else b + lf
        qf_s[:, t, :] = qf
        kf_s[:, t, :] = kf
        b_s[:, t, :] = b
    b_last = b
    dec_s[...] = jnp.broadcast_to(jnp.exp(b_last)[:, None, :], (HG_N, 8, MIXW))
    for t in range(HG_T):
        bt = b_s[:, t, :]
        qd_s[:, t, :] = qf_s[:, t, :] * jnp.exp(bt)
        kd_s[:, t, :] = kf_s[:, t, :] * jnp.exp(b_last - bt)


def _hg_scores(t, qf_s, kf_s, b_s, w_s, hm):
    qt = qf_s[:, t, :]
    bt = b_s[:, t, :]
    for s in range(t + 1):
        w = qt * kf_s[:, s, :]
        if s < t:
            w = w * jnp.exp(bt - b_s[:, s, :])
        w_s[pl.ds(s * HG_N, HG_N), :] = w.astype(bf16)
    return jnp.dot(w_s[pl.ds(0, (t + 1) * HG_N), :], hm, preferred_element_type=f32)


def _hgrn_fwd(zm3, lb, ng):
    def body(q_ref, f_ref, i_ref, g_ref, lb_ref, ng_ref, o_ref, oraw_ref,
             qf_s, kf_s, b_s, qd_s, kd_s, dec_s, w_s, st_s):
        lb = lb_ref[...]
        hm = _same_head(MIXW, HDIM, bf16)
        hmf = _same_head(MIXW, HDIM, f32)
        _hg_prepare(q_ref, f_ref, lb, qf_s, kf_s, b_s, qd_s, kd_s, dec_s)
        for t in range(HG_T):
            p = _hg_scores(t, qf_s, kf_s, b_s, w_s, hm)
            acc = jnp.zeros((HG_N, MIXW), f32)
            for s in range(t + 1):
                acc = acc + p[s * HG_N:(s + 1) * HG_N] * i_ref[:, s, :]
            oraw_ref[:, t, :] = acc
        st_s[...] = jnp.zeros((MIXW, MIXW), f32)

        def step(n, carry):
            st = st_s[...]
            oraw_ref[n] = oraw_ref[n] + _dot_nt(qd_s[n], st)
            st_s[...] = st * dec_s[n][0:1] + _dot_tn(i_ref[n], kd_s[n]) * hmf
            return carry

        lax.fori_loop(0, HG_N, step, 0)
        ngv = ng_ref[...]
        for t in range(HG_T):
            o = oraw_ref[:, t, :]
            ms = _dot_hl(o * o, hm) * (1.0 / HDIM)
            o_ref[:, t, :] = (o * lax.rsqrt(ms + EPS) * ngv * _silu(g_ref[:, t, :])).astype(bf16)

    col = lambda j: pl.BlockSpec((HG_N, HG_T, MIXW), lambda i: (0, 0, j))
    vec = pl.BlockSpec((1, MIXW), lambda i: (0, 0))
    blk = pl.BlockSpec((HG_N, HG_T, MIXW), lambda i: (0, 0, 0))
    s3 = pltpu.VMEM((HG_N, HG_T, MIXW), f32)
    return pl.pallas_call(
        body, name="hgrn_fwd", grid=(1,),
        in_specs=[col(3), col(4), col(5), col(6), vec, vec],
        out_specs=[blk, blk],
        out_shape=[_sds((HG_N, HG_T, MIXW), bf16), _sds((HG_N, HG_T, MIXW), f32)],
        scratch_shapes=[s3, s3, s3, s3, s3, pltpu.VMEM((HG_N, 8, MIXW), f32),
                        pltpu.VMEM((HG_T * HG_N, MIXW), bf16), pltpu.VMEM((MIXW, MIXW), f32)],
        compiler_params=_params(("arbitrary",)),
    )(zm3, zm3, zm3, zm3, lb, ng)


def _gm_weights(ws_ref):
    tril = lax.broadcasted_iota(jnp.int32, (GM_T, GM_T), 0) >= lax.broadcasted_iota(jnp.int32, (GM_T, GM_T), 1)
    return tril, [jnp.where(tril, ws_ref[g], 0.0).astype(bf16) for g in range(NHEAD)]


def _gm_expand():
    r = lax.broadcasted_iota(jnp.int32, (8, MIXW), 0)
    c = lax.broadcasted_iota(jnp.int32, (8, MIXW), 1) // HDIM
    return (r == c).astype(bf16)


def _gm_mixed(vn, wts, bias, hm):
    vb = vn.astype(bf16)
    mixed = bias
    for g in range(NHEAD):
        mixed = mixed + jnp.where(hm[g], jnp.dot(wts[g], vb, preferred_element_type=f32), 0.0)
    return mixed


def _gm_bias(bs_ref):
    hi, lo = _split(bs_ref[...])
    et = _gm_expand()
    dn = (((0,), (0,)), ((), ()))
    return lax.dot_general(hi, et, dn, preferred_element_type=f32) + lax.dot_general(lo, et, dn, preferred_element_type=f32)


def _gmlp_fwd(zm, ng, ws, bs8):
    def body(u_ref, v_ref, ng_ref, ws_ref, bs_ref, o_ref):
        hm = _head_masks()
        _, wts = _gm_weights(ws_ref)
        bias = _gm_bias(bs_ref)
        ngv = ng_ref[...]

        def blk(n, carry):
            rows = pl.ds(pl.multiple_of(n * GM_T, GM_T), GM_T)
            vn = _rms(_gelu(v_ref[rows, :]), ngv)
            o_ref[rows, :] = (_gelu(u_ref[rows, :]) * _gm_mixed(vn, wts, bias, hm)).astype(bf16)
            return carry

        lax.fori_loop(0, SEQ // GM_T, blk, 0)

    col = lambda j: pl.BlockSpec((SEQ, MIXW), lambda i: (0, j))
    return pl.pallas_call(
        body, name="gmlp_fwd", grid=(1,),
        in_specs=[col(7), col(8), pl.BlockSpec((1, MIXW), lambda i: (0, 0)),
                  pl.BlockSpec((NHEAD, GM_T, GM_T), lambda i: (0, 0, 0)), pl.BlockSpec((8, GM_T), lambda i: (0, 0))],
        out_specs=pl.BlockSpec((SEQ, MIXW), lambda i: (0, 0)),
        out_shape=_sds((SEQ, MIXW), bf16),
        compiler_params=_params(("arbitrary",)),
    )(zm, zm, ng, ws, bs8)


def _lru_conv(x_ref, cw_ref, cb_ref, xp_s, xc_s):
    xp_s[pl.ds(0, 8), :] = jnp.zeros((8, MIXW), f32)
    xp_s[pl.ds(8, SEQ), :] = x_ref[...]
    cw = cw_ref[...]
    xc = cb_ref[...] + x_ref[...] * cw[3:4]
    for k in range(1, 4):
        xc = xc + xp_s[pl.ds(8 - k, SEQ), :] * cw[3 - k:4 - k]
    xc_s[...] = xc


def _lru_gates(xc, wa, ba, wx, bx, sp, first_row):
    r = _sigmoid(_dot(xc, wa) + ba)
    ig = _sigmoid(_dot(xc, wx) + bx)
    la = (-LRU_C) * r * sp
    a = jnp.exp(la)
    th = jnp.tanh(la)
    m2 = -2.0 * th / (1.0 - th)
    mult = jnp.where(first_row, 1.0, jnp.sqrt(jnp.maximum(m2, 0.0)))
    return a, mult, r, ig, m2


def _lru_scan(a, b, rev):
    row = lax.broadcasted_iota(jnp.int32, (LRU_T, 1), 0)
    k = 1
    while k < LRU_T:
        ok = (row < LRU_T - k) if rev else (row >= k)
        sh = (LRU_T - k) if rev else k
        a_sh = jnp.where(ok, pltpu.roll(a, sh, 0), 1.0)
        b_sh = jnp.where(ok, pltpu.roll(b, sh, 0), 0.0)
        b = b + a * b_sh
        a = a * a_sh
        k *= 2
    return a, b


def _lru_fwd(zm, cw8, cb, wa, ba, wx, bx, lam):
    def body(x_ref, g_ref, cw_ref, cb_ref, wa_ref, ba_ref, wx_ref, bx_ref, lam_ref, o_ref, h_ref, xp_s, xc_s):
        _lru_conv(x_ref, cw_ref, cb_ref, xp_s, xc_s)
        sp = jax.nn.softplus(-lam_ref[...])
        wa_v, wx_v, ba_v, bx_v = wa_ref[...], wx_ref[...], ba_ref[...], bx_ref[...]

        def chunk(c, h_prev):
            rows = pl.ds(pl.multiple_of(c * LRU_T, LRU_T), LRU_T)
            first = (lax.broadcasted_iota(jnp.int32, (LRU_T, 1), 0) + c * LRU_T) == 0
            xc = xc_s[rows, :]
            a, mult, _, ig, _ = _lru_gates(xc, wa_v, ba_v, wx_v, bx_v, sp, first)
            acum, hloc = _lru_scan(a, mult * (ig * xc), False)
            h = hloc + acum * h_prev
            h_ref[rows, :] = h
            o_ref[rows, :] = (h * _gelu(g_ref[rows, :])).astype(bf16)
            return h[LRU_T - 1:LRU_T, :]

        lax.fori_loop(0, SEQ // LRU_T, chunk, jnp.zeros((1, MIXW), f32))

    col = lambda j: pl.BlockSpec((SEQ, MIXW), lambda i: (0, j))
    vec = pl.BlockSpec((1, MIXW), lambda i: (0, 0))
    mat = pl.BlockSpec((MIXW, MIXW), lambda i: (0, 0))
    out = pl.BlockSpec((SEQ, MIXW), lambda i: (0, 0))
    return pl.pallas_call(
        body, name="lru_fwd", grid=(1,),
        in_specs=[col(9), col(10), pl.BlockSpec((8, MIXW), lambda i: (0, 0)), vec, mat, vec, mat, vec, vec],
        out_specs=[out, out],
        out_shape=[_sds((SEQ, MIXW), bf16), _sds((SEQ, MIXW), f32)],
        scratch_shapes=[pltpu.VMEM((SEQ + 8, MIXW), f32), pltpu.VMEM((SEQ, MIXW), f32)],
        compiler_params=_params(("arbitrary",)),
    )(zm, zm, cw8, cb, wa, ba, wx, bx, lam)


def _block_diag(w):
    out = jnp.zeros((MIXW, MIXW), w.dtype)
    for h in range(NHEAD):
        out = lax.dynamic_update_slice(out, w[h], (h * HDIM, h * HDIM))
    return out


def _diag_blocks(w):
    return jnp.stack([w[h * HDIM:(h + 1) * HDIM, h * HDIM:(h + 1) * HDIM] for h in range(NHEAD)])


ROW_TILE = 256


def _merge_fwd(outs, zg, wb, wo, x, g2):
    def body(oa_ref, ob_ref, oc_ref, od_ref, zg_ref, wb_ref, wo_ref, x_ref, g_ref, xo_ref, mg_ref, y_ref):
        merged = jnp.zeros((ROW_TILE, DM), f32)
        for n, o_ref in enumerate((oa_ref, ob_ref, oc_ref, od_ref)):
            proj = jnp.dot(o_ref[...], wb_ref[n], preferred_element_type=f32)
            merged = merged + _sigmoid(zg_ref[:, n * DM:(n + 1) * DM]) * proj
        mb = merged.astype(bf16)
        y = jnp.dot(mb, wo_ref[...], preferred_element_type=f32)
        mg_ref[...] = mb
        y_ref[...] = y
        xo_ref[...] = x_ref[...] + _rms(y, g_ref[...])

    row = lambda w: pl.BlockSpec((ROW_TILE, w), lambda i: (i, 0))
    return pl.pallas_call(
        body, name="merge_fwd", grid=(SEQ // ROW_TILE,),
        in_specs=[row(MIXW)] * 4 + [row(NGATE), pl.BlockSpec((NHEAD, MIXW, DM), lambda i: (0, 0, 0)),
                                    pl.BlockSpec((DM, DM), lambda i: (0, 0)), row(DM), pl.BlockSpec((1, DM), lambda i: (0, 0))],
        out_specs=[row(DM), row(DM), row(DM)],
        out_shape=[_sds((SEQ, DM), f32), _sds((SEQ, DM), bf16), _sds((SEQ, DM), f32)],
        compiler_params=_params(("parallel",)),
    )(*outs, zg, wb, wo, x, g2)


def _ffn_out(u, w2, x, g4):
    def body(u_ref, w_ref, x_ref, g_ref, xo_ref, f_ref):
        a = _silu(u_ref[:, :FFH]) * u_ref[:, FFH:]
        f = jnp.dot(a.astype(bf16), w_ref[...], preferred_element_type=f32)
        f_ref[...] = f
        xo_ref[...] = x_ref[...] + _rms(f, g_ref[...])

    row = lambda w: pl.BlockSpec((ROW_TILE, w), lambda i: (i, 0))
    return pl.pallas_call(
        body, name="ffn_out", grid=(SEQ // ROW_TILE,),
        in_specs=[row(2 * FFH), pl.BlockSpec((FFH, DM), lambda i: (0, 0)), row(DM), pl.BlockSpec((1, DM), lambda i: (0, 0))],
        out_specs=[row(DM), row(DM)],
        out_shape=[_sds((SEQ, DM), f32), _sds((SEQ, DM), f32)],
        compiler_params=_params(("parallel",)),
    )(u, w2, x, g4)


def _loss_head(x, tgt):
    tm = 512

    def body(x_ref, t_ref, l_ref, dx_ref):
        @pl.when(pl.program_id(0) == 0)
        def _():
            l_ref[...] = jnp.zeros((1, 1), f32)

        d = x_ref[...] - t_ref[...]
        dx_ref[...] = d * (1.0 / DM)
        l_ref[...] += (0.5 / DM) * jnp.sum(d * d).reshape(1, 1)

    row = pl.BlockSpec((tm, DM), lambda i: (i, 0))
    return pl.pallas_call(
        body, name="loss_head", grid=(SEQ // tm,),
        in_specs=[row, row], out_specs=[pl.BlockSpec((1, 1), lambda i: (0, 0)), row],
        out_shape=[_sds((1, 1), f32), _sds((SEQ, DM), f32)],
        compiler_params=_params(("arbitrary",)),
    )(x, tgt)


def _lb_fwd(logits):
    def body(lg_ref, o_ref):
        lg = lg_ref[...]
        e = jnp.exp(lg - jnp.max(lg, axis=0, keepdims=True))
        p = e / jnp.sum(e, axis=0, keepdims=True)
        acc = jnp.zeros((1, MIXW), f32)
        o_ref[0:1, :] = acc
        for l in range(1, DEPTH):
            acc = acc + p[l:l + 1]
            o_ref[l:l + 1, :] = acc

    return pl.pallas_call(body, name="lb_fwd", out_shape=_sds((DEPTH, MIXW), f32))(logits)


def _lb_bwd(logits, dlbs):
    def body(lg_ref, d_ref, o_ref):
        lg = lg_ref[...]
        e = jnp.exp(lg - jnp.max(lg, axis=0, keepdims=True))
        p = e / jnp.sum(e, axis=0, keepdims=True)
        d = d_ref[...]
        dp = [jnp.zeros((1, MIXW), f32)] * DEPTH
        acc = jnp.zeros((1, MIXW), f32)
        for j in range(DEPTH - 1, 0, -1):
            acc = acc + d[j:j + 1]
            dp[j] = acc
        inner = sum(p[j:j + 1] * dp[j] for j in range(DEPTH))
        for j in range(DEPTH):
            o_ref[j:j + 1, :] = p[j:j + 1] * (dp[j] - inner)

    return pl.pallas_call(body, name="lb_bwd", out_shape=_sds((DEPTH, MIXW), f32))(logits, dlbs)


def _pad_rows(a, rows=8):
    return jnp.concatenate([a, jnp.zeros((rows - a.shape[0], a.shape[1]), a.dtype)], axis=0)


def _layer_params(l, full, small, lbs):
    row = lambda name: small[name][l][None]
    return dict(
        _mix_weights(full), **(_ffn_weights(full) if "w_ffn_in" in full else {}),
        g1=row("norm_mix_pre"), g2=row("norm_mix_post"), g3=row("norm_ffn_pre"), g4=row("norm_ffn_post"),
        rb8=_pad_rows(small["attn_rel_bias"][l]), lb=lbs[l][None], hng=row("hgrn_norm_g"),
        gng=row("gmlp_norm_g"), gws=small["gmlp_ws"][l], gbs8=_pad_rows(small["gmlp_bs"][l]),
        cw8=_pad_rows(small["lru_conv_w"][l]), cb=row("lru_conv_b"),
        wa=_block_diag(small["lru_wa"][l]).astype(bf16), ba=row("lru_ba"),
        wx=_block_diag(small["lru_wx"][l]).astype(bf16), bx=row("lru_bx"), lam=row("lru_lambda"),
    )


def _mix_weights(full):
    return dict(wm=full["w_in"][:, :NMIX], wgt=full["w_in"][:, NMIX:], wb=full["w_branch"], wo=full["w_out"])


def _ffn_weights(full):
    return dict(w1=full["w_ffn_in"], w2=full["w_ffn_out"])


def _layer_fwd(x, p, late_ffn_weights=None):
    zm, h = _norm_matmul(x, p["g1"], p["wm"], 1408)
    zg = _matmul(h, p["wgt"], 1024)
    oa = _attn_fwd(zm, p["rb8"])
    ob3, obraw3 = _hgrn_fwd(zm.reshape(HG_N, HG_T, NMIX), p["lb"], p["hng"])
    oc = _gmlp_fwd(zm, p["gng"], p["gws"], p["gbs8"])
    od, hd = _lru_fwd(zm, p["cw8"], p["cb"], p["wa"], p["ba"], p["wx"], p["bx"], p["lam"])
    outs = (oa, ob3.reshape(SEQ, MIXW), oc, od)
    x1, merged, y = _merge_fwd(outs, zg, p["wb"], p["wo"], x, p["g2"])
    if late_ffn_weights is not None:
        p.update(late_ffn_weights(x1))
    u, h2 = _norm_matmul(x1, p["g3"], p["w1"], 1408)
    x2, f = _ffn_out(u, p["w2"], x1, p["g4"])
    saved = dict(x=x, h=h, zm=zm, zg=zg, outs=outs, obraw3=obraw3, hd=hd, x1=x1, merged=merged, y=y, u=u, h2=h2, f=f)
    return x2, saved


def _att_bias_grad(db_ref, o_ref):
    r = lax.broadcasted_iota(jnp.int32, (ATT_PAIR, ATT_PAIR), 0)
    c = lax.broadcasted_iota(jnp.int32, (ATT_PAIR, ATT_PAIR), 1)
    flip = (r + c == ATT_PAIR - 1).astype(bf16)
    rows = []
    for h in range(NHEAD):
        d = jnp.concatenate([db_ref[h], jnp.zeros((ATT_PAIR, ATT_WV - ATT_BAND), f32)], axis=1)
        hi, lo = _split(d)
        rev = jnp.dot(flip, hi, preferred_element_type=f32) + jnp.dot(flip, lo, preferred_element_type=f32)
        lined = pltpu.roll(rev, ATT_WV - (ATT_PAIR - 1), 1, stride=1, stride_axis=0)
        rows.append(jnp.sum(lined, axis=0, keepdims=True))
    dwv = jnp.concatenate(rows + [jnp.zeros((8 - NHEAD, ATT_WV), f32)], axis=0)
    hi, lo = _split(dwv)
    m = _att_offset_map()
    dn = (((1,), (1,)), ((), ()))
    o_ref[...] = lax.dot_general(hi, m, dn, preferred_element_type=f32) + lax.dot_general(lo, m, dn, preferred_element_type=f32)


def _attn_bwd(zm, rb8, do):
    def body(q_ref, k_ref, v_ref, rb_ref, do_ref, dz_ref, drb_ref, kp_ref, vp_ref, bm_ref, dk_s, dv_s, db_s):
        _att_pad_kv(k_ref, v_ref, kp_ref, vp_ref)
        _att_bias_tiles(rb_ref, bm_ref)
        dk_s[...] = jnp.zeros_like(dk_s)
        dv_s[...] = jnp.zeros_like(dv_s)
        db_s[...] = jnp.zeros_like(db_s)
        hm = _head_masks()
        scale = HDIM ** -0.5

        def pair(p, carry):
            r0 = pl.multiple_of(p * ATT_PAIR, ATT_PAIR)
            q = q_ref[pl.ds(r0, ATT_PAIR), :] * scale
            dout = do_ref[pl.ds(r0, ATT_PAIR), :]
            kb = kp_ref[pl.ds(r0, ATT_BAND), :]
            vb = vp_ref[pl.ds(r0, ATT_BAND), :]
            key_ok = (lax.broadcasted_iota(jnp.int32, (1, ATT_BAND), 1) + (r0 - ATT_PAD)) >= 0
            dq = jnp.zeros((ATT_PAIR, MIXW), f32)
            dkb = jnp.zeros((ATT_BAND, MIXW), f32)
            dvb = jnp.zeros((ATT_BAND, MIXW), f32)
            for h in range(NHEAD):
                qm = jnp.where(hm[h], q, 0.0).astype(bf16)
                dom = jnp.where(hm[h], dout, 0.0).astype(bf16)
                p_h = _att_probs(qm, kb, bm_ref[h], key_ok)
                dp = _dot_nt(dom, vb)
                ds = p_h * (dp - jnp.sum(dp * p_h, axis=-1, keepdims=True))
                dsb = ds.astype(bf16)
                dq = dq + jnp.where(hm[h], _dot(dsb, kb), 0.0)
                dkb = dkb + _dot_tn(dsb, qm)
                dvb = dvb + _dot_tn(p_h, dom)
                db_s[h] = db_s[h] + ds
            dz_ref[pl.ds(r0, ATT_PAIR), 0:MIXW] = (dq * scale).astype(bf16)
            dk_s[pl.ds(r0, ATT_BAND), :] = dk_s[pl.ds(r0, ATT_BAND), :] + dkb
            dv_s[pl.ds(r0, ATT_BAND), :] = dv_s[pl.ds(r0, ATT_BAND), :] + dvb
            return carry

        lax.fori_loop(0, SEQ // ATT_PAIR, pair, 0)
        dz_ref[:, MIXW:2 * MIXW] = dk_s[pl.ds(ATT_PAD, SEQ), :].astype(bf16)
        dz_ref[:, 2 * MIXW:3 * MIXW] = dv_s[pl.ds(ATT_PAD, SEQ), :].astype(bf16)
        _att_bias_grad(db_s, drb_ref)

    col = lambda j: pl.BlockSpec((SEQ, MIXW), lambda i: (0, j))
    return pl.pallas_call(
        body, name="attn_bwd", grid=(1,),
        in_specs=[col(0), col(1), col(2), pl.BlockSpec((8, REL_SIZE), lambda i: (0, 0)), pl.BlockSpec((SEQ, MIXW), lambda i: (0, 0))],
        out_specs=[pl.BlockSpec((SEQ, 3 * MIXW), lambda i: (0, 0)), pl.BlockSpec((8, REL_SIZE), lambda i: (0, 0))],
        out_shape=[_sds((SEQ, 3 * MIXW), bf16), _sds((8, REL_SIZE), f32)],
        scratch_shapes=[pltpu.VMEM((SEQ + ATT_PAD, MIXW), bf16), pltpu.VMEM((SEQ + ATT_PAD, MIXW), bf16),
                        pltpu.VMEM((NHEAD, ATT_PAIR, ATT_BAND), f32),
                        pltpu.VMEM((SEQ + ATT_PAD, MIXW), f32), pltpu.VMEM((SEQ + ATT_PAD, MIXW), f32),
                        pltpu.VMEM((NHEAD, ATT_PAIR, ATT_BAND), f32)],
        compiler_params=_params(("arbitrary",)),
    )(zm, zm, zm, rb8, do)


def _hgrn_out_bwd(zm3, ng, oraw3, do3):
    def body(g_ref, ng_ref, o_ref, do_ref, dor_ref, dg_ref, dng_ref):
        hm = _same_head(MIXW, HDIM, bf16)
        ngv = ng_ref[...]
        dng = jnp.zeros((1, MIXW), f32)
        for t in range(HG_T):
            o, g, d = o_ref[:, t, :], g_ref[:, t, :], do_ref[:, t, :]
            rs = lax.rsqrt(_dot_hl(o * o, hm) * (1.0 / HDIM) + EPS)
            y1 = o * rs
            dy2 = d * _silu(g)
            dg_ref[:, t, :] = (d * y1 * ngv * _dsilu(g)).astype(bf16)
            dng = dng + jnp.sum(dy2 * y1, axis=0, keepdims=True)
            dy1 = dy2 * ngv
            dor_ref[:, t, :] = rs * (dy1 - y1 * (_dot_hl(dy1 * y1, hm) * (1.0 / HDIM)))
        dng_ref[...] = jnp.broadcast_to(dng, (8, MIXW))

    blk = pl.BlockSpec((HG_N, HG_T, MIXW), lambda i: (0, 0, 0))
    return pl.pallas_call(
        body, name="hgrn_out_bwd", grid=(1,),
        in_specs=[pl.BlockSpec((HG_N, HG_T, MIXW), lambda i: (0, 0, 6)), pl.BlockSpec((1, MIXW), lambda i: (0, 0)), blk, blk],
        out_specs=[blk, blk, pl.BlockSpec((8, MIXW), lambda i: (0, 0))],
        out_shape=[_sds((HG_N, HG_T, MIXW), f32), _sds((HG_N, HG_T, MIXW), bf16), _sds((8, MIXW), f32)],
        compiler_params=_params(("arbitrary",)),
    )(zm3, ng, oraw3, do3)


def _hgrn_bwd(zm3, lb, dor3):
    def body(q_ref, f_ref, i_ref, lb_ref, dor_ref, dz_ref, dlb_ref,
             qf_s, kf_s, b_s, dq_s, dk_s, db_s, dv_s, w_s, x_s, st_s, cur_s):
        lb = lb_ref[...]
        hm = _same_head(MIXW, HDIM, bf16)
        hmf = _same_head(MIXW, HDIM, f32)
        b = None
        for t in range(HG_T):
            qf, kf, lf, _, _, _ = _hg_gates(q_ref[:, t, :], f_ref[:, t, :], lb)
            b = lf if b is None else b + lf
            qf_s[:, t, :] = qf
            kf_s[:, t, :] = kf
            b_s[:, t, :] = b

        def block_terms(n):
            bn = b_s[n]
            bl = bn[HG_T - 1:HG_T]
            eb = jnp.exp(bn)
            ek = jnp.exp(bl - bn)
            return qf_s[n] * eb, kf_s[n] * ek, jnp.exp(bl), eb, ek

        cur_s[...] = jnp.zeros((MIXW, MIXW), f32)

        def fwd_step(n, carry):
            _, kd, dec, _, _ = block_terms(n)
            st = cur_s[...]
            st_s[n] = st.astype(bf16)
            cur_s[...] = st * dec + _dot_tn(i_ref[n], kd) * hmf
            return carry

        lax.fori_loop(0, HG_N, fwd_step, 0)
        cur_s[...] = jnp.zeros((MIXW, MIXW), f32)
        last = lax.broadcasted_iota(jnp.int32, (HG_T, 1), 0) == HG_T - 1

        def bwd_step(j, carry):
            n = HG_N - 1 - j
            qd, kd, dec, eb, ek = block_terms(n)
            v, do_n = i_ref[n], dor_ref[n]
            dst = cur_s[...]
            st = st_s[n]
            dqd = _dot(do_n, st)
            dkd = _dot(v, dst)
            ddec = jnp.sum(dst * st.astype(f32), axis=0, keepdims=True)
            cur_s[...] = dst * dec + _dot_tn(do_n, qd) * hmf
            dq_s[n] = dqd * eb
            dk_s[n] = dkd * ek
            dv_s[n] = _dot_nt(kd, dst)
            dbl = jnp.sum(dkd * kd, axis=0, keepdims=True) + ddec * dec
            db_s[n] = dqd * qd - dkd * kd + jnp.where(last, dbl, 0.0)
            return carry

        lax.fori_loop(0, HG_N, bwd_step, 0)
        for t in range(HG_T):
            qt, bt, dot_t = qf_s[:, t, :], b_s[:, t, :], dor_ref[:, t, :]
            for s in range(t + 1):
                w = qt * kf_s[:, s, :]
                if s < t:
                    w = w * jnp.exp(bt - b_s[:, s, :])
                w_s[pl.ds(s * HG_N, HG_N), :] = w.astype(bf16)
                x_s[pl.ds(s * HG_N, HG_N), :] = (dot_t * i_ref[:, s, :]).astype(bf16)
            p = jnp.dot(w_s[pl.ds(0, (t + 1) * HG_N), :], hm, preferred_element_type=f32)
            dp = jnp.dot(x_s[pl.ds(0, (t + 1) * HG_N), :], hm, preferred_element_type=f32)
            dq_t = jnp.zeros((HG_N, MIXW), f32)
            db_t = jnp.zeros((HG_N, MIXW), f32)
            for s in range(t + 1):
                ps = p[s * HG_N:(s + 1) * HG_N]
                dps = dp[s * HG_N:(s + 1) * HG_N]
                ks = kf_s[:, s, :]
                dv_s[:, s, :] = dv_s[:, s, :] + ps * dot_t
                if s < t:
                    dec_ts = jnp.exp(bt - b_s[:, s, :])
                    g1 = dps * ks * dec_ts
                    dk_s[:, s, :] = dk_s[:, s, :] + dps * qt * dec_ts
                    gw = g1 * qt
                    db_t = db_t + gw
                    db_s[:, s, :] = db_s[:, s, :] - gw
                else:
                    g1 = dps * ks
                    dk_s[:, s, :] = dk_s[:, s, :] + dps * qt
                dq_t = dq_t + g1
            dq_s[:, t, :] = dq_s[:, t, :] + dq_t
            db_s[:, t, :] = db_s[:, t, :] + db_t
        run = jnp.zeros((HG_N, MIXW), f32)
        dlb = jnp.zeros((1, MIXW), f32)
        oml = 1.0 - lb
        for t in range(HG_T - 1, -1, -1):
            run = run + db_s[:, t, :]
            q = q_ref[:, t, :]
            _, _, _, sq, sg, f = _hg_gates(q, f_ref[:, t, :], lb)
            dkf = dk_s[:, t, :]
            df = jnp.where(f > LOG_FLOOR, run / f, 0.0)
            dsg = (df - dkf) * oml
            dlb = dlb + jnp.sum((df - dkf) * (1.0 - sg), axis=0, keepdims=True)
            dz_ref[:, t, 0:MIXW] = (dq_s[:, t, :] * sq * (1.0 + q * (1.0 - sq))).astype(bf16)
            dz_ref[:, t, MIXW:2 * MIXW] = (dsg * sg * (1.0 - sg)).astype(bf16)
            dz_ref[:, t, 2 * MIXW:3 * MIXW] = dv_s[:, t, :].astype(bf16)
        dlb_ref[...] = jnp.broadcast_to(dlb, (8, MIXW))

    one = pl.Buffered(1)
    col = lambda j: pl.BlockSpec((HG_N, HG_T, MIXW), lambda i: (0, 0, j), pipeline_mode=one)
    s3 = pltpu.VMEM((HG_N, HG_T, MIXW), f32)
    return pl.pallas_call(
        body, name="hgrn_bwd", grid=(1,),
        in_specs=[col(3), col(4), col(5), pl.BlockSpec((1, MIXW), lambda i: (0, 0)),
                  pl.BlockSpec((HG_N, HG_T, MIXW), lambda i: (0, 0, 0), pipeline_mode=one)],
        out_specs=[pl.BlockSpec((HG_N, HG_T, 3 * MIXW), lambda i: (0, 0, 0)), pl.BlockSpec((8, MIXW), lambda i: (0, 0))],
        out_shape=[_sds((HG_N, HG_T, 3 * MIXW), bf16), _sds((8, MIXW), f32)],
        scratch_shapes=[s3, s3, s3, s3, s3, s3, s3,
                        pltpu.VMEM((HG_T * HG_N, MIXW), bf16), pltpu.VMEM((HG_T * HG_N, MIXW), bf16),
                        pltpu.VMEM((HG_N, MIXW, MIXW), bf16), pltpu.VMEM((MIXW, MIXW), f32)],
        compiler_params=_params(("arbitrary",)),
    )(zm3, zm3, zm3, lb, dor3)


def _gmlp_bwd(zm, ng, ws, bs8, do):
    def body(u_ref, v_ref, ng_ref, ws_ref, bs_ref, do_ref, dz_ref, dws_ref, dng_ref, dbs_ref, dm_s):
        hm = _head_masks()
        tril, wts = _gm_weights(ws_ref)
        bias = _gm_bias(bs_ref)
        ngv = ng_ref[...]
        dws_ref[...] = jnp.zeros_like(dws_ref)
        dm_s[...] = jnp.zeros_like(dm_s)

        def blk(n, dng):
            rows = pl.ds(pl.multiple_of(n * GM_T, GM_T), GM_T)
            cu, cv, d = u_ref[rows, :], v_ref[rows, :], do_ref[rows, :]
            v = _gelu(cv)
            r = lax.rsqrt(jnp.mean(v * v, axis=-1, keepdims=True) + EPS)
            vh = v * r
            vn = vh * ngv
            u = _gelu(cu)
            dm = d * u
            dmb, vnb = dm.astype(bf16), vn.astype(bf16)
            dvn = jnp.zeros((GM_T, MIXW), f32)
            for g in range(NHEAD):
                dws_ref[g] = dws_ref[g] + _dot_nt(jnp.where(hm[g], dm, 0.0), vnb)
                dvn = dvn + jnp.where(hm[g], _dot_tn(wts[g], dmb), 0.0)
            dm_s[...] = dm_s[...] + dm
            dvh = dvn * ngv
            dv = r * (dvh - vh * jnp.mean(dvh * vh, axis=-1, keepdims=True))
            dz_ref[rows, 0:MIXW] = (d * _gm_mixed(vn, wts, bias, hm) * _dgelu(cu)).astype(bf16)
            dz_ref[rows, MIXW:2 * MIXW] = (dv * _dgelu(cv)).astype(bf16)
            return dng + jnp.sum(dvn * vh, axis=0, keepdims=True)

        dng = lax.fori_loop(0, SEQ // GM_T, blk, jnp.zeros((1, MIXW), f32))
        dng_ref[...] = jnp.broadcast_to(dng, (8, MIXW))
        for g in range(NHEAD):
            dws_ref[g] = jnp.where(tril, dws_ref[g], 0.0)
        dbs_ref[...] = _dot_nt_hl(_gm_expand(), dm_s[...])

    col = lambda j: pl.BlockSpec((SEQ, MIXW), lambda i: (0, j))
    return pl.pallas_call(
        body, name="gmlp_bwd", grid=(1,),
        in_specs=[col(7), col(8), pl.BlockSpec((1, MIXW), lambda i: (0, 0)),
                  pl.BlockSpec((NHEAD, GM_T, GM_T), lambda i: (0, 0, 0)), pl.BlockSpec((8, GM_T), lambda i: (0, 0)),
                  pl.BlockSpec((SEQ, MIXW), lambda i: (0, 0))],
        out_specs=[pl.BlockSpec((SEQ, 2 * MIXW), lambda i: (0, 0)), pl.BlockSpec((NHEAD, GM_T, GM_T), lambda i: (0, 0, 0)),
                   pl.BlockSpec((8, MIXW), lambda i: (0, 0)), pl.BlockSpec((8, GM_T), lambda i: (0, 0))],
        out_shape=[_sds((SEQ, 2 * MIXW), bf16), _sds((NHEAD, GM_T, GM_T), f32), _sds((8, MIXW), f32), _sds((8, GM_T), f32)],
        scratch_shapes=[pltpu.VMEM((GM_T, MIXW), f32)],
        compiler_params=_params(("arbitrary",)),
    )(zm, zm, ng, ws, bs8, do)


def _lru_bwd(zm, cw8, cb, wa, ba, wx, bx, lam, hd, do):
    nchunk = SEQ // LRU_T

    def body(x_ref, g_ref, cw_ref, cb_ref, wa_ref, ba_ref, wx_ref, bx_ref, lam_ref, h_ref, do_ref,
             dz_ref, dwa_ref, dwx_ref, dcw_ref, dvec_ref, xp_s, xc_s, dxc_s):
        _lru_conv(x_ref, cw_ref, cb_ref, xp_s, xc_s)
        lam_v = lam_ref[...]
        sp = jax.nn.softplus(-lam_v)
        sgl = _sigmoid(-lam_v)
        wa_v, wx_v, ba_v, bx_v = wa_ref[...], wx_ref[...], ba_ref[...], bx_ref[...]
        dwa_ref[...] = jnp.zeros_like(dwa_ref)
        dwx_ref[...] = jnp.zeros_like(dwx_ref)
        dxc_s[pl.ds(SEQ, 8), :] = jnp.zeros((8, MIXW), f32)
        row = lax.broadcasted_iota(jnp.int32, (LRU_T, 1), 0)
        zero = jnp.zeros((1, MIXW), f32)

        def chunk(j, carry):
            dh_next, a_next, dba, dbx, dlam = carry
            c = nchunk - 1 - j
            rows = pl.ds(pl.multiple_of(c * LRU_T, LRU_T), LRU_T)
            prev = pl.ds(pl.multiple_of(jnp.maximum(c - 1, 0) * LRU_T, LRU_T), LRU_T)
            first = (row + c * LRU_T) == 0
            xc, gate, d, h = xc_s[rows, :], g_ref[rows, :], do_ref[rows, :], h_ref[rows, :]
            a, mult, r, ig, m2 = _lru_gates(xc, wa_v, ba_v, wx_v, bx_v, sp, first)
            h_last = jnp.where(c > 0, h_ref[prev, :][LRU_T - 1:LRU_T, :], 0.0)
            h_m1 = jnp.where(row == 0, h_last, pltpu.roll(h, 1, 0))
            a_up = jnp.where(row == LRU_T - 1, a_next, pltpu.roll(a, LRU_T - 1, 0))
            acum, dh_loc = _lru_scan(a_up, d * _gelu(gate), True)
            dh = dh_loc + acum * dh_next
            dmult = jnp.where(first, 0.0, dh * (ig * xc))
            msq = jnp.sqrt(jnp.maximum(m2, 0.0))
            dla = dh * h_m1 * a + jnp.where(m2 > 0.0, -dmult * (1.0 - m2) / msq, 0.0)
            dpr = dla * (-LRU_C) * sp * r * (1.0 - r)
            dpi = dh * mult * xc * ig * (1.0 - ig)
            dxc_s[rows, :] = dh * mult * ig + _dot_nt(dpr, wa_v) + _dot_nt(dpi, wx_v)
            dwa_ref[...] = dwa_ref[...] + _dot_tn(xc, dpr)
            dwx_ref[...] = dwx_ref[...] + _dot_tn(xc, dpi)
            dz_ref[rows, MIXW:2 * MIXW] = (d * h * _dgelu(gate)).astype(bf16)
            return (dh[0:1], a[0:1], dba + jnp.sum(dpr, axis=0, keepdims=True), dbx + jnp.sum(dpi, axis=0, keepdims=True),
                    dlam + jnp.sum(dla * r, axis=0, keepdims=True) * (LRU_C * sgl))

        _, _, dba, dbx, dlam = lax.fori_loop(0, nchunk, chunk, (zero, zero, zero, zero, zero))
        cw = cw_ref[...]
        dxc = dxc_s[pl.ds(0, SEQ), :]
        dx = dxc * cw[3:4]
        dcw = [None] * 4
        dcw[3] = jnp.sum(dxc * x_ref[...], axis=0, keepdims=True)
        for k in range(1, 4):
            dx = dx + dxc_s[pl.ds(k, SEQ), :] * cw[3 - k:4 - k]
            dcw[3 - k] = jnp.sum(dxc * xp_s[pl.ds(8 - k, SEQ), :], axis=0, keepdims=True)
        dz_ref[:, 0:MIXW] = dx.astype(bf16)
        dcw_ref[...] = jnp.concatenate(dcw + [jnp.zeros((4, MIXW), f32)], axis=0)
        dvec_ref[...] = jnp.concatenate([jnp.sum(dxc, axis=0, keepdims=True), dba, dbx, dlam, jnp.zeros((4, MIXW), f32)], axis=0)

    col = lambda j: pl.BlockSpec((SEQ, MIXW), lambda i: (0, j))
    vec = pl.BlockSpec((1, MIXW), lambda i: (0, 0))
    vec8 = pl.BlockSpec((8, MIXW), lambda i: (0, 0))
    mat = pl.BlockSpec((MIXW, MIXW), lambda i: (0, 0))
    full = pl.BlockSpec((SEQ, MIXW), lambda i: (0, 0))
    return pl.pallas_call(
        body, name="lru_bwd", grid=(1,),
        in_specs=[col(9), col(10), vec8, vec, mat, vec, mat, vec, vec, full, full],
        out_specs=[pl.BlockSpec((SEQ, 2 * MIXW), lambda i: (0, 0)), mat, mat, vec8, vec8],
        out_shape=[_sds((SEQ, 2 * MIXW), bf16), _sds((MIXW, MIXW), f32), _sds((MIXW, MIXW), f32),
                   _sds((8, MIXW), f32), _sds((8, MIXW), f32)],
        scratch_shapes=[pltpu.VMEM((SEQ + 8, MIXW), f32), pltpu.VMEM((SEQ, MIXW), f32), pltpu.VMEM((SEQ + 8, MIXW), f32)],
        compiler_params=_params(("arbitrary",)),
    )(zm, zm, cw8, cb, wa, ba, wx, bx, lam, hd, do)


def _matmul_tn(a, b, tm, tn, b_col0=0):
    m = a.shape[1]
    n = tn if b_col0 else b.shape[1]
    off = b_col0 // tn

    def body(a_ref, b_ref, o_ref):
        o_ref[...] = _dot_tn(a_ref[...], b_ref[...]).astype(bf16)

    return pl.pallas_call(
        body, name="matmul_tn", grid=(m // tm, n // tn),
        in_specs=[pl.BlockSpec((SEQ, tm), lambda i, j: (0, i)), pl.BlockSpec((SEQ, tn), lambda i, j: (0, j + off))],
        out_specs=pl.BlockSpec((tm, tn), lambda i, j: (i, j)),
        out_shape=_sds((m, n), bf16),
        compiler_params=_params(("parallel", "arbitrary")),
    )(a, b)


def _matmul_nt_norm(pairs, x, g, dres):
    tm = 512
    steps = [a.shape[1] // t for a, _, t in pairs]
    starts = [sum(steps[:i]) for i in range(len(pairs))]
    total = sum(steps)
    npair = len(pairs)

    def body(*refs):
        a_refs, w_refs = refs[0:2 * npair:2], refs[1:2 * npair:2]
        x_ref, g_ref, dres_ref, dx_ref, dg_ref, acc_s = refs[2 * npair:]
        i, k = pl.program_id(0), pl.program_id(1)

        @pl.when(k == 0)
        def _():
            acc_s[...] = jnp.zeros_like(acc_s)

        @pl.when((i == 0) & (k == 0))
        def _():
            dg_ref[...] = jnp.zeros_like(dg_ref)

        for q in range(npair):
            @pl.when((k >= starts[q]) & (k < starts[q] + steps[q]))
            def _(q=q):
                acc_s[...] += _dot_nt(a_refs[q][...], w_refs[q][...])

        @pl.when(k == total - 1)
        def _():
            dx, dg = _rms_bwd(x_ref[...], g_ref[...], acc_s[...])
            dx_ref[...] = dres_ref[...] + dx
            dg_ref[...] += dg

    in_specs, args = [], []
    for q, (a, w, t) in enumerate(pairs):
        kmap = lambda k, q=q: jnp.clip(k - starts[q], 0, steps[q] - 1)
        in_specs += [pl.BlockSpec((tm, t), lambda i, k, kmap=kmap: (i, kmap(k))),
                     pl.BlockSpec((DM, t), lambda i, k, kmap=kmap: (0, kmap(k)))]
        args += [a, w]
    row = pl.BlockSpec((tm, DM), lambda i, k: (i, 0))
    vec = pl.BlockSpec((1, DM), lambda i, k: (0, 0))
    return pl.pallas_call(
        body, name="matmul_nt_norm", grid=(SEQ // tm, total),
        in_specs=in_specs + [row, vec, row], out_specs=[row, vec],
        out_shape=[_sds((SEQ, DM), f32), _sds((1, DM), f32)],
        scratch_shapes=[pltpu.VMEM((tm, DM), f32)],
        compiler_params=_params(("arbitrary", "arbitrary")),
    )(*args, x, g, dres)


def _merge_bwd(dx1, y, g2, outs, zg, wb, wo):
    def body(dx_ref, y_ref, g_ref, oa_ref, ob_ref, oc_ref, od_ref, zg_ref, wb_ref, wo_ref,
             da_ref, db_ref, dc_ref, dd_ref, dzg_ref, dpj_ref, dy_ref, dg_ref):
        @pl.when(pl.program_id(0) == 0)
        def _():
            dg_ref[...] = jnp.zeros_like(dg_ref)

        dy, dg = _rms_bwd(y_ref[...], g_ref[...], dx_ref[...])
        dg_ref[...] += dg
        dyb = dy.astype(bf16)
        dy_ref[...] = dyb
        dmerged = _dot_nt(dyb, wo_ref[...])
        for n, (o_ref, do_ref) in enumerate(((oa_ref, da_ref), (ob_ref, db_ref), (oc_ref, dc_ref), (od_ref, dd_ref))):
            cols = slice(n * DM, (n + 1) * DM)
            gate = _sigmoid(zg_ref[:, cols])
            proj = jnp.dot(o_ref[...], wb_ref[n], preferred_element_type=f32)
            dproj = (dmerged * gate).astype(bf16)
            dpj_ref[:, cols] = dproj
            dzg_ref[:, cols] = (dmerged * proj * gate * (1.0 - gate)).astype(bf16)
            do_ref[...] = _dot_nt(dproj, wb_ref[n])

    row = lambda w: pl.BlockSpec((ROW_TILE, w), lambda i: (i, 0))
    vec = pl.BlockSpec((1, DM), lambda i: (0, 0))
    return pl.pallas_call(
        body, name="merge_bwd", grid=(SEQ // ROW_TILE,),
        in_specs=[row(DM), row(DM), vec] + [row(MIXW)] * 4 + [row(NGATE), pl.BlockSpec((NHEAD, MIXW, DM), lambda i: (0, 0, 0)),
                                                              pl.BlockSpec((DM, DM), lambda i: (0, 0))],
        out_specs=[row(MIXW)] * 4 + [row(NGATE), row(NGATE), row(DM), vec],
        out_shape=[_sds((SEQ, MIXW), f32)] * 4 + [_sds((SEQ, NGATE), bf16), _sds((SEQ, NGATE), bf16), _sds((SEQ, DM), bf16),
                                                  _sds((1, DM), f32)],
        compiler_params=_params(("arbitrary",)),
    )(dx1, y, g2, *outs, zg, wb, wo)


def _ffn_bwd(dx2, f, g4, u, w2):
    def body(dx_ref, f_ref, g_ref, u_ref, w_ref, du_ref, a_ref, df_ref, dg_ref):
        @pl.when(pl.program_id(0) == 0)
        def _():
            dg_ref[...] = jnp.zeros_like(dg_ref)

        df, dg = _rms_bwd(f_ref[...], g_ref[...], dx_ref[...])
        dg_ref[...] += dg
        dfb = df.astype(bf16)
        df_ref[...] = dfb
        da = _dot_nt(dfb, w_ref[...])
        gt, up = u_ref[:, :FFH], u_ref[:, FFH:]
        a_ref[...] = (_silu(gt) * up).astype(bf16)
        du_ref[:, :FFH] = (da * up * _dsilu(gt)).astype(bf16)
        du_ref[:, FFH:] = (da * _silu(gt)).astype(bf16)

    row = lambda w: pl.BlockSpec((ROW_TILE, w), lambda i: (i, 0))
    vec = pl.BlockSpec((1, DM), lambda i: (0, 0))
    return pl.pallas_call(
        body, name="ffn_bwd", grid=(SEQ // ROW_TILE,),
        in_specs=[row(DM), row(DM), vec, row(2 * FFH), pl.BlockSpec((FFH, DM), lambda i: (0, 0))],
        out_specs=[row(2 * FFH), row(FFH), row(DM), vec],
        out_shape=[_sds((SEQ, 2 * FFH), bf16), _sds((SEQ, FFH), bf16), _sds((SEQ, DM), bf16), _sds((1, DM), f32)],
        compiler_params=_params(("arbitrary",)),
    )(dx2, f, g4, u, w2)


def _layer_bwd(dx2, p, sv, ffn_grads_ready=None):
    du, act, df, dg4 = _ffn_bwd(dx2, sv["f"], p["g4"], sv["u"], p["w2"])
    dw2 = _matmul_tn(act, df, 1408, DM)
    dx1, dg3 = _matmul_nt_norm([(du, p["w1"], 1408)], sv["x1"], p["g3"], dx2)
    dw1 = _matmul_tn(sv["h2"], du, DM, 1408)
    if ffn_grads_ready is not None:
        dx1 = ffn_grads_ready(dict(w_ffn_in=dw1, w_ffn_out=dw2), dx1)
    *dos, dzg, dproj, dy, dg2 = _merge_bwd(dx1, sv["y"], p["g2"], sv["outs"], sv["zg"], p["wb"], p["wo"])
    dwo = _matmul_tn(sv["merged"], dy, DM, DM)
    dwb = jnp.stack([_matmul_tn(sv["outs"][n], dproj, MIXW, DM, b_col0=n * DM) if n else
                     _matmul_tn(sv["outs"][0], dproj[:, :DM], MIXW, DM) for n in range(NHEAD)])
    zm = sv["zm"]
    zm3 = zm.reshape(HG_N, HG_T, NMIX)
    dza, drb = _attn_bwd(zm, p["rb8"], dos[0])
    dor, dgb, dhng = _hgrn_out_bwd(zm3, p["hng"], sv["obraw3"], dos[1].reshape(HG_N, HG_T, MIXW))
    dzb, dlb = _hgrn_bwd(zm3, p["lb"], dor)
    dzc, dws, dgng, dbs = _gmlp_bwd(zm, p["gng"], p["gws"], p["gbs8"], dos[2])
    dzd, dwa, dwx, dcw, dvec = _lru_bwd(zm, p["cw8"], p["cb"], p["wa"], p["ba"], p["wx"], p["bx"], p["lam"], sv["hd"], dos[3])
    dzm = jnp.concatenate([dza, dzb.reshape(SEQ, 3 * MIXW), dgb.reshape(SEQ, MIXW), dzc, dzd], axis=1)
    dx0, dg1 = _matmul_nt_norm([(dzm, p["wm"], 1408), (dzg, p["wgt"], 1024)], sv["x"], p["g1"], dx1)
    dwin = jnp.concatenate([_matmul_tn(sv["h"], dzm, DM, 1408), _matmul_tn(sv["h"], dzg, DM, 1024)], axis=1)
    big = dict(w_in=dwin, w_branch=dwb, w_out=dwo, w_ffn_in=dw1, w_ffn_out=dw2)
    small = dict(
        norm_mix_pre=dg1[0], norm_mix_post=dg2[0], norm_ffn_pre=dg3[0], norm_ffn_post=dg4[0],
        attn_rel_bias=drb[:NHEAD], lb=dlb[0], hgrn_norm_g=dhng[0], gmlp_norm_g=dgng[0], gmlp_ws=dws, gmlp_bs=dbs[:NHEAD],
        lru_conv_w=dcw[:NHEAD], lru_conv_b=dvec[0], lru_wa=_diag_blocks(dwa), lru_ba=dvec[1], lru_wx=_diag_blocks(dwx),
        lru_bx=dvec[2], lru_lambda=dvec[3],
    )
    return dx0, big, small


MIX_BIG = ("w_in", "w_branch", "w_out")
FFN_BIG = ("w_ffn_in", "w_ffn_out")
BIG = MIX_BIG + FFN_BIG
SMALL = ("norm_mix_pre", "norm_mix_post", "norm_ffn_pre", "norm_ffn_post", "attn_rel_bias", "hgrn_lb_logits", "hgrn_norm_g",
         "gmlp_norm_g", "gmlp_ws", "gmlp_bs", "lru_conv_w", "lru_conv_b", "lru_wa", "lru_ba", "lru_wx", "lru_bx", "lru_lambda")


def _local_step(x, tgt, full, small):
    lbs = _lb_fwd(small["hgrn_lb_logits"])
    params, saved = [], []
    for l in range(DEPTH):
        p = _layer_params(l, {k: full[k][l] for k in BIG}, small, lbs)
        x, sv = _layer_fwd(x, p)
        params.append(p)
        saved.append(sv)
    loss, dx = _loss_head(x, tgt)
    bigs, smalls = [None] * DEPTH, [None] * DEPTH
    for l in range(DEPTH - 1, -1, -1):
        dx, bigs[l], smalls[l] = _layer_bwd(dx, params[l], saved[l])
    gbig = {k: jnp.stack([bigs[l][k] for l in range(DEPTH)]) for k in BIG}
    gsmall = {k: jnp.stack([smalls[l][k] for l in range(DEPTH)]) for k in smalls[0]}
    gsmall["hgrn_lb_logits"] = _lb_bwd(small["hgrn_lb_logits"], gsmall.pop("lb"))
    return loss, dx, gbig, gsmall


HBM_ANY = pl.BlockSpec(memory_space=pl.ANY)


def _mesh_pos():
    return lax.axis_index("x"), lax.axis_index("y"), lax.axis_index("c")


def _all_gather(x, name):
    def body(x_ref, out_ref, send_sems, recv_sems, local_sem):
        ax, ay, ac = _mesh_pos()
        me, sibling = (ax, ay, ac), (ax, ay, 1 - ac)
        chips = [(1 - ax, ay), (ax, 1 - ay), (1 - ax, 1 - ay)]

        def slot(px, py, pc):
            return out_ref.at[4 * px + 2 * py + pc]

        def copy(k, block, to, src=None):
            return pltpu.make_async_remote_copy(
                src_ref=slot(*block) if src is None else src, dst_ref=slot(*block),
                send_sem=send_sems.at[k], recv_sem=recv_sems.at[k], device_id=to, device_id_type=MESH_ID)

        mine = pltpu.make_async_copy(x_ref, slot(*me), local_sem)
        mine.start()
        first = [copy(0, me, sibling, src=x_ref)]
        first += [copy(1 + j, me, (*chip, ac), src=x_ref) for j, chip in enumerate(chips)]
        for cp in first:
            cp.start()
        passed = [copy(4 + j, (*chip, ac), sibling) for j, chip in enumerate(chips)]
        for j, chip in enumerate(chips):
            copy(1 + j, (*chip, ac), me).wait_recv()
            passed[j].start()
        copy(0, sibling, me).wait_recv()
        for j, chip in enumerate(chips):
            copy(4 + j, (*chip, 1 - ac), me).wait_recv()
        for cp in first + passed:
            cp.wait_send()
        mine.wait()

    return pl.pallas_call(
        body, name=name, out_shape=_sds((NDEV,) + x.shape, x.dtype),
        in_specs=[HBM_ANY], out_specs=HBM_ANY,
        scratch_shapes=[pltpu.SemaphoreType.DMA((7,)), pltpu.SemaphoreType.DMA((7,)), pltpu.SemaphoreType.DMA],
    )(x)


def _exchange(g, name):
    def body(g_ref, out_ref, send_sems, recv_sems, local_sem):
        ax, ay, ac = _mesh_pos()
        me = 4 * ax + 2 * ay + ac
        mine = pltpu.make_async_copy(g_ref.at[me], out_ref.at[me], local_sem)
        mine.start()
        copies = []
        for k in range(1, NDEV):
            px = 1 - ax if k & 4 else ax
            py = 1 - ay if k & 2 else ay
            pc = 1 - ac if k & 1 else ac
            copies.append(pltpu.make_async_remote_copy(
                src_ref=g_ref.at[4 * px + 2 * py + pc], dst_ref=out_ref.at[me],
                send_sem=send_sems.at[k - 1], recv_sem=recv_sems.at[k - 1], device_id=(px, py, pc), device_id_type=MESH_ID))
        for cp in copies:
            cp.start()
        for cp in copies:
            cp.wait()
        mine.wait()

    return pl.pallas_call(
        body, name=name, out_shape=_sds(g.shape, g.dtype),
        in_specs=[HBM_ANY], out_specs=HBM_ANY,
        scratch_shapes=[pltpu.SemaphoreType.DMA((7,)), pltpu.SemaphoreType.DMA((7,)), pltpu.SemaphoreType.DMA],
    )(g)


def _peer(ax, ay, ac, k):
    return (1 - ax if k & 4 else ax, 1 - ay if k & 2 else ay, 1 - ac if k & 1 else ac)


def _handshake(peers):
    barrier = pltpu.get_barrier_semaphore()
    for peer in peers:
        pl.semaphore_signal(barrier, inc=1, device_id=peer, device_id_type=MESH_ID)
    pl.semaphore_wait(barrier, len(peers))


SEQUENCER = dict(axis_name="seq", num_cores=1)
GATHER_ID = 1
EXCHANGE_ID = 2


def _gather_sc(xs, name):
    n = len(xs)

    def body(*refs):
        srcs, outs = refs[:n], refs[n:2 * n]
        send_sems, recv_sems, local_sems = refs[2 * n:]
        ax, ay, ac = _mesh_pos()
        me, sibling = (ax, ay, ac), (ax, ay, 1 - ac)
        chips = [(1 - ax, ay), (ax, 1 - ay), (1 - ax, 1 - ay)]
        _handshake([sibling] + [(*chip, ac) for chip in chips])

        def slot(i, px, py, pc):
            return outs[i].at[4 * px + 2 * py + pc]

        def copy(i, k, block, to, src=None):
            return pltpu.make_async_remote_copy(
                src_ref=slot(i, *block) if src is None else src, dst_ref=slot(i, *block),
                send_sem=send_sems.at[7 * i + k], recv_sem=recv_sems.at[7 * i + k], device_id=to, device_id_type=MESH_ID)

        mine = [pltpu.make_async_copy(srcs[i], slot(i, *me), local_sems.at[i]) for i in range(n)]
        first = []
        for i in range(n):
            first += [copy(i, 1 + j, me, (*chip, ac), src=srcs[i]) for j, chip in enumerate(chips)]
        for i in range(n):
            first += [copy(i, 0, me, sibling, src=srcs[i])]
        for cp in first + mine:
            cp.start()
        passed = []
        for i in range(n):
            for j, chip in enumerate(chips):
                copy(i, 1 + j, (*chip, ac), me).wait_recv()
                passed.append(copy(i, 4 + j, (*chip, ac), sibling))
                passed[-1].start()
        for i in range(n):
            copy(i, 0, sibling, me).wait_recv()
            for j, chip in enumerate(chips):
                copy(i, 4 + j, (*chip, 1 - ac), me).wait_recv()
        for cp in first + passed:
            cp.wait_send()
        for cp in mine:
            cp.wait()

    return pl.kernel(
        body, name=name, out_type=[_sds((NDEV,) + x.shape, x.dtype) for x in xs],
        mesh=plsc.ScalarSubcoreMesh(**SEQUENCER),
        scratch_types=[pltpu.SemaphoreType.DMA((7 * n,)), pltpu.SemaphoreType.DMA((7 * n,)), pltpu.SemaphoreType.DMA((n,))],
        compiler_params=pltpu.CompilerParams(collective_id=GATHER_ID),
    )(*xs)


def _exchange_sc(gs, name):
    n = len(gs)

    def body(*refs):
        srcs, outs = refs[:n], refs[n:2 * n]
        send_sems, recv_sems, local_sems = refs[2 * n:]
        ax, ay, ac = _mesh_pos()
        me = 4 * ax + 2 * ay + ac
        peers = [_peer(ax, ay, ac, k) for k in range(1, NDEV)]
        _handshake(peers)
        mine = [pltpu.make_async_copy(srcs[i].at[me], outs[i].at[me], local_sems.at[i]) for i in range(n)]
        copies = []
        for i in range(n):
            for k, (px, py, pc) in enumerate(peers):
                copies.append(pltpu.make_async_remote_copy(
                    src_ref=srcs[i].at[4 * px + 2 * py + pc], dst_ref=outs[i].at[me],
                    send_sem=send_sems.at[7 * i + k], recv_sem=recv_sems.at[7 * i + k],
                    device_id=(px, py, pc), device_id_type=MESH_ID))
        for cp in copies + mine:
            cp.start()
        for cp in copies + mine:
            cp.wait()

    return pl.kernel(
        body, name=name, out_type=[_sds(g.shape, g.dtype) for g in gs],
        mesh=plsc.ScalarSubcoreMesh(**SEQUENCER),
        scratch_types=[pltpu.SemaphoreType.DMA((7 * n,)), pltpu.SemaphoreType.DMA((7 * n,)), pltpu.SemaphoreType.DMA((n,))],
        compiler_params=pltpu.CompilerParams(collective_id=EXCHANGE_ID),
    )(*gs)


def _row_tile(rows, cols):
    cap = max(8, (1 << 18) // cols)
    if rows <= cap:
        return rows
    best = None
    for t in range(8, cap + 1, 8):
        if rows % t == 0:
            best = t
    assert best is not None, (rows, cols)
    return best


def _sum_parts(parts, name):
    npart, rows, cols = parts.shape
    tr = _row_tile(rows, cols)

    def body(p_ref, o_ref):
        g = p_ref[0].astype(f32)
        for j in range(1, npart):
            g = g + p_ref[j].astype(f32)
        o_ref[...] = g

    return pl.pallas_call(
        body, name=name, grid=(rows // tr,),
        in_specs=[pl.BlockSpec((npart, tr, cols), lambda i: (0, i, 0))], out_specs=pl.BlockSpec((tr, cols), lambda i: (i, 0)),
        out_shape=_sds((rows, cols), f32), compiler_params=_params(("parallel",)),
    )(parts)


def _adamw(parts, w, m, v, name):
    npart, rows, cols = parts.shape
    tr = _row_tile(rows, cols)
    c1 = 1.0 / (1.0 - ADAM_B1 ** ADAM_STEP)
    c2 = 1.0 / (1.0 - ADAM_B2 ** ADAM_STEP)

    def body(p_ref, w_ref, m_ref, v_ref, g_ref, d_ref, mo_ref, vo_ref):
        g = p_ref[0].astype(f32)
        for j in range(1, npart):
            g = g + p_ref[j].astype(f32)
        mn = ADAM_B1 * m_ref[...] + (1.0 - ADAM_B1) * g
        vn = ADAM_B2 * v_ref[...] + (1.0 - ADAM_B2) * (g * g)
        g_ref[...] = g
        mo_ref[...] = mn
        vo_ref[...] = vn
        d_ref[...] = (-ADAM_LR) * ((mn * c1) / (jnp.sqrt(vn * c2) + ADAM_EPS) + ADAM_WD * w_ref[...])

    blk = pl.BlockSpec((tr, cols), lambda i: (i, 0))
    return pl.pallas_call(
        body, name=name, grid=(rows // tr,),
        in_specs=[pl.BlockSpec((npart, tr, cols), lambda i: (0, i, 0)), blk, blk, blk], out_specs=[blk] * 4,
        out_shape=[_sds((rows, cols), f32)] * 4, compiler_params=_params(("parallel",)),
    )(parts, w, m, v)


def _adamw_layer(parts, w, m, v, acc, l, name):
    npart, rows, cols = parts.shape
    tr = _row_tile(rows, cols)
    c1 = 1.0 / (1.0 - ADAM_B1 ** ADAM_STEP)
    c2 = 1.0 / (1.0 - ADAM_B2 ** ADAM_STEP)

    def body(p_ref, w_ref, m_ref, v_ref, *refs):
        g_ref, d_ref, mo_ref, vo_ref = refs[-4:]
        g = p_ref[0].astype(f32)
        for j in range(1, npart):
            g = g + p_ref[j].astype(f32)
        mn = ADAM_B1 * m_ref[...] + (1.0 - ADAM_B1) * g
        vn = ADAM_B2 * v_ref[...] + (1.0 - ADAM_B2) * (g * g)
        g_ref[...] = g
        mo_ref[...] = mn
        vo_ref[...] = vn
        d_ref[...] = (-ADAM_LR) * ((mn * c1) / (jnp.sqrt(vn * c2) + ADAM_EPS) + ADAM_WD * w_ref[...])

    blk = pl.BlockSpec((None, tr, cols), lambda i: (l, i, 0))
    prev = [] if acc is None else list(acc)
    return pl.pallas_call(
        body, name=name, grid=(rows // tr,),
        in_specs=[pl.BlockSpec((npart, tr, cols), lambda i: (0, i, 0)), blk, blk, blk] + [HBM_ANY] * len(prev),
        out_specs=[blk] * 4, out_shape=[_sds(w.shape, f32)] * 4,
        input_output_aliases={4 + j: j for j in range(len(prev))},
        compiler_params=_params(("parallel",)),
    )(parts, w, m, v, *prev)


def _pack(arrays):
    rows = []
    for a in arrays:
        flat = a.reshape(-1)
        pad = (-flat.shape[0]) % 1024
        rows.append(jnp.concatenate([flat, jnp.zeros((pad,), flat.dtype)]).reshape(-1, 128))
    return jnp.concatenate(rows, axis=0)


def _unpack(flat, shapes):
    out, r = [], 0
    for s in shapes:
        n = math.prod(s)
        nr = (n + 1023) // 1024 * 8
        out.append(flat[r:r + nr].reshape(-1)[:n].reshape(s))
        r += nr
    return out


BIG_SHARD_AXIS = dict(w_in=2, w_branch=3, w_out=1, w_ffn_in=2, w_ffn_out=1)
SHARDED_SMALL = ("attn_rel_bias", "lru_conv_w")


def _to_blocks(full, axis):
    s = full.shape
    cut = full.reshape(s[:axis] + (NDEV, s[axis] // NDEV) + s[axis + 1:])
    return jnp.moveaxis(cut, axis, 0)


def _from_blocks(blocks, axis):
    moved = jnp.moveaxis(blocks, 0, axis)
    s = moved.shape
    return moved.reshape(s[:axis] + (s[axis] * s[axis + 1],) + s[axis + 2:])


def _flat2(a):
    return a.reshape(-1, a.shape[-1])


def _my_slice(a, n):
    ax, ay, ac = _mesh_pos()
    return lax.dynamic_slice_in_dim(a, (4 * ax + 2 * ay + ac) * n, n, axis=a.ndim - 1)


_WEIGHTS = ("norm_mix_pre", "norm_mix_post", "norm_ffn_pre", "norm_ffn_post", "w_in", "attn_rel_bias", "hgrn_lb_logits",
            "hgrn_norm_g", "gmlp_norm_g", "gmlp_ws", "gmlp_bs", "lru_conv_w", "lru_conv_b", "lru_wa", "lru_ba", "lru_wx",
            "lru_bx", "lru_lambda", "w_branch", "w_out", "w_ffn_in", "w_ffn_out")


def _step(x, loss_target, w, m, v):
    gathered = []
    for l in range(DEPTH):
        gathered.append(tuple(_gather_sc([w[k][l].astype(bf16) for k in keys], "gather_%s%d" % (half, l))
                              for half, keys in (("mix", MIX_BIG), ("ffn", FFN_BIG))))
    cut = jnp.concatenate([w[k] for k in SHARDED_SMALL], axis=-1)
    parts = _all_gather(_pack([cut]), "gather_small").reshape(NDEV, -1)[:, :math.prod(cut.shape)].reshape((NDEV,) + cut.shape)
    small = {k: w[k] for k in SMALL if k not in SHARDED_SMALL}
    at = 0
    for k in SHARDED_SMALL:
        n = w[k].shape[-1]
        small[k] = _from_blocks(parts[..., at:at + n], 2)
        at += n
    loss, dx, layers = _step_forward(x, loss_target, gathered, small)
    flat3 = lambda a: a.reshape((DEPTH, -1, a.shape[-1]))
    acc = {k: None for k in BIG}
    smalls = [None] * DEPTH

    def send(grads, keys, dx, name):
        blocks = [_to_blocks(grads[k], BIG_SHARD_AXIS[k] - 1) for k in keys]
        blocks, dx = lax.optimization_barrier((blocks, dx))
        return dict(zip(keys, _exchange_sc(blocks, name))), dx

    for l in range(DEPTH - 1, -1, -1):
        got = {}

        def ffn_grads_ready(grads, dx1, l=l, got=got):
            part, dx1 = send(grads, FFN_BIG, dx1, "exchange_ffn%d" % l)
            got.update(part)
            return dx1

        dx, gbig, smalls[l] = _step_backward(dx, layers[l], ffn_grads_ready)
        part, dx = send(gbig, MIX_BIG, dx, "exchange_mix%d" % l)
        got.update(part)
        for k in BIG:
            w3 = flat3(w[k])
            acc[k] = _adamw_layer(got[k].reshape((NDEV,) + w3.shape[1:]), w3, flat3(m[k]), flat3(v[k]), acc[k], l,
                                  "adamw_%s_%d" % (k, l))
    grads, deltas, new_m, new_v = {}, {}, {}, {}
    for k in BIG:
        grads[k], deltas[k], new_m[k], new_v[k] = (o.reshape(w[k].shape) for o in acc[k])
    gsmall = {k: jnp.stack([smalls[l][k] for l in range(DEPTH)]) for k in smalls[0]}
    gsmall["hgrn_lb_logits"] = _lb_bwd(small["hgrn_lb_logits"], gsmall.pop("lb"))
    shapes = [gsmall[k].shape for k in SMALL]
    sums = _unpack(_sum_parts(_all_gather(_pack([gsmall[k] for k in SMALL]), "gather_small_grads"), "sum_small_grads"), shapes)
    gs = dict(zip(SMALL, sums))
    for k in SHARDED_SMALL:
        gs[k] = _my_slice(gs[k], w[k].shape[-1])
    packed = [_pack([d[k] for k in SMALL]) for d in (gs, w, m, v)]
    outs = _adamw(packed[0][None], packed[1], packed[2], packed[3], "adamw_small")
    shapes = [w[k].shape for k in SMALL]
    for d, o in zip((grads, deltas, new_m, new_v), outs):
        d.update(zip(SMALL, _unpack(o, shapes)))
    total = lax.psum(loss[0, 0], ("x", "y", "c"))
    return total, dx[None], grads, deltas, new_m, new_v


def _step_forward(x, loss_target, gathered, small):
    lbs = _lb_fwd(small["hgrn_lb_logits"])
    x = x[0]
    layers = []

    def weights(blocks, keys, after):
        if after is not None:
            blocks, _ = lax.optimization_barrier((blocks, after))
        return {k: _from_blocks(g, BIG_SHARD_AXIS[k] - 1) for k, g in zip(keys, blocks)}

    for l in range(DEPTH):
        mix, ffn = gathered[l]
        p = _layer_params(l, weights(mix, MIX_BIG, x if l else None), small, lbs)
        x, sv = _layer_fwd(x, p, lambda x1, ffn=ffn: _ffn_weights(weights(ffn, FFN_BIG, x1)))
        layers.append((p, sv))
    loss, dx = _loss_head(x, loss_target[0])
    return loss, dx, layers


def _step_backward(dx, layer, ffn_grads_ready):
    return _layer_bwd(dx, *layer, ffn_grads_ready)


def kernel(x, norm_mix_pre, norm_mix_post, norm_ffn_pre, norm_ffn_post, w_in, attn_rel_bias, hgrn_lb_logits, hgrn_norm_g, gmlp_norm_g, gmlp_ws, gmlp_bs, lru_conv_w, lru_conv_b, lru_wa, lru_ba, lru_wx, lru_bx, lru_lambda, w_branch, w_out, w_ffn_in, w_ffn_out, loss_target, m_norm_mix_pre, m_norm_mix_post, m_norm_ffn_pre, m_norm_ffn_post, m_w_in, m_attn_rel_bias, m_hgrn_lb_logits, m_hgrn_norm_g, m_gmlp_norm_g, m_gmlp_ws, m_gmlp_bs, m_lru_conv_w, m_lru_conv_b, m_lru_wa, m_lru_ba, m_lru_wx, m_lru_bx, m_lru_lambda, m_w_branch, m_w_out, m_w_ffn_in, m_w_ffn_out, v_norm_mix_pre, v_norm_mix_post, v_norm_ffn_pre, v_norm_ffn_post, v_w_in, v_attn_rel_bias, v_hgrn_lb_logits, v_hgrn_norm_g, v_gmlp_norm_g, v_gmlp_ws, v_gmlp_bs, v_lru_conv_w, v_lru_conv_b, v_lru_wa, v_lru_ba, v_lru_wx, v_lru_bx, v_lru_lambda, v_w_branch, v_w_out, v_w_ffn_in, v_w_ffn_out):
    w = dict(zip(_WEIGHTS, (norm_mix_pre, norm_mix_post, norm_ffn_pre, norm_ffn_post, w_in, attn_rel_bias, hgrn_lb_logits, hgrn_norm_g, gmlp_norm_g, gmlp_ws, gmlp_bs, lru_conv_w, lru_conv_b, lru_wa, lru_ba, lru_wx, lru_bx, lru_lambda, w_branch, w_out, w_ffn_in, w_ffn_out)))
    m = dict(zip(_WEIGHTS, (m_norm_mix_pre, m_norm_mix_post, m_norm_ffn_pre, m_norm_ffn_post, m_w_in, m_attn_rel_bias, m_hgrn_lb_logits, m_hgrn_norm_g, m_gmlp_norm_g, m_gmlp_ws, m_gmlp_bs, m_lru_conv_w, m_lru_conv_b, m_lru_wa, m_lru_ba, m_lru_wx, m_lru_bx, m_lru_lambda, m_w_branch, m_w_out, m_w_ffn_in, m_w_ffn_out)))
    v = dict(zip(_WEIGHTS, (v_norm_mix_pre, v_norm_mix_post, v_norm_ffn_pre, v_norm_ffn_post, v_w_in, v_attn_rel_bias, v_hgrn_lb_logits, v_hgrn_norm_g, v_gmlp_norm_g, v_gmlp_ws, v_gmlp_bs, v_lru_conv_w, v_lru_conv_b, v_lru_wa, v_lru_ba, v_lru_wx, v_lru_bx, v_lru_lambda, v_w_branch, v_w_out, v_w_ffn_in, v_w_ffn_out)))
    loss, grad_x, grads, deltas, new_m, new_v = _step(x, loss_target, w, m, v)
    return (loss, grad_x, *[grads[k] for k in _WEIGHTS], *[deltas[k] for k in _WEIGHTS],
            *[new_m[k] for k in _WEIGHTS], *[new_v[k] for k in _WEIGHTS])
```

```python
import math

import jax
import jax.numpy as jnp
from jax import lax
from jax.experimental import pallas as pl
from jax.experimental.pallas import tpu as pltpu
from jax.experimental.pallas import tpu_sc as plsc

f32 = jnp.float32
bf16 = jnp.bfloat16

SEQ = 2048
DM = 1024
DEPTH = 4
NDEV = 8
MIXW = 256
NHEAD = 4
HDIM = 64
NMIX = 11 * MIXW
NGATE = 4 * DM
FFH = 2816
EPS = 1e-6
NEG_BIG = -1e30
LOG_FLOOR = 1e-30
LRU_C = 8.0
REL_SIZE = 320
ATT_PAIR = 128
ATT_BAND = 640
ATT_PAD = 512
ATT_WV = 768
HG_T = 16
HG_N = SEQ // HG_T
GM_T = 128
LRU_T = 128
ADAM_LR, ADAM_B1, ADAM_B2, ADAM_EPS, ADAM_WD, ADAM_STEP = 0.001, 0.9, 0.999, 1e-8, 0.01, 10
V7X_VMEM_LIMIT = 56 * 1024 * 1024
GELU_C0 = math.sqrt(2.0 / math.pi)
GELU_C1 = 0.044715
MESH_ID = pl.DeviceIdType.MESH


def _params(sem=None):
    if sem is None:
        return pltpu.CompilerParams(vmem_limit_bytes=V7X_VMEM_LIMIT)
    return pltpu.CompilerParams(dimension_semantics=sem, vmem_limit_bytes=V7X_VMEM_LIMIT)


def _sds(shape, dtype):
    return jax.ShapeDtypeStruct(shape, dtype)


def _dot(a, b):
    return jnp.dot(a.astype(bf16), b.astype(bf16), preferred_element_type=f32)


def _dot_nt(a, b):
    return lax.dot_general(a.astype(bf16), b.astype(bf16), (((1,), (1,)), ((), ())), preferred_element_type=f32)


def _dot_tn(a, b):
    return lax.dot_general(a.astype(bf16), b.astype(bf16), (((0,), (0,)), ((), ())), preferred_element_type=f32)


def _split(a):
    hi = a.astype(bf16)
    lo = (a - hi.astype(f32)).astype(bf16)
    return hi, lo


def _dot_hl(a, m):
    hi, lo = _split(a)
    return jnp.dot(hi, m, preferred_element_type=f32) + jnp.dot(lo, m, preferred_element_type=f32)


def _dot_nt_hl(m, a):
    hi, lo = _split(a)
    dn = (((1,), (1,)), ((), ()))
    return lax.dot_general(m, hi, dn, preferred_element_type=f32) + lax.dot_general(m, lo, dn, preferred_element_type=f32)


def _sigmoid(x):
    return jax.nn.sigmoid(x)


def _silu(x):
    return x * _sigmoid(x)


def _dsilu(x):
    s = _sigmoid(x)
    return s * (1.0 + x * (1.0 - s))


def _gelu(x):
    return 0.5 * x * (1.0 + jnp.tanh(GELU_C0 * (x + GELU_C1 * x * x * x)))


def _dgelu(x):
    t = jnp.tanh(GELU_C0 * (x + GELU_C1 * x * x * x))
    return 0.5 * (1.0 + t) + 0.5 * x * (1.0 - t * t) * GELU_C0 * (1.0 + 3.0 * GELU_C1 * x * x)


def _rms(x, g):
    r = lax.rsqrt(jnp.mean(x * x, axis=-1, keepdims=True) + EPS)
    return x * r * g


def _rms_bwd(x, g, dy):
    r = lax.rsqrt(jnp.mean(x * x, axis=-1, keepdims=True) + EPS)
    xh = x * r
    dxh = dy * g
    dx = r * (dxh - xh * jnp.mean(dxh * xh, axis=-1, keepdims=True))
    return dx, jnp.sum(dy * xh, axis=0, keepdims=True)


def _same_head(n, width, dtype):
    r = lax.broadcasted_iota(jnp.int32, (n, n), 0) // width
    c = lax.broadcasted_iota(jnp.int32, (n, n), 1) // width
    return (r == c).astype(dtype)


def _head_masks(rows=1):
    lane = lax.broadcasted_iota(jnp.int32, (rows, MIXW), 1) // HDIM
    return [lane == h for h in range(NHEAD)]


def _norm_matmul(x, g, w, tn):
    n = w.shape[1]
    tm = 1024

    def body(x_ref, g_ref, w_ref, z_ref, h_ref):
        @pl.when(pl.program_id(1) == 0)
        def _():
            h_ref[...] = _rms(x_ref[...], g_ref[...]).astype(bf16)

        z_ref[...] = jnp.dot(h_ref[...], w_ref[...], preferred_element_type=f32)

    return pl.pallas_call(
        body, name="norm_matmul", grid=(SEQ // tm, n // tn),
        in_specs=[pl.BlockSpec((tm, DM), lambda i, j: (i, 0)), pl.BlockSpec((1, DM), lambda i, j: (0, 0)),
                  pl.BlockSpec((DM, tn), lambda i, j: (0, j))],
        out_specs=[pl.BlockSpec((tm, tn), lambda i, j: (i, j)), pl.BlockSpec((tm, DM), lambda i, j: (i, 0))],
        out_shape=[_sds((SEQ, n), f32), _sds((SEQ, DM), bf16)],
        compiler_params=_params(("parallel", "arbitrary")),
    )(x, g, w)


def _matmul(a, w, tn):
    k, n = w.shape
    tm = 1024

    def body(a_ref, w_ref, z_ref):
        z_ref[...] = jnp.dot(a_ref[...], w_ref[...], preferred_element_type=f32)

    return pl.pallas_call(
        body, name="matmul", grid=(SEQ // tm, n // tn),
        in_specs=[pl.BlockSpec((tm, k), lambda i, j: (i, 0)), pl.BlockSpec((k, tn), lambda i, j: (0, j))],
        out_specs=pl.BlockSpec((tm, tn), lambda i, j: (i, j)),
        out_shape=_sds((SEQ, n), f32),
        compiler_params=_params(("parallel", "arbitrary")),
    )(a, w)


def _att_offset_map():
    i = lax.broadcasted_iota(jnp.int32, (REL_SIZE, ATT_WV), 0)
    t = lax.broadcasted_iota(jnp.int32, (REL_SIZE, ATT_WV), 1)
    e = jnp.where(t <= ATT_BAND, t, t - ATT_WV)
    idx = jnp.clip(ATT_PAD - e, -(HDIM - 1), 256) + (HDIM - 1)
    return (idx == i).astype(bf16)


def _att_band_valid():
    qc = lax.broadcasted_iota(jnp.int32, (ATT_PAIR, ATT_BAND), 0) // HDIM
    kc = lax.broadcasted_iota(jnp.int32, (ATT_PAIR, ATT_BAND), 1) // HDIM
    return (kc >= qc) & (kc <= qc + 8)


def _att_bias_tiles(rb_ref, bm_ref):
    wv = _dot_hl(rb_ref[...], _att_offset_map())
    valid = _att_band_valid()
    for h in range(NHEAD):
        rows = jnp.broadcast_to(wv[h:h + 1, :], (ATT_PAIR, ATT_WV))
        tile = pltpu.roll(rows, 0, 1, stride=1, stride_axis=0)[:, :ATT_BAND]
        bm_ref[h] = jnp.where(valid, tile, NEG_BIG)


def _att_pad_kv(k_ref, v_ref, kp_ref, vp_ref):
    kp_ref[pl.ds(0, ATT_PAD), :] = jnp.zeros((ATT_PAD, MIXW), bf16)
    vp_ref[pl.ds(0, ATT_PAD), :] = jnp.zeros((ATT_PAD, MIXW), bf16)
    kp_ref[pl.ds(ATT_PAD, SEQ), :] = k_ref[...].astype(bf16)
    vp_ref[pl.ds(ATT_PAD, SEQ), :] = v_ref[...].astype(bf16)


def _att_probs(qm, kb, bm, key_ok):
    s = _dot_nt(qm, kb) + bm
    s = jnp.where(key_ok, s, NEG_BIG)
    m = jnp.max(s, axis=-1, keepdims=True)
    e = jnp.exp(s - m)
    return e / jnp.sum(e, axis=-1, keepdims=True)


def _attn_fwd(zm, rb8):
    def body(q_ref, k_ref, v_ref, rb_ref, o_ref, kp_ref, vp_ref, bm_ref):
        _att_pad_kv(k_ref, v_ref, kp_ref, vp_ref)
        _att_bias_tiles(rb_ref, bm_ref)
        hm = _head_masks()

        def pair(p, carry):
            r0 = pl.multiple_of(p * ATT_PAIR, ATT_PAIR)
            q = q_ref[pl.ds(r0, ATT_PAIR), :] * (HDIM ** -0.5)
            kb = kp_ref[pl.ds(r0, ATT_BAND), :]
            vb = vp_ref[pl.ds(r0, ATT_BAND), :]
            key_ok = (lax.broadcasted_iota(jnp.int32, (1, ATT_BAND), 1) + (r0 - ATT_PAD)) >= 0
            o = jnp.zeros((ATT_PAIR, MIXW), f32)
            for h in range(NHEAD):
                qm = jnp.where(hm[h], q, 0.0)
                p_h = _att_probs(qm, kb, bm_ref[h], key_ok)
                o = o + jnp.where(hm[h], _dot(p_h, vb), 0.0)
            o_ref[pl.ds(r0, ATT_PAIR), :] = o.astype(bf16)
            return carry

        lax.fori_loop(0, SEQ // ATT_PAIR, pair, 0)

    col = lambda j: pl.BlockSpec((SEQ, MIXW), lambda i: (0, j))
    return pl.pallas_call(
        body, name="attn_fwd", grid=(1,),
        in_specs=[col(0), col(1), col(2), pl.BlockSpec((8, REL_SIZE), lambda i: (0, 0))],
        out_specs=pl.BlockSpec((SEQ, MIXW), lambda i: (0, 0)),
        out_shape=_sds((SEQ, MIXW), bf16),
        scratch_shapes=[pltpu.VMEM((SEQ + ATT_PAD, MIXW), bf16), pltpu.VMEM((SEQ + ATT_PAD, MIXW), bf16),
                        pltpu.VMEM((NHEAD, ATT_PAIR, ATT_BAND), f32)],
        compiler_params=_params(("arbitrary",)),
    )(zm, zm, zm, rb8)


def _hg_gates(q, fz, lb):
    sq = _sigmoid(q)
    sg = _sigmoid(fz)
    f = lb + (1.0 - lb) * sg
    return q * sq, (1.0 - lb) * (1.0 - sg), jnp.log(jnp.maximum(f, LOG_FLOOR)), sq, sg, f


def _hg_prepare(q_ref, f_ref, lb, qf_s, kf_s, b_s, qd_s, kd_s, dec_s):
    b = None
    for t in range(HG_T):
        qf, kf, lf, _, _, _ = _hg_gates(q_ref[:, t, :], f_ref[:, t, :], lb)
        b = lf if b is None else b + lf
        qf_s[:, t, :] = qf
        kf_s[:, t, :] = kf
        b_s[:, t, :] = b
    b_last = b
    dec_s[...] = jnp.broadcast_to(jnp.exp(b_last)[:, None, :], (HG_N, 8, MIXW))
    for t in range(HG_T):
        bt = b_s[:, t, :]
        qd_s[:, t, :] = qf_s[:, t, :] * jnp.exp(bt)
        kd_s[:, t, :] = kf_s[:, t, :] * jnp.exp(b_last - bt)


def _hg_scores(t, qf_s, kf_s, b_s, w_s, hm):
    qt = qf_s[:, t, :]
    bt = b_s[:, t, :]
    for s in range(t + 1):
        w = qt * kf_s[:, s, :]
        if s < t:
            w = w * jnp.exp(bt - b_s[:, s, :])
        w_s[pl.ds(s * HG_N, HG_N), :] = w.astype(bf16)
    return jnp.dot(w_s[pl.ds(0, (t + 1) * HG_N), :], hm, preferred_element_type=f32)


def _hgrn_fwd(zm3, lb, ng):
    def body(q_ref, f_ref, i_ref, g_ref, lb_ref, ng_ref, o_ref, oraw_ref, states_ref,
             qf_s, kf_s, b_s, qd_s, kd_s, dec_s, w_s, st_s):
        lb = lb_ref[...]
        hm = _same_head(MIXW, HDIM, bf16)
        hmf = _same_head(MIXW, HDIM, f32)
        _hg_prepare(q_ref, f_ref, lb, qf_s, kf_s, b_s, qd_s, kd_s, dec_s)
        for t in range(HG_T):
            p = _hg_scores(t, qf_s, kf_s, b_s, w_s, hm)
            acc = jnp.zeros((HG_N, MIXW), f32)
            for s in range(t + 1):
                acc = acc + p[s * HG_N:(s + 1) * HG_N] * i_ref[:, s, :]
            oraw_ref[:, t, :] = acc
        st_s[...] = jnp.zeros((MIXW, MIXW), f32)

        def step(n, carry):
            st = st_s[...]
            stb = st.astype(bf16)
            states_ref[n] = stb
            oraw_ref[n] = oraw_ref[n] + _dot_nt(qd_s[n], stb)
            st_s[...] = st * dec_s[n][0:1] + _dot_tn(i_ref[n], kd_s[n]) * hmf
            return carry

        lax.fori_loop(0, HG_N, step, 0)
        ngv = ng_ref[...]
        for t in range(HG_T):
            o = oraw_ref[:, t, :]
            ms = _dot_hl(o * o, hm) * (1.0 / HDIM)
            o_ref[:, t, :] = (o * lax.rsqrt(ms + EPS) * ngv * _silu(g_ref[:, t, :])).astype(bf16)

    one = pl.Buffered(1)
    col = lambda j: pl.BlockSpec((HG_N, HG_T, MIXW), lambda i: (0, 0, j), pipeline_mode=one)
    vec = pl.BlockSpec((1, MIXW), lambda i: (0, 0))
    blk = pl.BlockSpec((HG_N, HG_T, MIXW), lambda i: (0, 0, 0))
    s3 = pltpu.VMEM((HG_N, HG_T, MIXW), f32)
    return pl.pallas_call(
        body, name="hgrn_fwd", grid=(1,),
        in_specs=[col(3), col(4), col(5), col(6), vec, vec],
        out_specs=[blk, blk, pl.BlockSpec((HG_N, MIXW, MIXW), lambda i: (0, 0, 0), pipeline_mode=one)],
        out_shape=[_sds((HG_N, HG_T, MIXW), bf16), _sds((HG_N, HG_T, MIXW), f32), _sds((HG_N, MIXW, MIXW), bf16)],
        scratch_shapes=[s3, s3, s3, s3, s3, pltpu.VMEM((HG_N, 8, MIXW), f32),
                        pltpu.VMEM((HG_T * HG_N, MIXW), bf16), pltpu.VMEM((MIXW, MIXW), f32)],
        compiler_params=_params(("arbitrary",)),
    )(zm3, zm3, zm3, zm3, lb, ng)


def _gm_weights(ws_ref):
    tril = lax.broadcasted_iota(jnp.int32, (GM_T, GM_T), 0) >= lax.broadcasted_iota(jnp.int32, (GM_T, GM_T), 1)
    return tril, [jnp.where(tril, ws_ref[g], 0.0).astype(bf16) for g in range(NHEAD)]


def _gm_expand():
    r = lax.broadcasted_iota(jnp.int32, (8, MIXW), 0)
    c = lax.broadcasted_iota(jnp.int32, (8, MIXW), 1) // HDIM
    return (r == c).astype(bf16)


def _gm_mixed(vn, wts, bias, hm):
    vb = vn.astype(bf16)
    mixed = bias
    for g in range(NHEAD):
        mixed = mixed + jnp.where(hm[g], jnp.dot(wts[g], vb, preferred_element_type=f32), 0.0)
    return mixed


def _gm_bias(bs_ref):
    hi, lo = _split(bs_ref[...])
    et = _gm_expand()
    dn = (((0,), (0,)), ((), ()))
    return lax.dot_general(hi, et, dn, preferred_element_type=f32) + lax.dot_general(lo, et, dn, preferred_element_type=f32)


def _gmlp_fwd(zm, ng, ws, bs8):
    def body(u_ref, v_ref, ng_ref, ws_ref, bs_ref, o_ref):
        hm = _head_masks()
        _, wts = _gm_weights(ws_ref)
        bias = _gm_bias(bs_ref)
        ngv = ng_ref[...]

        def blk(n, carry):
            rows = pl.ds(pl.multiple_of(n * GM_T, GM_T), GM_T)
            vn = _rms(_gelu(v_ref[rows, :]), ngv)
            o_ref[rows, :] = (_gelu(u_ref[rows, :]) * _gm_mixed(vn, wts, bias, hm)).astype(bf16)
            return carry

        lax.fori_loop(0, SEQ // GM_T, blk, 0)

    col = lambda j: pl.BlockSpec((SEQ, MIXW), lambda i: (0, j))
    return pl.pallas_call(
        body, name="gmlp_fwd", grid=(1,),
        in_specs=[col(7), col(8), pl.BlockSpec((1, MIXW), lambda i: (0, 0)),
                  pl.BlockSpec((NHEAD, GM_T, GM_T), lambda i: (0, 0, 0)), pl.BlockSpec((8, GM_T), lambda i: (0, 0))],
        out_specs=pl.BlockSpec((SEQ, MIXW), lambda i: (0, 0)),
        out_shape=_sds((SEQ, MIXW), bf16),
        compiler_params=_params(("arbitrary",)),
    )(zm, zm, ng, ws, bs8)


def _lru_conv(x_ref, cw_ref, cb_ref, xp_s, xc_s):
    xp_s[pl.ds(0, 8), :] = jnp.zeros((8, MIXW), f32)
    xp_s[pl.ds(8, SEQ), :] = x_ref[...]
    cw = cw_ref[...]
    xc = cb_ref[...] + x_ref[...] * cw[3:4]
    for k in range(1, 4):
        xc = xc + xp_s[pl.ds(8 - k, SEQ), :] * cw[3 - k:4 - k]
    xc_s[...] = xc


def _lru_gates(xc, wa, ba, wx, bx, sp, first_row):
    r = _sigmoid(_dot(xc, wa) + ba)
    ig = _sigmoid(_dot(xc, wx) + bx)
    la = (-LRU_C) * r * sp
    a = jnp.exp(la)
    th = jnp.tanh(la)
    m2 = -2.0 * th / (1.0 - th)
    mult = jnp.where(first_row, 1.0, jnp.sqrt(jnp.maximum(m2, 0.0)))
    return a, mult, r, ig, m2


def _lru_scan(a, b, rev):
    row = lax.broadcasted_iota(jnp.int32, (LRU_T, 1), 0)
    k = 1
    while k < LRU_T:
        ok = (row < LRU_T - k) if rev else (row >= k)
        sh = (LRU_T - k) if rev else k
        a_sh = jnp.where(ok, pltpu.roll(a, sh, 0), 1.0)
        b_sh = jnp.where(ok, pltpu.roll(b, sh, 0), 0.0)
        b = b + a * b_sh
        a = a * a_sh
        k *= 2
    return a, b


def _lru_fwd(zm, cw8, cb, wa, ba, wx, bx, lam):
    def body(x_ref, g_ref, cw_ref, cb_ref, wa_ref, ba_ref, wx_ref, bx_ref, lam_ref, o_ref, h_ref, xp_s, xc_s):
        _lru_conv(x_ref, cw_ref, cb_ref, xp_s, xc_s)
        sp = jax.nn.softplus(-lam_ref[...])
        wa_v, wx_v, ba_v, bx_v = wa_ref[...], wx_ref[...], ba_ref[...], bx_ref[...]

        def chunk(c, h_prev):
            rows = pl.ds(pl.multiple_of(c * LRU_T, LRU_T), LRU_T)
            first = (lax.broadcasted_iota(jnp.int32, (LRU_T, 1), 0) + c * LRU_T) == 0
            xc = xc_s[rows, :]
            a, mult, _, ig, _ = _lru_gates(xc, wa_v, ba_v, wx_v, bx_v, sp, first)
            acum, hloc = _lru_scan(a, mult * (ig * xc), False)
            h = hloc + acum * h_prev
            h_ref[rows, :] = h
            o_ref[rows, :] = (h * _gelu(g_ref[rows, :])).astype(bf16)
            return h[LRU_T - 1:LRU_T, :]

        lax.fori_loop(0, SEQ // LRU_T, chunk, jnp.zeros((1, MIXW), f32))

    col = lambda j: pl.BlockSpec((SEQ, MIXW), lambda i: (0, j))
    vec = pl.BlockSpec((1, MIXW), lambda i: (0, 0))
    mat = pl.BlockSpec((MIXW, MIXW), lambda i: (0, 0))
    out = pl.BlockSpec((SEQ, MIXW), lambda i: (0, 0))
    return pl.pallas_call(
        body, name="lru_fwd", grid=(1,),
        in_specs=[col(9), col(10), pl.BlockSpec((8, MIXW), lambda i: (0, 0)), vec, mat, vec, mat, vec, vec],
        out_specs=[out, out],
        out_shape=[_sds((SEQ, MIXW), bf16), _sds((SEQ, MIXW), f32)],
        scratch_shapes=[pltpu.VMEM((SEQ + 8, MIXW), f32), pltpu.VMEM((SEQ, MIXW), f32)],
        compiler_params=_params(("arbitrary",)),
    )(zm, zm, cw8, cb, wa, ba, wx, bx, lam)


def _block_diag(w):
    out = jnp.zeros((MIXW, MIXW), w.dtype)
    for h in range(NHEAD):
        out = lax.dynamic_update_slice(out, w[h], (h * HDIM, h * HDIM))
    return out


def _diag_blocks(w):
    return jnp.stack([w[h * HDIM:(h + 1) * HDIM, h * HDIM:(h + 1) * HDIM] for h in range(NHEAD)])


ROW_TILE = 256


def _merge_fwd(outs, zg, wb, wo, x, g2):
    def body(oa_ref, ob_ref, oc_ref, od_ref, zg_ref, wb_ref, wo_ref, x_ref, g_ref, xo_ref, mg_ref, y_ref):
        merged = jnp.zeros((ROW_TILE, DM), f32)
        for n, o_ref in enumerate((oa_ref, ob_ref, oc_ref, od_ref)):
            proj = jnp.dot(o_ref[...], wb_ref[n], preferred_element_type=f32)
            merged = merged + _sigmoid(zg_ref[:, n * DM:(n + 1) * DM]) * proj
        mb = merged.astype(bf16)
        y = jnp.dot(mb, wo_ref[...], preferred_element_type=f32)
        mg_ref[...] = mb
        y_ref[...] = y
        xo_ref[...] = x_ref[...] + _rms(y, g_ref[...])

    row = lambda w: pl.BlockSpec((ROW_TILE, w), lambda i: (i, 0))
    return pl.pallas_call(
        body, name="merge_fwd", grid=(SEQ // ROW_TILE,),
        in_specs=[row(MIXW)] * 4 + [row(NGATE), pl.BlockSpec((NHEAD, MIXW, DM), lambda i: (0, 0, 0)),
                                    pl.BlockSpec((DM, DM), lambda i: (0, 0)), row(DM), pl.BlockSpec((1, DM), lambda i: (0, 0))],
        out_specs=[row(DM), row(DM), row(DM)],
        out_shape=[_sds((SEQ, DM), f32), _sds((SEQ, DM), bf16), _sds((SEQ, DM), f32)],
        compiler_params=_params(("parallel",)),
    )(*outs, zg, wb, wo, x, g2)


def _ffn_out(u, w2, x, g4):
    def body(u_ref, w_ref, x_ref, g_ref, xo_ref, f_ref):
        a = _silu(u_ref[:, :FFH]) * u_ref[:, FFH:]
        f = jnp.dot(a.astype(bf16), w_ref[...], preferred_element_type=f32)
        f_ref[...] = f
        xo_ref[...] = x_ref[...] + _rms(f, g_ref[...])

    row = lambda w: pl.BlockSpec((ROW_TILE, w), lambda i: (i, 0))
    return pl.pallas_call(
        body, name="ffn_out", grid=(SEQ // ROW_TILE,),
        in_specs=[row(2 * FFH), pl.BlockSpec((FFH, DM), lambda i: (0, 0)), row(DM), pl.BlockSpec((1, DM), lambda i: (0, 0))],
        out_specs=[row(DM), row(DM)],
        out_shape=[_sds((SEQ, DM), f32), _sds((SEQ, DM), f32)],
        compiler_params=_params(("parallel",)),
    )(u, w2, x, g4)


def _loss_head(x, tgt):
    tm = 512

    def body(x_ref, t_ref, l_ref, dx_ref):
        @pl.when(pl.program_id(0) == 0)
        def _():
            l_ref[...] = jnp.zeros((1, 1), f32)

        d = x_ref[...] - t_ref[...]
        dx_ref[...] = d * (1.0 / DM)
        l_ref[...] += (0.5 / DM) * jnp.sum(d * d).reshape(1, 1)

    row = pl.BlockSpec((tm, DM), lambda i: (i, 0))
    return pl.pallas_call(
        body, name="loss_head", grid=(SEQ // tm,),
        in_specs=[row, row], out_specs=[pl.BlockSpec((1, 1), lambda i: (0, 0)), row],
        out_shape=[_sds((1, 1), f32), _sds((SEQ, DM), f32)],
        compiler_params=_params(("arbitrary",)),
    )(x, tgt)


def _lb_fwd(logits):
    def body(lg_ref, o_ref):
        lg = lg_ref[...]
        e = jnp.exp(lg - jnp.max(lg, axis=0, keepdims=True))
        p = e / jnp.sum(e, axis=0, keepdims=True)
        acc = jnp.zeros((1, MIXW), f32)
        o_ref[0:1, :] = acc
        for l in range(1, DEPTH):
            acc = acc + p[l:l + 1]
            o_ref[l:l + 1, :] = acc

    return pl.pallas_call(body, name="lb_fwd", out_shape=_sds((DEPTH, MIXW), f32))(logits)


def _lb_bwd(logits, dlbs):
    def body(lg_ref, d_ref, o_ref):
        lg = lg_ref[...]
        e = jnp.exp(lg - jnp.max(lg, axis=0, keepdims=True))
        p = e / jnp.sum(e, axis=0, keepdims=True)
        d = d_ref[...]
        dp = [jnp.zeros((1, MIXW), f32)] * DEPTH
        acc = jnp.zeros((1, MIXW), f32)
        for j in range(DEPTH - 1, 0, -1):
            acc = acc + d[j:j + 1]
            dp[j] = acc
        inner = sum(p[j:j + 1] * dp[j] for j in range(DEPTH))
        for j in range(DEPTH):
            o_ref[j:j + 1, :] = p[j:j + 1] * (dp[j] - inner)

    return pl.pallas_call(body, name="lb_bwd", out_shape=_sds((DEPTH, MIXW), f32))(logits, dlbs)


def _pad_rows(a, rows=8):
    return jnp.concatenate([a, jnp.zeros((rows - a.shape[0], a.shape[1]), a.dtype)], axis=0)


def _layer_params(l, full, small, lbs):
    row = lambda name: small[name][l][None]
    return dict(
        _mix_weights(full), **(_ffn_weights(full) if "w_ffn_in" in full else {}),
        g1=row("norm_mix_pre"), g2=row("norm_mix_post"), g3=row("norm_ffn_pre"), g4=row("norm_ffn_post"),
        rb8=_pad_rows(small["attn_rel_bias"][l]), lb=lbs[l][None], hng=row("hgrn_norm_g"),
        gng=row("gmlp_norm_g"), gws=small["gmlp_ws"][l], gbs8=_pad_rows(small["gmlp_bs"][l]),
        cw8=_pad_rows(small["lru_conv_w"][l]), cb=row("lru_conv_b"),
        wa=_block_diag(small["lru_wa"][l]).astype(bf16), ba=row("lru_ba"),
        wx=_block_diag(small["lru_wx"][l]).astype(bf16), bx=row("lru_bx"), lam=row("lru_lambda"),
    )


def _mix_weights(full):
    return dict(wm=full["w_in"][:, :NMIX], wgt=full["w_in"][:, NMIX:], wb=full["w_branch"], wo=full["w_out"])


def _ffn_weights(full):
    return dict(w1=full["w_ffn_in"], w2=full["w_ffn_out"])


def _layer_fwd(x, p, late_ffn_weights=None):
    zm, h = _norm_matmul(x, p["g1"], p["wm"], 1408)
    zg = _matmul(h, p["wgt"], 1024)
    oa = _attn_fwd(zm, p["rb8"])
    ob3, obraw3, hstates = _hgrn_fwd(zm.reshape(HG_N, HG_T, NMIX), p["lb"], p["hng"])
    oc = _gmlp_fwd(zm, p["gng"], p["gws"], p["gbs8"])
    od, hd = _lru_fwd(zm, p["cw8"], p["cb"], p["wa"], p["ba"], p["wx"], p["bx"], p["lam"])
    outs = (oa, ob3.reshape(SEQ, MIXW), oc, od)
    x1, merged, y = _merge_fwd(outs, zg, p["wb"], p["wo"], x, p["g2"])
    if late_ffn_weights is not None:
        p.update(late_ffn_weights(x1))
    u, h2 = _norm_matmul(x1, p["g3"], p["w1"], 1408)
    x2, f = _ffn_out(u, p["w2"], x1, p["g4"])
    saved = dict(x=x, h=h, zm=zm, zg=zg, outs=outs, obraw3=obraw3, hstates=hstates, hd=hd, x1=x1, merged=merged, y=y, u=u, h2=h2, f=f)
    return x2, saved


def _att_bias_grad(db_ref, o_ref):
    r = lax.broadcasted_iota(jnp.int32, (ATT_PAIR, ATT_PAIR), 0)
    c = lax.broadcasted_iota(jnp.int32, (ATT_PAIR, ATT_PAIR), 1)
    flip = (r + c == ATT_PAIR - 1).astype(bf16)
    rows = []
    for h in range(NHEAD):
        d = jnp.concatenate([db_ref[h], jnp.zeros((ATT_PAIR, ATT_WV - ATT_BAND), f32)], axis=1)
        hi, lo = _split(d)
        rev = jnp.dot(flip, hi, preferred_element_type=f32) + jnp.dot(flip, lo, preferred_element_type=f32)
        lined = pltpu.roll(rev, ATT_WV - (ATT_PAIR - 1), 1, stride=1, stride_axis=0)
        rows.append(jnp.sum(lined, axis=0, keepdims=True))
    dwv = jnp.concatenate(rows + [jnp.zeros((8 - NHEAD, ATT_WV), f32)], axis=0)
    hi, lo = _split(dwv)
    m = _att_offset_map()
    dn = (((1,), (1,)), ((), ()))
    o_ref[...] = lax.dot_general(hi, m, dn, preferred_element_type=f32) + lax.dot_general(lo, m, dn, preferred_element_type=f32)


def _attn_bwd(zm, rb8, do):
    def body(q_ref, k_ref, v_ref, rb_ref, do_ref, dz_ref, drb_ref, kp_ref, vp_ref, bm_ref, dk_s, dv_s, db_s):
        _att_pad_kv(k_ref, v_ref, kp_ref, vp_ref)
        _att_bias_tiles(rb_ref, bm_ref)
        dk_s[...] = jnp.zeros_like(dk_s)
        dv_s[...] = jnp.zeros_like(dv_s)
        db_s[...] = jnp.zeros_like(db_s)
        hm = _head_masks()
        scale = HDIM ** -0.5

        def pair(p, carry):
            r0 = pl.multiple_of(p * ATT_PAIR, ATT_PAIR)
            q = q_ref[pl.ds(r0, ATT_PAIR), :] * scale
            dout = do_ref[pl.ds(r0, ATT_PAIR), :]
            kb = kp_ref[pl.ds(r0, ATT_BAND), :]
            vb = vp_ref[pl.ds(r0, ATT_BAND), :]
            key_ok = (lax.broadcasted_iota(jnp.int32, (1, ATT_BAND), 1) + (r0 - ATT_PAD)) >= 0
            dq = jnp.zeros((ATT_PAIR, MIXW), f32)
            dkb = jnp.zeros((ATT_BAND, MIXW), f32)
            dvb = jnp.zeros((ATT_BAND, MIXW), f32)
            for h in range(NHEAD):
                qm = jnp.where(hm[h], q, 0.0).astype(bf16)
                dom = jnp.where(hm[h], dout, 0.0).astype(bf16)
                p_h = _att_probs(qm, kb, bm_ref[h], key_ok)
                dp = _dot_nt(dom, vb)
                ds = p_h * (dp - jnp.sum(dp * p_h, axis=-1, keepdims=True))
                dsb = ds.astype(bf16)
                dq = dq + jnp.where(hm[h], _dot(dsb, kb), 0.0)
                dkb = dkb + _dot_tn(dsb, qm)
                dvb = dvb + _dot_tn(p_h, dom)
                db_s[h] = db_s[h] + ds
            dz_ref[pl.ds(r0, ATT_PAIR), 0:MIXW] = (dq * scale).astype(bf16)
            dk_s[pl.ds(r0, ATT_BAND), :] = dk_s[pl.ds(r0, ATT_BAND), :] + dkb
            dv_s[pl.ds(r0, ATT_BAND), :] = dv_s[pl.ds(r0, ATT_BAND), :] + dvb
            return carry

        lax.fori_loop(0, SEQ // ATT_PAIR, pair, 0)
        dz_ref[:, MIXW:2 * MIXW] = dk_s[pl.ds(ATT_PAD, SEQ), :].astype(bf16)
        dz_ref[:, 2 * MIXW:3 * MIXW] = dv_s[pl.ds(ATT_PAD, SEQ), :].astype(bf16)
        _att_bias_grad(db_s, drb_ref)

    col = lambda j: pl.BlockSpec((SEQ, MIXW), lambda i: (0, j))
    return pl.pallas_call(
        body, name="attn_bwd", grid=(1,),
        in_specs=[col(0), col(1), col(2), pl.BlockSpec((8, REL_SIZE), lambda i: (0, 0)), pl.BlockSpec((SEQ, MIXW), lambda i: (0, 0))],
        out_specs=[pl.BlockSpec((SEQ, 3 * MIXW), lambda i: (0, 0)), pl.BlockSpec((8, REL_SIZE), lambda i: (0, 0))],
        out_shape=[_sds((SEQ, 3 * MIXW), bf16), _sds((8, REL_SIZE), f32)],
        scratch_shapes=[pltpu.VMEM((SEQ + ATT_PAD, MIXW), bf16), pltpu.VMEM((SEQ + ATT_PAD, MIXW), bf16),
                        pltpu.VMEM((NHEAD, ATT_PAIR, ATT_BAND), f32),
                        pltpu.VMEM((SEQ + ATT_PAD, MIXW), f32), pltpu.VMEM((SEQ + ATT_PAD, MIXW), f32),
                        pltpu.VMEM((NHEAD, ATT_PAIR, ATT_BAND), f32)],
        compiler_params=_params(("arbitrary",)),
    )(zm, zm, zm, rb8, do)


def _hgrn_out_bwd(zm3, ng, oraw3, do3):
    def body(g_ref, ng_ref, o_ref, do_ref, dor_ref, dg_ref, dng_ref):
        hm = _same_head(MIXW, HDIM, bf16)
        ngv = ng_ref[...]
        dng = jnp.zeros((1, MIXW), f32)
        for t in range(HG_T):
            o, g, d = o_ref[:, t, :], g_ref[:, t, :], do_ref[:, t, :]
            rs = lax.rsqrt(_dot_hl(o * o, hm) * (1.0 / HDIM) + EPS)
            y1 = o * rs
            dy2 = d * _silu(g)
            dg_ref[:, t, :] = (d * y1 * ngv * _dsilu(g)).astype(bf16)
            dng = dng + jnp.sum(dy2 * y1, axis=0, keepdims=True)
            dy1 = dy2 * ngv
            dor_ref[:, t, :] = rs * (dy1 - y1 * (_dot_hl(dy1 * y1, hm) * (1.0 / HDIM)))
        dng_ref[...] = jnp.broadcast_to(dng, (8, MIXW))

    blk = pl.BlockSpec((HG_N, HG_T, MIXW), lambda i: (0, 0, 0))
    return pl.pallas_call(
        body, name="hgrn_out_bwd", grid=(1,),
        in_specs=[pl.BlockSpec((HG_N, HG_T, MIXW), lambda i: (0, 0, 6)), pl.BlockSpec((1, MIXW), lambda i: (0, 0)), blk, blk],
        out_specs=[blk, blk, pl.BlockSpec((8, MIXW), lambda i: (0, 0))],
        out_shape=[_sds((HG_N, HG_T, MIXW), f32), _sds((HG_N, HG_T, MIXW), bf16), _sds((8, MIXW), f32)],
        compiler_params=_params(("arbitrary",)),
    )(zm3, ng, oraw3, do3)


def _hgrn_bwd(zm3, lb, dor3, states):
    def body(q_ref, f_ref, i_ref, lb_ref, dor_ref, st_s, dz_ref, dlb_ref,
             qf_s, kf_s, b_s, dq_s, dk_s, db_s, dv_s, w_s, x_s, cur_s):
        lb = lb_ref[...]
        hm = _same_head(MIXW, HDIM, bf16)
        hmf = _same_head(MIXW, HDIM, f32)
        b = None
        for t in range(HG_T):
            qf, kf, lf, _, _, _ = _hg_gates(q_ref[:, t, :], f_ref[:, t, :], lb)
            b = lf if b is None else b + lf
            qf_s[:, t, :] = qf
            kf_s[:, t, :] = kf
            b_s[:, t, :] = b

        def block_terms(n):
            bn = b_s[n]
            bl = bn[HG_T - 1:HG_T]
            eb = jnp.exp(bn)
            ek = jnp.exp(bl - bn)
            return qf_s[n] * eb, kf_s[n] * ek, jnp.exp(bl), eb, ek

        cur_s[...] = jnp.zeros((MIXW, MIXW), f32)
        last = lax.broadcasted_iota(jnp.int32, (HG_T, 1), 0) == HG_T - 1

        def bwd_step(j, carry):
            n = HG_N - 1 - j
            qd, kd, dec, eb, ek = block_terms(n)
            v, do_n = i_ref[n], dor_ref[n]
            dst = cur_s[...]
            st = st_s[n]
            dqd = _dot(do_n, st)
            dkd = _dot(v, dst)
            ddec = jnp.sum(dst * st.astype(f32), axis=0, keepdims=True)
            cur_s[...] = dst * dec + _dot_tn(do_n, qd) * hmf
            dq_s[n] = dqd * eb
            dk_s[n] = dkd * ek
            dv_s[n] = _dot_nt(kd, dst)
            dbl = jnp.sum(dkd * kd, axis=0, keepdims=True) + ddec * dec
            db_s[n] = dqd * qd - dkd * kd + jnp.where(last, dbl, 0.0)
            return carry

        lax.fori_loop(0, HG_N, bwd_step, 0)
        for t in range(HG_T):
            qt, bt, dot_t = qf_s[:, t, :], b_s[:, t, :], dor_ref[:, t, :]
            for s in range(t + 1):
                w = qt * kf_s[:, s, :]
                if s < t:
                    w = w * jnp.exp(bt - b_s[:, s, :])
                w_s[pl.ds(s * HG_N, HG_N), :] = w.astype(bf16)
                x_s[pl.ds(s * HG_N, HG_N), :] = (dot_t * i_ref[:, s, :]).astype(bf16)
            p = jnp.dot(w_s[pl.ds(0, (t + 1) * HG_N), :], hm, preferred_element_type=f32)
            dp = jnp.dot(x_s[pl.ds(0, (t + 1) * HG_N), :], hm, preferred_element_type=f32)
            dq_t = jnp.zeros((HG_N, MIXW), f32)
            db_t = jnp.zeros((HG_N, MIXW), f32)
            for s in range(t + 1):
                ps = p[s * HG_N:(s + 1) * HG_N]
                dps = dp[s * HG_N:(s + 1) * HG_N]
                ks = kf_s[:, s, :]
                dv_s[:, s, :] = dv_s[:, s, :] + ps * dot_t
                if s < t:
                    dec_ts = jnp.exp(bt - b_s[:, s, :])
                    g1 = dps * ks * dec_ts
                    dk_s[:, s, :] = dk_s[:, s, :] + dps * qt * dec_ts
                    gw = g1 * qt
                    db_t = db_t + gw
                    db_s[:, s, :] = db_s[:, s, :] - gw
                else:
                    g1 = dps * ks
                    dk_s[:, s, :] = dk_s[:, s, :] + dps * qt
                dq_t = dq_t + g1
            dq_s[:, t, :] = dq_s[:, t, :] + dq_t
            db_s[:, t, :] = db_s[:, t, :] + db_t
        run = jnp.zeros((HG_N, MIXW), f32)
        dlb = jnp.zeros((1, MIXW), f32)
        oml = 1.0 - lb
        for t in range(HG_T - 1, -1, -1):
            run = run + db_s[:, t, :]
            q = q_ref[:, t, :]
            _, _, _, sq, sg, f = _hg_gates(q, f_ref[:, t, :], lb)
            dkf = dk_s[:, t, :]
            df = jnp.where(f > LOG_FLOOR, run / f, 0.0)
            dsg = (df - dkf) * oml
            dlb = dlb + jnp.sum((df - dkf) * (1.0 - sg), axis=0, keepdims=True)
            dz_ref[:, t, 0:MIXW] = (dq_s[:, t, :] * sq * (1.0 + q * (1.0 - sq))).astype(bf16)
            dz_ref[:, t, MIXW:2 * MIXW] = (dsg * sg * (1.0 - sg)).astype(bf16)
            dz_ref[:, t, 2 * MIXW:3 * MIXW] = dv_s[:, t, :].astype(bf16)
        dlb_ref[...] = jnp.broadcast_to(dlb, (8, MIXW))

    one = pl.Buffered(1)
    col = lambda j: pl.BlockSpec((HG_N, HG_T, MIXW), lambda i: (0, 0, j), pipeline_mode=one)
    s3 = pltpu.VMEM((HG_N, HG_T, MIXW), f32)
    return pl.pallas_call(
        body, name="hgrn_bwd", grid=(1,),
        in_specs=[col(3), col(4), col(5), pl.BlockSpec((1, MIXW), lambda i: (0, 0)),
                  pl.BlockSpec((HG_N, HG_T, MIXW), lambda i: (0, 0, 0), pipeline_mode=one),
                  pl.BlockSpec((HG_N, MIXW, MIXW), lambda i: (0, 0, 0), pipeline_mode=one)],
        out_specs=[pl.BlockSpec((HG_N, HG_T, 3 * MIXW), lambda i: (0, 0, 0)), pl.BlockSpec((8, MIXW), lambda i: (0, 0))],
        out_shape=[_sds((HG_N, HG_T, 3 * MIXW), bf16), _sds((8, MIXW), f32)],
        scratch_shapes=[s3, s3, s3, s3, s3, s3, s3,
                        pltpu.VMEM((HG_T * HG_N, MIXW), bf16), pltpu.VMEM((HG_T * HG_N, MIXW), bf16),
                        pltpu.VMEM((MIXW, MIXW), f32)],
        compiler_params=_params(("arbitrary",)),
    )(zm3, zm3, zm3, lb, dor3, states)


def _gmlp_bwd(zm, ng, ws, bs8, do):
    def body(u_ref, v_ref, ng_ref, ws_ref, bs_ref, do_ref, dz_ref, dws_ref, dng_ref, dbs_ref, dm_s):
        hm = _head_masks()
        tril, wts = _gm_weights(ws_ref)
        bias = _gm_bias(bs_ref)
        ngv = ng_ref[...]
        dws_ref[...] = jnp.zeros_like(dws_ref)
        dm_s[...] = jnp.zeros_like(dm_s)

        def blk(n, dng):
            rows = pl.ds(pl.multiple_of(n * GM_T, GM_T), GM_T)
            cu, cv, d = u_ref[rows, :], v_ref[rows, :], do_ref[rows, :]
            v = _gelu(cv)
            r = lax.rsqrt(jnp.mean(v * v, axis=-1, keepdims=True) + EPS)
            vh = v * r
            vn = vh * ngv
            u = _gelu(cu)
            dm = d * u
            dmb, vnb = dm.astype(bf16), vn.astype(bf16)
            dvn = jnp.zeros((GM_T, MIXW), f32)
            for g in range(NHEAD):
                dws_ref[g] = dws_ref[g] + _dot_nt(jnp.where(hm[g], dm, 0.0), vnb)
                dvn = dvn + jnp.where(hm[g], _dot_tn(wts[g], dmb), 0.0)
            dm_s[...] = dm_s[...] + dm
            dvh = dvn * ngv
            dv = r * (dvh - vh * jnp.mean(dvh * vh, axis=-1, keepdims=True))
            dz_ref[rows, 0:MIXW] = (d * _gm_mixed(vn, wts, bias, hm) * _dgelu(cu)).astype(bf16)
            dz_ref[rows, MIXW:2 * MIXW] = (dv * _dgelu(cv)).astype(bf16)
            return dng + jnp.sum(dvn * vh, axis=0, keepdims=True)

        dng = lax.fori_loop(0, SEQ // GM_T, blk, jnp.zeros((1, MIXW), f32))
        dng_ref[...] = jnp.broadcast_to(dng, (8, MIXW))
        for g in range(NHEAD):
            dws_ref[g] = jnp.where(tril, dws_ref[g], 0.0)
        dbs_ref[...] = _dot_nt_hl(_gm_expand(), dm_s[...])

    col = lambda j: pl.BlockSpec((SEQ, MIXW), lambda i: (0, j))
    return pl.pallas_call(
        body, name="gmlp_bwd", grid=(1,),
        in_specs=[col(7), col(8), pl.BlockSpec((1, MIXW), lambda i: (0, 0)),
                  pl.BlockSpec((NHEAD, GM_T, GM_T), lambda i: (0, 0, 0)), pl.BlockSpec((8, GM_T), lambda i: (0, 0)),
                  pl.BlockSpec((SEQ, MIXW), lambda i: (0, 0))],
        out_specs=[pl.BlockSpec((SEQ, 2 * MIXW), lambda i: (0, 0)), pl.BlockSpec((NHEAD, GM_T, GM_T), lambda i: (0, 0, 0)),
                   pl.BlockSpec((8, MIXW), lambda i: (0, 0)), pl.BlockSpec((8, GM_T), lambda i: (0, 0))],
        out_shape=[_sds((SEQ, 2 * MIXW), bf16), _sds((NHEAD, GM_T, GM_T), f32), _sds((8, MIXW), f32), _sds((8, GM_T), f32)],
        scratch_shapes=[pltpu.VMEM((GM_T, MIXW), f32)],
        compiler_params=_params(("arbitrary",)),
    )(zm, zm, ng, ws, bs8, do)


def _lru_bwd(zm, cw8, cb, wa, ba, wx, bx, lam, hd, do):
    nchunk = SEQ // LRU_T

    def body(x_ref, g_ref, cw_ref, cb_ref, wa_ref, ba_ref, wx_ref, bx_ref, lam_ref, h_ref, do_ref,
             dz_ref, dwa_ref, dwx_ref, dcw_ref, dvec_ref, xp_s, xc_s, dxc_s):
        _lru_conv(x_ref, cw_ref, cb_ref, xp_s, xc_s)
        lam_v = lam_ref[...]
        sp = jax.nn.softplus(-lam_v)
        sgl = _sigmoid(-lam_v)
        wa_v, wx_v, ba_v, bx_v = wa_ref[...], wx_ref[...], ba_ref[...], bx_ref[...]
        dwa_ref[...] = jnp.zeros_like(dwa_ref)
        dwx_ref[...] = jnp.zeros_like(dwx_ref)
        dxc_s[pl.ds(SEQ, 8), :] = jnp.zeros((8, MIXW), f32)
        row = lax.broadcasted_iota(jnp.int32, (LRU_T, 1), 0)
        zero = jnp.zeros((1, MIXW), f32)

        def chunk(j, carry):
            dh_next, a_next, dba, dbx, dlam = carry
            c = nchunk - 1 - j
            rows = pl.ds(pl.multiple_of(c * LRU_T, LRU_T), LRU_T)
            prev = pl.ds(pl.multiple_of(jnp.maximum(c - 1, 0) * LRU_T, LRU_T), LRU_T)
            first = (row + c * LRU_T) == 0
            xc, gate, d, h = xc_s[rows, :], g_ref[rows, :], do_ref[rows, :], h_ref[rows, :]
            a, mult, r, ig, m2 = _lru_gates(xc, wa_v, ba_v, wx_v, bx_v, sp, first)
            h_last = jnp.where(c > 0, h_ref[prev, :][LRU_T - 1:LRU_T, :], 0.0)
            h_m1 = jnp.where(row == 0, h_last, pltpu.roll(h, 1, 0))
            a_up = jnp.where(row == LRU_T - 1, a_next, pltpu.roll(a, LRU_T - 1, 0))
            acum, dh_loc = _lru_scan(a_up, d * _gelu(gate), True)
            dh = dh_loc + acum * dh_next
            dmult = jnp.where(first, 0.0, dh * (ig * xc))
            msq = jnp.sqrt(jnp.maximum(m2, 0.0))
            dla = dh * h_m1 * a + jnp.where(m2 > 0.0, -dmult * (1.0 - m2) / msq, 0.0)
            dpr = dla * (-LRU_C) * sp * r * (1.0 - r)
            dpi = dh * mult * xc * ig * (1.0 - ig)
            dxc_s[rows, :] = dh * mult * ig + _dot_nt(dpr, wa_v) + _dot_nt(dpi, wx_v)
            dwa_ref[...] = dwa_ref[...] + _dot_tn(xc, dpr)
            dwx_ref[...] = dwx_ref[...] + _dot_tn(xc, dpi)
            dz_ref[rows, MIXW:2 * MIXW] = (d * h * _dgelu(gate)).astype(bf16)
            return (dh[0:1], a[0:1], dba + jnp.sum(dpr, axis=0, keepdims=True), dbx + jnp.sum(dpi, axis=0, keepdims=True),
                    dlam + jnp.sum(dla * r, axis=0, keepdims=True) * (LRU_C * sgl))

        _, _, dba, dbx, dlam = lax.fori_loop(0, nchunk, chunk, (zero, zero, zero, zero, zero))
        cw = cw_ref[...]
        dxc = dxc_s[pl.ds(0, SEQ), :]
        dx = dxc * cw[3:4]
        dcw = [None] * 4
        dcw[3] = jnp.sum(dxc * x_ref[...], axis=0, keepdims=True)
        for k in range(1, 4):
            dx = dx + dxc_s[pl.ds(k, SEQ), :] * cw[3 - k:4 - k]
            dcw[3 - k] = jnp.sum(dxc * xp_s[pl.ds(8 - k, SEQ), :], axis=0, keepdims=True)
        dz_ref[:, 0:MIXW] = dx.astype(bf16)
        dcw_ref[...] = jnp.concatenate(dcw + [jnp.zeros((4, MIXW), f32)], axis=0)
        dvec_ref[...] = jnp.concatenate([jnp.sum(dxc, axis=0, keepdims=True), dba, dbx, dlam, jnp.zeros((4, MIXW), f32)], axis=0)

    col = lambda j: pl.BlockSpec((SEQ, MIXW), lambda i: (0, j))
    vec = pl.BlockSpec((1, MIXW), lambda i: (0, 0))
    vec8 = pl.BlockSpec((8, MIXW), lambda i: (0, 0))
    mat = pl.BlockSpec((MIXW, MIXW), lambda i: (0, 0))
    full = pl.BlockSpec((SEQ, MIXW), lambda i: (0, 0))
    return pl.pallas_call(
        body, name="lru_bwd", grid=(1,),
        in_specs=[col(9), col(10), vec8, vec, mat, vec, mat, vec, vec, full, full],
        out_specs=[pl.BlockSpec((SEQ, 2 * MIXW), lambda i: (0, 0)), mat, mat, vec8, vec8],
        out_shape=[_sds((SEQ, 2 * MIXW), bf16), _sds((MIXW, MIXW), f32), _sds((MIXW, MIXW), f32),
                   _sds((8, MIXW), f32), _sds((8, MIXW), f32)],
        scratch_shapes=[pltpu.VMEM((SEQ + 8, MIXW), f32), pltpu.VMEM((SEQ, MIXW), f32), pltpu.VMEM((SEQ + 8, MIXW), f32)],
        compiler_params=_params(("arbitrary",)),
    )(zm, zm, cw8, cb, wa, ba, wx, bx, lam, hd, do)


def _matmul_tn(a, b, tm, tn, b_col0=0):
    m = a.shape[1]
    n = tn if b_col0 else b.shape[1]
    off = b_col0 // tn

    def body(a_ref, b_ref, o_ref):
        o_ref[...] = _dot_tn(a_ref[...], b_ref[...]).astype(bf16)

    return pl.pallas_call(
        body, name="matmul_tn", grid=(m // tm, n // tn),
        in_specs=[pl.BlockSpec((SEQ, tm), lambda i, j: (0, i)), pl.BlockSpec((SEQ, tn), lambda i, j: (0, j + off))],
        out_specs=pl.BlockSpec((tm, tn), lambda i, j: (i, j)),
        out_shape=_sds((m, n), bf16),
        compiler_params=_params(("parallel", "arbitrary")),
    )(a, b)


def _matmul_nt_norm(pairs, x, g, dres):
    tm = 512
    steps = [a.shape[1] // t for a, _, t in pairs]
    starts = [sum(steps[:i]) for i in range(len(pairs))]
    total = sum(steps)
    npair = len(pairs)

    def body(*refs):
        a_refs, w_refs = refs[0:2 * npair:2], refs[1:2 * npair:2]
        x_ref, g_ref, dres_ref, dx_ref, dg_ref, acc_s = refs[2 * npair:]
        i, k = pl.program_id(0), pl.program_id(1)

        @pl.when(k == 0)
        def _():
            acc_s[...] = jnp.zeros_like(acc_s)

        @pl.when((i == 0) & (k == 0))
        def _():
            dg_ref[...] = jnp.zeros_like(dg_ref)

        for q in range(npair):
            @pl.when((k >= starts[q]) & (k < starts[q] + steps[q]))
            def _(q=q):
                acc_s[...] += _dot_nt(a_refs[q][...], w_refs[q][...])

        @pl.when(k == total - 1)
        def _():
            dx, dg = _rms_bwd(x_ref[...], g_ref[...], acc_s[...])
            dx_ref[...] = dres_ref[...] + dx
            dg_ref[...] += dg

    in_specs, args = [], []
    for q, (a, w, t) in enumerate(pairs):
        kmap = lambda k, q=q: jnp.clip(k - starts[q], 0, steps[q] - 1)
        in_specs += [pl.BlockSpec((tm, t), lambda i, k, kmap=kmap: (i, kmap(k))),
                     pl.BlockSpec((DM, t), lambda i, k, kmap=kmap: (0, kmap(k)))]
        args += [a, w]
    row = pl.BlockSpec((tm, DM), lambda i, k: (i, 0))
    vec = pl.BlockSpec((1, DM), lambda i, k: (0, 0))
    return pl.pallas_call(
        body, name="matmul_nt_norm", grid=(SEQ // tm, total),
        in_specs=in_specs + [row, vec, row], out_specs=[row, vec],
        out_shape=[_sds((SEQ, DM), f32), _sds((1, DM), f32)],
        scratch_shapes=[pltpu.VMEM((tm, DM), f32)],
        compiler_params=_params(("arbitrary", "arbitrary")),
    )(*args, x, g, dres)


def _merge_bwd(dx1, y, g2, outs, zg, wb, wo):
    def body(dx_ref, y_ref, g_ref, oa_ref, ob_ref, oc_ref, od_ref, zg_ref, wb_ref, wo_ref,
             da_ref, db_ref, dc_ref, dd_ref, dzg_ref, dpj_ref, dy_ref, dg_ref):
        @pl.when(pl.program_id(0) == 0)
        def _():
            dg_ref[...] = jnp.zeros_like(dg_ref)

        dy, dg = _rms_bwd(y_ref[...], g_ref[...], dx_ref[...])
        dg_ref[...] += dg
        dyb = dy.astype(bf16)
        dy_ref[...] = dyb
        dmerged = _dot_nt(dyb, wo_ref[...])
        for n, (o_ref, do_ref) in enumerate(((oa_ref, da_ref), (ob_ref, db_ref), (oc_ref, dc_ref), (od_ref, dd_ref))):
            cols = slice(n * DM, (n + 1) * DM)
            gate = _sigmoid(zg_ref[:, cols])
            proj = jnp.dot(o_ref[...], wb_ref[n], preferred_element_type=f32)
            dproj = (dmerged * gate).astype(bf16)
            dpj_ref[:, cols] = dproj
            dzg_ref[:, cols] = (dmerged * proj * gate * (1.0 - gate)).astype(bf16)
            do_ref[...] = _dot_nt(dproj, wb_ref[n])

    row = lambda w: pl.BlockSpec((ROW_TILE, w), lambda i: (i, 0))
    vec = pl.BlockSpec((1, DM), lambda i: (0, 0))
    return pl.pallas_call(
        body, name="merge_bwd", grid=(SEQ // ROW_TILE,),
        in_specs=[row(DM), row(DM), vec] + [row(MIXW)] * 4 + [row(NGATE), pl.BlockSpec((NHEAD, MIXW, DM), lambda i: (0, 0, 0)),
                                                              pl.BlockSpec((DM, DM), lambda i: (0, 0))],
        out_specs=[row(MIXW)] * 4 + [row(NGATE), row(NGATE), row(DM), vec],
        out_shape=[_sds((SEQ, MIXW), f32)] * 4 + [_sds((SEQ, NGATE), bf16), _sds((SEQ, NGATE), bf16), _sds((SEQ, DM), bf16),
                                                  _sds((1, DM), f32)],
        compiler_params=_params(("arbitrary",)),
    )(dx1, y, g2, *outs, zg, wb, wo)


def _ffn_bwd(dx2, f, g4, u, w2):
    def body(dx_ref, f_ref, g_ref, u_ref, w_ref, du_ref, a_ref, df_ref, dg_ref):
        @pl.when(pl.program_id(0) == 0)
        def _():
            dg_ref[...] = jnp.zeros_like(dg_ref)

        df, dg = _rms_bwd(f_ref[...], g_ref[...], dx_ref[...])
        dg_ref[...] += dg
        dfb = df.astype(bf16)
        df_ref[...] = dfb
        da = _dot_nt(dfb, w_ref[...])
        gt, up = u_ref[:, :FFH], u_ref[:, FFH:]
        a_ref[...] = (_silu(gt) * up).astype(bf16)
        du_ref[:, :FFH] = (da * up * _dsilu(gt)).astype(bf16)
        du_ref[:, FFH:] = (da * _silu(gt)).astype(bf16)

    row = lambda w: pl.BlockSpec((ROW_TILE, w), lambda i: (i, 0))
    vec = pl.BlockSpec((1, DM), lambda i: (0, 0))
    return pl.pallas_call(
        body, name="ffn_bwd", grid=(SEQ // ROW_TILE,),
        in_specs=[row(DM), row(DM), vec, row(2 * FFH), pl.BlockSpec((FFH, DM), lambda i: (0, 0))],
        out_specs=[row(2 * FFH), row(FFH), row(DM), vec],
        out_shape=[_sds((SEQ, 2 * FFH), bf16), _sds((SEQ, FFH), bf16), _sds((SEQ, DM), bf16), _sds((1, DM), f32)],
        compiler_params=_params(("arbitrary",)),
    )(dx2, f, g4, u, w2)


def _layer_bwd(dx2, p, sv, ffn_grads_ready=None):
    du, act, df, dg4 = _ffn_bwd(dx2, sv["f"], p["g4"], sv["u"], p["w2"])
    dw2 = _matmul_tn(act, df, 1408, DM)
    dx1, dg3 = _matmul_nt_norm([(du, p["w1"], 1408)], sv["x1"], p["g3"], dx2)
    dw1 = _matmul_tn(sv["h2"], du, DM, 1408)
    if ffn_grads_ready is not None:
        dx1 = ffn_grads_ready(dict(w_ffn_in=dw1, w_ffn_out=dw2), dx1)
    *dos, dzg, dproj, dy, dg2 = _merge_bwd(dx1, sv["y"], p["g2"], sv["outs"], sv["zg"], p["wb"], p["wo"])
    dwo = _matmul_tn(sv["merged"], dy, DM, DM)
    dwb = jnp.stack([_matmul_tn(sv["outs"][n], dproj, MIXW, DM, b_col0=n * DM) if n else
                     _matmul_tn(sv["outs"][0], dproj[:, :DM], MIXW, DM) for n in range(NHEAD)])
    zm = sv["zm"]
    zm3 = zm.reshape(HG_N, HG_T, NMIX)
    dza, drb = _attn_bwd(zm, p["rb8"], dos[0])
    dor, dgb, dhng = _hgrn_out_bwd(zm3, p["hng"], sv["obraw3"], dos[1].reshape(HG_N, HG_T, MIXW))
    dzb, dlb = _hgrn_bwd(zm3, p["lb"], dor, sv["hstates"])
    dzc, dws, dgng, dbs = _gmlp_bwd(zm, p["gng"], p["gws"], p["gbs8"], dos[2])
    dzd, dwa, dwx, dcw, dvec = _lru_bwd(zm, p["cw8"], p["cb"], p["wa"], p["ba"], p["wx"], p["bx"], p["lam"], sv["hd"], dos[3])
    dzm = jnp.concatenate([dza, dzb.reshape(SEQ, 3 * MIXW), dgb.reshape(SEQ, MIXW), dzc, dzd], axis=1)
    dx0, dg1 = _matmul_nt_norm([(dzm, p["wm"], 1408), (dzg, p["wgt"], 1024)], sv["x"], p["g1"], dx1)
    dwin = jnp.concatenate([_matmul_tn(sv["h"], dzm, DM, 1408), _matmul_tn(sv["h"], dzg, DM, 1024)], axis=1)
    big = dict(w_in=dwin, w_branch=dwb, w_out=dwo, w_ffn_in=dw1, w_ffn_out=dw2)
    small = dict(
        norm_mix_pre=dg1[0], norm_mix_post=dg2[0], norm_ffn_pre=dg3[0], norm_ffn_post=dg4[0],
        attn_rel_bias=drb[:NHEAD], lb=dlb[0], hgrn_norm_g=dhng[0], gmlp_norm_g=dgng[0], gmlp_ws=dws, gmlp_bs=dbs[:NHEAD],
        lru_conv_w=dcw[:NHEAD], lru_conv_b=dvec[0], lru_wa=_diag_blocks(dwa), lru_ba=dvec[1], lru_wx=_diag_blocks(dwx),
        lru_bx=dvec[2], lru_lambda=dvec[3],
    )
    return dx0, big, small


MIX_BIG = ("w_in", "w_branch", "w_out")
FFN_BIG = ("w_ffn_in", "w_ffn_out")
BIG = MIX_BIG + FFN_BIG
SMALL = ("norm_mix_pre", "norm_mix_post", "norm_ffn_pre", "norm_ffn_post", "attn_rel_bias", "hgrn_lb_logits", "hgrn_norm_g",
         "gmlp_norm_g", "gmlp_ws", "gmlp_bs", "lru_conv_w", "lru_conv_b", "lru_wa", "lru_ba", "lru_wx", "lru_bx", "lru_lambda")


def _local_step(x, tgt, full, small):
    lbs = _lb_fwd(small["hgrn_lb_logits"])
    params, saved = [], []
    for l in range(DEPTH):
        p = _layer_params(l, {k: full[k][l] for k in BIG}, small, lbs)
        x, sv = _layer_fwd(x, p)
        params.append(p)
        saved.append(sv)
    loss, dx = _loss_head(x, tgt)
    bigs, smalls = [None] * DEPTH, [None] * DEPTH
    for l in range(DEPTH - 1, -1, -1):
        dx, bigs[l], smalls[l] = _layer_bwd(dx, params[l], saved[l])
    gbig = {k: jnp.stack([bigs[l][k] for l in range(DEPTH)]) for k in BIG}
    gsmall = {k: jnp.stack([smalls[l][k] for l in range(DEPTH)]) for k in smalls[0]}
    gsmall["hgrn_lb_logits"] = _lb_bwd(small["hgrn_lb_logits"], gsmall.pop("lb"))
    return loss, dx, gbig, gsmall


HBM_ANY = pl.BlockSpec(memory_space=pl.ANY)


def _mesh_pos():
    return lax.axis_index("x"), lax.axis_index("y"), lax.axis_index("c")


def _all_gather(x, name):
    def body(x_ref, out_ref, send_sems, recv_sems, local_sem):
        ax, ay, ac = _mesh_pos()
        me, sibling = (ax, ay, ac), (ax, ay, 1 - ac)
        chips = [(1 - ax, ay), (ax, 1 - ay), (1 - ax, 1 - ay)]

        def slot(px, py, pc):
            return out_ref.at[4 * px + 2 * py + pc]

        def copy(k, block, to, src=None):
            return pltpu.make_async_remote_copy(
                src_ref=slot(*block) if src is None else src, dst_ref=slot(*block),
                send_sem=send_sems.at[k], recv_sem=recv_sems.at[k], device_id=to, device_id_type=MESH_ID)

        mine = pltpu.make_async_copy(x_ref, slot(*me), local_sem)
        mine.start()
        first = [copy(0, me, sibling, src=x_ref)]
        first += [copy(1 + j, me, (*chip, ac), src=x_ref) for j, chip in enumerate(chips)]
        for cp in first:
            cp.start()
        passed = [copy(4 + j, (*chip, ac), sibling) for j, chip in enumerate(chips)]
        for j, chip in enumerate(chips):
            copy(1 + j, (*chip, ac), me).wait_recv()
            passed[j].start()
        copy(0, sibling, me).wait_recv()
        for j, chip in enumerate(chips):
            copy(4 + j, (*chip, 1 - ac), me).wait_recv()
        for cp in first + passed:
            cp.wait_send()
        mine.wait()

    return pl.pallas_call(
        body, name=name, out_shape=_sds((NDEV,) + x.shape, x.dtype),
        in_specs=[HBM_ANY], out_specs=HBM_ANY,
        scratch_shapes=[pltpu.SemaphoreType.DMA((7,)), pltpu.SemaphoreType.DMA((7,)), pltpu.SemaphoreType.DMA],
    )(x)


def _exchange(g, name):
    def body(g_ref, out_ref, send_sems, recv_sems, local_sem):
        ax, ay, ac = _mesh_pos()
        me = 4 * ax + 2 * ay + ac
        mine = pltpu.make_async_copy(g_ref.at[me], out_ref.at[me], local_sem)
        mine.start()
        copies = []
        for k in range(1, NDEV):
            px = 1 - ax if k & 4 else ax
            py = 1 - ay if k & 2 else ay
            pc = 1 - ac if k & 1 else ac
            copies.append(pltpu.make_async_remote_copy(
                src_ref=g_ref.at[4 * px + 2 * py + pc], dst_ref=out_ref.at[me],
                send_sem=send_sems.at[k - 1], recv_sem=recv_sems.at[k - 1], device_id=(px, py, pc), device_id_type=MESH_ID))
        for cp in copies:
            cp.start()
        for cp in copies:
            cp.wait()
        mine.wait()

    return pl.pallas_call(
        body, name=name, out_shape=_sds(g.shape, g.dtype),
        in_specs=[HBM_ANY], out_specs=HBM_ANY,
        scratch_shapes=[pltpu.SemaphoreType.DMA((7,)), pltpu.SemaphoreType.DMA((7,)), pltpu.SemaphoreType.DMA],
    )(g)


def _peer(ax, ay, ac, k):
    return (1 - ax if k & 4 else ax, 1 - ay if k & 2 else ay, 1 - ac if k & 1 else ac)


def _handshake(peers):
    barrier = pltpu.get_barrier_semaphore()
    for peer in peers:
        pl.semaphore_signal(barrier, inc=1, device_id=peer, device_id_type=MESH_ID)
    pl.semaphore_wait(barrier, len(peers))


SEQUENCER = dict(axis_name="seq", num_cores=1)
GATHER_ID = 1
EXCHANGE_ID = 2


def _gather_sc(xs, name):
    n = len(xs)

    def body(*refs):
        srcs, outs = refs[:n], refs[n:2 * n]
        send_sems, recv_sems, local_sems = refs[2 * n:]
        ax, ay, ac = _mesh_pos()
        me, sibling = (ax, ay, ac), (ax, ay, 1 - ac)
        chips = [(1 - ax, ay), (ax, 1 - ay), (1 - ax, 1 - ay)]
        _handshake([sibling] + [(*chip, ac) for chip in chips])

        def slot(i, px, py, pc):
            return outs[i].at[4 * px + 2 * py + pc]

        def copy(i, k, block, to, src=None):
            return pltpu.make_async_remote_copy(
                src_ref=slot(i, *block) if src is None else src, dst_ref=slot(i, *block),
                send_sem=send_sems.at[7 * i + k], recv_sem=recv_sems.at[7 * i + k], device_id=to, device_id_type=MESH_ID)

        mine = [pltpu.make_async_copy(srcs[i], slot(i, *me), local_sems.at[i]) for i in range(n)]
        first = []
        for i in range(n):
            first += [copy(i, 1 + j, me, (*chip, ac), src=srcs[i]) for j, chip in enumerate(chips)]
        for i in range(n):
            first += [copy(i, 0, me, sibling, src=srcs[i])]
        for cp in first + mine:
            cp.start()
        passed = []
        for i in range(n):
            for j, chip in enumerate(chips):
                copy(i, 1 + j, (*chip, ac), me).wait_recv()
                passed.append(copy(i, 4 + j, (*chip, ac), sibling))
                passed[-1].start()
        for i in range(n):
            copy(i, 0, sibling, me).wait_recv()
            for j, chip in enumerate(chips):
                copy(i, 4 + j, (*chip, 1 - ac), me).wait_recv()
        for cp in first + passed:
            cp.wait_send()
        for cp in mine:
            cp.wait()

    return pl.kernel(
        body, name=name, out_type=[_sds((NDEV,) + x.shape, x.dtype) for x in xs],
        mesh=plsc.ScalarSubcoreMesh(**SEQUENCER),
        scratch_types=[pltpu.SemaphoreType.DMA((7 * n,)), pltpu.SemaphoreType.DMA((7 * n,)), pltpu.SemaphoreType.DMA((n,))],
        compiler_params=pltpu.CompilerParams(collective_id=GATHER_ID),
    )(*xs)


def _exchange_sc(gs, name):
    n = len(gs)

    def body(*refs):
        srcs, outs = refs[:n], refs[n:2 * n]
        send_sems, recv_sems, local_sems = refs[2 * n:]
        ax, ay, ac = _mesh_pos()
        me = 4 * ax + 2 * ay + ac
        peers = [_peer(ax, ay, ac, k) for k in range(1, NDEV)]
        _handshake(peers)
        mine = [pltpu.make_async_copy(srcs[i].at[me], outs[i].at[me], local_sems.at[i]) for i in range(n)]
        copies = []
        for i in range(n):
            for k, (px, py, pc) in enumerate(peers):
                copies.append(pltpu.make_async_remote_copy(
                    src_ref=srcs[i].at[4 * px + 2 * py + pc], dst_ref=outs[i].at[me],
                    send_sem=send_sems.at[7 * i + k], recv_sem=recv_sems.at[7 * i + k],
                    device_id=(px, py, pc), device_id_type=MESH_ID))
        for cp in copies + mine:
            cp.start()
        for cp in copies + mine:
            cp.wait()

    return pl.kernel(
        body, name=name, out_type=[_sds(g.shape, g.dtype) for g in gs],
        mesh=plsc.ScalarSubcoreMesh(**SEQUENCER),
        scratch_types=[pltpu.SemaphoreType.DMA((7 * n,)), pltpu.SemaphoreType.DMA((7 * n,)), pltpu.SemaphoreType.DMA((n,))],
        compiler_params=pltpu.CompilerParams(collective_id=EXCHANGE_ID),
    )(*gs)


def _row_tile(rows, cols):
    cap = max(8, (1 << 18) // cols)
    if rows <= cap:
        return rows
    best = None
    for t in range(8, cap + 1, 8):
        if rows % t == 0:
            best = t
    assert best is not None, (rows, cols)
    return best


def _sum_parts(parts, name):
    npart, rows, cols = parts.shape
    tr = _row_tile(rows, cols)

    def body(p_ref, o_ref):
        g = p_ref[0].astype(f32)
        for j in range(1, npart):
            g = g + p_ref[j].astype(f32)
        o_ref[...] = g

    return pl.pallas_call(
        body, name=name, grid=(rows // tr,),
        in_specs=[pl.BlockSpec((npart, tr, cols), lambda i: (0, i, 0))], out_specs=pl.BlockSpec((tr, cols), lambda i: (i, 0)),
        out_shape=_sds((rows, cols), f32), compiler_params=_params(("parallel",)),
    )(parts)


def _adamw(parts, w, m, v, name):
    npart, rows, cols = parts.shape
    tr = _row_tile(rows, cols)
    c1 = 1.0 / (1.0 - ADAM_B1 ** ADAM_STEP)
    c2 = 1.0 / (1.0 - ADAM_B2 ** ADAM_STEP)

    def body(p_ref, w_ref, m_ref, v_ref, g_ref, d_ref, mo_ref, vo_ref):
        g = p_ref[0].astype(f32)
        for j in range(1, npart):
            g = g + p_ref[j].astype(f32)
        mn = ADAM_B1 * m_ref[...] + (1.0 - ADAM_B1) * g
        vn = ADAM_B2 * v_ref[...] + (1.0 - ADAM_B2) * (g * g)
        g_ref[...] = g
        mo_ref[...] = mn
        vo_ref[...] = vn
        d_ref[...] = (-ADAM_LR) * ((mn * c1) / (jnp.sqrt(vn * c2) + ADAM_EPS) + ADAM_WD * w_ref[...])

    blk = pl.BlockSpec((tr, cols), lambda i: (i, 0))
    return pl.pallas_call(
        body, name=name, grid=(rows // tr,),
        in_specs=[pl.BlockSpec((npart, tr, cols), lambda i: (0, i, 0)), blk, blk, blk], out_specs=[blk] * 4,
        out_shape=[_sds((rows, cols), f32)] * 4, compiler_params=_params(("parallel",)),
    )(parts, w, m, v)


def _adamw_layer(parts, w, m, v, acc, l, name):
    npart, rows, cols = parts.shape
    tr = _row_tile(rows, cols)
    c1 = 1.0 / (1.0 - ADAM_B1 ** ADAM_STEP)
    c2 = 1.0 / (1.0 - ADAM_B2 ** ADAM_STEP)

    def body(p_ref, w_ref, m_ref, v_ref, *refs):
        g_ref, d_ref, mo_ref, vo_ref = refs[-4:]
        g = p_ref[0].astype(f32)
        for j in range(1, npart):
            g = g + p_ref[j].astype(f32)
        mn = ADAM_B1 * m_ref[...] + (1.0 - ADAM_B1) * g
        vn = ADAM_B2 * v_ref[...] + (1.0 - ADAM_B2) * (g * g)
        g_ref[...] = g
        mo_ref[...] = mn
        vo_ref[...] = vn
        d_ref[...] = (-ADAM_LR) * ((mn * c1) / (jnp.sqrt(vn * c2) + ADAM_EPS) + ADAM_WD * w_ref[...])

    blk = pl.BlockSpec((None, tr, cols), lambda i: (l, i, 0))
    prev = [] if acc is None else list(acc)
    return pl.pallas_call(
        body, name=name, grid=(rows // tr,),
        in_specs=[pl.BlockSpec((npart, tr, cols), lambda i: (0, i, 0)), blk, blk, blk] + [HBM_ANY] * len(prev),
        out_specs=[blk] * 4, out_shape=[_sds(w.shape, f32)] * 4,
        input_output_aliases={4 + j: j for j in range(len(prev))},
        compiler_params=_params(("parallel",)),
    )(parts, w, m, v, *prev)


def _pack(arrays):
    rows = []
    for a in arrays:
        flat = a.reshape(-1)
        pad = (-flat.shape[0]) % 1024
        rows.append(jnp.concatenate([flat, jnp.zeros((pad,), flat.dtype)]).reshape(-1, 128))
    return jnp.concatenate(rows, axis=0)


def _unpack(flat, shapes):
    out, r = [], 0
    for s in shapes:
        n = math.prod(s)
        nr = (n + 1023) // 1024 * 8
        out.append(flat[r:r + nr].reshape(-1)[:n].reshape(s))
        r += nr
    return out


BIG_SHARD_AXIS = dict(w_in=2, w_branch=3, w_out=1, w_ffn_in=2, w_ffn_out=1)
SHARDED_SMALL = ("attn_rel_bias", "lru_conv_w")


def _to_blocks(full, axis):
    s = full.shape
    cut = full.reshape(s[:axis] + (NDEV, s[axis] // NDEV) + s[axis + 1:])
    return jnp.moveaxis(cut, axis, 0)


def _from_blocks(blocks, axis):
    moved = jnp.moveaxis(blocks, 0, axis)
    s = moved.shape
    return moved.reshape(s[:axis] + (s[axis] * s[axis + 1],) + s[axis + 2:])


def _flat2(a):
    return a.reshape(-1, a.shape[-1])


def _my_slice(a, n):
    ax, ay, ac = _mesh_pos()
    return lax.dynamic_slice_in_dim(a, (4 * ax + 2 * ay + ac) * n, n, axis=a.ndim - 1)


_WEIGHTS = ("norm_mix_pre", "norm_mix_post", "norm_ffn_pre", "norm_ffn_post", "w_in", "attn_rel_bias", "hgrn_lb_logits",
            "hgrn_norm_g", "gmlp_norm_g", "gmlp_ws", "gmlp_bs", "lru_conv_w", "lru_conv_b", "lru_wa", "lru_ba", "lru_wx",
            "lru_bx", "lru_lambda", "w_branch", "w_out", "w_ffn_in", "w_ffn_out")


def _step(x, loss_target, w, m, v):
    gathered = []
    for l in range(DEPTH):
        gathered.append(tuple(_gather_sc([w[k][l].astype(bf16) for k in keys], "gather_%s%d" % (half, l))
                              for half, keys in (("mix", MIX_BIG), ("ffn", FFN_BIG))))
    cut = jnp.concatenate([w[k] for k in SHARDED_SMALL], axis=-1)
    parts = _all_gather(_pack([cut]), "gather_small").reshape(NDEV, -1)[:, :math.prod(cut.shape)].reshape((NDEV,) + cut.shape)
    small = {k: w[k] for k in SMALL if k not in SHARDED_SMALL}
    at = 0
    for k in SHARDED_SMALL:
        n = w[k].shape[-1]
        small[k] = _from_blocks(parts[..., at:at + n], 2)
        at += n
    loss, dx, layers = _step_forward(x, loss_target, gathered, small)
    flat3 = lambda a: a.reshape((DEPTH, -1, a.shape[-1]))
    acc = {k: None for k in BIG}
    smalls = [None] * DEPTH

    def send(grads, keys, dx, name):
        blocks = [_to_blocks(grads[k], BIG_SHARD_AXIS[k] - 1) for k in keys]
        blocks, dx = lax.optimization_barrier((blocks, dx))
        return dict(zip(keys, _exchange_sc(blocks, name))), dx

    def update(got, l, after):
        if after is not None:
            got, _ = lax.optimization_barrier((got, after))
        for k, g in got.items():
            w3 = flat3(w[k])
            acc[k] = _adamw_layer(g.reshape((NDEV,) + w3.shape[1:]), w3, flat3(m[k]), flat3(v[k]), acc[k], l,
                                  "adamw_%s_%d" % (k, l))

    waiting = []
    for l in range(DEPTH - 1, -1, -1):
        got_ffn = {}

        def ffn_grads_ready(grads, dx1, l=l, got_ffn=got_ffn):
            part, dx1 = send(grads, FFN_BIG, dx1, "exchange_ffn%d" % l)
            got_ffn.update(part)
            while waiting:
                update(*waiting.pop(), dx1)
            return dx1

        dx, gbig, smalls[l] = _step_backward(dx, layers[l], ffn_grads_ready)
        got_mix, dx = send(gbig, MIX_BIG, dx, "exchange_mix%d" % l)
        update(got_ffn, l, dx)
        waiting.append((got_mix, l))
    update(*waiting.pop(), None)
    grads, deltas, new_m, new_v = {}, {}, {}, {}
    for k in BIG:
        grads[k], deltas[k], new_m[k], new_v[k] = (o.reshape(w[k].shape) for o in acc[k])
    gsmall = {k: jnp.stack([smalls[l][k] for l in range(DEPTH)]) for k in smalls[0]}
    gsmall["hgrn_lb_logits"] = _lb_bwd(small["hgrn_lb_logits"], gsmall.pop("lb"))
    shapes = [gsmall[k].shape for k in SMALL]
    sums = _unpack(_sum_parts(_all_gather(_pack([gsmall[k] for k in SMALL]), "gather_small_grads"), "sum_small_grads"), shapes)
    gs = dict(zip(SMALL, sums))
    for k in SHARDED_SMALL:
        gs[k] = _my_slice(gs[k], w[k].shape[-1])
    packed = [_pack([d[k] for k in SMALL]) for d in (gs, w, m, v)]
    outs = _adamw(packed[0][None], packed[1], packed[2], packed[3], "adamw_small")
    shapes = [w[k].shape for k in SMALL]
    for d, o in zip((grads, deltas, new_m, new_v), outs):
        d.update(zip(SMALL, _unpack(o, shapes)))
    total = lax.psum(loss[0, 0], ("x", "y", "c"))
    return total, dx[None], grads, deltas, new_m, new_v


def _step_forward(x, loss_target, gathered, small):
    lbs = _lb_fwd(small["hgrn_lb_logits"])
    x = x[0]
    layers = []

    def weights(blocks, keys, after):
        if after is not None:
            blocks, _ = lax.optimization_barrier((blocks, after))
        return {k: _from_blocks(g, BIG_SHARD_AXIS[k] - 1) for k, g in zip(keys, blocks)}

    for l in range(DEPTH):
        mix, ffn = gathered[l]
        p = _layer_params(l, weights(mix, MIX_BIG, x if l else None), small, lbs)
        x, sv = _layer_fwd(x, p, lambda x1, ffn=ffn: _ffn_weights(weights(ffn, FFN_BIG, x1)))
        layers.append((p, sv))
    loss, dx = _loss_head(x, loss_target[0])
    return loss, dx, layers


def _step_backward(dx, layer, ffn_grads_ready):
    return _layer_bwd(dx, *layer, ffn_grads_ready)


def kernel(x, norm_mix_pre, norm_mix_post, norm_ffn_pre, norm_ffn_post, w_in, attn_rel_bias, hgrn_lb_logits, hgrn_norm_g, gmlp_norm_g, gmlp_ws, gmlp_bs, lru_conv_w, lru_conv_b, lru_wa, lru_ba, lru_wx, lru_bx, lru_lambda, w_branch, w_out, w_ffn_in, w_ffn_out, loss_target, m_norm_mix_pre, m_norm_mix_post, m_norm_ffn_pre, m_norm_ffn_post, m_w_in, m_attn_rel_bias, m_hgrn_lb_logits, m_hgrn_norm_g, m_gmlp_norm_g, m_gmlp_ws, m_gmlp_bs, m_lru_conv_w, m_lru_conv_b, m_lru_wa, m_lru_ba, m_lru_wx, m_lru_bx, m_lru_lambda, m_w_branch, m_w_out, m_w_ffn_in, m_w_ffn_out, v_norm_mix_pre, v_norm_mix_post, v_norm_ffn_pre, v_norm_ffn_post, v_w_in, v_attn_rel_bias, v_hgrn_lb_logits, v_hgrn_norm_g, v_gmlp_norm_g, v_gmlp_ws, v_gmlp_bs, v_lru_conv_w, v_lru_conv_b, v_lru_wa, v_lru_ba, v_lru_wx, v_lru_bx, v_lru_lambda, v_w_branch, v_w_out, v_w_ffn_in, v_w_ffn_out):
    w = dict(zip(_WEIGHTS, (norm_mix_pre, norm_mix_post, norm_ffn_pre, norm_ffn_post, w_in, attn_rel_bias, hgrn_lb_logits, hgrn_norm_g, gmlp_norm_g, gmlp_ws, gmlp_bs, lru_conv_w, lru_conv_b, lru_wa, lru_ba, lru_wx, lru_bx, lru_lambda, w_branch, w_out, w_ffn_in, w_ffn_out)))
    m = dict(zip(_WEIGHTS, (m_norm_mix_pre, m_norm_mix_post, m_norm_ffn_pre, m_norm_ffn_post, m_w_in, m_attn_rel_bias, m_hgrn_lb_logits, m_hgrn_norm_g, m_gmlp_norm_g, m_gmlp_ws, m_gmlp_bs, m_lru_conv_w, m_lru_conv_b, m_lru_wa, m_lru_ba, m_lru_wx, m_lru_bx, m_lru_lambda, m_w_branch, m_w_out, m_w_ffn_in, m_w_ffn_out)))
    v = dict(zip(_WEIGHTS, (v_norm_mix_pre, v_norm_mix_post, v_norm_ffn_pre, v_norm_ffn_post, v_w_in, v_attn_rel_bias, v_hgrn_lb_logits, v_hgrn_norm_g, v_gmlp_norm_g, v_gmlp_ws, v_gmlp_bs, v_lru_conv_w, v_lru_conv_b, v_lru_wa, v_lru_ba, v_lru_wx, v_lru_bx, v_lru_lambda, v_w_branch, v_w_out, v_w_ffn_in, v_w_ffn_out)))
    loss, grad_x, grads, deltas, new_m, new_v = _step(x, loss_target, w, m, v)
    return (loss, grad_x, *[grads[k] for k in _WEIGHTS], *[deltas[k] for k in _WEIGHTS],
            *[new_m[k] for k in _WEIGHTS], *[new_v[k] for k in _WEIGHTS])
```

```python
import math

import jax
import jax.numpy as jnp
from jax import lax
from jax.experimental import pallas as pl
from jax.experimental.pallas import tpu as pltpu
from jax.experimental.pallas import tpu_sc as plsc

f32 = jnp.float32
bf16 = jnp.bfloat16

SEQ = 2048
DM = 1024
DEPTH = 4
NDEV = 8
MIXW = 256
NHEAD = 4
HDIM = 64
NMIX = 11 * MIXW
NGATE = 4 * DM
FFH = 2816
EPS = 1e-6
NEG_BIG = -1e30
LOG_FLOOR = 1e-30
LRU_C = 8.0
REL_SIZE = 320
ATT_PAIR = 128
ATT_BAND = 640
ATT_PAD = 512
ATT_WV = 768
HG_T = 16
HG_N = SEQ // HG_T
GM_T = 128
LRU_T = 128
ADAM_LR, ADAM_B1, ADAM_B2, ADAM_EPS, ADAM_WD, ADAM_STEP = 0.001, 0.9, 0.999, 1e-8, 0.01, 10
V7X_VMEM_LIMIT = 56 * 1024 * 1024
GELU_C0 = math.sqrt(2.0 / math.pi)
GELU_C1 = 0.044715
MESH_ID = pl.DeviceIdType.MESH


def _params(sem=None):
    if sem is None:
        return pltpu.CompilerParams(vmem_limit_bytes=V7X_VMEM_LIMIT)
    return pltpu.CompilerParams(dimension_semantics=sem, vmem_limit_bytes=V7X_VMEM_LIMIT)


def _sds(shape, dtype):
    return jax.ShapeDtypeStruct(shape, dtype)


def _dot(a, b):
    return jnp.dot(a.astype(bf16), b.astype(bf16), preferred_element_type=f32)


def _dot_nt(a, b):
    return lax.dot_general(a.astype(bf16), b.astype(bf16), (((1,), (1,)), ((), ())), preferred_element_type=f32)


def _dot_tn(a, b):
    return lax.dot_general(a.astype(bf16), b.astype(bf16), (((0,), (0,)), ((), ())), preferred_element_type=f32)


def _split(a):
    hi = a.astype(bf16)
    lo = (a - hi.astype(f32)).astype(bf16)
    return hi, lo


def _dot_hl(a, m):
    hi, lo = _split(a)
    return jnp.dot(hi, m, preferred_element_type=f32) + jnp.dot(lo, m, preferred_element_type=f32)


def _dot_nt_hl(m, a):
    hi, lo = _split(a)
    dn = (((1,), (1,)), ((), ()))
    return lax.dot_general(m, hi, dn, preferred_element_type=f32) + lax.dot_general(m, lo, dn, preferred_element_type=f32)


def _sigmoid(x):
    return jax.nn.sigmoid(x)


def _silu(x):
    return x * _sigmoid(x)


def _dsilu(x):
    s = _sigmoid(x)
    return s * (1.0 + x * (1.0 - s))


def _gelu(x):
    return 0.5 * x * (1.0 + jnp.tanh(GELU_C0 * (x + GELU_C1 * x * x * x)))


def _dgelu(x):
    t = jnp.tanh(GELU_C0 * (x + GELU_C1 * x * x * x))
    return 0.5 * (1.0 + t) + 0.5 * x * (1.0 - t * t) * GELU_C0 * (1.0 + 3.0 * GELU_C1 * x * x)


def _rms(x, g):
    r = lax.rsqrt(jnp.mean(x * x, axis=-1, keepdims=True) + EPS)
    return x * r * g


def _rms_bwd(x, g, dy):
    r = lax.rsqrt(jnp.mean(x * x, axis=-1, keepdims=True) + EPS)
    xh = x * r
    dxh = dy * g
    dx = r * (dxh - xh * jnp.mean(dxh * xh, axis=-1, keepdims=True))
    return dx, jnp.sum(dy * xh, axis=0, keepdims=True)


def _same_head(n, width, dtype):
    r = lax.broadcasted_iota(jnp.int32, (n, n), 0) // width
    c = lax.broadcasted_iota(jnp.int32, (n, n), 1) // width
    return (r == c).astype(dtype)


def _head_masks(rows=1):
    lane = lax.broadcasted_iota(jnp.int32, (rows, MIXW), 1) // HDIM
    return [lane == h for h in range(NHEAD)]


def _norm_matmul(x, g, w, tn):
    n = w.shape[1]
    tm = 1024

    def body(x_ref, g_ref, w_ref, z_ref, h_ref):
        @pl.when(pl.program_id(1) == 0)
        def _():
            h_ref[...] = _rms(x_ref[...], g_ref[...]).astype(bf16)

        z_ref[...] = jnp.dot(h_ref[...], w_ref[...], preferred_element_type=f32)

    return pl.pallas_call(
        body, name="norm_matmul", grid=(SEQ // tm, n // tn),
        in_specs=[pl.BlockSpec((tm, DM), lambda i, j: (i, 0)), pl.BlockSpec((1, DM), lambda i, j: (0, 0)),
                  pl.BlockSpec((DM, tn), lambda i, j: (0, j))],
        out_specs=[pl.BlockSpec((tm, tn), lambda i, j: (i, j)), pl.BlockSpec((tm, DM), lambda i, j: (i, 0))],
        out_shape=[_sds((SEQ, n), f32), _sds((SEQ, DM), bf16)],
        compiler_params=_params(("parallel", "arbitrary")),
    )(x, g, w)


def _matmul(a, w, tn):
    k, n = w.shape
    tm = 1024

    def body(a_ref, w_ref, z_ref):
        z_ref[...] = jnp.dot(a_ref[...], w_ref[...], preferred_element_type=f32)

    return pl.pallas_call(
        body, name="matmul", grid=(SEQ // tm, n // tn),
        in_specs=[pl.BlockSpec((tm, k), lambda i, j: (i, 0)), pl.BlockSpec((k, tn), lambda i, j: (0, j))],
        out_specs=pl.BlockSpec((tm, tn), lambda i, j: (i, j)),
        out_shape=_sds((SEQ, n), f32),
        compiler_params=_params(("parallel", "arbitrary")),
    )(a, w)


def _att_offset_map():
    i = lax.broadcasted_iota(jnp.int32, (REL_SIZE, ATT_WV), 0)
    t = lax.broadcasted_iota(jnp.int32, (REL_SIZE, ATT_WV), 1)
    e = jnp.where(t <= ATT_BAND, t, t - ATT_WV)
    idx = jnp.clip(ATT_PAD - e, -(HDIM - 1), 256) + (HDIM - 1)
    return (idx == i).astype(bf16)


def _att_band_valid():
    qc = lax.broadcasted_iota(jnp.int32, (ATT_PAIR, ATT_BAND), 0) // HDIM
    kc = lax.broadcasted_iota(jnp.int32, (ATT_PAIR, ATT_BAND), 1) // HDIM
    return (kc >= qc) & (kc <= qc + 8)


def _att_bias_tiles(rb_ref, bm_ref):
    wv = _dot_hl(rb_ref[...], _att_offset_map())
    valid = _att_band_valid()
    for h in range(NHEAD):
        rows = jnp.broadcast_to(wv[h:h + 1, :], (ATT_PAIR, ATT_WV))
        tile = pltpu.roll(rows, 0, 1, stride=1, stride_axis=0)[:, :ATT_BAND]
        bm_ref[h] = jnp.where(valid, tile, NEG_BIG)


def _att_pad_kv(k_ref, v_ref, kp_ref, vp_ref):
    kp_ref[pl.ds(0, ATT_PAD), :] = jnp.zeros((ATT_PAD, MIXW), bf16)
    vp_ref[pl.ds(0, ATT_PAD), :] = jnp.zeros((ATT_PAD, MIXW), bf16)
    kp_ref[pl.ds(ATT_PAD, SEQ), :] = k_ref[...].astype(bf16)
    vp_ref[pl.ds(ATT_PAD, SEQ), :] = v_ref[...].astype(bf16)


def _att_probs(qm, kb, bm, key_ok):
    s = _dot_nt(qm, kb) + bm
    s = jnp.where(key_ok, s, NEG_BIG)
    m = jnp.max(s, axis=-1, keepdims=True)
    e = jnp.exp(s - m)
    return e / jnp.sum(e, axis=-1, keepdims=True)


def _attn_fwd(zm, rb8):
    def body(q_ref, k_ref, v_ref, rb_ref, o_ref, kp_ref, vp_ref, bm_ref):
        _att_pad_kv(k_ref, v_ref, kp_ref, vp_ref)
        _att_bias_tiles(rb_ref, bm_ref)
        hm = _head_masks()

        def pair(p, carry):
            r0 = pl.multiple_of(p * ATT_PAIR, ATT_PAIR)
            q = q_ref[pl.ds(r0, ATT_PAIR), :] * (HDIM ** -0.5)
            kb = kp_ref[pl.ds(r0, ATT_BAND), :]
            vb = vp_ref[pl.ds(r0, ATT_BAND), :]
            key_ok = (lax.broadcasted_iota(jnp.int32, (1, ATT_BAND), 1) + (r0 - ATT_PAD)) >= 0
            o = jnp.zeros((ATT_PAIR, MIXW), f32)
            for h in range(NHEAD):
                qm = jnp.where(hm[h], q, 0.0)
                p_h = _att_probs(qm, kb, bm_ref[h], key_ok)
                o = o + jnp.where(hm[h], _dot(p_h, vb), 0.0)
            o_ref[pl.ds(r0, ATT_PAIR), :] = o.astype(bf16)
            return carry

        lax.fori_loop(0, SEQ // ATT_PAIR, pair, 0)

    col = lambda j: pl.BlockSpec((SEQ, MIXW), lambda i: (0, j))
    return pl.pallas_call(
        body, name="attn_fwd", grid=(1,),
        in_specs=[col(0), col(1), col(2), pl.BlockSpec((8, REL_SIZE), lambda i: (0, 0))],
        out_specs=pl.BlockSpec((SEQ, MIXW), lambda i: (0, 0)),
        out_shape=_sds((SEQ, MIXW), bf16),
        scratch_shapes=[pltpu.VMEM((SEQ + ATT_PAD, MIXW), bf16), pltpu.VMEM((SEQ + ATT_PAD, MIXW), bf16),
                        pltpu.VMEM((NHEAD, ATT_PAIR, ATT_BAND), f32)],
        compiler_params=_params(("arbitrary",)),
    )(zm, zm, zm, rb8)


def _hg_gates(q, fz, lb):
    sq = _sigmoid(q)
    sg = _sigmoid(fz)
    f = lb + (1.0 - lb) * sg
    return q * sq, (1.0 - lb) * (1.0 - sg), jnp.log(jnp.maximum(f, LOG_FLOOR)), sq, sg, f


def _hg_prepare(q_ref, f_ref, lb, qf_s, kf_s, b_s, qd_s, kd_s, dec_s):
    b = None
    for t in range(HG_T):
        qf, kf, lf, _, _, _ = _hg_gates(q_ref[:, t, :], f_ref[:, t, :], lb)
        b = lf if b is None else b + lf
        qf_s[:, t, :] = qf
        kf_s[:, t, :] = kf
        b_s[:, t, :] = b
    b_last = b
    dec_s[...] = jnp.broadcast_to(jnp.exp(b_last)[:, None, :], (HG_N, 8, MIXW))
    for t in range(HG_T):
        bt = b_s[:, t, :]
        qd_s[:, t, :] = qf_s[:, t, :] * jnp.exp(bt)
        kd_s[:, t, :] = kf_s[:, t, :] * jnp.exp(b_last - bt)


def _hg_scores(t, qf_s, kf_s, b_s, w_s, hm):
    qt = qf_s[:, t, :]
    bt = b_s[:, t, :]
    for s in range(t + 1):
        w = qt * kf_s[:, s, :]
        if s < t:
            w = w * jnp.exp(bt - b_s[:, s, :])
        w_s[pl.ds(s * HG_N, HG_N), :] = w.astype(bf16)
    return jnp.dot(w_s[pl.ds(0, (t + 1) * HG_N), :], hm, preferred_element_type=f32)


def _hgrn_fwd(zm3, lb, ng):
    def body(q_ref, f_ref, i_ref, g_ref, lb_ref, ng_ref, o_ref, oraw_ref, states_ref,
             qf_s, kf_s, b_s, qd_s, kd_s, dec_s, w_s, st_s):
        lb = lb_ref[...]
        hm = _same_head(MIXW, HDIM, bf16)
        hmf = _same_head(MIXW, HDIM, f32)
        _hg_prepare(q_ref, f_ref, lb, qf_s, kf_s, b_s, qd_s, kd_s, dec_s)
        for t in range(HG_T):
            p = _hg_scores(t, qf_s, kf_s, b_s, w_s, hm)
            acc = jnp.zeros((HG_N, MIXW), f32)
            for s in range(t + 1):
                acc = acc + p[s * HG_N:(s + 1) * HG_N] * i_ref[:, s, :]
            oraw_ref[:, t, :] = acc
        st_s[...] = jnp.zeros((MIXW, MIXW), f32)

        def step(n, carry):
            st = st_s[...]
            stb = st.astype(bf16)
            states_ref[n] = stb
            oraw_ref[n] = oraw_ref[n] + _dot_nt(qd_s[n], stb)
            st_s[...] = st * dec_s[n][0:1] + _dot_tn(i_ref[n], kd_s[n]) * hmf
            return carry

        lax.fori_loop(0, HG_N, step, 0)
        ngv = ng_ref[...]
        for t in range(HG_T):
            o = oraw_ref[:, t, :]
            ms = _dot_hl(o * o, hm) * (1.0 / HDIM)
            o_ref[:, t, :] = (o * lax.rsqrt(ms + EPS) * ngv * _silu(g_ref[:, t, :])).astype(bf16)

    one = pl.Buffered(1)
    col = lambda j: pl.BlockSpec((HG_N, HG_T, MIXW), lambda i: (0, 0, j), pipeline_mode=one)
    vec = pl.BlockSpec((1, MIXW), lambda i: (0, 0))
    blk = pl.BlockSpec((HG_N, HG_T, MIXW), lambda i: (0, 0, 0))
    s3 = pltpu.VMEM((HG_N, HG_T, MIXW), f32)
    return pl.pallas_call(
        body, name="hgrn_fwd", grid=(1,),
        in_specs=[col(3), col(4), col(5), col(6), vec, vec],
        out_specs=[blk, blk, pl.BlockSpec((HG_N, MIXW, MIXW), lambda i: (0, 0, 0), pipeline_mode=one)],
        out_shape=[_sds((HG_N, HG_T, MIXW), bf16), _sds((HG_N, HG_T, MIXW), f32), _sds((HG_N, MIXW, MIXW), bf16)],
        scratch_shapes=[s3, s3, s3, s3, s3, pltpu.VMEM((HG_N, 8, MIXW), f32),
                        pltpu.VMEM((HG_T * HG_N, MIXW), bf16), pltpu.VMEM((MIXW, MIXW), f32)],
        compiler_params=_params(("arbitrary",)),
    )(zm3, zm3, zm3, zm3, lb, ng)


def _gm_weights(ws_ref):
    tril = lax.broadcasted_iota(jnp.int32, (GM_T, GM_T), 0) >= lax.broadcasted_iota(jnp.int32, (GM_T, GM_T), 1)
    return tril, [jnp.where(tril, ws_ref[g], 0.0).astype(bf16) for g in range(NHEAD)]


def _gm_expand():
    r = lax.broadcasted_iota(jnp.int32, (8, MIXW), 0)
    c = lax.broadcasted_iota(jnp.int32, (8, MIXW), 1) // HDIM
    return (r == c).astype(bf16)


def _gm_mixed(vn, wts, bias, hm):
    vb = vn.astype(bf16)
    mixed = bias
    for g in range(NHEAD):
        mixed = mixed + jnp.where(hm[g], jnp.dot(wts[g], vb, preferred_element_type=f32), 0.0)
    return mixed


def _gm_bias(bs_ref):
    hi, lo = _split(bs_ref[...])
    et = _gm_expand()
    dn = (((0,), (0,)), ((), ()))
    return lax.dot_general(hi, et, dn, preferred_element_type=f32) + lax.dot_general(lo, et, dn, preferred_element_type=f32)


def _gmlp_fwd(zm, ng, ws, bs8):
    def body(u_ref, v_ref, ng_ref, ws_ref, bs_ref, o_ref):
        hm = _head_masks()
        _, wts = _gm_weights(ws_ref)
        bias = _gm_bias(bs_ref)
        ngv = ng_ref[...]

        def blk(n, carry):
            rows = pl.ds(pl.multiple_of(n * GM_T, GM_T), GM_T)
            vn = _rms(_gelu(v_ref[rows, :]), ngv)
            o_ref[rows, :] = (_gelu(u_ref[rows, :]) * _gm_mixed(vn, wts, bias, hm)).astype(bf16)
            return carry

        lax.fori_loop(0, SEQ // GM_T, blk, 0)

    col = lambda j: pl.BlockSpec((SEQ, MIXW), lambda i: (0, j))
    return pl.pallas_call(
        body, name="gmlp_fwd", grid=(1,),
        in_specs=[col(7), col(8), pl.BlockSpec((1, MIXW), lambda i: (0, 0)),
                  pl.BlockSpec((NHEAD, GM_T, GM_T), lambda i: (0, 0, 0)), pl.BlockSpec((8, GM_T), lambda i: (0, 0))],
        out_specs=pl.BlockSpec((SEQ, MIXW), lambda i: (0, 0)),
        out_shape=_sds((SEQ, MIXW), bf16),
        compiler_params=_params(("arbitrary",)),
    )(zm, zm, ng, ws, bs8)


def _lru_conv(x_ref, cw_ref, cb_ref, xp_s, xc_s):
    xp_s[pl.ds(0, 8), :] = jnp.zeros((8, MIXW), f32)
    xp_s[pl.ds(8, SEQ), :] = x_ref[...]
    cw = cw_ref[...]
    xc = cb_ref[...] + x_ref[...] * cw[3:4]
    for k in range(1, 4):
        xc = xc + xp_s[pl.ds(8 - k, SEQ), :] * cw[3 - k:4 - k]
    xc_s[...] = xc


def _lru_gates(xc, wa, ba, wx, bx, sp, first_row):
    r = _sigmoid(_dot(xc, wa) + ba)
    ig = _sigmoid(_dot(xc, wx) + bx)
    la = (-LRU_C) * r * sp
    a = jnp.exp(la)
    th = jnp.tanh(la)
    m2 = -2.0 * th / (1.0 - th)
    mult = jnp.where(first_row, 1.0, jnp.sqrt(jnp.maximum(m2, 0.0)))
    return a, mult, r, ig, m2


def _lru_scan(a, b, rev):
    row = lax.broadcasted_iota(jnp.int32, (LRU_T, 1), 0)
    k = 1
    while k < LRU_T:
        ok = (row < LRU_T - k) if rev else (row >= k)
        sh = (LRU_T - k) if rev else k
        a_sh = jnp.where(ok, pltpu.roll(a, sh, 0), 1.0)
        b_sh = jnp.where(ok, pltpu.roll(b, sh, 0), 0.0)
        b = b + a * b_sh
        a = a * a_sh
        k *= 2
    return a, b


def _lru_fwd(zm, cw8, cb, wa, ba, wx, bx, lam):
    def body(x_ref, g_ref, cw_ref, cb_ref, wa_ref, ba_ref, wx_ref, bx_ref, lam_ref, o_ref, h_ref, xp_s, xc_s):
        _lru_conv(x_ref, cw_ref, cb_ref, xp_s, xc_s)
        sp = jax.nn.softplus(-lam_ref[...])
        wa_v, wx_v, ba_v, bx_v = wa_ref[...], wx_ref[...], ba_ref[...], bx_ref[...]

        def chunk(c, h_prev):
            rows = pl.ds(pl.multiple_of(c * LRU_T, LRU_T), LRU_T)
            first = (lax.broadcasted_iota(jnp.int32, (LRU_T, 1), 0) + c * LRU_T) == 0
            xc = xc_s[rows, :]
            a, mult, _, ig, _ = _lru_gates(xc, wa_v, ba_v, wx_v, bx_v, sp, first)
            acum, hloc = _lru_scan(a, mult * (ig * xc), False)
            h = hloc + acum * h_prev
            h_ref[rows, :] = h
            o_ref[rows, :] = (h * _gelu(g_ref[rows, :])).astype(bf16)
            return h[LRU_T - 1:LRU_T, :]

        lax.fori_loop(0, SEQ // LRU_T, chunk, jnp.zeros((1, MIXW), f32))

    col = lambda j: pl.BlockSpec((SEQ, MIXW), lambda i: (0, j))
    vec = pl.BlockSpec((1, MIXW), lambda i: (0, 0))
    mat = pl.BlockSpec((MIXW, MIXW), lambda i: (0, 0))
    out = pl.BlockSpec((SEQ, MIXW), lambda i: (0, 0))
    return pl.pallas_call(
        body, name="lru_fwd", grid=(1,),
        in_specs=[col(9), col(10), pl.BlockSpec((8, MIXW), lambda i: (0, 0)), vec, mat, vec, mat, vec, vec],
        out_specs=[out, out],
        out_shape=[_sds((SEQ, MIXW), bf16), _sds((SEQ, MIXW), f32)],
        scratch_shapes=[pltpu.VMEM((SEQ + 8, MIXW), f32), pltpu.VMEM((SEQ, MIXW), f32)],
        compiler_params=_params(("arbitrary",)),
    )(zm, zm, cw8, cb, wa, ba, wx, bx, lam)


def _block_diag(w):
    out = jnp.zeros((MIXW, MIXW), w.dtype)
    for h in range(NHEAD):
        out = lax.dynamic_update_slice(out, w[h], (h * HDIM, h * HDIM))
    return out


def _diag_blocks(w):
    return jnp.stack([w[h * HDIM:(h + 1) * HDIM, h * HDIM:(h + 1) * HDIM] for h in range(NHEAD)])


ROW_TILE = 256


def _merge_fwd(outs, zg, wb, wo, x, g2):
    def body(oa_ref, ob_ref, oc_ref, od_ref, zg_ref, wb_ref, wo_ref, x_ref, g_ref, xo_ref, mg_ref, y_ref):
        merged = jnp.zeros((ROW_TILE, DM), f32)
        for n, o_ref in enumerate((oa_ref, ob_ref, oc_ref, od_ref)):
            proj = jnp.dot(o_ref[...], wb_ref[n], preferred_element_type=f32)
            merged = merged + _sigmoid(zg_ref[:, n * DM:(n + 1) * DM]) * proj
        mb = merged.astype(bf16)
        y = jnp.dot(mb, wo_ref[...], preferred_element_type=f32)
        mg_ref[...] = mb
        y_ref[...] = y
        xo_ref[...] = x_ref[...] + _rms(y, g_ref[...])

    row = lambda w: pl.BlockSpec((ROW_TILE, w), lambda i: (i, 0))
    return pl.pallas_call(
        body, name="merge_fwd", grid=(SEQ // ROW_TILE,),
        in_specs=[row(MIXW)] * 4 + [row(NGATE), pl.BlockSpec((NHEAD, MIXW, DM), lambda i: (0, 0, 0)),
                                    pl.BlockSpec((DM, DM), lambda i: (0, 0)), row(DM), pl.BlockSpec((1, DM), lambda i: (0, 0))],
        out_specs=[row(DM), row(DM), row(DM)],
        out_shape=[_sds((SEQ, DM), f32), _sds((SEQ, DM), bf16), _sds((SEQ, DM), f32)],
        compiler_params=_params(("parallel",)),
    )(*outs, zg, wb, wo, x, g2)


def _ffn_out(u, w2, x, g4):
    def body(u_ref, w_ref, x_ref, g_ref, xo_ref, f_ref):
        a = _silu(u_ref[:, :FFH]) * u_ref[:, FFH:]
        f = jnp.dot(a.astype(bf16), w_ref[...], preferred_element_type=f32)
        f_ref[...] = f
        xo_ref[...] = x_ref[...] + _rms(f, g_ref[...])

    row = lambda w: pl.BlockSpec((ROW_TILE, w), lambda i: (i, 0))
    return pl.pallas_call(
        body, name="ffn_out", grid=(SEQ // ROW_TILE,),
        in_specs=[row(2 * FFH), pl.BlockSpec((FFH, DM), lambda i: (0, 0)), row(DM), pl.BlockSpec((1, DM), lambda i: (0, 0))],
        out_specs=[row(DM), row(DM)],
        out_shape=[_sds((SEQ, DM), f32), _sds((SEQ, DM), f32)],
        compiler_params=_params(("parallel",)),
    )(u, w2, x, g4)


def _loss_head(x, tgt):
    tm = 512

    def body(x_ref, t_ref, l_ref, dx_ref):
        @pl.when(pl.program_id(0) == 0)
        def _():
            l_ref[...] = jnp.zeros((1, 1), f32)

        d = x_ref[...] - t_ref[...]
        dx_ref[...] = d * (1.0 / DM)
        l_ref[...] += (0.5 / DM) * jnp.sum(d * d).reshape(1, 1)

    row = pl.BlockSpec((tm, DM), lambda i: (i, 0))
    return pl.pallas_call(
        body, name="loss_head", grid=(SEQ // tm,),
        in_specs=[row, row], out_specs=[pl.BlockSpec((1, 1), lambda i: (0, 0)), row],
        out_shape=[_sds((1, 1), f32), _sds((SEQ, DM), f32)],
        compiler_params=_params(("arbitrary",)),
    )(x, tgt)


def _lb_fwd(logits):
    def body(lg_ref, o_ref):
        lg = lg_ref[...]
        e = jnp.exp(lg - jnp.max(lg, axis=0, keepdims=True))
        p = e / jnp.sum(e, axis=0, keepdims=True)
        acc = jnp.zeros((1, MIXW), f32)
        o_ref[0:1, :] = acc
        for l in range(1, DEPTH):
            acc = acc + p[l:l + 1]
            o_ref[l:l + 1, :] = acc

    return pl.pallas_call(body, name="lb_fwd", out_shape=_sds((DEPTH, MIXW), f32))(logits)


def _lb_bwd(logits, dlbs):
    def body(lg_ref, d_ref, o_ref):
        lg = lg_ref[...]
        e = jnp.exp(lg - jnp.max(lg, axis=0, keepdims=True))
        p = e / jnp.sum(e, axis=0, keepdims=True)
        d = d_ref[...]
        dp = [jnp.zeros((1, MIXW), f32)] * DEPTH
        acc = jnp.zeros((1, MIXW), f32)
        for j in range(DEPTH - 1, 0, -1):
            acc = acc + d[j:j + 1]
            dp[j] = acc
        inner = sum(p[j:j + 1] * dp[j] for j in range(DEPTH))
        for j in range(DEPTH):
            o_ref[j:j + 1, :] = p[j:j + 1] * (dp[j] - inner)

    return pl.pallas_call(body, name="lb_bwd", out_shape=_sds((DEPTH, MIXW), f32))(logits, dlbs)


def _pad_rows(a, rows=8):
    return jnp.concatenate([a, jnp.zeros((rows - a.shape[0], a.shape[1]), a.dtype)], axis=0)


def _layer_params(l, full, small, lbs):
    row = lambda name: small[name][l][None]
    return dict(
        _mix_weights(full), **(_ffn_weights(full) if "w_ffn_in" in full else {}),
        g1=row("norm_mix_pre"), g2=row("norm_mix_post"), g3=row("norm_ffn_pre"), g4=row("norm_ffn_post"),
        rb8=_pad_rows(small["attn_rel_bias"][l]), lb=lbs[l][None], hng=row("hgrn_norm_g"),
        gng=row("gmlp_norm_g"), gws=small["gmlp_ws"][l], gbs8=_pad_rows(small["gmlp_bs"][l]),
        cw8=_pad_rows(small["lru_conv_w"][l]), cb=row("lru_conv_b"),
        wa=_block_diag(small["lru_wa"][l]).astype(bf16), ba=row("lru_ba"),
        wx=_block_diag(small["lru_wx"][l]).astype(bf16), bx=row("lru_bx"), lam=row("lru_lambda"),
    )


def _mix_weights(full):
    return dict(wm=full["w_in"][:, :NMIX], wgt=full["w_in"][:, NMIX:], wb=full["w_branch"], wo=full["w_out"])


def _ffn_weights(full):
    return dict(w1=full["w_ffn_in"], w2=full["w_ffn_out"])


def _layer_fwd(x, p, late_ffn_weights=None):
    zm, h = _norm_matmul(x, p["g1"], p["wm"], 1408)
    zg = _matmul(h, p["wgt"], 1024)
    oa = _attn_fwd(zm, p["rb8"])
    ob3, obraw3, hstates = _hgrn_fwd(zm.reshape(HG_N, HG_T, NMIX), p["lb"], p["hng"])
    oc = _gmlp_fwd(zm, p["gng"], p["gws"], p["gbs8"])
    od, hd = _lru_fwd(zm, p["cw8"], p["cb"], p["wa"], p["ba"], p["wx"], p["bx"], p["lam"])
    outs = (oa, ob3.reshape(SEQ, MIXW), oc, od)
    x1, merged, y = _merge_fwd(outs, zg, p["wb"], p["wo"], x, p["g2"])
    if late_ffn_weights is not None:
        p.update(late_ffn_weights(x1))
    u, h2 = _norm_matmul(x1, p["g3"], p["w1"], 1408)
    x2, f = _ffn_out(u, p["w2"], x1, p["g4"])
    saved = dict(x=x, h=h, zm=zm, zg=zg, outs=outs, obraw3=obraw3, hstates=hstates, hd=hd, x1=x1, merged=merged, y=y, u=u, h2=h2, f=f)
    return x2, saved


def _att_bias_grad(db_ref, o_ref):
    r = lax.broadcasted_iota(jnp.int32, (ATT_PAIR, ATT_PAIR), 0)
    c = lax.broadcasted_iota(jnp.int32, (ATT_PAIR, ATT_PAIR), 1)
    flip = (r + c == ATT_PAIR - 1).astype(bf16)
    rows = []
    for h in range(NHEAD):
        d = jnp.concatenate([db_ref[h], jnp.zeros((ATT_PAIR, ATT_WV - ATT_BAND), f32)], axis=1)
        hi, lo = _split(d)
        rev = jnp.dot(flip, hi, preferred_element_type=f32) + jnp.dot(flip, lo, preferred_element_type=f32)
        lined = pltpu.roll(rev, ATT_WV - (ATT_PAIR - 1), 1, stride=1, stride_axis=0)
        rows.append(jnp.sum(lined, axis=0, keepdims=True))
    dwv = jnp.concatenate(rows + [jnp.zeros((8 - NHEAD, ATT_WV), f32)], axis=0)
    hi, lo = _split(dwv)
    m = _att_offset_map()
    dn = (((1,), (1,)), ((), ()))
    o_ref[...] = lax.dot_general(hi, m, dn, preferred_element_type=f32) + lax.dot_general(lo, m, dn, preferred_element_type=f32)


def _attn_bwd(zm, rb8, do):
    def body(q_ref, k_ref, v_ref, rb_ref, do_ref, dz_ref, drb_ref, kp_ref, vp_ref, bm_ref, dk_s, dv_s, db_s):
        _att_pad_kv(k_ref, v_ref, kp_ref, vp_ref)
        _att_bias_tiles(rb_ref, bm_ref)
        dk_s[...] = jnp.zeros_like(dk_s)
        dv_s[...] = jnp.zeros_like(dv_s)
        db_s[...] = jnp.zeros_like(db_s)
        hm = _head_masks()
        scale = HDIM ** -0.5

        def pair(p, carry):
            r0 = pl.multiple_of(p * ATT_PAIR, ATT_PAIR)
            q = q_ref[pl.ds(r0, ATT_PAIR), :] * scale
            dout = do_ref[pl.ds(r0, ATT_PAIR), :]
            kb = kp_ref[pl.ds(r0, ATT_BAND), :]
            vb = vp_ref[pl.ds(r0, ATT_BAND), :]
            key_ok = (lax.broadcasted_iota(jnp.int32, (1, ATT_BAND), 1) + (r0 - ATT_PAD)) >= 0
            dq = jnp.zeros((ATT_PAIR, MIXW), f32)
            dkb = jnp.zeros((ATT_BAND, MIXW), f32)
            dvb = jnp.zeros((ATT_BAND, MIXW), f32)
            for h in range(NHEAD):
                qm = jnp.where(hm[h], q, 0.0).astype(bf16)
                dom = jnp.where(hm[h], dout, 0.0).astype(bf16)
                p_h = _att_probs(qm, kb, bm_ref[h], key_ok)
                dp = _dot_nt(dom, vb)
                ds = p_h * (dp - jnp.sum(dp * p_h, axis=-1, keepdims=True))
                dsb = ds.astype(bf16)
                dq = dq + jnp.where(hm[h], _dot(dsb, kb), 0.0)
                dkb = dkb + _dot_tn(dsb, qm)
                dvb = dvb + _dot_tn(p_h, dom)
                db_s[h] = db_s[h] + ds
            dz_ref[pl.ds(r0, ATT_PAIR), 0:MIXW] = (dq * scale).astype(bf16)
            dk_s[pl.ds(r0, ATT_BAND), :] = dk_s[pl.ds(r0, ATT_BAND), :] + dkb
            dv_s[pl.ds(r0, ATT_BAND), :] = dv_s[pl.ds(r0, ATT_BAND), :] + dvb
            return carry

        lax.fori_loop(0, SEQ // ATT_PAIR, pair, 0)
        dz_ref[:, MIXW:2 * MIXW] = dk_s[pl.ds(ATT_PAD, SEQ), :].astype(bf16)
        dz_ref[:, 2 * MIXW:3 * MIXW] = dv_s[pl.ds(ATT_PAD, SEQ), :].astype(bf16)
        _att_bias_grad(db_s, drb_ref)

    col = lambda j: pl.BlockSpec((SEQ, MIXW), lambda i: (0, j))
    return pl.pallas_call(
        body, name="attn_bwd", grid=(1,),
        in_specs=[col(0), col(1), col(2), pl.BlockSpec((8, REL_SIZE), lambda i: (0, 0)), pl.BlockSpec((SEQ, MIXW), lambda i: (0, 0))],
        out_specs=[pl.BlockSpec((SEQ, 3 * MIXW), lambda i: (0, 0)), pl.BlockSpec((8, REL_SIZE), lambda i: (0, 0))],
        out_shape=[_sds((SEQ, 3 * MIXW), bf16), _sds((8, REL_SIZE), f32)],
        scratch_shapes=[pltpu.VMEM((SEQ + ATT_PAD, MIXW), bf16), pltpu.VMEM((SEQ + ATT_PAD, MIXW), bf16),
                        pltpu.VMEM((NHEAD, ATT_PAIR, ATT_BAND), f32),
                        pltpu.VMEM((SEQ + ATT_PAD, MIXW), f32), pltpu.VMEM((SEQ + ATT_PAD, MIXW), f32),
                        pltpu.VMEM((NHEAD, ATT_PAIR, ATT_BAND), f32)],
        compiler_params=_params(("arbitrary",)),
    )(zm, zm, zm, rb8, do)


def _hgrn_out_bwd(zm3, ng, oraw3, do3):
    def body(g_ref, ng_ref, o_ref, do_ref, dor_ref, dg_ref, dng_ref):
        hm = _same_head(MIXW, HDIM, bf16)
        ngv = ng_ref[...]
        dng = jnp.zeros((1, MIXW), f32)
        for t in range(HG_T):
            o, g, d = o_ref[:, t, :], g_ref[:, t, :], do_ref[:, t, :]
            rs = lax.rsqrt(_dot_hl(o * o, hm) * (1.0 / HDIM) + EPS)
            y1 = o * rs
            dy2 = d * _silu(g)
            dg_ref[:, t, :] = (d * y1 * ngv * _dsilu(g)).astype(bf16)
            dng = dng + jnp.sum(dy2 * y1, axis=0, keepdims=True)
            dy1 = dy2 * ngv
            dor_ref[:, t, :] = rs * (dy1 - y1 * (_dot_hl(dy1 * y1, hm) * (1.0 / HDIM)))
        dng_ref[...] = jnp.broadcast_to(dng, (8, MIXW))

    blk = pl.BlockSpec((HG_N, HG_T, MIXW), lambda i: (0, 0, 0))
    return pl.pallas_call(
        body, name="hgrn_out_bwd", grid=(1,),
        in_specs=[pl.BlockSpec((HG_N, HG_T, MIXW), lambda i: (0, 0, 6)), pl.BlockSpec((1, MIXW), lambda i: (0, 0)), blk, blk],
        out_specs=[blk, blk, pl.BlockSpec((8, MIXW), lambda i: (0, 0))],
        out_shape=[_sds((HG_N, HG_T, MIXW), f32), _sds((HG_N, HG_T, MIXW), bf16), _sds((8, MIXW), f32)],
        compiler_params=_params(("arbitrary",)),
    )(zm3, ng, oraw3, do3)


def _hgrn_bwd(zm3, lb, dor3, states):
    def body(q_ref, f_ref, i_ref, lb_ref, dor_ref, st_s, dz_ref, dlb_ref,
             qf_s, kf_s, b_s, dq_s, dk_s, db_s, dv_s, w_s, x_s, cur_s):
        lb = lb_ref[...]
        hm = _same_head(MIXW, HDIM, bf16)
        hmf = _same_head(MIXW, HDIM, f32)
        b = None
        for t in range(HG_T):
            qf, kf, lf, _, _, _ = _hg_gates(q_ref[:, t, :], f_ref[:, t, :], lb)
            b = lf if b is None else b + lf
            qf_s[:, t, :] = qf
            kf_s[:, t, :] = kf
            b_s[:, t, :] = b

        def block_terms(n):
            bn = b_s[n]
            bl = bn[HG_T - 1:HG_T]
            eb = jnp.exp(bn)
            ek = jnp.exp(bl - bn)
            return qf_s[n] * eb, kf_s[n] * ek, jnp.exp(bl), eb, ek

        cur_s[...] = jnp.zeros((MIXW, MIXW), f32)
        last = lax.broadcasted_iota(jnp.int32, (HG_T, 1), 0) == HG_T - 1

        def bwd_step(j, carry):
            n = HG_N - 1 - j
            qd, kd, dec, eb, ek = block_terms(n)
            v, do_n = i_ref[n], dor_ref[n]
            dst = cur_s[...]
            st = st_s[n]
            dqd = _dot(do_n, st)
            dkd = _dot(v, dst)
            ddec = jnp.sum(dst * st.astype(f32), axis=0, keepdims=True)
            cur_s[...] = dst * dec + _dot_tn(do_n, qd) * hmf
            dq_s[n] = dqd * eb
            dk_s[n] = dkd * ek
            dv_s[n] = _dot_nt(kd, dst)
            dbl = jnp.sum(dkd * kd, axis=0, keepdims=True) + ddec * dec
            db_s[n] = dqd * qd - dkd * kd + jnp.where(last, dbl, 0.0)
            return carry

        lax.fori_loop(0, HG_N, bwd_step, 0)
        for t in range(HG_T):
            qt, bt, dot_t = qf_s[:, t, :], b_s[:, t, :], dor_ref[:, t, :]
            for s in range(t + 1):
                w = qt * kf_s[:, s, :]
                if s < t:
                    w = w * jnp.exp(bt - b_s[:, s, :])
                w_s[pl.ds(s * HG_N, HG_N), :] = w.astype(bf16)
                x_s[pl.ds(s * HG_N, HG_N), :] = (dot_t * i_ref[:, s, :]).astype(bf16)
            p = jnp.dot(w_s[pl.ds(0, (t + 1) * HG_N), :], hm, preferred_element_type=f32)
            dp = jnp.dot(x_s[pl.ds(0, (t + 1) * HG_N), :], hm, preferred_element_type=f32)
            dq_t = jnp.zeros((HG_N, MIXW), f32)
            db_t = jnp.zeros((HG_N, MIXW), f32)
            for s in range(t + 1):
                ps = p[s * HG_N:(s + 1) * HG_N]
                dps = dp[s * HG_N:(s + 1) * HG_N]
                ks = kf_s[:, s, :]
                dv_s[:, s, :] = dv_s[:, s, :] + ps * dot_t
                if s < t:
                    dec_ts = jnp.exp(bt - b_s[:, s, :])
                    g1 = dps * ks * dec_ts
                    dk_s[:, s, :] = dk_s[:, s, :] + dps * qt * dec_ts
                    gw = g1 * qt
                    db_t = db_t + gw
                    db_s[:, s, :] = db_s[:, s, :] - gw
                else:
                    g1 = dps * ks
                    dk_s[:, s, :] = dk_s[:, s, :] + dps * qt
                dq_t = dq_t + g1
            dq_s[:, t, :] = dq_s[:, t, :] + dq_t
            db_s[:, t, :] = db_s[:, t, :] + db_t
        run = jnp.zeros((HG_N, MIXW), f32)
        dlb = jnp.zeros((1, MIXW), f32)
        oml = 1.0 - lb
        for t in range(HG_T - 1, -1, -1):
            run = run + db_s[:, t, :]
            q = q_ref[:, t, :]
            _, _, _, sq, sg, f = _hg_gates(q, f_ref[:, t, :], lb)
            dkf = dk_s[:, t, :]
            df = jnp.where(f > LOG_FLOOR, run / f, 0.0)
            dsg = (df - dkf) * oml
            dlb = dlb + jnp.sum((df - dkf) * (1.0 - sg), axis=0, keepdims=True)
            dz_ref[:, t, 0:MIXW] = (dq_s[:, t, :] * sq * (1.0 + q * (1.0 - sq))).astype(bf16)
            dz_ref[:, t, MIXW:2 * MIXW] = (dsg * sg * (1.0 - sg)).astype(bf16)
            dz_ref[:, t, 2 * MIXW:3 * MIXW] = dv_s[:, t, :].astype(bf16)
        dlb_ref[...] = jnp.broadcast_to(dlb, (8, MIXW))

    one = pl.Buffered(1)
    col = lambda j: pl.BlockSpec((HG_N, HG_T, MIXW), lambda i: (0, 0, j), pipeline_mode=one)
    s3 = pltpu.VMEM((HG_N, HG_T, MIXW), f32)
    return pl.pallas_call(
        body, name="hgrn_bwd", grid=(1,),
        in_specs=[col(3), col(4), col(5), pl.BlockSpec((1, MIXW), lambda i: (0, 0)),
                  pl.BlockSpec((HG_N, HG_T, MIXW), lambda i: (0, 0, 0), pipeline_mode=one),
                  pl.BlockSpec((HG_N, MIXW, MIXW), lambda i: (0, 0, 0), pipeline_mode=one)],
        out_specs=[pl.BlockSpec((HG_N, HG_T, 3 * MIXW), lambda i: (0, 0, 0)), pl.BlockSpec((8, MIXW), lambda i: (0, 0))],
        out_shape=[_sds((HG_N, HG_T, 3 * MIXW), bf16), _sds((8, MIXW), f32)],
        scratch_shapes=[s3, s3, s3, s3, s3, s3, s3,
                        pltpu.VMEM((HG_T * HG_N, MIXW), bf16), pltpu.VMEM((HG_T * HG_N, MIXW), bf16),
                        pltpu.VMEM((MIXW, MIXW), f32)],
        compiler_params=_params(("arbitrary",)),
    )(zm3, zm3, zm3, lb, dor3, states)


def _gmlp_bwd(zm, ng, ws, bs8, do):
    def body(u_ref, v_ref, ng_ref, ws_ref, bs_ref, do_ref, dz_ref, dws_ref, dng_ref, dbs_ref, dm_s):
        hm = _head_masks()
        tril, wts = _gm_weights(ws_ref)
        bias = _gm_bias(bs_ref)
        ngv = ng_ref[...]
        dws_ref[...] = jnp.zeros_like(dws_ref)
        dm_s[...] = jnp.zeros_like(dm_s)

        def blk(n, dng):
            rows = pl.ds(pl.multiple_of(n * GM_T, GM_T), GM_T)
            cu, cv, d = u_ref[rows, :], v_ref[rows, :], do_ref[rows, :]
            v = _gelu(cv)
            r = lax.rsqrt(jnp.mean(v * v, axis=-1, keepdims=True) + EPS)
            vh = v * r
            vn = vh * ngv
            u = _gelu(cu)
            dm = d * u
            dmb, vnb = dm.astype(bf16), vn.astype(bf16)
            dvn = jnp.zeros((GM_T, MIXW), f32)
            for g in range(NHEAD):
                dws_ref[g] = dws_ref[g] + _dot_nt(jnp.where(hm[g], dm, 0.0), vnb)
                dvn = dvn + jnp.where(hm[g], _dot_tn(wts[g], dmb), 0.0)
            dm_s[...] = dm_s[...] + dm
            dvh = dvn * ngv
            dv = r * (dvh - vh * jnp.mean(dvh * vh, axis=-1, keepdims=True))
            dz_ref[rows, 0:MIXW] = (d * _gm_mixed(vn, wts, bias, hm) * _dgelu(cu)).astype(bf16)
            dz_ref[rows, MIXW:2 * MIXW] = (dv * _dgelu(cv)).astype(bf16)
            return dng + jnp.sum(dvn * vh, axis=0, keepdims=True)

        dng = lax.fori_loop(0, SEQ // GM_T, blk, jnp.zeros((1, MIXW), f32))
        dng_ref[...] = jnp.broadcast_to(dng, (8, MIXW))
        for g in range(NHEAD):
            dws_ref[g] = jnp.where(tril, dws_ref[g], 0.0)
        dbs_ref[...] = _dot_nt_hl(_gm_expand(), dm_s[...])

    col = lambda j: pl.BlockSpec((SEQ, MIXW), lambda i: (0, j))
    return pl.pallas_call(
        body, name="gmlp_bwd", grid=(1,),
        in_specs=[col(7), col(8), pl.BlockSpec((1, MIXW), lambda i: (0, 0)),
                  pl.BlockSpec((NHEAD, GM_T, GM_T), lambda i: (0, 0, 0)), pl.BlockSpec((8, GM_T), lambda i: (0, 0)),
                  pl.BlockSpec((SEQ, MIXW), lambda i: (0, 0))],
        out_specs=[pl.BlockSpec((SEQ, 2 * MIXW), lambda i: (0, 0)), pl.BlockSpec((NHEAD, GM_T, GM_T), lambda i: (0, 0, 0)),
                   pl.BlockSpec((8, MIXW), lambda i: (0, 0)), pl.BlockSpec((8, GM_T), lambda i: (0, 0))],
        out_shape=[_sds((SEQ, 2 * MIXW), bf16), _sds((NHEAD, GM_T, GM_T), f32), _sds((8, MIXW), f32), _sds((8, GM_T), f32)],
        scratch_shapes=[pltpu.VMEM((GM_T, MIXW), f32)],
        compiler_params=_params(("arbitrary",)),
    )(zm, zm, ng, ws, bs8, do)


def _lru_bwd(zm, cw8, cb, wa, ba, wx, bx, lam, hd, do):
    nchunk = SEQ // LRU_T

    def body(x_ref, g_ref, cw_ref, cb_ref, wa_ref, ba_ref, wx_ref, bx_ref, lam_ref, h_ref, do_ref,
             dz_ref, dwa_ref, dwx_ref, dcw_ref, dvec_ref, xp_s, xc_s, dxc_s):
        _lru_conv(x_ref, cw_ref, cb_ref, xp_s, xc_s)
        lam_v = lam_ref[...]
        sp = jax.nn.softplus(-lam_v)
        sgl = _sigmoid(-lam_v)
        wa_v, wx_v, ba_v, bx_v = wa_ref[...], wx_ref[...], ba_ref[...], bx_ref[...]
        dwa_ref[...] = jnp.zeros_like(dwa_ref)
        dwx_ref[...] = jnp.zeros_like(dwx_ref)
        dxc_s[pl.ds(SEQ, 8), :] = jnp.zeros((8, MIXW), f32)
        row = lax.broadcasted_iota(jnp.int32, (LRU_T, 1), 0)
        zero = jnp.zeros((1, MIXW), f32)

        def chunk(j, carry):
            dh_next, a_next, dba, dbx, dlam = carry
            c = nchunk - 1 - j
            rows = pl.ds(pl.multiple_of(c * LRU_T, LRU_T), LRU_T)
            prev = pl.ds(pl.multiple_of(jnp.maximum(c - 1, 0) * LRU_T, LRU_T), LRU_T)
            first = (row + c * LRU_T) == 0
            xc, gate, d, h = xc_s[rows, :], g_ref[rows, :], do_ref[rows, :], h_ref[rows, :]
            a, mult, r, ig, m2 = _lru_gates(xc, wa_v, ba_v, wx_v, bx_v, sp, first)
            h_last = jnp.where(c > 0, h_ref[prev, :][LRU_T - 1:LRU_T, :], 0.0)
            h_m1 = jnp.where(row == 0, h_last, pltpu.roll(h, 1, 0))
            a_up = jnp.where(row == LRU_T - 1, a_next, pltpu.roll(a, LRU_T - 1, 0))
            acum, dh_loc = _lru_scan(a_up, d * _gelu(gate), True)
            dh = dh_loc + acum * dh_next
            dmult = jnp.where(first, 0.0, dh * (ig * xc))
            msq = jnp.sqrt(jnp.maximum(m2, 0.0))
            dla = dh * h_m1 * a + jnp.where(m2 > 0.0, -dmult * (1.0 - m2) / msq, 0.0)
            dpr = dla * (-LRU_C) * sp * r * (1.0 - r)
            dpi = dh * mult * xc * ig * (1.0 - ig)
            dxc_s[rows, :] = dh * mult * ig + _dot_nt(dpr, wa_v) + _dot_nt(dpi, wx_v)
            dwa_ref[...] = dwa_ref[...] + _dot_tn(xc, dpr)
            dwx_ref[...] = dwx_ref[...] + _dot_tn(xc, dpi)
            dz_ref[rows, MIXW:2 * MIXW] = (d * h * _dgelu(gate)).astype(bf16)
            return (dh[0:1], a[0:1], dba + jnp.sum(dpr, axis=0, keepdims=True), dbx + jnp.sum(dpi, axis=0, keepdims=True),
                    dlam + jnp.sum(dla * r, axis=0, keepdims=True) * (LRU_C * sgl))

        _, _, dba, dbx, dlam = lax.fori_loop(0, nchunk, chunk, (zero, zero, zero, zero, zero))
        cw = cw_ref[...]
        dxc = dxc_s[pl.ds(0, SEQ), :]
        dx = dxc * cw[3:4]
        dcw = [None] * 4
        dcw[3] = jnp.sum(dxc * x_ref[...], axis=0, keepdims=True)
        for k in range(1, 4):
            dx = dx + dxc_s[pl.ds(k, SEQ), :] * cw[3 - k:4 - k]
            dcw[3 - k] = jnp.sum(dxc * xp_s[pl.ds(8 - k, SEQ), :], axis=0, keepdims=True)
        dz_ref[:, 0:MIXW] = dx.astype(bf16)
        dcw_ref[...] = jnp.concatenate(dcw + [jnp.zeros((4, MIXW), f32)], axis=0)
        dvec_ref[...] = jnp.concatenate([jnp.sum(dxc, axis=0, keepdims=True), dba, dbx, dlam, jnp.zeros((4, MIXW), f32)], axis=0)

    col = lambda j: pl.BlockSpec((SEQ, MIXW), lambda i: (0, j))
    vec = pl.BlockSpec((1, MIXW), lambda i: (0, 0))
    vec8 = pl.BlockSpec((8, MIXW), lambda i: (0, 0))
    mat = pl.BlockSpec((MIXW, MIXW), lambda i: (0, 0))
    full = pl.BlockSpec((SEQ, MIXW), lambda i: (0, 0))
    return pl.pallas_call(
        body, name="lru_bwd", grid=(1,),
        in_specs=[col(9), col(10), vec8, vec, mat, vec, mat, vec, vec, full, full],
        out_specs=[pl.BlockSpec((SEQ, 2 * MIXW), lambda i: (0, 0)), mat, mat, vec8, vec8],
        out_shape=[_sds((SEQ, 2 * MIXW), bf16), _sds((MIXW, MIXW), f32), _sds((MIXW, MIXW), f32),
                   _sds((8, MIXW), f32), _sds((8, MIXW), f32)],
        scratch_shapes=[pltpu.VMEM((SEQ + 8, MIXW), f32), pltpu.VMEM((SEQ, MIXW), f32), pltpu.VMEM((SEQ + 8, MIXW), f32)],
        compiler_params=_params(("arbitrary",)),
    )(zm, zm, cw8, cb, wa, ba, wx, bx, lam, hd, do)


def _matmul_tn(a, b, tm, tn, b_col0=0):
    m = a.shape[1]
    n = tn if b_col0 else b.shape[1]
    off = b_col0 // tn

    def body(a_ref, b_ref, o_ref):
        o_ref[...] = _dot_tn(a_ref[...], b_ref[...]).astype(bf16)

    return pl.pallas_call(
        body, name="matmul_tn", grid=(m // tm, n // tn),
        in_specs=[pl.BlockSpec((SEQ, tm), lambda i, j: (0, i)), pl.BlockSpec((SEQ, tn), lambda i, j: (0, j + off))],
        out_specs=pl.BlockSpec((tm, tn), lambda i, j: (i, j)),
        out_shape=_sds((m, n), bf16),
        compiler_params=_params(("parallel", "arbitrary")),
    )(a, b)


def _matmul_nt_norm(pairs, x, g, dres):
    tm = 512
    steps = [a.shape[1] // t for a, _, t in pairs]
    starts = [sum(steps[:i]) for i in range(len(pairs))]
    total = sum(steps)
    npair = len(pairs)

    def body(*refs):
        a_refs, w_refs = refs[0:2 * npair:2], refs[1:2 * npair:2]
        x_ref, g_ref, dres_ref, dx_ref, dg_ref, acc_s = refs[2 * npair:]
        i, k = pl.program_id(0), pl.program_id(1)

        @pl.when(k == 0)
        def _():
            acc_s[...] = jnp.zeros_like(acc_s)

        @pl.when((i == 0) & (k == 0))
        def _():
            dg_ref[...] = jnp.zeros_like(dg_ref)

        for q in range(npair):
            @pl.when((k >= starts[q]) & (k < starts[q] + steps[q]))
            def _(q=q):
                acc_s[...] += _dot_nt(a_refs[q][...], w_refs[q][...])

        @pl.when(k == total - 1)
        def _():
            dx, dg = _rms_bwd(x_ref[...], g_ref[...], acc_s[...])
            dx_ref[...] = dres_ref[...] + dx
            dg_ref[...] += dg

    in_specs, args = [], []
    for q, (a, w, t) in enumerate(pairs):
        kmap = lambda k, q=q: jnp.clip(k - starts[q], 0, steps[q] - 1)
        in_specs += [pl.BlockSpec((tm, t), lambda i, k, kmap=kmap: (i, kmap(k))),
                     pl.BlockSpec((DM, t), lambda i, k, kmap=kmap: (0, kmap(k)))]
        args += [a, w]
    row = pl.BlockSpec((tm, DM), lambda i, k: (i, 0))
    vec = pl.BlockSpec((1, DM), lambda i, k: (0, 0))
    return pl.pallas_call(
        body, name="matmul_nt_norm", grid=(SEQ // tm, total),
        in_specs=in_specs + [row, vec, row], out_specs=[row, vec],
        out_shape=[_sds((SEQ, DM), f32), _sds((1, DM), f32)],
        scratch_shapes=[pltpu.VMEM((tm, DM), f32)],
        compiler_params=_params(("arbitrary", "arbitrary")),
    )(*args, x, g, dres)


def _merge_bwd(dx1, y, g2, outs, zg, wb, wo):
    def body(dx_ref, y_ref, g_ref, oa_ref, ob_ref, oc_ref, od_ref, zg_ref, wb_ref, wo_ref,
             da_ref, db_ref, dc_ref, dd_ref, dzg_ref, dpj_ref, dy_ref, dg_ref):
        @pl.when(pl.program_id(0) == 0)
        def _():
            dg_ref[...] = jnp.zeros_like(dg_ref)

        dy, dg = _rms_bwd(y_ref[...], g_ref[...], dx_ref[...])
        dg_ref[...] += dg
        dyb = dy.astype(bf16)
        dy_ref[...] = dyb
        dmerged = _dot_nt(dyb, wo_ref[...])
        for n, (o_ref, do_ref) in enumerate(((oa_ref, da_ref), (ob_ref, db_ref), (oc_ref, dc_ref), (od_ref, dd_ref))):
            cols = slice(n * DM, (n + 1) * DM)
            gate = _sigmoid(zg_ref[:, cols])
            proj = jnp.dot(o_ref[...], wb_ref[n], preferred_element_type=f32)
            dproj = (dmerged * gate).astype(bf16)
            dpj_ref[:, cols] = dproj
            dzg_ref[:, cols] = (dmerged * proj * gate * (1.0 - gate)).astype(bf16)
            do_ref[...] = _dot_nt(dproj, wb_ref[n])

    row = lambda w: pl.BlockSpec((ROW_TILE, w), lambda i: (i, 0))
    vec = pl.BlockSpec((1, DM), lambda i: (0, 0))
    return pl.pallas_call(
        body, name="merge_bwd", grid=(SEQ // ROW_TILE,),
        in_specs=[row(DM), row(DM), vec] + [row(MIXW)] * 4 + [row(NGATE), pl.BlockSpec((NHEAD, MIXW, DM), lambda i: (0, 0, 0)),
                                                              pl.BlockSpec((DM, DM), lambda i: (0, 0))],
        out_specs=[row(MIXW)] * 4 + [row(NGATE), row(NGATE), row(DM), vec],
        out_shape=[_sds((SEQ, MIXW), f32)] * 4 + [_sds((SEQ, NGATE), bf16), _sds((SEQ, NGATE), bf16), _sds((SEQ, DM), bf16),
                                                  _sds((1, DM), f32)],
        compiler_params=_params(("arbitrary",)),
    )(dx1, y, g2, *outs, zg, wb, wo)


def _ffn_bwd(dx2, f, g4, u, w2):
    def body(dx_ref, f_ref, g_ref, u_ref, w_ref, du_ref, a_ref, df_ref, dg_ref):
        @pl.when(pl.program_id(0) == 0)
        def _():
            dg_ref[...] = jnp.zeros_like(dg_ref)

        df, dg = _rms_bwd(f_ref[...], g_ref[...], dx_ref[...])
        dg_ref[...] += dg
        dfb = df.astype(bf16)
        df_ref[...] = dfb
        da = _dot_nt(dfb, w_ref[...])
        gt, up = u_ref[:, :FFH], u_ref[:, FFH:]
        a_ref[...] = (_silu(gt) * up).astype(bf16)
        du_ref[:, :FFH] = (da * up * _dsilu(gt)).astype(bf16)
        du_ref[:, FFH:] = (da * _silu(gt)).astype(bf16)

    row = lambda w: pl.BlockSpec((ROW_TILE, w), lambda i: (i, 0))
    vec = pl.BlockSpec((1, DM), lambda i: (0, 0))
    return pl.pallas_call(
        body, name="ffn_bwd", grid=(SEQ // ROW_TILE,),
        in_specs=[row(DM), row(DM), vec, row(2 * FFH), pl.BlockSpec((FFH, DM), lambda i: (0, 0))],
        out_specs=[row(2 * FFH), row(FFH), row(DM), vec],
        out_shape=[_sds((SEQ, 2 * FFH), bf16), _sds((SEQ, FFH), bf16), _sds((SEQ, DM), bf16), _sds((1, DM), f32)],
        compiler_params=_params(("arbitrary",)),
    )(dx2, f, g4, u, w2)


def _layer_bwd(dx2, p, sv, ffn_grads_ready=None):
    du, act, df, dg4 = _ffn_bwd(dx2, sv["f"], p["g4"], sv["u"], p["w2"])
    dw2 = _matmul_tn(act, df, 1408, DM)
    dx1, dg3 = _matmul_nt_norm([(du, p["w1"], 1408)], sv["x1"], p["g3"], dx2)
    dw1 = _matmul_tn(sv["h2"], du, DM, 1408)
    if ffn_grads_ready is not None:
        dx1 = ffn_grads_ready(dict(w_ffn_in=dw1, w_ffn_out=dw2), dx1)
    *dos, dzg, dproj, dy, dg2 = _merge_bwd(dx1, sv["y"], p["g2"], sv["outs"], sv["zg"], p["wb"], p["wo"])
    dwo = _matmul_tn(sv["merged"], dy, DM, DM)
    dwb = jnp.stack([_matmul_tn(sv["outs"][n], dproj, MIXW, DM, b_col0=n * DM) if n else
                     _matmul_tn(sv["outs"][0], dproj[:, :DM], MIXW, DM) for n in range(NHEAD)])
    zm = sv["zm"]
    zm3 = zm.reshape(HG_N, HG_T, NMIX)
    dza, drb = _attn_bwd(zm, p["rb8"], dos[0])
    dor, dgb, dhng = _hgrn_out_bwd(zm3, p["hng"], sv["obraw3"], dos[1].reshape(HG_N, HG_T, MIXW))
    dzb, dlb = _hgrn_bwd(zm3, p["lb"], dor, sv["hstates"])
    dzc, dws, dgng, dbs = _gmlp_bwd(zm, p["gng"], p["gws"], p["gbs8"], dos[2])
    dzd, dwa, dwx, dcw, dvec = _lru_bwd(zm, p["cw8"], p["cb"], p["wa"], p["ba"], p["wx"], p["bx"], p["lam"], sv["hd"], dos[3])
    dzm = jnp.concatenate([dza, dzb.reshape(SEQ, 3 * MIXW), dgb.reshape(SEQ, MIXW), dzc, dzd], axis=1)
    dx0, dg1 = _matmul_nt_norm([(dzm, p["wm"], 1408), (dzg, p["wgt"], 1024)], sv["x"], p["g1"], dx1)
    dwin = jnp.concatenate([_matmul_tn(sv["h"], dzm, DM, 1408), _matmul_tn(sv["h"], dzg, DM, 1024)], axis=1)
    big = dict(w_in=dwin, w_branch=dwb, w_out=dwo, w_ffn_in=dw1, w_ffn_out=dw2)
    small = dict(
        norm_mix_pre=dg1[0], norm_mix_post=dg2[0], norm_ffn_pre=dg3[0], norm_ffn_post=dg4[0],
        attn_rel_bias=drb[:NHEAD], lb=dlb[0], hgrn_norm_g=dhng[0], gmlp_norm_g=dgng[0], gmlp_ws=dws, gmlp_bs=dbs[:NHEAD],
        lru_conv_w=dcw[:NHEAD], lru_conv_b=dvec[0], lru_wa=_diag_blocks(dwa), lru_ba=dvec[1], lru_wx=_diag_blocks(dwx),
        lru_bx=dvec[2], lru_lambda=dvec[3],
    )
    return dx0, big, small


MIX_BIG = ("w_in", "w_branch", "w_out")
FFN_BIG = ("w_ffn_in", "w_ffn_out")
BIG = MIX_BIG + FFN_BIG
SMALL = ("norm_mix_pre", "norm_mix_post", "norm_ffn_pre", "norm_ffn_post", "attn_rel_bias", "hgrn_lb_logits", "hgrn_norm_g",
         "gmlp_norm_g", "gmlp_ws", "gmlp_bs", "lru_conv_w", "lru_conv_b", "lru_wa", "lru_ba", "lru_wx", "lru_bx", "lru_lambda")


def _local_step(x, tgt, full, small):
    lbs = _lb_fwd(small["hgrn_lb_logits"])
    params, saved = [], []
    for l in range(DEPTH):
        p = _layer_params(l, {k: full[k][l] for k in BIG}, small, lbs)
        x, sv = _layer_fwd(x, p)
        params.append(p)
        saved.append(sv)
    loss, dx = _loss_head(x, tgt)
    bigs, smalls = [None] * DEPTH, [None] * DEPTH
    for l in range(DEPTH - 1, -1, -1):
        dx, bigs[l], smalls[l] = _layer_bwd(dx, params[l], saved[l])
    gbig = {k: jnp.stack([bigs[l][k] for l in range(DEPTH)]) for k in BIG}
    gsmall = {k: jnp.stack([smalls[l][k] for l in range(DEPTH)]) for k in smalls[0]}
    gsmall["hgrn_lb_logits"] = _lb_bwd(small["hgrn_lb_logits"], gsmall.pop("lb"))
    return loss, dx, gbig, gsmall


HBM_ANY = pl.BlockSpec(memory_space=pl.ANY)


def _mesh_pos():
    return lax.axis_index("x"), lax.axis_index("y"), lax.axis_index("c")


def _all_gather(x, name):
    def body(x_ref, out_ref, send_sems, recv_sems, local_sem):
        ax, ay, ac = _mesh_pos()
        me, sibling = (ax, ay, ac), (ax, ay, 1 - ac)
        chips = [(1 - ax, ay), (ax, 1 - ay), (1 - ax, 1 - ay)]

        def slot(px, py, pc):
            return out_ref.at[4 * px + 2 * py + pc]

        def copy(k, block, to, src=None):
            return pltpu.make_async_remote_copy(
                src_ref=slot(*block) if src is None else src, dst_ref=slot(*block),
                send_sem=send_sems.at[k], recv_sem=recv_sems.at[k], device_id=to, device_id_type=MESH_ID)

        mine = pltpu.make_async_copy(x_ref, slot(*me), local_sem)
        mine.start()
        first = [copy(0, me, sibling, src=x_ref)]
        first += [copy(1 + j, me, (*chip, ac), src=x_ref) for j, chip in enumerate(chips)]
        for cp in first:
            cp.start()
        passed = [copy(4 + j, (*chip, ac), sibling) for j, chip in enumerate(chips)]
        for j, chip in enumerate(chips):
            copy(1 + j, (*chip, ac), me).wait_recv()
            passed[j].start()
        copy(0, sibling, me).wait_recv()
        for j, chip in enumerate(chips):
            copy(4 + j, (*chip, 1 - ac), me).wait_recv()
        for cp in first + passed:
            cp.wait_send()
        mine.wait()

    return pl.pallas_call(
        body, name=name, out_shape=_sds((NDEV,) + x.shape, x.dtype),
        in_specs=[HBM_ANY], out_specs=HBM_ANY,
        scratch_shapes=[pltpu.SemaphoreType.DMA((7,)), pltpu.SemaphoreType.DMA((7,)), pltpu.SemaphoreType.DMA],
    )(x)


def _exchange(g, name):
    def body(g_ref, out_ref, send_sems, recv_sems, local_sem):
        ax, ay, ac = _mesh_pos()
        me = 4 * ax + 2 * ay + ac
        mine = pltpu.make_async_copy(g_ref.at[me], out_ref.at[me], local_sem)
        mine.start()
        copies = []
        for k in range(1, NDEV):
            px = 1 - ax if k & 4 else ax
            py = 1 - ay if k & 2 else ay
            pc = 1 - ac if k & 1 else ac
            copies.append(pltpu.make_async_remote_copy(
                src_ref=g_ref.at[4 * px + 2 * py + pc], dst_ref=out_ref.at[me],
                send_sem=send_sems.at[k - 1], recv_sem=recv_sems.at[k - 1], device_id=(px, py, pc), device_id_type=MESH_ID))
        for cp in copies:
            cp.start()
        for cp in copies:
            cp.wait()
        mine.wait()

    return pl.pallas_call(
        body, name=name, out_shape=_sds(g.shape, g.dtype),
        in_specs=[HBM_ANY], out_specs=HBM_ANY,
        scratch_shapes=[pltpu.SemaphoreType.DMA((7,)), pltpu.SemaphoreType.DMA((7,)), pltpu.SemaphoreType.DMA],
    )(g)


def _peer(ax, ay, ac, k):
    return (1 - ax if k & 4 else ax, 1 - ay if k & 2 else ay, 1 - ac if k & 1 else ac)


def _handshake(peers):
    barrier = pltpu.get_barrier_semaphore()
    for peer in peers:
        pl.semaphore_signal(barrier, inc=1, device_id=peer, device_id_type=MESH_ID)
    pl.semaphore_wait(barrier, len(peers))


SEQUENCER = dict(axis_name="seq", num_cores=1)
GATHER_ID = 1
EXCHANGE_ID = 2


def _gather_sc(xs, name):
    n = len(xs)

    def body(*refs):
        srcs, outs = refs[:n], refs[n:2 * n]
        send_sems, recv_sems, local_sems = refs[2 * n:]
        ax, ay, ac = _mesh_pos()
        me, sibling = (ax, ay, ac), (ax, ay, 1 - ac)
        chips = [(1 - ax, ay), (ax, 1 - ay), (1 - ax, 1 - ay)]
        _handshake([sibling] + [(*chip, ac) for chip in chips])

        def slot(i, px, py, pc):
            return outs[i].at[4 * px + 2 * py + pc]

        def copy(i, k, block, to, src=None):
            return pltpu.make_async_remote_copy(
                src_ref=slot(i, *block) if src is None else src, dst_ref=slot(i, *block),
                send_sem=send_sems.at[7 * i + k], recv_sem=recv_sems.at[7 * i + k], device_id=to, device_id_type=MESH_ID)

        mine = [pltpu.make_async_copy(srcs[i], slot(i, *me), local_sems.at[i]) for i in range(n)]
        first = []
        for i in range(n):
            first += [copy(i, 1 + j, me, (*chip, ac), src=srcs[i]) for j, chip in enumerate(chips)]
        for i in range(n):
            first += [copy(i, 0, me, sibling, src=srcs[i])]
        for cp in first + mine:
            cp.start()
        passed = []
        for i in range(n):
            for j, chip in enumerate(chips):
                copy(i, 1 + j, (*chip, ac), me).wait_recv()
                passed.append(copy(i, 4 + j, (*chip, ac), sibling))
                passed[-1].start()
        for i in range(n):
            copy(i, 0, sibling, me).wait_recv()
            for j, chip in enumerate(chips):
                copy(i, 4 + j, (*chip, 1 - ac), me).wait_recv()
        for cp in first + passed:
            cp.wait_send()
        for cp in mine:
            cp.wait()

    return pl.kernel(
        body, name=name, out_type=[_sds((NDEV,) + x.shape, x.dtype) for x in xs],
        mesh=plsc.ScalarSubcoreMesh(**SEQUENCER),
        scratch_types=[pltpu.SemaphoreType.DMA((7 * n,)), pltpu.SemaphoreType.DMA((7 * n,)), pltpu.SemaphoreType.DMA((n,))],
        compiler_params=pltpu.CompilerParams(collective_id=GATHER_ID),
    )(*xs)


def _exchange_sc(gs, name):
    n = len(gs)

    def body(*refs):
        srcs, outs = refs[:n], refs[n:2 * n]
        send_sems, recv_sems, local_sems = refs[2 * n:]
        ax, ay, ac = _mesh_pos()
        me = 4 * ax + 2 * ay + ac
        peers = [_peer(ax, ay, ac, k) for k in range(1, NDEV)]
        _handshake(peers)
        mine = [pltpu.make_async_copy(srcs[i].at[me], outs[i].at[me], local_sems.at[i]) for i in range(n)]
        copies = []
        for i in range(n):
            for k, (px, py, pc) in enumerate(peers):
                copies.append(pltpu.make_async_remote_copy(
                    src_ref=srcs[i].at[4 * px + 2 * py + pc], dst_ref=outs[i].at[me],
                    send_sem=send_sems.at[7 * i + k], recv_sem=recv_sems.at[7 * i + k],
                    device_id=(px, py, pc), device_id_type=MESH_ID))
        for cp in copies + mine:
            cp.start()
        for cp in copies + mine:
            cp.wait()

    return pl.kernel(
        body, name=name, out_type=[_sds(g.shape, g.dtype) for g in gs],
        mesh=plsc.ScalarSubcoreMesh(**SEQUENCER),
        scratch_types=[pltpu.SemaphoreType.DMA((7 * n,)), pltpu.SemaphoreType.DMA((7 * n,)), pltpu.SemaphoreType.DMA((n,))],
        compiler_params=pltpu.CompilerParams(collective_id=EXCHANGE_ID),
    )(*gs)


HBM_SPEC = pl.BlockSpec(memory_space=pltpu.HBM)
SEM_SPEC = pl.BlockSpec(memory_space=pltpu.SEMAPHORE)
DATAFLOW = pltpu.SideEffectType.DATAFLOW_SIDE_EFFECTING


def _exchange_copies(srcs, lands, send_sems, recv_sems, local_sems):
    n = len(srcs)
    ax, ay, ac = _mesh_pos()
    me = 4 * ax + 2 * ay + ac
    copies = [pltpu.make_async_copy(srcs[i].at[me], lands[i].at[me], local_sems.at[i]) for i in range(n)]
    for i in range(n):
        for k in range(1, NDEV):
            px, py, pc = _peer(ax, ay, ac, k)
            copies.append(pltpu.make_async_remote_copy(
                src_ref=srcs[i].at[4 * px + 2 * py + pc], dst_ref=lands[i].at[me],
                send_sem=send_sems.at[7 * i + k - 1], recv_sem=recv_sems.at[7 * i + k - 1],
                device_id=(px, py, pc), device_id_type=MESH_ID))
    return copies


def _exchange_start(gs, name):
    n = len(gs)

    def body(*refs):
        srcs, lands = refs[:n], refs[n:2 * n]
        send_sems, recv_sems, local_sems = refs[2 * n:2 * n + 3]
        token = refs[-1]
        for cp in _exchange_copies(srcs, lands, send_sems, recv_sems, local_sems):
            cp.start()
        token[...] = jnp.zeros_like(token)

    hbm = [pltpu.HBM(g.shape, g.dtype) for g in gs]
    outs = pl.pallas_call(
        body, name=name,
        out_shape=(pltpu.SemaphoreType.DMA((7 * n,)), pltpu.SemaphoreType.DMA((7 * n,)), pltpu.SemaphoreType.DMA((n,)),
                   *hbm, *hbm, _sds((8, 128), f32)),
        in_specs=[HBM_SPEC] * (2 * n),
        out_specs=(SEM_SPEC, SEM_SPEC, SEM_SPEC, *[HBM_SPEC] * (2 * n), pl.BlockSpec(memory_space=pltpu.VMEM)),
        input_output_aliases={i: 3 + i for i in range(2 * n)},
        compiler_params=pltpu.CompilerParams(has_side_effects=DATAFLOW),
    )(*[pltpu.with_memory_space_constraint(g, pltpu.HBM) for g in gs],
      *[pltpu.with_memory_space_constraint(lax.empty(g.shape, g.dtype), pltpu.HBM) for g in gs])
    return outs[:-1], outs[-1]


def _exchange_wait(handles, after, name):
    n = (len(handles) - 3) // 2
    send_sems, recv_sems, local_sems = handles[:3]
    srcs, lands = handles[3:3 + n], handles[3 + n:]

    def body(*refs):
        srcs, lands = refs[:n], refs[n:2 * n]
        send_sems, recv_sems, local_sems = refs[2 * n:2 * n + 3]
        for cp in _exchange_copies(srcs, lands, send_sems, recv_sems, local_sems):
            cp.wait()

    hbm = [pltpu.HBM(g.shape, g.dtype) for g in srcs]
    outs = pl.pallas_call(
        body, name=name, out_shape=(*hbm, *hbm),
        in_specs=[HBM_SPEC] * (2 * n) + [SEM_SPEC] * 3 + [pl.BlockSpec(memory_space=pl.ANY)],
        out_specs=tuple([HBM_SPEC] * (2 * n)),
        input_output_aliases={i: i for i in range(2 * n)},
        compiler_params=pltpu.CompilerParams(has_side_effects=DATAFLOW),
    )(*srcs, *lands, send_sems, recv_sems, local_sems, after)
    return outs[n:]


def _row_tile(rows, cols):
    cap = max(8, (1 << 18) // cols)
    if rows <= cap:
        return rows
    best = None
    for t in range(8, cap + 1, 8):
        if rows % t == 0:
            best = t
    assert best is not None, (rows, cols)
    return best


def _sum_parts(parts, name):
    npart, rows, cols = parts.shape
    tr = _row_tile(rows, cols)

    def body(p_ref, o_ref):
        g = p_ref[0].astype(f32)
        for j in range(1, npart):
            g = g + p_ref[j].astype(f32)
        o_ref[...] = g

    return pl.pallas_call(
        body, name=name, grid=(rows // tr,),
        in_specs=[pl.BlockSpec((npart, tr, cols), lambda i: (0, i, 0))], out_specs=pl.BlockSpec((tr, cols), lambda i: (i, 0)),
        out_shape=_sds((rows, cols), f32), compiler_params=_params(("parallel",)),
    )(parts)


def _adamw(parts, w, m, v, name):
    npart, rows, cols = parts.shape
    tr = _row_tile(rows, cols)
    c1 = 1.0 / (1.0 - ADAM_B1 ** ADAM_STEP)
    c2 = 1.0 / (1.0 - ADAM_B2 ** ADAM_STEP)

    def body(p_ref, w_ref, m_ref, v_ref, g_ref, d_ref, mo_ref, vo_ref):
        g = p_ref[0].astype(f32)
        for j in range(1, npart):
            g = g + p_ref[j].astype(f32)
        mn = ADAM_B1 * m_ref[...] + (1.0 - ADAM_B1) * g
        vn = ADAM_B2 * v_ref[...] + (1.0 - ADAM_B2) * (g * g)
        g_ref[...] = g
        mo_ref[...] = mn
        vo_ref[...] = vn
        d_ref[...] = (-ADAM_LR) * ((mn * c1) / (jnp.sqrt(vn * c2) + ADAM_EPS) + ADAM_WD * w_ref[...])

    blk = pl.BlockSpec((tr, cols), lambda i: (i, 0))
    return pl.pallas_call(
        body, name=name, grid=(rows // tr,),
        in_specs=[pl.BlockSpec((npart, tr, cols), lambda i: (0, i, 0)), blk, blk, blk], out_specs=[blk] * 4,
        out_shape=[_sds((rows, cols), f32)] * 4, compiler_params=_params(("parallel",)),
    )(parts, w, m, v)


def _adamw_layer(parts, w, m, v, acc, l, name):
    npart, rows, cols = parts.shape
    tr = _row_tile(rows, cols)
    c1 = 1.0 / (1.0 - ADAM_B1 ** ADAM_STEP)
    c2 = 1.0 / (1.0 - ADAM_B2 ** ADAM_STEP)

    def body(p_ref, w_ref, m_ref, v_ref, *refs):
        g_ref, d_ref, mo_ref, vo_ref = refs[-4:]
        g = p_ref[0].astype(f32)
        for j in range(1, npart):
            g = g + p_ref[j].astype(f32)
        mn = ADAM_B1 * m_ref[...] + (1.0 - ADAM_B1) * g
        vn = ADAM_B2 * v_ref[...] + (1.0 - ADAM_B2) * (g * g)
        g_ref[...] = g
        mo_ref[...] = mn
        vo_ref[...] = vn
        d_ref[...] = (-ADAM_LR) * ((mn * c1) / (jnp.sqrt(vn * c2) + ADAM_EPS) + ADAM_WD * w_ref[...])

    blk = pl.BlockSpec((None, tr, cols), lambda i: (l, i, 0))
    prev = [] if acc is None else list(acc)
    return pl.pallas_call(
        body, name=name, grid=(rows // tr,),
        in_specs=[pl.BlockSpec((npart, tr, cols), lambda i: (0, i, 0)), blk, blk, blk] + [HBM_ANY] * len(prev),
        out_specs=[blk] * 4, out_shape=[_sds(w.shape, f32)] * 4,
        input_output_aliases={4 + j: j for j in range(len(prev))},
        compiler_params=_params(("parallel",)),
    )(parts, w, m, v, *prev)


def _pack(arrays):
    rows = []
    for a in arrays:
        flat = a.reshape(-1)
        pad = (-flat.shape[0]) % 1024
        rows.append(jnp.concatenate([flat, jnp.zeros((pad,), flat.dtype)]).reshape(-1, 128))
    return jnp.concatenate(rows, axis=0)


def _unpack(flat, shapes):
    out, r = [], 0
    for s in shapes:
        n = math.prod(s)
        nr = (n + 1023) // 1024 * 8
        out.append(flat[r:r + nr].reshape(-1)[:n].reshape(s))
        r += nr
    return out


BIG_SHARD_AXIS = dict(w_in=2, w_branch=3, w_out=1, w_ffn_in=2, w_ffn_out=1)
SHARDED_SMALL = ("attn_rel_bias", "lru_conv_w")


def _to_blocks(full, axis):
    s = full.shape
    cut = full.reshape(s[:axis] + (NDEV, s[axis] // NDEV) + s[axis + 1:])
    return jnp.moveaxis(cut, axis, 0)


def _from_blocks(blocks, axis):
    moved = jnp.moveaxis(blocks, 0, axis)
    s = moved.shape
    return moved.reshape(s[:axis] + (s[axis] * s[axis + 1],) + s[axis + 2:])


def _flat2(a):
    return a.reshape(-1, a.shape[-1])


def _my_slice(a, n):
    ax, ay, ac = _mesh_pos()
    return lax.dynamic_slice_in_dim(a, (4 * ax + 2 * ay + ac) * n, n, axis=a.ndim - 1)


_WEIGHTS = ("norm_mix_pre", "norm_mix_post", "norm_ffn_pre", "norm_ffn_post", "w_in", "attn_rel_bias", "hgrn_lb_logits",
            "hgrn_norm_g", "gmlp_norm_g", "gmlp_ws", "gmlp_bs", "lru_conv_w", "lru_conv_b", "lru_wa", "lru_ba", "lru_wx",
            "lru_bx", "lru_lambda", "w_branch", "w_out", "w_ffn_in", "w_ffn_out")


def _step(x, loss_target, w, m, v):
    gathered = []
    for l in range(DEPTH):
        gathered.append(tuple(_gather_sc([w[k][l].astype(bf16) for k in keys], "gather_%s%d" % (half, l))
                              for half, keys in (("mix", MIX_BIG), ("ffn", FFN_BIG))))
    cut = jnp.concatenate([w[k] for k in SHARDED_SMALL], axis=-1)
    parts = _all_gather(_pack([cut]), "gather_small").reshape(NDEV, -1)[:, :math.prod(cut.shape)].reshape((NDEV,) + cut.shape)
    small = {k: w[k] for k in SMALL if k not in SHARDED_SMALL}
    at = 0
    for k in SHARDED_SMALL:
        n = w[k].shape[-1]
        small[k] = _from_blocks(parts[..., at:at + n], 2)
        at += n
    loss, dx, layers = _step_forward(x, loss_target, gathered, small)
    flat3 = lambda a: a.reshape((DEPTH, -1, a.shape[-1]))
    acc = {k: None for k in BIG}
    smalls = [None] * DEPTH

    def send(grads, keys, dx, name):
        handles, token = _exchange_start([_to_blocks(grads[k], BIG_SHARD_AXIS[k] - 1) for k in keys], "start_" + name)
        _, dx = lax.optimization_barrier((token, dx))
        return (keys, handles, "wait_" + name), dx

    def update(sent, l, after):
        keys, handles, name = sent
        got = dict(zip(keys, _exchange_wait(handles, after, name)))
        for k, g in got.items():
            w3 = flat3(w[k])
            acc[k] = _adamw_layer(g.reshape((NDEV,) + w3.shape[1:]), w3, flat3(m[k]), flat3(v[k]), acc[k], l,
                                  "adamw_%s_%d" % (k, l))

    waiting = []
    for l in range(DEPTH - 1, -1, -1):
        sent_ffn = []

        def ffn_grads_ready(grads, dx1, l=l, sent_ffn=sent_ffn):
            sent, dx1 = send(grads, FFN_BIG, dx1, "exchange_ffn%d" % l)
            sent_ffn.append(sent)
            while waiting:
                update(*waiting.pop(), dx1)
            return dx1

        dx, gbig, smalls[l] = _step_backward(dx, layers[l], ffn_grads_ready)
        sent_mix, dx = send(gbig, MIX_BIG, dx, "exchange_mix%d" % l)
        update(sent_ffn[0], l, dx)
        waiting.append((sent_mix, l))
    update(*waiting.pop(), dx)
    grads, deltas, new_m, new_v = {}, {}, {}, {}
    for k in BIG:
        grads[k], deltas[k], new_m[k], new_v[k] = (o.reshape(w[k].shape) for o in acc[k])
    gsmall = {k: jnp.stack([smalls[l][k] for l in range(DEPTH)]) for k in smalls[0]}
    gsmall["hgrn_lb_logits"] = _lb_bwd(small["hgrn_lb_logits"], gsmall.pop("lb"))
    shapes = [gsmall[k].shape for k in SMALL]
    sums = _unpack(_sum_parts(_all_gather(_pack([gsmall[k] for k in SMALL]), "gather_small_grads"), "sum_small_grads"), shapes)
    gs = dict(zip(SMALL, sums))
    for k in SHARDED_SMALL:
        gs[k] = _my_slice(gs[k], w[k].shape[-1])
    packed = [_pack([d[k] for k in SMALL]) for d in (gs, w, m, v)]
    outs = _adamw(packed[0][None], packed[1], packed[2], packed[3], "adamw_small")
    shapes = [w[k].shape for k in SMALL]
    for d, o in zip((grads, deltas, new_m, new_v), outs):
        d.update(zip(SMALL, _unpack(o, shapes)))
    total = lax.psum(loss[0, 0], ("x", "y", "c"))
    return total, dx[None], grads, deltas, new_m, new_v


def _step_forward(x, loss_target, gathered, small):
    lbs = _lb_fwd(small["hgrn_lb_logits"])
    x = x[0]
    layers = []

    def weights(blocks, keys, after):
        if after is not None:
            blocks, _ = lax.optimization_barrier((blocks, after))
        return {k: _from_blocks(g, BIG_SHARD_AXIS[k] - 1) for k, g in zip(keys, blocks)}

    for l in range(DEPTH):
        mix, ffn = gathered[l]
        p = _layer_params(l, weights(mix, MIX_BIG, x if l else None), small, lbs)
        x, sv = _layer_fwd(x, p, lambda x1, ffn=ffn: _ffn_weights(weights(ffn, FFN_BIG, x1)))
        layers.append((p, sv))
    loss, dx = _loss_head(x, loss_target[0])
    return loss, dx, layers


def _step_backward(dx, layer, ffn_grads_ready):
    return _layer_bwd(dx, *layer, ffn_grads_ready)


def kernel(x, norm_mix_pre, norm_mix_post, norm_ffn_pre, norm_ffn_post, w_in, attn_rel_bias, hgrn_lb_logits, hgrn_norm_g, gmlp_norm_g, gmlp_ws, gmlp_bs, lru_conv_w, lru_conv_b, lru_wa, lru_ba, lru_wx, lru_bx, lru_lambda, w_branch, w_out, w_ffn_in, w_ffn_out, loss_target, m_norm_mix_pre, m_norm_mix_post, m_norm_ffn_pre, m_norm_ffn_post, m_w_in, m_attn_rel_bias, m_hgrn_lb_logits, m_hgrn_norm_g, m_gmlp_norm_g, m_gmlp_ws, m_gmlp_bs, m_lru_conv_w, m_lru_conv_b, m_lru_wa, m_lru_ba, m_lru_wx, m_lru_bx, m_lru_lambda, m_w_branch, m_w_out, m_w_ffn_in, m_w_ffn_out, v_norm_mix_pre, v_norm_mix_post, v_norm_ffn_pre, v_norm_ffn_post, v_w_in, v_attn_rel_bias, v_hgrn_lb_logits, v_hgrn_norm_g, v_gmlp_norm_g, v_gmlp_ws, v_gmlp_bs, v_lru_conv_w, v_lru_conv_b, v_lru_wa, v_lru_ba, v_lru_wx, v_lru_bx, v_lru_lambda, v_w_branch, v_w_out, v_w_ffn_in, v_w_ffn_out):
    w = dict(zip(_WEIGHTS, (norm_mix_pre, norm_mix_post, norm_ffn_pre, norm_ffn_post, w_in, attn_rel_bias, hgrn_lb_logits, hgrn_norm_g, gmlp_norm_g, gmlp_ws, gmlp_bs, lru_conv_w, lru_conv_b, lru_wa, lru_ba, lru_wx, lru_bx, lru_lambda, w_branch, w_out, w_ffn_in, w_ffn_out)))
    m = dict(zip(_WEIGHTS, (m_norm_mix_pre, m_norm_mix_post, m_norm_ffn_pre, m_norm_ffn_post, m_w_in, m_attn_rel_bias, m_hgrn_lb_logits, m_hgrn_norm_g, m_gmlp_norm_g, m_gmlp_ws, m_gmlp_bs, m_lru_conv_w, m_lru_conv_b, m_lru_wa, m_lru_ba, m_lru_wx, m_lru_bx, m_lru_lambda, m_w_branch, m_w_out, m_w_ffn_in, m_w_ffn_out)))
    v = dict(zip(_WEIGHTS, (v_norm_mix_pre, v_norm_mix_post, v_norm_ffn_pre, v_norm_ffn_post, v_w_in, v_attn_rel_bias, v_hgrn_lb_logits, v_hgrn_norm_g, v_gmlp_norm_g, v_gmlp_ws, v_gmlp_bs, v_lru_conv_w, v_lru_conv_b, v_lru_wa, v_lru_ba, v_lru_wx, v_lru_bx, v_lru_lambda, v_w_branch, v_w_out, v_w_ffn_in, v_w_ffn_out)))
    loss, grad_x, grads, deltas, new_m, new_v = _step(x, loss_target, w, m, v)
    return (loss, grad_x, *[grads[k] for k in _WEIGHTS], *[deltas[k] for k in _WEIGHTS],
            *[new_m[k] for k in _WEIGHTS], *[new_v[k] for k in _WEIGHTS])
```

```python
import math

import jax
import jax.numpy as jnp
from jax import lax
from jax.experimental import pallas as pl
from jax.experimental.pallas import tpu as pltpu
from jax.experimental.pallas import tpu_sc as plsc

f32 = jnp.float32
bf16 = jnp.bfloat16

SEQ = 2048
DM = 1024
DEPTH = 4
NDEV = 8
MIXW = 256
NHEAD = 4
HDIM = 64
NMIX = 11 * MIXW
NGATE = 4 * DM
FFH = 2816
EPS = 1e-6
NEG_BIG = -1e30
LOG_FLOOR = 1e-30
LRU_C = 8.0
REL_SIZE = 320
ATT_PAIR = 128
ATT_BAND = 640
ATT_PAD = 512
ATT_WV = 768
HG_T = 16
HG_N = SEQ // HG_T
GM_T = 128
LRU_T = 128
ADAM_LR, ADAM_B1, ADAM_B2, ADAM_EPS, ADAM_WD, ADAM_STEP = 0.001, 0.9, 0.999, 1e-8, 0.01, 10
V7X_VMEM_LIMIT = 56 * 1024 * 1024
GELU_C0 = math.sqrt(2.0 / math.pi)
GELU_C1 = 0.044715
MESH_ID = pl.DeviceIdType.MESH


def _params(sem=None):
    if sem is None:
        return pltpu.CompilerParams(vmem_limit_bytes=V7X_VMEM_LIMIT)
    return pltpu.CompilerParams(dimension_semantics=sem, vmem_limit_bytes=V7X_VMEM_LIMIT)


def _sds(shape, dtype):
    return jax.ShapeDtypeStruct(shape, dtype)


def _dot(a, b):
    return jnp.dot(a.astype(bf16), b.astype(bf16), preferred_element_type=f32)


def _dot_nt(a, b):
    return lax.dot_general(a.astype(bf16), b.astype(bf16), (((1,), (1,)), ((), ())), preferred_element_type=f32)


def _dot_tn(a, b):
    return lax.dot_general(a.astype(bf16), b.astype(bf16), (((0,), (0,)), ((), ())), preferred_element_type=f32)


def _split(a):
    hi = a.astype(bf16)
    lo = (a - hi.astype(f32)).astype(bf16)
    return hi, lo


def _dot_hl(a, m):
    hi, lo = _split(a)
    return jnp.dot(hi, m, preferred_element_type=f32) + jnp.dot(lo, m, preferred_element_type=f32)


def _dot_nt_hl(m, a):
    hi, lo = _split(a)
    dn = (((1,), (1,)), ((), ()))
    return lax.dot_general(m, hi, dn, preferred_element_type=f32) + lax.dot_general(m, lo, dn, preferred_element_type=f32)


def _sigmoid(x):
    return jax.nn.sigmoid(x)


def _silu(x):
    return x * _sigmoid(x)


def _dsilu(x):
    s = _sigmoid(x)
    return s * (1.0 + x * (1.0 - s))


def _gelu(x):
    return 0.5 * x * (1.0 + jnp.tanh(GELU_C0 * (x + GELU_C1 * x * x * x)))


def _dgelu(x):
    t = jnp.tanh(GELU_C0 * (x + GELU_C1 * x * x * x))
    return 0.5 * (1.0 + t) + 0.5 * x * (1.0 - t * t) * GELU_C0 * (1.0 + 3.0 * GELU_C1 * x * x)


def _rms(x, g):
    r = lax.rsqrt(jnp.mean(x * x, axis=-1, keepdims=True) + EPS)
    return x * r * g


def _rms_bwd(x, g, dy):
    r = lax.rsqrt(jnp.mean(x * x, axis=-1, keepdims=True) + EPS)
    xh = x * r
    dxh = dy * g
    dx = r * (dxh - xh * jnp.mean(dxh * xh, axis=-1, keepdims=True))
    return dx, jnp.sum(dy * xh, axis=0, keepdims=True)


def _same_head(n, width, dtype):
    r = lax.broadcasted_iota(jnp.int32, (n, n), 0) // width
    c = lax.broadcasted_iota(jnp.int32, (n, n), 1) // width
    return (r == c).astype(dtype)


def _head_masks(rows=1):
    lane = lax.broadcasted_iota(jnp.int32, (rows, MIXW), 1) // HDIM
    return [lane == h for h in range(NHEAD)]


def _norm_matmul(x, g, w, tn):
    n = w.shape[1]
    tm = 1024

    def body(x_ref, g_ref, w_ref, z_ref, h_ref):
        @pl.when(pl.program_id(1) == 0)
        def _():
            h_ref[...] = _rms(x_ref[...], g_ref[...]).astype(bf16)

        z_ref[...] = jnp.dot(h_ref[...], w_ref[...], preferred_element_type=f32)

    return pl.pallas_call(
        body, name="norm_matmul", grid=(SEQ // tm, n // tn),
        in_specs=[pl.BlockSpec((tm, DM), lambda i, j: (i, 0)), pl.BlockSpec((1, DM), lambda i, j: (0, 0)),
                  pl.BlockSpec((DM, tn), lambda i, j: (0, j))],
        out_specs=[pl.BlockSpec((tm, tn), lambda i, j: (i, j)), pl.BlockSpec((tm, DM), lambda i, j: (i, 0))],
        out_shape=[_sds((SEQ, n), f32), _sds((SEQ, DM), bf16)],
        compiler_params=_params(("parallel", "arbitrary")),
    )(x, g, w)


def _matmul(a, w, tn):
    k, n = w.shape
    tm = 1024

    def body(a_ref, w_ref, z_ref):
        z_ref[...] = jnp.dot(a_ref[...], w_ref[...], preferred_element_type=f32)

    return pl.pallas_call(
        body, name="matmul", grid=(SEQ // tm, n // tn),
        in_specs=[pl.BlockSpec((tm, k), lambda i, j: (i, 0)), pl.BlockSpec((k, tn), lambda i, j: (0, j))],
        out_specs=pl.BlockSpec((tm, tn), lambda i, j: (i, j)),
        out_shape=_sds((SEQ, n), f32),
        compiler_params=_params(("parallel", "arbitrary")),
    )(a, w)


def _att_offset_map():
    i = lax.broadcasted_iota(jnp.int32, (REL_SIZE, ATT_WV), 0)
    t = lax.broadcasted_iota(jnp.int32, (REL_SIZE, ATT_WV), 1)
    e = jnp.where(t <= ATT_BAND, t, t - ATT_WV)
    idx = jnp.clip(ATT_PAD - e, -(HDIM - 1), 256) + (HDIM - 1)
    return (idx == i).astype(bf16)


def _att_band_valid():
    qc = lax.broadcasted_iota(jnp.int32, (ATT_PAIR, ATT_BAND), 0) // HDIM
    kc = lax.broadcasted_iota(jnp.int32, (ATT_PAIR, ATT_BAND), 1) // HDIM
    return (kc >= qc) & (kc <= qc + 8)


def _att_bias_tiles(rb_ref, bm_ref):
    wv = _dot_hl(rb_ref[...], _att_offset_map())
    valid = _att_band_valid()
    for h in range(NHEAD):
        rows = jnp.broadcast_to(wv[h:h + 1, :], (ATT_PAIR, ATT_WV))
        tile = pltpu.roll(rows, 0, 1, stride=1, stride_axis=0)[:, :ATT_BAND]
        bm_ref[h] = jnp.where(valid, tile, NEG_BIG)


def _att_pad_kv(k_ref, v_ref, kp_ref, vp_ref):
    kp_ref[pl.ds(0, ATT_PAD), :] = jnp.zeros((ATT_PAD, MIXW), bf16)
    vp_ref[pl.ds(0, ATT_PAD), :] = jnp.zeros((ATT_PAD, MIXW), bf16)
    kp_ref[pl.ds(ATT_PAD, SEQ), :] = k_ref[...].astype(bf16)
    vp_ref[pl.ds(ATT_PAD, SEQ), :] = v_ref[...].astype(bf16)


def _att_probs(qm, kb, bm, key_ok):
    s = _dot_nt(qm, kb) + bm
    s = jnp.where(key_ok, s, NEG_BIG)
    m = jnp.max(s, axis=-1, keepdims=True)
    e = jnp.exp(s - m)
    return e / jnp.sum(e, axis=-1, keepdims=True)


def _attn_fwd(zm, rb8):
    def body(q_ref, k_ref, v_ref, rb_ref, o_ref, kp_ref, vp_ref, bm_ref):
        _att_pad_kv(k_ref, v_ref, kp_ref, vp_ref)
        _att_bias_tiles(rb_ref, bm_ref)
        hm = _head_masks()

        def pair(p, carry):
            r0 = pl.multiple_of(p * ATT_PAIR, ATT_PAIR)
            q = q_ref[pl.ds(r0, ATT_PAIR), :] * (HDIM ** -0.5)
            kb = kp_ref[pl.ds(r0, ATT_BAND), :]
            vb = vp_ref[pl.ds(r0, ATT_BAND), :]
            key_ok = (lax.broadcasted_iota(jnp.int32, (1, ATT_BAND), 1) + (r0 - ATT_PAD)) >= 0
            o = jnp.zeros((ATT_PAIR, MIXW), f32)
            for h in range(NHEAD):
                qm = jnp.where(hm[h], q, 0.0)
                p_h = _att_probs(qm, kb, bm_ref[h], key_ok)
                o = o + jnp.where(hm[h], _dot(p_h, vb), 0.0)
            o_ref[pl.ds(r0, ATT_PAIR), :] = o.astype(bf16)
            return carry

        lax.fori_loop(0, SEQ // ATT_PAIR, pair, 0)

    col = lambda j: pl.BlockSpec((SEQ, MIXW), lambda i: (0, j))
    return pl.pallas_call(
        body, name="attn_fwd", grid=(1,),
        in_specs=[col(0), col(1), col(2), pl.BlockSpec((8, REL_SIZE), lambda i: (0, 0))],
        out_specs=pl.BlockSpec((SEQ, MIXW), lambda i: (0, 0)),
        out_shape=_sds((SEQ, MIXW), bf16),
        scratch_shapes=[pltpu.VMEM((SEQ + ATT_PAD, MIXW), bf16), pltpu.VMEM((SEQ + ATT_PAD, MIXW), bf16),
                        pltpu.VMEM((NHEAD, ATT_PAIR, ATT_BAND), f32)],
        compiler_params=_params(("arbitrary",)),
    )(zm, zm, zm, rb8)


def _hg_gates(q, fz, lb):
    sq = _sigmoid(q)
    sg = _sigmoid(fz)
    f = lb + (1.0 - lb) * sg
    return q * sq, (1.0 - lb) * (1.0 - sg), jnp.log(jnp.maximum(f, LOG_FLOOR)), sq, sg, f


def _hg_prepare(q_ref, f_ref, lb, qf_s, kf_s, b_s, qd_s, kd_s, dec_s):
    b = None
    for t in range(HG_T):
        qf, kf, lf, _, _, _ = _hg_gates(q_ref[:, t, :], f_ref[:, t, :], lb)
        b = lf if b is None else b + lf
        qf_s[:, t, :] = qf
        kf_s[:, t, :] = kf
        b_s[:, t, :] = b
    b_last = b
    dec_s[...] = jnp.broadcast_to(jnp.exp(b_last)[:, None, :], (HG_N, 8, MIXW))
    for t in range(HG_T):
        bt = b_s[:, t, :]
        qd_s[:, t, :] = qf_s[:, t, :] * jnp.exp(bt)
        kd_s[:, t, :] = kf_s[:, t, :] * jnp.exp(b_last - bt)


def _hg_scores(t, qf_s, kf_s, b_s, w_s, hm):
    qt = qf_s[:, t, :]
    bt = b_s[:, t, :]
    for s in range(t + 1):
        w = qt * kf_s[:, s, :]
        if s < t:
            w = w * jnp.exp(bt - b_s[:, s, :])
        w_s[pl.ds(s * HG_N, HG_N), :] = w.astype(bf16)
    return jnp.dot(w_s[pl.ds(0, (t + 1) * HG_N), :], hm, preferred_element_type=f32)


def _hgrn_fwd(zm3, lb, ng):
    def body(q_ref, f_ref, i_ref, g_ref, lb_ref, ng_ref, o_ref, oraw_ref, states_ref,
             qf_s, kf_s, b_s, qd_s, kd_s, dec_s, w_s, st_s):
        lb = lb_ref[...]
        hm = _same_head(MIXW, HDIM, bf16)
        hmf = _same_head(MIXW, HDIM, f32)
        _hg_prepare(q_ref, f_ref, lb, qf_s, kf_s, b_s, qd_s, kd_s, dec_s)
        for t in range(HG_T):
            p = _hg_scores(t, qf_s, kf_s, b_s, w_s, hm)
            acc = jnp.zeros((HG_N, MIXW), f32)
            for s in range(t + 1):
                acc = acc + p[s * HG_N:(s + 1) * HG_N] * i_ref[:, s, :]
            oraw_ref[:, t, :] = acc
        st_s[...] = jnp.zeros((MIXW, MIXW), f32)

        def step(n, carry):
            st = st_s[...]
            stb = st.astype(bf16)
            states_ref[n] = stb
            oraw_ref[n] = oraw_ref[n] + _dot_nt(qd_s[n], stb)
            st_s[...] = st * dec_s[n][0:1] + _dot_tn(i_ref[n], kd_s[n]) * hmf
            return carry

        lax.fori_loop(0, HG_N, step, 0)
        ngv = ng_ref[...]
        for t in range(HG_T):
            o = oraw_ref[:, t, :]
            ms = _dot_hl(o * o, hm) * (1.0 / HDIM)
            o_ref[:, t, :] = (o * lax.rsqrt(ms + EPS) * ngv * _silu(g_ref[:, t, :])).astype(bf16)

    one = pl.Buffered(1)
    col = lambda j: pl.BlockSpec((HG_N, HG_T, MIXW), lambda i: (0, 0, j), pipeline_mode=one)
    vec = pl.BlockSpec((1, MIXW), lambda i: (0, 0))
    blk = pl.BlockSpec((HG_N, HG_T, MIXW), lambda i: (0, 0, 0))
    s3 = pltpu.VMEM((HG_N, HG_T, MIXW), f32)
    return pl.pallas_call(
        body, name="hgrn_fwd", grid=(1,),
        in_specs=[col(3), col(4), col(5), col(6), vec, vec],
        out_specs=[blk, blk, pl.BlockSpec((HG_N, MIXW, MIXW), lambda i: (0, 0, 0), pipeline_mode=one)],
        out_shape=[_sds((HG_N, HG_T, MIXW), bf16), _sds((HG_N, HG_T, MIXW), f32), _sds((HG_N, MIXW, MIXW), bf16)],
        scratch_shapes=[s3, s3, s3, s3, s3, pltpu.VMEM((HG_N, 8, MIXW), f32),
                        pltpu.VMEM((HG_T * HG_N, MIXW), bf16), pltpu.VMEM((MIXW, MIXW), f32)],
        compiler_params=_params(("arbitrary",)),
    )(zm3, zm3, zm3, zm3, lb, ng)


def _gm_weights(ws_ref):
    tril = lax.broadcasted_iota(jnp.int32, (GM_T, GM_T), 0) >= lax.broadcasted_iota(jnp.int32, (GM_T, GM_T), 1)
    return tril, [jnp.where(tril, ws_ref[g], 0.0).astype(bf16) for g in range(NHEAD)]


def _gm_expand():
    r = lax.broadcasted_iota(jnp.int32, (8, MIXW), 0)
    c = lax.broadcasted_iota(jnp.int32, (8, MIXW), 1) // HDIM
    return (r == c).astype(bf16)


def _gm_mixed(vn, wts, bias, hm):
    vb = vn.astype(bf16)
    mixed = bias
    for g in range(NHEAD):
        mixed = mixed + jnp.where(hm[g], jnp.dot(wts[g], vb, preferred_element_type=f32), 0.0)
    return mixed


def _gm_bias(bs_ref):
    hi, lo = _split(bs_ref[...])
    et = _gm_expand()
    dn = (((0,), (0,)), ((), ()))
    return lax.dot_general(hi, et, dn, preferred_element_type=f32) + lax.dot_general(lo, et, dn, preferred_element_type=f32)


def _gmlp_fwd(zm, ng, ws, bs8):
    def body(u_ref, v_ref, ng_ref, ws_ref, bs_ref, o_ref):
        hm = _head_masks()
        _, wts = _gm_weights(ws_ref)
        bias = _gm_bias(bs_ref)
        ngv = ng_ref[...]

        def blk(n, carry):
            rows = pl.ds(pl.multiple_of(n * GM_T, GM_T), GM_T)
            vn = _rms(_gelu(v_ref[rows, :]), ngv)
            o_ref[rows, :] = (_gelu(u_ref[rows, :]) * _gm_mixed(vn, wts, bias, hm)).astype(bf16)
            return carry

        lax.fori_loop(0, SEQ // GM_T, blk, 0)

    col = lambda j: pl.BlockSpec((SEQ, MIXW), lambda i: (0, j))
    return pl.pallas_call(
        body, name="gmlp_fwd", grid=(1,),
        in_specs=[col(7), col(8), pl.BlockSpec((1, MIXW), lambda i: (0, 0)),
                  pl.BlockSpec((NHEAD, GM_T, GM_T), lambda i: (0, 0, 0)), pl.BlockSpec((8, GM_T), lambda i: (0, 0))],
        out_specs=pl.BlockSpec((SEQ, MIXW), lambda i: (0, 0)),
        out_shape=_sds((SEQ, MIXW), bf16),
        compiler_params=_params(("arbitrary",)),
    )(zm, zm, ng, ws, bs8)


def _lru_conv(x_ref, cw_ref, cb_ref, xp_s, xc_s):
    xp_s[pl.ds(0, 8), :] = jnp.zeros((8, MIXW), f32)
    xp_s[pl.ds(8, SEQ), :] = x_ref[...]
    cw = cw_ref[...]
    xc = cb_ref[...] + x_ref[...] * cw[3:4]
    for k in range(1, 4):
        xc = xc + xp_s[pl.ds(8 - k, SEQ), :] * cw[3 - k:4 - k]
    xc_s[...] = xc


def _lru_gates(xc, wa, ba, wx, bx, sp, first_row):
    r = _sigmoid(_dot(xc, wa) + ba)
    ig = _sigmoid(_dot(xc, wx) + bx)
    la = (-LRU_C) * r * sp
    a = jnp.exp(la)
    th = jnp.tanh(la)
    m2 = -2.0 * th / (1.0 - th)
    mult = jnp.where(first_row, 1.0, jnp.sqrt(jnp.maximum(m2, 0.0)))
    return a, mult, r, ig, m2


def _lru_scan(a, b, rev):
    row = lax.broadcasted_iota(jnp.int32, (LRU_T, 1), 0)
    k = 1
    while k < LRU_T:
        ok = (row < LRU_T - k) if rev else (row >= k)
        sh = (LRU_T - k) if rev else k
        a_sh = jnp.where(ok, pltpu.roll(a, sh, 0), 1.0)
        b_sh = jnp.where(ok, pltpu.roll(b, sh, 0), 0.0)
        b = b + a * b_sh
        a = a * a_sh
        k *= 2
    return a, b


def _lru_fwd(zm, cw8, cb, wa, ba, wx, bx, lam):
    def body(x_ref, g_ref, cw_ref, cb_ref, wa_ref, ba_ref, wx_ref, bx_ref, lam_ref, o_ref, h_ref, xp_s, xc_s):
        _lru_conv(x_ref, cw_ref, cb_ref, xp_s, xc_s)
        sp = jax.nn.softplus(-lam_ref[...])
        wa_v, wx_v, ba_v, bx_v = wa_ref[...], wx_ref[...], ba_ref[...], bx_ref[...]

        def chunk(c, h_prev):
            rows = pl.ds(pl.multiple_of(c * LRU_T, LRU_T), LRU_T)
            first = (lax.broadcasted_iota(jnp.int32, (LRU_T, 1), 0) + c * LRU_T) == 0
            xc = xc_s[rows, :]
            a, mult, _, ig, _ = _lru_gates(xc, wa_v, ba_v, wx_v, bx_v, sp, first)
            acum, hloc = _lru_scan(a, mult * (ig * xc), False)
            h = hloc + acum * h_prev
            h_ref[rows, :] = h
            o_ref[rows, :] = (h * _gelu(g_ref[rows, :])).astype(bf16)
            return h[LRU_T - 1:LRU_T, :]

        lax.fori_loop(0, SEQ // LRU_T, chunk, jnp.zeros((1, MIXW), f32))

    col = lambda j: pl.BlockSpec((SEQ, MIXW), lambda i: (0, j))
    vec = pl.BlockSpec((1, MIXW), lambda i: (0, 0))
    mat = pl.BlockSpec((MIXW, MIXW), lambda i: (0, 0))
    out = pl.BlockSpec((SEQ, MIXW), lambda i: (0, 0))
    return pl.pallas_call(
        body, name="lru_fwd", grid=(1,),
        in_specs=[col(9), col(10), pl.BlockSpec((8, MIXW), lambda i: (0, 0)), vec, mat, vec, mat, vec, vec],
        out_specs=[out, out],
        out_shape=[_sds((SEQ, MIXW), bf16), _sds((SEQ, MIXW), f32)],
        scratch_shapes=[pltpu.VMEM((SEQ + 8, MIXW), f32), pltpu.VMEM((SEQ, MIXW), f32)],
        compiler_params=_params(("arbitrary",)),
    )(zm, zm, cw8, cb, wa, ba, wx, bx, lam)


def _block_diag(w):
    out = jnp.zeros((MIXW, MIXW), w.dtype)
    for h in range(NHEAD):
        out = lax.dynamic_update_slice(out, w[h], (h * HDIM, h * HDIM))
    return out


def _diag_blocks(w):
    return jnp.stack([w[h * HDIM:(h + 1) * HDIM, h * HDIM:(h + 1) * HDIM] for h in range(NHEAD)])


ROW_TILE = 256


def _merge_fwd(outs, zg, wb, wo, x, g2):
    def body(oa_ref, ob_ref, oc_ref, od_ref, zg_ref, wb_ref, wo_ref, x_ref, g_ref, xo_ref, mg_ref, y_ref):
        merged = jnp.zeros((ROW_TILE, DM), f32)
        for n, o_ref in enumerate((oa_ref, ob_ref, oc_ref, od_ref)):
            proj = jnp.dot(o_ref[...], wb_ref[n], preferred_element_type=f32)
            merged = merged + _sigmoid(zg_ref[:, n * DM:(n + 1) * DM]) * proj
        mb = merged.astype(bf16)
        y = jnp.dot(mb, wo_ref[...], preferred_element_type=f32)
        mg_ref[...] = mb
        y_ref[...] = y
        xo_ref[...] = x_ref[...] + _rms(y, g_ref[...])

    row = lambda w: pl.BlockSpec((ROW_TILE, w), lambda i: (i, 0))
    return pl.pallas_call(
        body, name="merge_fwd", grid=(SEQ // ROW_TILE,),
        in_specs=[row(MIXW)] * 4 + [row(NGATE), pl.BlockSpec((NHEAD, MIXW, DM), lambda i: (0, 0, 0)),
                                    pl.BlockSpec((DM, DM), lambda i: (0, 0)), row(DM), pl.BlockSpec((1, DM), lambda i: (0, 0))],
        out_specs=[row(DM), row(DM), row(DM)],
        out_shape=[_sds((SEQ, DM), f32), _sds((SEQ, DM), bf16), _sds((SEQ, DM), f32)],
        compiler_params=_params(("parallel",)),
    )(*outs, zg, wb, wo, x, g2)


def _ffn_out(u, w2, x, g4):
    def body(u_ref, w_ref, x_ref, g_ref, xo_ref, f_ref):
        a = _silu(u_ref[:, :FFH]) * u_ref[:, FFH:]
        f = jnp.dot(a.astype(bf16), w_ref[...], preferred_element_type=f32)
        f_ref[...] = f
        xo_ref[...] = x_ref[...] + _rms(f, g_ref[...])

    row = lambda w: pl.BlockSpec((ROW_TILE, w), lambda i: (i, 0))
    return pl.pallas_call(
        body, name="ffn_out", grid=(SEQ // ROW_TILE,),
        in_specs=[row(2 * FFH), pl.BlockSpec((FFH, DM), lambda i: (0, 0)), row(DM), pl.BlockSpec((1, DM), lambda i: (0, 0))],
        out_specs=[row(DM), row(DM)],
        out_shape=[_sds((SEQ, DM), f32), _sds((SEQ, DM), f32)],
        compiler_params=_params(("parallel",)),
    )(u, w2, x, g4)


def _loss_head(x, tgt):
    tm = 512

    def body(x_ref, t_ref, l_ref, dx_ref):
        @pl.when(pl.program_id(0) == 0)
        def _():
            l_ref[...] = jnp.zeros((1, 1), f32)

        d = x_ref[...] - t_ref[...]
        dx_ref[...] = d * (1.0 / DM)
        l_ref[...] += (0.5 / DM) * jnp.sum(d * d).reshape(1, 1)

    row = pl.BlockSpec((tm, DM), lambda i: (i, 0))
    return pl.pallas_call(
        body, name="loss_head", grid=(SEQ // tm,),
        in_specs=[row, row], out_specs=[pl.BlockSpec((1, 1), lambda i: (0, 0)), row],
        out_shape=[_sds((1, 1), f32), _sds((SEQ, DM), f32)],
        compiler_params=_params(("arbitrary",)),
    )(x, tgt)


def _lb_fwd(logits):
    def body(lg_ref, o_ref):
        lg = lg_ref[...]
        e = jnp.exp(lg - jnp.max(lg, axis=0, keepdims=True))
        p = e / jnp.sum(e, axis=0, keepdims=True)
        acc = jnp.zeros((1, MIXW), f32)
        o_ref[0:1, :] = acc
        for l in range(1, DEPTH):
            acc = acc + p[l:l + 1]
            o_ref[l:l + 1, :] = acc

    return pl.pallas_call(body, name="lb_fwd", out_shape=_sds((DEPTH, MIXW), f32))(logits)


def _lb_bwd(logits, dlbs):
    def body(lg_ref, d_ref, o_ref):
        lg = lg_ref[...]
        e = jnp.exp(lg - jnp.max(lg, axis=0, keepdims=True))
        p = e / jnp.sum(e, axis=0, keepdims=True)
        d = d_ref[...]
        dp = [jnp.zeros((1, MIXW), f32)] * DEPTH
        acc = jnp.zeros((1, MIXW), f32)
        for j in range(DEPTH - 1, 0, -1):
            acc = acc + d[j:j + 1]
            dp[j] = acc
        inner = sum(p[j:j + 1] * dp[j] for j in range(DEPTH))
        for j in range(DEPTH):
            o_ref[j:j + 1, :] = p[j:j + 1] * (dp[j] - inner)

    return pl.pallas_call(body, name="lb_bwd", out_shape=_sds((DEPTH, MIXW), f32))(logits, dlbs)


def _pad_rows(a, rows=8):
    return jnp.concatenate([a, jnp.zeros((rows - a.shape[0], a.shape[1]), a.dtype)], axis=0)


def _layer_params(l, full, small, lbs):
    row = lambda name: small[name][l][None]
    return dict(
        _mix_weights(full), **(_ffn_weights(full) if "w_ffn_in" in full else {}),
        g1=row("norm_mix_pre"), g2=row("norm_mix_post"), g3=row("norm_ffn_pre"), g4=row("norm_ffn_post"),
        rb8=_pad_rows(small["attn_rel_bias"][l]), lb=lbs[l][None], hng=row("hgrn_norm_g"),
        gng=row("gmlp_norm_g"), gws=small["gmlp_ws"][l], gbs8=_pad_rows(small["gmlp_bs"][l]),
        cw8=_pad_rows(small["lru_conv_w"][l]), cb=row("lru_conv_b"),
        wa=_block_diag(small["lru_wa"][l]).astype(bf16), ba=row("lru_ba"),
        wx=_block_diag(small["lru_wx"][l]).astype(bf16), bx=row("lru_bx"), lam=row("lru_lambda"),
    )


def _mix_weights(full):
    return dict(wm=full["w_in"][:, :NMIX], wgt=full["w_in"][:, NMIX:], wb=full["w_branch"], wo=full["w_out"])


def _ffn_weights(full):
    return dict(w1=full["w_ffn_in"], w2=full["w_ffn_out"])


def _layer_fwd(x, p, late_ffn_weights=None):
    zm, h = _norm_matmul(x, p["g1"], p["wm"], 1408)
    zg = _matmul(h, p["wgt"], 1024)
    oa = _attn_fwd(zm, p["rb8"])
    ob3, obraw3, hstates = _hgrn_fwd(zm.reshape(HG_N, HG_T, NMIX), p["lb"], p["hng"])
    oc = _gmlp_fwd(zm, p["gng"], p["gws"], p["gbs8"])
    od, hd = _lru_fwd(zm, p["cw8"], p["cb"], p["wa"], p["ba"], p["wx"], p["bx"], p["lam"])
    outs = (oa, ob3.reshape(SEQ, MIXW), oc, od)
    x1, merged, y = _merge_fwd(outs, zg, p["wb"], p["wo"], x, p["g2"])
    if late_ffn_weights is not None:
        p.update(late_ffn_weights(x1))
    u, h2 = _norm_matmul(x1, p["g3"], p["w1"], 1408)
    x2, f = _ffn_out(u, p["w2"], x1, p["g4"])
    saved = dict(x=x, h=h, zm=zm, zg=zg, outs=outs, obraw3=obraw3, hstates=hstates, hd=hd, x1=x1, merged=merged, y=y, u=u, h2=h2, f=f)
    return x2, saved


def _att_bias_grad(db_ref, o_ref):
    r = lax.broadcasted_iota(jnp.int32, (ATT_PAIR, ATT_PAIR), 0)
    c = lax.broadcasted_iota(jnp.int32, (ATT_PAIR, ATT_PAIR), 1)
    flip = (r + c == ATT_PAIR - 1).astype(bf16)
    rows = []
    for h in range(NHEAD):
        d = jnp.concatenate([db_ref[h], jnp.zeros((ATT_PAIR, ATT_WV - ATT_BAND), f32)], axis=1)
        hi, lo = _split(d)
        rev = jnp.dot(flip, hi, preferred_element_type=f32) + jnp.dot(flip, lo, preferred_element_type=f32)
        lined = pltpu.roll(rev, ATT_WV - (ATT_PAIR - 1), 1, stride=1, stride_axis=0)
        rows.append(jnp.sum(lined, axis=0, keepdims=True))
    dwv = jnp.concatenate(rows + [jnp.zeros((8 - NHEAD, ATT_WV), f32)], axis=0)
    hi, lo = _split(dwv)
    m = _att_offset_map()
    dn = (((1,), (1,)), ((), ()))
    o_ref[...] = lax.dot_general(hi, m, dn, preferred_element_type=f32) + lax.dot_general(lo, m, dn, preferred_element_type=f32)


def _attn_bwd(zm, rb8, do):
    def body(q_ref, k_ref, v_ref, rb_ref, do_ref, dz_ref, drb_ref, kp_ref, vp_ref, bm_ref, dk_s, dv_s, db_s):
        _att_pad_kv(k_ref, v_ref, kp_ref, vp_ref)
        _att_bias_tiles(rb_ref, bm_ref)
        dk_s[...] = jnp.zeros_like(dk_s)
        dv_s[...] = jnp.zeros_like(dv_s)
        db_s[...] = jnp.zeros_like(db_s)
        hm = _head_masks()
        scale = HDIM ** -0.5

        def pair(p, carry):
            r0 = pl.multiple_of(p * ATT_PAIR, ATT_PAIR)
            q = q_ref[pl.ds(r0, ATT_PAIR), :] * scale
            dout = do_ref[pl.ds(r0, ATT_PAIR), :]
            kb = kp_ref[pl.ds(r0, ATT_BAND), :]
            vb = vp_ref[pl.ds(r0, ATT_BAND), :]
            key_ok = (lax.broadcasted_iota(jnp.int32, (1, ATT_BAND), 1) + (r0 - ATT_PAD)) >= 0
            dq = jnp.zeros((ATT_PAIR, MIXW), f32)
            dkb = jnp.zeros((ATT_BAND, MIXW), f32)
            dvb = jnp.zeros((ATT_BAND, MIXW), f32)
            for h in range(NHEAD):
                qm = jnp.where(hm[h], q, 0.0).astype(bf16)
                dom = jnp.where(hm[h], dout, 0.0).astype(bf16)
                p_h = _att_probs(qm, kb, bm_ref[h], key_ok)
                dp = _dot_nt(dom, vb)
                ds = p_h * (dp - jnp.sum(dp * p_h, axis=-1, keepdims=True))
                dsb = ds.astype(bf16)
                dq = dq + jnp.where(hm[h], _dot(dsb, kb), 0.0)
                dkb = dkb + _dot_tn(dsb, qm)
                dvb = dvb + _dot_tn(p_h, dom)
                db_s[h] = db_s[h] + ds
            dz_ref[pl.ds(r0, ATT_PAIR), 0:MIXW] = (dq * scale).astype(bf16)
            dk_s[pl.ds(r0, ATT_BAND), :] = dk_s[pl.ds(r0, ATT_BAND), :] + dkb
            dv_s[pl.ds(r0, ATT_BAND), :] = dv_s[pl.ds(r0, ATT_BAND), :] + dvb
            return carry

        lax.fori_loop(0, SEQ // ATT_PAIR, pair, 0)
        dz_ref[:, MIXW:2 * MIXW] = dk_s[pl.ds(ATT_PAD, SEQ), :].astype(bf16)
        dz_ref[:, 2 * MIXW:3 * MIXW] = dv_s[pl.ds(ATT_PAD, SEQ), :].astype(bf16)
        _att_bias_grad(db_s, drb_ref)

    col = lambda j: pl.BlockSpec((SEQ, MIXW), lambda i: (0, j))
    return pl.pallas_call(
        body, name="attn_bwd", grid=(1,),
        in_specs=[col(0), col(1), col(2), pl.BlockSpec((8, REL_SIZE), lambda i: (0, 0)), pl.BlockSpec((SEQ, MIXW), lambda i: (0, 0))],
        out_specs=[pl.BlockSpec((SEQ, 3 * MIXW), lambda i: (0, 0)), pl.BlockSpec((8, REL_SIZE), lambda i: (0, 0))],
        out_shape=[_sds((SEQ, 3 * MIXW), bf16), _sds((8, REL_SIZE), f32)],
        scratch_shapes=[pltpu.VMEM((SEQ + ATT_PAD, MIXW), bf16), pltpu.VMEM((SEQ + ATT_PAD, MIXW), bf16),
                        pltpu.VMEM((NHEAD, ATT_PAIR, ATT_BAND), f32),
                        pltpu.VMEM((SEQ + ATT_PAD, MIXW), f32), pltpu.VMEM((SEQ + ATT_PAD, MIXW), f32),
                        pltpu.VMEM((NHEAD, ATT_PAIR, ATT_BAND), f32)],
        compiler_params=_params(("arbitrary",)),
    )(zm, zm, zm, rb8, do)


def _hgrn_out_bwd(zm3, ng, oraw3, do3):
    def body(g_ref, ng_ref, o_ref, do_ref, dor_ref, dg_ref, dng_ref):
        hm = _same_head(MIXW, HDIM, bf16)
        ngv = ng_ref[...]
        dng = jnp.zeros((1, MIXW), f32)
        for t in range(HG_T):
            o, g, d = o_ref[:, t, :], g_ref[:, t, :], do_ref[:, t, :]
            rs = lax.rsqrt(_dot_hl(o * o, hm) * (1.0 / HDIM) + EPS)
            y1 = o * rs
            dy2 = d * _silu(g)
            dg_ref[:, t, :] = (d * y1 * ngv * _dsilu(g)).astype(bf16)
            dng = dng + jnp.sum(dy2 * y1, axis=0, keepdims=True)
            dy1 = dy2 * ngv
            dor_ref[:, t, :] = rs * (dy1 - y1 * (_dot_hl(dy1 * y1, hm) * (1.0 / HDIM)))
        dng_ref[...] = jnp.broadcast_to(dng, (8, MIXW))

    blk = pl.BlockSpec((HG_N, HG_T, MIXW), lambda i: (0, 0, 0))
    return pl.pallas_call(
        body, name="hgrn_out_bwd", grid=(1,),
        in_specs=[pl.BlockSpec((HG_N, HG_T, MIXW), lambda i: (0, 0, 6)), pl.BlockSpec((1, MIXW), lambda i: (0, 0)), blk, blk],
        out_specs=[blk, blk, pl.BlockSpec((8, MIXW), lambda i: (0, 0))],
        out_shape=[_sds((HG_N, HG_T, MIXW), f32), _sds((HG_N, HG_T, MIXW), bf16), _sds((8, MIXW), f32)],
        compiler_params=_params(("arbitrary",)),
    )(zm3, ng, oraw3, do3)


def _hgrn_bwd(zm3, lb, dor3, states):
    def body(q_ref, f_ref, i_ref, lb_ref, dor_ref, st_s, dz_ref, dlb_ref,
             qf_s, kf_s, b_s, dq_s, dk_s, db_s, dv_s, w_s, x_s, cur_s):
        lb = lb_ref[...]
        hm = _same_head(MIXW, HDIM, bf16)
        hmf = _same_head(MIXW, HDIM, f32)
        b = None
        for t in range(HG_T):
            qf, kf, lf, _, _, _ = _hg_gates(q_ref[:, t, :], f_ref[:, t, :], lb)
            b = lf if b is None else b + lf
            qf_s[:, t, :] = qf
            kf_s[:, t, :] = kf
            b_s[:, t, :] = b

        def block_terms(n):
            bn = b_s[n]
            bl = bn[HG_T - 1:HG_T]
            eb = jnp.exp(bn)
            ek = jnp.exp(bl - bn)
            return qf_s[n] * eb, kf_s[n] * ek, jnp.exp(bl), eb, ek

        cur_s[...] = jnp.zeros((MIXW, MIXW), f32)
        last = lax.broadcasted_iota(jnp.int32, (HG_T, 1), 0) == HG_T - 1

        def bwd_step(j, carry):
            n = HG_N - 1 - j
            qd, kd, dec, eb, ek = block_terms(n)
            v, do_n = i_ref[n], dor_ref[n]
            dst = cur_s[...]
            st = st_s[n]
            dqd = _dot(do_n, st)
            dkd = _dot(v, dst)
            ddec = jnp.sum(dst * st.astype(f32), axis=0, keepdims=True)
            cur_s[...] = dst * dec + _dot_tn(do_n, qd) * hmf
            dq_s[n] = dqd * eb
            dk_s[n] = dkd * ek
            dv_s[n] = _dot_nt(kd, dst)
            dbl = jnp.sum(dkd * kd, axis=0, keepdims=True) + ddec * dec
            db_s[n] = dqd * qd - dkd * kd + jnp.where(last, dbl, 0.0)
            return carry

        lax.fori_loop(0, HG_N, bwd_step, 0)
        for t in range(HG_T):
            qt, bt, dot_t = qf_s[:, t, :], b_s[:, t, :], dor_ref[:, t, :]
            for s in range(t + 1):
                w = qt * kf_s[:, s, :]
                if s < t:
                    w = w * jnp.exp(bt - b_s[:, s, :])
                w_s[pl.ds(s * HG_N, HG_N), :] = w.astype(bf16)
                x_s[pl.ds(s * HG_N, HG_N), :] = (dot_t * i_ref[:, s, :]).astype(bf16)
            p = jnp.dot(w_s[pl.ds(0, (t + 1) * HG_N), :], hm, preferred_element_type=f32)
            dp = jnp.dot(x_s[pl.ds(0, (t + 1) * HG_N), :], hm, preferred_element_type=f32)
            dq_t = jnp.zeros((HG_N, MIXW), f32)
            db_t = jnp.zeros((HG_N, MIXW), f32)
            for s in range(t + 1):
                ps = p[s * HG_N:(s + 1) * HG_N]
                dps = dp[s * HG_N:(s + 1) * HG_N]
                ks = kf_s[:, s, :]
                dv_s[:, s, :] = dv_s[:, s, :] + ps * dot_t
                if s < t:
                    dec_ts = jnp.exp(bt - b_s[:, s, :])
                    g1 = dps * ks * dec_ts
                    dk_s[:, s, :] = dk_s[:, s, :] + dps * qt * dec_ts
                    gw = g1 * qt
                    db_t = db_t + gw
                    db_s[:, s, :] = db_s[:, s, :] - gw
                else:
                    g1 = dps * ks
                    dk_s[:, s, :] = dk_s[:, s, :] + dps * qt
                dq_t = dq_t + g1
            dq_s[:, t, :] = dq_s[:, t, :] + dq_t
            db_s[:, t, :] = db_s[:, t, :] + db_t
        run = jnp.zeros((HG_N, MIXW), f32)
        dlb = jnp.zeros((1, MIXW), f32)
        oml = 1.0 - lb
        for t in range(HG_T - 1, -1, -1):
            run = run + db_s[:, t, :]
            q = q_ref[:, t, :]
            _, _, _, sq, sg, f = _hg_gates(q, f_ref[:, t, :], lb)
            dkf = dk_s[:, t, :]
            df = jnp.where(f > LOG_FLOOR, run / f, 0.0)
            dsg = (df - dkf) * oml
            dlb = dlb + jnp.sum((df - dkf) * (1.0 - sg), axis=0, keepdims=True)
            dz_ref[:, t, 0:MIXW] = (dq_s[:, t, :] * sq * (1.0 + q * (1.0 - sq))).astype(bf16)
            dz_ref[:, t, MIXW:2 * MIXW] = (dsg * sg * (1.0 - sg)).astype(bf16)
            dz_ref[:, t, 2 * MIXW:3 * MIXW] = dv_s[:, t, :].astype(bf16)
        dlb_ref[...] = jnp.broadcast_to(dlb, (8, MIXW))

    one = pl.Buffered(1)
    col = lambda j: pl.BlockSpec((HG_N, HG_T, MIXW), lambda i: (0, 0, j), pipeline_mode=one)
    s3 = pltpu.VMEM((HG_N, HG_T, MIXW), f32)
    return pl.pallas_call(
        body, name="hgrn_bwd", grid=(1,),
        in_specs=[col(3), col(4), col(5), pl.BlockSpec((1, MIXW), lambda i: (0, 0)),
                  pl.BlockSpec((HG_N, HG_T, MIXW), lambda i: (0, 0, 0), pipeline_mode=one),
                  pl.BlockSpec((HG_N, MIXW, MIXW), lambda i: (0, 0, 0), pipeline_mode=one)],
        out_specs=[pl.BlockSpec((HG_N, HG_T, 3 * MIXW), lambda i: (0, 0, 0)), pl.BlockSpec((8, MIXW), lambda i: (0, 0))],
        out_shape=[_sds((HG_N, HG_T, 3 * MIXW), bf16), _sds((8, MIXW), f32)],
        scratch_shapes=[s3, s3, s3, s3, s3, s3, s3,
                        pltpu.VMEM((HG_T * HG_N, MIXW), bf16), pltpu.VMEM((HG_T * HG_N, MIXW), bf16),
                        pltpu.VMEM((MIXW, MIXW), f32)],
        compiler_params=_params(("arbitrary",)),
    )(zm3, zm3, zm3, lb, dor3, states)


def _gmlp_bwd(zm, ng, ws, bs8, do):
    def body(u_ref, v_ref, ng_ref, ws_ref, bs_ref, do_ref, dz_ref, dws_ref, dng_ref, dbs_ref, dm_s):
        hm = _head_masks()
        tril, wts = _gm_weights(ws_ref)
        bias = _gm_bias(bs_ref)
        ngv = ng_ref[...]
        dws_ref[...] = jnp.zeros_like(dws_ref)
        dm_s[...] = jnp.zeros_like(dm_s)

        def blk(n, dng):
            rows = pl.ds(pl.multiple_of(n * GM_T, GM_T), GM_T)
            cu, cv, d = u_ref[rows, :], v_ref[rows, :], do_ref[rows, :]
            v = _gelu(cv)
            r = lax.rsqrt(jnp.mean(v * v, axis=-1, keepdims=True) + EPS)
            vh = v * r
            vn = vh * ngv
            u = _gelu(cu)
            dm = d * u
            dmb, vnb = dm.astype(bf16), vn.astype(bf16)
            dvn = jnp.zeros((GM_T, MIXW), f32)
            for g in range(NHEAD):
                dws_ref[g] = dws_ref[g] + _dot_nt(jnp.where(hm[g], dm, 0.0), vnb)
                dvn = dvn + jnp.where(hm[g], _dot_tn(wts[g], dmb), 0.0)
            dm_s[...] = dm_s[...] + dm
            dvh = dvn * ngv
            dv = r * (dvh - vh * jnp.mean(dvh * vh, axis=-1, keepdims=True))
            dz_ref[rows, 0:MIXW] = (d * _gm_mixed(vn, wts, bias, hm) * _dgelu(cu)).astype(bf16)
            dz_ref[rows, MIXW:2 * MIXW] = (dv * _dgelu(cv)).astype(bf16)
            return dng + jnp.sum(dvn * vh, axis=0, keepdims=True)

        dng = lax.fori_loop(0, SEQ // GM_T, blk, jnp.zeros((1, MIXW), f32))
        dng_ref[...] = jnp.broadcast_to(dng, (8, MIXW))
        for g in range(NHEAD):
            dws_ref[g] = jnp.where(tril, dws_ref[g], 0.0)
        dbs_ref[...] = _dot_nt_hl(_gm_expand(), dm_s[...])

    col = lambda j: pl.BlockSpec((SEQ, MIXW), lambda i: (0, j))
    return pl.pallas_call(
        body, name="gmlp_bwd", grid=(1,),
        in_specs=[col(7), col(8), pl.BlockSpec((1, MIXW), lambda i: (0, 0)),
                  pl.BlockSpec((NHEAD, GM_T, GM_T), lambda i: (0, 0, 0)), pl.BlockSpec((8, GM_T), lambda i: (0, 0)),
                  pl.BlockSpec((SEQ, MIXW), lambda i: (0, 0))],
        out_specs=[pl.BlockSpec((SEQ, 2 * MIXW), lambda i: (0, 0)), pl.BlockSpec((NHEAD, GM_T, GM_T), lambda i: (0, 0, 0)),
                   pl.BlockSpec((8, MIXW), lambda i: (0, 0)), pl.BlockSpec((8, GM_T), lambda i: (0, 0))],
        out_shape=[_sds((SEQ, 2 * MIXW), bf16), _sds((NHEAD, GM_T, GM_T), f32), _sds((8, MIXW), f32), _sds((8, GM_T), f32)],
        scratch_shapes=[pltpu.VMEM((GM_T, MIXW), f32)],
        compiler_params=_params(("arbitrary",)),
    )(zm, zm, ng, ws, bs8, do)


def _lru_bwd(zm, cw8, cb, wa, ba, wx, bx, lam, hd, do):
    nchunk = SEQ // LRU_T

    def body(x_ref, g_ref, cw_ref, cb_ref, wa_ref, ba_ref, wx_ref, bx_ref, lam_ref, h_ref, do_ref,
             dz_ref, dwa_ref, dwx_ref, dcw_ref, dvec_ref, xp_s, xc_s, dxc_s):
        _lru_conv(x_ref, cw_ref, cb_ref, xp_s, xc_s)
        lam_v = lam_ref[...]
        sp = jax.nn.softplus(-lam_v)
        sgl = _sigmoid(-lam_v)
        wa_v, wx_v, ba_v, bx_v = wa_ref[...], wx_ref[...], ba_ref[...], bx_ref[...]
        dwa_ref[...] = jnp.zeros_like(dwa_ref)
        dwx_ref[...] = jnp.zeros_like(dwx_ref)
        dxc_s[pl.ds(SEQ, 8), :] = jnp.zeros((8, MIXW), f32)
        row = lax.broadcasted_iota(jnp.int32, (LRU_T, 1), 0)
        zero = jnp.zeros((1, MIXW), f32)

        def chunk(j, carry):
            dh_next, a_next, dba, dbx, dlam = carry
            c = nchunk - 1 - j
            rows = pl.ds(pl.multiple_of(c * LRU_T, LRU_T), LRU_T)
            prev = pl.ds(pl.multiple_of(jnp.maximum(c - 1, 0) * LRU_T, LRU_T), LRU_T)
            first = (row + c * LRU_T) == 0
            xc, gate, d, h = xc_s[rows, :], g_ref[rows, :], do_ref[rows, :], h_ref[rows, :]
            a, mult, r, ig, m2 = _lru_gates(xc, wa_v, ba_v, wx_v, bx_v, sp, first)
            h_last = jnp.where(c > 0, h_ref[prev, :][LRU_T - 1:LRU_T, :], 0.0)
            h_m1 = jnp.where(row == 0, h_last, pltpu.roll(h, 1, 0))
            a_up = jnp.where(row == LRU_T - 1, a_next, pltpu.roll(a, LRU_T - 1, 0))
            acum, dh_loc = _lru_scan(a_up, d * _gelu(gate), True)
            dh = dh_loc + acum * dh_next
            dmult = jnp.where(first, 0.0, dh * (ig * xc))
            msq = jnp.sqrt(jnp.maximum(m2, 0.0))
            dla = dh * h_m1 * a + jnp.where(m2 > 0.0, -dmult * (1.0 - m2) / msq, 0.0)
            dpr = dla * (-LRU_C) * sp * r * (1.0 - r)
            dpi = dh * mult * xc * ig * (1.0 - ig)
            dxc_s[rows, :] = dh * mult * ig + _dot_nt(dpr, wa_v) + _dot_nt(dpi, wx_v)
            dwa_ref[...] = dwa_ref[...] + _dot_tn(xc, dpr)
            dwx_ref[...] = dwx_ref[...] + _dot_tn(xc, dpi)
            dz_ref[rows, MIXW:2 * MIXW] = (d * h * _dgelu(gate)).astype(bf16)
            return (dh[0:1], a[0:1], dba + jnp.sum(dpr, axis=0, keepdims=True), dbx + jnp.sum(dpi, axis=0, keepdims=True),
                    dlam + jnp.sum(dla * r, axis=0, keepdims=True) * (LRU_C * sgl))

        _, _, dba, dbx, dlam = lax.fori_loop(0, nchunk, chunk, (zero, zero, zero, zero, zero))
        cw = cw_ref[...]
        dxc = dxc_s[pl.ds(0, SEQ), :]
        dx = dxc * cw[3:4]
        dcw = [None] * 4
        dcw[3] = jnp.sum(dxc * x_ref[...], axis=0, keepdims=True)
        for k in range(1, 4):
            dx = dx + dxc_s[pl.ds(k, SEQ), :] * cw[3 - k:4 - k]
            dcw[3 - k] = jnp.sum(dxc * xp_s[pl.ds(8 - k, SEQ), :], axis=0, keepdims=True)
        dz_ref[:, 0:MIXW] = dx.astype(bf16)
        dcw_ref[...] = jnp.concatenate(dcw + [jnp.zeros((4, MIXW), f32)], axis=0)
        dvec_ref[...] = jnp.concatenate([jnp.sum(dxc, axis=0, keepdims=True), dba, dbx, dlam, jnp.zeros((4, MIXW), f32)], axis=0)

    col = lambda j: pl.BlockSpec((SEQ, MIXW), lambda i: (0, j))
    vec = pl.BlockSpec((1, MIXW), lambda i: (0, 0))
    vec8 = pl.BlockSpec((8, MIXW), lambda i: (0, 0))
    mat = pl.BlockSpec((MIXW, MIXW), lambda i: (0, 0))
    full = pl.BlockSpec((SEQ, MIXW), lambda i: (0, 0))
    return pl.pallas_call(
        body, name="lru_bwd", grid=(1,),
        in_specs=[col(9), col(10), vec8, vec, mat, vec, mat, vec, vec, full, full],
        out_specs=[pl.BlockSpec((SEQ, 2 * MIXW), lambda i: (0, 0)), mat, mat, vec8, vec8],
        out_shape=[_sds((SEQ, 2 * MIXW), bf16), _sds((MIXW, MIXW), f32), _sds((MIXW, MIXW), f32),
                   _sds((8, MIXW), f32), _sds((8, MIXW), f32)],
        scratch_shapes=[pltpu.VMEM((SEQ + 8, MIXW), f32), pltpu.VMEM((SEQ, MIXW), f32), pltpu.VMEM((SEQ + 8, MIXW), f32)],
        compiler_params=_params(("arbitrary",)),
    )(zm, zm, cw8, cb, wa, ba, wx, bx, lam, hd, do)


def _matmul_tn(a, b, tm, tn, b_col0=0):
    m = a.shape[1]
    n = tn if b_col0 else b.shape[1]
    off = b_col0 // tn

    def body(a_ref, b_ref, o_ref):
        o_ref[...] = _dot_tn(a_ref[...], b_ref[...]).astype(bf16)

    return pl.pallas_call(
        body, name="matmul_tn", grid=(m // tm, n // tn),
        in_specs=[pl.BlockSpec((SEQ, tm), lambda i, j: (0, i)), pl.BlockSpec((SEQ, tn), lambda i, j: (0, j + off))],
        out_specs=pl.BlockSpec((tm, tn), lambda i, j: (i, j)),
        out_shape=_sds((m, n), bf16),
        compiler_params=_params(("parallel", "arbitrary")),
    )(a, b)


def _matmul_nt_norm(pairs, x, g, dres):
    tm = 512
    steps = [a.shape[1] // t for a, _, t in pairs]
    starts = [sum(steps[:i]) for i in range(len(pairs))]
    total = sum(steps)
    npair = len(pairs)

    def body(*refs):
        a_refs, w_refs = refs[0:2 * npair:2], refs[1:2 * npair:2]
        x_ref, g_ref, dres_ref, dx_ref, dg_ref, acc_s = refs[2 * npair:]
        i, k = pl.program_id(0), pl.program_id(1)

        @pl.when(k == 0)
        def _():
            acc_s[...] = jnp.zeros_like(acc_s)

        @pl.when((i == 0) & (k == 0))
        def _():
            dg_ref[...] = jnp.zeros_like(dg_ref)

        for q in range(npair):
            @pl.when((k >= starts[q]) & (k < starts[q] + steps[q]))
            def _(q=q):
                acc_s[...] += _dot_nt(a_refs[q][...], w_refs[q][...])

        @pl.when(k == total - 1)
        def _():
            dx, dg = _rms_bwd(x_ref[...], g_ref[...], acc_s[...])
            dx_ref[...] = dres_ref[...] + dx
            dg_ref[...] += dg

    in_specs, args = [], []
    for q, (a, w, t) in enumerate(pairs):
        kmap = lambda k, q=q: jnp.clip(k - starts[q], 0, steps[q] - 1)
        in_specs += [pl.BlockSpec((tm, t), lambda i, k, kmap=kmap: (i, kmap(k))),
                     pl.BlockSpec((DM, t), lambda i, k, kmap=kmap: (0, kmap(k)))]
        args += [a, w]
    row = pl.BlockSpec((tm, DM), lambda i, k: (i, 0))
    vec = pl.BlockSpec((1, DM), lambda i, k: (0, 0))
    return pl.pallas_call(
        body, name="matmul_nt_norm", grid=(SEQ // tm, total),
        in_specs=in_specs + [row, vec, row], out_specs=[row, vec],
        out_shape=[_sds((SEQ, DM), f32), _sds((1, DM), f32)],
        scratch_shapes=[pltpu.VMEM((tm, DM), f32)],
        compiler_params=_params(("arbitrary", "arbitrary")),
    )(*args, x, g, dres)


def _merge_bwd(dx1, y, g2, outs, zg, wb, wo):
    def body(dx_ref, y_ref, g_ref, oa_ref, ob_ref, oc_ref, od_ref, zg_ref, wb_ref, wo_ref,
             da_ref, db_ref, dc_ref, dd_ref, dzg_ref, dpj_ref, dy_ref, dg_ref):
        @pl.when(pl.program_id(0) == 0)
        def _():
            dg_ref[...] = jnp.zeros_like(dg_ref)

        dy, dg = _rms_bwd(y_ref[...], g_ref[...], dx_ref[...])
        dg_ref[...] += dg
        dyb = dy.astype(bf16)
        dy_ref[...] = dyb
        dmerged = _dot_nt(dyb, wo_ref[...])
        for n, (o_ref, do_ref) in enumerate(((oa_ref, da_ref), (ob_ref, db_ref), (oc_ref, dc_ref), (od_ref, dd_ref))):
            cols = slice(n * DM, (n + 1) * DM)
            gate = _sigmoid(zg_ref[:, cols])
            proj = jnp.dot(o_ref[...], wb_ref[n], preferred_element_type=f32)
            dproj = (dmerged * gate).astype(bf16)
            dpj_ref[:, cols] = dproj
            dzg_ref[:, cols] = (dmerged * proj * gate * (1.0 - gate)).astype(bf16)
            do_ref[...] = _dot_nt(dproj, wb_ref[n])

    row = lambda w: pl.BlockSpec((ROW_TILE, w), lambda i: (i, 0))
    vec = pl.BlockSpec((1, DM), lambda i: (0, 0))
    return pl.pallas_call(
        body, name="merge_bwd", grid=(SEQ // ROW_TILE,),
        in_specs=[row(DM), row(DM), vec] + [row(MIXW)] * 4 + [row(NGATE), pl.BlockSpec((NHEAD, MIXW, DM), lambda i: (0, 0, 0)),
                                                              pl.BlockSpec((DM, DM), lambda i: (0, 0))],
        out_specs=[row(MIXW)] * 4 + [row(NGATE), row(NGATE), row(DM), vec],
        out_shape=[_sds((SEQ, MIXW), f32)] * 4 + [_sds((SEQ, NGATE), bf16), _sds((SEQ, NGATE), bf16), _sds((SEQ, DM), bf16),
                                                  _sds((1, DM), f32)],
        compiler_params=_params(("arbitrary",)),
    )(dx1, y, g2, *outs, zg, wb, wo)


def _ffn_bwd(dx2, f, g4, u, w2):
    def body(dx_ref, f_ref, g_ref, u_ref, w_ref, du_ref, a_ref, df_ref, dg_ref):
        @pl.when(pl.program_id(0) == 0)
        def _():
            dg_ref[...] = jnp.zeros_like(dg_ref)

        df, dg = _rms_bwd(f_ref[...], g_ref[...], dx_ref[...])
        dg_ref[...] += dg
        dfb = df.astype(bf16)
        df_ref[...] = dfb
        da = _dot_nt(dfb, w_ref[...])
        gt, up = u_ref[:, :FFH], u_ref[:, FFH:]
        a_ref[...] = (_silu(gt) * up).astype(bf16)
        du_ref[:, :FFH] = (da * up * _dsilu(gt)).astype(bf16)
        du_ref[:, FFH:] = (da * _silu(gt)).astype(bf16)

    row = lambda w: pl.BlockSpec((ROW_TILE, w), lambda i: (i, 0))
    vec = pl.BlockSpec((1, DM), lambda i: (0, 0))
    return pl.pallas_call(
        body, name="ffn_bwd", grid=(SEQ // ROW_TILE,),
        in_specs=[row(DM), row(DM), vec, row(2 * FFH), pl.BlockSpec((FFH, DM), lambda i: (0, 0))],
        out_specs=[row(2 * FFH), row(FFH), row(DM), vec],
        out_shape=[_sds((SEQ, 2 * FFH), bf16), _sds((SEQ, FFH), bf16), _sds((SEQ, DM), bf16), _sds((1, DM), f32)],
        compiler_params=_params(("arbitrary",)),
    )(dx2, f, g4, u, w2)


def _layer_bwd(dx2, p, sv, ffn_grads_ready=None):
    du, act, df, dg4 = _ffn_bwd(dx2, sv["f"], p["g4"], sv["u"], p["w2"])
    dw2 = _matmul_tn(act, df, 1408, DM)
    dx1, dg3 = _matmul_nt_norm([(du, p["w1"], 1408)], sv["x1"], p["g3"], dx2)
    dw1 = _matmul_tn(sv["h2"], du, DM, 1408)
    g2 = p["g2"]
    if ffn_grads_ready is not None:
        g2 = g2 + ffn_grads_ready(dict(w_ffn_in=dw1, w_ffn_out=dw2), dx1)
    *dos, dzg, dproj, dy, dg2 = _merge_bwd(dx1, sv["y"], g2, sv["outs"], sv["zg"], p["wb"], p["wo"])
    dwo = _matmul_tn(sv["merged"], dy, DM, DM)
    dwb = jnp.stack([_matmul_tn(sv["outs"][n], dproj, MIXW, DM, b_col0=n * DM) if n else
                     _matmul_tn(sv["outs"][0], dproj[:, :DM], MIXW, DM) for n in range(NHEAD)])
    zm = sv["zm"]
    zm3 = zm.reshape(HG_N, HG_T, NMIX)
    dza, drb = _attn_bwd(zm, p["rb8"], dos[0])
    dor, dgb, dhng = _hgrn_out_bwd(zm3, p["hng"], sv["obraw3"], dos[1].reshape(HG_N, HG_T, MIXW))
    dzb, dlb = _hgrn_bwd(zm3, p["lb"], dor, sv["hstates"])
    dzc, dws, dgng, dbs = _gmlp_bwd(zm, p["gng"], p["gws"], p["gbs8"], dos[2])
    dzd, dwa, dwx, dcw, dvec = _lru_bwd(zm, p["cw8"], p["cb"], p["wa"], p["ba"], p["wx"], p["bx"], p["lam"], sv["hd"], dos[3])
    dzm = jnp.concatenate([dza, dzb.reshape(SEQ, 3 * MIXW), dgb.reshape(SEQ, MIXW), dzc, dzd], axis=1)
    dx0, dg1 = _matmul_nt_norm([(dzm, p["wm"], 1408), (dzg, p["wgt"], 1024)], sv["x"], p["g1"], dx1)
    dwin = jnp.concatenate([_matmul_tn(sv["h"], dzm, DM, 1408), _matmul_tn(sv["h"], dzg, DM, 1024)], axis=1)
    big = dict(w_in=dwin, w_branch=dwb, w_out=dwo, w_ffn_in=dw1, w_ffn_out=dw2)
    small = dict(
        norm_mix_pre=dg1[0], norm_mix_post=dg2[0], norm_ffn_pre=dg3[0], norm_ffn_post=dg4[0],
        attn_rel_bias=drb[:NHEAD], lb=dlb[0], hgrn_norm_g=dhng[0], gmlp_norm_g=dgng[0], gmlp_ws=dws, gmlp_bs=dbs[:NHEAD],
        lru_conv_w=dcw[:NHEAD], lru_conv_b=dvec[0], lru_wa=_diag_blocks(dwa), lru_ba=dvec[1], lru_wx=_diag_blocks(dwx),
        lru_bx=dvec[2], lru_lambda=dvec[3],
    )
    return dx0, big, small


MIX_BIG = ("w_in", "w_branch", "w_out")
FFN_BIG = ("w_ffn_in", "w_ffn_out")
BIG = MIX_BIG + FFN_BIG
SMALL = ("norm_mix_pre", "norm_mix_post", "norm_ffn_pre", "norm_ffn_post", "attn_rel_bias", "hgrn_lb_logits", "hgrn_norm_g",
         "gmlp_norm_g", "gmlp_ws", "gmlp_bs", "lru_conv_w", "lru_conv_b", "lru_wa", "lru_ba", "lru_wx", "lru_bx", "lru_lambda")


def _local_step(x, tgt, full, small):
    lbs = _lb_fwd(small["hgrn_lb_logits"])
    params, saved = [], []
    for l in range(DEPTH):
        p = _layer_params(l, {k: full[k][l] for k in BIG}, small, lbs)
        x, sv = _layer_fwd(x, p)
        params.append(p)
        saved.append(sv)
    loss, dx = _loss_head(x, tgt)
    bigs, smalls = [None] * DEPTH, [None] * DEPTH
    for l in range(DEPTH - 1, -1, -1):
        dx, bigs[l], smalls[l] = _layer_bwd(dx, params[l], saved[l])
    gbig = {k: jnp.stack([bigs[l][k] for l in range(DEPTH)]) for k in BIG}
    gsmall = {k: jnp.stack([smalls[l][k] for l in range(DEPTH)]) for k in smalls[0]}
    gsmall["hgrn_lb_logits"] = _lb_bwd(small["hgrn_lb_logits"], gsmall.pop("lb"))
    return loss, dx, gbig, gsmall


HBM_ANY = pl.BlockSpec(memory_space=pl.ANY)


def _mesh_pos():
    return lax.axis_index("x"), lax.axis_index("y"), lax.axis_index("c")


def _all_gather(x, name):
    def body(x_ref, out_ref, send_sems, recv_sems, local_sem):
        ax, ay, ac = _mesh_pos()
        me, sibling = (ax, ay, ac), (ax, ay, 1 - ac)
        chips = [(1 - ax, ay), (ax, 1 - ay), (1 - ax, 1 - ay)]

        def slot(px, py, pc):
            return out_ref.at[4 * px + 2 * py + pc]

        def copy(k, block, to, src=None):
            return pltpu.make_async_remote_copy(
                src_ref=slot(*block) if src is None else src, dst_ref=slot(*block),
                send_sem=send_sems.at[k], recv_sem=recv_sems.at[k], device_id=to, device_id_type=MESH_ID)

        mine = pltpu.make_async_copy(x_ref, slot(*me), local_sem)
        mine.start()
        first = [copy(0, me, sibling, src=x_ref)]
        first += [copy(1 + j, me, (*chip, ac), src=x_ref) for j, chip in enumerate(chips)]
        for cp in first:
            cp.start()
        passed = [copy(4 + j, (*chip, ac), sibling) for j, chip in enumerate(chips)]
        for j, chip in enumerate(chips):
            copy(1 + j, (*chip, ac), me).wait_recv()
            passed[j].start()
        copy(0, sibling, me).wait_recv()
        for j, chip in enumerate(chips):
            copy(4 + j, (*chip, 1 - ac), me).wait_recv()
        for cp in first + passed:
            cp.wait_send()
        mine.wait()

    return pl.pallas_call(
        body, name=name, out_shape=_sds((NDEV,) + x.shape, x.dtype),
        in_specs=[HBM_ANY], out_specs=HBM_ANY,
        scratch_shapes=[pltpu.SemaphoreType.DMA((7,)), pltpu.SemaphoreType.DMA((7,)), pltpu.SemaphoreType.DMA],
    )(x)


def _exchange(g, name):
    def body(g_ref, out_ref, send_sems, recv_sems, local_sem):
        ax, ay, ac = _mesh_pos()
        me = 4 * ax + 2 * ay + ac
        mine = pltpu.make_async_copy(g_ref.at[me], out_ref.at[me], local_sem)
        mine.start()
        copies = []
        for k in range(1, NDEV):
            px = 1 - ax if k & 4 else ax
            py = 1 - ay if k & 2 else ay
            pc = 1 - ac if k & 1 else ac
            copies.append(pltpu.make_async_remote_copy(
                src_ref=g_ref.at[4 * px + 2 * py + pc], dst_ref=out_ref.at[me],
                send_sem=send_sems.at[k - 1], recv_sem=recv_sems.at[k - 1], device_id=(px, py, pc), device_id_type=MESH_ID))
        for cp in copies:
            cp.start()
        for cp in copies:
            cp.wait()
        mine.wait()

    return pl.pallas_call(
        body, name=name, out_shape=_sds(g.shape, g.dtype),
        in_specs=[HBM_ANY], out_specs=HBM_ANY,
        scratch_shapes=[pltpu.SemaphoreType.DMA((7,)), pltpu.SemaphoreType.DMA((7,)), pltpu.SemaphoreType.DMA],
    )(g)


def _peer(ax, ay, ac, k):
    return (1 - ax if k & 4 else ax, 1 - ay if k & 2 else ay, 1 - ac if k & 1 else ac)


def _handshake(peers):
    barrier = pltpu.get_barrier_semaphore()
    for peer in peers:
        pl.semaphore_signal(barrier, inc=1, device_id=peer, device_id_type=MESH_ID)
    pl.semaphore_wait(barrier, len(peers))


SEQUENCER = dict(axis_name="seq", num_cores=1)
GATHER_ID = 1
EXCHANGE_ID = 2


def _gather_sc(xs, name):
    n = len(xs)

    def body(*refs):
        srcs, outs = refs[:n], refs[n:2 * n]
        send_sems, recv_sems, local_sems = refs[2 * n:]
        ax, ay, ac = _mesh_pos()
        me, sibling = (ax, ay, ac), (ax, ay, 1 - ac)
        chips = [(1 - ax, ay), (ax, 1 - ay), (1 - ax, 1 - ay)]
        _handshake([sibling] + [(*chip, ac) for chip in chips])

        def slot(i, px, py, pc):
            return outs[i].at[4 * px + 2 * py + pc]

        def copy(i, k, block, to, src=None):
            return pltpu.make_async_remote_copy(
                src_ref=slot(i, *block) if src is None else src, dst_ref=slot(i, *block),
                send_sem=send_sems.at[7 * i + k], recv_sem=recv_sems.at[7 * i + k], device_id=to, device_id_type=MESH_ID)

        mine = [pltpu.make_async_copy(srcs[i], slot(i, *me), local_sems.at[i]) for i in range(n)]
        first = []
        for i in range(n):
            first += [copy(i, 1 + j, me, (*chip, ac), src=srcs[i]) for j, chip in enumerate(chips)]
        for i in range(n):
            first += [copy(i, 0, me, sibling, src=srcs[i])]
        for cp in first + mine:
            cp.start()
        passed = []
        for i in range(n):
            for j, chip in enumerate(chips):
                copy(i, 1 + j, (*chip, ac), me).wait_recv()
                passed.append(copy(i, 4 + j, (*chip, ac), sibling))
                passed[-1].start()
        for i in range(n):
            copy(i, 0, sibling, me).wait_recv()
            for j, chip in enumerate(chips):
                copy(i, 4 + j, (*chip, 1 - ac), me).wait_recv()
        for cp in first + passed:
            cp.wait_send()
        for cp in mine:
            cp.wait()

    return pl.kernel(
        body, name=name, out_type=[_sds((NDEV,) + x.shape, x.dtype) for x in xs],
        mesh=plsc.ScalarSubcoreMesh(**SEQUENCER),
        scratch_types=[pltpu.SemaphoreType.DMA((7 * n,)), pltpu.SemaphoreType.DMA((7 * n,)), pltpu.SemaphoreType.DMA((n,))],
        compiler_params=pltpu.CompilerParams(collective_id=GATHER_ID),
    )(*xs)


def _exchange_sc(gs, name):
    n = len(gs)

    def body(*refs):
        srcs, outs = refs[:n], refs[n:2 * n]
        send_sems, recv_sems, local_sems = refs[2 * n:]
        ax, ay, ac = _mesh_pos()
        me = 4 * ax + 2 * ay + ac
        peers = [_peer(ax, ay, ac, k) for k in range(1, NDEV)]
        _handshake(peers)
        mine = [pltpu.make_async_copy(srcs[i].at[me], outs[i].at[me], local_sems.at[i]) for i in range(n)]
        copies = []
        for i in range(n):
            for k, (px, py, pc) in enumerate(peers):
                copies.append(pltpu.make_async_remote_copy(
                    src_ref=srcs[i].at[4 * px + 2 * py + pc], dst_ref=outs[i].at[me],
                    send_sem=send_sems.at[7 * i + k], recv_sem=recv_sems.at[7 * i + k],
                    device_id=(px, py, pc), device_id_type=MESH_ID))
        for cp in copies + mine:
            cp.start()
        for cp in copies + mine:
            cp.wait()

    return pl.kernel(
        body, name=name, out_type=[_sds(g.shape, g.dtype) for g in gs],
        mesh=plsc.ScalarSubcoreMesh(**SEQUENCER),
        scratch_types=[pltpu.SemaphoreType.DMA((7 * n,)), pltpu.SemaphoreType.DMA((7 * n,)), pltpu.SemaphoreType.DMA((n,))],
        compiler_params=pltpu.CompilerParams(collective_id=EXCHANGE_ID),
    )(*gs)


HBM_SPEC = pl.BlockSpec(memory_space=pltpu.HBM)
SEM_SPEC = pl.BlockSpec(memory_space=pltpu.SEMAPHORE)
DATAFLOW = pltpu.SideEffectType.DATAFLOW_SIDE_EFFECTING


def _exchange_copies(srcs, lands, send_sems, recv_sems, local_sems):
    n = len(srcs)
    ax, ay, ac = _mesh_pos()
    me = 4 * ax + 2 * ay + ac
    copies = [pltpu.make_async_copy(srcs[i].at[me], lands[i].at[me], local_sems.at[i]) for i in range(n)]
    for i in range(n):
        for k in range(1, NDEV):
            px, py, pc = _peer(ax, ay, ac, k)
            copies.append(pltpu.make_async_remote_copy(
                src_ref=srcs[i].at[4 * px + 2 * py + pc], dst_ref=lands[i].at[me],
                send_sem=send_sems.at[7 * i + k - 1], recv_sem=recv_sems.at[7 * i + k - 1],
                device_id=(px, py, pc), device_id_type=MESH_ID))
    return copies


def _exchange_start(gs, name):
    n = len(gs)

    def body(*refs):
        srcs, lands = refs[:n], refs[n:2 * n]
        send_sems, recv_sems, local_sems = refs[2 * n:2 * n + 3]
        token = refs[-1]
        for cp in _exchange_copies(srcs, lands, send_sems, recv_sems, local_sems):
            cp.start()
        token[...] = jnp.zeros_like(token)

    hbm = [pltpu.HBM(g.shape, g.dtype) for g in gs]
    outs = pl.pallas_call(
        body, name=name,
        out_shape=(pltpu.SemaphoreType.DMA((7 * n,)), pltpu.SemaphoreType.DMA((7 * n,)), pltpu.SemaphoreType.DMA((n,)),
                   *hbm, *hbm, _sds((8, 128), f32)),
        in_specs=[HBM_SPEC] * (2 * n),
        out_specs=(SEM_SPEC, SEM_SPEC, SEM_SPEC, *[HBM_SPEC] * (2 * n), pl.BlockSpec(memory_space=pltpu.VMEM)),
        input_output_aliases={i: 3 + i for i in range(2 * n)},
        compiler_params=pltpu.CompilerParams(has_side_effects=DATAFLOW),
    )(*[pltpu.with_memory_space_constraint(g, pltpu.HBM) for g in gs],
      *[pltpu.with_memory_space_constraint(lax.empty(g.shape, g.dtype), pltpu.HBM) for g in gs])
    return outs[:-1], outs[-1]


def _exchange_wait(handles, after, name):
    n = (len(handles) - 3) // 2
    send_sems, recv_sems, local_sems = handles[:3]
    srcs, lands = handles[3:3 + n], handles[3 + n:]

    def body(*refs):
        srcs, lands = refs[:n], refs[n:2 * n]
        send_sems, recv_sems, local_sems = refs[2 * n:2 * n + 3]
        for cp in _exchange_copies(srcs, lands, send_sems, recv_sems, local_sems):
            cp.wait()

    hbm = [pltpu.HBM(g.shape, g.dtype) for g in srcs]
    outs = pl.pallas_call(
        body, name=name, out_shape=(*hbm, *hbm),
        in_specs=[HBM_SPEC] * (2 * n) + [SEM_SPEC] * 3 + [pl.BlockSpec(memory_space=pl.ANY)],
        out_specs=tuple([HBM_SPEC] * (2 * n)),
        input_output_aliases={i: i for i in range(2 * n)},
        compiler_params=pltpu.CompilerParams(has_side_effects=DATAFLOW),
    )(*srcs, *lands, send_sems, recv_sems, local_sems, after)
    return outs[n:]


def _row_tile(rows, cols):
    cap = max(8, (1 << 18) // cols)
    if rows <= cap:
        return rows
    best = None
    for t in range(8, cap + 1, 8):
        if rows % t == 0:
            best = t
    assert best is not None, (rows, cols)
    return best


def _sum_parts(parts, name):
    npart, rows, cols = parts.shape
    tr = _row_tile(rows, cols)

    def body(p_ref, o_ref):
        g = p_ref[0].astype(f32)
        for j in range(1, npart):
            g = g + p_ref[j].astype(f32)
        o_ref[...] = g

    return pl.pallas_call(
        body, name=name, grid=(rows // tr,),
        in_specs=[pl.BlockSpec((npart, tr, cols), lambda i: (0, i, 0))], out_specs=pl.BlockSpec((tr, cols), lambda i: (i, 0)),
        out_shape=_sds((rows, cols), f32), compiler_params=_params(("parallel",)),
    )(parts)


def _adamw(parts, w, m, v, name):
    npart, rows, cols = parts.shape
    tr = _row_tile(rows, cols)
    c1 = 1.0 / (1.0 - ADAM_B1 ** ADAM_STEP)
    c2 = 1.0 / (1.0 - ADAM_B2 ** ADAM_STEP)

    def body(p_ref, w_ref, m_ref, v_ref, g_ref, d_ref, mo_ref, vo_ref):
        g = p_ref[0].astype(f32)
        for j in range(1, npart):
            g = g + p_ref[j].astype(f32)
        mn = ADAM_B1 * m_ref[...] + (1.0 - ADAM_B1) * g
        vn = ADAM_B2 * v_ref[...] + (1.0 - ADAM_B2) * (g * g)
        g_ref[...] = g
        mo_ref[...] = mn
        vo_ref[...] = vn
        d_ref[...] = (-ADAM_LR) * ((mn * c1) / (jnp.sqrt(vn * c2) + ADAM_EPS) + ADAM_WD * w_ref[...])

    blk = pl.BlockSpec((tr, cols), lambda i: (i, 0))
    return pl.pallas_call(
        body, name=name, grid=(rows // tr,),
        in_specs=[pl.BlockSpec((npart, tr, cols), lambda i: (0, i, 0)), blk, blk, blk], out_specs=[blk] * 4,
        out_shape=[_sds((rows, cols), f32)] * 4, compiler_params=_params(("parallel",)),
    )(parts, w, m, v)


def _adamw_layer(parts, w, m, v, acc, l, name):
    npart, rows, cols = parts.shape
    tr = _row_tile(rows, cols)
    c1 = 1.0 / (1.0 - ADAM_B1 ** ADAM_STEP)
    c2 = 1.0 / (1.0 - ADAM_B2 ** ADAM_STEP)

    def body(p_ref, w_ref, m_ref, v_ref, *refs):
        g_ref, d_ref, mo_ref, vo_ref = refs[-4:]
        g = p_ref[0].astype(f32)
        for j in range(1, npart):
            g = g + p_ref[j].astype(f32)
        mn = ADAM_B1 * m_ref[...] + (1.0 - ADAM_B1) * g
        vn = ADAM_B2 * v_ref[...] + (1.0 - ADAM_B2) * (g * g)
        g_ref[...] = g
        mo_ref[...] = mn
        vo_ref[...] = vn
        d_ref[...] = (-ADAM_LR) * ((mn * c1) / (jnp.sqrt(vn * c2) + ADAM_EPS) + ADAM_WD * w_ref[...])

    blk = pl.BlockSpec((None, tr, cols), lambda i: (l, i, 0))
    prev = [] if acc is None else list(acc)
    return pl.pallas_call(
        body, name=name, grid=(rows // tr,),
        in_specs=[pl.BlockSpec((npart, tr, cols), lambda i: (0, i, 0)), blk, blk, blk] + [HBM_ANY] * len(prev),
        out_specs=[blk] * 4, out_shape=[_sds(w.shape, f32)] * 4,
        input_output_aliases={4 + j: j for j in range(len(prev))},
        compiler_params=_params(("parallel",)),
    )(parts, w, m, v, *prev)


def _pack(arrays):
    rows = []
    for a in arrays:
        flat = a.reshape(-1)
        pad = (-flat.shape[0]) % 1024
        rows.append(jnp.concatenate([flat, jnp.zeros((pad,), flat.dtype)]).reshape(-1, 128))
    return jnp.concatenate(rows, axis=0)


def _unpack(flat, shapes):
    out, r = [], 0
    for s in shapes:
        n = math.prod(s)
        nr = (n + 1023) // 1024 * 8
        out.append(flat[r:r + nr].reshape(-1)[:n].reshape(s))
        r += nr
    return out


BIG_SHARD_AXIS = dict(w_in=2, w_branch=3, w_out=1, w_ffn_in=2, w_ffn_out=1)
SHARDED_SMALL = ("attn_rel_bias", "lru_conv_w")


def _to_blocks(full, axis):
    s = full.shape
    cut = full.reshape(s[:axis] + (NDEV, s[axis] // NDEV) + s[axis + 1:])
    return jnp.moveaxis(cut, axis, 0)


def _from_blocks(blocks, axis):
    moved = jnp.moveaxis(blocks, 0, axis)
    s = moved.shape
    return moved.reshape(s[:axis] + (s[axis] * s[axis + 1],) + s[axis + 2:])


def _flat2(a):
    return a.reshape(-1, a.shape[-1])


def _my_slice(a, n):
    ax, ay, ac = _mesh_pos()
    return lax.dynamic_slice_in_dim(a, (4 * ax + 2 * ay + ac) * n, n, axis=a.ndim - 1)


_WEIGHTS = ("norm_mix_pre", "norm_mix_post", "norm_ffn_pre", "norm_ffn_post", "w_in", "attn_rel_bias", "hgrn_lb_logits",
            "hgrn_norm_g", "gmlp_norm_g", "gmlp_ws", "gmlp_bs", "lru_conv_w", "lru_conv_b", "lru_wa", "lru_ba", "lru_wx",
            "lru_bx", "lru_lambda", "w_branch", "w_out", "w_ffn_in", "w_ffn_out")


def _step(x, loss_target, w, m, v):
    gathered = []
    for l in range(DEPTH):
        gathered.append(tuple(_gather_sc([w[k][l].astype(bf16) for k in keys], "gather_%s%d" % (half, l))
                              for half, keys in (("mix", MIX_BIG), ("ffn", FFN_BIG))))
    cut = jnp.concatenate([w[k] for k in SHARDED_SMALL], axis=-1)
    parts = _all_gather(_pack([cut]), "gather_small").reshape(NDEV, -1)[:, :math.prod(cut.shape)].reshape((NDEV,) + cut.shape)
    small = {k: w[k] for k in SMALL if k not in SHARDED_SMALL}
    at = 0
    for k in SHARDED_SMALL:
        n = w[k].shape[-1]
        small[k] = _from_blocks(parts[..., at:at + n], 2)
        at += n
    loss, dx, layers = _step_forward(x, loss_target, gathered, small)
    flat3 = lambda a: a.reshape((DEPTH, -1, a.shape[-1]))
    acc = {k: None for k in BIG}
    smalls = [None] * DEPTH

    def send(grads, keys, name):
        handles, token = _exchange_start([_to_blocks(grads[k], BIG_SHARD_AXIS[k] - 1) for k in keys], "start_" + name)
        return (keys, handles, "wait_" + name), token[0:1, 0:1]

    def update(sent, l, after):
        keys, handles, name = sent
        got = dict(zip(keys, _exchange_wait(handles, after, name)))
        for k, g in got.items():
            w3 = flat3(w[k])
            acc[k] = _adamw_layer(g.reshape((NDEV,) + w3.shape[1:]), w3, flat3(m[k]), flat3(v[k]), acc[k], l,
                                  "adamw_%s_%d" % (k, l))

    waiting = []
    for l in range(DEPTH - 1, -1, -1):
        sent_ffn = []

        def ffn_grads_ready(grads, dx1, l=l, sent_ffn=sent_ffn):
            sent, zero = send(grads, FFN_BIG, "exchange_ffn%d" % l)
            sent_ffn.append(sent)
            while waiting:
                update(*waiting.pop(), dx1)
            return zero

        dx, gbig, smalls[l] = _step_backward(dx, layers[l], ffn_grads_ready)
        sent_mix, zero = send(gbig, MIX_BIG, "exchange_mix%d" % l)
        if l:
            below = layers[l - 1][0]
            below["g4"] = below["g4"] + zero
        update(sent_ffn[0], l, dx)
        waiting.append((sent_mix, l))
    update(*waiting.pop(), dx)
    grads, deltas, new_m, new_v = {}, {}, {}, {}
    for k in BIG:
        grads[k], deltas[k], new_m[k], new_v[k] = (o.reshape(w[k].shape) for o in acc[k])
    gsmall = {k: jnp.stack([smalls[l][k] for l in range(DEPTH)]) for k in smalls[0]}
    gsmall["hgrn_lb_logits"] = _lb_bwd(small["hgrn_lb_logits"], gsmall.pop("lb"))
    shapes = [gsmall[k].shape for k in SMALL]
    sums = _unpack(_sum_parts(_all_gather(_pack([gsmall[k] for k in SMALL]), "gather_small_grads"), "sum_small_grads"), shapes)
    gs = dict(zip(SMALL, sums))
    for k in SHARDED_SMALL:
        gs[k] = _my_slice(gs[k], w[k].shape[-1])
    packed = [_pack([d[k] for k in SMALL]) for d in (gs, w, m, v)]
    outs = _adamw(packed[0][None], packed[1], packed[2], packed[3], "adamw_small")
    shapes = [w[k].shape for k in SMALL]
    for d, o in zip((grads, deltas, new_m, new_v), outs):
        d.update(zip(SMALL, _unpack(o, shapes)))
    total = lax.psum(loss[0, 0], ("x", "y", "c"))
    return total, dx[None], grads, deltas, new_m, new_v


def _step_forward(x, loss_target, gathered, small):
    lbs = _lb_fwd(small["hgrn_lb_logits"])
    x = x[0]
    layers = []

    def weights(blocks, keys, after):
        if after is not None:
            blocks, _ = lax.optimization_barrier((blocks, after))
        return {k: _from_blocks(g, BIG_SHARD_AXIS[k] - 1) for k, g in zip(keys, blocks)}

    for l in range(DEPTH):
        mix, ffn = gathered[l]
        p = _layer_params(l, weights(mix, MIX_BIG, x if l else None), small, lbs)
        x, sv = _layer_fwd(x, p, lambda x1, ffn=ffn: _ffn_weights(weights(ffn, FFN_BIG, x1)))
        layers.append((p, sv))
    loss, dx = _loss_head(x, loss_target[0])
    return loss, dx, layers


def _step_backward(dx, layer, ffn_grads_ready):
    return _layer_bwd(dx, *layer, ffn_grads_ready)


def kernel(x, norm_mix_pre, norm_mix_post, norm_ffn_pre, norm_ffn_post, w_in, attn_rel_bias, hgrn_lb_logits, hgrn_norm_g, gmlp_norm_g, gmlp_ws, gmlp_bs, lru_conv_w, lru_conv_b, lru_wa, lru_ba, lru_wx, lru_bx, lru_lambda, w_branch, w_out, w_ffn_in, w_ffn_out, loss_target, m_norm_mix_pre, m_norm_mix_post, m_norm_ffn_pre, m_norm_ffn_post, m_w_in, m_attn_rel_bias, m_hgrn_lb_logits, m_hgrn_norm_g, m_gmlp_norm_g, m_gmlp_ws, m_gmlp_bs, m_lru_conv_w, m_lru_conv_b, m_lru_wa, m_lru_ba, m_lru_wx, m_lru_bx, m_lru_lambda, m_w_branch, m_w_out, m_w_ffn_in, m_w_ffn_out, v_norm_mix_pre, v_norm_mix_post, v_norm_ffn_pre, v_norm_ffn_post, v_w_in, v_attn_rel_bias, v_hgrn_lb_logits, v_hgrn_norm_g, v_gmlp_norm_g, v_gmlp_ws, v_gmlp_bs, v_lru_conv_w, v_lru_conv_b, v_lru_wa, v_lru_ba, v_lru_wx, v_lru_bx, v_lru_lambda, v_w_branch, v_w_out, v_w_ffn_in, v_w_ffn_out):
    w = dict(zip(_WEIGHTS, (norm_mix_pre, norm_mix_post, norm_ffn_pre, norm_ffn_post, w_in, attn_rel_bias, hgrn_lb_logits, hgrn_norm_g, gmlp_norm_g, gmlp_ws, gmlp_bs, lru_conv_w, lru_conv_b, lru_wa, lru_ba, lru_wx, lru_bx, lru_lambda, w_branch, w_out, w_ffn_in, w_ffn_out)))
    m = dict(zip(_WEIGHTS, (m_norm_mix_pre, m_norm_mix_post, m_norm_ffn_pre, m_norm_ffn_post, m_w_in, m_attn_rel_bias, m_hgrn_lb_logits, m_hgrn_norm_g, m_gmlp_norm_g, m_gmlp_ws, m_gmlp_bs, m_lru_conv_w, m_lru_conv_b, m_lru_wa, m_lru_ba, m_lru_wx, m_lru_bx, m_lru_lambda, m_w_branch, m_w_out, m_w_ffn_in, m_w_ffn_out)))
    v = dict(zip(_WEIGHTS, (v_norm_mix_pre, v_norm_mix_post, v_norm_ffn_pre, v_norm_ffn_post, v_w_in, v_attn_rel_bias, v_hgrn_lb_logits, v_hgrn_norm_g, v_gmlp_norm_g, v_gmlp_ws, v_gmlp_bs, v_lru_conv_w, v_lru_conv_b, v_lru_wa, v_lru_ba, v_lru_wx, v_lru_bx, v_lru_lambda, v_w_branch, v_w_out, v_w_ffn_in, v_w_ffn_out)))
    loss, grad_x, grads, deltas, new_m, new_v = _step(x, loss_target, w, m, v)
    return (loss, grad_x, *[grads[k] for k in _WEIGHTS], *[deltas[k] for k in _WEIGHTS],
            *[new_m[k] for k in _WEIGHTS], *[new_v[k] for k in _WEIGHTS])
```

```python
import math

import jax
import jax.numpy as jnp
from jax import lax
from jax.experimental import pallas as pl
from jax.experimental.pallas import tpu as pltpu
from jax.experimental.pallas import tpu_sc as plsc

f32 = jnp.float32
bf16 = jnp.bfloat16

SEQ = 2048
DM = 1024
DEPTH = 4
NDEV = 8
MIXW = 256
NHEAD = 4
HDIM = 64
NMIX = 11 * MIXW
NGATE = 4 * DM
FFH = 2816
EPS = 1e-6
NEG_BIG = -1e30
LOG_FLOOR = 1e-30
LRU_C = 8.0
REL_SIZE = 320
ATT_PAIR = 128
ATT_BAND = 640
ATT_PAD = 512
ATT_WV = 768
HG_T = 16
HG_N = SEQ // HG_T
GM_T = 128
LRU_T = 128
ADAM_LR, ADAM_B1, ADAM_B2, ADAM_EPS, ADAM_WD, ADAM_STEP = 0.001, 0.9, 0.999, 1e-8, 0.01, 10
V7X_VMEM_LIMIT = 56 * 1024 * 1024
GELU_C0 = math.sqrt(2.0 / math.pi)
GELU_C1 = 0.044715
MESH_ID = pl.DeviceIdType.MESH


def _params(sem=None):
    if sem is None:
        return pltpu.CompilerParams(vmem_limit_bytes=V7X_VMEM_LIMIT)
    return pltpu.CompilerParams(dimension_semantics=sem, vmem_limit_bytes=V7X_VMEM_LIMIT)


def _sds(shape, dtype):
    return jax.ShapeDtypeStruct(shape, dtype)


def _dot(a, b):
    return jnp.dot(a.astype(bf16), b.astype(bf16), preferred_element_type=f32)


def _dot_nt(a, b):
    return lax.dot_general(a.astype(bf16), b.astype(bf16), (((1,), (1,)), ((), ())), preferred_element_type=f32)


def _dot_tn(a, b):
    return lax.dot_general(a.astype(bf16), b.astype(bf16), (((0,), (0,)), ((), ())), preferred_element_type=f32)


def _split(a):
    hi = a.astype(bf16)
    lo = (a - hi.astype(f32)).astype(bf16)
    return hi, lo


def _dot_hl(a, m):
    hi, lo = _split(a)
    return jnp.dot(hi, m, preferred_element_type=f32) + jnp.dot(lo, m, preferred_element_type=f32)


def _dot_nt_hl(m, a):
    hi, lo = _split(a)
    dn = (((1,), (1,)), ((), ()))
    return lax.dot_general(m, hi, dn, preferred_element_type=f32) + lax.dot_general(m, lo, dn, preferred_element_type=f32)


def _sigmoid(x):
    return jax.nn.sigmoid(x)


def _silu(x):
    return x * _sigmoid(x)


def _dsilu(x):
    s = _sigmoid(x)
    return s * (1.0 + x * (1.0 - s))


def _gelu(x):
    return 0.5 * x * (1.0 + jnp.tanh(GELU_C0 * (x + GELU_C1 * x * x * x)))


def _dgelu(x):
    t = jnp.tanh(GELU_C0 * (x + GELU_C1 * x * x * x))
    return 0.5 * (1.0 + t) + 0.5 * x * (1.0 - t * t) * GELU_C0 * (1.0 + 3.0 * GELU_C1 * x * x)


def _rms(x, g):
    r = lax.rsqrt(jnp.mean(x * x, axis=-1, keepdims=True) + EPS)
    return x * r * g


def _rms_bwd(x, g, dy):
    r = lax.rsqrt(jnp.mean(x * x, axis=-1, keepdims=True) + EPS)
    xh = x * r
    dxh = dy * g
    dx = r * (dxh - xh * jnp.mean(dxh * xh, axis=-1, keepdims=True))
    return dx, jnp.sum(dy * xh, axis=0, keepdims=True)


def _same_head(n, width, dtype):
    r = lax.broadcasted_iota(jnp.int32, (n, n), 0) // width
    c = lax.broadcasted_iota(jnp.int32, (n, n), 1) // width
    return (r == c).astype(dtype)


def _head_masks(rows=1):
    lane = lax.broadcasted_iota(jnp.int32, (rows, MIXW), 1) // HDIM
    return [lane == h for h in range(NHEAD)]


def _norm_matmul(x, g, w, tn):
    n = w.shape[1]
    tm = 1024

    def body(x_ref, g_ref, w_ref, z_ref, h_ref):
        @pl.when(pl.program_id(1) == 0)
        def _():
            h_ref[...] = _rms(x_ref[...], g_ref[...]).astype(bf16)

        z_ref[...] = jnp.dot(h_ref[...], w_ref[...], preferred_element_type=f32)

    return pl.pallas_call(
        body, name="norm_matmul", grid=(SEQ // tm, n // tn),
        in_specs=[pl.BlockSpec((tm, DM), lambda i, j: (i, 0)), pl.BlockSpec((1, DM), lambda i, j: (0, 0)),
                  pl.BlockSpec((DM, tn), lambda i, j: (0, j))],
        out_specs=[pl.BlockSpec((tm, tn), lambda i, j: (i, j)), pl.BlockSpec((tm, DM), lambda i, j: (i, 0))],
        out_shape=[_sds((SEQ, n), f32), _sds((SEQ, DM), bf16)],
        compiler_params=_params(("parallel", "arbitrary")),
    )(x, g, w)


def _matmul(a, w, tn):
    k, n = w.shape
    tm = 1024

    def body(a_ref, w_ref, z_ref):
        z_ref[...] = jnp.dot(a_ref[...], w_ref[...], preferred_element_type=f32)

    return pl.pallas_call(
        body, name="matmul", grid=(SEQ // tm, n // tn),
        in_specs=[pl.BlockSpec((tm, k), lambda i, j: (i, 0)), pl.BlockSpec((k, tn), lambda i, j: (0, j))],
        out_specs=pl.BlockSpec((tm, tn), lambda i, j: (i, j)),
        out_shape=_sds((SEQ, n), f32),
        compiler_params=_params(("parallel", "arbitrary")),
    )(a, w)


def _att_offset_map():
    i = lax.broadcasted_iota(jnp.int32, (REL_SIZE, ATT_WV), 0)
    t = lax.broadcasted_iota(jnp.int32, (REL_SIZE, ATT_WV), 1)
    e = jnp.where(t <= ATT_BAND, t, t - ATT_WV)
    idx = jnp.clip(ATT_PAD - e, -(HDIM - 1), 256) + (HDIM - 1)
    return (idx == i).astype(bf16)


def _att_band_valid():
    qc = lax.broadcasted_iota(jnp.int32, (ATT_PAIR, ATT_BAND), 0) // HDIM
    kc = lax.broadcasted_iota(jnp.int32, (ATT_PAIR, ATT_BAND), 1) // HDIM
    return (kc >= qc) & (kc <= qc + 8)


def _att_bias_tiles(rb_ref, bm_ref):
    wv = _dot_hl(rb_ref[...], _att_offset_map())
    valid = _att_band_valid()
    for h in range(NHEAD):
        rows = jnp.broadcast_to(wv[h:h + 1, :], (ATT_PAIR, ATT_WV))
        tile = pltpu.roll(rows, 0, 1, stride=1, stride_axis=0)[:, :ATT_BAND]
        bm_ref[h] = jnp.where(valid, tile, NEG_BIG)


def _att_pad_kv(k_ref, v_ref, kp_ref, vp_ref):
    kp_ref[pl.ds(0, ATT_PAD), :] = jnp.zeros((ATT_PAD, MIXW), bf16)
    vp_ref[pl.ds(0, ATT_PAD), :] = jnp.zeros((ATT_PAD, MIXW), bf16)
    kp_ref[pl.ds(ATT_PAD, SEQ), :] = k_ref[...].astype(bf16)
    vp_ref[pl.ds(ATT_PAD, SEQ), :] = v_ref[...].astype(bf16)


def _att_probs(qm, kb, bm, key_ok):
    s = _dot_nt(qm, kb) + bm
    s = jnp.where(key_ok, s, NEG_BIG)
    m = jnp.max(s, axis=-1, keepdims=True)
    e = jnp.exp(s - m)
    return e / jnp.sum(e, axis=-1, keepdims=True)


def _attn_fwd(zm, rb8):
    def body(q_ref, k_ref, v_ref, rb_ref, o_ref, kp_ref, vp_ref, bm_ref):
        _att_pad_kv(k_ref, v_ref, kp_ref, vp_ref)
        _att_bias_tiles(rb_ref, bm_ref)
        hm = _head_masks()

        def pair(p, carry):
            r0 = pl.multiple_of(p * ATT_PAIR, ATT_PAIR)
            q = q_ref[pl.ds(r0, ATT_PAIR), :] * (HDIM ** -0.5)
            kb = kp_ref[pl.ds(r0, ATT_BAND), :]
            vb = vp_ref[pl.ds(r0, ATT_BAND), :]
            key_ok = (lax.broadcasted_iota(jnp.int32, (1, ATT_BAND), 1) + (r0 - ATT_PAD)) >= 0
            o = jnp.zeros((ATT_PAIR, MIXW), f32)
            for h in range(NHEAD):
                qm = jnp.where(hm[h], q, 0.0)
                p_h = _att_probs(qm, kb, bm_ref[h], key_ok)
                o = o + jnp.where(hm[h], _dot(p_h, vb), 0.0)
            o_ref[pl.ds(r0, ATT_PAIR), :] = o.astype(bf16)
            return carry

        lax.fori_loop(0, SEQ // ATT_PAIR, pair, 0)

    col = lambda j: pl.BlockSpec((SEQ, MIXW), lambda i: (0, j))
    return pl.pallas_call(
        body, name="attn_fwd", grid=(1,),
        in_specs=[col(0), col(1), col(2), pl.BlockSpec((8, REL_SIZE), lambda i: (0, 0))],
        out_specs=pl.BlockSpec((SEQ, MIXW), lambda i: (0, 0)),
        out_shape=_sds((SEQ, MIXW), bf16),
        scratch_shapes=[pltpu.VMEM((SEQ + ATT_PAD, MIXW), bf16), pltpu.VMEM((SEQ + ATT_PAD, MIXW), bf16),
                        pltpu.VMEM((NHEAD, ATT_PAIR, ATT_BAND), f32)],
        compiler_params=_params(("arbitrary",)),
    )(zm, zm, zm, rb8)


def _hg_gates(q, fz, lb):
    sq = _sigmoid(q)
    sg = _sigmoid(fz)
    f = lb + (1.0 - lb) * sg
    return q * sq, (1.0 - lb) * (1.0 - sg), jnp.log(jnp.maximum(f, LOG_FLOOR)), sq, sg, f


def _hg_prepare(q_ref, f_ref, lb, qf_s, kf_s, b_s, qd_s, kd_s, dec_s):
    b = None
    for t in range(HG_T):
        qf, kf, lf, _, _, _ = _hg_gates(q_ref[:, t, :], f_ref[:, t, :], lb)
        b = lf if b is None else b + lf
        qf_s[:, t, :] = qf
        kf_s[:, t, :] = kf
        b_s[:, t, :] = b
    b_last = b
    dec_s[...] = jnp.broadcast_to(jnp.exp(b_last)[:, None, :], (HG_N, 8, MIXW))
    for t in range(HG_T):
        bt = b_s[:, t, :]
        qd_s[:, t, :] = qf_s[:, t, :] * jnp.exp(bt)
        kd_s[:, t, :] = kf_s[:, t, :] * jnp.exp(b_last - bt)


def _hg_scores(t, qf_s, kf_s, b_s, w_s, hm):
    qt = qf_s[:, t, :]
    bt = b_s[:, t, :]
    for s in range(t + 1):
        w = qt * kf_s[:, s, :]
        if s < t:
            w = w * jnp.exp(bt - b_s[:, s, :])
        w_s[pl.ds(s * HG_N, HG_N), :] = w.astype(bf16)
    return jnp.dot(w_s[pl.ds(0, (t + 1) * HG_N), :], hm, preferred_element_type=f32)


def _hgrn_fwd(zm3, lb, ng):
    def body(q_ref, f_ref, i_ref, g_ref, lb_ref, ng_ref, o_ref, oraw_ref, states_ref,
             qf_s, kf_s, b_s, qd_s, kd_s, dec_s, w_s, st_s):
        lb = lb_ref[...]
        hm = _same_head(MIXW, HDIM, bf16)
        hmf = _same_head(MIXW, HDIM, f32)
        _hg_prepare(q_ref, f_ref, lb, qf_s, kf_s, b_s, qd_s, kd_s, dec_s)
        for t in range(HG_T):
            p = _hg_scores(t, qf_s, kf_s, b_s, w_s, hm)
            acc = jnp.zeros((HG_N, MIXW), f32)
            for s in range(t + 1):
                acc = acc + p[s * HG_N:(s + 1) * HG_N] * i_ref[:, s, :]
            oraw_ref[:, t, :] = acc
        st_s[...] = jnp.zeros((MIXW, MIXW), f32)

        def step(n, carry):
            st = st_s[...]
            stb = st.astype(bf16)
            states_ref[n] = stb
            oraw_ref[n] = oraw_ref[n] + _dot_nt(qd_s[n], stb)
            st_s[...] = st * dec_s[n][0:1] + _dot_tn(i_ref[n], kd_s[n]) * hmf
            return carry

        lax.fori_loop(0, HG_N, step, 0, unroll=2)
        ngv = ng_ref[...]
        for t in range(HG_T):
            o = oraw_ref[:, t, :]
            ms = _dot_hl(o * o, hm) * (1.0 / HDIM)
            o_ref[:, t, :] = (o * lax.rsqrt(ms + EPS) * ngv * _silu(g_ref[:, t, :])).astype(bf16)

    one = pl.Buffered(1)
    col = lambda j: pl.BlockSpec((HG_N, HG_T, MIXW), lambda i: (0, 0, j), pipeline_mode=one)
    vec = pl.BlockSpec((1, MIXW), lambda i: (0, 0))
    blk = pl.BlockSpec((HG_N, HG_T, MIXW), lambda i: (0, 0, 0))
    s3 = pltpu.VMEM((HG_N, HG_T, MIXW), f32)
    return pl.pallas_call(
        body, name="hgrn_fwd", grid=(1,),
        in_specs=[col(3), col(4), col(5), col(6), vec, vec],
        out_specs=[blk, blk, pl.BlockSpec((HG_N, MIXW, MIXW), lambda i: (0, 0, 0), pipeline_mode=one)],
        out_shape=[_sds((HG_N, HG_T, MIXW), bf16), _sds((HG_N, HG_T, MIXW), f32), _sds((HG_N, MIXW, MIXW), bf16)],
        scratch_shapes=[s3, s3, s3, s3, s3, pltpu.VMEM((HG_N, 8, MIXW), f32),
                        pltpu.VMEM((HG_T * HG_N, MIXW), bf16), pltpu.VMEM((MIXW, MIXW), f32)],
        compiler_params=_params(("arbitrary",)),
    )(zm3, zm3, zm3, zm3, lb, ng)


def _gm_weights(ws_ref):
    tril = lax.broadcasted_iota(jnp.int32, (GM_T, GM_T), 0) >= lax.broadcasted_iota(jnp.int32, (GM_T, GM_T), 1)
    return tril, [jnp.where(tril, ws_ref[g], 0.0).astype(bf16) for g in range(NHEAD)]


def _gm_expand():
    r = lax.broadcasted_iota(jnp.int32, (8, MIXW), 0)
    c = lax.broadcasted_iota(jnp.int32, (8, MIXW), 1) // HDIM
    return (r == c).astype(bf16)


def _gm_mixed(vn, wts, bias, hm):
    vb = vn.astype(bf16)
    mixed = bias
    for g in range(NHEAD):
        mixed = mixed + jnp.where(hm[g], jnp.dot(wts[g], vb, preferred_element_type=f32), 0.0)
    return mixed


def _gm_bias(bs_ref):
    hi, lo = _split(bs_ref[...])
    et = _gm_expand()
    dn = (((0,), (0,)), ((), ()))
    return lax.dot_general(hi, et, dn, preferred_element_type=f32) + lax.dot_general(lo, et, dn, preferred_element_type=f32)


def _gmlp_fwd(zm, ng, ws, bs8):
    def body(u_ref, v_ref, ng_ref, ws_ref, bs_ref, o_ref):
        hm = _head_masks()
        _, wts = _gm_weights(ws_ref)
        bias = _gm_bias(bs_ref)
        ngv = ng_ref[...]

        def blk(n, carry):
            rows = pl.ds(pl.multiple_of(n * GM_T, GM_T), GM_T)
            vn = _rms(_gelu(v_ref[rows, :]), ngv)
            o_ref[rows, :] = (_gelu(u_ref[rows, :]) * _gm_mixed(vn, wts, bias, hm)).astype(bf16)
            return carry

        lax.fori_loop(0, SEQ // GM_T, blk, 0)

    col = lambda j: pl.BlockSpec((SEQ, MIXW), lambda i: (0, j))
    return pl.pallas_call(
        body, name="gmlp_fwd", grid=(1,),
        in_specs=[col(7), col(8), pl.BlockSpec((1, MIXW), lambda i: (0, 0)),
                  pl.BlockSpec((NHEAD, GM_T, GM_T), lambda i: (0, 0, 0)), pl.BlockSpec((8, GM_T), lambda i: (0, 0))],
        out_specs=pl.BlockSpec((SEQ, MIXW), lambda i: (0, 0)),
        out_shape=_sds((SEQ, MIXW), bf16),
        compiler_params=_params(("arbitrary",)),
    )(zm, zm, ng, ws, bs8)


def _lru_conv(x_ref, cw_ref, cb_ref, xp_s, xc_s):
    xp_s[pl.ds(0, 8), :] = jnp.zeros((8, MIXW), f32)
    xp_s[pl.ds(8, SEQ), :] = x_ref[...]
    cw = cw_ref[...]
    xc = cb_ref[...] + x_ref[...] * cw[3:4]
    for k in range(1, 4):
        xc = xc + xp_s[pl.ds(8 - k, SEQ), :] * cw[3 - k:4 - k]
    xc_s[...] = xc


def _lru_gates(xc, wa, ba, wx, bx, sp, first_row):
    r = _sigmoid(_dot(xc, wa) + ba)
    ig = _sigmoid(_dot(xc, wx) + bx)
    la = (-LRU_C) * r * sp
    a = jnp.exp(la)
    th = jnp.tanh(la)
    m2 = -2.0 * th / (1.0 - th)
    mult = jnp.where(first_row, 1.0, jnp.sqrt(jnp.maximum(m2, 0.0)))
    return a, mult, r, ig, m2


def _lru_scan(a, b, rev):
    row = lax.broadcasted_iota(jnp.int32, (LRU_T, 1), 0)
    k = 1
    while k < LRU_T:
        ok = (row < LRU_T - k) if rev else (row >= k)
        sh = (LRU_T - k) if rev else k
        a_sh = jnp.where(ok, pltpu.roll(a, sh, 0), 1.0)
        b_sh = jnp.where(ok, pltpu.roll(b, sh, 0), 0.0)
        b = b + a * b_sh
        a = a * a_sh
        k *= 2
    return a, b


def _lru_fwd(zm, cw8, cb, wa, ba, wx, bx, lam):
    def body(x_ref, g_ref, cw_ref, cb_ref, wa_ref, ba_ref, wx_ref, bx_ref, lam_ref, o_ref, h_ref, xp_s, xc_s):
        _lru_conv(x_ref, cw_ref, cb_ref, xp_s, xc_s)
        sp = jax.nn.softplus(-lam_ref[...])
        wa_v, wx_v, ba_v, bx_v = wa_ref[...], wx_ref[...], ba_ref[...], bx_ref[...]

        def chunk(c, h_prev):
            rows = pl.ds(pl.multiple_of(c * LRU_T, LRU_T), LRU_T)
            first = (lax.broadcasted_iota(jnp.int32, (LRU_T, 1), 0) + c * LRU_T) == 0
            xc = xc_s[rows, :]
            a, mult, _, ig, _ = _lru_gates(xc, wa_v, ba_v, wx_v, bx_v, sp, first)
            acum, hloc = _lru_scan(a, mult * (ig * xc), False)
            h = hloc + acum * h_prev
            h_ref[rows, :] = h
            o_ref[rows, :] = (h * _gelu(g_ref[rows, :])).astype(bf16)
            return h[LRU_T - 1:LRU_T, :]

        lax.fori_loop(0, SEQ // LRU_T, chunk, jnp.zeros((1, MIXW), f32))

    col = lambda j: pl.BlockSpec((SEQ, MIXW), lambda i: (0, j))
    vec = pl.BlockSpec((1, MIXW), lambda i: (0, 0))
    mat = pl.BlockSpec((MIXW, MIXW), lambda i: (0, 0))
    out = pl.BlockSpec((SEQ, MIXW), lambda i: (0, 0))
    return pl.pallas_call(
        body, name="lru_fwd", grid=(1,),
        in_specs=[col(9), col(10), pl.BlockSpec((8, MIXW), lambda i: (0, 0)), vec, mat, vec, mat, vec, vec],
        out_specs=[out, out],
        out_shape=[_sds((SEQ, MIXW), bf16), _sds((SEQ, MIXW), f32)],
        scratch_shapes=[pltpu.VMEM((SEQ + 8, MIXW), f32), pltpu.VMEM((SEQ, MIXW), f32)],
        compiler_params=_params(("arbitrary",)),
    )(zm, zm, cw8, cb, wa, ba, wx, bx, lam)


def _block_diag(w):
    out = jnp.zeros((MIXW, MIXW), w.dtype)
    for h in range(NHEAD):
        out = lax.dynamic_update_slice(out, w[h], (h * HDIM, h * HDIM))
    return out


def _diag_blocks(w):
    return jnp.stack([w[h * HDIM:(h + 1) * HDIM, h * HDIM:(h + 1) * HDIM] for h in range(NHEAD)])


ROW_TILE = 256


def _merge_fwd(outs, zg, wb, wo, x, g2):
    def body(oa_ref, ob_ref, oc_ref, od_ref, zg_ref, wb_ref, wo_ref, x_ref, g_ref, xo_ref, mg_ref, y_ref):
        merged = jnp.zeros((ROW_TILE, DM), f32)
        for n, o_ref in enumerate((oa_ref, ob_ref, oc_ref, od_ref)):
            proj = jnp.dot(o_ref[...], wb_ref[n], preferred_element_type=f32)
            merged = merged + _sigmoid(zg_ref[:, n * DM:(n + 1) * DM]) * proj
        mb = merged.astype(bf16)
        y = jnp.dot(mb, wo_ref[...], preferred_element_type=f32)
        mg_ref[...] = mb
        y_ref[...] = y
        xo_ref[...] = x_ref[...] + _rms(y, g_ref[...])

    row = lambda w: pl.BlockSpec((ROW_TILE, w), lambda i: (i, 0))
    return pl.pallas_call(
        body, name="merge_fwd", grid=(SEQ // ROW_TILE,),
        in_specs=[row(MIXW)] * 4 + [row(NGATE), pl.BlockSpec((NHEAD, MIXW, DM), lambda i: (0, 0, 0)),
                                    pl.BlockSpec((DM, DM), lambda i: (0, 0)), row(DM), pl.BlockSpec((1, DM), lambda i: (0, 0))],
        out_specs=[row(DM), row(DM), row(DM)],
        out_shape=[_sds((SEQ, DM), f32), _sds((SEQ, DM), bf16), _sds((SEQ, DM), f32)],
        compiler_params=_params(("parallel",)),
    )(*outs, zg, wb, wo, x, g2)


def _ffn_out(u, w2, x, g4):
    def body(u_ref, w_ref, x_ref, g_ref, xo_ref, f_ref):
        a = _silu(u_ref[:, :FFH]) * u_ref[:, FFH:]
        f = jnp.dot(a.astype(bf16), w_ref[...], preferred_element_type=f32)
        f_ref[...] = f
        xo_ref[...] = x_ref[...] + _rms(f, g_ref[...])

    row = lambda w: pl.BlockSpec((ROW_TILE, w), lambda i: (i, 0))
    return pl.pallas_call(
        body, name="ffn_out", grid=(SEQ // ROW_TILE,),
        in_specs=[row(2 * FFH), pl.BlockSpec((FFH, DM), lambda i: (0, 0)), row(DM), pl.BlockSpec((1, DM), lambda i: (0, 0))],
        out_specs=[row(DM), row(DM)],
        out_shape=[_sds((SEQ, DM), f32), _sds((SEQ, DM), f32)],
        compiler_params=_params(("parallel",)),
    )(u, w2, x, g4)


def _loss_head(x, tgt):
    tm = 512

    def body(x_ref, t_ref, l_ref, dx_ref):
        @pl.when(pl.program_id(0) == 0)
        def _():
            l_ref[...] = jnp.zeros((1, 1), f32)

        d = x_ref[...] - t_ref[...]
        dx_ref[...] = d * (1.0 / DM)
        l_ref[...] += (0.5 / DM) * jnp.sum(d * d).reshape(1, 1)

    row = pl.BlockSpec((tm, DM), lambda i: (i, 0))
    return pl.pallas_call(
        body, name="loss_head", grid=(SEQ // tm,),
        in_specs=[row, row], out_specs=[pl.BlockSpec((1, 1), lambda i: (0, 0)), row],
        out_shape=[_sds((1, 1), f32), _sds((SEQ, DM), f32)],
        compiler_params=_params(("arbitrary",)),
    )(x, tgt)


def _lb_fwd(logits):
    def body(lg_ref, o_ref):
        lg = lg_ref[...]
        e = jnp.exp(lg - jnp.max(lg, axis=0, keepdims=True))
        p = e / jnp.sum(e, axis=0, keepdims=True)
        acc = jnp.zeros((1, MIXW), f32)
        o_ref[0:1, :] = acc
        for l in range(1, DEPTH):
            acc = acc + p[l:l + 1]
            o_ref[l:l + 1, :] = acc

    return pl.pallas_call(body, name="lb_fwd", out_shape=_sds((DEPTH, MIXW), f32))(logits)


def _lb_bwd(logits, dlbs):
    def body(lg_ref, d_ref, o_ref):
        lg = lg_ref[...]
        e = jnp.exp(lg - jnp.max(lg, axis=0, keepdims=True))
        p = e / jnp.sum(e, axis=0, keepdims=True)
        d = d_ref[...]
        dp = [jnp.zeros((1, MIXW), f32)] * DEPTH
        acc = jnp.zeros((1, MIXW), f32)
        for j in range(DEPTH - 1, 0, -1):
            acc = acc + d[j:j + 1]
            dp[j] = acc
        inner = sum(p[j:j + 1] * dp[j] for j in range(DEPTH))
        for j in range(DEPTH):
            o_ref[j:j + 1, :] = p[j:j + 1] * (dp[j] - inner)

    return pl.pallas_call(body, name="lb_bwd", out_shape=_sds((DEPTH, MIXW), f32))(logits, dlbs)


def _pad_rows(a, rows=8):
    return jnp.concatenate([a, jnp.zeros((rows - a.shape[0], a.shape[1]), a.dtype)], axis=0)


def _layer_params(l, full, small, lbs):
    row = lambda name: small[name][l][None]
    return dict(
        _mix_weights(full), **(_ffn_weights(full) if "w_ffn_in" in full else {}),
        g1=row("norm_mix_pre"), g2=row("norm_mix_post"), g3=row("norm_ffn_pre"), g4=row("norm_ffn_post"),
        rb8=_pad_rows(small["attn_rel_bias"][l]), lb=lbs[l][None], hng=row("hgrn_norm_g"),
        gng=row("gmlp_norm_g"), gws=small["gmlp_ws"][l], gbs8=_pad_rows(small["gmlp_bs"][l]),
        cw8=_pad_rows(small["lru_conv_w"][l]), cb=row("lru_conv_b"),
        wa=_block_diag(small["lru_wa"][l]).astype(bf16), ba=row("lru_ba"),
        wx=_block_diag(small["lru_wx"][l]).astype(bf16), bx=row("lru_bx"), lam=row("lru_lambda"),
    )


def _mix_weights(full):
    return dict(wm=full["w_in"][:, :NMIX], wgt=full["w_in"][:, NMIX:], wb=full["w_branch"], wo=full["w_out"])


def _ffn_weights(full):
    return dict(w1=full["w_ffn_in"], w2=full["w_ffn_out"])


def _layer_fwd(x, p, late_ffn_weights=None):
    zm, h = _norm_matmul(x, p["g1"], p["wm"], 1408)
    zg = _matmul(h, p["wgt"], 1024)
    oa = _attn_fwd(zm, p["rb8"])
    ob3, obraw3, hstates = _hgrn_fwd(zm.reshape(HG_N, HG_T, NMIX), p["lb"], p["hng"])
    oc = _gmlp_fwd(zm, p["gng"], p["gws"], p["gbs8"])
    od, hd = _lru_fwd(zm, p["cw8"], p["cb"], p["wa"], p["ba"], p["wx"], p["bx"], p["lam"])
    outs = (oa, ob3.reshape(SEQ, MIXW), oc, od)
    x1, merged, y = _merge_fwd(outs, zg, p["wb"], p["wo"], x, p["g2"])
    if late_ffn_weights is not None:
        p.update(late_ffn_weights(x1))
    u, h2 = _norm_matmul(x1, p["g3"], p["w1"], 1408)
    x2, f = _ffn_out(u, p["w2"], x1, p["g4"])
    saved = dict(x=x, h=h, zm=zm, zg=zg, outs=outs, obraw3=obraw3, hstates=hstates, hd=hd, x1=x1, merged=merged, y=y, u=u, h2=h2, f=f)
    return x2, saved


def _att_bias_grad(db_ref, o_ref):
    r = lax.broadcasted_iota(jnp.int32, (ATT_PAIR, ATT_PAIR), 0)
    c = lax.broadcasted_iota(jnp.int32, (ATT_PAIR, ATT_PAIR), 1)
    flip = (r + c == ATT_PAIR - 1).astype(bf16)
    rows = []
    for h in range(NHEAD):
        d = jnp.concatenate([db_ref[h], jnp.zeros((ATT_PAIR, ATT_WV - ATT_BAND), f32)], axis=1)
        hi, lo = _split(d)
        rev = jnp.dot(flip, hi, preferred_element_type=f32) + jnp.dot(flip, lo, preferred_element_type=f32)
        lined = pltpu.roll(rev, ATT_WV - (ATT_PAIR - 1), 1, stride=1, stride_axis=0)
        rows.append(jnp.sum(lined, axis=0, keepdims=True))
    dwv = jnp.concatenate(rows + [jnp.zeros((8 - NHEAD, ATT_WV), f32)], axis=0)
    hi, lo = _split(dwv)
    m = _att_offset_map()
    dn = (((1,), (1,)), ((), ()))
    o_ref[...] = lax.dot_general(hi, m, dn, preferred_element_type=f32) + lax.dot_general(lo, m, dn, preferred_element_type=f32)


def _attn_bwd(zm, rb8, do):
    def body(q_ref, k_ref, v_ref, rb_ref, do_ref, dz_ref, drb_ref, kp_ref, vp_ref, bm_ref, dk_s, dv_s, db_s):
        _att_pad_kv(k_ref, v_ref, kp_ref, vp_ref)
        _att_bias_tiles(rb_ref, bm_ref)
        dk_s[...] = jnp.zeros_like(dk_s)
        dv_s[...] = jnp.zeros_like(dv_s)
        db_s[...] = jnp.zeros_like(db_s)
        hm = _head_masks()
        scale = HDIM ** -0.5

        def pair(p, carry):
            r0 = pl.multiple_of(p * ATT_PAIR, ATT_PAIR)
            q = q_ref[pl.ds(r0, ATT_PAIR), :] * scale
            dout = do_ref[pl.ds(r0, ATT_PAIR), :]
            kb = kp_ref[pl.ds(r0, ATT_BAND), :]
            vb = vp_ref[pl.ds(r0, ATT_BAND), :]
            key_ok = (lax.broadcasted_iota(jnp.int32, (1, ATT_BAND), 1) + (r0 - ATT_PAD)) >= 0
            dq = jnp.zeros((ATT_PAIR, MIXW), f32)
            dkb = jnp.zeros((ATT_BAND, MIXW), f32)
            dvb = jnp.zeros((ATT_BAND, MIXW), f32)
            for h in range(NHEAD):
                qm = jnp.where(hm[h], q, 0.0).astype(bf16)
                dom = jnp.where(hm[h], dout, 0.0).astype(bf16)
                p_h = _att_probs(qm, kb, bm_ref[h], key_ok)
                dp = _dot_nt(dom, vb)
                ds = p_h * (dp - jnp.sum(dp * p_h, axis=-1, keepdims=True))
                dsb = ds.astype(bf16)
                dq = dq + jnp.where(hm[h], _dot(dsb, kb), 0.0)
                dkb = dkb + _dot_tn(dsb, qm)
                dvb = dvb + _dot_tn(p_h, dom)
                db_s[h] = db_s[h] + ds
            dz_ref[pl.ds(r0, ATT_PAIR), 0:MIXW] = (dq * scale).astype(bf16)
            dk_s[pl.ds(r0, ATT_BAND), :] = dk_s[pl.ds(r0, ATT_BAND), :] + dkb
            dv_s[pl.ds(r0, ATT_BAND), :] = dv_s[pl.ds(r0, ATT_BAND), :] + dvb
            return carry

        lax.fori_loop(0, SEQ // ATT_PAIR, pair, 0)
        dz_ref[:, MIXW:2 * MIXW] = dk_s[pl.ds(ATT_PAD, SEQ), :].astype(bf16)
        dz_ref[:, 2 * MIXW:3 * MIXW] = dv_s[pl.ds(ATT_PAD, SEQ), :].astype(bf16)
        _att_bias_grad(db_s, drb_ref)

    col = lambda j: pl.BlockSpec((SEQ, MIXW), lambda i: (0, j))
    return pl.pallas_call(
        body, name="attn_bwd", grid=(1,),
        in_specs=[col(0), col(1), col(2), pl.BlockSpec((8, REL_SIZE), lambda i: (0, 0)), pl.BlockSpec((SEQ, MIXW), lambda i: (0, 0))],
        out_specs=[pl.BlockSpec((SEQ, 3 * MIXW), lambda i: (0, 0)), pl.BlockSpec((8, REL_SIZE), lambda i: (0, 0))],
        out_shape=[_sds((SEQ, 3 * MIXW), bf16), _sds((8, REL_SIZE), f32)],
        scratch_shapes=[pltpu.VMEM((SEQ + ATT_PAD, MIXW), bf16), pltpu.VMEM((SEQ + ATT_PAD, MIXW), bf16),
                        pltpu.VMEM((NHEAD, ATT_PAIR, ATT_BAND), f32),
                        pltpu.VMEM((SEQ + ATT_PAD, MIXW), f32), pltpu.VMEM((SEQ + ATT_PAD, MIXW), f32),
                        pltpu.VMEM((NHEAD, ATT_PAIR, ATT_BAND), f32)],
        compiler_params=_params(("arbitrary",)),
    )(zm, zm, zm, rb8, do)


def _hgrn_out_bwd(zm3, ng, oraw3, do3):
    def body(g_ref, ng_ref, o_ref, do_ref, dor_ref, dg_ref, dng_ref):
        hm = _same_head(MIXW, HDIM, bf16)
        ngv = ng_ref[...]
        dng = jnp.zeros((1, MIXW), f32)
        for t in range(HG_T):
            o, g, d = o_ref[:, t, :], g_ref[:, t, :], do_ref[:, t, :]
            rs = lax.rsqrt(_dot_hl(o * o, hm) * (1.0 / HDIM) + EPS)
            y1 = o * rs
            dy2 = d * _silu(g)
            dg_ref[:, t, :] = (d * y1 * ngv * _dsilu(g)).astype(bf16)
            dng = dng + jnp.sum(dy2 * y1, axis=0, keepdims=True)
            dy1 = dy2 * ngv
            dor_ref[:, t, :] = rs * (dy1 - y1 * (_dot_hl(dy1 * y1, hm) * (1.0 / HDIM)))
        dng_ref[...] = jnp.broadcast_to(dng, (8, MIXW))

    blk = pl.BlockSpec((HG_N, HG_T, MIXW), lambda i: (0, 0, 0))
    return pl.pallas_call(
        body, name="hgrn_out_bwd", grid=(1,),
        in_specs=[pl.BlockSpec((HG_N, HG_T, MIXW), lambda i: (0, 0, 6)), pl.BlockSpec((1, MIXW), lambda i: (0, 0)), blk, blk],
        out_specs=[blk, blk, pl.BlockSpec((8, MIXW), lambda i: (0, 0))],
        out_shape=[_sds((HG_N, HG_T, MIXW), f32), _sds((HG_N, HG_T, MIXW), bf16), _sds((8, MIXW), f32)],
        compiler_params=_params(("arbitrary",)),
    )(zm3, ng, oraw3, do3)


def _hgrn_bwd(zm3, lb, dor3, states):
    def body(q_ref, f_ref, i_ref, lb_ref, dor_ref, st_s, dz_ref, dlb_ref,
             qf_s, kf_s, b_s, dq_s, dk_s, db_s, dv_s, w_s, x_s, cur_s):
        lb = lb_ref[...]
        hm = _same_head(MIXW, HDIM, bf16)
        hmf = _same_head(MIXW, HDIM, f32)
        b = None
        for t in range(HG_T):
            qf, kf, lf, _, _, _ = _hg_gates(q_ref[:, t, :], f_ref[:, t, :], lb)
            b = lf if b is None else b + lf
            qf_s[:, t, :] = qf
            kf_s[:, t, :] = kf
            b_s[:, t, :] = b

        def block_terms(n):
            bn = b_s[n]
            bl = bn[HG_T - 1:HG_T]
            eb = jnp.exp(bn)
            ek = jnp.exp(bl - bn)
            return qf_s[n] * eb, kf_s[n] * ek, jnp.exp(bl), eb, ek

        cur_s[...] = jnp.zeros((MIXW, MIXW), f32)
        last = lax.broadcasted_iota(jnp.int32, (HG_T, 1), 0) == HG_T - 1

        def bwd_step(j, carry):
            n = HG_N - 1 - j
            qd, kd, dec, eb, ek = block_terms(n)
            v, do_n = i_ref[n], dor_ref[n]
            dst = cur_s[...]
            st = st_s[n]
            dqd = _dot(do_n, st)
            dkd = _dot(v, dst)
            ddec = jnp.sum(dst * st.astype(f32), axis=0, keepdims=True)
            cur_s[...] = dst * dec + _dot_tn(do_n, qd) * hmf
            dq_s[n] = dqd * eb
            dk_s[n] = dkd * ek
            dv_s[n] = _dot_nt(kd, dst)
            dbl = jnp.sum(dkd * kd, axis=0, keepdims=True) + ddec * dec
            db_s[n] = dqd * qd - dkd * kd + jnp.where(last, dbl, 0.0)
            return carry

        lax.fori_loop(0, HG_N, bwd_step, 0, unroll=2)
        for t in range(HG_T):
            qt, bt, dot_t = qf_s[:, t, :], b_s[:, t, :], dor_ref[:, t, :]
            for s in range(t + 1):
                w = qt * kf_s[:, s, :]
                if s < t:
                    w = w * jnp.exp(bt - b_s[:, s, :])
                w_s[pl.ds(s * HG_N, HG_N), :] = w.astype(bf16)
                x_s[pl.ds(s * HG_N, HG_N), :] = (dot_t * i_ref[:, s, :]).astype(bf16)
            p = jnp.dot(w_s[pl.ds(0, (t + 1) * HG_N), :], hm, preferred_element_type=f32)
            dp = jnp.dot(x_s[pl.ds(0, (t + 1) * HG_N), :], hm, preferred_element_type=f32)
            dq_t = jnp.zeros((HG_N, MIXW), f32)
            db_t = jnp.zeros((HG_N, MIXW), f32)
            for s in range(t + 1):
                ps = p[s * HG_N:(s + 1) * HG_N]
                dps = dp[s * HG_N:(s + 1) * HG_N]
                ks = kf_s[:, s, :]
                dv_s[:, s, :] = dv_s[:, s, :] + ps * dot_t
                if s < t:
                    dec_ts = jnp.exp(bt - b_s[:, s, :])
                    g1 = dps * ks * dec_ts
                    dk_s[:, s, :] = dk_s[:, s, :] + dps * qt * dec_ts
                    gw = g1 * qt
                    db_t = db_t + gw
                    db_s[:, s, :] = db_s[:, s, :] - gw
                else:
                    g1 = dps * ks
                    dk_s[:, s, :] = dk_s[:, s, :] + dps * qt
                dq_t = dq_t + g1
            dq_s[:, t, :] = dq_s[:, t, :] + dq_t
            db_s[:, t, :] = db_s[:, t, :] + db_t
        run = jnp.zeros((HG_N, MIXW), f32)
        dlb = jnp.zeros((1, MIXW), f32)
        oml = 1.0 - lb
        for t in range(HG_T - 1, -1, -1):
            run = run + db_s[:, t, :]
            q = q_ref[:, t, :]
            _, _, _, sq, sg, f = _hg_gates(q, f_ref[:, t, :], lb)
            dkf = dk_s[:, t, :]
            df = jnp.where(f > LOG_FLOOR, run / f, 0.0)
            dsg = (df - dkf) * oml
            dlb = dlb + jnp.sum((df - dkf) * (1.0 - sg), axis=0, keepdims=True)
            dz_ref[:, t, 0:MIXW] = (dq_s[:, t, :] * sq * (1.0 + q * (1.0 - sq))).astype(bf16)
            dz_ref[:, t, MIXW:2 * MIXW] = (dsg * sg * (1.0 - sg)).astype(bf16)
            dz_ref[:, t, 2 * MIXW:3 * MIXW] = dv_s[:, t, :].astype(bf16)
        dlb_ref[...] = jnp.broadcast_to(dlb, (8, MIXW))

    one = pl.Buffered(1)
    col = lambda j: pl.BlockSpec((HG_N, HG_T, MIXW), lambda i: (0, 0, j), pipeline_mode=one)
    s3 = pltpu.VMEM((HG_N, HG_T, MIXW), f32)
    return pl.pallas_call(
        body, name="hgrn_bwd", grid=(1,),
        in_specs=[col(3), col(4), col(5), pl.BlockSpec((1, MIXW), lambda i: (0, 0)),
                  pl.BlockSpec((HG_N, HG_T, MIXW), lambda i: (0, 0, 0), pipeline_mode=one),
                  pl.BlockSpec((HG_N, MIXW, MIXW), lambda i: (0, 0, 0), pipeline_mode=one)],
        out_specs=[pl.BlockSpec((HG_N, HG_T, 3 * MIXW), lambda i: (0, 0, 0)), pl.BlockSpec((8, MIXW), lambda i: (0, 0))],
        out_shape=[_sds((HG_N, HG_T, 3 * MIXW), bf16), _sds((8, MIXW), f32)],
        scratch_shapes=[s3, s3, s3, s3, s3, s3, s3,
                        pltpu.VMEM((HG_T * HG_N, MIXW), bf16), pltpu.VMEM((HG_T * HG_N, MIXW), bf16),
                        pltpu.VMEM((MIXW, MIXW), f32)],
        compiler_params=_params(("arbitrary",)),
    )(zm3, zm3, zm3, lb, dor3, states)


def _gmlp_bwd(zm, ng, ws, bs8, do):
    def body(u_ref, v_ref, ng_ref, ws_ref, bs_ref, do_ref, dz_ref, dws_ref, dng_ref, dbs_ref, dm_s):
        hm = _head_masks()
        tril, wts = _gm_weights(ws_ref)
        bias = _gm_bias(bs_ref)
        ngv = ng_ref[...]
        dws_ref[...] = jnp.zeros_like(dws_ref)
        dm_s[...] = jnp.zeros_like(dm_s)

        def blk(n, dng):
            rows = pl.ds(pl.multiple_of(n * GM_T, GM_T), GM_T)
            cu, cv, d = u_ref[rows, :], v_ref[rows, :], do_ref[rows, :]
            v = _gelu(cv)
            r = lax.rsqrt(jnp.mean(v * v, axis=-1, keepdims=True) + EPS)
            vh = v * r
            vn = vh * ngv
            u = _gelu(cu)
            dm = d * u
            dmb, vnb = dm.astype(bf16), vn.astype(bf16)
            dvn = jnp.zeros((GM_T, MIXW), f32)
            for g in range(NHEAD):
                dws_ref[g] = dws_ref[g] + _dot_nt(jnp.where(hm[g], dm, 0.0), vnb)
                dvn = dvn + jnp.where(hm[g], _dot_tn(wts[g], dmb), 0.0)
            dm_s[...] = dm_s[...] + dm
            dvh = dvn * ngv
            dv = r * (dvh - vh * jnp.mean(dvh * vh, axis=-1, keepdims=True))
            dz_ref[rows, 0:MIXW] = (d * _gm_mixed(vn, wts, bias, hm) * _dgelu(cu)).astype(bf16)
            dz_ref[rows, MIXW:2 * MIXW] = (dv * _dgelu(cv)).astype(bf16)
            return dng + jnp.sum(dvn * vh, axis=0, keepdims=True)

        dng = lax.fori_loop(0, SEQ // GM_T, blk, jnp.zeros((1, MIXW), f32))
        dng_ref[...] = jnp.broadcast_to(dng, (8, MIXW))
        for g in range(NHEAD):
            dws_ref[g] = jnp.where(tril, dws_ref[g], 0.0)
        dbs_ref[...] = _dot_nt_hl(_gm_expand(), dm_s[...])

    col = lambda j: pl.BlockSpec((SEQ, MIXW), lambda i: (0, j))
    return pl.pallas_call(
        body, name="gmlp_bwd", grid=(1,),
        in_specs=[col(7), col(8), pl.BlockSpec((1, MIXW), lambda i: (0, 0)),
                  pl.BlockSpec((NHEAD, GM_T, GM_T), lambda i: (0, 0, 0)), pl.BlockSpec((8, GM_T), lambda i: (0, 0)),
                  pl.BlockSpec((SEQ, MIXW), lambda i: (0, 0))],
        out_specs=[pl.BlockSpec((SEQ, 2 * MIXW), lambda i: (0, 0)), pl.BlockSpec((NHEAD, GM_T, GM_T), lambda i: (0, 0, 0)),
                   pl.BlockSpec((8, MIXW), lambda i: (0, 0)), pl.BlockSpec((8, GM_T), lambda i: (0, 0))],
        out_shape=[_sds((SEQ, 2 * MIXW), bf16), _sds((NHEAD, GM_T, GM_T), f32), _sds((8, MIXW), f32), _sds((8, GM_T), f32)],
        scratch_shapes=[pltpu.VMEM((GM_T, MIXW), f32)],
        compiler_params=_params(("arbitrary",)),
    )(zm, zm, ng, ws, bs8, do)


def _lru_bwd(zm, cw8, cb, wa, ba, wx, bx, lam, hd, do):
    nchunk = SEQ // LRU_T

    def body(x_ref, g_ref, cw_ref, cb_ref, wa_ref, ba_ref, wx_ref, bx_ref, lam_ref, h_ref, do_ref,
             dz_ref, dwa_ref, dwx_ref, dcw_ref, dvec_ref, xp_s, xc_s, dxc_s):
        _lru_conv(x_ref, cw_ref, cb_ref, xp_s, xc_s)
        lam_v = lam_ref[...]
        sp = jax.nn.softplus(-lam_v)
        sgl = _sigmoid(-lam_v)
        wa_v, wx_v, ba_v, bx_v = wa_ref[...], wx_ref[...], ba_ref[...], bx_ref[...]
        dwa_ref[...] = jnp.zeros_like(dwa_ref)
        dwx_ref[...] = jnp.zeros_like(dwx_ref)
        dxc_s[pl.ds(SEQ, 8), :] = jnp.zeros((8, MIXW), f32)
        row = lax.broadcasted_iota(jnp.int32, (LRU_T, 1), 0)
        zero = jnp.zeros((1, MIXW), f32)

        def chunk(j, carry):
            dh_next, a_next, dba, dbx, dlam = carry
            c = nchunk - 1 - j
            rows = pl.ds(pl.multiple_of(c * LRU_T, LRU_T), LRU_T)
            prev = pl.ds(pl.multiple_of(jnp.maximum(c - 1, 0) * LRU_T, LRU_T), LRU_T)
            first = (row + c * LRU_T) == 0
            xc, gate, d, h = xc_s[rows, :], g_ref[rows, :], do_ref[rows, :], h_ref[rows, :]
            a, mult, r, ig, m2 = _lru_gates(xc, wa_v, ba_v, wx_v, bx_v, sp, first)
            h_last = jnp.where(c > 0, h_ref[prev, :][LRU_T - 1:LRU_T, :], 0.0)
            h_m1 = jnp.where(row == 0, h_last, pltpu.roll(h, 1, 0))
            a_up = jnp.where(row == LRU_T - 1, a_next, pltpu.roll(a, LRU_T - 1, 0))
            acum, dh_loc = _lru_scan(a_up, d * _gelu(gate), True)
            dh = dh_loc + acum * dh_next
            dmult = jnp.where(first, 0.0, dh * (ig * xc))
            msq = jnp.sqrt(jnp.maximum(m2, 0.0))
            dla = dh * h_m1 * a + jnp.where(m2 > 0.0, -dmult * (1.0 - m2) / msq, 0.0)
            dpr = dla * (-LRU_C) * sp * r * (1.0 - r)
            dpi = dh * mult * xc * ig * (1.0 - ig)
            dxc_s[rows, :] = dh * mult * ig + _dot_nt(dpr, wa_v) + _dot_nt(dpi, wx_v)
            dwa_ref[...] = dwa_ref[...] + _dot_tn(xc, dpr)
            dwx_ref[...] = dwx_ref[...] + _dot_tn(xc, dpi)
            dz_ref[rows, MIXW:2 * MIXW] = (d * h * _dgelu(gate)).astype(bf16)
            return (dh[0:1], a[0:1], dba + jnp.sum(dpr, axis=0, keepdims=True), dbx + jnp.sum(dpi, axis=0, keepdims=True),
                    dlam + jnp.sum(dla * r, axis=0, keepdims=True) * (LRU_C * sgl))

        _, _, dba, dbx, dlam = lax.fori_loop(0, nchunk, chunk, (zero, zero, zero, zero, zero))
        cw = cw_ref[...]
        dxc = dxc_s[pl.ds(0, SEQ), :]
        dx = dxc * cw[3:4]
        dcw = [None] * 4
        dcw[3] = jnp.sum(dxc * x_ref[...], axis=0, keepdims=True)
        for k in range(1, 4):
            dx = dx + dxc_s[pl.ds(k, SEQ), :] * cw[3 - k:4 - k]
            dcw[3 - k] = jnp.sum(dxc * xp_s[pl.ds(8 - k, SEQ), :], axis=0, keepdims=True)
        dz_ref[:, 0:MIXW] = dx.astype(bf16)
        dcw_ref[...] = jnp.concatenate(dcw + [jnp.zeros((4, MIXW), f32)], axis=0)
        dvec_ref[...] = jnp.concatenate([jnp.sum(dxc, axis=0, keepdims=True), dba, dbx, dlam, jnp.zeros((4, MIXW), f32)], axis=0)

    col = lambda j: pl.BlockSpec((SEQ, MIXW), lambda i: (0, j))
    vec = pl.BlockSpec((1, MIXW), lambda i: (0, 0))
    vec8 = pl.BlockSpec((8, MIXW), lambda i: (0, 0))
    mat = pl.BlockSpec((MIXW, MIXW), lambda i: (0, 0))
    full = pl.BlockSpec((SEQ, MIXW), lambda i: (0, 0))
    return pl.pallas_call(
        body, name="lru_bwd", grid=(1,),
        in_specs=[col(9), col(10), vec8, vec, mat, vec, mat, vec, vec, full, full],
        out_specs=[pl.BlockSpec((SEQ, 2 * MIXW), lambda i: (0, 0)), mat, mat, vec8, vec8],
        out_shape=[_sds((SEQ, 2 * MIXW), bf16), _sds((MIXW, MIXW), f32), _sds((MIXW, MIXW), f32),
                   _sds((8, MIXW), f32), _sds((8, MIXW), f32)],
        scratch_shapes=[pltpu.VMEM((SEQ + 8, MIXW), f32), pltpu.VMEM((SEQ, MIXW), f32), pltpu.VMEM((SEQ + 8, MIXW), f32)],
        compiler_params=_params(("arbitrary",)),
    )(zm, zm, cw8, cb, wa, ba, wx, bx, lam, hd, do)


def _matmul_tn(a, b, tm, tn, b_col0=0):
    m = a.shape[1]
    n = tn if b_col0 else b.shape[1]
    off = b_col0 // tn

    def body(a_ref, b_ref, o_ref):
        o_ref[...] = _dot_tn(a_ref[...], b_ref[...]).astype(bf16)

    return pl.pallas_call(
        body, name="matmul_tn", grid=(m // tm, n // tn),
        in_specs=[pl.BlockSpec((SEQ, tm), lambda i, j: (0, i)), pl.BlockSpec((SEQ, tn), lambda i, j: (0, j + off))],
        out_specs=pl.BlockSpec((tm, tn), lambda i, j: (i, j)),
        out_shape=_sds((m, n), bf16),
        compiler_params=_params(("parallel", "arbitrary")),
    )(a, b)


def _matmul_nt_norm(pairs, x, g, dres):
    tm = 512
    steps = [a.shape[1] // t for a, _, t in pairs]
    starts = [sum(steps[:i]) for i in range(len(pairs))]
    total = sum(steps)
    npair = len(pairs)

    def body(*refs):
        a_refs, w_refs = refs[0:2 * npair:2], refs[1:2 * npair:2]
        x_ref, g_ref, dres_ref, dx_ref, dg_ref, acc_s = refs[2 * npair:]
        i, k = pl.program_id(0), pl.program_id(1)

        @pl.when(k == 0)
        def _():
            acc_s[...] = jnp.zeros_like(acc_s)

        @pl.when((i == 0) & (k == 0))
        def _():
            dg_ref[...] = jnp.zeros_like(dg_ref)

        for q in range(npair):
            @pl.when((k >= starts[q]) & (k < starts[q] + steps[q]))
            def _(q=q):
                acc_s[...] += _dot_nt(a_refs[q][...], w_refs[q][...])

        @pl.when(k == total - 1)
        def _():
            dx, dg = _rms_bwd(x_ref[...], g_ref[...], acc_s[...])
            dx_ref[...] = dres_ref[...] + dx
            dg_ref[...] += dg

    in_specs, args = [], []
    for q, (a, w, t) in enumerate(pairs):
        kmap = lambda k, q=q: jnp.clip(k - starts[q], 0, steps[q] - 1)
        in_specs += [pl.BlockSpec((tm, t), lambda i, k, kmap=kmap: (i, kmap(k))),
                     pl.BlockSpec((DM, t), lambda i, k, kmap=kmap: (0, kmap(k)))]
        args += [a, w]
    row = pl.BlockSpec((tm, DM), lambda i, k: (i, 0))
    vec = pl.BlockSpec((1, DM), lambda i, k: (0, 0))
    return pl.pallas_call(
        body, name="matmul_nt_norm", grid=(SEQ // tm, total),
        in_specs=in_specs + [row, vec, row], out_specs=[row, vec],
        out_shape=[_sds((SEQ, DM), f32), _sds((1, DM), f32)],
        scratch_shapes=[pltpu.VMEM((tm, DM), f32)],
        compiler_params=_params(("arbitrary", "arbitrary")),
    )(*args, x, g, dres)


def _merge_bwd(dx1, y, g2, outs, zg, wb, wo):
    def body(dx_ref, y_ref, g_ref, oa_ref, ob_ref, oc_ref, od_ref, zg_ref, wb_ref, wo_ref,
             da_ref, db_ref, dc_ref, dd_ref, dzg_ref, dpj_ref, dy_ref, dg_ref):
        @pl.when(pl.program_id(0) == 0)
        def _():
            dg_ref[...] = jnp.zeros_like(dg_ref)

        dy, dg = _rms_bwd(y_ref[...], g_ref[...], dx_ref[...])
        dg_ref[...] += dg
        dyb = dy.astype(bf16)
        dy_ref[...] = dyb
        dmerged = _dot_nt(dyb, wo_ref[...])
        for n, (o_ref, do_ref) in enumerate(((oa_ref, da_ref), (ob_ref, db_ref), (oc_ref, dc_ref), (od_ref, dd_ref))):
            cols = slice(n * DM, (n + 1) * DM)
            gate = _sigmoid(zg_ref[:, cols])
            proj = jnp.dot(o_ref[...], wb_ref[n], preferred_element_type=f32)
            dproj = (dmerged * gate).astype(bf16)
            dpj_ref[:, cols] = dproj
            dzg_ref[:, cols] = (dmerged * proj * gate * (1.0 - gate)).astype(bf16)
            do_ref[...] = _dot_nt(dproj, wb_ref[n])

    row = lambda w: pl.BlockSpec((ROW_TILE, w), lambda i: (i, 0))
    vec = pl.BlockSpec((1, DM), lambda i: (0, 0))
    return pl.pallas_call(
        body, name="merge_bwd", grid=(SEQ // ROW_TILE,),
        in_specs=[row(DM), row(DM), vec] + [row(MIXW)] * 4 + [row(NGATE), pl.BlockSpec((NHEAD, MIXW, DM), lambda i: (0, 0, 0)),
                                                              pl.BlockSpec((DM, DM), lambda i: (0, 0))],
        out_specs=[row(MIXW)] * 4 + [row(NGATE), row(NGATE), row(DM), vec],
        out_shape=[_sds((SEQ, MIXW), f32)] * 4 + [_sds((SEQ, NGATE), bf16), _sds((SEQ, NGATE), bf16), _sds((SEQ, DM), bf16),
                                                  _sds((1, DM), f32)],
        compiler_params=_params(("arbitrary",)),
    )(dx1, y, g2, *outs, zg, wb, wo)


def _ffn_bwd(dx2, f, g4, u, w2):
    def body(dx_ref, f_ref, g_ref, u_ref, w_ref, du_ref, a_ref, df_ref, dg_ref):
        @pl.when(pl.program_id(0) == 0)
        def _():
            dg_ref[...] = jnp.zeros_like(dg_ref)

        df, dg = _rms_bwd(f_ref[...], g_ref[...], dx_ref[...])
        dg_ref[...] += dg
        dfb = df.astype(bf16)
        df_ref[...] = dfb
        da = _dot_nt(dfb, w_ref[...])
        gt, up = u_ref[:, :FFH], u_ref[:, FFH:]
        a_ref[...] = (_silu(gt) * up).astype(bf16)
        du_ref[:, :FFH] = (da * up * _dsilu(gt)).astype(bf16)
        du_ref[:, FFH:] = (da * _silu(gt)).astype(bf16)

    row = lambda w: pl.BlockSpec((ROW_TILE, w), lambda i: (i, 0))
    vec = pl.BlockSpec((1, DM), lambda i: (0, 0))
    return pl.pallas_call(
        body, name="ffn_bwd", grid=(SEQ // ROW_TILE,),
        in_specs=[row(DM), row(DM), vec, row(2 * FFH), pl.BlockSpec((FFH, DM), lambda i: (0, 0))],
        out_specs=[row(2 * FFH), row(FFH), row(DM), vec],
        out_shape=[_sds((SEQ, 2 * FFH), bf16), _sds((SEQ, FFH), bf16), _sds((SEQ, DM), bf16), _sds((1, DM), f32)],
        compiler_params=_params(("arbitrary",)),
    )(dx2, f, g4, u, w2)


def _layer_bwd(dx2, p, sv, ffn_grads_ready=None, mix_grads_ready=None):
    du, act, df, dg4 = _ffn_bwd(dx2, sv["f"], p["g4"], sv["u"], p["w2"])
    dw2 = _matmul_tn(act, df, 1408, DM)
    dx1, dg3 = _matmul_nt_norm([(du, p["w1"], 1408)], sv["x1"], p["g3"], dx2)
    dw1 = _matmul_tn(sv["h2"], du, DM, 1408)
    g2 = p["g2"]
    if ffn_grads_ready is not None:
        g2 = g2 + ffn_grads_ready(dict(w_ffn_in=dw1, w_ffn_out=dw2), dx1)
    *dos, dzg, dproj, dy, dg2 = _merge_bwd(dx1, sv["y"], g2, sv["outs"], sv["zg"], p["wb"], p["wo"])
    dwo = _matmul_tn(sv["merged"], dy, DM, DM)
    dwb = jnp.stack([_matmul_tn(sv["outs"][n], dproj, MIXW, DM, b_col0=n * DM) if n else
                     _matmul_tn(sv["outs"][0], dproj[:, :DM], MIXW, DM) for n in range(NHEAD)])
    zm = sv["zm"]
    zm3 = zm.reshape(HG_N, HG_T, NMIX)
    dza, drb = _attn_bwd(zm, p["rb8"], dos[0])
    dor, dgb, dhng = _hgrn_out_bwd(zm3, p["hng"], sv["obraw3"], dos[1].reshape(HG_N, HG_T, MIXW))
    dzb, dlb = _hgrn_bwd(zm3, p["lb"], dor, sv["hstates"])
    dzc, dws, dgng, dbs = _gmlp_bwd(zm, p["gng"], p["gws"], p["gbs8"], dos[2])
    dzd, dwa, dwx, dcw, dvec = _lru_bwd(zm, p["cw8"], p["cb"], p["wa"], p["ba"], p["wx"], p["bx"], p["lam"], sv["hd"], dos[3])
    dzm = jnp.concatenate([dza, dzb.reshape(SEQ, 3 * MIXW), dgb.reshape(SEQ, MIXW), dzc, dzd], axis=1)
    dwin = jnp.concatenate([_matmul_tn(sv["h"], dzm, DM, 1408), _matmul_tn(sv["h"], dzg, DM, 1024)], axis=1)
    big = dict(w_in=dwin, w_branch=dwb, w_out=dwo, w_ffn_in=dw1, w_ffn_out=dw2)
    g1 = p["g1"]
    if mix_grads_ready is not None:
        g1 = g1 + mix_grads_ready(big)
    dx0, dg1 = _matmul_nt_norm([(dzm, p["wm"], 1408), (dzg, p["wgt"], 1024)], sv["x"], g1, dx1)
    small = dict(
        norm_mix_pre=dg1[0], norm_mix_post=dg2[0], norm_ffn_pre=dg3[0], norm_ffn_post=dg4[0],
        attn_rel_bias=drb[:NHEAD], lb=dlb[0], hgrn_norm_g=dhng[0], gmlp_norm_g=dgng[0], gmlp_ws=dws, gmlp_bs=dbs[:NHEAD],
        lru_conv_w=dcw[:NHEAD], lru_conv_b=dvec[0], lru_wa=_diag_blocks(dwa), lru_ba=dvec[1], lru_wx=_diag_blocks(dwx),
        lru_bx=dvec[2], lru_lambda=dvec[3],
    )
    return dx0, big, small


MIX_BIG = ("w_in", "w_branch", "w_out")
FFN_BIG = ("w_ffn_in", "w_ffn_out")
BIG = MIX_BIG + FFN_BIG
SMALL = ("norm_mix_pre", "norm_mix_post", "norm_ffn_pre", "norm_ffn_post", "attn_rel_bias", "hgrn_lb_logits", "hgrn_norm_g",
         "gmlp_norm_g", "gmlp_ws", "gmlp_bs", "lru_conv_w", "lru_conv_b", "lru_wa", "lru_ba", "lru_wx", "lru_bx", "lru_lambda")


def _local_step(x, tgt, full, small):
    lbs = _lb_fwd(small["hgrn_lb_logits"])
    params, saved = [], []
    for l in range(DEPTH):
        p = _layer_params(l, {k: full[k][l] for k in BIG}, small, lbs)
        x, sv = _layer_fwd(x, p)
        params.append(p)
        saved.append(sv)
    loss, dx = _loss_head(x, tgt)
    bigs, smalls = [None] * DEPTH, [None] * DEPTH
    for l in range(DEPTH - 1, -1, -1):
        dx, bigs[l], smalls[l] = _layer_bwd(dx, params[l], saved[l])
    gbig = {k: jnp.stack([bigs[l][k] for l in range(DEPTH)]) for k in BIG}
    gsmall = {k: jnp.stack([smalls[l][k] for l in range(DEPTH)]) for k in smalls[0]}
    gsmall["hgrn_lb_logits"] = _lb_bwd(small["hgrn_lb_logits"], gsmall.pop("lb"))
    return loss, dx, gbig, gsmall


HBM_ANY = pl.BlockSpec(memory_space=pl.ANY)


def _mesh_pos():
    return lax.axis_index("x"), lax.axis_index("y"), lax.axis_index("c")


def _all_gather(x, name):
    def body(x_ref, out_ref, send_sems, recv_sems, local_sem):
        ax, ay, ac = _mesh_pos()
        me, sibling = (ax, ay, ac), (ax, ay, 1 - ac)
        chips = [(1 - ax, ay), (ax, 1 - ay), (1 - ax, 1 - ay)]

        def slot(px, py, pc):
            return out_ref.at[4 * px + 2 * py + pc]

        def copy(k, block, to, src=None):
            return pltpu.make_async_remote_copy(
                src_ref=slot(*block) if src is None else src, dst_ref=slot(*block),
                send_sem=send_sems.at[k], recv_sem=recv_sems.at[k], device_id=to, device_id_type=MESH_ID)

        mine = pltpu.make_async_copy(x_ref, slot(*me), local_sem)
        mine.start()
        first = [copy(0, me, sibling, src=x_ref)]
        first += [copy(1 + j, me, (*chip, ac), src=x_ref) for j, chip in enumerate(chips)]
        for cp in first:
            cp.start()
        passed = [copy(4 + j, (*chip, ac), sibling) for j, chip in enumerate(chips)]
        for j, chip in enumerate(chips):
            copy(1 + j, (*chip, ac), me).wait_recv()
            passed[j].start()
        copy(0, sibling, me).wait_recv()
        for j, chip in enumerate(chips):
            copy(4 + j, (*chip, 1 - ac), me).wait_recv()
        for cp in first + passed:
            cp.wait_send()
        mine.wait()

    return pl.pallas_call(
        body, name=name, out_shape=_sds((NDEV,) + x.shape, x.dtype),
        in_specs=[HBM_ANY], out_specs=HBM_ANY,
        scratch_shapes=[pltpu.SemaphoreType.DMA((7,)), pltpu.SemaphoreType.DMA((7,)), pltpu.SemaphoreType.DMA],
    )(x)


def _exchange(g, name):
    def body(g_ref, out_ref, send_sems, recv_sems, local_sem):
        ax, ay, ac = _mesh_pos()
        me = 4 * ax + 2 * ay + ac
        mine = pltpu.make_async_copy(g_ref.at[me], out_ref.at[me], local_sem)
        mine.start()
        copies = []
        for k in range(1, NDEV):
            px = 1 - ax if k & 4 else ax
            py = 1 - ay if k & 2 else ay
            pc = 1 - ac if k & 1 else ac
            copies.append(pltpu.make_async_remote_copy(
                src_ref=g_ref.at[4 * px + 2 * py + pc], dst_ref=out_ref.at[me],
                send_sem=send_sems.at[k - 1], recv_sem=recv_sems.at[k - 1], device_id=(px, py, pc), device_id_type=MESH_ID))
        for cp in copies:
            cp.start()
        for cp in copies:
            cp.wait()
        mine.wait()

    return pl.pallas_call(
        body, name=name, out_shape=_sds(g.shape, g.dtype),
        in_specs=[HBM_ANY], out_specs=HBM_ANY,
        scratch_shapes=[pltpu.SemaphoreType.DMA((7,)), pltpu.SemaphoreType.DMA((7,)), pltpu.SemaphoreType.DMA],
    )(g)


def _peer(ax, ay, ac, k):
    return (1 - ax if k & 4 else ax, 1 - ay if k & 2 else ay, 1 - ac if k & 1 else ac)


def _handshake(peers):
    barrier = pltpu.get_barrier_semaphore()
    for peer in peers:
        pl.semaphore_signal(barrier, inc=1, device_id=peer, device_id_type=MESH_ID)
    pl.semaphore_wait(barrier, len(peers))


SEQUENCER = dict(axis_name="seq", num_cores=1)
GATHER_ID = 1
EXCHANGE_ID = 2


def _gather_sc(xs, name):
    n = len(xs)

    def body(*refs):
        srcs, outs = refs[:n], refs[n:2 * n]
        send_sems, recv_sems, local_sems = refs[2 * n:]
        ax, ay, ac = _mesh_pos()
        me, sibling = (ax, ay, ac), (ax, ay, 1 - ac)
        chips = [(1 - ax, ay), (ax, 1 - ay), (1 - ax, 1 - ay)]
        _handshake([sibling] + [(*chip, ac) for chip in chips])

        def slot(i, px, py, pc):
            return outs[i].at[4 * px + 2 * py + pc]

        def copy(i, k, block, to, src=None):
            return pltpu.make_async_remote_copy(
                src_ref=slot(i, *block) if src is None else src, dst_ref=slot(i, *block),
                send_sem=send_sems.at[7 * i + k], recv_sem=recv_sems.at[7 * i + k], device_id=to, device_id_type=MESH_ID)

        mine = [pltpu.make_async_copy(srcs[i], slot(i, *me), local_sems.at[i]) for i in range(n)]
        first = []
        for i in range(n):
            first += [copy(i, 1 + j, me, (*chip, ac), src=srcs[i]) for j, chip in enumerate(chips)]
        for i in range(n):
            first += [copy(i, 0, me, sibling, src=srcs[i])]
        for cp in first + mine:
            cp.start()
        passed = []
        for i in range(n):
            for j, chip in enumerate(chips):
                copy(i, 1 + j, (*chip, ac), me).wait_recv()
                passed.append(copy(i, 4 + j, (*chip, ac), sibling))
                passed[-1].start()
        for i in range(n):
            copy(i, 0, sibling, me).wait_recv()
            for j, chip in enumerate(chips):
                copy(i, 4 + j, (*chip, 1 - ac), me).wait_recv()
        for cp in first + passed:
            cp.wait_send()
        for cp in mine:
            cp.wait()

    return pl.kernel(
        body, name=name, out_type=[_sds((NDEV,) + x.shape, x.dtype) for x in xs],
        mesh=plsc.ScalarSubcoreMesh(**SEQUENCER),
        scratch_types=[pltpu.SemaphoreType.DMA((7 * n,)), pltpu.SemaphoreType.DMA((7 * n,)), pltpu.SemaphoreType.DMA((n,))],
        compiler_params=pltpu.CompilerParams(collective_id=GATHER_ID),
    )(*xs)


def _exchange_sc(gs, name):
    n = len(gs)

    def body(*refs):
        srcs, outs = refs[:n], refs[n:2 * n]
        send_sems, recv_sems, local_sems = refs[2 * n:]
        ax, ay, ac = _mesh_pos()
        me = 4 * ax + 2 * ay + ac
        peers = [_peer(ax, ay, ac, k) for k in range(1, NDEV)]
        _handshake(peers)
        mine = [pltpu.make_async_copy(srcs[i].at[me], outs[i].at[me], local_sems.at[i]) for i in range(n)]
        copies = []
        for i in range(n):
            for k, (px, py, pc) in enumerate(peers):
                copies.append(pltpu.make_async_remote_copy(
                    src_ref=srcs[i].at[4 * px + 2 * py + pc], dst_ref=outs[i].at[me],
                    send_sem=send_sems.at[7 * i + k], recv_sem=recv_sems.at[7 * i + k],
                    device_id=(px, py, pc), device_id_type=MESH_ID))
        for cp in copies + mine:
            cp.start()
        for cp in copies + mine:
            cp.wait()

    return pl.kernel(
        body, name=name, out_type=[_sds(g.shape, g.dtype) for g in gs],
        mesh=plsc.ScalarSubcoreMesh(**SEQUENCER),
        scratch_types=[pltpu.SemaphoreType.DMA((7 * n,)), pltpu.SemaphoreType.DMA((7 * n,)), pltpu.SemaphoreType.DMA((n,))],
        compiler_params=pltpu.CompilerParams(collective_id=EXCHANGE_ID),
    )(*gs)


HBM_SPEC = pl.BlockSpec(memory_space=pltpu.HBM)
SEM_SPEC = pl.BlockSpec(memory_space=pltpu.SEMAPHORE)
DATAFLOW = pltpu.SideEffectType.DATAFLOW_SIDE_EFFECTING


def _exchange_copies(srcs, lands, send_sems, recv_sems, local_sems):
    n = len(srcs)
    ax, ay, ac = _mesh_pos()
    me = 4 * ax + 2 * ay + ac
    copies = [pltpu.make_async_copy(srcs[i].at[me], lands[i].at[me], local_sems.at[i]) for i in range(n)]
    for i in range(n):
        for k in range(1, NDEV):
            px, py, pc = _peer(ax, ay, ac, k)
            copies.append(pltpu.make_async_remote_copy(
                src_ref=srcs[i].at[4 * px + 2 * py + pc], dst_ref=lands[i].at[me],
                send_sem=send_sems.at[7 * i + k - 1], recv_sem=recv_sems.at[7 * i + k - 1],
                device_id=(px, py, pc), device_id_type=MESH_ID))
    return copies


def _exchange_start(gs, name):
    n = len(gs)

    def body(*refs):
        srcs, lands = refs[:n], refs[n:2 * n]
        send_sems, recv_sems, local_sems = refs[2 * n:2 * n + 3]
        token = refs[-1]
        for cp in _exchange_copies(srcs, lands, send_sems, recv_sems, local_sems):
            cp.start()
        token[...] = jnp.zeros_like(token)

    hbm = [pltpu.HBM(g.shape, g.dtype) for g in gs]
    outs = pl.pallas_call(
        body, name=name,
        out_shape=(pltpu.SemaphoreType.DMA((7 * n,)), pltpu.SemaphoreType.DMA((7 * n,)), pltpu.SemaphoreType.DMA((n,)),
                   *hbm, *hbm, _sds((8, 128), f32)),
        in_specs=[HBM_SPEC] * (2 * n),
        out_specs=(SEM_SPEC, SEM_SPEC, SEM_SPEC, *[HBM_SPEC] * (2 * n), pl.BlockSpec(memory_space=pltpu.VMEM)),
        input_output_aliases={i: 3 + i for i in range(2 * n)},
        compiler_params=pltpu.CompilerParams(has_side_effects=DATAFLOW),
    )(*[pltpu.with_memory_space_constraint(g, pltpu.HBM) for g in gs],
      *[pltpu.with_memory_space_constraint(lax.empty(g.shape, g.dtype), pltpu.HBM) for g in gs])
    return outs[:-1], outs[-1]


def _exchange_wait(handles, after, name):
    n = (len(handles) - 3) // 2
    send_sems, recv_sems, local_sems = handles[:3]
    srcs, lands = handles[3:3 + n], handles[3 + n:]

    def body(*refs):
        srcs, lands = refs[:n], refs[n:2 * n]
        send_sems, recv_sems, local_sems = refs[2 * n:2 * n + 3]
        for cp in _exchange_copies(srcs, lands, send_sems, recv_sems, local_sems):
            cp.wait()

    hbm = [pltpu.HBM(g.shape, g.dtype) for g in srcs]
    outs = pl.pallas_call(
        body, name=name, out_shape=(*hbm, *hbm),
        in_specs=[HBM_SPEC] * (2 * n) + [SEM_SPEC] * 3 + [pl.BlockSpec(memory_space=pl.ANY)],
        out_specs=tuple([HBM_SPEC] * (2 * n)),
        input_output_aliases={i: i for i in range(2 * n)},
        compiler_params=pltpu.CompilerParams(has_side_effects=DATAFLOW),
    )(*srcs, *lands, send_sems, recv_sems, local_sems, after)
    return outs[n:]


def _row_tile(rows, cols):
    cap = max(8, (1 << 18) // cols)
    if rows <= cap:
        return rows
    best = None
    for t in range(8, cap + 1, 8):
        if rows % t == 0:
            best = t
    assert best is not None, (rows, cols)
    return best


def _sum_parts(parts, name):
    npart, rows, cols = parts.shape
    tr = _row_tile(rows, cols)

    def body(p_ref, o_ref):
        g = p_ref[0].astype(f32)
        for j in range(1, npart):
            g = g + p_ref[j].astype(f32)
        o_ref[...] = g

    return pl.pallas_call(
        body, name=name, grid=(rows // tr,),
        in_specs=[pl.BlockSpec((npart, tr, cols), lambda i: (0, i, 0))], out_specs=pl.BlockSpec((tr, cols), lambda i: (i, 0)),
        out_shape=_sds((rows, cols), f32), compiler_params=_params(("parallel",)),
    )(parts)


def _adamw(parts, w, m, v, name):
    npart, rows, cols = parts.shape
    tr = _row_tile(rows, cols)
    c1 = 1.0 / (1.0 - ADAM_B1 ** ADAM_STEP)
    c2 = 1.0 / (1.0 - ADAM_B2 ** ADAM_STEP)

    def body(p_ref, w_ref, m_ref, v_ref, g_ref, d_ref, mo_ref, vo_ref):
        g = p_ref[0].astype(f32)
        for j in range(1, npart):
            g = g + p_ref[j].astype(f32)
        mn = ADAM_B1 * m_ref[...] + (1.0 - ADAM_B1) * g
        vn = ADAM_B2 * v_ref[...] + (1.0 - ADAM_B2) * (g * g)
        g_ref[...] = g
        mo_ref[...] = mn
        vo_ref[...] = vn
        d_ref[...] = (-ADAM_LR) * ((mn * c1) / (jnp.sqrt(vn * c2) + ADAM_EPS) + ADAM_WD * w_ref[...])

    blk = pl.BlockSpec((tr, cols), lambda i: (i, 0))
    return pl.pallas_call(
        body, name=name, grid=(rows // tr,),
        in_specs=[pl.BlockSpec((npart, tr, cols), lambda i: (0, i, 0)), blk, blk, blk], out_specs=[blk] * 4,
        out_shape=[_sds((rows, cols), f32)] * 4, compiler_params=_params(("parallel",)),
    )(parts, w, m, v)


def _adamw_layer(parts, w, m, v, acc, l, name):
    npart, rows, cols = parts.shape
    tr = _row_tile(rows, cols)
    c1 = 1.0 / (1.0 - ADAM_B1 ** ADAM_STEP)
    c2 = 1.0 / (1.0 - ADAM_B2 ** ADAM_STEP)

    def body(p_ref, w_ref, m_ref, v_ref, *refs):
        g_ref, d_ref, mo_ref, vo_ref = refs[-4:]
        g = p_ref[0].astype(f32)
        for j in range(1, npart):
            g = g + p_ref[j].astype(f32)
        mn = ADAM_B1 * m_ref[...] + (1.0 - ADAM_B1) * g
        vn = ADAM_B2 * v_ref[...] + (1.0 - ADAM_B2) * (g * g)
        g_ref[...] = g
        mo_ref[...] = mn
        vo_ref[...] = vn
        d_ref[...] = (-ADAM_LR) * ((mn * c1) / (jnp.sqrt(vn * c2) + ADAM_EPS) + ADAM_WD * w_ref[...])

    blk = pl.BlockSpec((None, tr, cols), lambda i: (l, i, 0))
    prev = [] if acc is None else list(acc)
    return pl.pallas_call(
        body, name=name, grid=(rows // tr,),
        in_specs=[pl.BlockSpec((npart, tr, cols), lambda i: (0, i, 0)), blk, blk, blk] + [HBM_ANY] * len(prev),
        out_specs=[blk] * 4, out_shape=[_sds(w.shape, f32)] * 4,
        input_output_aliases={4 + j: j for j in range(len(prev))},
        compiler_params=_params(("parallel",)),
    )(parts, w, m, v, *prev)


def _pack(arrays):
    rows = []
    for a in arrays:
        flat = a.reshape(-1)
        pad = (-flat.shape[0]) % 1024
        rows.append(jnp.concatenate([flat, jnp.zeros((pad,), flat.dtype)]).reshape(-1, 128))
    return jnp.concatenate(rows, axis=0)


def _unpack(flat, shapes):
    out, r = [], 0
    for s in shapes:
        n = math.prod(s)
        nr = (n + 1023) // 1024 * 8
        out.append(flat[r:r + nr].reshape(-1)[:n].reshape(s))
        r += nr
    return out


BIG_SHARD_AXIS = dict(w_in=2, w_branch=3, w_out=1, w_ffn_in=2, w_ffn_out=1)
SHARDED_SMALL = ("attn_rel_bias", "lru_conv_w")


def _to_blocks(full, axis):
    s = full.shape
    cut = full.reshape(s[:axis] + (NDEV, s[axis] // NDEV) + s[axis + 1:])
    return jnp.moveaxis(cut, axis, 0)


def _from_blocks(blocks, axis):
    moved = jnp.moveaxis(blocks, 0, axis)
    s = moved.shape
    return moved.reshape(s[:axis] + (s[axis] * s[axis + 1],) + s[axis + 2:])


def _flat2(a):
    return a.reshape(-1, a.shape[-1])


def _my_slice(a, n):
    ax, ay, ac = _mesh_pos()
    return lax.dynamic_slice_in_dim(a, (4 * ax + 2 * ay + ac) * n, n, axis=a.ndim - 1)


_WEIGHTS = ("norm_mix_pre", "norm_mix_post", "norm_ffn_pre", "norm_ffn_post", "w_in", "attn_rel_bias", "hgrn_lb_logits",
            "hgrn_norm_g", "gmlp_norm_g", "gmlp_ws", "gmlp_bs", "lru_conv_w", "lru_conv_b", "lru_wa", "lru_ba", "lru_wx",
            "lru_bx", "lru_lambda", "w_branch", "w_out", "w_ffn_in", "w_ffn_out")


def _step(x, loss_target, w, m, v):
    gathered = []
    for l in range(DEPTH):
        gathered.append(tuple(_gather_sc([w[k][l].astype(bf16) for k in keys], "gather_%s%d" % (half, l))
                              for half, keys in (("mix", MIX_BIG), ("ffn", FFN_BIG))))
    cut = jnp.concatenate([w[k] for k in SHARDED_SMALL], axis=-1)
    parts = _all_gather(_pack([cut]), "gather_small").reshape(NDEV, -1)[:, :math.prod(cut.shape)].reshape((NDEV,) + cut.shape)
    small = {k: w[k] for k in SMALL if k not in SHARDED_SMALL}
    at = 0
    for k in SHARDED_SMALL:
        n = w[k].shape[-1]
        small[k] = _from_blocks(parts[..., at:at + n], 2)
        at += n
    loss, dx, layers = _step_forward(x, loss_target, gathered, small)
    flat3 = lambda a: a.reshape((DEPTH, -1, a.shape[-1]))
    acc = {k: None for k in BIG}
    smalls = [None] * DEPTH

    def send(grads, keys, name):
        handles, token = _exchange_start([_to_blocks(grads[k], BIG_SHARD_AXIS[k] - 1) for k in keys], "start_" + name)
        return (keys, handles, "wait_" + name), token[0:1, 0:1]

    def update(sent, l, after):
        keys, handles, name = sent
        got = dict(zip(keys, _exchange_wait(handles, after, name)))
        for k, g in got.items():
            w3 = flat3(w[k])
            acc[k] = _adamw_layer(g.reshape((NDEV,) + w3.shape[1:]), w3, flat3(m[k]), flat3(v[k]), acc[k], l,
                                  "adamw_%s_%d" % (k, l))

    waiting = []
    for l in range(DEPTH - 1, -1, -1):
        sent_ffn = []

        def ffn_grads_ready(grads, dx1, l=l, sent_ffn=sent_ffn):
            sent, zero = send(grads, FFN_BIG, "exchange_ffn%d" % l)
            sent_ffn.append(sent)
            while waiting:
                update(*waiting.pop(), dx1)
            return zero

        sent_mix = []

        def mix_grads_ready(grads, l=l, sent_mix=sent_mix):
            sent, zero = send(grads, MIX_BIG, "exchange_mix%d" % l)
            sent_mix.append(sent)
            return zero

        dx, _, smalls[l] = _step_backward(dx, layers[l], ffn_grads_ready, mix_grads_ready)
        update(sent_ffn[0], l, dx)
        waiting.append((sent_mix[0], l))
    grads, deltas, new_m, new_v = {}, {}, {}, {}
    gsmall ={k: jnp.stack([smalls[l][k] for l in range(DEPTH)]) for k in smalls[0]}
    gsmall["hgrn_lb_logits"] = _lb_bwd(small["hgrn_lb_logits"], gsmall.pop("lb"))
    shapes = [gsmall[k].shape for k in SMALL]
    sums = _unpack(_sum_parts(_all_gather(_pack([gsmall[k] for k in SMALL]), "gather_small_grads"), "sum_small_grads"), shapes)
    gs = dict(zip(SMALL, sums))
    for k in SHARDED_SMALL:
        gs[k] = _my_slice(gs[k], w[k].shape[-1])
    packed = [_pack([d[k] for k in SMALL]) for d in (gs, w, m, v)]
    outs = _adamw(packed[0][None], packed[1], packed[2], packed[3], "adamw_small")
    shapes = [w[k].shape for k in SMALL]
    for d, o in zip((grads, deltas, new_m, new_v), outs):
        d.update(zip(SMALL, _unpack(o, shapes)))
    update(*waiting.pop(), outs[1])
    for k in BIG:
        grads[k], deltas[k], new_m[k], new_v[k] = (o.reshape(w[k].shape) for o in acc[k])
    total = lax.psum(loss[0, 0], ("x", "y", "c"))
    return total, dx[None], grads, deltas, new_m, new_v


def _step_forward(x, loss_target, gathered, small):
    lbs = _lb_fwd(small["hgrn_lb_logits"])
    x = x[0]
    layers = []

    def weights(blocks, keys, after):
        if after is not None:
            blocks, _ = lax.optimization_barrier((blocks, after))
        return {k: _from_blocks(g, BIG_SHARD_AXIS[k] - 1) for k, g in zip(keys, blocks)}

    for l in range(DEPTH):
        mix, ffn = gathered[l]
        p = _layer_params(l, weights(mix, MIX_BIG, x if l else None), small, lbs)
        x, sv = _layer_fwd(x, p, lambda x1, ffn=ffn: _ffn_weights(weights(ffn, FFN_BIG, x1)))
        layers.append((p, sv))
    loss, dx = _loss_head(x, loss_target[0])
    return loss, dx, layers


def _step_backward(dx, layer, ffn_grads_ready, mix_grads_ready):
    return _layer_bwd(dx, *layer, ffn_grads_ready, mix_grads_ready)


def kernel(x, norm_mix_pre, norm_mix_post, norm_ffn_pre, norm_ffn_post, w_in, attn_rel_bias, hgrn_lb_logits, hgrn_norm_g, gmlp_norm_g, gmlp_ws, gmlp_bs, lru_conv_w, lru_conv_b, lru_wa, lru_ba, lru_wx, lru_bx, lru_lambda, w_branch, w_out, w_ffn_in, w_ffn_out, loss_target, m_norm_mix_pre, m_norm_mix_post, m_norm_ffn_pre, m_norm_ffn_post, m_w_in, m_attn_rel_bias, m_hgrn_lb_logits, m_hgrn_norm_g, m_gmlp_norm_g, m_gmlp_ws, m_gmlp_bs, m_lru_conv_w, m_lru_conv_b, m_lru_wa, m_lru_ba, m_lru_wx, m_lru_bx, m_lru_lambda, m_w_branch, m_w_out, m_w_ffn_in, m_w_ffn_out, v_norm_mix_pre, v_norm_mix_post, v_norm_ffn_pre, v_norm_ffn_post, v_w_in, v_attn_rel_bias, v_hgrn_lb_logits, v_hgrn_norm_g, v_gmlp_norm_g, v_gmlp_ws, v_gmlp_bs, v_lru_conv_w, v_lru_conv_b, v_lru_wa, v_lru_ba, v_lru_wx, v_lru_bx, v_lru_lambda, v_w_branch, v_w_out, v_w_ffn_in, v_w_ffn_out):
    w = dict(zip(_WEIGHTS, (norm_mix_pre, norm_mix_post, norm_ffn_pre, norm_ffn_post, w_in, attn_rel_bias, hgrn_lb_logits, hgrn_norm_g, gmlp_norm_g, gmlp_ws, gmlp_bs, lru_conv_w, lru_conv_b, lru_wa, lru_ba, lru_wx, lru_bx, lru_lambda, w_branch, w_out, w_ffn_in, w_ffn_out)))
    m = dict(zip(_WEIGHTS, (m_norm_mix_pre, m_norm_mix_post, m_norm_ffn_pre, m_norm_ffn_post, m_w_in, m_attn_rel_bias, m_hgrn_lb_logits, m_hgrn_norm_g, m_gmlp_norm_g, m_gmlp_ws, m_gmlp_bs, m_lru_conv_w, m_lru_conv_b, m_lru_wa, m_lru_ba, m_lru_wx, m_lru_bx, m_lru_lambda, m_w_branch, m_w_out, m_w_ffn_in, m_w_ffn_out)))
    v = dict(zip(_WEIGHTS, (v_norm_mix_pre, v_norm_mix_post, v_norm_ffn_pre, v_norm_ffn_post, v_w_in, v_attn_rel_bias, v_hgrn_lb_logits, v_hgrn_norm_g, v_gmlp_norm_g, v_gmlp_ws, v_gmlp_bs, v_lru_conv_w, v_lru_conv_b, v_lru_wa, v_lru_ba, v_lru_wx, v_lru_bx, v_lru_lambda, v_w_branch, v_w_out, v_w_ffn_in, v_w_ffn_out)))
    loss, grad_x, grads, deltas, new_m, new_v = _step(x, loss_target, w, m, v)
    return (loss, grad_x, *[grads[k] for k in _WEIGHTS], *[deltas[k] for k in _WEIGHTS],
            *[new_m[k] for k in _WEIGHTS], *[new_v[k] for k in _WEIGHTS])
```

```python
import math

import jax
import jax.numpy as jnp
from jax import lax
from jax.experimental import pallas as pl
from jax.experimental.pallas import tpu as pltpu
from jax.experimental.pallas import tpu_sc as plsc

f32 = jnp.float32
bf16 = jnp.bfloat16

SEQ = 2048
DM = 1024
DEPTH = 4
NDEV = 8
MIXW = 256
NHEAD = 4
HDIM = 64
NMIX = 11 * MIXW
NGATE = 4 * DM
FFH = 2816
EPS = 1e-6
NEG_BIG = -1e30
LOG_FLOOR = 1e-30
LRU_C = 8.0
REL_SIZE = 320
ATT_PAIR = 128
ATT_BAND = 640
ATT_PAD = 512
ATT_WV = 768
HG_T = 16
HG_N = SEQ // HG_T
GM_T = 128
LRU_T = 128
ADAM_LR, ADAM_B1, ADAM_B2, ADAM_EPS, ADAM_WD, ADAM_STEP = 0.001, 0.9, 0.999, 1e-8, 0.01, 10
V7X_VMEM_LIMIT = 56 * 1024 * 1024
GELU_C0 = math.sqrt(2.0 / math.pi)
GELU_C1 = 0.044715
MESH_ID = pl.DeviceIdType.MESH


def _params(sem=None):
    if sem is None:
        return pltpu.CompilerParams(vmem_limit_bytes=V7X_VMEM_LIMIT)
    return pltpu.CompilerParams(dimension_semantics=sem, vmem_limit_bytes=V7X_VMEM_LIMIT)


def _sds(shape, dtype):
    return jax.ShapeDtypeStruct(shape, dtype)


def _dot(a, b):
    return jnp.dot(a.astype(bf16), b.astype(bf16), preferred_element_type=f32)


def _dot_nt(a, b):
    return lax.dot_general(a.astype(bf16), b.astype(bf16), (((1,), (1,)), ((), ())), preferred_element_type=f32)


def _dot_tn(a, b):
    return lax.dot_general(a.astype(bf16), b.astype(bf16), (((0,), (0,)), ((), ())), preferred_element_type=f32)


def _split(a):
    hi = a.astype(bf16)
    lo = (a - hi.astype(f32)).astype(bf16)
    return hi, lo


def _dot_hl(a, m):
    hi, lo = _split(a)
    return jnp.dot(hi, m, preferred_element_type=f32) + jnp.dot(lo, m, preferred_element_type=f32)


def _dot_nt_hl(m, a):
    hi, lo = _split(a)
    dn = (((1,), (1,)), ((), ()))
    return lax.dot_general(m, hi, dn, preferred_element_type=f32) + lax.dot_general(m, lo, dn, preferred_element_type=f32)


def _sigmoid(x):
    return jax.nn.sigmoid(x)


def _silu(x):
    return x * _sigmoid(x)


def _dsilu(x):
    s = _sigmoid(x)
    return s * (1.0 + x * (1.0 - s))


def _gelu(x):
    return 0.5 * x * (1.0 + jnp.tanh(GELU_C0 * (x + GELU_C1 * x * x * x)))


def _dgelu(x):
    t = jnp.tanh(GELU_C0 * (x + GELU_C1 * x * x * x))
    return 0.5 * (1.0 + t) + 0.5 * x * (1.0 - t * t) * GELU_C0 * (1.0 + 3.0 * GELU_C1 * x * x)


def _rms(x, g):
    r = lax.rsqrt(jnp.mean(x * x, axis=-1, keepdims=True) + EPS)
    return x * r * g


def _rms_bwd(x, g, dy):
    r = lax.rsqrt(jnp.mean(x * x, axis=-1, keepdims=True) + EPS)
    xh = x * r
    dxh = dy * g
    dx = r * (dxh - xh * jnp.mean(dxh * xh, axis=-1, keepdims=True))
    return dx, jnp.sum(dy * xh, axis=0, keepdims=True)


def _same_head(n, width, dtype):
    r = lax.broadcasted_iota(jnp.int32, (n, n), 0) // width
    c = lax.broadcasted_iota(jnp.int32, (n, n), 1) // width
    return (r == c).astype(dtype)


def _head_masks(rows=1):
    lane = lax.broadcasted_iota(jnp.int32, (rows, MIXW), 1) // HDIM
    return [lane == h for h in range(NHEAD)]


def _norm_matmul(x, g, w, tn):
    n = w.shape[1]
    tm = 1024

    def body(x_ref, g_ref, w_ref, z_ref, h_ref):
        @pl.when(pl.program_id(1) == 0)
        def _():
            h_ref[...] = _rms(x_ref[...], g_ref[...]).astype(bf16)

        z_ref[...] = jnp.dot(h_ref[...], w_ref[...], preferred_element_type=f32)

    return pl.pallas_call(
        body, name="norm_matmul", grid=(SEQ // tm, n // tn),
        in_specs=[pl.BlockSpec((tm, DM), lambda i, j: (i, 0)), pl.BlockSpec((1, DM), lambda i, j: (0, 0)),
                  pl.BlockSpec((DM, tn), lambda i, j: (0, j))],
        out_specs=[pl.BlockSpec((tm, tn), lambda i, j: (i, j)), pl.BlockSpec((tm, DM), lambda i, j: (i, 0))],
        out_shape=[_sds((SEQ, n), f32), _sds((SEQ, DM), bf16)],
        compiler_params=_params(("parallel", "arbitrary")),
    )(x, g, w)


def _matmul(a, w, tn):
    k, n = w.shape
    tm = 1024

    def body(a_ref, w_ref, z_ref):
        z_ref[...] = jnp.dot(a_ref[...], w_ref[...], preferred_element_type=f32)

    return pl.pallas_call(
        body, name="matmul", grid=(SEQ // tm, n // tn),
        in_specs=[pl.BlockSpec((tm, k), lambda i, j: (i, 0)), pl.BlockSpec((k, tn), lambda i, j: (0, j))],
        out_specs=pl.BlockSpec((tm, tn), lambda i, j: (i, j)),
        out_shape=_sds((SEQ, n), f32),
        compiler_params=_params(("parallel", "arbitrary")),
    )(a, w)


def _att_offset_map():
    i = lax.broadcasted_iota(jnp.int32, (REL_SIZE, ATT_WV), 0)
    t = lax.broadcasted_iota(jnp.int32, (REL_SIZE, ATT_WV), 1)
    e = jnp.where(t <= ATT_BAND, t, t - ATT_WV)
    idx = jnp.clip(ATT_PAD - e, -(HDIM - 1), 256) + (HDIM - 1)
    return (idx == i).astype(bf16)


def _att_band_valid():
    qc = lax.broadcasted_iota(jnp.int32, (ATT_PAIR, ATT_BAND), 0) // HDIM
    kc = lax.broadcasted_iota(jnp.int32, (ATT_PAIR, ATT_BAND), 1) // HDIM
    return (kc >= qc) & (kc <= qc + 8)


def _att_bias_tiles(rb_ref, bm_ref):
    wv = _dot_hl(rb_ref[...], _att_offset_map())
    valid = _att_band_valid()
    for h in range(NHEAD):
        rows = jnp.broadcast_to(wv[h:h + 1, :], (ATT_PAIR, ATT_WV))
        tile = pltpu.roll(rows, 0, 1, stride=1, stride_axis=0)[:, :ATT_BAND]
        bm_ref[h] = jnp.where(valid, tile, NEG_BIG)


def _att_pad_kv(k_ref, v_ref, kp_ref, vp_ref):
    kp_ref[pl.ds(0, ATT_PAD), :] = jnp.zeros((ATT_PAD, MIXW), bf16)
    vp_ref[pl.ds(0, ATT_PAD), :] = jnp.zeros((ATT_PAD, MIXW), bf16)
    kp_ref[pl.ds(ATT_PAD, SEQ), :] = k_ref[...].astype(bf16)
    vp_ref[pl.ds(ATT_PAD, SEQ), :] = v_ref[...].astype(bf16)


def _att_probs(qm, kb, bm, key_ok):
    s = _dot_nt(qm, kb) + bm
    s = jnp.where(key_ok, s, NEG_BIG)
    m = jnp.max(s, axis=-1, keepdims=True)
    e = jnp.exp(s - m)
    return e / jnp.sum(e, axis=-1, keepdims=True)


def _attn_fwd(zm, rb8):
    def body(q_ref, k_ref, v_ref, rb_ref, o_ref, kp_ref, vp_ref, bm_ref):
        _att_pad_kv(k_ref, v_ref, kp_ref, vp_ref)
        _att_bias_tiles(rb_ref, bm_ref)
        hm = _head_masks()

        def pair(p, carry):
            r0 = pl.multiple_of(p * ATT_PAIR, ATT_PAIR)
            q = q_ref[pl.ds(r0, ATT_PAIR), :] * (HDIM ** -0.5)
            kb = kp_ref[pl.ds(r0, ATT_BAND), :]
            vb = vp_ref[pl.ds(r0, ATT_BAND), :]
            key_ok = (lax.broadcasted_iota(jnp.int32, (1, ATT_BAND), 1) + (r0 - ATT_PAD)) >= 0
            o = jnp.zeros((ATT_PAIR, MIXW), f32)
            for h in range(NHEAD):
                qm = jnp.where(hm[h], q, 0.0)
                p_h = _att_probs(qm, kb, bm_ref[h], key_ok)
                o = o + jnp.where(hm[h], _dot(p_h, vb), 0.0)
            o_ref[pl.ds(r0, ATT_PAIR), :] = o.astype(bf16)
            return carry

        lax.fori_loop(0, SEQ // ATT_PAIR, pair, 0)

    col = lambda j: pl.BlockSpec((SEQ, MIXW), lambda i: (0, j))
    return pl.pallas_call(
        body, name="attn_fwd", grid=(1,),
        in_specs=[col(0), col(1), col(2), pl.BlockSpec((8, REL_SIZE), lambda i: (0, 0))],
        out_specs=pl.BlockSpec((SEQ, MIXW), lambda i: (0, 0)),
        out_shape=_sds((SEQ, MIXW), bf16),
        scratch_shapes=[pltpu.VMEM((SEQ + ATT_PAD, MIXW), bf16), pltpu.VMEM((SEQ + ATT_PAD, MIXW), bf16),
                        pltpu.VMEM((NHEAD, ATT_PAIR, ATT_BAND), f32)],
        compiler_params=_params(("arbitrary",)),
    )(zm, zm, zm, rb8)


def _hg_gates(q, fz, lb):
    sq = _sigmoid(q)
    sg = _sigmoid(fz)
    f = lb + (1.0 - lb) * sg
    return q * sq, (1.0 - lb) * (1.0 - sg), jnp.log(jnp.maximum(f, LOG_FLOOR)), sq, sg, f


def _hg_prepare(q_ref, f_ref, lb, qf_s, kf_s, b_s, qd_s, kd_s, dec_s):
    b = None
    for t in range(HG_T):
        qf, kf, lf, _, _, _ = _hg_gates(q_ref[:, t, :], f_ref[:, t, :], lb)
        b = lf if b is None else b + lf
        qf_s[:, t, :] = qf
        kf_s[:, t, :] = kf
        b_s[:, t, :] = b
    b_last = b
    dec_s[...] = jnp.broadcast_to(jnp.exp(b_last)[:, None, :], (HG_N, 8, MIXW))
    for t in range(HG_T):
        bt = b_s[:, t, :]
        qd_s[:, t, :] = qf_s[:, t, :] * jnp.exp(bt)
        kd_s[:, t, :] = kf_s[:, t, :] * jnp.exp(b_last - bt)


def _hg_scores(t, qf_s, kf_s, b_s, w_s, hm):
    qt = qf_s[:, t, :]
    bt = b_s[:, t, :]
    for s in range(t + 1):
        w = qt * kf_s[:, s, :]
        if s < t:
            w = w * jnp.exp(bt - b_s[:, s, :])
        w_s[pl.ds(s * HG_N, HG_N), :] = w.astype(bf16)
    return jnp.dot(w_s[pl.ds(0, (t + 1) * HG_N), :], hm, preferred_element_type=f32)


def _hgrn_fwd(zm3, lb, ng):
    def body(q_ref, f_ref, i_ref, g_ref, lb_ref, ng_ref, o_ref, oraw_ref, states_ref,
             qf_s, kf_s, b_s, qd_s, kd_s, dec_s, w_s, st_s):
        lb = lb_ref[...]
        hm = _same_head(MIXW, HDIM, bf16)
        hmf = _same_head(MIXW, HDIM, f32)
        _hg_prepare(q_ref, f_ref, lb, qf_s, kf_s, b_s, qd_s, kd_s, dec_s)
        for t in range(HG_T):
            p = _hg_scores(t, qf_s, kf_s, b_s, w_s, hm)
            acc = jnp.zeros((HG_N, MIXW), f32)
            for s in range(t + 1):
                acc = acc + p[s * HG_N:(s + 1) * HG_N] * i_ref[:, s, :]
            oraw_ref[:, t, :] = acc
        st_s[...] = jnp.zeros((MIXW, MIXW), f32)

        def step(n, carry):
            st = st_s[...]
            stb = st.astype(bf16)
            states_ref[n] = stb
            oraw_ref[n] = oraw_ref[n] + _dot_nt(qd_s[n], stb)
            st_s[...] = st * dec_s[n][0:1] + _dot_tn(i_ref[n], kd_s[n]) * hmf
            return carry

        lax.fori_loop(0, HG_N, step, 0, unroll=2)
        ngv = ng_ref[...]
        for t in range(HG_T):
            o = oraw_ref[:, t, :]
            ms = _dot_hl(o * o, hm) * (1.0 / HDIM)
            o_ref[:, t, :] = (o * lax.rsqrt(ms + EPS) * ngv * _silu(g_ref[:, t, :])).astype(bf16)

    one = pl.Buffered(1)
    col = lambda j: pl.BlockSpec((HG_N, HG_T, MIXW), lambda i: (0, 0, j), pipeline_mode=one)
    vec = pl.BlockSpec((1, MIXW), lambda i: (0, 0))
    blk = pl.BlockSpec((HG_N, HG_T, MIXW), lambda i: (0, 0, 0))
    s3 = pltpu.VMEM((HG_N, HG_T, MIXW), f32)
    return pl.pallas_call(
        body, name="hgrn_fwd", grid=(1,),
        in_specs=[col(3), col(4), col(5), col(6), vec, vec],
        out_specs=[blk, blk, pl.BlockSpec((HG_N, MIXW, MIXW), lambda i: (0, 0, 0), pipeline_mode=one)],
        out_shape=[_sds((HG_N, HG_T, MIXW), bf16), _sds((HG_N, HG_T, MIXW), f32), _sds((HG_N, MIXW, MIXW), bf16)],
        scratch_shapes=[s3, s3, s3, s3, s3, pltpu.VMEM((HG_N, 8, MIXW), f32),
                        pltpu.VMEM((HG_T * HG_N, MIXW), bf16), pltpu.VMEM((MIXW, MIXW), f32)],
        compiler_params=_params(("arbitrary",)),
    )(zm3, zm3, zm3, zm3, lb, ng)


def _gm_weights(ws_ref):
    tril = lax.broadcasted_iota(jnp.int32, (GM_T, GM_T), 0) >= lax.broadcasted_iota(jnp.int32, (GM_T, GM_T), 1)
    return tril, [jnp.where(tril, ws_ref[g], 0.0).astype(bf16) for g in range(NHEAD)]


def _gm_expand():
    r = lax.broadcasted_iota(jnp.int32, (8, MIXW), 0)
    c = lax.broadcasted_iota(jnp.int32, (8, MIXW), 1) // HDIM
    return (r == c).astype(bf16)


def _gm_mixed(vn, wts, bias, hm):
    vb = vn.astype(bf16)
    mixed = bias
    for g in range(NHEAD):
        mixed = mixed + jnp.where(hm[g], jnp.dot(wts[g], vb, preferred_element_type=f32), 0.0)
    return mixed


def _gm_bias(bs_ref):
    hi, lo = _split(bs_ref[...])
    et = _gm_expand()
    dn = (((0,), (0,)), ((), ()))
    return lax.dot_general(hi, et, dn, preferred_element_type=f32) + lax.dot_general(lo, et, dn, preferred_element_type=f32)


def _gmlp_fwd(zm, ng, ws, bs8):
    def body(u_ref, v_ref, ng_ref, ws_ref, bs_ref, o_ref):
        hm = _head_masks()
        _, wts = _gm_weights(ws_ref)
        bias = _gm_bias(bs_ref)
        ngv = ng_ref[...]

        def blk(n, carry):
            rows = pl.ds(pl.multiple_of(n * GM_T, GM_T), GM_T)
            vn = _rms(_gelu(v_ref[rows, :]), ngv)
            o_ref[rows, :] = (_gelu(u_ref[rows, :]) * _gm_mixed(vn, wts, bias, hm)).astype(bf16)
            return carry

        lax.fori_loop(0, SEQ // GM_T, blk, 0)

    col = lambda j: pl.BlockSpec((SEQ, MIXW), lambda i: (0, j))
    return pl.pallas_call(
        body, name="gmlp_fwd", grid=(1,),
        in_specs=[col(7), col(8), pl.BlockSpec((1, MIXW), lambda i: (0, 0)),
                  pl.BlockSpec((NHEAD, GM_T, GM_T), lambda i: (0, 0, 0)), pl.BlockSpec((8, GM_T), lambda i: (0, 0))],
        out_specs=pl.BlockSpec((SEQ, MIXW), lambda i: (0, 0)),
        out_shape=_sds((SEQ, MIXW), bf16),
        compiler_params=_params(("arbitrary",)),
    )(zm, zm, ng, ws, bs8)


def _lru_conv(x_ref, cw_ref, cb_ref, xp_s, xc_s):
    xp_s[pl.ds(0, 8), :] = jnp.zeros((8, MIXW), f32)
    xp_s[pl.ds(8, SEQ), :] = x_ref[...]
    cw = cw_ref[...]
    xc = cb_ref[...] + x_ref[...] * cw[3:4]
    for k in range(1, 4):
        xc = xc + xp_s[pl.ds(8 - k, SEQ), :] * cw[3 - k:4 - k]
    xc_s[...] = xc


def _lru_gates(xc, wa, ba, wx, bx, sp, first_row):
    r = _sigmoid(_dot(xc, wa) + ba)
    ig = _sigmoid(_dot(xc, wx) + bx)
    la = (-LRU_C) * r * sp
    a = jnp.exp(la)
    th = jnp.tanh(la)
    m2 = -2.0 * th / (1.0 - th)
    mult = jnp.where(first_row, 1.0, jnp.sqrt(jnp.maximum(m2, 0.0)))
    return a, mult, r, ig, m2


def _lru_scan(a, b, rev):
    row = lax.broadcasted_iota(jnp.int32, (LRU_T, 1), 0)
    k = 1
    while k < LRU_T:
        ok = (row < LRU_T - k) if rev else (row >= k)
        sh = (LRU_T - k) if rev else k
        a_sh = jnp.where(ok, pltpu.roll(a, sh, 0), 1.0)
        b_sh = jnp.where(ok, pltpu.roll(b, sh, 0), 0.0)
        b = b + a * b_sh
        a = a * a_sh
        k *= 2
    return a, b


def _lru_fwd(zm, cw8, cb, wa, ba, wx, bx, lam):
    def body(x_ref, g_ref, cw_ref, cb_ref, wa_ref, ba_ref, wx_ref, bx_ref, lam_ref, o_ref, h_ref, xp_s, xc_s):
        _lru_conv(x_ref, cw_ref, cb_ref, xp_s, xc_s)
        sp = jax.nn.softplus(-lam_ref[...])
        wa_v, wx_v, ba_v, bx_v = wa_ref[...], wx_ref[...], ba_ref[...], bx_ref[...]

        def chunk(c, h_prev):
            rows = pl.ds(pl.multiple_of(c * LRU_T, LRU_T), LRU_T)
            first = (lax.broadcasted_iota(jnp.int32, (LRU_T, 1), 0) + c * LRU_T) == 0
            xc = xc_s[rows, :]
            a, mult, _, ig, _ = _lru_gates(xc, wa_v, ba_v, wx_v, bx_v, sp, first)
            acum, hloc = _lru_scan(a, mult * (ig * xc), False)
            h = hloc + acum * h_prev
            h_ref[rows, :] = h
            o_ref[rows, :] = (h * _gelu(g_ref[rows, :])).astype(bf16)
            return h[LRU_T - 1:LRU_T, :]

        lax.fori_loop(0, SEQ // LRU_T, chunk, jnp.zeros((1, MIXW), f32))

    col = lambda j: pl.BlockSpec((SEQ, MIXW), lambda i: (0, j))
    vec = pl.BlockSpec((1, MIXW), lambda i: (0, 0))
    mat = pl.BlockSpec((MIXW, MIXW), lambda i: (0, 0))
    out = pl.BlockSpec((SEQ, MIXW), lambda i: (0, 0))
    return pl.pallas_call(
        body, name="lru_fwd", grid=(1,),
        in_specs=[col(9), col(10), pl.BlockSpec((8, MIXW), lambda i: (0, 0)), vec, mat, vec, mat, vec, vec],
        out_specs=[out, out],
        out_shape=[_sds((SEQ, MIXW), bf16), _sds((SEQ, MIXW), f32)],
        scratch_shapes=[pltpu.VMEM((SEQ + 8, MIXW), f32), pltpu.VMEM((SEQ, MIXW), f32)],
        compiler_params=_params(("arbitrary",)),
    )(zm, zm, cw8, cb, wa, ba, wx, bx, lam)


def _block_diag(w):
    out = jnp.zeros((MIXW, MIXW), w.dtype)
    for h in range(NHEAD):
        out = lax.dynamic_update_slice(out, w[h], (h * HDIM, h * HDIM))
    return out


def _diag_blocks(w):
    return jnp.stack([w[h * HDIM:(h + 1) * HDIM, h * HDIM:(h + 1) * HDIM] for h in range(NHEAD)])


ROW_TILE = 256


def _merge_fwd(outs, zg, wb, wo, x, g2):
    def body(oa_ref, ob_ref, oc_ref, od_ref, zg_ref, wb_ref, wo_ref, x_ref, g_ref, xo_ref, mg_ref, y_ref):
        merged = jnp.zeros((ROW_TILE, DM), f32)
        for n, o_ref in enumerate((oa_ref, ob_ref, oc_ref, od_ref)):
            proj = jnp.dot(o_ref[...], wb_ref[n], preferred_element_type=f32)
            merged = merged + _sigmoid(zg_ref[:, n * DM:(n + 1) * DM]) * proj
        mb = merged.astype(bf16)
        y = jnp.dot(mb, wo_ref[...], preferred_element_type=f32)
        mg_ref[...] = mb
        y_ref[...] = y
        xo_ref[...] = x_ref[...] + _rms(y, g_ref[...])

    row = lambda w: pl.BlockSpec((ROW_TILE, w), lambda i: (i, 0))
    return pl.pallas_call(
        body, name="merge_fwd", grid=(SEQ // ROW_TILE,),
        in_specs=[row(MIXW)] * 4 + [row(NGATE), pl.BlockSpec((NHEAD, MIXW, DM), lambda i: (0, 0, 0)),
                                    pl.BlockSpec((DM, DM), lambda i: (0, 0)), row(DM), pl.BlockSpec((1, DM), lambda i: (0, 0))],
        out_specs=[row(DM), row(DM), row(DM)],
        out_shape=[_sds((SEQ, DM), f32), _sds((SEQ, DM), bf16), _sds((SEQ, DM), f32)],
        compiler_params=_params(("parallel",)),
    )(*outs, zg, wb, wo, x, g2)


def _ffn_out(u, w2, x, g4):
    def body(u_ref, w_ref, x_ref, g_ref, xo_ref, f_ref):
        a = _silu(u_ref[:, :FFH]) * u_ref[:, FFH:]
        f = jnp.dot(a.astype(bf16), w_ref[...], preferred_element_type=f32)
        f_ref[...] = f
        xo_ref[...] = x_ref[...] + _rms(f, g_ref[...])

    row = lambda w: pl.BlockSpec((ROW_TILE, w), lambda i: (i, 0))
    return pl.pallas_call(
        body, name="ffn_out", grid=(SEQ // ROW_TILE,),
        in_specs=[row(2 * FFH), pl.BlockSpec((FFH, DM), lambda i: (0, 0)), row(DM), pl.BlockSpec((1, DM), lambda i: (0, 0))],
        out_specs=[row(DM), row(DM)],
        out_shape=[_sds((SEQ, DM), f32), _sds((SEQ, DM), f32)],
        compiler_params=_params(("parallel",)),
    )(u, w2, x, g4)


def _loss_head(x, tgt):
    tm = 512

    def body(x_ref, t_ref, l_ref, dx_ref):
        @pl.when(pl.program_id(0) == 0)
        def _():
            l_ref[...] = jnp.zeros((1, 1), f32)

        d = x_ref[...] - t_ref[...]
        dx_ref[...] = d * (1.0 / DM)
        l_ref[...] += (0.5 / DM) * jnp.sum(d * d).reshape(1, 1)

    row = pl.BlockSpec((tm, DM), lambda i: (i, 0))
    return pl.pallas_call(
        body, name="loss_head", grid=(SEQ // tm,),
        in_specs=[row, row], out_specs=[pl.BlockSpec((1, 1), lambda i: (0, 0)), row],
        out_shape=[_sds((1, 1), f32), _sds((SEQ, DM), f32)],
        compiler_params=_params(("arbitrary",)),
    )(x, tgt)


def _lb_fwd(logits):
    def body(lg_ref, o_ref):
        lg = lg_ref[...]
        e = jnp.exp(lg - jnp.max(lg, axis=0, keepdims=True))
        p = e / jnp.sum(e, axis=0, keepdims=True)
        acc = jnp.zeros((1, MIXW), f32)
        o_ref[0:1, :] = acc
        for l in range(1, DEPTH):
            acc = acc + p[l:l + 1]
            o_ref[l:l + 1, :] = acc

    return pl.pallas_call(body, name="lb_fwd", out_shape=_sds((DEPTH, MIXW), f32))(logits)


def _lb_bwd(logits, dlbs):
    def body(lg_ref, d_ref, o_ref):
        lg = lg_ref[...]
        e = jnp.exp(lg - jnp.max(lg, axis=0, keepdims=True))
        p = e / jnp.sum(e, axis=0, keepdims=True)
        d = d_ref[...]
        dp = [jnp.zeros((1, MIXW), f32)] * DEPTH
        acc = jnp.zeros((1, MIXW), f32)
        for j in range(DEPTH - 1, 0, -1):
            acc = acc + d[j:j + 1]
            dp[j] = acc
        inner = sum(p[j:j + 1] * dp[j] for j in range(DEPTH))
        for j in range(DEPTH):
            o_ref[j:j + 1, :] = p[j:j + 1] * (dp[j] - inner)

    return pl.pallas_call(body, name="lb_bwd", out_shape=_sds((DEPTH, MIXW), f32))(logits, dlbs)


def _pad_rows(a, rows=8):
    return jnp.concatenate([a, jnp.zeros((rows - a.shape[0], a.shape[1]), a.dtype)], axis=0)


def _layer_params(l, full, small, lbs):
    row = lambda name: small[name][l][None]
    return dict(
        _mix_weights(full), **(_ffn_weights(full) if "w_ffn_in" in full else {}),
        g1=row("norm_mix_pre"), g2=row("norm_mix_post"), g3=row("norm_ffn_pre"), g4=row("norm_ffn_post"),
        rb8=_pad_rows(small["attn_rel_bias"][l]), lb=lbs[l][None], hng=row("hgrn_norm_g"),
        gng=row("gmlp_norm_g"), gws=small["gmlp_ws"][l], gbs8=_pad_rows(small["gmlp_bs"][l]),
        cw8=_pad_rows(small["lru_conv_w"][l]), cb=row("lru_conv_b"),
        wa=_block_diag(small["lru_wa"][l]).astype(bf16), ba=row("lru_ba"),
        wx=_block_diag(small["lru_wx"][l]).astype(bf16), bx=row("lru_bx"), lam=row("lru_lambda"),
    )


def _mix_weights(full):
    return dict(wm=full["w_in"][:, :NMIX], wgt=full["w_in"][:, NMIX:], wb=full["w_branch"], wo=full["w_out"])


def _ffn_weights(full):
    return dict(w1=full["w_ffn_in"], w2=full["w_ffn_out"])


def _layer_fwd(x, p, late_ffn_weights=None):
    zm, h = _norm_matmul(x, p["g1"], p["wm"], 1408)
    zg = _matmul(h, p["wgt"], 1024)
    oa = _attn_fwd(zm, p["rb8"])
    ob3, obraw3, hstates = _hgrn_fwd(zm.reshape(HG_N, HG_T, NMIX), p["lb"], p["hng"])
    oc = _gmlp_fwd(zm, p["gng"], p["gws"], p["gbs8"])
    od, hd = _lru_fwd(zm, p["cw8"], p["cb"], p["wa"], p["ba"], p["wx"], p["bx"], p["lam"])
    outs = (oa, ob3.reshape(SEQ, MIXW), oc, od)
    x1, merged, y = _merge_fwd(outs, zg, p["wb"], p["wo"], x, p["g2"])
    if late_ffn_weights is not None:
        p.update(late_ffn_weights(x1))
    u, h2 = _norm_matmul(x1, p["g3"], p["w1"], 1408)
    x2, f = _ffn_out(u, p["w2"], x1, p["g4"])
    saved = dict(x=x, h=h, zm=zm, zg=zg, outs=outs, obraw3=obraw3, hstates=hstates, hd=hd, x1=x1, merged=merged, y=y, u=u, h2=h2, f=f)
    return x2, saved


def _att_bias_grad(db_ref, o_ref):
    r = lax.broadcasted_iota(jnp.int32, (ATT_PAIR, ATT_PAIR), 0)
    c = lax.broadcasted_iota(jnp.int32, (ATT_PAIR, ATT_PAIR), 1)
    flip = (r + c == ATT_PAIR - 1).astype(bf16)
    rows = []
    for h in range(NHEAD):
        d = jnp.concatenate([db_ref[h], jnp.zeros((ATT_PAIR, ATT_WV - ATT_BAND), f32)], axis=1)
        hi, lo = _split(d)
        rev = jnp.dot(flip, hi, preferred_element_type=f32) + jnp.dot(flip, lo, preferred_element_type=f32)
        lined = pltpu.roll(rev, ATT_WV - (ATT_PAIR - 1), 1, stride=1, stride_axis=0)
        rows.append(jnp.sum(lined, axis=0, keepdims=True))
    dwv = jnp.concatenate(rows + [jnp.zeros((8 - NHEAD, ATT_WV), f32)], axis=0)
    hi, lo = _split(dwv)
    m = _att_offset_map()
    dn = (((1,), (1,)), ((), ()))
    o_ref[...] = lax.dot_general(hi, m, dn, preferred_element_type=f32) + lax.dot_general(lo, m, dn, preferred_element_type=f32)


def _attn_bwd(zm, rb8, do):
    def body(q_ref, k_ref, v_ref, rb_ref, do_ref, dz_ref, drb_ref, kp_ref, vp_ref, bm_ref, dk_s, dv_s, db_s):
        _att_pad_kv(k_ref, v_ref, kp_ref, vp_ref)
        _att_bias_tiles(rb_ref, bm_ref)
        dk_s[...] = jnp.zeros_like(dk_s)
        dv_s[...] = jnp.zeros_like(dv_s)
        db_s[...] = jnp.zeros_like(db_s)
        hm = _head_masks()
        scale = HDIM ** -0.5

        def pair(p, carry):
            r0 = pl.multiple_of(p * ATT_PAIR, ATT_PAIR)
            q = q_ref[pl.ds(r0, ATT_PAIR), :] * scale
            dout = do_ref[pl.ds(r0, ATT_PAIR), :]
            kb = kp_ref[pl.ds(r0, ATT_BAND), :]
            vb = vp_ref[pl.ds(r0, ATT_BAND), :]
            key_ok = (lax.broadcasted_iota(jnp.int32, (1, ATT_BAND), 1) + (r0 - ATT_PAD)) >= 0
            dq = jnp.zeros((ATT_PAIR, MIXW), f32)
            dkb = jnp.zeros((ATT_BAND, MIXW), f32)
            dvb = jnp.zeros((ATT_BAND, MIXW), f32)
            for h in range(NHEAD):
                qm = jnp.where(hm[h], q, 0.0).astype(bf16)
                dom = jnp.where(hm[h], dout, 0.0).astype(bf16)
                p_h = _att_probs(qm, kb, bm_ref[h], key_ok)
                dp = _dot_nt(dom, vb)
                ds = p_h * (dp - jnp.sum(dp * p_h, axis=-1, keepdims=True))
                dsb = ds.astype(bf16)
                dq = dq + jnp.where(hm[h], _dot(dsb, kb), 0.0)
                dkb = dkb + _dot_tn(dsb, qm)
                dvb = dvb + _dot_tn(p_h, dom)
                db_s[h] = db_s[h] + ds
            dz_ref[pl.ds(r0, ATT_PAIR), 0:MIXW] = (dq * scale).astype(bf16)
            dk_s[pl.ds(r0, ATT_BAND), :] = dk_s[pl.ds(r0, ATT_BAND), :] + dkb
            dv_s[pl.ds(r0, ATT_BAND), :] = dv_s[pl.ds(r0, ATT_BAND), :] + dvb
            return carry

        lax.fori_loop(0, SEQ // ATT_PAIR, pair, 0)
        dz_ref[:, MIXW:2 * MIXW] = dk_s[pl.ds(ATT_PAD, SEQ), :].astype(bf16)
        dz_ref[:, 2 * MIXW:3 * MIXW] = dv_s[pl.ds(ATT_PAD, SEQ), :].astype(bf16)
        _att_bias_grad(db_s, drb_ref)

    col = lambda j: pl.BlockSpec((SEQ, MIXW), lambda i: (0, j))
    return pl.pallas_call(
        body, name="attn_bwd", grid=(1,),
        in_specs=[col(0), col(1), col(2), pl.BlockSpec((8, REL_SIZE), lambda i: (0, 0)), pl.BlockSpec((SEQ, MIXW), lambda i: (0, 0))],
        out_specs=[pl.BlockSpec((SEQ, 3 * MIXW), lambda i: (0, 0)), pl.BlockSpec((8, REL_SIZE), lambda i: (0, 0))],
        out_shape=[_sds((SEQ, 3 * MIXW), bf16), _sds((8, REL_SIZE), f32)],
        scratch_shapes=[pltpu.VMEM((SEQ + ATT_PAD, MIXW), bf16), pltpu.VMEM((SEQ + ATT_PAD, MIXW), bf16),
                        pltpu.VMEM((NHEAD, ATT_PAIR, ATT_BAND), f32),
                        pltpu.VMEM((SEQ + ATT_PAD, MIXW), f32), pltpu.VMEM((SEQ + ATT_PAD, MIXW), f32),
                        pltpu.VMEM((NHEAD, ATT_PAIR, ATT_BAND), f32)],
        compiler_params=_params(("arbitrary",)),
    )(zm, zm, zm, rb8, do)


def _hgrn_out_bwd(zm3, ng, oraw3, do3):
    def body(g_ref, ng_ref, o_ref, do_ref, dor_ref, dg_ref, dng_ref):
        hm = _same_head(MIXW, HDIM, bf16)
        ngv = ng_ref[...]
        dng = jnp.zeros((1, MIXW), f32)
        for t in range(HG_T):
            o, g, d = o_ref[:, t, :], g_ref[:, t, :], do_ref[:, t, :]
            rs = lax.rsqrt(_dot_hl(o * o, hm) * (1.0 / HDIM) + EPS)
            y1 = o * rs
            dy2 = d * _silu(g)
            dg_ref[:, t, :] = (d * y1 * ngv * _dsilu(g)).astype(bf16)
            dng = dng + jnp.sum(dy2 * y1, axis=0, keepdims=True)
            dy1 = dy2 * ngv
            dor_ref[:, t, :] = rs * (dy1 - y1 * (_dot_hl(dy1 * y1, hm) * (1.0 / HDIM)))
        dng_ref[...] = jnp.broadcast_to(dng, (8, MIXW))

    blk = pl.BlockSpec((HG_N, HG_T, MIXW), lambda i: (0, 0, 0))
    return pl.pallas_call(
        body, name="hgrn_out_bwd", grid=(1,),
        in_specs=[pl.BlockSpec((HG_N, HG_T, MIXW), lambda i: (0, 0, 6)), pl.BlockSpec((1, MIXW), lambda i: (0, 0)), blk, blk],
        out_specs=[blk, blk, pl.BlockSpec((8, MIXW), lambda i: (0, 0))],
        out_shape=[_sds((HG_N, HG_T, MIXW), f32), _sds((HG_N, HG_T, MIXW), bf16), _sds((8, MIXW), f32)],
        compiler_params=_params(("arbitrary",)),
    )(zm3, ng, oraw3, do3)


def _hgrn_bwd(zm3, lb, dor3, states):
    def body(q_ref, f_ref, i_ref, lb_ref, dor_ref, st_s, dz_ref, dlb_ref,
             qf_s, kf_s, b_s, dq_s, dk_s, db_s, dv_s, w_s, x_s, cur_s):
        lb = lb_ref[...]
        hm = _same_head(MIXW, HDIM, bf16)
        hmf = _same_head(MIXW, HDIM, f32)
        b = None
        for t in range(HG_T):
            qf, kf, lf, _, _, _ = _hg_gates(q_ref[:, t, :], f_ref[:, t, :], lb)
            b = lf if b is None else b + lf
            qf_s[:, t, :] = qf
            kf_s[:, t, :] = kf
            b_s[:, t, :] = b

        def block_terms(n):
            bn = b_s[n]
            bl = bn[HG_T - 1:HG_T]
            eb = jnp.exp(bn)
            ek = jnp.exp(bl - bn)
            return qf_s[n] * eb, kf_s[n] * ek, jnp.exp(bl), eb, ek

        cur_s[...] = jnp.zeros((MIXW, MIXW), f32)
        last = lax.broadcasted_iota(jnp.int32, (HG_T, 1), 0) == HG_T - 1

        def bwd_step(j, carry):
            n = HG_N - 1 - j
            qd, kd, dec, eb, ek = block_terms(n)
            v, do_n = i_ref[n], dor_ref[n]
            dst = cur_s[...]
            st = st_s[n]
            dqd = _dot(do_n, st)
            dkd = _dot(v, dst)
            ddec = jnp.sum(dst * st.astype(f32), axis=0, keepdims=True)
            cur_s[...] = dst * dec + _dot_tn(do_n, qd) * hmf
            dq_s[n] = dqd * eb
            dk_s[n] = dkd * ek
            dv_s[n] = _dot_nt(kd, dst)
            dbl = jnp.sum(dkd * kd, axis=0, keepdims=True) + ddec * dec
            db_s[n] = dqd * qd - dkd * kd + jnp.where(last, dbl, 0.0)
            return carry

        lax.fori_loop(0, HG_N, bwd_step, 0, unroll=2)
        for t in range(HG_T):
            qt, bt, dot_t = qf_s[:, t, :], b_s[:, t, :], dor_ref[:, t, :]
            for s in range(t + 1):
                w = qt * kf_s[:, s, :]
                if s < t:
                    w = w * jnp.exp(bt - b_s[:, s, :])
                w_s[pl.ds(s * HG_N, HG_N), :] = w.astype(bf16)
                x_s[pl.ds(s * HG_N, HG_N), :] = (dot_t * i_ref[:, s, :]).astype(bf16)
            p = jnp.dot(w_s[pl.ds(0, (t + 1) * HG_N), :], hm, preferred_element_type=f32)
            dp = jnp.dot(x_s[pl.ds(0, (t + 1) * HG_N), :], hm, preferred_element_type=f32)
            dq_t = jnp.zeros((HG_N, MIXW), f32)
            db_t = jnp.zeros((HG_N, MIXW), f32)
            for s in range(t + 1):
                ps = p[s * HG_N:(s + 1) * HG_N]
                dps = dp[s * HG_N:(s + 1) * HG_N]
                ks = kf_s[:, s, :]
                dv_s[:, s, :] = dv_s[:, s, :] + ps * dot_t
                if s < t:
                    dec_ts = jnp.exp(bt - b_s[:, s, :])
                    g1 = dps * ks * dec_ts
                    dk_s[:, s, :] = dk_s[:, s, :] + dps * qt * dec_ts
                    gw = g1 * qt
                    db_t = db_t + gw
                    db_s[:, s, :] = db_s[:, s, :] - gw
                else:
                    g1 = dps * ks
                    dk_s[:, s, :] = dk_s[:, s, :] + dps * qt
                dq_t = dq_t + g1
            dq_s[:, t, :] = dq_s[:, t, :] + dq_t
            db_s[:, t, :] = db_s[:, t, :] + db_t
        run = jnp.zeros((HG_N, MIXW), f32)
        dlb = jnp.zeros((1, MIXW), f32)
        oml = 1.0 - lb
        for t in range(HG_T - 1, -1, -1):
            run = run + db_s[:, t, :]
            q = q_ref[:, t, :]
            _, _, _, sq, sg, f = _hg_gates(q, f_ref[:, t, :], lb)
            dkf = dk_s[:, t, :]
            df = jnp.where(f > LOG_FLOOR, run / f, 0.0)
            dsg = (df - dkf) * oml
            dlb = dlb + jnp.sum((df - dkf) * (1.0 - sg), axis=0, keepdims=True)
            dz_ref[:, t, 0:MIXW] = (dq_s[:, t, :] * sq * (1.0 + q * (1.0 - sq))).astype(bf16)
            dz_ref[:, t, MIXW:2 * MIXW] = (dsg * sg * (1.0 - sg)).astype(bf16)
            dz_ref[:, t, 2 * MIXW:3 * MIXW] = dv_s[:, t, :].astype(bf16)
        dlb_ref[...] = jnp.broadcast_to(dlb, (8, MIXW))

    one = pl.Buffered(1)
    col = lambda j: pl.BlockSpec((HG_N, HG_T, MIXW), lambda i: (0, 0, j), pipeline_mode=one)
    s3 = pltpu.VMEM((HG_N, HG_T, MIXW), f32)
    return pl.pallas_call(
        body, name="hgrn_bwd", grid=(1,),
        in_specs=[col(3), col(4), col(5), pl.BlockSpec((1, MIXW), lambda i: (0, 0)),
                  pl.BlockSpec((HG_N, HG_T, MIXW), lambda i: (0, 0, 0), pipeline_mode=one),
                  pl.BlockSpec((HG_N, MIXW, MIXW), lambda i: (0, 0, 0), pipeline_mode=one)],
        out_specs=[pl.BlockSpec((HG_N, HG_T, 3 * MIXW), lambda i: (0, 0, 0)), pl.BlockSpec((8, MIXW), lambda i: (0, 0))],
        out_shape=[_sds((HG_N, HG_T, 3 * MIXW), bf16), _sds((8, MIXW), f32)],
        scratch_shapes=[s3, s3, s3, s3, s3, s3, s3,
                        pltpu.VMEM((HG_T * HG_N, MIXW), bf16), pltpu.VMEM((HG_T * HG_N, MIXW), bf16),
                        pltpu.VMEM((MIXW, MIXW), f32)],
        compiler_params=_params(("arbitrary",)),
    )(zm3, zm3, zm3, lb, dor3, states)


def _gmlp_bwd(zm, ng, ws, bs8, do):
    def body(u_ref, v_ref, ng_ref, ws_ref, bs_ref, do_ref, dz_ref, dws_ref, dng_ref, dbs_ref, dm_s):
        hm = _head_masks()
        tril, wts = _gm_weights(ws_ref)
        bias = _gm_bias(bs_ref)
        ngv = ng_ref[...]
        dws_ref[...] = jnp.zeros_like(dws_ref)
        dm_s[...] = jnp.zeros_like(dm_s)

        def blk(n, dng):
            rows = pl.ds(pl.multiple_of(n * GM_T, GM_T), GM_T)
            cu, cv, d = u_ref[rows, :], v_ref[rows, :], do_ref[rows, :]
            v = _gelu(cv)
            r = lax.rsqrt(jnp.mean(v * v, axis=-1, keepdims=True) + EPS)
            vh = v * r
            vn = vh * ngv
            u = _gelu(cu)
            dm = d * u
            dmb, vnb = dm.astype(bf16), vn.astype(bf16)
            dvn = jnp.zeros((GM_T, MIXW), f32)
            for g in range(NHEAD):
                dws_ref[g] = dws_ref[g] + _dot_nt(jnp.where(hm[g], dm, 0.0), vnb)
                dvn = dvn + jnp.where(hm[g], _dot_tn(wts[g], dmb), 0.0)
            dm_s[...] = dm_s[...] + dm
            dvh = dvn * ngv
            dv = r * (dvh - vh * jnp.mean(dvh * vh, axis=-1, keepdims=True))
            dz_ref[rows, 0:MIXW] = (d * _gm_mixed(vn, wts, bias, hm) * _dgelu(cu)).astype(bf16)
            dz_ref[rows, MIXW:2 * MIXW] = (dv * _dgelu(cv)).astype(bf16)
            return dng + jnp.sum(dvn * vh, axis=0, keepdims=True)

        dng = lax.fori_loop(0, SEQ // GM_T, blk, jnp.zeros((1, MIXW), f32))
        dng_ref[...] = jnp.broadcast_to(dng, (8, MIXW))
        for g in range(NHEAD):
            dws_ref[g] = jnp.where(tril, dws_ref[g], 0.0)
        dbs_ref[...] = _dot_nt_hl(_gm_expand(), dm_s[...])

    col = lambda j: pl.BlockSpec((SEQ, MIXW), lambda i: (0, j))
    return pl.pallas_call(
        body, name="gmlp_bwd", grid=(1,),
        in_specs=[col(7), col(8), pl.BlockSpec((1, MIXW), lambda i: (0, 0)),
                  pl.BlockSpec((NHEAD, GM_T, GM_T), lambda i: (0, 0, 0)), pl.BlockSpec((8, GM_T), lambda i: (0, 0)),
                  pl.BlockSpec((SEQ, MIXW), lambda i: (0, 0))],
        out_specs=[pl.BlockSpec((SEQ, 2 * MIXW), lambda i: (0, 0)), pl.BlockSpec((NHEAD, GM_T, GM_T), lambda i: (0, 0, 0)),
                   pl.BlockSpec((8, MIXW), lambda i: (0, 0)), pl.BlockSpec((8, GM_T), lambda i: (0, 0))],
        out_shape=[_sds((SEQ, 2 * MIXW), bf16), _sds((NHEAD, GM_T, GM_T), f32), _sds((8, MIXW), f32), _sds((8, GM_T), f32)],
        scratch_shapes=[pltpu.VMEM((GM_T, MIXW), f32)],
        compiler_params=_params(("arbitrary",)),
    )(zm, zm, ng, ws, bs8, do)


def _lru_bwd(zm, cw8, cb, wa, ba, wx, bx, lam, hd, do):
    nchunk = SEQ // LRU_T

    def body(x_ref, g_ref, cw_ref, cb_ref, wa_ref, ba_ref, wx_ref, bx_ref, lam_ref, h_ref, do_ref,
             dz_ref, dwa_ref, dwx_ref, dcw_ref, dvec_ref, xp_s, xc_s, dxc_s):
        _lru_conv(x_ref, cw_ref, cb_ref, xp_s, xc_s)
        lam_v = lam_ref[...]
        sp = jax.nn.softplus(-lam_v)
        sgl = _sigmoid(-lam_v)
        wa_v, wx_v, ba_v, bx_v = wa_ref[...], wx_ref[...], ba_ref[...], bx_ref[...]
        dwa_ref[...] = jnp.zeros_like(dwa_ref)
        dwx_ref[...] = jnp.zeros_like(dwx_ref)
        dxc_s[pl.ds(SEQ, 8), :] = jnp.zeros((8, MIXW), f32)
        row = lax.broadcasted_iota(jnp.int32, (LRU_T, 1), 0)
        zero = jnp.zeros((1, MIXW), f32)

        def chunk(j, carry):
            dh_next, a_next, dba, dbx, dlam = carry
            c = nchunk - 1 - j
            rows = pl.ds(pl.multiple_of(c * LRU_T, LRU_T), LRU_T)
            prev = pl.ds(pl.multiple_of(jnp.maximum(c - 1, 0) * LRU_T, LRU_T), LRU_T)
            first = (row + c * LRU_T) == 0
            xc, gate, d, h = xc_s[rows, :], g_ref[rows, :], do_ref[rows, :], h_ref[rows, :]
            a, mult, r, ig, m2 = _lru_gates(xc, wa_v, ba_v, wx_v, bx_v, sp, first)
            h_last = jnp.where(c > 0, h_ref[prev, :][LRU_T - 1:LRU_T, :], 0.0)
            h_m1 = jnp.where(row == 0, h_last, pltpu.roll(h, 1, 0))
            a_up = jnp.where(row == LRU_T - 1, a_next, pltpu.roll(a, LRU_T - 1, 0))
            acum, dh_loc = _lru_scan(a_up, d * _gelu(gate), True)
            dh = dh_loc + acum * dh_next
            dmult = jnp.where(first, 0.0, dh * (ig * xc))
            msq = jnp.sqrt(jnp.maximum(m2, 0.0))
            dla = dh * h_m1 * a + jnp.where(m2 > 0.0, -dmult * (1.0 - m2) / msq, 0.0)
            dpr = dla * (-LRU_C) * sp * r * (1.0 - r)
            dpi = dh * mult * xc * ig * (1.0 - ig)
            dxc_s[rows, :] = dh * mult * ig + _dot_nt(dpr, wa_v) + _dot_nt(dpi, wx_v)
            dwa_ref[...] = dwa_ref[...] + _dot_tn(xc, dpr)
            dwx_ref[...] = dwx_ref[...] + _dot_tn(xc, dpi)
            dz_ref[rows, MIXW:2 * MIXW] = (d * h * _dgelu(gate)).astype(bf16)
            return (dh[0:1], a[0:1], dba + jnp.sum(dpr, axis=0, keepdims=True), dbx + jnp.sum(dpi, axis=0, keepdims=True),
                    dlam + jnp.sum(dla * r, axis=0, keepdims=True) * (LRU_C * sgl))

        _, _, dba, dbx, dlam = lax.fori_loop(0, nchunk, chunk, (zero, zero, zero, zero, zero))
        cw = cw_ref[...]
        dxc = dxc_s[pl.ds(0, SEQ), :]
        dx = dxc * cw[3:4]
        dcw = [None] * 4
        dcw[3] = jnp.sum(dxc * x_ref[...], axis=0, keepdims=True)
        for k in range(1, 4):
            dx = dx + dxc_s[pl.ds(k, SEQ), :] * cw[3 - k:4 - k]
            dcw[3 - k] = jnp.sum(dxc * xp_s[pl.ds(8 - k, SEQ), :], axis=0, keepdims=True)
        dz_ref[:, 0:MIXW] = dx.astype(bf16)
        dcw_ref[...] = jnp.concatenate(dcw + [jnp.zeros((4, MIXW), f32)], axis=0)
        dvec_ref[...] = jnp.concatenate([jnp.sum(dxc, axis=0, keepdims=True), dba, dbx, dlam, jnp.zeros((4, MIXW), f32)], axis=0)

    col = lambda j: pl.BlockSpec((SEQ, MIXW), lambda i: (0, j))
    vec = pl.BlockSpec((1, MIXW), lambda i: (0, 0))
    vec8 = pl.BlockSpec((8, MIXW), lambda i: (0, 0))
    mat = pl.BlockSpec((MIXW, MIXW), lambda i: (0, 0))
    full = pl.BlockSpec((SEQ, MIXW), lambda i: (0, 0))
    return pl.pallas_call(
        body, name="lru_bwd", grid=(1,),
        in_specs=[col(9), col(10), vec8, vec, mat, vec, mat, vec, vec, full, full],
        out_specs=[pl.BlockSpec((SEQ, 2 * MIXW), lambda i: (0, 0)), mat, mat, vec8, vec8],
        out_shape=[_sds((SEQ, 2 * MIXW), bf16), _sds((MIXW, MIXW), f32), _sds((MIXW, MIXW), f32),
                   _sds((8, MIXW), f32), _sds((8, MIXW), f32)],
        scratch_shapes=[pltpu.VMEM((SEQ + 8, MIXW), f32), pltpu.VMEM((SEQ, MIXW), f32), pltpu.VMEM((SEQ + 8, MIXW), f32)],
        compiler_params=_params(("arbitrary",)),
    )(zm, zm, cw8, cb, wa, ba, wx, bx, lam, hd, do)


def _matmul_tn(a, b, tm, tn, b_col0=0):
    m = a.shape[1]
    n = tn if b_col0 else b.shape[1]
    off = b_col0 // tn

    def body(a_ref, b_ref, o_ref):
        o_ref[...] = _dot_tn(a_ref[...], b_ref[...]).astype(bf16)

    return pl.pallas_call(
        body, name="matmul_tn", grid=(m // tm, n // tn),
        in_specs=[pl.BlockSpec((SEQ, tm), lambda i, j: (0, i)), pl.BlockSpec((SEQ, tn), lambda i, j: (0, j + off))],
        out_specs=pl.BlockSpec((tm, tn), lambda i, j: (i, j)),
        out_shape=_sds((m, n), bf16),
        compiler_params=_params(("parallel", "arbitrary")),
    )(a, b)


def _matmul_nt_norm(pairs, x, g, dres):
    tm = 1024
    steps = [a.shape[1] // t for a, _, t in pairs]
    starts = [sum(steps[:i]) for i in range(len(pairs))]
    total = sum(steps)
    npair = len(pairs)

    def body(*refs):
        a_refs, w_refs = refs[0:2 * npair:2], refs[1:2 * npair:2]
        x_ref, g_ref, dres_ref, dx_ref, dg_ref, acc_s = refs[2 * npair:]
        i, k = pl.program_id(0), pl.program_id(1)

        @pl.when(k == 0)
        def _():
            acc_s[...] = jnp.zeros_like(acc_s)

        @pl.when((i == 0) & (k == 0))
        def _():
            dg_ref[...] = jnp.zeros_like(dg_ref)

        for q in range(npair):
            @pl.when((k >= starts[q]) & (k < starts[q] + steps[q]))
            def _(q=q):
                acc_s[...] += _dot_nt(a_refs[q][...], w_refs[q][...])

        @pl.when(k == total - 1)
        def _():
            dx, dg = _rms_bwd(x_ref[...], g_ref[...], acc_s[...])
            dx_ref[...] = dres_ref[...] + dx
            dg_ref[...] += dg

    in_specs, args = [], []
    for q, (a, w, t) in enumerate(pairs):
        kmap = lambda k, q=q: jnp.clip(k - starts[q], 0, steps[q] - 1)
        in_specs += [pl.BlockSpec((tm, t), lambda i, k, kmap=kmap: (i, kmap(k))),
                     pl.BlockSpec((DM, t), lambda i, k, kmap=kmap: (0, kmap(k)))]
        args += [a, w]
    row = pl.BlockSpec((tm, DM), lambda i, k: (i, 0))
    vec = pl.BlockSpec((1, DM), lambda i, k: (0, 0))
    return pl.pallas_call(
        body, name="matmul_nt_norm", grid=(SEQ // tm, total),
        in_specs=in_specs + [row, vec, row], out_specs=[row, vec],
        out_shape=[_sds((SEQ, DM), f32), _sds((1, DM), f32)],
        scratch_shapes=[pltpu.VMEM((tm, DM), f32)],
        compiler_params=_params(("arbitrary", "arbitrary")),
    )(*args, x, g, dres)


def _merge_bwd(dx1, y, g2, outs, zg, wb, wo):
    def body(dx_ref, y_ref, g_ref, oa_ref, ob_ref, oc_ref, od_ref, zg_ref, wb_ref, wo_ref,
             da_ref, db_ref, dc_ref, dd_ref, dzg_ref, dpj_ref, dy_ref, dg_ref):
        @pl.when(pl.program_id(0) == 0)
        def _():
            dg_ref[...] = jnp.zeros_like(dg_ref)

        dy, dg = _rms_bwd(y_ref[...], g_ref[...], dx_ref[...])
        dg_ref[...] += dg
        dyb = dy.astype(bf16)
        dy_ref[...] = dyb
        dmerged = _dot_nt(dyb, wo_ref[...])
        for n, (o_ref, do_ref) in enumerate(((oa_ref, da_ref), (ob_ref, db_ref), (oc_ref, dc_ref), (od_ref, dd_ref))):
            cols = slice(n * DM, (n + 1) * DM)
            gate = _sigmoid(zg_ref[:, cols])
            proj = jnp.dot(o_ref[...], wb_ref[n], preferred_element_type=f32)
            dproj = (dmerged * gate).astype(bf16)
            dpj_ref[:, cols] = dproj
            dzg_ref[:, cols] = (dmerged * proj * gate * (1.0 - gate)).astype(bf16)
            do_ref[...] = _dot_nt(dproj, wb_ref[n])

    row = lambda w: pl.BlockSpec((ROW_TILE, w), lambda i: (i, 0))
    vec = pl.BlockSpec((1, DM), lambda i: (0, 0))
    return pl.pallas_call(
        body, name="merge_bwd", grid=(SEQ // ROW_TILE,),
        in_specs=[row(DM), row(DM), vec] + [row(MIXW)] * 4 + [row(NGATE), pl.BlockSpec((NHEAD, MIXW, DM), lambda i: (0, 0, 0)),
                                                              pl.BlockSpec((DM, DM), lambda i: (0, 0))],
        out_specs=[row(MIXW)] * 4 + [row(NGATE), row(NGATE), row(DM), vec],
        out_shape=[_sds((SEQ, MIXW), f32)] * 4 + [_sds((SEQ, NGATE), bf16), _sds((SEQ, NGATE), bf16), _sds((SEQ, DM), bf16),
                                                  _sds((1, DM), f32)],
        compiler_params=_params(("arbitrary",)),
    )(dx1, y, g2, *outs, zg, wb, wo)


def _ffn_bwd(dx2, f, g4, u, w2):
    def body(dx_ref, f_ref, g_ref, u_ref, w_ref, du_ref, a_ref, df_ref, dg_ref):
        @pl.when(pl.program_id(0) == 0)
        def _():
            dg_ref[...] = jnp.zeros_like(dg_ref)

        df, dg = _rms_bwd(f_ref[...], g_ref[...], dx_ref[...])
        dg_ref[...] += dg
        dfb = df.astype(bf16)
        df_ref[...] = dfb
        da = _dot_nt(dfb, w_ref[...])
        gt, up = u_ref[:, :FFH], u_ref[:, FFH:]
        a_ref[...] = (_silu(gt) * up).astype(bf16)
        du_ref[:, :FFH] = (da * up * _dsilu(gt)).astype(bf16)
        du_ref[:, FFH:] = (da * _silu(gt)).astype(bf16)

    row = lambda w: pl.BlockSpec((ROW_TILE, w), lambda i: (i, 0))
    vec = pl.BlockSpec((1, DM), lambda i: (0, 0))
    return pl.pallas_call(
        body, name="ffn_bwd", grid=(SEQ // ROW_TILE,),
        in_specs=[row(DM), row(DM), vec, row(2 * FFH), pl.BlockSpec((FFH, DM), lambda i: (0, 0))],
        out_specs=[row(2 * FFH), row(FFH), row(DM), vec],
        out_shape=[_sds((SEQ, 2 * FFH), bf16), _sds((SEQ, FFH), bf16), _sds((SEQ, DM), bf16), _sds((1, DM), f32)],
        compiler_params=_params(("arbitrary",)),
    )(dx2, f, g4, u, w2)


def _layer_bwd(dx2, p, sv, ffn_grads_ready=None, mix_grads_ready=None):
    du, act, df, dg4 = _ffn_bwd(dx2, sv["f"], p["g4"], sv["u"], p["w2"])
    dw2 = _matmul_tn(act, df, 1408, DM)
    dx1, dg3 = _matmul_nt_norm([(du, p["w1"], 1408)], sv["x1"], p["g3"], dx2)
    dw1 = _matmul_tn(sv["h2"], du, DM, 1408)
    g2 = p["g2"]
    if ffn_grads_ready is not None:
        g2 = g2 + ffn_grads_ready(dict(w_ffn_in=dw1, w_ffn_out=dw2), dx1)
    *dos, dzg, dproj, dy, dg2 = _merge_bwd(dx1, sv["y"], g2, sv["outs"], sv["zg"], p["wb"], p["wo"])
    dwo = _matmul_tn(sv["merged"], dy, DM, DM)
    dwb = jnp.stack([_matmul_tn(sv["outs"][n], dproj, MIXW, DM, b_col0=n * DM) if n else
                     _matmul_tn(sv["outs"][0], dproj[:, :DM], MIXW, DM) for n in range(NHEAD)])
    zm = sv["zm"]
    zm3 = zm.reshape(HG_N, HG_T, NMIX)
    dza, drb = _attn_bwd(zm, p["rb8"], dos[0])
    dor, dgb, dhng = _hgrn_out_bwd(zm3, p["hng"], sv["obraw3"], dos[1].reshape(HG_N, HG_T, MIXW))
    dzb, dlb = _hgrn_bwd(zm3, p["lb"], dor, sv["hstates"])
    dzc, dws, dgng, dbs = _gmlp_bwd(zm, p["gng"], p["gws"], p["gbs8"], dos[2])
    dzd, dwa, dwx, dcw, dvec = _lru_bwd(zm, p["cw8"], p["cb"], p["wa"], p["ba"], p["wx"], p["bx"], p["lam"], sv["hd"], dos[3])
    dzm = jnp.concatenate([dza, dzb.reshape(SEQ, 3 * MIXW), dgb.reshape(SEQ, MIXW), dzc, dzd], axis=1)
    dwin = jnp.concatenate([_matmul_tn(sv["h"], dzm, DM, 1408), _matmul_tn(sv["h"], dzg, DM, 1024)], axis=1)
    big = dict(w_in=dwin, w_branch=dwb, w_out=dwo, w_ffn_in=dw1, w_ffn_out=dw2)
    g1 = p["g1"]
    if mix_grads_ready is not None:
        g1 = g1 + mix_grads_ready(big)
    dx0, dg1 = _matmul_nt_norm([(dzm, p["wm"], 1408), (dzg, p["wgt"], 1024)], sv["x"], g1, dx1)
    small = dict(
        norm_mix_pre=dg1[0], norm_mix_post=dg2[0], norm_ffn_pre=dg3[0], norm_ffn_post=dg4[0],
        attn_rel_bias=drb[:NHEAD], lb=dlb[0], hgrn_norm_g=dhng[0], gmlp_norm_g=dgng[0], gmlp_ws=dws, gmlp_bs=dbs[:NHEAD],
        lru_conv_w=dcw[:NHEAD], lru_conv_b=dvec[0], lru_wa=_diag_blocks(dwa), lru_ba=dvec[1], lru_wx=_diag_blocks(dwx),
        lru_bx=dvec[2], lru_lambda=dvec[3],
    )
    return dx0, big, small


MIX_BIG = ("w_in", "w_branch", "w_out")
FFN_BIG = ("w_ffn_in", "w_ffn_out")
BIG = MIX_BIG + FFN_BIG
SMALL = ("norm_mix_pre", "norm_mix_post", "norm_ffn_pre", "norm_ffn_post", "attn_rel_bias", "hgrn_lb_logits", "hgrn_norm_g",
         "gmlp_norm_g", "gmlp_ws", "gmlp_bs", "lru_conv_w", "lru_conv_b", "lru_wa", "lru_ba", "lru_wx", "lru_bx", "lru_lambda")


def _local_step(x, tgt, full, small):
    lbs = _lb_fwd(small["hgrn_lb_logits"])
    params, saved = [], []
    for l in range(DEPTH):
        p = _layer_params(l, {k: full[k][l] for k in BIG}, small, lbs)
        x, sv = _layer_fwd(x, p)
        params.append(p)
        saved.append(sv)
    loss, dx = _loss_head(x, tgt)
    bigs, smalls = [None] * DEPTH, [None] * DEPTH
    for l in range(DEPTH - 1, -1, -1):
        dx, bigs[l], smalls[l] = _layer_bwd(dx, params[l], saved[l])
    gbig = {k: jnp.stack([bigs[l][k] for l in range(DEPTH)]) for k in BIG}
    gsmall = {k: jnp.stack([smalls[l][k] for l in range(DEPTH)]) for k in smalls[0]}
    gsmall["hgrn_lb_logits"] = _lb_bwd(small["hgrn_lb_logits"], gsmall.pop("lb"))
    return loss, dx, gbig, gsmall


HBM_ANY = pl.BlockSpec(memory_space=pl.ANY)


def _mesh_pos():
    return lax.axis_index("x"), lax.axis_index("y"), lax.axis_index("c")


def _all_gather(x, name):
    def body(x_ref, out_ref, send_sems, recv_sems, local_sem):
        ax, ay, ac = _mesh_pos()
        me, sibling = (ax, ay, ac), (ax, ay, 1 - ac)
        chips = [(1 - ax, ay), (ax, 1 - ay), (1 - ax, 1 - ay)]

        def slot(px, py, pc):
            return out_ref.at[4 * px + 2 * py + pc]

        def copy(k, block, to, src=None):
            return pltpu.make_async_remote_copy(
                src_ref=slot(*block) if src is None else src, dst_ref=slot(*block),
                send_sem=send_sems.at[k], recv_sem=recv_sems.at[k], device_id=to, device_id_type=MESH_ID)

        mine = pltpu.make_async_copy(x_ref, slot(*me), local_sem)
        mine.start()
        first = [copy(0, me, sibling, src=x_ref)]
        first += [copy(1 + j, me, (*chip, ac), src=x_ref) for j, chip in enumerate(chips)]
        for cp in first:
            cp.start()
        passed = [copy(4 + j, (*chip, ac), sibling) for j, chip in enumerate(chips)]
        for j, chip in enumerate(chips):
            copy(1 + j, (*chip, ac), me).wait_recv()
            passed[j].start()
        copy(0, sibling, me).wait_recv()
        for j, chip in enumerate(chips):
            copy(4 + j, (*chip, 1 - ac), me).wait_recv()
        for cp in first + passed:
            cp.wait_send()
        mine.wait()

    return pl.pallas_call(
        body, name=name, out_shape=_sds((NDEV,) + x.shape, x.dtype),
        in_specs=[HBM_ANY], out_specs=HBM_ANY,
        scratch_shapes=[pltpu.SemaphoreType.DMA((7,)), pltpu.SemaphoreType.DMA((7,)), pltpu.SemaphoreType.DMA],
    )(x)


def _exchange(g, name):
    def body(g_ref, out_ref, send_sems, recv_sems, local_sem):
        ax, ay, ac = _mesh_pos()
        me = 4 * ax + 2 * ay + ac
        mine = pltpu.make_async_copy(g_ref.at[me], out_ref.at[me], local_sem)
        mine.start()
        copies = []
        for k in range(1, NDEV):
            px = 1 - ax if k & 4 else ax
            py = 1 - ay if k & 2 else ay
            pc = 1 - ac if k & 1 else ac
            copies.append(pltpu.make_async_remote_copy(
                src_ref=g_ref.at[4 * px + 2 * py + pc], dst_ref=out_ref.at[me],
                send_sem=send_sems.at[k - 1], recv_sem=recv_sems.at[k - 1], device_id=(px, py, pc), device_id_type=MESH_ID))
        for cp in copies:
            cp.start()
        for cp in copies:
            cp.wait()
        mine.wait()

    return pl.pallas_call(
        body, name=name, out_shape=_sds(g.shape, g.dtype),
        in_specs=[HBM_ANY], out_specs=HBM_ANY,
        scratch_shapes=[pltpu.SemaphoreType.DMA((7,)), pltpu.SemaphoreType.DMA((7,)), pltpu.SemaphoreType.DMA],
    )(g)


def _peer(ax, ay, ac, k):
    return (1 - ax if k & 4 else ax, 1 - ay if k & 2 else ay, 1 - ac if k & 1 else ac)


def _handshake(peers):
    barrier = pltpu.get_barrier_semaphore()
    for peer in peers:
        pl.semaphore_signal(barrier, inc=1, device_id=peer, device_id_type=MESH_ID)
    pl.semaphore_wait(barrier, len(peers))


SEQUENCER = dict(axis_name="seq", num_cores=1)
GATHER_ID = 1
EXCHANGE_ID = 2


def _gather_sc(xs, name):
    n = len(xs)

    def body(*refs):
        srcs, outs = refs[:n], refs[n:2 * n]
        send_sems, recv_sems, local_sems = refs[2 * n:]
        ax, ay, ac = _mesh_pos()
        me, sibling = (ax, ay, ac), (ax, ay, 1 - ac)
        chips = [(1 - ax, ay), (ax, 1 - ay), (1 - ax, 1 - ay)]
        _handshake([sibling] + [(*chip, ac) for chip in chips])

        def slot(i, px, py, pc):
            return outs[i].at[4 * px + 2 * py + pc]

        def copy(i, k, block, to, src=None):
            return pltpu.make_async_remote_copy(
                src_ref=slot(i, *block) if src is None else src, dst_ref=slot(i, *block),
                send_sem=send_sems.at[7 * i + k], recv_sem=recv_sems.at[7 * i + k], device_id=to, device_id_type=MESH_ID)

        mine = [pltpu.make_async_copy(srcs[i], slot(i, *me), local_sems.at[i]) for i in range(n)]
        first = []
        for i in range(n):
            first += [copy(i, 1 + j, me, (*chip, ac), src=srcs[i]) for j, chip in enumerate(chips)]
        for i in range(n):
            first += [copy(i, 0, me, sibling, src=srcs[i])]
        for cp in first + mine:
            cp.start()
        passed = []
        for i in range(n):
            for j, chip in enumerate(chips):
                copy(i, 1 + j, (*chip, ac), me).wait_recv()
                passed.append(copy(i, 4 + j, (*chip, ac), sibling))
                passed[-1].start()
        for i in range(n):
            copy(i, 0, sibling, me).wait_recv()
            for j, chip in enumerate(chips):
                copy(i, 4 + j, (*chip, 1 - ac), me).wait_recv()
        for cp in first + passed:
            cp.wait_send()
        for cp in mine:
            cp.wait()

    return pl.kernel(
        body, name=name, out_type=[_sds((NDEV,) + x.shape, x.dtype) for x in xs],
        mesh=plsc.ScalarSubcoreMesh(**SEQUENCER),
        scratch_types=[pltpu.SemaphoreType.DMA((7 * n,)), pltpu.SemaphoreType.DMA((7 * n,)), pltpu.SemaphoreType.DMA((n,))],
        compiler_params=pltpu.CompilerParams(collective_id=GATHER_ID),
    )(*xs)


def _exchange_sc(gs, name):
    n = len(gs)

    def body(*refs):
        srcs, outs = refs[:n], refs[n:2 * n]
        send_sems, recv_sems, local_sems = refs[2 * n:]
        ax, ay, ac = _mesh_pos()
        me = 4 * ax + 2 * ay + ac
        peers = [_peer(ax, ay, ac, k) for k in range(1, NDEV)]
        _handshake(peers)
        mine = [pltpu.make_async_copy(srcs[i].at[me], outs[i].at[me], local_sems.at[i]) for i in range(n)]
        copies = []
        for i in range(n):
            for k, (px, py, pc) in enumerate(peers):
                copies.append(pltpu.make_async_remote_copy(
                    src_ref=srcs[i].at[4 * px + 2 * py + pc], dst_ref=outs[i].at[me],
                    send_sem=send_sems.at[7 * i + k], recv_sem=recv_sems.at[7 * i + k],
                    device_id=(px, py, pc), device_id_type=MESH_ID))
        for cp in copies + mine:
            cp.start()
        for cp in copies + mine:
            cp.wait()

    return pl.kernel(
        body, name=name, out_type=[_sds(g.shape, g.dtype) for g in gs],
        mesh=plsc.ScalarSubcoreMesh(**SEQUENCER),
        scratch_types=[pltpu.SemaphoreType.DMA((7 * n,)), pltpu.SemaphoreType.DMA((7 * n,)), pltpu.SemaphoreType.DMA((n,))],
        compiler_params=pltpu.CompilerParams(collective_id=EXCHANGE_ID),
    )(*gs)


HBM_SPEC = pl.BlockSpec(memory_space=pltpu.HBM)
SEM_SPEC = pl.BlockSpec(memory_space=pltpu.SEMAPHORE)
DATAFLOW = pltpu.SideEffectType.DATAFLOW_SIDE_EFFECTING


def _exchange_copies(srcs, lands, send_sems, recv_sems, local_sems):
    n = len(srcs)
    ax, ay, ac = _mesh_pos()
    me = 4 * ax + 2 * ay + ac
    copies = [pltpu.make_async_copy(srcs[i].at[me], lands[i].at[me], local_sems.at[i]) for i in range(n)]
    for i in range(n):
        for k in range(1, NDEV):
            px, py, pc = _peer(ax, ay, ac, k)
            copies.append(pltpu.make_async_remote_copy(
                src_ref=srcs[i].at[4 * px + 2 * py + pc], dst_ref=lands[i].at[me],
                send_sem=send_sems.at[7 * i + k - 1], recv_sem=recv_sems.at[7 * i + k - 1],
                device_id=(px, py, pc), device_id_type=MESH_ID))
    return copies


def _exchange_start(gs, name):
    n = len(gs)

    def body(*refs):
        srcs, lands = refs[:n], refs[n:2 * n]
        send_sems, recv_sems, local_sems = refs[2 * n:2 * n + 3]
        token = refs[-1]
        for cp in _exchange_copies(srcs, lands, send_sems, recv_sems, local_sems):
            cp.start()
        token[...] = jnp.zeros_like(token)

    hbm = [pltpu.HBM(g.shape, g.dtype) for g in gs]
    outs = pl.pallas_call(
        body, name=name,
        out_shape=(pltpu.SemaphoreType.DMA((7 * n,)), pltpu.SemaphoreType.DMA((7 * n,)), pltpu.SemaphoreType.DMA((n,)),
                   *hbm, *hbm, _sds((8, 128), f32)),
        in_specs=[HBM_SPEC] * (2 * n),
        out_specs=(SEM_SPEC, SEM_SPEC, SEM_SPEC, *[HBM_SPEC] * (2 * n), pl.BlockSpec(memory_space=pltpu.VMEM)),
        input_output_aliases={i: 3 + i for i in range(2 * n)},
        compiler_params=pltpu.CompilerParams(has_side_effects=DATAFLOW),
    )(*[pltpu.with_memory_space_constraint(g, pltpu.HBM) for g in gs],
      *[pltpu.with_memory_space_constraint(lax.empty(g.shape, g.dtype), pltpu.HBM) for g in gs])
    return outs[:-1], outs[-1]


def _exchange_wait(handles, after, name):
    n = (len(handles) - 3) // 2
    send_sems, recv_sems, local_sems = handles[:3]
    srcs, lands = handles[3:3 + n], handles[3 + n:]

    def body(*refs):
        srcs, lands = refs[:n], refs[n:2 * n]
        send_sems, recv_sems, local_sems = refs[2 * n:2 * n + 3]
        for cp in _exchange_copies(srcs, lands, send_sems, recv_sems, local_sems):
            cp.wait()

    hbm = [pltpu.HBM(g.shape, g.dtype) for g in srcs]
    outs = pl.pallas_call(
        body, name=name, out_shape=(*hbm, *hbm),
        in_specs=[HBM_SPEC] * (2 * n) + [SEM_SPEC] * 3 + [pl.BlockSpec(memory_space=pl.ANY)],
        out_specs=tuple([HBM_SPEC] * (2 * n)),
        input_output_aliases={i: i for i in range(2 * n)},
        compiler_params=pltpu.CompilerParams(has_side_effects=DATAFLOW),
    )(*srcs, *lands, send_sems, recv_sems, local_sems, after)
    return outs[n:]


def _row_tile(rows, cols):
    cap = max(8, (1 << 18) // cols)
    if rows <= cap:
        return rows
    best = None
    for t in range(8, cap + 1, 8):
        if rows % t == 0:
            best = t
    assert best is not None, (rows, cols)
    return best


def _sum_parts(parts, name):
    npart, rows, cols = parts.shape
    tr = _row_tile(rows, cols)

    def body(p_ref, o_ref):
        g = p_ref[0].astype(f32)
        for j in range(1, npart):
            g = g + p_ref[j].astype(f32)
        o_ref[...] = g

    return pl.pallas_call(
        body, name=name, grid=(rows // tr,),
        in_specs=[pl.BlockSpec((npart, tr, cols), lambda i: (0, i, 0))], out_specs=pl.BlockSpec((tr, cols), lambda i: (i, 0)),
        out_shape=_sds((rows, cols), f32), compiler_params=_params(("parallel",)),
    )(parts)


def _adamw(parts, w, m, v, name):
    npart, rows, cols = parts.shape
    tr = _row_tile(rows, cols)
    c1 = 1.0 / (1.0 - ADAM_B1 ** ADAM_STEP)
    c2 = 1.0 / (1.0 - ADAM_B2 ** ADAM_STEP)

    def body(p_ref, w_ref, m_ref, v_ref, g_ref, d_ref, mo_ref, vo_ref):
        g = p_ref[0].astype(f32)
        for j in range(1, npart):
            g = g + p_ref[j].astype(f32)
        mn = ADAM_B1 * m_ref[...] + (1.0 - ADAM_B1) * g
        vn = ADAM_B2 * v_ref[...] + (1.0 - ADAM_B2) * (g * g)
        g_ref[...] = g
        mo_ref[...] = mn
        vo_ref[...] = vn
        d_ref[...] = (-ADAM_LR) * ((mn * c1) / (jnp.sqrt(vn * c2) + ADAM_EPS) + ADAM_WD * w_ref[...])

    blk = pl.BlockSpec((tr, cols), lambda i: (i, 0))
    return pl.pallas_call(
        body, name=name, grid=(rows // tr,),
        in_specs=[pl.BlockSpec((npart, tr, cols), lambda i: (0, i, 0)), blk, blk, blk], out_specs=[blk] * 4,
        out_shape=[_sds((rows, cols), f32)] * 4, compiler_params=_params(("parallel",)),
    )(parts, w, m, v)


def _adamw_layer(parts, w, m, v, acc, l, name):
    npart, rows, cols = parts.shape
    tr = _row_tile(rows, cols)
    c1 = 1.0 / (1.0 - ADAM_B1 ** ADAM_STEP)
    c2 = 1.0 / (1.0 - ADAM_B2 ** ADAM_STEP)

    def body(p_ref, w_ref, m_ref, v_ref, *refs):
        g_ref, d_ref, mo_ref, vo_ref = refs[-4:]
        g = p_ref[0].astype(f32)
        for j in range(1, npart):
            g = g + p_ref[j].astype(f32)
        mn = ADAM_B1 * m_ref[...] + (1.0 - ADAM_B1) * g
        vn = ADAM_B2 * v_ref[...] + (1.0 - ADAM_B2) * (g * g)
        g_ref[...] = g
        mo_ref[...] = mn
        vo_ref[...] = vn
        d_ref[...] = (-ADAM_LR) * ((mn * c1) / (jnp.sqrt(vn * c2) + ADAM_EPS) + ADAM_WD * w_ref[...])

    blk = pl.BlockSpec((None, tr, cols), lambda i: (l, i, 0))
    prev = [] if acc is None else list(acc)
    return pl.pallas_call(
        body, name=name, grid=(rows // tr,),
        in_specs=[pl.BlockSpec((npart, tr, cols), lambda i: (0, i, 0)), blk, blk, blk] + [HBM_ANY] * len(prev),
        out_specs=[blk] * 4, out_shape=[_sds(w.shape, f32)] * 4,
        input_output_aliases={4 + j: j for j in range(len(prev))},
        compiler_params=_params(("parallel",)),
    )(parts, w, m, v, *prev)


def _pack(arrays):
    rows = []
    for a in arrays:
        flat = a.reshape(-1)
        pad = (-flat.shape[0]) % 1024
        rows.append(jnp.concatenate([flat, jnp.zeros((pad,), flat.dtype)]).reshape(-1, 128))
    return jnp.concatenate(rows, axis=0)


def _unpack(flat, shapes):
    out, r = [], 0
    for s in shapes:
        n = math.prod(s)
        nr = (n + 1023) // 1024 * 8
        out.append(flat[r:r + nr].reshape(-1)[:n].reshape(s))
        r += nr
    return out


BIG_SHARD_AXIS = dict(w_in=2, w_branch=3, w_out=1, w_ffn_in=2, w_ffn_out=1)
SHARDED_SMALL = ("attn_rel_bias", "lru_conv_w")


def _to_blocks(full, axis):
    s = full.shape
    cut = full.reshape(s[:axis] + (NDEV, s[axis] // NDEV) + s[axis + 1:])
    return jnp.moveaxis(cut, axis, 0)


def _from_blocks(blocks, axis):
    moved = jnp.moveaxis(blocks, 0, axis)
    s = moved.shape
    return moved.reshape(s[:axis] + (s[axis] * s[axis + 1],) + s[axis + 2:])


def _flat2(a):
    return a.reshape(-1, a.shape[-1])


def _my_slice(a, n):
    ax, ay, ac = _mesh_pos()
    return lax.dynamic_slice_in_dim(a, (4 * ax + 2 * ay + ac) * n, n, axis=a.ndim - 1)


_WEIGHTS = ("norm_mix_pre", "norm_mix_post", "norm_ffn_pre", "norm_ffn_post", "w_in", "attn_rel_bias", "hgrn_lb_logits",
            "hgrn_norm_g", "gmlp_norm_g", "gmlp_ws", "gmlp_bs", "lru_conv_w", "lru_conv_b", "lru_wa", "lru_ba", "lru_wx",
            "lru_bx", "lru_lambda", "w_branch", "w_out", "w_ffn_in", "w_ffn_out")


def _step(x, loss_target, w, m, v):
    gathered = []
    for l in range(DEPTH):
        gathered.append(tuple(_gather_sc([w[k][l].astype(bf16) for k in keys], "gather_%s%d" % (half, l))
                              for half, keys in (("mix", MIX_BIG), ("ffn", FFN_BIG))))
    cut = jnp.concatenate([w[k] for k in SHARDED_SMALL], axis=-1)
    parts = _all_gather(_pack([cut]), "gather_small").reshape(NDEV, -1)[:, :math.prod(cut.shape)].reshape((NDEV,) + cut.shape)
    small = {k: w[k] for k in SMALL if k not in SHARDED_SMALL}
    at = 0
    for k in SHARDED_SMALL:
        n = w[k].shape[-1]
        small[k] = _from_blocks(parts[..., at:at + n], 2)
        at += n
    loss, dx, layers = _step_forward(x, loss_target, gathered, small)
    flat3 = lambda a: a.reshape((DEPTH, -1, a.shape[-1]))
    acc = {k: None for k in BIG}
    smalls = [None] * DEPTH

    def send(grads, keys, name):
        handles, token = _exchange_start([_to_blocks(grads[k], BIG_SHARD_AXIS[k] - 1) for k in keys], "start_" + name)
        return (keys, handles, "wait_" + name), token[0:1, 0:1]

    def update(sent, l, after):
        keys, handles, name = sent
        got = dict(zip(keys, _exchange_wait(handles, after, name)))
        for k, g in got.items():
            w3 = flat3(w[k])
            acc[k] = _adamw_layer(g.reshape((NDEV,) + w3.shape[1:]), w3, flat3(m[k]), flat3(v[k]), acc[k], l,
                                  "adamw_%s_%d" % (k, l))

    waiting = []
    for l in range(DEPTH - 1, -1, -1):
        sent_ffn = []

        def ffn_grads_ready(grads, dx1, l=l, sent_ffn=sent_ffn):
            sent, zero = send(grads, FFN_BIG, "exchange_ffn%d" % l)
            sent_ffn.append(sent)
            while waiting:
                update(*waiting.pop(), dx1)
            return zero

        sent_mix = []

        def mix_grads_ready(grads, l=l, sent_mix=sent_mix):
            sent, zero = send(grads, MIX_BIG, "exchange_mix%d" % l)
            sent_mix.append(sent)
            return zero

        dx, _, smalls[l] = _step_backward(dx, layers[l], ffn_grads_ready, mix_grads_ready)
        update(sent_ffn[0], l, dx)
        waiting.append((sent_mix[0], l))
    grads, deltas, new_m, new_v = {}, {}, {}, {}
    gsmall ={k: jnp.stack([smalls[l][k] for l in range(DEPTH)]) for k in smalls[0]}
    gsmall["hgrn_lb_logits"] = _lb_bwd(small["hgrn_lb_logits"], gsmall.pop("lb"))
    shapes = [gsmall[k].shape for k in SMALL]
    sums = _unpack(_sum_parts(_all_gather(_pack([gsmall[k] for k in SMALL]), "gather_small_grads"), "sum_small_grads"), shapes)
    gs = dict(zip(SMALL, sums))
    for k in SHARDED_SMALL:
        gs[k] = _my_slice(gs[k], w[k].shape[-1])
    packed = [_pack([d[k] for k in SMALL]) for d in (gs, w, m, v)]
    outs = _adamw(packed[0][None], packed[1], packed[2], packed[3], "adamw_small")
    shapes = [w[k].shape for k in SMALL]
    for d, o in zip((grads, deltas, new_m, new_v), outs):
        d.update(zip(SMALL, _unpack(o, shapes)))
    update(*waiting.pop(), outs[1])
    for k in BIG:
        grads[k], deltas[k], new_m[k], new_v[k] = (o.reshape(w[k].shape) for o in acc[k])
    total = lax.psum(loss[0, 0], ("x", "y", "c"))
    return total, dx[None], grads, deltas, new_m, new_v


def _step_forward(x, loss_target, gathered, small):
    lbs = _lb_fwd(small["hgrn_lb_logits"])
    x = x[0]
    layers = []

    def weights(blocks, keys, after):
        if after is not None:
            blocks, _ = lax.optimization_barrier((blocks, after))
        return {k: _from_blocks(g, BIG_SHARD_AXIS[k] - 1) for k, g in zip(keys, blocks)}

    for l in range(DEPTH):
        mix, ffn = gathered[l]
        p = _layer_params(l, weights(mix, MIX_BIG, x if l else None), small, lbs)
        x, sv = _layer_fwd(x, p, lambda x1, ffn=ffn: _ffn_weights(weights(ffn, FFN_BIG, x1)))
        layers.append((p, sv))
    loss, dx = _loss_head(x, loss_target[0])
    return loss, dx, layers


def _step_backward(dx, layer, ffn_grads_ready, mix_grads_ready):
    return _layer_bwd(dx, *layer, ffn_grads_ready, mix_grads_ready)


def kernel(x, norm_mix_pre, norm_mix_post, norm_ffn_pre, norm_ffn_post, w_in, attn_rel_bias, hgrn_lb_logits, hgrn_norm_g, gmlp_norm_g, gmlp_ws, gmlp_bs, lru_conv_w, lru_conv_b, lru_wa, lru_ba, lru_wx, lru_bx, lru_lambda, w_branch, w_out, w_ffn_in, w_ffn_out, loss_target, m_norm_mix_pre, m_norm_mix_post, m_norm_ffn_pre, m_norm_ffn_post, m_w_in, m_attn_rel_bias, m_hgrn_lb_logits, m_hgrn_norm_g, m_gmlp_norm_g, m_gmlp_ws, m_gmlp_bs, m_lru_conv_w, m_lru_conv_b, m_lru_wa, m_lru_ba, m_lru_wx, m_lru_bx, m_lru_lambda, m_w_branch, m_w_out, m_w_ffn_in, m_w_ffn_out, v_norm_mix_pre, v_norm_mix_post, v_norm_ffn_pre, v_norm_ffn_post, v_w_in, v_attn_rel_bias, v_hgrn_lb_logits, v_hgrn_norm_g, v_gmlp_norm_g, v_gmlp_ws, v_gmlp_bs, v_lru_conv_w, v_lru_conv_b, v_lru_wa, v_lru_ba, v_lru_wx, v_lru_bx, v_lru_lambda, v_w_branch, v_w_out, v_w_ffn_in, v_w_ffn_out):
    w = dict(zip(_WEIGHTS, (norm_mix_pre, norm_mix_post, norm_ffn_pre, norm_ffn_post, w_in, attn_rel_bias, hgrn_lb_logits, hgrn_norm_g, gmlp_norm_g, gmlp_ws, gmlp_bs, lru_conv_w, lru_conv_b, lru_wa, lru_ba, lru_wx, lru_bx, lru_lambda, w_branch, w_out, w_ffn_in, w_ffn_out)))
    m = dict(zip(_WEIGHTS, (m_norm_mix_pre, m_norm_mix_post, m_norm_ffn_pre, m_norm_ffn_post, m_w_in, m_attn_rel_bias, m_hgrn_lb_logits, m_hgrn_norm_g, m_gmlp_norm_g, m_gmlp_ws, m_gmlp_bs, m_lru_conv_w, m_lru_conv_b, m_lru_wa, m_lru_ba, m_lru_wx, m_lru_bx, m_lru_lambda, m_w_branch, m_w_out, m_w_ffn_in, m_w_ffn_out)))
    v = dict(zip(_WEIGHTS, (v_norm_mix_pre, v_norm_mix_post, v_norm_ffn_pre, v_norm_ffn_post, v_w_in, v_attn_rel_bias, v_hgrn_lb_logits, v_hgrn_norm_g, v_gmlp_norm_g, v_gmlp_ws, v_gmlp_bs, v_lru_conv_w, v_lru_conv_b, v_lru_wa, v_lru_ba, v_lru_wx, v_lru_bx, v_lru_lambda, v_w_branch, v_w_out, v_w_ffn_in, v_w_ffn_out)))
    loss, grad_x, grads, deltas, new_m, new_v = _step(x, loss_target, w, m, v)
    return (loss, grad_x, *[grads[k] for k in _WEIGHTS], *[deltas[k] for k in _WEIGHTS],
            *[new_m[k] for k in _WEIGHTS], *[new_v[k] for k in _WEIGHTS])
```

```python
import math

import jax
import jax.numpy as jnp
from jax import lax
from jax.experimental import pallas as pl
from jax.experimental.pallas import tpu as pltpu
from jax.experimental.pallas import tpu_sc as plsc

f32 = jnp.float32
bf16 = jnp.bfloat16

SEQ = 2048
DM = 1024
DEPTH = 4
NDEV = 8
MIXW = 256
NHEAD = 4
HDIM = 64
NMIX = 11 * MIXW
NGATE = 4 * DM
FFH = 2816
EPS = 1e-6
NEG_BIG = -1e30
LOG_FLOOR = 1e-30
LRU_C = 8.0
REL_SIZE = 320
ATT_PAIR = 128
ATT_BAND = 640
ATT_PAD = 512
ATT_WV = 768
HG_T = 16
HG_N = SEQ // HG_T
GM_T = 128
LRU_T = 128
ADAM_LR, ADAM_B1, ADAM_B2, ADAM_EPS, ADAM_WD, ADAM_STEP = 0.001, 0.9, 0.999, 1e-8, 0.01, 10
V7X_VMEM_LIMIT = 56 * 1024 * 1024
GELU_C0 = math.sqrt(2.0 / math.pi)
GELU_C1 = 0.044715
MESH_ID = pl.DeviceIdType.MESH


def _params(sem=None):
    if sem is None:
        return pltpu.CompilerParams(vmem_limit_bytes=V7X_VMEM_LIMIT)
    return pltpu.CompilerParams(dimension_semantics=sem, vmem_limit_bytes=V7X_VMEM_LIMIT)


def _sds(shape, dtype):
    return jax.ShapeDtypeStruct(shape, dtype)


def _dot(a, b):
    return jnp.dot(a.astype(bf16), b.astype(bf16), preferred_element_type=f32)


def _dot_nt(a, b):
    return lax.dot_general(a.astype(bf16), b.astype(bf16), (((1,), (1,)), ((), ())), preferred_element_type=f32)


def _dot_tn(a, b):
    return lax.dot_general(a.astype(bf16), b.astype(bf16), (((0,), (0,)), ((), ())), preferred_element_type=f32)


def _split(a):
    hi = a.astype(bf16)
    lo = (a - hi.astype(f32)).astype(bf16)
    return hi, lo


def _dot_hl(a, m):
    hi, lo = _split(a)
    return jnp.dot(hi, m, preferred_element_type=f32) + jnp.dot(lo, m, preferred_element_type=f32)


def _dot_nt_hl(m, a):
    hi, lo = _split(a)
    dn = (((1,), (1,)), ((), ()))
    return lax.dot_general(m, hi, dn, preferred_element_type=f32) + lax.dot_general(m, lo, dn, preferred_element_type=f32)


def _sigmoid(x):
    return jax.nn.sigmoid(x)


def _silu(x):
    return x * _sigmoid(x)


def _dsilu(x):
    s = _sigmoid(x)
    return s * (1.0 + x * (1.0 - s))


def _gelu(x):
    return 0.5 * x * (1.0 + jnp.tanh(GELU_C0 * (x + GELU_C1 * x * x * x)))


def _dgelu(x):
    t = jnp.tanh(GELU_C0 * (x + GELU_C1 * x * x * x))
    return 0.5 * (1.0 + t) + 0.5 * x * (1.0 - t * t) * GELU_C0 * (1.0 + 3.0 * GELU_C1 * x * x)


def _rms(x, g):
    r = lax.rsqrt(jnp.mean(x * x, axis=-1, keepdims=True) + EPS)
    return x * r * g


def _rms_bwd(x, g, dy):
    r = lax.rsqrt(jnp.mean(x * x, axis=-1, keepdims=True) + EPS)
    xh = x * r
    dxh = dy * g
    dx = r * (dxh - xh * jnp.mean(dxh * xh, axis=-1, keepdims=True))
    return dx, jnp.sum(dy * xh, axis=0, keepdims=True)


def _same_head(n, width, dtype):
    r = lax.broadcasted_iota(jnp.int32, (n, n), 0) // width
    c = lax.broadcasted_iota(jnp.int32, (n, n), 1) // width
    return (r == c).astype(dtype)


def _head_masks(rows=1):
    lane = lax.broadcasted_iota(jnp.int32, (rows, MIXW), 1) // HDIM
    return [lane == h for h in range(NHEAD)]


def _norm_matmul(x, g, w, tn):
    n = w.shape[1]
    tm = 1024

    def body(x_ref, g_ref, w_ref, z_ref, h_ref):
        @pl.when(pl.program_id(1) == 0)
        def _():
            h_ref[...] = _rms(x_ref[...], g_ref[...]).astype(bf16)

        z_ref[...] = jnp.dot(h_ref[...], w_ref[...], preferred_element_type=f32)

    return pl.pallas_call(
        body, name="norm_matmul", grid=(SEQ // tm, n // tn),
        in_specs=[pl.BlockSpec((tm, DM), lambda i, j: (i, 0)), pl.BlockSpec((1, DM), lambda i, j: (0, 0)),
                  pl.BlockSpec((DM, tn), lambda i, j: (0, j))],
        out_specs=[pl.BlockSpec((tm, tn), lambda i, j: (i, j)), pl.BlockSpec((tm, DM), lambda i, j: (i, 0))],
        out_shape=[_sds((SEQ, n), f32), _sds((SEQ, DM), bf16)],
        compiler_params=_params(("parallel", "arbitrary")),
    )(x, g, w)


def _matmul(a, w, tn):
    k, n = w.shape
    tm = 1024

    def body(a_ref, w_ref, z_ref):
        z_ref[...] = jnp.dot(a_ref[...], w_ref[...], preferred_element_type=f32)

    return pl.pallas_call(
        body, name="matmul", grid=(SEQ // tm, n // tn),
        in_specs=[pl.BlockSpec((tm, k), lambda i, j: (i, 0)), pl.BlockSpec((k, tn), lambda i, j: (0, j))],
        out_specs=pl.BlockSpec((tm, tn), lambda i, j: (i, j)),
        out_shape=_sds((SEQ, n), f32),
        compiler_params=_params(("parallel", "arbitrary")),
    )(a, w)


def _att_offset_map():
    i = lax.broadcasted_iota(jnp.int32, (REL_SIZE, ATT_WV), 0)
    t = lax.broadcasted_iota(jnp.int32, (REL_SIZE, ATT_WV), 1)
    e = jnp.where(t <= ATT_BAND, t, t - ATT_WV)
    idx = jnp.clip(ATT_PAD - e, -(HDIM - 1), 256) + (HDIM - 1)
    return (idx == i).astype(bf16)


def _att_band_valid():
    qc = lax.broadcasted_iota(jnp.int32, (ATT_PAIR, ATT_BAND), 0) // HDIM
    kc = lax.broadcasted_iota(jnp.int32, (ATT_PAIR, ATT_BAND), 1) // HDIM
    return (kc >= qc) & (kc <= qc + 8)


def _att_bias_tiles(rb_ref, bm_ref):
    wv = _dot_hl(rb_ref[...], _att_offset_map())
    valid = _att_band_valid()
    for h in range(NHEAD):
        rows = jnp.broadcast_to(wv[h:h + 1, :], (ATT_PAIR, ATT_WV))
        tile = pltpu.roll(rows, 0, 1, stride=1, stride_axis=0)[:, :ATT_BAND]
        bm_ref[h] = jnp.where(valid, tile, NEG_BIG)


def _att_pad_kv(k_ref, v_ref, kp_ref, vp_ref):
    kp_ref[pl.ds(0, ATT_PAD), :] = jnp.zeros((ATT_PAD, MIXW), bf16)
    vp_ref[pl.ds(0, ATT_PAD), :] = jnp.zeros((ATT_PAD, MIXW), bf16)
    kp_ref[pl.ds(ATT_PAD, SEQ), :] = k_ref[...].astype(bf16)
    vp_ref[pl.ds(ATT_PAD, SEQ), :] = v_ref[...].astype(bf16)


def _att_probs(qm, kb, bm, key_ok):
    s = _dot_nt(qm, kb) + bm
    s = jnp.where(key_ok, s, NEG_BIG)
    m = jnp.max(s, axis=-1, keepdims=True)
    e = jnp.exp(s - m)
    return e / jnp.sum(e, axis=-1, keepdims=True)


def _attn_fwd(zm, rb8):
    def body(q_ref, k_ref, v_ref, rb_ref, o_ref, kp_ref, vp_ref, bm_ref):
        _att_pad_kv(k_ref, v_ref, kp_ref, vp_ref)
        _att_bias_tiles(rb_ref, bm_ref)
        hm = _head_masks()

        def pair(p, carry):
            r0 = pl.multiple_of(p * ATT_PAIR, ATT_PAIR)
            q = q_ref[pl.ds(r0, ATT_PAIR), :] * (HDIM ** -0.5)
            kb = kp_ref[pl.ds(r0, ATT_BAND), :]
            vb = vp_ref[pl.ds(r0, ATT_BAND), :]
            key_ok = (lax.broadcasted_iota(jnp.int32, (1, ATT_BAND), 1) + (r0 - ATT_PAD)) >= 0
            o = jnp.zeros((ATT_PAIR, MIXW), f32)
            for h in range(NHEAD):
                qm = jnp.where(hm[h], q, 0.0)
                p_h = _att_probs(qm, kb, bm_ref[h], key_ok)
                o = o + jnp.where(hm[h], _dot(p_h, vb), 0.0)
            o_ref[pl.ds(r0, ATT_PAIR), :] = o.astype(bf16)
            return carry

        lax.fori_loop(0, SEQ // ATT_PAIR, pair, 0)

    col = lambda j: pl.BlockSpec((SEQ, MIXW), lambda i: (0, j))
    return pl.pallas_call(
        body, name="attn_fwd", grid=(1,),
        in_specs=[col(0), col(1), col(2), pl.BlockSpec((8, REL_SIZE), lambda i: (0, 0))],
        out_specs=pl.BlockSpec((SEQ, MIXW), lambda i: (0, 0)),
        out_shape=_sds((SEQ, MIXW), bf16),
        scratch_shapes=[pltpu.VMEM((SEQ + ATT_PAD, MIXW), bf16), pltpu.VMEM((SEQ + ATT_PAD, MIXW), bf16),
                        pltpu.VMEM((NHEAD, ATT_PAIR, ATT_BAND), f32)],
        compiler_params=_params(("arbitrary",)),
    )(zm, zm, zm, rb8)


def _hg_gates(q, fz, lb):
    sq = _sigmoid(q)
    sg = _sigmoid(fz)
    f = lb + (1.0 - lb) * sg
    return q * sq, (1.0 - lb) * (1.0 - sg), jnp.log(jnp.maximum(f, LOG_FLOOR)), sq, sg, f


def _hg_prepare(q_ref, f_ref, lb, qf_s, kf_s, b_s, qd_s, kd_s, dec_s):
    b = None
    for t in range(HG_T):
        qf, kf, lf, _, _, _ = _hg_gates(q_ref[:, t, :], f_ref[:, t, :], lb)
        b = lf if b is None else b + lf
        qf_s[:, t, :] = qf
        kf_s[:, t, :] = kf
        b_s[:, t, :] = b
    b_last = b
    dec_s[...] = jnp.broadcast_to(jnp.exp(b_last)[:, None, :], (HG_N, 8, MIXW))
    for t in range(HG_T):
        bt = b_s[:, t, :]
        qd_s[:, t, :] = qf_s[:, t, :] * jnp.exp(bt)
        kd_s[:, t, :] = kf_s[:, t, :] * jnp.exp(b_last - bt)


def _hg_scores(t, qf_s, kf_s, b_s, w_s, hm):
    qt = qf_s[:, t, :]
    bt = b_s[:, t, :]
    for s in range(t + 1):
        w = qt * kf_s[:, s, :]
        if s < t:
            w = w * jnp.exp(bt - b_s[:, s, :])
        w_s[pl.ds(s * HG_N, HG_N), :] = w.astype(bf16)
    return jnp.dot(w_s[pl.ds(0, (t + 1) * HG_N), :], hm, preferred_element_type=f32)


def _hgrn_fwd(zm3, lb, ng):
    def body(q_ref, f_ref, i_ref, g_ref, lb_ref, ng_ref, o_ref, oraw_ref, states_ref,
             qf_s, kf_s, b_s, qd_s, kd_s, dec_s, w_s, st_s):
        lb = lb_ref[...]
        hm = _same_head(MIXW, HDIM, bf16)
        hmf = _same_head(MIXW, HDIM, f32)
        _hg_prepare(q_ref, f_ref, lb, qf_s, kf_s, b_s, qd_s, kd_s, dec_s)
        for t in range(HG_T):
            p = _hg_scores(t, qf_s, kf_s, b_s, w_s, hm)
            acc = jnp.zeros((HG_N, MIXW), f32)
            for s in range(t + 1):
                acc = acc + p[s * HG_N:(s + 1) * HG_N] * i_ref[:, s, :]
            oraw_ref[:, t, :] = acc
        st_s[...] = jnp.zeros((MIXW, MIXW), f32)

        def step(n, carry):
            st = st_s[...]
            stb = st.astype(bf16)
            states_ref[n] = stb
            oraw_ref[n] = oraw_ref[n] + _dot_nt(qd_s[n], stb)
            st_s[...] = st * dec_s[n][0:1] + _dot_tn(i_ref[n], kd_s[n]) * hmf
            return carry

        lax.fori_loop(0, HG_N, step, 0, unroll=2)
        ngv = ng_ref[...]
        for t in range(HG_T):
            o = oraw_ref[:, t, :]
            ms = _dot_hl(o * o, hm) * (1.0 / HDIM)
            o_ref[:, t, :] = (o * lax.rsqrt(ms + EPS) * ngv * _silu(g_ref[:, t, :])).astype(bf16)

    one = pl.Buffered(1)
    col = lambda j: pl.BlockSpec((HG_N, HG_T, MIXW), lambda i: (0, 0, j), pipeline_mode=one)
    vec = pl.BlockSpec((1, MIXW), lambda i: (0, 0))
    blk = pl.BlockSpec((HG_N, HG_T, MIXW), lambda i: (0, 0, 0))
    s3 = pltpu.VMEM((HG_N, HG_T, MIXW), f32)
    return pl.pallas_call(
        body, name="hgrn_fwd", grid=(1,),
        in_specs=[col(3), col(4), col(5), col(6), vec, vec],
        out_specs=[blk, blk, pl.BlockSpec((HG_N, MIXW, MIXW), lambda i: (0, 0, 0), pipeline_mode=one)],
        out_shape=[_sds((HG_N, HG_T, MIXW), bf16), _sds((HG_N, HG_T, MIXW), f32), _sds((HG_N, MIXW, MIXW), bf16)],
        scratch_shapes=[s3, s3, s3, s3, s3, pltpu.VMEM((HG_N, 8, MIXW), f32),
                        pltpu.VMEM((HG_T * HG_N, MIXW), bf16), pltpu.VMEM((MIXW, MIXW), f32)],
        compiler_params=_params(("arbitrary",)),
    )(zm3, zm3, zm3, zm3, lb, ng)


def _gm_weights(ws_ref):
    tril = lax.broadcasted_iota(jnp.int32, (GM_T, GM_T), 0) >= lax.broadcasted_iota(jnp.int32, (GM_T, GM_T), 1)
    return tril, [jnp.where(tril, ws_ref[g], 0.0).astype(bf16) for g in range(NHEAD)]


def _gm_expand():
    r = lax.broadcasted_iota(jnp.int32, (8, MIXW), 0)
    c = lax.broadcasted_iota(jnp.int32, (8, MIXW), 1) // HDIM
    return (r == c).astype(bf16)


def _gm_mixed(vn, wts, bias, hm):
    vb = vn.astype(bf16)
    mixed = bias
    for g in range(NHEAD):
        mixed = mixed + jnp.where(hm[g], jnp.dot(wts[g], vb, preferred_element_type=f32), 0.0)
    return mixed


def _gm_bias(bs_ref):
    hi, lo = _split(bs_ref[...])
    et = _gm_expand()
    dn = (((0,), (0,)), ((), ()))
    return lax.dot_general(hi, et, dn, preferred_element_type=f32) + lax.dot_general(lo, et, dn, preferred_element_type=f32)


def _gmlp_fwd(zm, ng, ws, bs8):
    def body(u_ref, v_ref, ng_ref, ws_ref, bs_ref, o_ref):
        hm = _head_masks()
        _, wts = _gm_weights(ws_ref)
        bias = _gm_bias(bs_ref)
        ngv = ng_ref[...]

        def blk(n, carry):
            rows = pl.ds(pl.multiple_of(n * GM_T, GM_T), GM_T)
            vn = _rms(_gelu(v_ref[rows, :]), ngv)
            o_ref[rows, :] = (_gelu(u_ref[rows, :]) * _gm_mixed(vn, wts, bias, hm)).astype(bf16)
            return carry

        lax.fori_loop(0, SEQ // GM_T, blk, 0)

    col = lambda j: pl.BlockSpec((SEQ, MIXW), lambda i: (0, j))
    return pl.pallas_call(
        body, name="gmlp_fwd", grid=(1,),
        in_specs=[col(7), col(8), pl.BlockSpec((1, MIXW), lambda i: (0, 0)),
                  pl.BlockSpec((NHEAD, GM_T, GM_T), lambda i: (0, 0, 0)), pl.BlockSpec((8, GM_T), lambda i: (0, 0))],
        out_specs=pl.BlockSpec((SEQ, MIXW), lambda i: (0, 0)),
        out_shape=_sds((SEQ, MIXW), bf16),
        compiler_params=_params(("arbitrary",)),
    )(zm, zm, ng, ws, bs8)


def _lru_conv(x_ref, cw_ref, cb_ref, xp_s, xc_s):
    xp_s[pl.ds(0, 8), :] = jnp.zeros((8, MIXW), f32)
    xp_s[pl.ds(8, SEQ), :] = x_ref[...]
    cw = cw_ref[...]
    xc = cb_ref[...] + x_ref[...] * cw[3:4]
    for k in range(1, 4):
        xc = xc + xp_s[pl.ds(8 - k, SEQ), :] * cw[3 - k:4 - k]
    xc_s[...] = xc


def _lru_gates(xc, wa, ba, wx, bx, sp, first_row):
    r = _sigmoid(_dot(xc, wa) + ba)
    ig = _sigmoid(_dot(xc, wx) + bx)
    la = (-LRU_C) * r * sp
    a = jnp.exp(la)
    th = jnp.tanh(la)
    m2 = -2.0 * th / (1.0 - th)
    mult = jnp.where(first_row, 1.0, jnp.sqrt(jnp.maximum(m2, 0.0)))
    return a, mult, r, ig, m2


def _lru_scan(a, b, rev):
    row = lax.broadcasted_iota(jnp.int32, (LRU_T, 1), 0)
    k = 1
    while k < LRU_T:
        ok = (row < LRU_T - k) if rev else (row >= k)
        sh = (LRU_T - k) if rev else k
        a_sh = jnp.where(ok, pltpu.roll(a, sh, 0), 1.0)
        b_sh = jnp.where(ok, pltpu.roll(b, sh, 0), 0.0)
        b = b + a * b_sh
        a = a * a_sh
        k *= 2
    return a, b


def _lru_fwd(zm, cw8, cb, wa, ba, wx, bx, lam):
    def body(x_ref, g_ref, cw_ref, cb_ref, wa_ref, ba_ref, wx_ref, bx_ref, lam_ref, o_ref, h_ref, xp_s, xc_s):
        _lru_conv(x_ref, cw_ref, cb_ref, xp_s, xc_s)
        sp = jax.nn.softplus(-lam_ref[...])
        wa_v, wx_v, ba_v, bx_v = wa_ref[...], wx_ref[...], ba_ref[...], bx_ref[...]

        def chunk(c, h_prev):
            rows = pl.ds(pl.multiple_of(c * LRU_T, LRU_T), LRU_T)
            first = (lax.broadcasted_iota(jnp.int32, (LRU_T, 1), 0) + c * LRU_T) == 0
            xc = xc_s[rows, :]
            a, mult, _, ig, _ = _lru_gates(xc, wa_v, ba_v, wx_v, bx_v, sp, first)
            acum, hloc = _lru_scan(a, mult * (ig * xc), False)
            h = hloc + acum * h_prev
            h_ref[rows, :] = h
            o_ref[rows, :] = (h * _gelu(g_ref[rows, :])).astype(bf16)
            return h[LRU_T - 1:LRU_T, :]

        lax.fori_loop(0, SEQ // LRU_T, chunk, jnp.zeros((1, MIXW), f32))

    col = lambda j: pl.BlockSpec((SEQ, MIXW), lambda i: (0, j))
    vec = pl.BlockSpec((1, MIXW), lambda i: (0, 0))
    mat = pl.BlockSpec((MIXW, MIXW), lambda i: (0, 0))
    out = pl.BlockSpec((SEQ, MIXW), lambda i: (0, 0))
    return pl.pallas_call(
        body, name="lru_fwd", grid=(1,),
        in_specs=[col(9), col(10), pl.BlockSpec((8, MIXW), lambda i: (0, 0)), vec, mat, vec, mat, vec, vec],
        out_specs=[out, out],
        out_shape=[_sds((SEQ, MIXW), bf16), _sds((SEQ, MIXW), f32)],
        scratch_shapes=[pltpu.VMEM((SEQ + 8, MIXW), f32), pltpu.VMEM((SEQ, MIXW), f32)],
        compiler_params=_params(("arbitrary",)),
    )(zm, zm, cw8, cb, wa, ba, wx, bx, lam)


def _block_diag(w):
    out = jnp.zeros((MIXW, MIXW), w.dtype)
    for h in range(NHEAD):
        out = lax.dynamic_update_slice(out, w[h], (h * HDIM, h * HDIM))
    return out


def _diag_blocks(w):
    return jnp.stack([w[h * HDIM:(h + 1) * HDIM, h * HDIM:(h + 1) * HDIM] for h in range(NHEAD)])


ROW_TILE = 256


def _merge_fwd(outs, zg, wb, wo, x, g2):
    def body(oa_ref, ob_ref, oc_ref, od_ref, zg_ref, wb_ref, wo_ref, x_ref, g_ref, xo_ref, mg_ref, y_ref):
        merged = jnp.zeros((ROW_TILE, DM), f32)
        for n, o_ref in enumerate((oa_ref, ob_ref, oc_ref, od_ref)):
            proj = jnp.dot(o_ref[...], wb_ref[n], preferred_element_type=f32)
            merged = merged + _sigmoid(zg_ref[:, n * DM:(n + 1) * DM]) * proj
        mb = merged.astype(bf16)
        y = jnp.dot(mb, wo_ref[...], preferred_element_type=f32)
        mg_ref[...] = mb
        y_ref[...] = y
        xo_ref[...] = x_ref[...] + _rms(y, g_ref[...])

    row = lambda w: pl.BlockSpec((ROW_TILE, w), lambda i: (i, 0))
    return pl.pallas_call(
        body, name="merge_fwd", grid=(SEQ // ROW_TILE,),
        in_specs=[row(MIXW)] * 4 + [row(NGATE), pl.BlockSpec((NHEAD, MIXW, DM), lambda i: (0, 0, 0)),
                                    pl.BlockSpec((DM, DM), lambda i: (0, 0)), row(DM), pl.BlockSpec((1, DM), lambda i: (0, 0))],
        out_specs=[row(DM), row(DM), row(DM)],
        out_shape=[_sds((SEQ, DM), f32), _sds((SEQ, DM), bf16), _sds((SEQ, DM), f32)],
        compiler_params=_params(("parallel",)),
    )(*outs, zg, wb, wo, x, g2)


def _ffn_out(u, w2, x, g4):
    def body(u_ref, w_ref, x_ref, g_ref, xo_ref, f_ref):
        a = _silu(u_ref[:, :FFH]) * u_ref[:, FFH:]
        f = jnp.dot(a.astype(bf16), w_ref[...], preferred_element_type=f32)
        f_ref[...] = f
        xo_ref[...] = x_ref[...] + _rms(f, g_ref[...])

    row = lambda w: pl.BlockSpec((ROW_TILE, w), lambda i: (i, 0))
    return pl.pallas_call(
        body, name="ffn_out", grid=(SEQ // ROW_TILE,),
        in_specs=[row(2 * FFH), pl.BlockSpec((FFH, DM), lambda i: (0, 0)), row(DM), pl.BlockSpec((1, DM), lambda i: (0, 0))],
        out_specs=[row(DM), row(DM)],
        out_shape=[_sds((SEQ, DM), f32), _sds((SEQ, DM), f32)],
        compiler_params=_params(("parallel",)),
    )(u, w2, x, g4)


def _loss_head(x, tgt):
    tm = 512

    def body(x_ref, t_ref, l_ref, dx_ref):
        @pl.when(pl.program_id(0) == 0)
        def _():
            l_ref[...] = jnp.zeros((1, 1), f32)

        d = x_ref[...] - t_ref[...]
        dx_ref[...] = d * (1.0 / DM)
        l_ref[...] += (0.5 / DM) * jnp.sum(d * d).reshape(1, 1)

    row = pl.BlockSpec((tm, DM), lambda i: (i, 0))
    return pl.pallas_call(
        body, name="loss_head", grid=(SEQ // tm,),
        in_specs=[row, row], out_specs=[pl.BlockSpec((1, 1), lambda i: (0, 0)), row],
        out_shape=[_sds((1, 1), f32), _sds((SEQ, DM), f32)],
        compiler_params=_params(("arbitrary",)),
    )(x, tgt)


def _lb_fwd(logits):
    def body(lg_ref, o_ref):
        lg = lg_ref[...]
        e = jnp.exp(lg - jnp.max(lg, axis=0, keepdims=True))
        p = e / jnp.sum(e, axis=0, keepdims=True)
        acc = jnp.zeros((1, MIXW), f32)
        o_ref[0:1, :] = acc
        for l in range(1, DEPTH):
            acc = acc + p[l:l + 1]
            o_ref[l:l + 1, :] = acc

    return pl.pallas_call(body, name="lb_fwd", out_shape=_sds((DEPTH, MIXW), f32))(logits)


def _lb_bwd(logits, dlbs):
    def body(lg_ref, d_ref, o_ref):
        lg = lg_ref[...]
        e = jnp.exp(lg - jnp.max(lg, axis=0, keepdims=True))
        p = e / jnp.sum(e, axis=0, keepdims=True)
        d = d_ref[...]
        dp = [jnp.zeros((1, MIXW), f32)] * DEPTH
        acc = jnp.zeros((1, MIXW), f32)
        for j in range(DEPTH - 1, 0, -1):
            acc = acc + d[j:j + 1]
            dp[j] = acc
        inner = sum(p[j:j + 1] * dp[j] for j in range(DEPTH))
        for j in range(DEPTH):
            o_ref[j:j + 1, :] = p[j:j + 1] * (dp[j] - inner)

    return pl.pallas_call(body, name="lb_bwd", out_shape=_sds((DEPTH, MIXW), f32))(logits, dlbs)


def _pad_rows(a, rows=8):
    return jnp.concatenate([a, jnp.zeros((rows - a.shape[0], a.shape[1]), a.dtype)], axis=0)


def _layer_params(l, full, small, lbs):
    row = lambda name: small[name][l][None]
    return dict(
        _mix_weights(full), **(_ffn_weights(full) if "w_ffn_in" in full else {}),
        g1=row("norm_mix_pre"), g2=row("norm_mix_post"), g3=row("norm_ffn_pre"), g4=row("norm_ffn_post"),
        rb8=_pad_rows(small["attn_rel_bias"][l]), lb=lbs[l][None], hng=row("hgrn_norm_g"),
        gng=row("gmlp_norm_g"), gws=small["gmlp_ws"][l], gbs8=_pad_rows(small["gmlp_bs"][l]),
        cw8=_pad_rows(small["lru_conv_w"][l]), cb=row("lru_conv_b"),
        wa=_block_diag(small["lru_wa"][l]).astype(bf16), ba=row("lru_ba"),
        wx=_block_diag(small["lru_wx"][l]).astype(bf16), bx=row("lru_bx"), lam=row("lru_lambda"),
    )


def _mix_weights(full):
    return dict(wm=full["w_in"][:, :NMIX], wgt=full["w_in"][:, NMIX:], wb=full["w_branch"], wo=full["w_out"])


def _ffn_weights(full):
    return dict(w1=full["w_ffn_in"], w2=full["w_ffn_out"])


def _layer_fwd(x, p, late_ffn_weights=None):
    zm, h = _norm_matmul(x, p["g1"], p["wm"], 1408)
    zg = _matmul(h, p["wgt"], 1024)
    oa = _attn_fwd(zm, p["rb8"])
    ob3, obraw3, hstates = _hgrn_fwd(zm.reshape(HG_N, HG_T, NMIX), p["lb"], p["hng"])
    oc = _gmlp_fwd(zm, p["gng"], p["gws"], p["gbs8"])
    od, hd = _lru_fwd(zm, p["cw8"], p["cb"], p["wa"], p["ba"], p["wx"], p["bx"], p["lam"])
    outs = (oa, ob3.reshape(SEQ, MIXW), oc, od)
    x1, merged, y = _merge_fwd(outs, zg, p["wb"], p["wo"], x, p["g2"])
    if late_ffn_weights is not None:
        p.update(late_ffn_weights(x1))
    u, h2 = _norm_matmul(x1, p["g3"], p["w1"], 1408)
    x2, f = _ffn_out(u, p["w2"], x1, p["g4"])
    saved = dict(x=x, h=h, zm=zm, zg=zg, outs=outs, obraw3=obraw3, hstates=hstates, hd=hd, x1=x1, merged=merged, y=y, u=u, h2=h2, f=f)
    return x2, saved


def _att_bias_grad(db_ref, o_ref):
    r = lax.broadcasted_iota(jnp.int32, (ATT_PAIR, ATT_PAIR), 0)
    c = lax.broadcasted_iota(jnp.int32, (ATT_PAIR, ATT_PAIR), 1)
    flip = (r + c == ATT_PAIR - 1).astype(bf16)
    rows = []
    for h in range(NHEAD):
        d = jnp.concatenate([db_ref[h], jnp.zeros((ATT_PAIR, ATT_WV - ATT_BAND), f32)], axis=1)
        hi, lo = _split(d)
        rev = jnp.dot(flip, hi, preferred_element_type=f32) + jnp.dot(flip, lo, preferred_element_type=f32)
        lined = pltpu.roll(rev, ATT_WV - (ATT_PAIR - 1), 1, stride=1, stride_axis=0)
        rows.append(jnp.sum(lined, axis=0, keepdims=True))
    dwv = jnp.concatenate(rows + [jnp.zeros((8 - NHEAD, ATT_WV), f32)], axis=0)
    hi, lo = _split(dwv)
    m = _att_offset_map()
    dn = (((1,), (1,)), ((), ()))
    o_ref[...] = lax.dot_general(hi, m, dn, preferred_element_type=f32) + lax.dot_general(lo, m, dn, preferred_element_type=f32)


def _attn_bwd(zm, rb8, do):
    def body(q_ref, k_ref, v_ref, rb_ref, do_ref, dz_ref, drb_ref, kp_ref, vp_ref, bm_ref, dk_s, dv_s, db_s):
        _att_pad_kv(k_ref, v_ref, kp_ref, vp_ref)
        _att_bias_tiles(rb_ref, bm_ref)
        dk_s[...] = jnp.zeros_like(dk_s)
        dv_s[...] = jnp.zeros_like(dv_s)
        db_s[...] = jnp.zeros_like(db_s)
        hm = _head_masks()
        scale = HDIM ** -0.5

        def pair(p, carry):
            r0 = pl.multiple_of(p * ATT_PAIR, ATT_PAIR)
            q = q_ref[pl.ds(r0, ATT_PAIR), :] * scale
            dout = do_ref[pl.ds(r0, ATT_PAIR), :]
            kb = kp_ref[pl.ds(r0, ATT_BAND), :]
            vb = vp_ref[pl.ds(r0, ATT_BAND), :]
            key_ok = (lax.broadcasted_iota(jnp.int32, (1, ATT_BAND), 1) + (r0 - ATT_PAD)) >= 0
            dq = jnp.zeros((ATT_PAIR, MIXW), f32)
            dkb = jnp.zeros((ATT_BAND, MIXW), f32)
            dvb = jnp.zeros((ATT_BAND, MIXW), f32)
            for h in range(NHEAD):
                qm = jnp.where(hm[h], q, 0.0).astype(bf16)
                dom = jnp.where(hm[h], dout, 0.0).astype(bf16)
                p_h = _att_probs(qm, kb, bm_ref[h], key_ok)
                dp = _dot_nt(dom, vb)
                ds = p_h * (dp - jnp.sum(dp * p_h, axis=-1, keepdims=True))
                dsb = ds.astype(bf16)
                dq = dq + jnp.where(hm[h], _dot(dsb, kb), 0.0)
                dkb = dkb + _dot_tn(dsb, qm)
                dvb = dvb + _dot_tn(p_h, dom)
                db_s[h] = db_s[h] + ds
            dz_ref[pl.ds(r0, ATT_PAIR), 0:MIXW] = (dq * scale).astype(bf16)
            dk_s[pl.ds(r0, ATT_BAND), :] = dk_s[pl.ds(r0, ATT_BAND), :] + dkb
            dv_s[pl.ds(r0, ATT_BAND), :] = dv_s[pl.ds(r0, ATT_BAND), :] + dvb
            return carry

        lax.fori_loop(0, SEQ // ATT_PAIR, pair, 0)
        dz_ref[:, MIXW:2 * MIXW] = dk_s[pl.ds(ATT_PAD, SEQ), :].astype(bf16)
        dz_ref[:, 2 * MIXW:3 * MIXW] = dv_s[pl.ds(ATT_PAD, SEQ), :].astype(bf16)
        _att_bias_grad(db_s, drb_ref)

    col = lambda j: pl.BlockSpec((SEQ, MIXW), lambda i: (0, j))
    return pl.pallas_call(
        body, name="attn_bwd", grid=(1,),
        in_specs=[col(0), col(1), col(2), pl.BlockSpec((8, REL_SIZE), lambda i: (0, 0)), pl.BlockSpec((SEQ, MIXW), lambda i: (0, 0))],
        out_specs=[pl.BlockSpec((SEQ, 3 * MIXW), lambda i: (0, 0)), pl.BlockSpec((8, REL_SIZE), lambda i: (0, 0))],
        out_shape=[_sds((SEQ, 3 * MIXW), bf16), _sds((8, REL_SIZE), f32)],
        scratch_shapes=[pltpu.VMEM((SEQ + ATT_PAD, MIXW), bf16), pltpu.VMEM((SEQ + ATT_PAD, MIXW), bf16),
                        pltpu.VMEM((NHEAD, ATT_PAIR, ATT_BAND), f32),
                        pltpu.VMEM((SEQ + ATT_PAD, MIXW), f32), pltpu.VMEM((SEQ + ATT_PAD, MIXW), f32),
                        pltpu.VMEM((NHEAD, ATT_PAIR, ATT_BAND), f32)],
        compiler_params=_params(("arbitrary",)),
    )(zm, zm, zm, rb8, do)


def _hgrn_out_bwd(zm3, ng, oraw3, do3):
    def body(g_ref, ng_ref, o_ref, do_ref, dor_ref, dg_ref, dng_ref):
        hm = _same_head(MIXW, HDIM, bf16)
        ngv = ng_ref[...]
        dng = jnp.zeros((1, MIXW), f32)
        for t in range(HG_T):
            o, g, d = o_ref[:, t, :], g_ref[:, t, :], do_ref[:, t, :]
            rs = lax.rsqrt(_dot_hl(o * o, hm) * (1.0 / HDIM) + EPS)
            y1 = o * rs
            dy2 = d * _silu(g)
            dg_ref[:, t, :] = (d * y1 * ngv * _dsilu(g)).astype(bf16)
            dng = dng + jnp.sum(dy2 * y1, axis=0, keepdims=True)
            dy1 = dy2 * ngv
            dor_ref[:, t, :] = rs * (dy1 - y1 * (_dot_hl(dy1 * y1, hm) * (1.0 / HDIM)))
        dng_ref[...] = jnp.broadcast_to(dng, (8, MIXW))

    blk = pl.BlockSpec((HG_N, HG_T, MIXW), lambda i: (0, 0, 0))
    return pl.pallas_call(
        body, name="hgrn_out_bwd", grid=(1,),
        in_specs=[pl.BlockSpec((HG_N, HG_T, MIXW), lambda i: (0, 0, 6)), pl.BlockSpec((1, MIXW), lambda i: (0, 0)), blk, blk],
        out_specs=[blk, blk, pl.BlockSpec((8, MIXW), lambda i: (0, 0))],
        out_shape=[_sds((HG_N, HG_T, MIXW), f32), _sds((HG_N, HG_T, MIXW), bf16), _sds((8, MIXW), f32)],
        compiler_params=_params(("arbitrary",)),
    )(zm3, ng, oraw3, do3)


def _hgrn_bwd(zm3, lb, dor3, states):
    def body(q_ref, f_ref, i_ref, lb_ref, dor_ref, st_s, dz_ref, dlb_ref,
             qf_s, kf_s, b_s, dq_s, dk_s, db_s, dv_s, w_s, x_s, cur_s):
        lb = lb_ref[...]
        hm = _same_head(MIXW, HDIM, bf16)
        hmf = _same_head(MIXW, HDIM, f32)
        b = None
        for t in range(HG_T):
            qf, kf, lf, _, _, _ = _hg_gates(q_ref[:, t, :], f_ref[:, t, :], lb)
            b = lf if b is None else b + lf
            qf_s[:, t, :] = qf
            kf_s[:, t, :] = kf
            b_s[:, t, :] = b

        def block_terms(n):
            bn = b_s[n]
            bl = bn[HG_T - 1:HG_T]
            eb = jnp.exp(bn)
            ek = jnp.exp(bl - bn)
            return qf_s[n] * eb, kf_s[n] * ek, jnp.exp(bl), eb, ek

        cur_s[...] = jnp.zeros((MIXW, MIXW), f32)
        last = lax.broadcasted_iota(jnp.int32, (HG_T, 1), 0) == HG_T - 1

        def bwd_step(j, carry):
            n = HG_N - 1 - j
            qd, kd, dec, eb, ek = block_terms(n)
            v, do_n = i_ref[n], dor_ref[n]
            dst = cur_s[...]
            st = st_s[n]
            dqd = _dot(do_n, st)
            dkd = _dot(v, dst)
            ddec = jnp.sum(dst * st.astype(f32), axis=0, keepdims=True)
            cur_s[...] = dst * dec + _dot_tn(do_n, qd) * hmf
            dq_s[n] = dqd * eb
            dk_s[n] = dkd * ek
            dv_s[n] = _dot_nt(kd, dst)
            dbl = jnp.sum(dkd * kd, axis=0, keepdims=True) + ddec * dec
            db_s[n] = dqd * qd - dkd * kd + jnp.where(last, dbl, 0.0)
            return carry

        lax.fori_loop(0, HG_N, bwd_step, 0, unroll=2)
        for t in range(HG_T):
            qt, bt, dot_t = qf_s[:, t, :], b_s[:, t, :], dor_ref[:, t, :]
            for s in range(t + 1):
                w = qt * kf_s[:, s, :]
                if s < t:
                    w = w * jnp.exp(bt - b_s[:, s, :])
                w_s[pl.ds(s * HG_N, HG_N), :] = w.astype(bf16)
                x_s[pl.ds(s * HG_N, HG_N), :] = (dot_t * i_ref[:, s, :]).astype(bf16)
            p = jnp.dot(w_s[pl.ds(0, (t + 1) * HG_N), :], hm, preferred_element_type=f32)
            dp = jnp.dot(x_s[pl.ds(0, (t + 1) * HG_N), :], hm, preferred_element_type=f32)
            dq_t = jnp.zeros((HG_N, MIXW), f32)
            db_t = jnp.zeros((HG_N, MIXW), f32)
            for s in range(t + 1):
                ps = p[s * HG_N:(s + 1) * HG_N]
                dps = dp[s * HG_N:(s + 1) * HG_N]
                ks = kf_s[:, s, :]
                dv_s[:, s, :] = dv_s[:, s, :] + ps * dot_t
                if s < t:
                    dec_ts = jnp.exp(bt - b_s[:, s, :])
                    g1 = dps * ks * dec_ts
                    dk_s[:, s, :] = dk_s[:, s, :] + dps * qt * dec_ts
                    gw = g1 * qt
                    db_t = db_t + gw
                    db_s[:, s, :] = db_s[:, s, :] - gw
                else:
                    g1 = dps * ks
                    dk_s[:, s, :] = dk_s[:, s, :] + dps * qt
                dq_t = dq_t + g1
            dq_s[:, t, :] = dq_s[:, t, :] + dq_t
            db_s[:, t, :] = db_s[:, t, :] + db_t
        run = jnp.zeros((HG_N, MIXW), f32)
        dlb = jnp.zeros((1, MIXW), f32)
        oml = 1.0 - lb
        for t in range(HG_T - 1, -1, -1):
            run = run + db_s[:, t, :]
            q = q_ref[:, t, :]
            _, _, _, sq, sg, f = _hg_gates(q, f_ref[:, t, :], lb)
            dkf = dk_s[:, t, :]
            df = jnp.where(f > LOG_FLOOR, run / f, 0.0)
            dsg = (df - dkf) * oml
            dlb = dlb + jnp.sum((df - dkf) * (1.0 - sg), axis=0, keepdims=True)
            dz_ref[:, t, 0:MIXW] = (dq_s[:, t, :] * sq * (1.0 + q * (1.0 - sq))).astype(bf16)
            dz_ref[:, t, MIXW:2 * MIXW] = (dsg * sg * (1.0 - sg)).astype(bf16)
            dz_ref[:, t, 2 * MIXW:3 * MIXW] = dv_s[:, t, :].astype(bf16)
        dlb_ref[...] = jnp.broadcast_to(dlb, (8, MIXW))

    one = pl.Buffered(1)
    col = lambda j: pl.BlockSpec((HG_N, HG_T, MIXW), lambda i: (0, 0, j), pipeline_mode=one)
    s3 = pltpu.VMEM((HG_N, HG_T, MIXW), f32)
    return pl.pallas_call(
        body, name="hgrn_bwd", grid=(1,),
        in_specs=[col(3), col(4), col(5), pl.BlockSpec((1, MIXW), lambda i: (0, 0)),
                  pl.BlockSpec((HG_N, HG_T, MIXW), lambda i: (0, 0, 0), pipeline_mode=one),
                  pl.BlockSpec((HG_N, MIXW, MIXW), lambda i: (0, 0, 0), pipeline_mode=one)],
        out_specs=[pl.BlockSpec((HG_N, HG_T, 3 * MIXW), lambda i: (0, 0, 0)), pl.BlockSpec((8, MIXW), lambda i: (0, 0))],
        out_shape=[_sds((HG_N, HG_T, 3 * MIXW), bf16), _sds((8, MIXW), f32)],
        scratch_shapes=[s3, s3, s3, s3, s3, s3, s3,
                        pltpu.VMEM((HG_T * HG_N, MIXW), bf16), pltpu.VMEM((HG_T * HG_N, MIXW), bf16),
                        pltpu.VMEM((MIXW, MIXW), f32)],
        compiler_params=_params(("arbitrary",)),
    )(zm3, zm3, zm3, lb, dor3, states)


def _gmlp_bwd(zm, ng, ws, bs8, do):
    def body(u_ref, v_ref, ng_ref, ws_ref, bs_ref, do_ref, dz_ref, dws_ref, dng_ref, dbs_ref, dm_s):
        hm = _head_masks()
        tril, wts = _gm_weights(ws_ref)
        bias = _gm_bias(bs_ref)
        ngv = ng_ref[...]
        dws_ref[...] = jnp.zeros_like(dws_ref)
        dm_s[...] = jnp.zeros_like(dm_s)

        def blk(n, dng):
            rows = pl.ds(pl.multiple_of(n * GM_T, GM_T), GM_T)
            cu, cv, d = u_ref[rows, :], v_ref[rows, :], do_ref[rows, :]
            v = _gelu(cv)
            r = lax.rsqrt(jnp.mean(v * v, axis=-1, keepdims=True) + EPS)
            vh = v * r
            vn = vh * ngv
            u = _gelu(cu)
            dm = d * u
            dmb, vnb = dm.astype(bf16), vn.astype(bf16)
            dvn = jnp.zeros((GM_T, MIXW), f32)
            for g in range(NHEAD):
                dws_ref[g] = dws_ref[g] + _dot_nt(jnp.where(hm[g], dm, 0.0), vnb)
                dvn = dvn + jnp.where(hm[g], _dot_tn(wts[g], dmb), 0.0)
            dm_s[...] = dm_s[...] + dm
            dvh = dvn * ngv
            dv = r * (dvh - vh * jnp.mean(dvh * vh, axis=-1, keepdims=True))
            dz_ref[rows, 0:MIXW] = (d * _gm_mixed(vn, wts, bias, hm) * _dgelu(cu)).astype(bf16)
            dz_ref[rows, MIXW:2 * MIXW] = (dv * _dgelu(cv)).astype(bf16)
            return dng + jnp.sum(dvn * vh, axis=0, keepdims=True)

        dng = lax.fori_loop(0, SEQ // GM_T, blk, jnp.zeros((1, MIXW), f32))
        dng_ref[...] = jnp.broadcast_to(dng, (8, MIXW))
        for g in range(NHEAD):
            dws_ref[g] = jnp.where(tril, dws_ref[g], 0.0)
        dbs_ref[...] = _dot_nt_hl(_gm_expand(), dm_s[...])

    col = lambda j: pl.BlockSpec((SEQ, MIXW), lambda i: (0, j))
    return pl.pallas_call(
        body, name="gmlp_bwd", grid=(1,),
        in_specs=[col(7), col(8), pl.BlockSpec((1, MIXW), lambda i: (0, 0)),
                  pl.BlockSpec((NHEAD, GM_T, GM_T), lambda i: (0, 0, 0)), pl.BlockSpec((8, GM_T), lambda i: (0, 0)),
                  pl.BlockSpec((SEQ, MIXW), lambda i: (0, 0))],
        out_specs=[pl.BlockSpec((SEQ, 2 * MIXW), lambda i: (0, 0)), pl.BlockSpec((NHEAD, GM_T, GM_T), lambda i: (0, 0, 0)),
                   pl.BlockSpec((8, MIXW), lambda i: (0, 0)), pl.BlockSpec((8, GM_T), lambda i: (0, 0))],
        out_shape=[_sds((SEQ, 2 * MIXW), bf16), _sds((NHEAD, GM_T, GM_T), f32), _sds((8, MIXW), f32), _sds((8, GM_T), f32)],
        scratch_shapes=[pltpu.VMEM((GM_T, MIXW), f32)],
        compiler_params=_params(("arbitrary",)),
    )(zm, zm, ng, ws, bs8, do)


def _lru_bwd(zm, cw8, cb, wa, ba, wx, bx, lam, hd, do):
    nchunk = SEQ // LRU_T

    def body(x_ref, g_ref, cw_ref, cb_ref, wa_ref, ba_ref, wx_ref, bx_ref, lam_ref, h_ref, do_ref,
             dz_ref, dwa_ref, dwx_ref, dcw_ref, dvec_ref, xp_s, xc_s, dxc_s):
        _lru_conv(x_ref, cw_ref, cb_ref, xp_s, xc_s)
        lam_v = lam_ref[...]
        sp = jax.nn.softplus(-lam_v)
        sgl = _sigmoid(-lam_v)
        wa_v, wx_v, ba_v, bx_v = wa_ref[...], wx_ref[...], ba_ref[...], bx_ref[...]
        dwa_ref[...] = jnp.zeros_like(dwa_ref)
        dwx_ref[...] = jnp.zeros_like(dwx_ref)
        dxc_s[pl.ds(SEQ, 8), :] = jnp.zeros((8, MIXW), f32)
        row = lax.broadcasted_iota(jnp.int32, (LRU_T, 1), 0)
        zero = jnp.zeros((1, MIXW), f32)

        def chunk(j, carry):
            dh_next, a_next, dba, dbx, dlam = carry
            c = nchunk - 1 - j
            rows = pl.ds(pl.multiple_of(c * LRU_T, LRU_T), LRU_T)
            prev = pl.ds(pl.multiple_of(jnp.maximum(c - 1, 0) * LRU_T, LRU_T), LRU_T)
            first = (row + c * LRU_T) == 0
            xc, gate, d, h = xc_s[rows, :], g_ref[rows, :], do_ref[rows, :], h_ref[rows, :]
            a, mult, r, ig, m2 = _lru_gates(xc, wa_v, ba_v, wx_v, bx_v, sp, first)
            h_last = jnp.where(c > 0, h_ref[prev, :][LRU_T - 1:LRU_T, :], 0.0)
            h_m1 = jnp.where(row == 0, h_last, pltpu.roll(h, 1, 0))
            a_up = jnp.where(row == LRU_T - 1, a_next, pltpu.roll(a, LRU_T - 1, 0))
            acum, dh_loc = _lru_scan(a_up, d * _gelu(gate), True)
            dh = dh_loc + acum * dh_next
            dmult = jnp.where(first, 0.0, dh * (ig * xc))
            msq = jnp.sqrt(jnp.maximum(m2, 0.0))
            dla = dh * h_m1 * a + jnp.where(m2 > 0.0, -dmult * (1.0 - m2) / msq, 0.0)
            dpr = dla * (-LRU_C) * sp * r * (1.0 - r)
            dpi = dh * mult * xc * ig * (1.0 - ig)
            dxc_s[rows, :] = dh * mult * ig + _dot_nt(dpr, wa_v) + _dot_nt(dpi, wx_v)
            dwa_ref[...] = dwa_ref[...] + _dot_tn(xc, dpr)
            dwx_ref[...] = dwx_ref[...] + _dot_tn(xc, dpi)
            dz_ref[rows, MIXW:2 * MIXW] = (d * h * _dgelu(gate)).astype(bf16)
            return (dh[0:1], a[0:1], dba + jnp.sum(dpr, axis=0, keepdims=True), dbx + jnp.sum(dpi, axis=0, keepdims=True),
                    dlam + jnp.sum(dla * r, axis=0, keepdims=True) * (LRU_C * sgl))

        _, _, dba, dbx, dlam = lax.fori_loop(0, nchunk, chunk, (zero, zero, zero, zero, zero))
        cw = cw_ref[...]
        dxc = dxc_s[pl.ds(0, SEQ), :]
        dx = dxc * cw[3:4]
        dcw = [None] * 4
        dcw[3] = jnp.sum(dxc * x_ref[...], axis=0, keepdims=True)
        for k in range(1, 4):
            dx = dx + dxc_s[pl.ds(k, SEQ), :] * cw[3 - k:4 - k]
            dcw[3 - k] = jnp.sum(dxc * xp_s[pl.ds(8 - k, SEQ), :], axis=0, keepdims=True)
        dz_ref[:, 0:MIXW] = dx.astype(bf16)
        dcw_ref[...] = jnp.concatenate(dcw + [jnp.zeros((4, MIXW), f32)], axis=0)
        dvec_ref[...] = jnp.concatenate([jnp.sum(dxc, axis=0, keepdims=True), dba, dbx, dlam, jnp.zeros((4, MIXW), f32)], axis=0)

    col = lambda j: pl.BlockSpec((SEQ, MIXW), lambda i: (0, j))
    vec = pl.BlockSpec((1, MIXW), lambda i: (0, 0))
    vec8 = pl.BlockSpec((8, MIXW), lambda i: (0, 0))
    mat = pl.BlockSpec((MIXW, MIXW), lambda i: (0, 0))
    full = pl.BlockSpec((SEQ, MIXW), lambda i: (0, 0))
    return pl.pallas_call(
        body, name="lru_bwd", grid=(1,),
        in_specs=[col(9), col(10), vec8, vec, mat, vec, mat, vec, vec, full, full],
        out_specs=[pl.BlockSpec((SEQ, 2 * MIXW), lambda i: (0, 0)), mat, mat, vec8, vec8],
        out_shape=[_sds((SEQ, 2 * MIXW), bf16), _sds((MIXW, MIXW), f32), _sds((MIXW, MIXW), f32),
                   _sds((8, MIXW), f32), _sds((8, MIXW), f32)],
        scratch_shapes=[pltpu.VMEM((SEQ + 8, MIXW), f32), pltpu.VMEM((SEQ, MIXW), f32), pltpu.VMEM((SEQ + 8, MIXW), f32)],
        compiler_params=_params(("arbitrary",)),
    )(zm, zm, cw8, cb, wa, ba, wx, bx, lam, hd, do)


def _matmul_tn(a, b, tm, tn, b_col0=0):
    m = a.shape[1]
    n = tn if b_col0 else b.shape[1]
    off = b_col0 // tn

    def body(a_ref, b_ref, o_ref, at_s):
        @pl.when(pl.program_id(1) == 0)
        def _():
            at_s[...] = a_ref[...].T

        o_ref[...] = jnp.dot(at_s[...], b_ref[...], preferred_element_type=f32).astype(bf16)

    return pl.pallas_call(
        body, name="matmul_tn", grid=(m // tm, n // tn),
        in_specs=[pl.BlockSpec((SEQ, tm), lambda i, j: (0, i)), pl.BlockSpec((SEQ, tn), lambda i, j: (0, j + off))],
        out_specs=pl.BlockSpec((tm, tn), lambda i, j: (i, j)),
        out_shape=_sds((m, n), bf16),
        scratch_shapes=[pltpu.VMEM((tm, SEQ), bf16)],
        compiler_params=_params(("parallel", "arbitrary")),
    )(a, b)


def _matmul_nt_norm(pairs, x, g, dres):
    tm = 1024
    steps = [a.shape[1] // t for a, _, t in pairs]
    starts = [sum(steps[:i]) for i in range(len(pairs))]
    total = sum(steps)
    npair = len(pairs)

    def body(*refs):
        a_refs, w_refs = refs[0:2 * npair:2], refs[1:2 * npair:2]
        x_ref, g_ref, dres_ref, dx_ref, dg_ref, acc_s = refs[2 * npair:]
        i, k = pl.program_id(0), pl.program_id(1)

        @pl.when(k == 0)
        def _():
            acc_s[...] = jnp.zeros_like(acc_s)

        @pl.when((i == 0) & (k == 0))
        def _():
            dg_ref[...] = jnp.zeros_like(dg_ref)

        for q in range(npair):
            @pl.when((k >= starts[q]) & (k < starts[q] + steps[q]))
            def _(q=q):
                acc_s[...] += _dot_nt(a_refs[q][...], w_refs[q][...])

        @pl.when(k == total - 1)
        def _():
            dx, dg = _rms_bwd(x_ref[...], g_ref[...], acc_s[...])
            dx_ref[...] = dres_ref[...] + dx
            dg_ref[...] += dg

    in_specs, args = [], []
    for q, (a, w, t) in enumerate(pairs):
        kmap = lambda k, q=q: jnp.clip(k - starts[q], 0, steps[q] - 1)
        in_specs += [pl.BlockSpec((tm, t), lambda i, k, kmap=kmap: (i, kmap(k))),
                     pl.BlockSpec((DM, t), lambda i, k, kmap=kmap: (0, kmap(k)))]
        args += [a, w]
    row = pl.BlockSpec((tm, DM), lambda i, k: (i, 0))
    vec = pl.BlockSpec((1, DM), lambda i, k: (0, 0))
    return pl.pallas_call(
        body, name="matmul_nt_norm", grid=(SEQ // tm, total),
        in_specs=in_specs + [row, vec, row], out_specs=[row, vec],
        out_shape=[_sds((SEQ, DM), f32), _sds((1, DM), f32)],
        scratch_shapes=[pltpu.VMEM((tm, DM), f32)],
        compiler_params=_params(("arbitrary", "arbitrary")),
    )(*args, x, g, dres)


def _merge_bwd(dx1, y, g2, outs, zg, wb, wo):
    def body(dx_ref, y_ref, g_ref, oa_ref, ob_ref, oc_ref, od_ref, zg_ref, wb_ref, wo_ref,
             da_ref, db_ref, dc_ref, dd_ref, dzg_ref, dpj_ref, dy_ref, dg_ref):
        @pl.when(pl.program_id(0) == 0)
        def _():
            dg_ref[...] = jnp.zeros_like(dg_ref)

        dy, dg = _rms_bwd(y_ref[...], g_ref[...], dx_ref[...])
        dg_ref[...] += dg
        dyb = dy.astype(bf16)
        dy_ref[...] = dyb
        dmerged = _dot_nt(dyb, wo_ref[...])
        for n, (o_ref, do_ref) in enumerate(((oa_ref, da_ref), (ob_ref, db_ref), (oc_ref, dc_ref), (od_ref, dd_ref))):
            cols = slice(n * DM, (n + 1) * DM)
            gate = _sigmoid(zg_ref[:, cols])
            proj = jnp.dot(o_ref[...], wb_ref[n], preferred_element_type=f32)
            dproj = (dmerged * gate).astype(bf16)
            dpj_ref[:, cols] = dproj
            dzg_ref[:, cols] = (dmerged * proj * gate * (1.0 - gate)).astype(bf16)
            do_ref[...] = _dot_nt(dproj, wb_ref[n])

    row = lambda w: pl.BlockSpec((ROW_TILE, w), lambda i: (i, 0))
    vec = pl.BlockSpec((1, DM), lambda i: (0, 0))
    return pl.pallas_call(
        body, name="merge_bwd", grid=(SEQ // ROW_TILE,),
        in_specs=[row(DM), row(DM), vec] + [row(MIXW)] * 4 + [row(NGATE), pl.BlockSpec((NHEAD, MIXW, DM), lambda i: (0, 0, 0)),
                                                              pl.BlockSpec((DM, DM), lambda i: (0, 0))],
        out_specs=[row(MIXW)] * 4 + [row(NGATE), row(NGATE), row(DM), vec],
        out_shape=[_sds((SEQ, MIXW), f32)] * 4 + [_sds((SEQ, NGATE), bf16), _sds((SEQ, NGATE), bf16), _sds((SEQ, DM), bf16),
                                                  _sds((1, DM), f32)],
        compiler_params=_params(("arbitrary",)),
    )(dx1, y, g2, *outs, zg, wb, wo)


def _ffn_bwd(dx2, f, g4, u, w2):
    def body(dx_ref, f_ref, g_ref, u_ref, w_ref, du_ref, a_ref, df_ref, dg_ref):
        @pl.when(pl.program_id(0) == 0)
        def _():
            dg_ref[...] = jnp.zeros_like(dg_ref)

        df, dg = _rms_bwd(f_ref[...], g_ref[...], dx_ref[...])
        dg_ref[...] += dg
        dfb = df.astype(bf16)
        df_ref[...] = dfb
        da = _dot_nt(dfb, w_ref[...])
        gt, up = u_ref[:, :FFH], u_ref[:, FFH:]
        a_ref[...] = (_silu(gt) * up).astype(bf16)
        du_ref[:, :FFH] = (da * up * _dsilu(gt)).astype(bf16)
        du_ref[:, FFH:] = (da * _silu(gt)).astype(bf16)

    row = lambda w: pl.BlockSpec((ROW_TILE, w), lambda i: (i, 0))
    vec = pl.BlockSpec((1, DM), lambda i: (0, 0))
    return pl.pallas_call(
        body, name="ffn_bwd", grid=(SEQ // ROW_TILE,),
        in_specs=[row(DM), row(DM), vec, row(2 * FFH), pl.BlockSpec((FFH, DM), lambda i: (0, 0))],
        out_specs=[row(2 * FFH), row(FFH), row(DM), vec],
        out_shape=[_sds((SEQ, 2 * FFH), bf16), _sds((SEQ, FFH), bf16), _sds((SEQ, DM), bf16), _sds((1, DM), f32)],
        compiler_params=_params(("arbitrary",)),
    )(dx2, f, g4, u, w2)


def _layer_bwd(dx2, p, sv, ffn_grads_ready=None, mix_grads_ready=None):
    du, act, df, dg4 = _ffn_bwd(dx2, sv["f"], p["g4"], sv["u"], p["w2"])
    dw2 = _matmul_tn(act, df, 1408, DM)
    dx1, dg3 = _matmul_nt_norm([(du, p["w1"], 1408)], sv["x1"], p["g3"], dx2)
    dw1 = _matmul_tn(sv["h2"], du, DM, 1408)
    g2 = p["g2"]
    if ffn_grads_ready is not None:
        g2 = g2 + ffn_grads_ready(dict(w_ffn_in=dw1, w_ffn_out=dw2), dx1)
    *dos, dzg, dproj, dy, dg2 = _merge_bwd(dx1, sv["y"], g2, sv["outs"], sv["zg"], p["wb"], p["wo"])
    dwo = _matmul_tn(sv["merged"], dy, DM, DM)
    dwb = jnp.stack([_matmul_tn(sv["outs"][n], dproj, MIXW, DM, b_col0=n * DM) if n else
                     _matmul_tn(sv["outs"][0], dproj[:, :DM], MIXW, DM) for n in range(NHEAD)])
    zm = sv["zm"]
    zm3 = zm.reshape(HG_N, HG_T, NMIX)
    dza, drb = _attn_bwd(zm, p["rb8"], dos[0])
    dor, dgb, dhng = _hgrn_out_bwd(zm3, p["hng"], sv["obraw3"], dos[1].reshape(HG_N, HG_T, MIXW))
    dzb, dlb = _hgrn_bwd(zm3, p["lb"], dor, sv["hstates"])
    dzc, dws, dgng, dbs = _gmlp_bwd(zm, p["gng"], p["gws"], p["gbs8"], dos[2])
    dzd, dwa, dwx, dcw, dvec = _lru_bwd(zm, p["cw8"], p["cb"], p["wa"], p["ba"], p["wx"], p["bx"], p["lam"], sv["hd"], dos[3])
    dzm = jnp.concatenate([dza, dzb.reshape(SEQ, 3 * MIXW), dgb.reshape(SEQ, MIXW), dzc, dzd], axis=1)
    dwin = jnp.concatenate([_matmul_tn(sv["h"], dzm, DM, 1408), _matmul_tn(sv["h"], dzg, DM, 1024)], axis=1)
    big = dict(w_in=dwin, w_branch=dwb, w_out=dwo, w_ffn_in=dw1, w_ffn_out=dw2)
    g1 = p["g1"]
    if mix_grads_ready is not None:
        g1 = g1 + mix_grads_ready(big)
    dx0, dg1 = _matmul_nt_norm([(dzm, p["wm"], 1408), (dzg, p["wgt"], 1024)], sv["x"], g1, dx1)
    small = dict(
        norm_mix_pre=dg1[0], norm_mix_post=dg2[0], norm_ffn_pre=dg3[0], norm_ffn_post=dg4[0],
        attn_rel_bias=drb[:NHEAD], lb=dlb[0], hgrn_norm_g=dhng[0], gmlp_norm_g=dgng[0], gmlp_ws=dws, gmlp_bs=dbs[:NHEAD],
        lru_conv_w=dcw[:NHEAD], lru_conv_b=dvec[0], lru_wa=_diag_blocks(dwa), lru_ba=dvec[1], lru_wx=_diag_blocks(dwx),
        lru_bx=dvec[2], lru_lambda=dvec[3],
    )
    return dx0, big, small


MIX_BIG = ("w_in", "w_branch", "w_out")
FFN_BIG = ("w_ffn_in", "w_ffn_out")
BIG = MIX_BIG + FFN_BIG
SMALL = ("norm_mix_pre", "norm_mix_post", "norm_ffn_pre", "norm_ffn_post", "attn_rel_bias", "hgrn_lb_logits", "hgrn_norm_g",
         "gmlp_norm_g", "gmlp_ws", "gmlp_bs", "lru_conv_w", "lru_conv_b", "lru_wa", "lru_ba", "lru_wx", "lru_bx", "lru_lambda")


def _local_step(x, tgt, full, small):
    lbs = _lb_fwd(small["hgrn_lb_logits"])
    params, saved = [], []
    for l in range(DEPTH):
        p = _layer_params(l, {k: full[k][l] for k in BIG}, small, lbs)
        x, sv = _layer_fwd(x, p)
        params.append(p)
        saved.append(sv)
    loss, dx = _loss_head(x, tgt)
    bigs, smalls = [None] * DEPTH, [None] * DEPTH
    for l in range(DEPTH - 1, -1, -1):
        dx, bigs[l], smalls[l] = _layer_bwd(dx, params[l], saved[l])
    gbig = {k: jnp.stack([bigs[l][k] for l in range(DEPTH)]) for k in BIG}
    gsmall = {k: jnp.stack([smalls[l][k] for l in range(DEPTH)]) for k in smalls[0]}
    gsmall["hgrn_lb_logits"] = _lb_bwd(small["hgrn_lb_logits"], gsmall.pop("lb"))
    return loss, dx, gbig, gsmall


HBM_ANY = pl.BlockSpec(memory_space=pl.ANY)


def _mesh_pos():
    return lax.axis_index("x"), lax.axis_index("y"), lax.axis_index("c")


def _all_gather(x, name):
    def body(x_ref, out_ref, send_sems, recv_sems, local_sem):
        ax, ay, ac = _mesh_pos()
        me, sibling = (ax, ay, ac), (ax, ay, 1 - ac)
        chips = [(1 - ax, ay), (ax, 1 - ay), (1 - ax, 1 - ay)]

        def slot(px, py, pc):
            return out_ref.at[4 * px + 2 * py + pc]

        def copy(k, block, to, src=None):
            return pltpu.make_async_remote_copy(
                src_ref=slot(*block) if src is None else src, dst_ref=slot(*block),
                send_sem=send_sems.at[k], recv_sem=recv_sems.at[k], device_id=to, device_id_type=MESH_ID)

        mine = pltpu.make_async_copy(x_ref, slot(*me), local_sem)
        mine.start()
        first = [copy(0, me, sibling, src=x_ref)]
        first += [copy(1 + j, me, (*chip, ac), src=x_ref) for j, chip in enumerate(chips)]
        for cp in first:
            cp.start()
        passed = [copy(4 + j, (*chip, ac), sibling) for j, chip in enumerate(chips)]
        for j, chip in enumerate(chips):
            copy(1 + j, (*chip, ac), me).wait_recv()
            passed[j].start()
        copy(0, sibling, me).wait_recv()
        for j, chip in enumerate(chips):
            copy(4 + j, (*chip, 1 - ac), me).wait_recv()
        for cp in first + passed:
            cp.wait_send()
        mine.wait()

    return pl.pallas_call(
        body, name=name, out_shape=_sds((NDEV,) + x.shape, x.dtype),
        in_specs=[HBM_ANY], out_specs=HBM_ANY,
        scratch_shapes=[pltpu.SemaphoreType.DMA((7,)), pltpu.SemaphoreType.DMA((7,)), pltpu.SemaphoreType.DMA],
    )(x)


def _exchange(g, name):
    def body(g_ref, out_ref, send_sems, recv_sems, local_sem):
        ax, ay, ac = _mesh_pos()
        me = 4 * ax + 2 * ay + ac
        mine = pltpu.make_async_copy(g_ref.at[me], out_ref.at[me], local_sem)
        mine.start()
        copies = []
        for k in range(1, NDEV):
            px = 1 - ax if k & 4 else ax
            py = 1 - ay if k & 2 else ay
            pc = 1 - ac if k & 1 else ac
            copies.append(pltpu.make_async_remote_copy(
                src_ref=g_ref.at[4 * px + 2 * py + pc], dst_ref=out_ref.at[me],
                send_sem=send_sems.at[k - 1], recv_sem=recv_sems.at[k - 1], device_id=(px, py, pc), device_id_type=MESH_ID))
        for cp in copies:
            cp.start()
        for cp in copies:
            cp.wait()
        mine.wait()

    return pl.pallas_call(
        body, name=name, out_shape=_sds(g.shape, g.dtype),
        in_specs=[HBM_ANY], out_specs=HBM_ANY,
        scratch_shapes=[pltpu.SemaphoreType.DMA((7,)), pltpu.SemaphoreType.DMA((7,)), pltpu.SemaphoreType.DMA],
    )(g)


def _peer(ax, ay, ac, k):
    return (1 - ax if k & 4 else ax, 1 - ay if k & 2 else ay, 1 - ac if k & 1 else ac)


def _handshake(peers):
    barrier = pltpu.get_barrier_semaphore()
    for peer in peers:
        pl.semaphore_signal(barrier, inc=1, device_id=peer, device_id_type=MESH_ID)
    pl.semaphore_wait(barrier, len(peers))


SEQUENCER = dict(axis_name="seq", num_cores=1)
GATHER_ID = 1
EXCHANGE_ID = 2


def _gather_sc(xs, name):
    n = len(xs)

    def body(*refs):
        srcs, outs = refs[:n], refs[n:2 * n]
        send_sems, recv_sems, local_sems = refs[2 * n:]
        ax, ay, ac = _mesh_pos()
        me, sibling = (ax, ay, ac), (ax, ay, 1 - ac)
        chips = [(1 - ax, ay), (ax, 1 - ay), (1 - ax, 1 - ay)]
        _handshake([sibling] + [(*chip, ac) for chip in chips])

        def slot(i, px, py, pc):
            return outs[i].at[4 * px + 2 * py + pc]

        def copy(i, k, block, to, src=None):
            return pltpu.make_async_remote_copy(
                src_ref=slot(i, *block) if src is None else src, dst_ref=slot(i, *block),
                send_sem=send_sems.at[7 * i + k], recv_sem=recv_sems.at[7 * i + k], device_id=to, device_id_type=MESH_ID)

        mine = [pltpu.make_async_copy(srcs[i], slot(i, *me), local_sems.at[i]) for i in range(n)]
        first = []
        for i in range(n):
            first += [copy(i, 1 + j, me, (*chip, ac), src=srcs[i]) for j, chip in enumerate(chips)]
        for i in range(n):
            first += [copy(i, 0, me, sibling, src=srcs[i])]
        for cp in first + mine:
            cp.start()
        passed = []
        for i in range(n):
            for j, chip in enumerate(chips):
                copy(i, 1 + j, (*chip, ac), me).wait_recv()
                passed.append(copy(i, 4 + j, (*chip, ac), sibling))
                passed[-1].start()
        for i in range(n):
            copy(i, 0, sibling, me).wait_recv()
            for j, chip in enumerate(chips):
                copy(i, 4 + j, (*chip, 1 - ac), me).wait_recv()
        for cp in first + passed:
            cp.wait_send()
        for cp in mine:
            cp.wait()

    return pl.kernel(
        body, name=name, out_type=[_sds((NDEV,) + x.shape, x.dtype) for x in xs],
        mesh=plsc.ScalarSubcoreMesh(**SEQUENCER),
        scratch_types=[pltpu.SemaphoreType.DMA((7 * n,)), pltpu.SemaphoreType.DMA((7 * n,)), pltpu.SemaphoreType.DMA((n,))],
        compiler_params=pltpu.CompilerParams(collective_id=GATHER_ID),
    )(*xs)


def _exchange_sc(gs, name):
    n = len(gs)

    def body(*refs):
        srcs, outs = refs[:n], refs[n:2 * n]
        send_sems, recv_sems, local_sems = refs[2 * n:]
        ax, ay, ac = _mesh_pos()
        me = 4 * ax + 2 * ay + ac
        peers = [_peer(ax, ay, ac, k) for k in range(1, NDEV)]
        _handshake(peers)
        mine = [pltpu.make_async_copy(srcs[i].at[me], outs[i].at[me], local_sems.at[i]) for i in range(n)]
        copies = []
        for i in range(n):
            for k, (px, py, pc) in enumerate(peers):
                copies.append(pltpu.make_async_remote_copy(
                    src_ref=srcs[i].at[4 * px + 2 * py + pc], dst_ref=outs[i].at[me],
                    send_sem=send_sems.at[7 * i + k], recv_sem=recv_sems.at[7 * i + k],
                    device_id=(px, py, pc), device_id_type=MESH_ID))
        for cp in copies + mine:
            cp.start()
        for cp in copies + mine:
            cp.wait()

    return pl.kernel(
        body, name=name, out_type=[_sds(g.shape, g.dtype) for g in gs],
        mesh=plsc.ScalarSubcoreMesh(**SEQUENCER),
        scratch_types=[pltpu.SemaphoreType.DMA((7 * n,)), pltpu.SemaphoreType.DMA((7 * n,)), pltpu.SemaphoreType.DMA((n,))],
        compiler_params=pltpu.CompilerParams(collective_id=EXCHANGE_ID),
    )(*gs)


HBM_SPEC = pl.BlockSpec(memory_space=pltpu.HBM)
SEM_SPEC = pl.BlockSpec(memory_space=pltpu.SEMAPHORE)
DATAFLOW = pltpu.SideEffectType.DATAFLOW_SIDE_EFFECTING


def _exchange_copies(srcs, lands, send_sems, recv_sems, local_sems):
    n = len(srcs)
    ax, ay, ac = _mesh_pos()
    me = 4 * ax + 2 * ay + ac
    copies = [pltpu.make_async_copy(srcs[i].at[me], lands[i].at[me], local_sems.at[i]) for i in range(n)]
    for i in range(n):
        for k in range(1, NDEV):
            px, py, pc = _peer(ax, ay, ac, k)
            copies.append(pltpu.make_async_remote_copy(
                src_ref=srcs[i].at[4 * px + 2 * py + pc], dst_ref=lands[i].at[me],
                send_sem=send_sems.at[7 * i + k - 1], recv_sem=recv_sems.at[7 * i + k - 1],
                device_id=(px, py, pc), device_id_type=MESH_ID))
    return copies


def _exchange_start(gs, name):
    n = len(gs)

    def body(*refs):
        srcs, lands = refs[:n], refs[n:2 * n]
        send_sems, recv_sems, local_sems = refs[2 * n:2 * n + 3]
        token = refs[-1]
        for cp in _exchange_copies(srcs, lands, send_sems, recv_sems, local_sems):
            cp.start()
        token[...] = jnp.zeros_like(token)

    hbm = [pltpu.HBM(g.shape, g.dtype) for g in gs]
    outs = pl.pallas_call(
        body, name=name,
        out_shape=(pltpu.SemaphoreType.DMA((7 * n,)), pltpu.SemaphoreType.DMA((7 * n,)), pltpu.SemaphoreType.DMA((n,)),
                   *hbm, *hbm, _sds((8, 128), f32)),
        in_specs=[HBM_SPEC] * (2 * n),
        out_specs=(SEM_SPEC, SEM_SPEC, SEM_SPEC, *[HBM_SPEC] * (2 * n), pl.BlockSpec(memory_space=pltpu.VMEM)),
        input_output_aliases={i: 3 + i for i in range(2 * n)},
        compiler_params=pltpu.CompilerParams(has_side_effects=DATAFLOW),
    )(*[pltpu.with_memory_space_constraint(g, pltpu.HBM) for g in gs],
      *[pltpu.with_memory_space_constraint(lax.empty(g.shape, g.dtype), pltpu.HBM) for g in gs])
    return outs[:-1], outs[-1]


def _exchange_wait(handles, after, name):
    n = (len(handles) - 3) // 2
    send_sems, recv_sems, local_sems = handles[:3]
    srcs, lands = handles[3:3 + n], handles[3 + n:]

    def body(*refs):
        srcs, lands = refs[:n], refs[n:2 * n]
        send_sems, recv_sems, local_sems = refs[2 * n:2 * n + 3]
        for cp in _exchange_copies(srcs, lands, send_sems, recv_sems, local_sems):
            cp.wait()

    hbm = [pltpu.HBM(g.shape, g.dtype) for g in srcs]
    outs = pl.pallas_call(
        body, name=name, out_shape=(*hbm, *hbm),
        in_specs=[HBM_SPEC] * (2 * n) + [SEM_SPEC] * 3 + [pl.BlockSpec(memory_space=pl.ANY)],
        out_specs=tuple([HBM_SPEC] * (2 * n)),
        input_output_aliases={i: i for i in range(2 * n)},
        compiler_params=pltpu.CompilerParams(has_side_effects=DATAFLOW),
    )(*srcs, *lands, send_sems, recv_sems, local_sems, after)
    return outs[n:]


def _row_tile(rows, cols):
    cap = max(8, (1 << 18) // cols)
    if rows <= cap:
        return rows
    best = None
    for t in range(8, cap + 1, 8):
        if rows % t == 0:
            best = t
    assert best is not None, (rows, cols)
    return best


def _sum_parts(parts, name):
    npart, rows, cols = parts.shape
    tr = _row_tile(rows, cols)

    def body(p_ref, o_ref):
        g = p_ref[0].astype(f32)
        for j in range(1, npart):
            g = g + p_ref[j].astype(f32)
        o_ref[...] = g

    return pl.pallas_call(
        body, name=name, grid=(rows // tr,),
        in_specs=[pl.BlockSpec((npart, tr, cols), lambda i: (0, i, 0))], out_specs=pl.BlockSpec((tr, cols), lambda i: (i, 0)),
        out_shape=_sds((rows, cols), f32), compiler_params=_params(("parallel",)),
    )(parts)


def _adamw(parts, w, m, v, name):
    npart, rows, cols = parts.shape
    tr = _row_tile(rows, cols)
    c1 = 1.0 / (1.0 - ADAM_B1 ** ADAM_STEP)
    c2 = 1.0 / (1.0 - ADAM_B2 ** ADAM_STEP)

    def body(p_ref, w_ref, m_ref, v_ref, g_ref, d_ref, mo_ref, vo_ref):
        g = p_ref[0].astype(f32)
        for j in range(1, npart):
            g = g + p_ref[j].astype(f32)
        mn = ADAM_B1 * m_ref[...] + (1.0 - ADAM_B1) * g
        vn = ADAM_B2 * v_ref[...] + (1.0 - ADAM_B2) * (g * g)
        g_ref[...] = g
        mo_ref[...] = mn
        vo_ref[...] = vn
        d_ref[...] = (-ADAM_LR) * ((mn * c1) / (jnp.sqrt(vn * c2) + ADAM_EPS) + ADAM_WD * w_ref[...])

    blk = pl.BlockSpec((tr, cols), lambda i: (i, 0))
    return pl.pallas_call(
        body, name=name, grid=(rows // tr,),
        in_specs=[pl.BlockSpec((npart, tr, cols), lambda i: (0, i, 0)), blk, blk, blk], out_specs=[blk] * 4,
        out_shape=[_sds((rows, cols), f32)] * 4, compiler_params=_params(("parallel",)),
    )(parts, w, m, v)


def _adamw_layer(parts, w, m, v, acc, l, name):
    npart, rows, cols = parts.shape
    tr = _row_tile(rows, cols)
    c1 = 1.0 / (1.0 - ADAM_B1 ** ADAM_STEP)
    c2 = 1.0 / (1.0 - ADAM_B2 ** ADAM_STEP)

    def body(p_ref, w_ref, m_ref, v_ref, *refs):
        g_ref, d_ref, mo_ref, vo_ref = refs[-4:]
        g = p_ref[0].astype(f32)
        for j in range(1, npart):
            g = g + p_ref[j].astype(f32)
        mn = ADAM_B1 * m_ref[...] + (1.0 - ADAM_B1) * g
        vn = ADAM_B2 * v_ref[...] + (1.0 - ADAM_B2) * (g * g)
        g_ref[...] = g
        mo_ref[...] = mn
        vo_ref[...] = vn
        d_ref[...] = (-ADAM_LR) * ((mn * c1) / (jnp.sqrt(vn * c2) + ADAM_EPS) + ADAM_WD * w_ref[...])

    blk = pl.BlockSpec((None, tr, cols), lambda i: (l, i, 0))
    prev = [] if acc is None else list(acc)
    return pl.pallas_call(
        body, name=name, grid=(rows // tr,),
        in_specs=[pl.BlockSpec((npart, tr, cols), lambda i: (0, i, 0)), blk, blk, blk] + [HBM_ANY] * len(prev),
        out_specs=[blk] * 4, out_shape=[_sds(w.shape, f32)] * 4,
        input_output_aliases={4 + j: j for j in range(len(prev))},
        compiler_params=_params(("parallel",)),
    )(parts, w, m, v, *prev)


def _pack(arrays):
    rows = []
    for a in arrays:
        flat = a.reshape(-1)
        pad = (-flat.shape[0]) % 1024
        rows.append(jnp.concatenate([flat, jnp.zeros((pad,), flat.dtype)]).reshape(-1, 128))
    return jnp.concatenate(rows, axis=0)


def _unpack(flat, shapes):
    out, r = [], 0
    for s in shapes:
        n = math.prod(s)
        nr = (n + 1023) // 1024 * 8
        out.append(flat[r:r + nr].reshape(-1)[:n].reshape(s))
        r += nr
    return out


BIG_SHARD_AXIS = dict(w_in=2, w_branch=3, w_out=1, w_ffn_in=2, w_ffn_out=1)
SHARDED_SMALL = ("attn_rel_bias", "lru_conv_w")


def _to_blocks(full, axis):
    s = full.shape
    cut = full.reshape(s[:axis] + (NDEV, s[axis] // NDEV) + s[axis + 1:])
    return jnp.moveaxis(cut, axis, 0)


def _from_blocks(blocks, axis):
    moved = jnp.moveaxis(blocks, 0, axis)
    s = moved.shape
    return moved.reshape(s[:axis] + (s[axis] * s[axis + 1],) + s[axis + 2:])


def _flat2(a):
    return a.reshape(-1, a.shape[-1])


def _my_slice(a, n):
    ax, ay, ac = _mesh_pos()
    return lax.dynamic_slice_in_dim(a, (4 * ax + 2 * ay + ac) * n, n, axis=a.ndim - 1)


_WEIGHTS = ("norm_mix_pre", "norm_mix_post", "norm_ffn_pre", "norm_ffn_post", "w_in", "attn_rel_bias", "hgrn_lb_logits",
            "hgrn_norm_g", "gmlp_norm_g", "gmlp_ws", "gmlp_bs", "lru_conv_w", "lru_conv_b", "lru_wa", "lru_ba", "lru_wx",
            "lru_bx", "lru_lambda", "w_branch", "w_out", "w_ffn_in", "w_ffn_out")


def _step(x, loss_target, w, m, v):
    gathered = []
    for l in range(DEPTH):
        gathered.append(tuple(_gather_sc([w[k][l].astype(bf16) for k in keys], "gather_%s%d" % (half, l))
                              for half, keys in (("mix", MIX_BIG), ("ffn", FFN_BIG))))
    cut = jnp.concatenate([w[k] for k in SHARDED_SMALL], axis=-1)
    parts = _all_gather(_pack([cut]), "gather_small").reshape(NDEV, -1)[:, :math.prod(cut.shape)].reshape((NDEV,) + cut.shape)
    small = {k: w[k] for k in SMALL if k not in SHARDED_SMALL}
    at = 0
    for k in SHARDED_SMALL:
        n = w[k].shape[-1]
        small[k] = _from_blocks(parts[..., at:at + n], 2)
        at += n
    loss, dx, layers = _step_forward(x, loss_target, gathered, small)
    flat3 = lambda a: a.reshape((DEPTH, -1, a.shape[-1]))
    acc = {k: None for k in BIG}
    smalls = [None] * DEPTH

    def send(grads, keys, name):
        handles, token = _exchange_start([_to_blocks(grads[k], BIG_SHARD_AXIS[k] - 1) for k in keys], "start_" + name)
        return (keys, handles, "wait_" + name), token[0:1, 0:1]

    def update(sent, l, after):
        keys, handles, name = sent
        got = dict(zip(keys, _exchange_wait(handles, after, name)))
        for k, g in got.items():
            w3 = flat3(w[k])
            acc[k] = _adamw_layer(g.reshape((NDEV,) + w3.shape[1:]), w3, flat3(m[k]), flat3(v[k]), acc[k], l,
                                  "adamw_%s_%d" % (k, l))

    waiting = []
    for l in range(DEPTH - 1, -1, -1):
        sent_ffn = []

        def ffn_grads_ready(grads, dx1, l=l, sent_ffn=sent_ffn):
            sent, zero = send(grads, FFN_BIG, "exchange_ffn%d" % l)
            sent_ffn.append(sent)
            while waiting:
                update(*waiting.pop(), dx1)
            return zero

        sent_mix = []

        def mix_grads_ready(grads, l=l, sent_mix=sent_mix):
            sent, zero = send(grads, MIX_BIG, "exchange_mix%d" % l)
            sent_mix.append(sent)
            return zero

        dx, _, smalls[l] = _step_backward(dx, layers[l], ffn_grads_ready, mix_grads_ready)
        update(sent_ffn[0], l, dx)
        waiting.append((sent_mix[0], l))
    grads, deltas, new_m, new_v = {}, {}, {}, {}
    gsmall ={k: jnp.stack([smalls[l][k] for l in range(DEPTH)]) for k in smalls[0]}
    gsmall["hgrn_lb_logits"] = _lb_bwd(small["hgrn_lb_logits"], gsmall.pop("lb"))
    shapes = [gsmall[k].shape for k in SMALL]
    sums = _unpack(_sum_parts(_all_gather(_pack([gsmall[k] for k in SMALL]), "gather_small_grads"), "sum_small_grads"), shapes)
    gs = dict(zip(SMALL, sums))
    for k in SHARDED_SMALL:
        gs[k] = _my_slice(gs[k], w[k].shape[-1])
    packed = [_pack([d[k] for k in SMALL]) for d in (gs, w, m, v)]
    outs = _adamw(packed[0][None], packed[1], packed[2], packed[3], "adamw_small")
    shapes = [w[k].shape for k in SMALL]
    for d, o in zip((grads, deltas, new_m, new_v), outs):
        d.update(zip(SMALL, _unpack(o, shapes)))
    update(*waiting.pop(), outs[1])
    for k in BIG:
        grads[k], deltas[k], new_m[k], new_v[k] = (o.reshape(w[k].shape) for o in acc[k])
    total = lax.psum(loss[0, 0], ("x", "y", "c"))
    return total, dx[None], grads, deltas, new_m, new_v


def _step_forward(x, loss_target, gathered, small):
    lbs = _lb_fwd(small["hgrn_lb_logits"])
    x = x[0]
    layers = []

    def weights(blocks, keys, after):
        if after is not None:
            blocks, _ = lax.optimization_barrier((blocks, after))
        return {k: _from_blocks(g, BIG_SHARD_AXIS[k] - 1) for k, g in zip(keys, blocks)}

    for l in range(DEPTH):
        mix, ffn = gathered[l]
        p = _layer_params(l, weights(mix, MIX_BIG, x if l else None), small, lbs)
        x, sv = _layer_fwd(x, p, lambda x1, ffn=ffn: _ffn_weights(weights(ffn, FFN_BIG, x1)))
        layers.append((p, sv))
    loss, dx = _loss_head(x, loss_target[0])
    return loss, dx, layers


def _step_backward(dx, layer, ffn_grads_ready, mix_grads_ready):
    return _layer_bwd(dx, *layer, ffn_grads_ready, mix_grads_ready)


def kernel(x, norm_mix_pre, norm_mix_post, norm_ffn_pre, norm_ffn_post, w_in, attn_rel_bias, hgrn_lb_logits, hgrn_norm_g, gmlp_norm_g, gmlp_ws, gmlp_bs, lru_conv_w, lru_conv_b, lru_wa, lru_ba, lru_wx, lru_bx, lru_lambda, w_branch, w_out, w_ffn_in, w_ffn_out, loss_target, m_norm_mix_pre, m_norm_mix_post, m_norm_ffn_pre, m_norm_ffn_post, m_w_in, m_attn_rel_bias, m_hgrn_lb_logits, m_hgrn_norm_g, m_gmlp_norm_g, m_gmlp_ws, m_gmlp_bs, m_lru_conv_w, m_lru_conv_b, m_lru_wa, m_lru_ba, m_lru_wx, m_lru_bx, m_lru_lambda, m_w_branch, m_w_out, m_w_ffn_in, m_w_ffn_out, v_norm_mix_pre, v_norm_mix_post, v_norm_ffn_pre, v_norm_ffn_post, v_w_in, v_attn_rel_bias, v_hgrn_lb_logits, v_hgrn_norm_g, v_gmlp_norm_g, v_gmlp_ws, v_gmlp_bs, v_lru_conv_w, v_lru_conv_b, v_lru_wa, v_lru_ba, v_lru_wx, v_lru_bx, v_lru_lambda, v_w_branch, v_w_out, v_w_ffn_in, v_w_ffn_out):
    w = dict(zip(_WEIGHTS, (norm_mix_pre, norm_mix_post, norm_ffn_pre, norm_ffn_post, w_in, attn_rel_bias, hgrn_lb_logits, hgrn_norm_g, gmlp_norm_g, gmlp_ws, gmlp_bs, lru_conv_w, lru_conv_b, lru_wa, lru_ba, lru_wx, lru_bx, lru_lambda, w_branch, w_out, w_ffn_in, w_ffn_out)))
    m = dict(zip(_WEIGHTS, (m_norm_mix_pre, m_norm_mix_post, m_norm_ffn_pre, m_norm_ffn_post, m_w_in, m_attn_rel_bias, m_hgrn_lb_logits, m_hgrn_norm_g, m_gmlp_norm_g, m_gmlp_ws, m_gmlp_bs, m_lru_conv_w, m_lru_conv_b, m_lru_wa, m_lru_ba, m_lru_wx, m_lru_bx, m_lru_lambda, m_w_branch, m_w_out, m_w_ffn_in, m_w_ffn_out)))
    v = dict(zip(_WEIGHTS, (v_norm_mix_pre, v_norm_mix_post, v_norm_ffn_pre, v_norm_ffn_post, v_w_in, v_attn_rel_bias, v_hgrn_lb_logits, v_hgrn_norm_g, v_gmlp_norm_g, v_gmlp_ws, v_gmlp_bs, v_lru_conv_w, v_lru_conv_b, v_lru_wa, v_lru_ba, v_lru_wx, v_lru_bx, v_lru_lambda, v_w_branch, v_w_out, v_w_ffn_in, v_w_ffn_out)))
    loss, grad_x, grads, deltas, new_m, new_v = _step(x, loss_target, w, m, v)
    return (loss, grad_x, *[grads[k] for k in _WEIGHTS], *[deltas[k] for k in _WEIGHTS],
            *[new_m[k] for k in _WEIGHTS], *[new_v[k] for k in _WEIGHTS])
```

```python
import math

import jax
import jax.numpy as jnp
from jax import lax
from jax.experimental import pallas as pl
from jax.experimental.pallas import tpu as pltpu
from jax.experimental.pallas import tpu_sc as plsc

f32 = jnp.float32
bf16 = jnp.bfloat16

SEQ = 2048
DM = 1024
DEPTH = 4
NDEV = 8
MIXW = 256
NHEAD = 4
HDIM = 64
NMIX = 11 * MIXW
NGATE = 4 * DM
FFH = 2816
EPS = 1e-6
NEG_BIG = -1e30
LOG_FLOOR = 1e-30
LRU_C = 8.0
REL_SIZE = 320
ATT_PAIR = 128
ATT_BAND = 640
ATT_PAD = 512
ATT_WV = 768
HG_T = 16
HG_N = SEQ // HG_T
GM_T = 128
LRU_T = 128
ADAM_LR, ADAM_B1, ADAM_B2, ADAM_EPS, ADAM_WD, ADAM_STEP = 0.001, 0.9, 0.999, 1e-8, 0.01, 10
V7X_VMEM_LIMIT = 56 * 1024 * 1024
GELU_C0 = math.sqrt(2.0 / math.pi)
GELU_C1 = 0.044715
MESH_ID = pl.DeviceIdType.MESH


def _params(sem=None):
    if sem is None:
        return pltpu.CompilerParams(vmem_limit_bytes=V7X_VMEM_LIMIT)
    return pltpu.CompilerParams(dimension_semantics=sem, vmem_limit_bytes=V7X_VMEM_LIMIT)


def _sds(shape, dtype):
    return jax.ShapeDtypeStruct(shape, dtype)


def _dot(a, b):
    return jnp.dot(a.astype(bf16), b.astype(bf16), preferred_element_type=f32)


def _dot_nt(a, b):
    return lax.dot_general(a.astype(bf16), b.astype(bf16), (((1,), (1,)), ((), ())), preferred_element_type=f32)


def _dot_tn(a, b):
    return lax.dot_general(a.astype(bf16), b.astype(bf16), (((0,), (0,)), ((), ())), preferred_element_type=f32)


def _split(a):
    hi = a.astype(bf16)
    lo = (a - hi.astype(f32)).astype(bf16)
    return hi, lo


def _dot_hl(a, m):
    hi, lo = _split(a)
    return jnp.dot(hi, m, preferred_element_type=f32) + jnp.dot(lo, m, preferred_element_type=f32)


def _dot_nt_hl(m, a):
    hi, lo = _split(a)
    dn = (((1,), (1,)), ((), ()))
    return lax.dot_general(m, hi, dn, preferred_element_type=f32) + lax.dot_general(m, lo, dn, preferred_element_type=f32)


def _sigmoid(x):
    return jax.nn.sigmoid(x)


def _silu(x):
    return x * _sigmoid(x)


def _dsilu(x):
    s = _sigmoid(x)
    return s * (1.0 + x * (1.0 - s))


def _gelu(x):
    return 0.5 * x * (1.0 + jnp.tanh(GELU_C0 * (x + GELU_C1 * x * x * x)))


def _dgelu(x):
    t = jnp.tanh(GELU_C0 * (x + GELU_C1 * x * x * x))
    return 0.5 * (1.0 + t) + 0.5 * x * (1.0 - t * t) * GELU_C0 * (1.0 + 3.0 * GELU_C1 * x * x)


def _rms(x, g):
    r = lax.rsqrt(jnp.mean(x * x, axis=-1, keepdims=True) + EPS)
    return x * r * g


def _rms_bwd(x, g, dy):
    r = lax.rsqrt(jnp.mean(x * x, axis=-1, keepdims=True) + EPS)
    xh = x * r
    dxh = dy * g
    dx = r * (dxh - xh * jnp.mean(dxh * xh, axis=-1, keepdims=True))
    return dx, jnp.sum(dy * xh, axis=0, keepdims=True)


def _same_head(n, width, dtype):
    r = lax.broadcasted_iota(jnp.int32, (n, n), 0) // width
    c = lax.broadcasted_iota(jnp.int32, (n, n), 1) // width
    return (r == c).astype(dtype)


def _head_masks(rows=1):
    lane = lax.broadcasted_iota(jnp.int32, (rows, MIXW), 1) // HDIM
    return [lane == h for h in range(NHEAD)]


def _norm_matmul(x, g, w, tn):
    n = w.shape[1]
    tm = 1024

    def body(x_ref, g_ref, w_ref, z_ref, h_ref):
        @pl.when(pl.program_id(1) == 0)
        def _():
            h_ref[...] = _rms(x_ref[...], g_ref[...]).astype(bf16)

        z_ref[...] = jnp.dot(h_ref[...], w_ref[...], preferred_element_type=f32)

    return pl.pallas_call(
        body, name="norm_matmul", grid=(SEQ // tm, n // tn),
        in_specs=[pl.BlockSpec((tm, DM), lambda i, j: (i, 0)), pl.BlockSpec((1, DM), lambda i, j: (0, 0)),
                  pl.BlockSpec((DM, tn), lambda i, j: (0, j))],
        out_specs=[pl.BlockSpec((tm, tn), lambda i, j: (i, j)), pl.BlockSpec((tm, DM), lambda i, j: (i, 0))],
        out_shape=[_sds((SEQ, n), f32), _sds((SEQ, DM), bf16)],
        compiler_params=_params(("parallel", "arbitrary")),
    )(x, g, w)


def _matmul(a, w, tn):
    k, n = w.shape
    tm = 1024

    def body(a_ref, w_ref, z_ref):
        z_ref[...] = jnp.dot(a_ref[...], w_ref[...], preferred_element_type=f32)

    return pl.pallas_call(
        body, name="matmul", grid=(SEQ // tm, n // tn),
        in_specs=[pl.BlockSpec((tm, k), lambda i, j: (i, 0)), pl.BlockSpec((k, tn), lambda i, j: (0, j))],
        out_specs=pl.BlockSpec((tm, tn), lambda i, j: (i, j)),
        out_shape=_sds((SEQ, n), f32),
        compiler_params=_params(("parallel", "arbitrary")),
    )(a, w)


def _att_offset_map():
    i = lax.broadcasted_iota(jnp.int32, (REL_SIZE, ATT_WV), 0)
    t = lax.broadcasted_iota(jnp.int32, (REL_SIZE, ATT_WV), 1)
    e = jnp.where(t <= ATT_BAND, t, t - ATT_WV)
    idx = jnp.clip(ATT_PAD - e, -(HDIM - 1), 256) + (HDIM - 1)
    return (idx == i).astype(bf16)


def _att_band_valid():
    qc = lax.broadcasted_iota(jnp.int32, (ATT_PAIR, ATT_BAND), 0) // HDIM
    kc = lax.broadcasted_iota(jnp.int32, (ATT_PAIR, ATT_BAND), 1) // HDIM
    return (kc >= qc) & (kc <= qc + 8)


def _att_bias_tiles(rb_ref, bm_ref):
    wv = _dot_hl(rb_ref[...], _att_offset_map())
    valid = _att_band_valid()
    for h in range(NHEAD):
        rows = jnp.broadcast_to(wv[h:h + 1, :], (ATT_PAIR, ATT_WV))
        tile = pltpu.roll(rows, 0, 1, stride=1, stride_axis=0)[:, :ATT_BAND]
        bm_ref[h] = jnp.where(valid, tile, NEG_BIG)


def _att_pad_kv(k_ref, v_ref, kp_ref, vp_ref):
    kp_ref[pl.ds(0, ATT_PAD), :] = jnp.zeros((ATT_PAD, MIXW), bf16)
    vp_ref[pl.ds(0, ATT_PAD), :] = jnp.zeros((ATT_PAD, MIXW), bf16)
    kp_ref[pl.ds(ATT_PAD, SEQ), :] = k_ref[...].astype(bf16)
    vp_ref[pl.ds(ATT_PAD, SEQ), :] = v_ref[...].astype(bf16)


def _att_probs(qm, kb, bm, key_ok):
    s = _dot_nt(qm, kb) + bm
    s = jnp.where(key_ok, s, NEG_BIG)
    m = jnp.max(s, axis=-1, keepdims=True)
    e = jnp.exp(s - m)
    return e / jnp.sum(e, axis=-1, keepdims=True)


def _attn_fwd(zm, rb8):
    def body(q_ref, k_ref, v_ref, rb_ref, o_ref, kp_ref, vp_ref, bm_ref):
        _att_pad_kv(k_ref, v_ref, kp_ref, vp_ref)
        _att_bias_tiles(rb_ref, bm_ref)
        hm = _head_masks()

        def pair(p, carry):
            r0 = pl.multiple_of(p * ATT_PAIR, ATT_PAIR)
            q = q_ref[pl.ds(r0, ATT_PAIR), :] * (HDIM ** -0.5)
            kb = kp_ref[pl.ds(r0, ATT_BAND), :]
            vb = vp_ref[pl.ds(r0, ATT_BAND), :]
            key_ok = (lax.broadcasted_iota(jnp.int32, (1, ATT_BAND), 1) + (r0 - ATT_PAD)) >= 0
            o = jnp.zeros((ATT_PAIR, MIXW), f32)
            for h in range(NHEAD):
                qm = jnp.where(hm[h], q, 0.0)
                p_h = _att_probs(qm, kb, bm_ref[h], key_ok)
                o = o + jnp.where(hm[h], _dot(p_h, vb), 0.0)
            o_ref[pl.ds(r0, ATT_PAIR), :] = o.astype(bf16)
            return carry

        lax.fori_loop(0, SEQ // ATT_PAIR, pair, 0)

    col = lambda j: pl.BlockSpec((SEQ, MIXW), lambda i: (0, j))
    return pl.pallas_call(
        body, name="attn_fwd", grid=(1,),
        in_specs=[col(0), col(1), col(2), pl.BlockSpec((8, REL_SIZE), lambda i: (0, 0))],
        out_specs=pl.BlockSpec((SEQ, MIXW), lambda i: (0, 0)),
        out_shape=_sds((SEQ, MIXW), bf16),
        scratch_shapes=[pltpu.VMEM((SEQ + ATT_PAD, MIXW), bf16), pltpu.VMEM((SEQ + ATT_PAD, MIXW), bf16),
                        pltpu.VMEM((NHEAD, ATT_PAIR, ATT_BAND), f32)],
        compiler_params=_params(("arbitrary",)),
    )(zm, zm, zm, rb8)


def _hg_gates(q, fz, lb):
    sq = _sigmoid(q)
    sg = _sigmoid(fz)
    f = lb + (1.0 - lb) * sg
    return q * sq, (1.0 - lb) * (1.0 - sg), jnp.log(jnp.maximum(f, LOG_FLOOR)), sq, sg, f


def _hg_prepare(q_ref, f_ref, lb, qf_s, kf_s, b_s, qd_s, kd_s, dec_s):
    b = None
    for t in range(HG_T):
        qf, kf, lf, _, _, _ = _hg_gates(q_ref[:, t, :], f_ref[:, t, :], lb)
        b = lf if b is None else b + lf
        qf_s[:, t, :] = qf
        kf_s[:, t, :] = kf
        b_s[:, t, :] = b
    b_last = b
    dec_s[...] = jnp.broadcast_to(jnp.exp(b_last)[:, None, :], (HG_N, 8, MIXW))
    for t in range(HG_T):
        bt = b_s[:, t, :]
        qd_s[:, t, :] = qf_s[:, t, :] * jnp.exp(bt)
        kd_s[:, t, :] = kf_s[:, t, :] * jnp.exp(b_last - bt)


def _hg_scores(t, qf_s, kf_s, b_s, w_s, hm):
    qt = qf_s[:, t, :]
    bt = b_s[:, t, :]
    for s in range(t + 1):
        w = qt * kf_s[:, s, :]
        if s < t:
            w = w * jnp.exp(bt - b_s[:, s, :])
        w_s[pl.ds(s * HG_N, HG_N), :] = w.astype(bf16)
    return jnp.dot(w_s[pl.ds(0, (t + 1) * HG_N), :], hm, preferred_element_type=f32)


def _hgrn_fwd(zm3, lb, ng):
    def body(q_ref, f_ref, i_ref, g_ref, lb_ref, ng_ref, o_ref, oraw_ref, states_ref,
             qf_s, kf_s, b_s, qd_s, kd_s, dec_s, w_s, st_s):
        lb = lb_ref[...]
        hm = _same_head(MIXW, HDIM, bf16)
        hmf = _same_head(MIXW, HDIM, f32)
        _hg_prepare(q_ref, f_ref, lb, qf_s, kf_s, b_s, qd_s, kd_s, dec_s)
        for t in range(HG_T):
            p = _hg_scores(t, qf_s, kf_s, b_s, w_s, hm)
            acc = jnp.zeros((HG_N, MIXW), f32)
            for s in range(t + 1):
                acc = acc + p[s * HG_N:(s + 1) * HG_N] * i_ref[:, s, :]
            oraw_ref[:, t, :] = acc
        st_s[...] = jnp.zeros((MIXW, MIXW), f32)

        def step(n, carry):
            st = st_s[...]
            stb = st.astype(bf16)
            states_ref[n] = stb
            oraw_ref[n] = oraw_ref[n] + _dot_nt(qd_s[n], stb)
            st_s[...] = st * dec_s[n][0:1] + _dot_tn(i_ref[n], kd_s[n]) * hmf
            return carry

        lax.fori_loop(0, HG_N, step, 0, unroll=2)
        ngv = ng_ref[...]
        for t in range(HG_T):
            o = oraw_ref[:, t, :]
            ms = _dot_hl(o * o, hm) * (1.0 / HDIM)
            o_ref[:, t, :] = (o * lax.rsqrt(ms + EPS) * ngv * _silu(g_ref[:, t, :])).astype(bf16)

    one = pl.Buffered(1)
    col = lambda j: pl.BlockSpec((HG_N, HG_T, MIXW), lambda i: (0, 0, j), pipeline_mode=one)
    vec = pl.BlockSpec((1, MIXW), lambda i: (0, 0))
    blk = pl.BlockSpec((HG_N, HG_T, MIXW), lambda i: (0, 0, 0))
    s3 = pltpu.VMEM((HG_N, HG_T, MIXW), f32)
    return pl.pallas_call(
        body, name="hgrn_fwd", grid=(1,),
        in_specs=[col(3), col(4), col(5), col(6), vec, vec],
        out_specs=[blk, blk, pl.BlockSpec((HG_N, MIXW, MIXW), lambda i: (0, 0, 0), pipeline_mode=one)],
        out_shape=[_sds((HG_N, HG_T, MIXW), bf16), _sds((HG_N, HG_T, MIXW), f32), _sds((HG_N, MIXW, MIXW), bf16)],
        scratch_shapes=[s3, s3, s3, s3, s3, pltpu.VMEM((HG_N, 8, MIXW), f32),
                        pltpu.VMEM((HG_T * HG_N, MIXW), bf16), pltpu.VMEM((MIXW, MIXW), f32)],
        compiler_params=_params(("arbitrary",)),
    )(zm3, zm3, zm3, zm3, lb, ng)


def _gm_weights(ws_ref):
    tril = lax.broadcasted_iota(jnp.int32, (GM_T, GM_T), 0) >= lax.broadcasted_iota(jnp.int32, (GM_T, GM_T), 1)
    return tril, [jnp.where(tril, ws_ref[g], 0.0).astype(bf16) for g in range(NHEAD)]


def _gm_expand():
    r = lax.broadcasted_iota(jnp.int32, (8, MIXW), 0)
    c = lax.broadcasted_iota(jnp.int32, (8, MIXW), 1) // HDIM
    return (r == c).astype(bf16)


def _gm_mixed(vn, wts, bias, hm):
    vb = vn.astype(bf16)
    mixed = bias
    for g in range(NHEAD):
        mixed = mixed + jnp.where(hm[g], jnp.dot(wts[g], vb, preferred_element_type=f32), 0.0)
    return mixed


def _gm_bias(bs_ref):
    hi, lo = _split(bs_ref[...])
    et = _gm_expand()
    dn = (((0,), (0,)), ((), ()))
    return lax.dot_general(hi, et, dn, preferred_element_type=f32) + lax.dot_general(lo, et, dn, preferred_element_type=f32)


def _gmlp_fwd(zm, ng, ws, bs8):
    def body(u_ref, v_ref, ng_ref, ws_ref, bs_ref, o_ref):
        hm = _head_masks()
        _, wts = _gm_weights(ws_ref)
        bias = _gm_bias(bs_ref)
        ngv = ng_ref[...]

        def blk(n, carry):
            rows = pl.ds(pl.multiple_of(n * GM_T, GM_T), GM_T)
            vn = _rms(_gelu(v_ref[rows, :]), ngv)
            o_ref[rows, :] = (_gelu(u_ref[rows, :]) * _gm_mixed(vn, wts, bias, hm)).astype(bf16)
            return carry

        lax.fori_loop(0, SEQ // GM_T, blk, 0)

    col = lambda j: pl.BlockSpec((SEQ, MIXW), lambda i: (0, j))
    return pl.pallas_call(
        body, name="gmlp_fwd", grid=(1,),
        in_specs=[col(7), col(8), pl.BlockSpec((1, MIXW), lambda i: (0, 0)),
                  pl.BlockSpec((NHEAD, GM_T, GM_T), lambda i: (0, 0, 0)), pl.BlockSpec((8, GM_T), lambda i: (0, 0))],
        out_specs=pl.BlockSpec((SEQ, MIXW), lambda i: (0, 0)),
        out_shape=_sds((SEQ, MIXW), bf16),
        compiler_params=_params(("arbitrary",)),
    )(zm, zm, ng, ws, bs8)


def _lru_conv(x_ref, cw_ref, cb_ref, xp_s, xc_s):
    xp_s[pl.ds(0, 8), :] = jnp.zeros((8, MIXW), f32)
    xp_s[pl.ds(8, SEQ), :] = x_ref[...]
    cw = cw_ref[...]
    xc = cb_ref[...] + x_ref[...] * cw[3:4]
    for k in range(1, 4):
        xc = xc + xp_s[pl.ds(8 - k, SEQ), :] * cw[3 - k:4 - k]
    xc_s[...] = xc


def _lru_gates(xc, wa, ba, wx, bx, sp, first_row):
    r = _sigmoid(_dot(xc, wa) + ba)
    ig = _sigmoid(_dot(xc, wx) + bx)
    la = (-LRU_C) * r * sp
    a = jnp.exp(la)
    th = jnp.tanh(la)
    m2 = -2.0 * th / (1.0 - th)
    mult = jnp.where(first_row, 1.0, jnp.sqrt(jnp.maximum(m2, 0.0)))
    return a, mult, r, ig, m2


def _lru_scan(a, b, rev):
    row = lax.broadcasted_iota(jnp.int32, (LRU_T, 1), 0)
    k = 1
    while k < LRU_T:
        ok = (row < LRU_T - k) if rev else (row >= k)
        sh = (LRU_T - k) if rev else k
        a_sh = jnp.where(ok, pltpu.roll(a, sh, 0), 1.0)
        b_sh = jnp.where(ok, pltpu.roll(b, sh, 0), 0.0)
        b = b + a * b_sh
        a = a * a_sh
        k *= 2
    return a, b


def _lru_fwd(zm, cw8, cb, wa, ba, wx, bx, lam):
    def body(x_ref, g_ref, cw_ref, cb_ref, wa_ref, ba_ref, wx_ref, bx_ref, lam_ref, o_ref, h_ref, xp_s, xc_s):
        _lru_conv(x_ref, cw_ref, cb_ref, xp_s, xc_s)
        sp = jax.nn.softplus(-lam_ref[...])
        wa_v, wx_v, ba_v, bx_v = wa_ref[...], wx_ref[...], ba_ref[...], bx_ref[...]

        def chunk(c, h_prev):
            rows = pl.ds(pl.multiple_of(c * LRU_T, LRU_T), LRU_T)
            first = (lax.broadcasted_iota(jnp.int32, (LRU_T, 1), 0) + c * LRU_T) == 0
            xc = xc_s[rows, :]
            a, mult, _, ig, _ = _lru_gates(xc, wa_v, ba_v, wx_v, bx_v, sp, first)
            acum, hloc = _lru_scan(a, mult * (ig * xc), False)
            h = hloc + acum * h_prev
            h_ref[rows, :] = h
            o_ref[rows, :] = (h * _gelu(g_ref[rows, :])).astype(bf16)
            return h[LRU_T - 1:LRU_T, :]

        lax.fori_loop(0, SEQ // LRU_T, chunk, jnp.zeros((1, MIXW), f32))

    col = lambda j: pl.BlockSpec((SEQ, MIXW), lambda i: (0, j))
    vec = pl.BlockSpec((1, MIXW), lambda i: (0, 0))
    mat = pl.BlockSpec((MIXW, MIXW), lambda i: (0, 0))
    out = pl.BlockSpec((SEQ, MIXW), lambda i: (0, 0))
    return pl.pallas_call(
        body, name="lru_fwd", grid=(1,),
        in_specs=[col(9), col(10), pl.BlockSpec((8, MIXW), lambda i: (0, 0)), vec, mat, vec, mat, vec, vec],
        out_specs=[out, out],
        out_shape=[_sds((SEQ, MIXW), bf16), _sds((SEQ, MIXW), f32)],
        scratch_shapes=[pltpu.VMEM((SEQ + 8, MIXW), f32), pltpu.VMEM((SEQ, MIXW), f32)],
        compiler_params=_params(("arbitrary",)),
    )(zm, zm, cw8, cb, wa, ba, wx, bx, lam)


def _block_diag(w):
    out = jnp.zeros((MIXW, MIXW), w.dtype)
    for h in range(NHEAD):
        out = lax.dynamic_update_slice(out, w[h], (h * HDIM, h * HDIM))
    return out


def _diag_blocks(w):
    return jnp.stack([w[h * HDIM:(h + 1) * HDIM, h * HDIM:(h + 1) * HDIM] for h in range(NHEAD)])


ROW_TILE = 256


def _merge_fwd(outs, zg, wb, wo, x, g2):
    def body(oa_ref, ob_ref, oc_ref, od_ref, zg_ref, wb_ref, wo_ref, x_ref, g_ref, xo_ref, mg_ref, y_ref):
        merged = jnp.zeros((ROW_TILE, DM), f32)
        for n, o_ref in enumerate((oa_ref, ob_ref, oc_ref, od_ref)):
            proj = jnp.dot(o_ref[...], wb_ref[n], preferred_element_type=f32)
            merged = merged + _sigmoid(zg_ref[:, n * DM:(n + 1) * DM]) * proj
        mb = merged.astype(bf16)
        y = jnp.dot(mb, wo_ref[...], preferred_element_type=f32)
        mg_ref[...] = mb
        y_ref[...] = y
        xo_ref[...] = x_ref[...] + _rms(y, g_ref[...])

    row = lambda w: pl.BlockSpec((ROW_TILE, w), lambda i: (i, 0))
    return pl.pallas_call(
        body, name="merge_fwd", grid=(SEQ // ROW_TILE,),
        in_specs=[row(MIXW)] * 4 + [row(NGATE), pl.BlockSpec((NHEAD, MIXW, DM), lambda i: (0, 0, 0)),
                                    pl.BlockSpec((DM, DM), lambda i: (0, 0)), row(DM), pl.BlockSpec((1, DM), lambda i: (0, 0))],
        out_specs=[row(DM), row(DM), row(DM)],
        out_shape=[_sds((SEQ, DM), f32), _sds((SEQ, DM), bf16), _sds((SEQ, DM), f32)],
        compiler_params=_params(("parallel",)),
    )(*outs, zg, wb, wo, x, g2)


def _ffn_out(u, w2, x, g4):
    def body(u_ref, w_ref, x_ref, g_ref, xo_ref, f_ref):
        a = _silu(u_ref[:, :FFH]) * u_ref[:, FFH:]
        f = jnp.dot(a.astype(bf16), w_ref[...], preferred_element_type=f32)
        f_ref[...] = f
        xo_ref[...] = x_ref[...] + _rms(f, g_ref[...])

    row = lambda w: pl.BlockSpec((ROW_TILE, w), lambda i: (i, 0))
    return pl.pallas_call(
        body, name="ffn_out", grid=(SEQ // ROW_TILE,),
        in_specs=[row(2 * FFH), pl.BlockSpec((FFH, DM), lambda i: (0, 0)), row(DM), pl.BlockSpec((1, DM), lambda i: (0, 0))],
        out_specs=[row(DM), row(DM)],
        out_shape=[_sds((SEQ, DM), f32), _sds((SEQ, DM), f32)],
        compiler_params=_params(("parallel",)),
    )(u, w2, x, g4)


def _loss_head(x, tgt):
    tm = 512

    def body(x_ref, t_ref, l_ref, dx_ref):
        @pl.when(pl.program_id(0) == 0)
        def _():
            l_ref[...] = jnp.zeros((1, 1), f32)

        d = x_ref[...] - t_ref[...]
        dx_ref[...] = d * (1.0 / DM)
        l_ref[...] += (0.5 / DM) * jnp.sum(d * d).reshape(1, 1)

    row = pl.BlockSpec((tm, DM), lambda i: (i, 0))
    return pl.pallas_call(
        body, name="loss_head", grid=(SEQ // tm,),
        in_specs=[row, row], out_specs=[pl.BlockSpec((1, 1), lambda i: (0, 0)), row],
        out_shape=[_sds((1, 1), f32), _sds((SEQ, DM), f32)],
        compiler_params=_params(("arbitrary",)),
    )(x, tgt)


def _lb_fwd(logits):
    def body(lg_ref, o_ref):
        lg = lg_ref[...]
        e = jnp.exp(lg - jnp.max(lg, axis=0, keepdims=True))
        p = e / jnp.sum(e, axis=0, keepdims=True)
        acc = jnp.zeros((1, MIXW), f32)
        o_ref[0:1, :] = acc
        for l in range(1, DEPTH):
            acc = acc + p[l:l + 1]
            o_ref[l:l + 1, :] = acc

    return pl.pallas_call(body, name="lb_fwd", out_shape=_sds((DEPTH, MIXW), f32))(logits)


def _lb_bwd(logits, dlbs):
    def body(lg_ref, d_ref, o_ref):
        lg = lg_ref[...]
        e = jnp.exp(lg - jnp.max(lg, axis=0, keepdims=True))
        p = e / jnp.sum(e, axis=0, keepdims=True)
        d = d_ref[...]
        dp = [jnp.zeros((1, MIXW), f32)] * DEPTH
        acc = jnp.zeros((1, MIXW), f32)
        for j in range(DEPTH - 1, 0, -1):
            acc = acc + d[j:j + 1]
            dp[j] = acc
        inner = sum(p[j:j + 1] * dp[j] for j in range(DEPTH))
        for j in range(DEPTH):
            o_ref[j:j + 1, :] = p[j:j + 1] * (dp[j] - inner)

    return pl.pallas_call(body, name="lb_bwd", out_shape=_sds((DEPTH, MIXW), f32))(logits, dlbs)


def _pad_rows(a, rows=8):
    return jnp.concatenate([a, jnp.zeros((rows - a.shape[0], a.shape[1]), a.dtype)], axis=0)


def _layer_params(l, full, small, lbs):
    row = lambda name: small[name][l][None]
    return dict(
        _mix_weights(full), **(_ffn_weights(full) if "w_ffn_in" in full else {}),
        g1=row("norm_mix_pre"), g2=row("norm_mix_post"), g3=row("norm_ffn_pre"), g4=row("norm_ffn_post"),
        rb8=_pad_rows(small["attn_rel_bias"][l]), lb=lbs[l][None], hng=row("hgrn_norm_g"),
        gng=row("gmlp_norm_g"), gws=small["gmlp_ws"][l], gbs8=_pad_rows(small["gmlp_bs"][l]),
        cw8=_pad_rows(small["lru_conv_w"][l]), cb=row("lru_conv_b"),
        wa=_block_diag(small["lru_wa"][l]).astype(bf16), ba=row("lru_ba"),
        wx=_block_diag(small["lru_wx"][l]).astype(bf16), bx=row("lru_bx"), lam=row("lru_lambda"),
    )


def _mix_weights(full):
    return dict(wm=full["w_in"][:, :NMIX], wgt=full["w_in"][:, NMIX:], wb=full["w_branch"], wo=full["w_out"])


def _ffn_weights(full):
    return dict(w1=full["w_ffn_in"], w2=full["w_ffn_out"])


def _layer_fwd(x, p, late_ffn_weights=None):
    zm, h = _norm_matmul(x, p["g1"], p["wm"], 1408)
    zg = _matmul(h, p["wgt"], 1024)
    oa = _attn_fwd(zm, p["rb8"])
    ob3, obraw3, hstates = _hgrn_fwd(zm.reshape(HG_N, HG_T, NMIX), p["lb"], p["hng"])
    oc = _gmlp_fwd(zm, p["gng"], p["gws"], p["gbs8"])
    od, hd = _lru_fwd(zm, p["cw8"], p["cb"], p["wa"], p["ba"], p["wx"], p["bx"], p["lam"])
    outs = (oa, ob3.reshape(SEQ, MIXW), oc, od)
    x1, merged, y = _merge_fwd(outs, zg, p["wb"], p["wo"], x, p["g2"])
    if late_ffn_weights is not None:
        p.update(late_ffn_weights(x1))
    u, h2 = _norm_matmul(x1, p["g3"], p["w1"], 1408)
    x2, f = _ffn_out(u, p["w2"], x1, p["g4"])
    saved = dict(x=x, h=h, zm=zm, zg=zg, outs=outs, obraw3=obraw3, hstates=hstates, hd=hd, x1=x1, merged=merged, y=y, u=u, h2=h2, f=f)
    return x2, saved


def _att_bias_grad(db_ref, o_ref):
    r = lax.broadcasted_iota(jnp.int32, (ATT_PAIR, ATT_PAIR), 0)
    c = lax.broadcasted_iota(jnp.int32, (ATT_PAIR, ATT_PAIR), 1)
    flip = (r + c == ATT_PAIR - 1).astype(bf16)
    rows = []
    for h in range(NHEAD):
        d = jnp.concatenate([db_ref[h], jnp.zeros((ATT_PAIR, ATT_WV - ATT_BAND), f32)], axis=1)
        hi, lo = _split(d)
        rev = jnp.dot(flip, hi, preferred_element_type=f32) + jnp.dot(flip, lo, preferred_element_type=f32)
        lined = pltpu.roll(rev, ATT_WV - (ATT_PAIR - 1), 1, stride=1, stride_axis=0)
        rows.append(jnp.sum(lined, axis=0, keepdims=True))
    dwv = jnp.concatenate(rows + [jnp.zeros((8 - NHEAD, ATT_WV), f32)], axis=0)
    hi, lo = _split(dwv)
    m = _att_offset_map()
    dn = (((1,), (1,)), ((), ()))
    o_ref[...] = lax.dot_general(hi, m, dn, preferred_element_type=f32) + lax.dot_general(lo, m, dn, preferred_element_type=f32)


def _attn_bwd(zm, rb8, do):
    def body(q_ref, k_ref, v_ref, rb_ref, do_ref, dz_ref, drb_ref, kp_ref, vp_ref, bm_ref, dk_s, dv_s, db_s):
        _att_pad_kv(k_ref, v_ref, kp_ref, vp_ref)
        _att_bias_tiles(rb_ref, bm_ref)
        dk_s[...] = jnp.zeros_like(dk_s)
        dv_s[...] = jnp.zeros_like(dv_s)
        db_s[...] = jnp.zeros_like(db_s)
        hm = _head_masks()
        scale = HDIM ** -0.5

        def pair(p, carry):
            r0 = pl.multiple_of(p * ATT_PAIR, ATT_PAIR)
            q = q_ref[pl.ds(r0, ATT_PAIR), :] * scale
            dout = do_ref[pl.ds(r0, ATT_PAIR), :]
            kb = kp_ref[pl.ds(r0, ATT_BAND), :]
            vb = vp_ref[pl.ds(r0, ATT_BAND), :]
            key_ok = (lax.broadcasted_iota(jnp.int32, (1, ATT_BAND), 1) + (r0 - ATT_PAD)) >= 0
            dq = jnp.zeros((ATT_PAIR, MIXW), f32)
            dkb = jnp.zeros((ATT_BAND, MIXW), f32)
            dvb = jnp.zeros((ATT_BAND, MIXW), f32)
            for h in range(NHEAD):
                qm = jnp.where(hm[h], q, 0.0).astype(bf16)
                dom = jnp.where(hm[h], dout, 0.0).astype(bf16)
                p_h = _att_probs(qm, kb, bm_ref[h], key_ok)
                dp = _dot_nt(dom, vb)
                ds = p_h * (dp - jnp.sum(dp * p_h, axis=-1, keepdims=True))
                dsb = ds.astype(bf16)
                dq = dq + jnp.where(hm[h], _dot(dsb, kb), 0.0)
                dkb = dkb + _dot_tn(dsb, qm)
                dvb = dvb + _dot_tn(p_h, dom)
                db_s[h] = db_s[h] + ds
            dz_ref[pl.ds(r0, ATT_PAIR), 0:MIXW] = (dq * scale).astype(bf16)
            dk_s[pl.ds(r0, ATT_BAND), :] = dk_s[pl.ds(r0, ATT_BAND), :] + dkb
            dv_s[pl.ds(r0, ATT_BAND), :] = dv_s[pl.ds(r0, ATT_BAND), :] + dvb
            return carry

        lax.fori_loop(0, SEQ // ATT_PAIR, pair, 0)
        dz_ref[:, MIXW:2 * MIXW] = dk_s[pl.ds(ATT_PAD, SEQ), :].astype(bf16)
        dz_ref[:, 2 * MIXW:3 * MIXW] = dv_s[pl.ds(ATT_PAD, SEQ), :].astype(bf16)
        _att_bias_grad(db_s, drb_ref)

    col = lambda j: pl.BlockSpec((SEQ, MIXW), lambda i: (0, j))
    return pl.pallas_call(
        body, name="attn_bwd", grid=(1,),
        in_specs=[col(0), col(1), col(2), pl.BlockSpec((8, REL_SIZE), lambda i: (0, 0)), pl.BlockSpec((SEQ, MIXW), lambda i: (0, 0))],
        out_specs=[pl.BlockSpec((SEQ, 3 * MIXW), lambda i: (0, 0)), pl.BlockSpec((8, REL_SIZE), lambda i: (0, 0))],
        out_shape=[_sds((SEQ, 3 * MIXW), bf16), _sds((8, REL_SIZE), f32)],
        scratch_shapes=[pltpu.VMEM((SEQ + ATT_PAD, MIXW), bf16), pltpu.VMEM((SEQ + ATT_PAD, MIXW), bf16),
                        pltpu.VMEM((NHEAD, ATT_PAIR, ATT_BAND), f32),
                        pltpu.VMEM((SEQ + ATT_PAD, MIXW), f32), pltpu.VMEM((SEQ + ATT_PAD, MIXW), f32),
                        pltpu.VMEM((NHEAD, ATT_PAIR, ATT_BAND), f32)],
        compiler_params=_params(("arbitrary",)),
    )(zm, zm, zm, rb8, do)


def _hgrn_out_bwd(zm3, ng, oraw3, do3):
    def body(g_ref, ng_ref, o_ref, do_ref, dor_ref, dg_ref, dng_ref):
        hm = _same_head(MIXW, HDIM, bf16)
        ngv = ng_ref[...]
        dng = jnp.zeros((1, MIXW), f32)
        for t in range(HG_T):
            o, g, d = o_ref[:, t, :], g_ref[:, t, :], do_ref[:, t, :]
            rs = lax.rsqrt(_dot_hl(o * o, hm) * (1.0 / HDIM) + EPS)
            y1 = o * rs
            dy2 = d * _silu(g)
            dg_ref[:, t, :] = (d * y1 * ngv * _dsilu(g)).astype(bf16)
            dng = dng + jnp.sum(dy2 * y1, axis=0, keepdims=True)
            dy1 = dy2 * ngv
            dor_ref[:, t, :] = rs * (dy1 - y1 * (_dot_hl(dy1 * y1, hm) * (1.0 / HDIM)))
        dng_ref[...] = jnp.broadcast_to(dng, (8, MIXW))

    blk = pl.BlockSpec((HG_N, HG_T, MIXW), lambda i: (0, 0, 0))
    return pl.pallas_call(
        body, name="hgrn_out_bwd", grid=(1,),
        in_specs=[pl.BlockSpec((HG_N, HG_T, MIXW), lambda i: (0, 0, 6)), pl.BlockSpec((1, MIXW), lambda i: (0, 0)), blk, blk],
        out_specs=[blk, blk, pl.BlockSpec((8, MIXW), lambda i: (0, 0))],
        out_shape=[_sds((HG_N, HG_T, MIXW), f32), _sds((HG_N, HG_T, MIXW), bf16), _sds((8, MIXW), f32)],
        compiler_params=_params(("arbitrary",)),
    )(zm3, ng, oraw3, do3)


def _hgrn_bwd(zm3, lb, dor3, states):
    def body(q_ref, f_ref, i_ref, lb_ref, dor_ref, st_s, dz_ref, dlb_ref,
             qf_s, kf_s, b_s, dq_s, dk_s, db_s, dv_s, w_s, x_s, cur_s):
        lb = lb_ref[...]
        hm = _same_head(MIXW, HDIM, bf16)
        hmf = _same_head(MIXW, HDIM, f32)
        b = None
        for t in range(HG_T):
            qf, kf, lf, _, _, _ = _hg_gates(q_ref[:, t, :], f_ref[:, t, :], lb)
            b = lf if b is None else b + lf
            qf_s[:, t, :] = qf
            kf_s[:, t, :] = kf
            b_s[:, t, :] = b

        def block_terms(n):
            bn = b_s[n]
            bl = bn[HG_T - 1:HG_T]
            eb = jnp.exp(bn)
            ek = jnp.exp(bl - bn)
            return qf_s[n] * eb, kf_s[n] * ek, jnp.exp(bl), eb, ek

        cur_s[...] = jnp.zeros((MIXW, MIXW), f32)
        last = lax.broadcasted_iota(jnp.int32, (HG_T, 1), 0) == HG_T - 1

        def bwd_step(j, carry):
            n = HG_N - 1 - j
            qd, kd, dec, eb, ek = block_terms(n)
            v, do_n = i_ref[n], dor_ref[n]
            dst = cur_s[...]
            st = st_s[n]
            dqd = _dot(do_n, st)
            dkd = _dot(v, dst)
            ddec = jnp.sum(dst * st.astype(f32), axis=0, keepdims=True)
            cur_s[...] = dst * dec + _dot_tn(do_n, qd) * hmf
            dq_s[n] = dqd * eb
            dk_s[n] = dkd * ek
            dv_s[n] = _dot_nt(kd, dst)
            dbl = jnp.sum(dkd * kd, axis=0, keepdims=True) + ddec * dec
            db_s[n] = dqd * qd - dkd * kd + jnp.where(last, dbl, 0.0)
            return carry

        lax.fori_loop(0, HG_N, bwd_step, 0, unroll=2)
        for t in range(HG_T):
            qt, bt, dot_t = qf_s[:, t, :], b_s[:, t, :], dor_ref[:, t, :]
            for s in range(t + 1):
                w = qt * kf_s[:, s, :]
                if s < t:
                    w = w * jnp.exp(bt - b_s[:, s, :])
                w_s[pl.ds(s * HG_N, HG_N), :] = w.astype(bf16)
                x_s[pl.ds(s * HG_N, HG_N), :] = (dot_t * i_ref[:, s, :]).astype(bf16)
            p = jnp.dot(w_s[pl.ds(0, (t + 1) * HG_N), :], hm, preferred_element_type=f32)
            dp = jnp.dot(x_s[pl.ds(0, (t + 1) * HG_N), :], hm, preferred_element_type=f32)
            dq_t = jnp.zeros((HG_N, MIXW), f32)
            db_t = jnp.zeros((HG_N, MIXW), f32)
            for s in range(t + 1):
                ps = p[s * HG_N:(s + 1) * HG_N]
                dps = dp[s * HG_N:(s + 1) * HG_N]
                ks = kf_s[:, s, :]
                dv_s[:, s, :] = dv_s[:, s, :] + ps * dot_t
                if s < t:
                    dec_ts = jnp.exp(bt - b_s[:, s, :])
                    g1 = dps * ks * dec_ts
                    dk_s[:, s, :] = dk_s[:, s, :] + dps * qt * dec_ts
                    gw = g1 * qt
                    db_t = db_t + gw
                    db_s[:, s, :] = db_s[:, s, :] - gw
                else:
                    g1 = dps * ks
                    dk_s[:, s, :] = dk_s[:, s, :] + dps * qt
                dq_t = dq_t + g1
            dq_s[:, t, :] = dq_s[:, t, :] + dq_t
            db_s[:, t, :] = db_s[:, t, :] + db_t
        run = jnp.zeros((HG_N, MIXW), f32)
        dlb = jnp.zeros((1, MIXW), f32)
        oml = 1.0 - lb
        for t in range(HG_T - 1, -1, -1):
            run = run + db_s[:, t, :]
            q = q_ref[:, t, :]
            _, _, _, sq, sg, f = _hg_gates(q, f_ref[:, t, :], lb)
            dkf = dk_s[:, t, :]
            df = jnp.where(f > LOG_FLOOR, run / f, 0.0)
            dsg = (df - dkf) * oml
            dlb = dlb + jnp.sum((df - dkf) * (1.0 - sg), axis=0, keepdims=True)
            dz_ref[:, t, 0:MIXW] = (dq_s[:, t, :] * sq * (1.0 + q * (1.0 - sq))).astype(bf16)
            dz_ref[:, t, MIXW:2 * MIXW] = (dsg * sg * (1.0 - sg)).astype(bf16)
            dz_ref[:, t, 2 * MIXW:3 * MIXW] = dv_s[:, t, :].astype(bf16)
        dlb_ref[...] = jnp.broadcast_to(dlb, (8, MIXW))

    one = pl.Buffered(1)
    col = lambda j: pl.BlockSpec((HG_N, HG_T, MIXW), lambda i: (0, 0, j), pipeline_mode=one)
    s3 = pltpu.VMEM((HG_N, HG_T, MIXW), f32)
    return pl.pallas_call(
        body, name="hgrn_bwd", grid=(1,),
        in_specs=[col(3), col(4), col(5), pl.BlockSpec((1, MIXW), lambda i: (0, 0)),
                  pl.BlockSpec((HG_N, HG_T, MIXW), lambda i: (0, 0, 0), pipeline_mode=one),
                  pl.BlockSpec((HG_N, MIXW, MIXW), lambda i: (0, 0, 0), pipeline_mode=one)],
        out_specs=[pl.BlockSpec((HG_N, HG_T, 3 * MIXW), lambda i: (0, 0, 0)), pl.BlockSpec((8, MIXW), lambda i: (0, 0))],
        out_shape=[_sds((HG_N, HG_T, 3 * MIXW), bf16), _sds((8, MIXW), f32)],
        scratch_shapes=[s3, s3, s3, s3, s3, s3, s3,
                        pltpu.VMEM((HG_T * HG_N, MIXW), bf16), pltpu.VMEM((HG_T * HG_N, MIXW), bf16),
                        pltpu.VMEM((MIXW, MIXW), f32)],
        compiler_params=_params(("arbitrary",)),
    )(zm3, zm3, zm3, lb, dor3, states)


def _gmlp_bwd(zm, ng, ws, bs8, do):
    def body(u_ref, v_ref, ng_ref, ws_ref, bs_ref, do_ref, dz_ref, dws_ref, dng_ref, dbs_ref, dm_s):
        hm = _head_masks()
        tril, wts = _gm_weights(ws_ref)
        bias = _gm_bias(bs_ref)
        ngv = ng_ref[...]
        dws_ref[...] = jnp.zeros_like(dws_ref)
        dm_s[...] = jnp.zeros_like(dm_s)

        def blk(n, dng):
            rows = pl.ds(pl.multiple_of(n * GM_T, GM_T), GM_T)
            cu, cv, d = u_ref[rows, :], v_ref[rows, :], do_ref[rows, :]
            v = _gelu(cv)
            r = lax.rsqrt(jnp.mean(v * v, axis=-1, keepdims=True) + EPS)
            vh = v * r
            vn = vh * ngv
            u = _gelu(cu)
            dm = d * u
            dmb, vnb = dm.astype(bf16), vn.astype(bf16)
            dvn = jnp.zeros((GM_T, MIXW), f32)
            for g in range(NHEAD):
                dws_ref[g] = dws_ref[g] + _dot_nt(jnp.where(hm[g], dm, 0.0), vnb)
                dvn = dvn + jnp.where(hm[g], _dot_tn(wts[g], dmb), 0.0)
            dm_s[...] = dm_s[...] + dm
            dvh = dvn * ngv
            dv = r * (dvh - vh * jnp.mean(dvh * vh, axis=-1, keepdims=True))
            dz_ref[rows, 0:MIXW] = (d * _gm_mixed(vn, wts, bias, hm) * _dgelu(cu)).astype(bf16)
            dz_ref[rows, MIXW:2 * MIXW] = (dv * _dgelu(cv)).astype(bf16)
            return dng + jnp.sum(dvn * vh, axis=0, keepdims=True)

        dng = lax.fori_loop(0, SEQ // GM_T, blk, jnp.zeros((1, MIXW), f32))
        dng_ref[...] = jnp.broadcast_to(dng, (8, MIXW))
        for g in range(NHEAD):
            dws_ref[g] = jnp.where(tril, dws_ref[g], 0.0)
        dbs_ref[...] = _dot_nt_hl(_gm_expand(), dm_s[...])

    col = lambda j: pl.BlockSpec((SEQ, MIXW), lambda i: (0, j))
    return pl.pallas_call(
        body, name="gmlp_bwd", grid=(1,),
        in_specs=[col(7), col(8), pl.BlockSpec((1, MIXW), lambda i: (0, 0)),
                  pl.BlockSpec((NHEAD, GM_T, GM_T), lambda i: (0, 0, 0)), pl.BlockSpec((8, GM_T), lambda i: (0, 0)),
                  pl.BlockSpec((SEQ, MIXW), lambda i: (0, 0))],
        out_specs=[pl.BlockSpec((SEQ, 2 * MIXW), lambda i: (0, 0)), pl.BlockSpec((NHEAD, GM_T, GM_T), lambda i: (0, 0, 0)),
                   pl.BlockSpec((8, MIXW), lambda i: (0, 0)), pl.BlockSpec((8, GM_T), lambda i: (0, 0))],
        out_shape=[_sds((SEQ, 2 * MIXW), bf16), _sds((NHEAD, GM_T, GM_T), f32), _sds((8, MIXW), f32), _sds((8, GM_T), f32)],
        scratch_shapes=[pltpu.VMEM((GM_T, MIXW), f32)],
        compiler_params=_params(("arbitrary",)),
    )(zm, zm, ng, ws, bs8, do)


def _lru_bwd(zm, cw8, cb, wa, ba, wx, bx, lam, hd, do):
    nchunk = SEQ // LRU_T

    def body(x_ref, g_ref, cw_ref, cb_ref, wa_ref, ba_ref, wx_ref, bx_ref, lam_ref, h_ref, do_ref,
             dz_ref, dwa_ref, dwx_ref, dcw_ref, dvec_ref, xp_s, xc_s, dxc_s):
        _lru_conv(x_ref, cw_ref, cb_ref, xp_s, xc_s)
        lam_v = lam_ref[...]
        sp = jax.nn.softplus(-lam_v)
        sgl = _sigmoid(-lam_v)
        wa_v, wx_v, ba_v, bx_v = wa_ref[...], wx_ref[...], ba_ref[...], bx_ref[...]
        dwa_ref[...] = jnp.zeros_like(dwa_ref)
        dwx_ref[...] = jnp.zeros_like(dwx_ref)
        dxc_s[pl.ds(SEQ, 8), :] = jnp.zeros((8, MIXW), f32)
        row = lax.broadcasted_iota(jnp.int32, (LRU_T, 1), 0)
        zero = jnp.zeros((1, MIXW), f32)

        def chunk(j, carry):
            dh_next, a_next, dba, dbx, dlam = carry
            c = nchunk - 1 - j
            rows = pl.ds(pl.multiple_of(c * LRU_T, LRU_T), LRU_T)
            prev = pl.ds(pl.multiple_of(jnp.maximum(c - 1, 0) * LRU_T, LRU_T), LRU_T)
            first = (row + c * LRU_T) == 0
            xc, gate, d, h = xc_s[rows, :], g_ref[rows, :], do_ref[rows, :], h_ref[rows, :]
            a, mult, r, ig, m2 = _lru_gates(xc, wa_v, ba_v, wx_v, bx_v, sp, first)
            h_last = jnp.where(c > 0, h_ref[prev, :][LRU_T - 1:LRU_T, :], 0.0)
            h_m1 = jnp.where(row == 0, h_last, pltpu.roll(h, 1, 0))
            a_up = jnp.where(row == LRU_T - 1, a_next, pltpu.roll(a, LRU_T - 1, 0))
            acum, dh_loc = _lru_scan(a_up, d * _gelu(gate), True)
            dh = dh_loc + acum * dh_next
            dmult = jnp.where(first, 0.0, dh * (ig * xc))
            msq = jnp.sqrt(jnp.maximum(m2, 0.0))
            dla = dh * h_m1 * a + jnp.where(m2 > 0.0, -dmult * (1.0 - m2) / msq, 0.0)
            dpr = dla * (-LRU_C) * sp * r * (1.0 - r)
            dpi = dh * mult * xc * ig * (1.0 - ig)
            dxc_s[rows, :] = dh * mult * ig + _dot_nt(dpr, wa_v) + _dot_nt(dpi, wx_v)
            dwa_ref[...] = dwa_ref[...] + _dot_tn(xc, dpr)
            dwx_ref[...] = dwx_ref[...] + _dot_tn(xc, dpi)
            dz_ref[rows, MIXW:2 * MIXW] = (d * h * _dgelu(gate)).astype(bf16)
            return (dh[0:1], a[0:1], dba + jnp.sum(dpr, axis=0, keepdims=True), dbx + jnp.sum(dpi, axis=0, keepdims=True),
                    dlam + jnp.sum(dla * r, axis=0, keepdims=True) * (LRU_C * sgl))

        _, _, dba, dbx, dlam = lax.fori_loop(0, nchunk, chunk, (zero, zero, zero, zero, zero))
        cw = cw_ref[...]
        dxc = dxc_s[pl.ds(0, SEQ), :]
        dx = dxc * cw[3:4]
        dcw = [None] * 4
        dcw[3] = jnp.sum(dxc * x_ref[...], axis=0, keepdims=True)
        for k in range(1, 4):
            dx = dx + dxc_s[pl.ds(k, SEQ), :] * cw[3 - k:4 - k]
            dcw[3 - k] = jnp.sum(dxc * xp_s[pl.ds(8 - k, SEQ), :], axis=0, keepdims=True)
        dz_ref[:, 0:MIXW] = dx.astype(bf16)
        dcw_ref[...] = jnp.concatenate(dcw + [jnp.zeros((4, MIXW), f32)], axis=0)
        dvec_ref[...] = jnp.concatenate([jnp.sum(dxc, axis=0, keepdims=True), dba, dbx, dlam, jnp.zeros((4, MIXW), f32)], axis=0)

    col = lambda j: pl.BlockSpec((SEQ, MIXW), lambda i: (0, j))
    vec = pl.BlockSpec((1, MIXW), lambda i: (0, 0))
    vec8 = pl.BlockSpec((8, MIXW), lambda i: (0, 0))
    mat = pl.BlockSpec((MIXW, MIXW), lambda i: (0, 0))
    full = pl.BlockSpec((SEQ, MIXW), lambda i: (0, 0))
    return pl.pallas_call(
        body, name="lru_bwd", grid=(1,),
        in_specs=[col(9), col(10), vec8, vec, mat, vec, mat, vec, vec, full, full],
        out_specs=[pl.BlockSpec((SEQ, 2 * MIXW), lambda i: (0, 0)), mat, mat, vec8, vec8],
        out_shape=[_sds((SEQ, 2 * MIXW), bf16), _sds((MIXW, MIXW), f32), _sds((MIXW, MIXW), f32),
                   _sds((8, MIXW), f32), _sds((8, MIXW), f32)],
        scratch_shapes=[pltpu.VMEM((SEQ + 8, MIXW), f32), pltpu.VMEM((SEQ, MIXW), f32), pltpu.VMEM((SEQ + 8, MIXW), f32)],
        compiler_params=_params(("arbitrary",)),
    )(zm, zm, cw8, cb, wa, ba, wx, bx, lam, hd, do)


def _matmul_tn(a, b, tm, tn, b_col0=0):
    m = a.shape[1]
    n = tn if b_col0 else b.shape[1]
    off = b_col0 // tn

    def body(a_ref, b_ref, o_ref):
        o_ref[...] = _dot_tn(a_ref[...], b_ref[...]).astype(bf16)

    return pl.pallas_call(
        body, name="matmul_tn", grid=(m // tm, n // tn),
        in_specs=[pl.BlockSpec((SEQ, tm), lambda i, j: (0, i)), pl.BlockSpec((SEQ, tn), lambda i, j: (0, j + off))],
        out_specs=pl.BlockSpec((tm, tn), lambda i, j: (i, j)),
        out_shape=_sds((m, n), bf16),
        compiler_params=_params(("parallel", "arbitrary")),
    )(a, b)


def _matmul_nt_norm(pairs, x, g, dres):
    tm = 1024
    steps = [a.shape[1] // t for a, _, t in pairs]
    starts = [sum(steps[:i]) for i in range(len(pairs))]
    total = sum(steps)
    npair = len(pairs)

    def body(*refs):
        a_refs, w_refs = refs[0:2 * npair:2], refs[1:2 * npair:2]
        x_ref, g_ref, dres_ref, dx_ref, dg_ref, acc_s = refs[2 * npair:]
        i, k = pl.program_id(0), pl.program_id(1)

        @pl.when(k == 0)
        def _():
            acc_s[...] = jnp.zeros_like(acc_s)

        @pl.when((i == 0) & (k == 0))
        def _():
            dg_ref[...] = jnp.zeros_like(dg_ref)

        for q in range(npair):
            @pl.when((k >= starts[q]) & (k < starts[q] + steps[q]))
            def _(q=q):
                acc_s[...] += _dot_nt(a_refs[q][...], w_refs[q][...])

        @pl.when(k == total - 1)
        def _():
            dx, dg = _rms_bwd(x_ref[...], g_ref[...], acc_s[...])
            dx_ref[...] = dres_ref[...] + dx
            dg_ref[...] += dg

    in_specs, args = [], []
    for q, (a, w, t) in enumerate(pairs):
        kmap = lambda k, q=q: jnp.clip(k - starts[q], 0, steps[q] - 1)
        in_specs += [pl.BlockSpec((tm, t), lambda i, k, kmap=kmap: (i, kmap(k))),
                     pl.BlockSpec((DM, t), lambda i, k, kmap=kmap: (0, kmap(k)))]
        args += [a, w]
    row = pl.BlockSpec((tm, DM), lambda i, k: (i, 0))
    vec = pl.BlockSpec((1, DM), lambda i, k: (0, 0))
    return pl.pallas_call(
        body, name="matmul_nt_norm", grid=(SEQ // tm, total),
        in_specs=in_specs + [row, vec, row], out_specs=[row, vec],
        out_shape=[_sds((SEQ, DM), f32), _sds((1, DM), f32)],
        scratch_shapes=[pltpu.VMEM((tm, DM), f32)],
        compiler_params=_params(("arbitrary", "arbitrary")),
    )(*args, x, g, dres)


def _merge_bwd(dx1, y, g2, outs, zg, wb, wo):
    def body(dx_ref, y_ref, g_ref, oa_ref, ob_ref, oc_ref, od_ref, zg_ref, wb_ref, wo_ref,
             da_ref, db_ref, dc_ref, dd_ref, dzg_ref, dpj_ref, dy_ref, dg_ref):
        @pl.when(pl.program_id(0) == 0)
        def _():
            dg_ref[...] = jnp.zeros_like(dg_ref)

        dy, dg = _rms_bwd(y_ref[...], g_ref[...], dx_ref[...])
        dg_ref[...] += dg
        dyb = dy.astype(bf16)
        dy_ref[...] = dyb
        dmerged = _dot_nt(dyb, wo_ref[...])
        for n, (o_ref, do_ref) in enumerate(((oa_ref, da_ref), (ob_ref, db_ref), (oc_ref, dc_ref), (od_ref, dd_ref))):
            cols = slice(n * DM, (n + 1) * DM)
            gate = _sigmoid(zg_ref[:, cols])
            proj = jnp.dot(o_ref[...], wb_ref[n], preferred_element_type=f32)
            dproj = (dmerged * gate).astype(bf16)
            dpj_ref[:, cols] = dproj
            dzg_ref[:, cols] = (dmerged * proj * gate * (1.0 - gate)).astype(bf16)
            do_ref[...] = _dot_nt(dproj, wb_ref[n])

    row = lambda w: pl.BlockSpec((ROW_TILE, w), lambda i: (i, 0))
    vec = pl.BlockSpec((1, DM), lambda i: (0, 0))
    return pl.pallas_call(
        body, name="merge_bwd", grid=(SEQ // ROW_TILE,),
        in_specs=[row(DM), row(DM), vec] + [row(MIXW)] * 4 + [row(NGATE), pl.BlockSpec((NHEAD, MIXW, DM), lambda i: (0, 0, 0)),
                                                              pl.BlockSpec((DM, DM), lambda i: (0, 0))],
        out_specs=[row(MIXW)] * 4 + [row(NGATE), row(NGATE), row(DM), vec],
        out_shape=[_sds((SEQ, MIXW), f32)] * 4 + [_sds((SEQ, NGATE), bf16), _sds((SEQ, NGATE), bf16), _sds((SEQ, DM), bf16),
                                                  _sds((1, DM), f32)],
        compiler_params=_params(("arbitrary",)),
    )(dx1, y, g2, *outs, zg, wb, wo)


def _ffn_bwd(dx2, f, g4, u, w2):
    def body(dx_ref, f_ref, g_ref, u_ref, w_ref, du_ref, a_ref, df_ref, dg_ref):
        @pl.when(pl.program_id(0) == 0)
        def _():
            dg_ref[...] = jnp.zeros_like(dg_ref)

        df, dg = _rms_bwd(f_ref[...], g_ref[...], dx_ref[...])
        dg_ref[...] += dg
        dfb = df.astype(bf16)
        df_ref[...] = dfb
        da = _dot_nt(dfb, w_ref[...])
        gt, up = u_ref[:, :FFH], u_ref[:, FFH:]
        a_ref[...] = (_silu(gt) * up).astype(bf16)
        du_ref[:, :FFH] = (da * up * _dsilu(gt)).astype(bf16)
        du_ref[:, FFH:] = (da * _silu(gt)).astype(bf16)

    row = lambda w: pl.BlockSpec((ROW_TILE, w), lambda i: (i, 0))
    vec = pl.BlockSpec((1, DM), lambda i: (0, 0))
    return pl.pallas_call(
        body, name="ffn_bwd", grid=(SEQ // ROW_TILE,),
        in_specs=[row(DM), row(DM), vec, row(2 * FFH), pl.BlockSpec((FFH, DM), lambda i: (0, 0))],
        out_specs=[row(2 * FFH), row(FFH), row(DM), vec],
        out_shape=[_sds((SEQ, 2 * FFH), bf16), _sds((SEQ, FFH), bf16), _sds((SEQ, DM), bf16), _sds((1, DM), f32)],
        compiler_params=_params(("arbitrary",)),
    )(dx2, f, g4, u, w2)


def _layer_bwd(dx2, p, sv, ffn_grads_ready=None, mix_grads_ready=None):
    du, act, df, dg4 = _ffn_bwd(dx2, sv["f"], p["g4"], sv["u"], p["w2"])
    dw2 = _matmul_tn(act, df, 1408, DM)
    dx1, dg3 = _matmul_nt_norm([(du, p["w1"], 1408)], sv["x1"], p["g3"], dx2)
    dw1 = _matmul_tn(sv["h2"], du, DM, 1408)
    g2 = p["g2"]
    if ffn_grads_ready is not None:
        g2 = g2 + ffn_grads_ready(dict(w_ffn_in=dw1, w_ffn_out=dw2), dx1)
    *dos, dzg, dproj, dy, dg2 = _merge_bwd(dx1, sv["y"], g2, sv["outs"], sv["zg"], p["wb"], p["wo"])
    dwo = _matmul_tn(sv["merged"], dy, DM, DM)
    dwb = jnp.stack([_matmul_tn(sv["outs"][n], dproj, MIXW, DM, b_col0=n * DM) if n else
                     _matmul_tn(sv["outs"][0], dproj[:, :DM], MIXW, DM) for n in range(NHEAD)])
    zm = sv["zm"]
    zm3 = zm.reshape(HG_N, HG_T, NMIX)
    dza, drb = _attn_bwd(zm, p["rb8"], dos[0])
    dor, dgb, dhng = _hgrn_out_bwd(zm3, p["hng"], sv["obraw3"], dos[1].reshape(HG_N, HG_T, MIXW))
    dzb, dlb = _hgrn_bwd(zm3, p["lb"], dor, sv["hstates"])
    dzc, dws, dgng, dbs = _gmlp_bwd(zm, p["gng"], p["gws"], p["gbs8"], dos[2])
    dzd, dwa, dwx, dcw, dvec = _lru_bwd(zm, p["cw8"], p["cb"], p["wa"], p["ba"], p["wx"], p["bx"], p["lam"], sv["hd"], dos[3])
    dzm = jnp.concatenate([dza, dzb.reshape(SEQ, 3 * MIXW), dgb.reshape(SEQ, MIXW), dzc, dzd], axis=1)
    dwin = jnp.concatenate([_matmul_tn(sv["h"], dzm, DM, 1408), _matmul_tn(sv["h"], dzg, DM, 1024)], axis=1)
    big = dict(w_in=dwin, w_branch=dwb, w_out=dwo, w_ffn_in=dw1, w_ffn_out=dw2)
    g1 = p["g1"]
    if mix_grads_ready is not None:
        g1 = g1 + mix_grads_ready(big)
    dx0, dg1 = _matmul_nt_norm([(dzm, p["wm"], 1408), (dzg, p["wgt"], 1024)], sv["x"], g1, dx1)
    small = dict(
        norm_mix_pre=dg1[0], norm_mix_post=dg2[0], norm_ffn_pre=dg3[0], norm_ffn_post=dg4[0],
        attn_rel_bias=drb[:NHEAD], lb=dlb[0], hgrn_norm_g=dhng[0], gmlp_norm_g=dgng[0], gmlp_ws=dws, gmlp_bs=dbs[:NHEAD],
        lru_conv_w=dcw[:NHEAD], lru_conv_b=dvec[0], lru_wa=_diag_blocks(dwa), lru_ba=dvec[1], lru_wx=_diag_blocks(dwx),
        lru_bx=dvec[2], lru_lambda=dvec[3],
    )
    return dx0, big, small


MIX_BIG = ("w_in", "w_branch", "w_out")
FFN_BIG = ("w_ffn_in", "w_ffn_out")
BIG = MIX_BIG + FFN_BIG
SMALL = ("norm_mix_pre", "norm_mix_post", "norm_ffn_pre", "norm_ffn_post", "attn_rel_bias", "hgrn_lb_logits", "hgrn_norm_g",
         "gmlp_norm_g", "gmlp_ws", "gmlp_bs", "lru_conv_w", "lru_conv_b", "lru_wa", "lru_ba", "lru_wx", "lru_bx", "lru_lambda")


def _local_step(x, tgt, full, small):
    lbs = _lb_fwd(small["hgrn_lb_logits"])
    params, saved = [], []
    for l in range(DEPTH):
        p = _layer_params(l, {k: full[k][l] for k in BIG}, small, lbs)
        x, sv = _layer_fwd(x, p)
        params.append(p)
        saved.append(sv)
    loss, dx = _loss_head(x, tgt)
    bigs, smalls = [None] * DEPTH, [None] * DEPTH
    for l in range(DEPTH - 1, -1, -1):
        dx, bigs[l], smalls[l] = _layer_bwd(dx, params[l], saved[l])
    gbig = {k: jnp.stack([bigs[l][k] for l in range(DEPTH)]) for k in BIG}
    gsmall = {k: jnp.stack([smalls[l][k] for l in range(DEPTH)]) for k in smalls[0]}
    gsmall["hgrn_lb_logits"] = _lb_bwd(small["hgrn_lb_logits"], gsmall.pop("lb"))
    return loss, dx, gbig, gsmall


HBM_ANY = pl.BlockSpec(memory_space=pl.ANY)


def _mesh_pos():
    return lax.axis_index("x"), lax.axis_index("y"), lax.axis_index("c")


def _all_gather(x, name):
    def body(x_ref, out_ref, send_sems, recv_sems, local_sem):
        ax, ay, ac = _mesh_pos()
        me, sibling = (ax, ay, ac), (ax, ay, 1 - ac)
        chips = [(1 - ax, ay), (ax, 1 - ay), (1 - ax, 1 - ay)]

        def slot(px, py, pc):
            return out_ref.at[4 * px + 2 * py + pc]

        def copy(k, block, to, src=None):
            return pltpu.make_async_remote_copy(
                src_ref=slot(*block) if src is None else src, dst_ref=slot(*block),
                send_sem=send_sems.at[k], recv_sem=recv_sems.at[k], device_id=to, device_id_type=MESH_ID)

        mine = pltpu.make_async_copy(x_ref, slot(*me), local_sem)
        mine.start()
        first = [copy(0, me, sibling, src=x_ref)]
        first += [copy(1 + j, me, (*chip, ac), src=x_ref) for j, chip in enumerate(chips)]
        for cp in first:
            cp.start()
        passed = [copy(4 + j, (*chip, ac), sibling) for j, chip in enumerate(chips)]
        for j, chip in enumerate(chips):
            copy(1 + j, (*chip, ac), me).wait_recv()
            passed[j].start()
        copy(0, sibling, me).wait_recv()
        for j, chip in enumerate(chips):
            copy(4 + j, (*chip, 1 - ac), me).wait_recv()
        for cp in first + passed:
            cp.wait_send()
        mine.wait()

    return pl.pallas_call(
        body, name=name, out_shape=_sds((NDEV,) + x.shape, x.dtype),
        in_specs=[HBM_ANY], out_specs=HBM_ANY,
        scratch_shapes=[pltpu.SemaphoreType.DMA((7,)), pltpu.SemaphoreType.DMA((7,)), pltpu.SemaphoreType.DMA],
    )(x)


def _exchange(g, name):
    def body(g_ref, out_ref, send_sems, recv_sems, local_sem):
        ax, ay, ac = _mesh_pos()
        me = 4 * ax + 2 * ay + ac
        mine = pltpu.make_async_copy(g_ref.at[me], out_ref.at[me], local_sem)
        mine.start()
        copies = []
        for k in range(1, NDEV):
            px = 1 - ax if k & 4 else ax
            py = 1 - ay if k & 2 else ay
            pc = 1 - ac if k & 1 else ac
            copies.append(pltpu.make_async_remote_copy(
                src_ref=g_ref.at[4 * px + 2 * py + pc], dst_ref=out_ref.at[me],
                send_sem=send_sems.at[k - 1], recv_sem=recv_sems.at[k - 1], device_id=(px, py, pc), device_id_type=MESH_ID))
        for cp in copies:
            cp.start()
        for cp in copies:
            cp.wait()
        mine.wait()

    return pl.pallas_call(
        body, name=name, out_shape=_sds(g.shape, g.dtype),
        in_specs=[HBM_ANY], out_specs=HBM_ANY,
        scratch_shapes=[pltpu.SemaphoreType.DMA((7,)), pltpu.SemaphoreType.DMA((7,)), pltpu.SemaphoreType.DMA],
    )(g)


def _peer(ax, ay, ac, k):
    return (1 - ax if k & 4 else ax, 1 - ay if k & 2 else ay, 1 - ac if k & 1 else ac)


def _handshake(peers):
    barrier = pltpu.get_barrier_semaphore()
    for peer in peers:
        pl.semaphore_signal(barrier, inc=1, device_id=peer, device_id_type=MESH_ID)
    pl.semaphore_wait(barrier, len(peers))


SEQUENCER = dict(axis_name="seq", num_cores=1)
GATHER_ID = 1
EXCHANGE_ID = 2


def _gather_sc(xs, name):
    n = len(xs)

    def body(*refs):
        srcs, outs = refs[:n], refs[n:2 * n]
        send_sems, recv_sems, local_sems = refs[2 * n:]
        ax, ay, ac = _mesh_pos()
        me, sibling = (ax, ay, ac), (ax, ay, 1 - ac)
        chips = [(1 - ax, ay), (ax, 1 - ay), (1 - ax, 1 - ay)]
        _handshake([sibling] + [(*chip, ac) for chip in chips])

        def slot(i, px, py, pc):
            return outs[i].at[4 * px + 2 * py + pc]

        def copy(i, k, block, to, src=None):
            return pltpu.make_async_remote_copy(
                src_ref=slot(i, *block) if src is None else src, dst_ref=slot(i, *block),
                send_sem=send_sems.at[7 * i + k], recv_sem=recv_sems.at[7 * i + k], device_id=to, device_id_type=MESH_ID)

        mine = [pltpu.make_async_copy(srcs[i], slot(i, *me), local_sems.at[i]) for i in range(n)]
        first = []
        for i in range(n):
            first += [copy(i, 1 + j, me, (*chip, ac), src=srcs[i]) for j, chip in enumerate(chips)]
        for i in range(n):
            first += [copy(i, 0, me, sibling, src=srcs[i])]
        for cp in first + mine:
            cp.start()
        passed = []
        for i in range(n):
            for j, chip in enumerate(chips):
                copy(i, 1 + j, (*chip, ac), me).wait_recv()
                passed.append(copy(i, 4 + j, (*chip, ac), sibling))
                passed[-1].start()
        for i in range(n):
            copy(i, 0, sibling, me).wait_recv()
            for j, chip in enumerate(chips):
                copy(i, 4 + j, (*chip, 1 - ac), me).wait_recv()
        for cp in first + passed:
            cp.wait_send()
        for cp in mine:
            cp.wait()

    return pl.kernel(
        body, name=name, out_type=[_sds((NDEV,) + x.shape, x.dtype) for x in xs],
        mesh=plsc.ScalarSubcoreMesh(**SEQUENCER),
        scratch_types=[pltpu.SemaphoreType.DMA((7 * n,)), pltpu.SemaphoreType.DMA((7 * n,)), pltpu.SemaphoreType.DMA((n,))],
        compiler_params=pltpu.CompilerParams(collective_id=GATHER_ID),
    )(*xs)


def _exchange_sc(gs, name):
    n = len(gs)

    def body(*refs):
        srcs, outs = refs[:n], refs[n:2 * n]
        send_sems, recv_sems, local_sems = refs[2 * n:]
        ax, ay, ac = _mesh_pos()
        me = 4 * ax + 2 * ay + ac
        peers = [_peer(ax, ay, ac, k) for k in range(1, NDEV)]
        _handshake(peers)
        mine = [pltpu.make_async_copy(srcs[i].at[me], outs[i].at[me], local_sems.at[i]) for i in range(n)]
        copies = []
        for i in range(n):
            for k, (px, py, pc) in enumerate(peers):
                copies.append(pltpu.make_async_remote_copy(
                    src_ref=srcs[i].at[4 * px + 2 * py + pc], dst_ref=outs[i].at[me],
                    send_sem=send_sems.at[7 * i + k], recv_sem=recv_sems.at[7 * i + k],
                    device_id=(px, py, pc), device_id_type=MESH_ID))
        for cp in copies + mine:
            cp.start()
        for cp in copies + mine:
            cp.wait()

    return pl.kernel(
        body, name=name, out_type=[_sds(g.shape, g.dtype) for g in gs],
        mesh=plsc.ScalarSubcoreMesh(**SEQUENCER),
        scratch_types=[pltpu.SemaphoreType.DMA((7 * n,)), pltpu.SemaphoreType.DMA((7 * n,)), pltpu.SemaphoreType.DMA((n,))],
        compiler_params=pltpu.CompilerParams(collective_id=EXCHANGE_ID),
    )(*gs)


HBM_SPEC = pl.BlockSpec(memory_space=pltpu.HBM)
SEM_SPEC = pl.BlockSpec(memory_space=pltpu.SEMAPHORE)
DATAFLOW = pltpu.SideEffectType.DATAFLOW_SIDE_EFFECTING


def _exchange_copies(srcs, lands, send_sems, recv_sems, local_sems):
    n = len(srcs)
    ax, ay, ac = _mesh_pos()
    me = 4 * ax + 2 * ay + ac
    copies = [pltpu.make_async_copy(srcs[i].at[me], lands[i].at[me], local_sems.at[i]) for i in range(n)]
    for i in range(n):
        for k in range(1, NDEV):
            px, py, pc = _peer(ax, ay, ac, k)
            copies.append(pltpu.make_async_remote_copy(
                src_ref=srcs[i].at[4 * px + 2 * py + pc], dst_ref=lands[i].at[me],
                send_sem=send_sems.at[7 * i + k - 1], recv_sem=recv_sems.at[7 * i + k - 1],
                device_id=(px, py, pc), device_id_type=MESH_ID))
    return copies


def _exchange_start(gs, name):
    n = len(gs)

    def body(*refs):
        srcs, lands = refs[:n], refs[n:2 * n]
        send_sems, recv_sems, local_sems = refs[2 * n:2 * n + 3]
        token = refs[-1]
        for cp in _exchange_copies(srcs, lands, send_sems, recv_sems, local_sems):
            cp.start()
        token[...] = jnp.zeros_like(token)

    hbm = [pltpu.HBM(g.shape, g.dtype) for g in gs]
    outs = pl.pallas_call(
        body, name=name,
        out_shape=(pltpu.SemaphoreType.DMA((7 * n,)), pltpu.SemaphoreType.DMA((7 * n,)), pltpu.SemaphoreType.DMA((n,)),
                   *hbm, *hbm, _sds((8, 128), f32)),
        in_specs=[HBM_SPEC] * (2 * n),
        out_specs=(SEM_SPEC, SEM_SPEC, SEM_SPEC, *[HBM_SPEC] * (2 * n), pl.BlockSpec(memory_space=pltpu.VMEM)),
        input_output_aliases={i: 3 + i for i in range(2 * n)},
        compiler_params=pltpu.CompilerParams(has_side_effects=DATAFLOW),
    )(*[pltpu.with_memory_space_constraint(g, pltpu.HBM) for g in gs],
      *[pltpu.with_memory_space_constraint(lax.empty(g.shape, g.dtype), pltpu.HBM) for g in gs])
    return outs[:-1], outs[-1]


def _exchange_wait(handles, after, name):
    n = (len(handles) - 3) // 2
    send_sems, recv_sems, local_sems = handles[:3]
    srcs, lands = handles[3:3 + n], handles[3 + n:]

    def body(*refs):
        srcs, lands = refs[:n], refs[n:2 * n]
        send_sems, recv_sems, local_sems = refs[2 * n:2 * n + 3]
        for cp in _exchange_copies(srcs, lands, send_sems, recv_sems, local_sems):
            cp.wait()

    hbm = [pltpu.HBM(g.shape, g.dtype) for g in srcs]
    outs = pl.pallas_call(
        body, name=name, out_shape=(*hbm, *hbm),
        in_specs=[HBM_SPEC] * (2 * n) + [SEM_SPEC] * 3 + [pl.BlockSpec(memory_space=pl.ANY)],
        out_specs=tuple([HBM_SPEC] * (2 * n)),
        input_output_aliases={i: i for i in range(2 * n)},
        compiler_params=pltpu.CompilerParams(has_side_effects=DATAFLOW),
    )(*srcs, *lands, send_sems, recv_sems, local_sems, after)
    return outs[n:]


def _row_tile(rows, cols):
    cap = max(8, (1 << 18) // cols)
    if rows <= cap:
        return rows
    best = None
    for t in range(8, cap + 1, 8):
        if rows % t == 0:
            best = t
    assert best is not None, (rows, cols)
    return best


def _sum_parts(parts, name):
    npart, rows, cols = parts.shape
    tr = _row_tile(rows, cols)

    def body(p_ref, o_ref):
        g = p_ref[0].astype(f32)
        for j in range(1, npart):
            g = g + p_ref[j].astype(f32)
        o_ref[...] = g

    return pl.pallas_call(
        body, name=name, grid=(rows // tr,),
        in_specs=[pl.BlockSpec((npart, tr, cols), lambda i: (0, i, 0))], out_specs=pl.BlockSpec((tr, cols), lambda i: (i, 0)),
        out_shape=_sds((rows, cols), f32), compiler_params=_params(("parallel",)),
    )(parts)


def _adamw(parts, w, m, v, name):
    npart, rows, cols = parts.shape
    tr = _row_tile(rows, cols)
    c1 = 1.0 / (1.0 - ADAM_B1 ** ADAM_STEP)
    c2 = 1.0 / (1.0 - ADAM_B2 ** ADAM_STEP)

    def body(p_ref, w_ref, m_ref, v_ref, g_ref, d_ref, mo_ref, vo_ref):
        g = p_ref[0].astype(f32)
        for j in range(1, npart):
            g = g + p_ref[j].astype(f32)
        mn = ADAM_B1 * m_ref[...] + (1.0 - ADAM_B1) * g
        vn = ADAM_B2 * v_ref[...] + (1.0 - ADAM_B2) * (g * g)
        g_ref[...] = g
        mo_ref[...] = mn
        vo_ref[...] = vn
        d_ref[...] = (-ADAM_LR) * ((mn * c1) / (jnp.sqrt(vn * c2) + ADAM_EPS) + ADAM_WD * w_ref[...])

    blk = pl.BlockSpec((tr, cols), lambda i: (i, 0))
    return pl.pallas_call(
        body, name=name, grid=(rows // tr,),
        in_specs=[pl.BlockSpec((npart, tr, cols), lambda i: (0, i, 0)), blk, blk, blk], out_specs=[blk] * 4,
        out_shape=[_sds((rows, cols), f32)] * 4, compiler_params=_params(("parallel",)),
    )(parts, w, m, v)


def _adamw_layer(parts, w, m, v, acc, l, name):
    npart, rows, cols = parts.shape
    tr = _row_tile(rows, cols)
    c1 = 1.0 / (1.0 - ADAM_B1 ** ADAM_STEP)
    c2 = 1.0 / (1.0 - ADAM_B2 ** ADAM_STEP)

    def body(p_ref, w_ref, m_ref, v_ref, *refs):
        g_ref, d_ref, mo_ref, vo_ref = refs[-4:]
        g = p_ref[0].astype(f32)
        for j in range(1, npart):
            g = g + p_ref[j].astype(f32)
        mn = ADAM_B1 * m_ref[...] + (1.0 - ADAM_B1) * g
        vn = ADAM_B2 * v_ref[...] + (1.0 - ADAM_B2) * (g * g)
        g_ref[...] = g
        mo_ref[...] = mn
        vo_ref[...] = vn
        d_ref[...] = (-ADAM_LR) * ((mn * c1) / (jnp.sqrt(vn * c2) + ADAM_EPS) + ADAM_WD * w_ref[...])

    blk = pl.BlockSpec((None, tr, cols), lambda i: (l, i, 0))
    prev = [] if acc is None else list(acc)
    return pl.pallas_call(
        body, name=name, grid=(rows // tr,),
        in_specs=[pl.BlockSpec((npart, tr, cols), lambda i: (0, i, 0)), blk, blk, blk] + [HBM_ANY] * len(prev),
        out_specs=[blk] * 4, out_shape=[_sds(w.shape, f32)] * 4,
        input_output_aliases={4 + j: j for j in range(len(prev))},
        compiler_params=_params(("parallel",)),
    )(parts, w, m, v, *prev)


def _pack(arrays):
    rows = []
    for a in arrays:
        flat = a.reshape(-1)
        pad = (-flat.shape[0]) % 1024
        rows.append(jnp.concatenate([flat, jnp.zeros((pad,), flat.dtype)]).reshape(-1, 128))
    return jnp.concatenate(rows, axis=0)


def _unpack(flat, shapes):
    out, r = [], 0
    for s in shapes:
        n = math.prod(s)
        nr = (n + 1023) // 1024 * 8
        out.append(flat[r:r + nr].reshape(-1)[:n].reshape(s))
        r += nr
    return out


BIG_SHARD_AXIS = dict(w_in=2, w_branch=3, w_out=1, w_ffn_in=2, w_ffn_out=1)
SHARDED_SMALL = ("attn_rel_bias", "lru_conv_w")


def _to_blocks(full, axis):
    s = full.shape
    cut = full.reshape(s[:axis] + (NDEV, s[axis] // NDEV) + s[axis + 1:])
    return jnp.moveaxis(cut, axis, 0)


def _from_blocks(blocks, axis):
    moved = jnp.moveaxis(blocks, 0, axis)
    s = moved.shape
    return moved.reshape(s[:axis] + (s[axis] * s[axis + 1],) + s[axis + 2:])


def _flat2(a):
    return a.reshape(-1, a.shape[-1])


def _my_slice(a, n):
    ax, ay, ac = _mesh_pos()
    return lax.dynamic_slice_in_dim(a, (4 * ax + 2 * ay + ac) * n, n, axis=a.ndim - 1)


_WEIGHTS = ("norm_mix_pre", "norm_mix_post", "norm_ffn_pre", "norm_ffn_post", "w_in", "attn_rel_bias", "hgrn_lb_logits",
            "hgrn_norm_g", "gmlp_norm_g", "gmlp_ws", "gmlp_bs", "lru_conv_w", "lru_conv_b", "lru_wa", "lru_ba", "lru_wx",
            "lru_bx", "lru_lambda", "w_branch", "w_out", "w_ffn_in", "w_ffn_out")


def _step(x, loss_target, w, m, v):
    gathered = []
    for l in range(DEPTH):
        shard = lambda k: w[k][l].astype(bf16)
        if l == 0:
            mix = (list(_gather_sc([shard("w_in")], "gather_in0"))
                   + list(_gather_sc([shard(k) for k in MIX_BIG[1:]], "gather_mix0")))
        else:
            mix = _gather_sc([shard(k) for k in MIX_BIG], "gather_mix%d" % l)
        gathered.append((mix, _gather_sc([shard(k) for k in FFN_BIG], "gather_ffn%d" % l)))
    cut = jnp.concatenate([w[k] for k in SHARDED_SMALL], axis=-1)
    parts = _all_gather(_pack([cut]), "gather_small").reshape(NDEV, -1)[:, :math.prod(cut.shape)].reshape((NDEV,) + cut.shape)
    small = {k: w[k] for k in SMALL if k not in SHARDED_SMALL}
    at = 0
    for k in SHARDED_SMALL:
        n = w[k].shape[-1]
        small[k] = _from_blocks(parts[..., at:at + n], 2)
        at += n
    loss, dx, layers = _step_forward(x, loss_target, gathered, small)
    flat3 = lambda a: a.reshape((DEPTH, -1, a.shape[-1]))
    acc = {k: None for k in BIG}
    smalls = [None] * DEPTH

    def send(grads, keys, name):
        handles, token = _exchange_start([_to_blocks(grads[k], BIG_SHARD_AXIS[k] - 1) for k in keys], "start_" + name)
        return (keys, handles, "wait_" + name), token[0:1, 0:1]

    def update(sent, l, after):
        keys, handles, name = sent
        got = dict(zip(keys, _exchange_wait(handles, after, name)))
        for k, g in got.items():
            w3 = flat3(w[k])
            acc[k] = _adamw_layer(g.reshape((NDEV,) + w3.shape[1:]), w3, flat3(m[k]), flat3(v[k]), acc[k], l,
                                  "adamw_%s_%d" % (k, l))

    waiting = []
    for l in range(DEPTH - 1, -1, -1):
        sent_ffn = []

        def ffn_grads_ready(grads, dx1, l=l, sent_ffn=sent_ffn):
            sent, zero = send(grads, FFN_BIG, "exchange_ffn%d" % l)
            sent_ffn.append(sent)
            while waiting:
                update(*waiting.pop(), dx1)
            return zero

        sent_mix = []

        def mix_grads_ready(grads, l=l, sent_mix=sent_mix):
            sent, zero = send(grads, MIX_BIG, "exchange_mix%d" % l)
            sent_mix.append(sent)
            return zero

        dx, _, smalls[l] = _step_backward(dx, layers[l], ffn_grads_ready, mix_grads_ready)
        update(sent_ffn[0], l, dx)
        waiting.append((sent_mix[0], l))
    grads, deltas, new_m, new_v = {}, {}, {}, {}
    gsmall ={k: jnp.stack([smalls[l][k] for l in range(DEPTH)]) for k in smalls[0]}
    gsmall["hgrn_lb_logits"] = _lb_bwd(small["hgrn_lb_logits"], gsmall.pop("lb"))
    shapes = [gsmall[k].shape for k in SMALL]
    sums = _unpack(_sum_parts(_all_gather(_pack([gsmall[k] for k in SMALL]), "gather_small_grads"), "sum_small_grads"), shapes)
    gs = dict(zip(SMALL, sums))
    for k in SHARDED_SMALL:
        gs[k] = _my_slice(gs[k], w[k].shape[-1])
    packed = [_pack([d[k] for k in SMALL]) for d in (gs, w, m, v)]
    outs = _adamw(packed[0][None], packed[1], packed[2], packed[3], "adamw_small")
    shapes = [w[k].shape for k in SMALL]
    for d, o in zip((grads, deltas, new_m, new_v), outs):
        d.update(zip(SMALL, _unpack(o, shapes)))
    update(*waiting.pop(), outs[1])
    for k in BIG:
        grads[k], deltas[k], new_m[k], new_v[k] = (o.reshape(w[k].shape) for o in acc[k])
    total = lax.psum(loss[0, 0], ("x", "y", "c"))
    return total, dx[None], grads, deltas, new_m, new_v


def _step_forward(x, loss_target, gathered, small):
    lbs = _lb_fwd(small["hgrn_lb_logits"])
    x = x[0]
    layers = []

    def weights(blocks, keys, after):
        if after is not None:
            blocks, _ = lax.optimization_barrier((blocks, after))
        return {k: _from_blocks(g, BIG_SHARD_AXIS[k] - 1) for k, g in zip(keys, blocks)}

    for l in range(DEPTH):
        mix, ffn = gathered[l]
        p = _layer_params(l, weights(mix, MIX_BIG, x if l else None), small, lbs)
        x, sv = _layer_fwd(x, p, lambda x1, ffn=ffn: _ffn_weights(weights(ffn, FFN_BIG, x1)))
        layers.append((p, sv))
    loss, dx = _loss_head(x, loss_target[0])
    return loss, dx, layers


def _step_backward(dx, layer, ffn_grads_ready, mix_grads_ready):
    return _layer_bwd(dx, *layer, ffn_grads_ready, mix_grads_ready)


def kernel(x, norm_mix_pre, norm_mix_post, norm_ffn_pre, norm_ffn_post, w_in, attn_rel_bias, hgrn_lb_logits, hgrn_norm_g, gmlp_norm_g, gmlp_ws, gmlp_bs, lru_conv_w, lru_conv_b, lru_wa, lru_ba, lru_wx, lru_bx, lru_lambda, w_branch, w_out, w_ffn_in, w_ffn_out, loss_target, m_norm_mix_pre, m_norm_mix_post, m_norm_ffn_pre, m_norm_ffn_post, m_w_in, m_attn_rel_bias, m_hgrn_lb_logits, m_hgrn_norm_g, m_gmlp_norm_g, m_gmlp_ws, m_gmlp_bs, m_lru_conv_w, m_lru_conv_b, m_lru_wa, m_lru_ba, m_lru_wx, m_lru_bx, m_lru_lambda, m_w_branch, m_w_out, m_w_ffn_in, m_w_ffn_out, v_norm_mix_pre, v_norm_mix_post, v_norm_ffn_pre, v_norm_ffn_post, v_w_in, v_attn_rel_bias, v_hgrn_lb_logits, v_hgrn_norm_g, v_gmlp_norm_g, v_gmlp_ws, v_gmlp_bs, v_lru_conv_w, v_lru_conv_b, v_lru_wa, v_lru_ba, v_lru_wx, v_lru_bx, v_lru_lambda, v_w_branch, v_w_out, v_w_ffn_in, v_w_ffn_out):
    w = dict(zip(_WEIGHTS, (norm_mix_pre, norm_mix_post, norm_ffn_pre, norm_ffn_post, w_in, attn_rel_bias, hgrn_lb_logits, hgrn_norm_g, gmlp_norm_g, gmlp_ws, gmlp_bs, lru_conv_w, lru_conv_b, lru_wa, lru_ba, lru_wx, lru_bx, lru_lambda, w_branch, w_out, w_ffn_in, w_ffn_out)))
    m = dict(zip(_WEIGHTS, (m_norm_mix_pre, m_norm_mix_post, m_norm_ffn_pre, m_norm_ffn_post, m_w_in, m_attn_rel_bias, m_hgrn_lb_logits, m_hgrn_norm_g, m_gmlp_norm_g, m_gmlp_ws, m_gmlp_bs, m_lru_conv_w, m_lru_conv_b, m_lru_wa, m_lru_ba, m_lru_wx, m_lru_bx, m_lru_lambda, m_w_branch, m_w_out, m_w_ffn_in, m_w_ffn_out)))
    v = dict(zip(_WEIGHTS, (v_norm_mix_pre, v_norm_mix_post, v_norm_ffn_pre, v_norm_ffn_post, v_w_in, v_attn_rel_bias, v_hgrn_lb_logits, v_hgrn_norm_g, v_gmlp_norm_g, v_gmlp_ws, v_gmlp_bs, v_lru_conv_w, v_lru_conv_b, v_lru_wa, v_lru_ba, v_lru_wx, v_lru_bx, v_lru_lambda, v_w_branch, v_w_out, v_w_ffn_in, v_w_ffn_out)))
    loss, grad_x, grads, deltas, new_m, new_v = _step(x, loss_target, w, m, v)
    return (loss, grad_x, *[grads[k] for k in _WEIGHTS], *[deltas[k] for k in _WEIGHTS],
            *[new_m[k] for k in _WEIGHTS], *[new_v[k] for k in _WEIGHTS])
```

```python
import math

import jax
import jax.numpy as jnp
from jax import lax
from jax.experimental import pallas as pl
from jax.experimental.pallas import tpu as pltpu
from jax.experimental.pallas import tpu_sc as plsc

f32 = jnp.float32
bf16 = jnp.bfloat16

SEQ = 2048
DM = 1024
DEPTH = 4
NDEV = 8
MIXW = 256
NHEAD = 4
HDIM = 64
NMIX = 11 * MIXW
NGATE = 4 * DM
FFH = 2816
EPS = 1e-6
NEG_BIG = -1e30
LOG_FLOOR = 1e-30
LRU_C = 8.0
REL_SIZE = 320
ATT_PAIR = 128
ATT_BAND = 640
ATT_PAD = 512
ATT_WV = 768
HG_T = 16
HG_N = SEQ // HG_T
GM_T = 128
LRU_T = 128
ADAM_LR, ADAM_B1, ADAM_B2, ADAM_EPS, ADAM_WD, ADAM_STEP = 0.001, 0.9, 0.999, 1e-8, 0.01, 10
V7X_VMEM_LIMIT = 56 * 1024 * 1024
GELU_C0 = math.sqrt(2.0 / math.pi)
GELU_C1 = 0.044715
MESH_ID = pl.DeviceIdType.MESH


def _params(sem=None):
    if sem is None:
        return pltpu.CompilerParams(vmem_limit_bytes=V7X_VMEM_LIMIT)
    return pltpu.CompilerParams(dimension_semantics=sem, vmem_limit_bytes=V7X_VMEM_LIMIT)


def _sds(shape, dtype):
    return jax.ShapeDtypeStruct(shape, dtype)


def _dot(a, b):
    return jnp.dot(a.astype(bf16), b.astype(bf16), preferred_element_type=f32)


def _dot_nt(a, b):
    return lax.dot_general(a.astype(bf16), b.astype(bf16), (((1,), (1,)), ((), ())), preferred_element_type=f32)


def _dot_tn(a, b):
    return lax.dot_general(a.astype(bf16), b.astype(bf16), (((0,), (0,)), ((), ())), preferred_element_type=f32)


def _split(a):
    hi = a.astype(bf16)
    lo = (a - hi.astype(f32)).astype(bf16)
    return hi, lo


def _dot_hl(a, m):
    hi, lo = _split(a)
    return jnp.dot(hi, m, preferred_element_type=f32) + jnp.dot(lo, m, preferred_element_type=f32)


def _dot_nt_hl(m, a):
    hi, lo = _split(a)
    dn = (((1,), (1,)), ((), ()))
    return lax.dot_general(m, hi, dn, preferred_element_type=f32) + lax.dot_general(m, lo, dn, preferred_element_type=f32)


def _sigmoid(x):
    return jax.nn.sigmoid(x)


def _silu(x):
    return x * _sigmoid(x)


def _dsilu(x):
    s = _sigmoid(x)
    return s * (1.0 + x * (1.0 - s))


def _gelu(x):
    return 0.5 * x * (1.0 + jnp.tanh(GELU_C0 * (x + GELU_C1 * x * x * x)))


def _dgelu(x):
    t = jnp.tanh(GELU_C0 * (x + GELU_C1 * x * x * x))
    return 0.5 * (1.0 + t) + 0.5 * x * (1.0 - t * t) * GELU_C0 * (1.0 + 3.0 * GELU_C1 * x * x)


def _rms(x, g):
    r = lax.rsqrt(jnp.mean(x * x, axis=-1, keepdims=True) + EPS)
    return x * r * g


def _rms_bwd(x, g, dy):
    r = lax.rsqrt(jnp.mean(x * x, axis=-1, keepdims=True) + EPS)
    xh = x * r
    dxh = dy * g
    dx = r * (dxh - xh * jnp.mean(dxh * xh, axis=-1, keepdims=True))
    return dx, jnp.sum(dy * xh, axis=0, keepdims=True)


def _same_head(n, width, dtype):
    r = lax.broadcasted_iota(jnp.int32, (n, n), 0) // width
    c = lax.broadcasted_iota(jnp.int32, (n, n), 1) // width
    return (r == c).astype(dtype)


def _head_masks(rows=1):
    lane = lax.broadcasted_iota(jnp.int32, (rows, MIXW), 1) // HDIM
    return [lane == h for h in range(NHEAD)]


def _norm_matmul(x, g, w, tn):
    n = w.shape[1]
    tm = 1024

    def body(x_ref, g_ref, w_ref, z_ref, h_ref):
        @pl.when(pl.program_id(1) == 0)
        def _():
            h_ref[...] = _rms(x_ref[...], g_ref[...]).astype(bf16)

        z_ref[...] = jnp.dot(h_ref[...], w_ref[...], preferred_element_type=f32)

    return pl.pallas_call(
        body, name="norm_matmul", grid=(SEQ // tm, n // tn),
        in_specs=[pl.BlockSpec((tm, DM), lambda i, j: (i, 0)), pl.BlockSpec((1, DM), lambda i, j: (0, 0)),
                  pl.BlockSpec((DM, tn), lambda i, j: (0, j))],
        out_specs=[pl.BlockSpec((tm, tn), lambda i, j: (i, j)), pl.BlockSpec((tm, DM), lambda i, j: (i, 0))],
        out_shape=[_sds((SEQ, n), f32), _sds((SEQ, DM), bf16)],
        compiler_params=_params(("parallel", "arbitrary")),
    )(x, g, w)


def _matmul(a, w, tn):
    k, n = w.shape
    tm = 1024

    def body(a_ref, w_ref, z_ref):
        z_ref[...] = jnp.dot(a_ref[...], w_ref[...], preferred_element_type=f32)

    return pl.pallas_call(
        body, name="matmul", grid=(SEQ // tm, n // tn),
        in_specs=[pl.BlockSpec((tm, k), lambda i, j: (i, 0)), pl.BlockSpec((k, tn), lambda i, j: (0, j))],
        out_specs=pl.BlockSpec((tm, tn), lambda i, j: (i, j)),
        out_shape=_sds((SEQ, n), f32),
        compiler_params=_params(("parallel", "arbitrary")),
    )(a, w)


def _att_offset_map():
    i = lax.broadcasted_iota(jnp.int32, (REL_SIZE, ATT_WV), 0)
    t = lax.broadcasted_iota(jnp.int32, (REL_SIZE, ATT_WV), 1)
    e = jnp.where(t <= ATT_BAND, t, t - ATT_WV)
    idx = jnp.clip(ATT_PAD - e, -(HDIM - 1), 256) + (HDIM - 1)
    return (idx == i).astype(bf16)


def _att_band_valid():
    qc = lax.broadcasted_iota(jnp.int32, (ATT_PAIR, ATT_BAND), 0) // HDIM
    kc = lax.broadcasted_iota(jnp.int32, (ATT_PAIR, ATT_BAND), 1) // HDIM
    return (kc >= qc) & (kc <= qc + 8)


def _att_bias_tiles(rb_ref, bm_ref):
    wv = _dot_hl(rb_ref[...], _att_offset_map())
    valid = _att_band_valid()
    for h in range(NHEAD):
        rows = jnp.broadcast_to(wv[h:h + 1, :], (ATT_PAIR, ATT_WV))
        tile = pltpu.roll(rows, 0, 1, stride=1, stride_axis=0)[:, :ATT_BAND]
        bm_ref[h] = jnp.where(valid, tile, NEG_BIG)


def _att_pad_kv(k_ref, v_ref, kp_ref, vp_ref):
    kp_ref[pl.ds(0, ATT_PAD), :] = jnp.zeros((ATT_PAD, MIXW), bf16)
    vp_ref[pl.ds(0, ATT_PAD), :] = jnp.zeros((ATT_PAD, MIXW), bf16)
    kp_ref[pl.ds(ATT_PAD, SEQ), :] = k_ref[...].astype(bf16)
    vp_ref[pl.ds(ATT_PAD, SEQ), :] = v_ref[...].astype(bf16)


def _att_probs(qm, kb, bm, key_ok):
    s = _dot_nt(qm, kb) + bm
    s = jnp.where(key_ok, s, NEG_BIG)
    m = jnp.max(s, axis=-1, keepdims=True)
    e = jnp.exp(s - m)
    return e / jnp.sum(e, axis=-1, keepdims=True)


def _attn_fwd(zm, rb8):
    def body(q_ref, k_ref, v_ref, rb_ref, o_ref, kp_ref, vp_ref, bm_ref):
        _att_pad_kv(k_ref, v_ref, kp_ref, vp_ref)
        _att_bias_tiles(rb_ref, bm_ref)
        hm = _head_masks()

        def pair(p, carry):
            r0 = pl.multiple_of(p * ATT_PAIR, ATT_PAIR)
            q = q_ref[pl.ds(r0, ATT_PAIR), :] * (HDIM ** -0.5)
            kb = kp_ref[pl.ds(r0, ATT_BAND), :]
            vb = vp_ref[pl.ds(r0, ATT_BAND), :]
            key_ok = (lax.broadcasted_iota(jnp.int32, (1, ATT_BAND), 1) + (r0 - ATT_PAD)) >= 0
            o = jnp.zeros((ATT_PAIR, MIXW), f32)
            for h in range(NHEAD):
                qm = jnp.where(hm[h], q, 0.0)
                p_h = _att_probs(qm, kb, bm_ref[h], key_ok)
                o = o + jnp.where(hm[h], _dot(p_h, vb), 0.0)
            o_ref[pl.ds(r0, ATT_PAIR), :] = o.astype(bf16)
            return carry

        lax.fori_loop(0, SEQ // ATT_PAIR, pair, 0)

    col = lambda j: pl.BlockSpec((SEQ, MIXW), lambda i: (0, j))
    return pl.pallas_call(
        body, name="attn_fwd", grid=(1,),
        in_specs=[col(0), col(1), col(2), pl.BlockSpec((8, REL_SIZE), lambda i: (0, 0))],
        out_specs=pl.BlockSpec((SEQ, MIXW), lambda i: (0, 0)),
        out_shape=_sds((SEQ, MIXW), bf16),
        scratch_shapes=[pltpu.VMEM((SEQ + ATT_PAD, MIXW), bf16), pltpu.VMEM((SEQ + ATT_PAD, MIXW), bf16),
                        pltpu.VMEM((NHEAD, ATT_PAIR, ATT_BAND), f32)],
        compiler_params=_params(("arbitrary",)),
    )(zm, zm, zm, rb8)


def _hg_gates(q, fz, lb):
    sq = _sigmoid(q)
    sg = _sigmoid(fz)
    f = lb + (1.0 - lb) * sg
    return q * sq, (1.0 - lb) * (1.0 - sg), jnp.log(jnp.maximum(f, LOG_FLOOR)), sq, sg, f


def _hg_prepare(q_ref, f_ref, lb, qf_s, kf_s, b_s, qd_s, kd_s, dec_s):
    b = None
    for t in range(HG_T):
        qf, kf, lf, _, _, _ = _hg_gates(q_ref[:, t, :], f_ref[:, t, :], lb)
        b = lf if b is None else b + lf
        qf_s[:, t, :] = qf
        kf_s[:, t, :] = kf
        b_s[:, t, :] = b
    b_last = b
    dec_s[...] = jnp.broadcast_to(jnp.exp(b_last)[:, None, :], (HG_N, 8, MIXW))
    for t in range(HG_T):
        bt = b_s[:, t, :]
        qd_s[:, t, :] = qf_s[:, t, :] * jnp.exp(bt)
        kd_s[:, t, :] = kf_s[:, t, :] * jnp.exp(b_last - bt)


def _hg_scores(t, qf_s, kf_s, b_s, w_s, hm):
    qt = qf_s[:, t, :]
    bt = b_s[:, t, :]
    for s in range(t + 1):
        w = qt * kf_s[:, s, :]
        if s < t:
            w = w * jnp.exp(bt - b_s[:, s, :])
        w_s[pl.ds(s * HG_N, HG_N), :] = w.astype(bf16)
    return jnp.dot(w_s[pl.ds(0, (t + 1) * HG_N), :], hm, preferred_element_type=f32)


def _hgrn_fwd(zm3, lb, ng):
    def body(q_ref, f_ref, i_ref, g_ref, lb_ref, ng_ref, o_ref, oraw_ref, states_ref,
             qf_s, kf_s, b_s, qd_s, kd_s, dec_s, w_s, st_s):
        lb = lb_ref[...]
        hm = _same_head(MIXW, HDIM, bf16)
        hmf = _same_head(MIXW, HDIM, f32)
        _hg_prepare(q_ref, f_ref, lb, qf_s, kf_s, b_s, qd_s, kd_s, dec_s)
        for t in range(HG_T):
            p = _hg_scores(t, qf_s, kf_s, b_s, w_s, hm)
            acc = jnp.zeros((HG_N, MIXW), f32)
            for s in range(t + 1):
                acc = acc + p[s * HG_N:(s + 1) * HG_N] * i_ref[:, s, :]
            oraw_ref[:, t, :] = acc
        st_s[...] = jnp.zeros((MIXW, MIXW), f32)

        def step(n, carry):
            st = st_s[...]
            stb = st.astype(bf16)
            states_ref[n] = stb
            oraw_ref[n] = oraw_ref[n] + _dot_nt(qd_s[n], stb)
            st_s[...] = st * dec_s[n][0:1] + _dot_tn(i_ref[n], kd_s[n]) * hmf
            return carry

        lax.fori_loop(0, HG_N, step, 0, unroll=2)
        ngv = ng_ref[...]
        for t in range(HG_T):
            o = oraw_ref[:, t, :]
            ms = _dot_hl(o * o, hm) * (1.0 / HDIM)
            o_ref[:, t, :] = (o * lax.rsqrt(ms + EPS) * ngv * _silu(g_ref[:, t, :])).astype(bf16)

    one = pl.Buffered(1)
    col = lambda j: pl.BlockSpec((HG_N, HG_T, MIXW), lambda i: (0, 0, j), pipeline_mode=one)
    vec = pl.BlockSpec((1, MIXW), lambda i: (0, 0))
    blk = pl.BlockSpec((HG_N, HG_T, MIXW), lambda i: (0, 0, 0))
    s3 = pltpu.VMEM((HG_N, HG_T, MIXW), f32)
    return pl.pallas_call(
        body, name="hgrn_fwd", grid=(1,),
        in_specs=[col(3), col(4), col(5), col(6), vec, vec],
        out_specs=[blk, blk, pl.BlockSpec((HG_N, MIXW, MIXW), lambda i: (0, 0, 0), pipeline_mode=one)],
        out_shape=[_sds((HG_N, HG_T, MIXW), bf16), _sds((HG_N, HG_T, MIXW), f32), _sds((HG_N, MIXW, MIXW), bf16)],
        scratch_shapes=[s3, s3, s3, s3, s3, pltpu.VMEM((HG_N, 8, MIXW), f32),
                        pltpu.VMEM((HG_T * HG_N, MIXW), bf16), pltpu.VMEM((MIXW, MIXW), f32)],
        compiler_params=_params(("arbitrary",)),
    )(zm3, zm3, zm3, zm3, lb, ng)


def _gm_weights(ws_ref):
    tril = lax.broadcasted_iota(jnp.int32, (GM_T, GM_T), 0) >= lax.broadcasted_iota(jnp.int32, (GM_T, GM_T), 1)
    return tril, [jnp.where(tril, ws_ref[g], 0.0).astype(bf16) for g in range(NHEAD)]


def _gm_expand():
    r = lax.broadcasted_iota(jnp.int32, (8, MIXW), 0)
    c = lax.broadcasted_iota(jnp.int32, (8, MIXW), 1) // HDIM
    return (r == c).astype(bf16)


def _gm_mixed(vn, wts, bias, hm):
    vb = vn.astype(bf16)
    mixed = bias
    for g in range(NHEAD):
        mixed = mixed + jnp.where(hm[g], jnp.dot(wts[g], vb, preferred_element_type=f32), 0.0)
    return mixed


def _gm_bias(bs_ref):
    hi, lo = _split(bs_ref[...])
    et = _gm_expand()
    dn = (((0,), (0,)), ((), ()))
    return lax.dot_general(hi, et, dn, preferred_element_type=f32) + lax.dot_general(lo, et, dn, preferred_element_type=f32)


def _gmlp_fwd(zm, ng, ws, bs8):
    def body(u_ref, v_ref, ng_ref, ws_ref, bs_ref, o_ref):
        hm = _head_masks()
        _, wts = _gm_weights(ws_ref)
        bias = _gm_bias(bs_ref)
        ngv = ng_ref[...]

        def blk(n, carry):
            rows = pl.ds(pl.multiple_of(n * GM_T, GM_T), GM_T)
            vn = _rms(_gelu(v_ref[rows, :]), ngv)
            o_ref[rows, :] = (_gelu(u_ref[rows, :]) * _gm_mixed(vn, wts, bias, hm)).astype(bf16)
            return carry

        lax.fori_loop(0, SEQ // GM_T, blk, 0)

    col = lambda j: pl.BlockSpec((SEQ, MIXW), lambda i: (0, j))
    return pl.pallas_call(
        body, name="gmlp_fwd", grid=(1,),
        in_specs=[col(7), col(8), pl.BlockSpec((1, MIXW), lambda i: (0, 0)),
                  pl.BlockSpec((NHEAD, GM_T, GM_T), lambda i: (0, 0, 0)), pl.BlockSpec((8, GM_T), lambda i: (0, 0))],
        out_specs=pl.BlockSpec((SEQ, MIXW), lambda i: (0, 0)),
        out_shape=_sds((SEQ, MIXW), bf16),
        compiler_params=_params(("arbitrary",)),
    )(zm, zm, ng, ws, bs8)


def _lru_conv(x_ref, cw_ref, cb_ref, xp_s, xc_s):
    xp_s[pl.ds(0, 8), :] = jnp.zeros((8, MIXW), f32)
    xp_s[pl.ds(8, SEQ), :] = x_ref[...]
    cw = cw_ref[...]
    xc = cb_ref[...] + x_ref[...] * cw[3:4]
    for k in range(1, 4):
        xc = xc + xp_s[pl.ds(8 - k, SEQ), :] * cw[3 - k:4 - k]
    xc_s[...] = xc


def _lru_gates(xc, wa, ba, wx, bx, sp, first_row):
    r = _sigmoid(_dot(xc, wa) + ba)
    ig = _sigmoid(_dot(xc, wx) + bx)
    la = (-LRU_C) * r * sp
    a = jnp.exp(la)
    th = jnp.tanh(la)
    m2 = -2.0 * th / (1.0 - th)
    mult = jnp.where(first_row, 1.0, jnp.sqrt(jnp.maximum(m2, 0.0)))
    return a, mult, r, ig, m2


def _lru_scan(a, b, rev):
    row = lax.broadcasted_iota(jnp.int32, (LRU_T, 1), 0)
    k = 1
    while k < LRU_T:
        ok = (row < LRU_T - k) if rev else (row >= k)
        sh = (LRU_T - k) if rev else k
        a_sh = jnp.where(ok, pltpu.roll(a, sh, 0), 1.0)
        b_sh = jnp.where(ok, pltpu.roll(b, sh, 0), 0.0)
        b = b + a * b_sh
        a = a * a_sh
        k *= 2
    return a, b


def _lru_fwd(zm, cw8, cb, wa, ba, wx, bx, lam):
    def body(x_ref, g_ref, cw_ref, cb_ref, wa_ref, ba_ref, wx_ref, bx_ref, lam_ref, o_ref, h_ref, xp_s, xc_s):
        _lru_conv(x_ref, cw_ref, cb_ref, xp_s, xc_s)
        sp = jax.nn.softplus(-lam_ref[...])
        wa_v, wx_v, ba_v, bx_v = wa_ref[...], wx_ref[...], ba_ref[...], bx_ref[...]

        def chunk(c, h_prev):
            rows = pl.ds(pl.multiple_of(c * LRU_T, LRU_T), LRU_T)
            first = (lax.broadcasted_iota(jnp.int32, (LRU_T, 1), 0) + c * LRU_T) == 0
            xc = xc_s[rows, :]
            a, mult, _, ig, _ = _lru_gates(xc, wa_v, ba_v, wx_v, bx_v, sp, first)
            acum, hloc = _lru_scan(a, mult * (ig * xc), False)
            h = hloc + acum * h_prev
            h_ref[rows, :] = h
            o_ref[rows, :] = (h * _gelu(g_ref[rows, :])).astype(bf16)
            return h[LRU_T - 1:LRU_T, :]

        lax.fori_loop(0, SEQ // LRU_T, chunk, jnp.zeros((1, MIXW), f32))

    col = lambda j: pl.BlockSpec((SEQ, MIXW), lambda i: (0, j))
    vec = pl.BlockSpec((1, MIXW), lambda i: (0, 0))
    mat = pl.BlockSpec((MIXW, MIXW), lambda i: (0, 0))
    out = pl.BlockSpec((SEQ, MIXW), lambda i: (0, 0))
    return pl.pallas_call(
        body, name="lru_fwd", grid=(1,),
        in_specs=[col(9), col(10), pl.BlockSpec((8, MIXW), lambda i: (0, 0)), vec, mat, vec, mat, vec, vec],
        out_specs=[out, out],
        out_shape=[_sds((SEQ, MIXW), bf16), _sds((SEQ, MIXW), f32)],
        scratch_shapes=[pltpu.VMEM((SEQ + 8, MIXW), f32), pltpu.VMEM((SEQ, MIXW), f32)],
        compiler_params=_params(("arbitrary",)),
    )(zm, zm, cw8, cb, wa, ba, wx, bx, lam)


def _block_diag(w):
    out = jnp.zeros((MIXW, MIXW), w.dtype)
    for h in range(NHEAD):
        out = lax.dynamic_update_slice(out, w[h], (h * HDIM, h * HDIM))
    return out


def _diag_blocks(w):
    return jnp.stack([w[h * HDIM:(h + 1) * HDIM, h * HDIM:(h + 1) * HDIM] for h in range(NHEAD)])


ROW_TILE = 256


def _merge_fwd(outs, zg, wb, wo, x, g2):
    def body(oa_ref, ob_ref, oc_ref, od_ref, zg_ref, wb_ref, wo_ref, x_ref, g_ref, xo_ref, mg_ref, y_ref):
        merged = jnp.zeros((ROW_TILE, DM), f32)
        for n, o_ref in enumerate((oa_ref, ob_ref, oc_ref, od_ref)):
            proj = jnp.dot(o_ref[...], wb_ref[n], preferred_element_type=f32)
            merged = merged + _sigmoid(zg_ref[:, n * DM:(n + 1) * DM]) * proj
        mb = merged.astype(bf16)
        y = jnp.dot(mb, wo_ref[...], preferred_element_type=f32)
        mg_ref[...] = mb
        y_ref[...] = y
        xo_ref[...] = x_ref[...] + _rms(y, g_ref[...])

    row = lambda w: pl.BlockSpec((ROW_TILE, w), lambda i: (i, 0))
    return pl.pallas_call(
        body, name="merge_fwd", grid=(SEQ // ROW_TILE,),
        in_specs=[row(MIXW)] * 4 + [row(NGATE), pl.BlockSpec((NHEAD, MIXW, DM), lambda i: (0, 0, 0)),
                                    pl.BlockSpec((DM, DM), lambda i: (0, 0)), row(DM), pl.BlockSpec((1, DM), lambda i: (0, 0))],
        out_specs=[row(DM), row(DM), row(DM)],
        out_shape=[_sds((SEQ, DM), f32), _sds((SEQ, DM), bf16), _sds((SEQ, DM), f32)],
        compiler_params=_params(("parallel",)),
    )(*outs, zg, wb, wo, x, g2)


def _ffn_out(u, w2, x, g4):
    def body(u_ref, w_ref, x_ref, g_ref, xo_ref, f_ref):
        a = _silu(u_ref[:, :FFH]) * u_ref[:, FFH:]
        f = jnp.dot(a.astype(bf16), w_ref[...], preferred_element_type=f32)
        f_ref[...] = f
        xo_ref[...] = x_ref[...] + _rms(f, g_ref[...])

    row = lambda w: pl.BlockSpec((ROW_TILE, w), lambda i: (i, 0))
    return pl.pallas_call(
        body, name="ffn_out", grid=(SEQ // ROW_TILE,),
        in_specs=[row(2 * FFH), pl.BlockSpec((FFH, DM), lambda i: (0, 0)), row(DM), pl.BlockSpec((1, DM), lambda i: (0, 0))],
        out_specs=[row(DM), row(DM)],
        out_shape=[_sds((SEQ, DM), f32), _sds((SEQ, DM), f32)],
        compiler_params=_params(("parallel",)),
    )(u, w2, x, g4)


def _loss_head(x, tgt):
    tm = 512

    def body(x_ref, t_ref, l_ref, dx_ref):
        @pl.when(pl.program_id(0) == 0)
        def _():
            l_ref[...] = jnp.zeros((1, 1), f32)

        d = x_ref[...] - t_ref[...]
        dx_ref[...] = d * (1.0 / DM)
        l_ref[...] += (0.5 / DM) * jnp.sum(d * d).reshape(1, 1)

    row = pl.BlockSpec((tm, DM), lambda i: (i, 0))
    return pl.pallas_call(
        body, name="loss_head", grid=(SEQ // tm,),
        in_specs=[row, row], out_specs=[pl.BlockSpec((1, 1), lambda i: (0, 0)), row],
        out_shape=[_sds((1, 1), f32), _sds((SEQ, DM), f32)],
        compiler_params=_params(("arbitrary",)),
    )(x, tgt)


def _lb_fwd(logits):
    def body(lg_ref, o_ref):
        lg = lg_ref[...]
        e = jnp.exp(lg - jnp.max(lg, axis=0, keepdims=True))
        p = e / jnp.sum(e, axis=0, keepdims=True)
        acc = jnp.zeros((1, MIXW), f32)
        o_ref[0:1, :] = acc
        for l in range(1, DEPTH):
            acc = acc + p[l:l + 1]
            o_ref[l:l + 1, :] = acc

    return pl.pallas_call(body, name="lb_fwd", out_shape=_sds((DEPTH, MIXW), f32))(logits)


def _lb_bwd(logits, dlbs):
    def body(lg_ref, d_ref, o_ref):
        lg = lg_ref[...]
        e = jnp.exp(lg - jnp.max(lg, axis=0, keepdims=True))
        p = e / jnp.sum(e, axis=0, keepdims=True)
        d = d_ref[...]
        dp = [jnp.zeros((1, MIXW), f32)] * DEPTH
        acc = jnp.zeros((1, MIXW), f32)
        for j in range(DEPTH - 1, 0, -1):
            acc = acc + d[j:j + 1]
            dp[j] = acc
        inner = sum(p[j:j + 1] * dp[j] for j in range(DEPTH))
        for j in range(DEPTH):
            o_ref[j:j + 1, :] = p[j:j + 1] * (dp[j] - inner)

    return pl.pallas_call(body, name="lb_bwd", out_shape=_sds((DEPTH, MIXW), f32))(logits, dlbs)


def _pad_rows(a, rows=8):
    return jnp.concatenate([a, jnp.zeros((rows - a.shape[0], a.shape[1]), a.dtype)], axis=0)


def _layer_params(l, full, small, lbs):
    row = lambda name: small[name][l][None]
    return dict(
        _mix_weights(full), **(_ffn_weights(full) if "w_ffn_in" in full else {}),
        g1=row("norm_mix_pre"), g2=row("norm_mix_post"), g3=row("norm_ffn_pre"), g4=row("norm_ffn_post"),
        rb8=_pad_rows(small["attn_rel_bias"][l]), lb=lbs[l][None], hng=row("hgrn_norm_g"),
        gng=row("gmlp_norm_g"), gws=small["gmlp_ws"][l], gbs8=_pad_rows(small["gmlp_bs"][l]),
        cw8=_pad_rows(small["lru_conv_w"][l]), cb=row("lru_conv_b"),
        wa=_block_diag(small["lru_wa"][l]).astype(bf16), ba=row("lru_ba"),
        wx=_block_diag(small["lru_wx"][l]).astype(bf16), bx=row("lru_bx"), lam=row("lru_lambda"),
    )


def _mix_weights(full):
    return dict(wm=full["w_in"][:, :NMIX], wgt=full["w_in"][:, NMIX:], wb=full["w_branch"], wo=full["w_out"])


def _ffn_weights(full):
    return dict(w1=full["w_ffn_in"], w2=full["w_ffn_out"])


def _layer_fwd(x, p, late_ffn_weights=None):
    zm, h = _norm_matmul(x, p["g1"], p["wm"], 1408)
    zg = _matmul(h, p["wgt"], 1024)
    oa = _attn_fwd(zm, p["rb8"])
    ob3, obraw3, hstates = _hgrn_fwd(zm.reshape(HG_N, HG_T, NMIX), p["lb"], p["hng"])
    oc = _gmlp_fwd(zm, p["gng"], p["gws"], p["gbs8"])
    od, hd = _lru_fwd(zm, p["cw8"], p["cb"], p["wa"], p["ba"], p["wx"], p["bx"], p["lam"])
    outs = (oa, ob3.reshape(SEQ, MIXW), oc, od)
    x1, merged, y = _merge_fwd(outs, zg, p["wb"], p["wo"], x, p["g2"])
    if late_ffn_weights is not None:
        p.update(late_ffn_weights(x1))
    u, h2 = _norm_matmul(x1, p["g3"], p["w1"], 1408)
    x2, f = _ffn_out(u, p["w2"], x1, p["g4"])
    saved = dict(x=x, h=h, zm=zm, zg=zg, outs=outs, obraw3=obraw3, hstates=hstates, hd=hd, x1=x1, merged=merged, y=y, u=u, h2=h2, f=f)
    return x2, saved


def _att_bias_grad(db_ref, o_ref):
    r = lax.broadcasted_iota(jnp.int32, (ATT_PAIR, ATT_PAIR), 0)
    c = lax.broadcasted_iota(jnp.int32, (ATT_PAIR, ATT_PAIR), 1)
    flip = (r + c == ATT_PAIR - 1).astype(bf16)
    rows = []
    for h in range(NHEAD):
        d = jnp.concatenate([db_ref[h], jnp.zeros((ATT_PAIR, ATT_WV - ATT_BAND), f32)], axis=1)
        hi, lo = _split(d)
        rev = jnp.dot(flip, hi, preferred_element_type=f32) + jnp.dot(flip, lo, preferred_element_type=f32)
        lined = pltpu.roll(rev, ATT_WV - (ATT_PAIR - 1), 1, stride=1, stride_axis=0)
        rows.append(jnp.sum(lined, axis=0, keepdims=True))
    dwv = jnp.concatenate(rows + [jnp.zeros((8 - NHEAD, ATT_WV), f32)], axis=0)
    hi, lo = _split(dwv)
    m = _att_offset_map()
    dn = (((1,), (1,)), ((), ()))
    o_ref[...] = lax.dot_general(hi, m, dn, preferred_element_type=f32) + lax.dot_general(lo, m, dn, preferred_element_type=f32)


def _attn_bwd(zm, rb8, do):
    def body(q_ref, k_ref, v_ref, rb_ref, do_ref, dz_ref, drb_ref, kp_ref, vp_ref, bm_ref, dk_s, dv_s, db_s):
        _att_pad_kv(k_ref, v_ref, kp_ref, vp_ref)
        _att_bias_tiles(rb_ref, bm_ref)
        dk_s[...] = jnp.zeros_like(dk_s)
        dv_s[...] = jnp.zeros_like(dv_s)
        db_s[...] = jnp.zeros_like(db_s)
        hm = _head_masks()
        scale = HDIM ** -0.5

        def pair(p, carry):
            r0 = pl.multiple_of(p * ATT_PAIR, ATT_PAIR)
            q = q_ref[pl.ds(r0, ATT_PAIR), :] * scale
            dout = do_ref[pl.ds(r0, ATT_PAIR), :]
            kb = kp_ref[pl.ds(r0, ATT_BAND), :]
            vb = vp_ref[pl.ds(r0, ATT_BAND), :]
            key_ok = (lax.broadcasted_iota(jnp.int32, (1, ATT_BAND), 1) + (r0 - ATT_PAD)) >= 0
            dq = jnp.zeros((ATT_PAIR, MIXW), f32)
            dkb = jnp.zeros((ATT_BAND, MIXW), f32)
            dvb = jnp.zeros((ATT_BAND, MIXW), f32)
            for h in range(NHEAD):
                qm = jnp.where(hm[h], q, 0.0).astype(bf16)
                dom = jnp.where(hm[h], dout, 0.0).astype(bf16)
                p_h = _att_probs(qm, kb, bm_ref[h], key_ok)
                dp = _dot_nt(dom, vb)
                ds = p_h * (dp - jnp.sum(dp * p_h, axis=-1, keepdims=True))
                dsb = ds.astype(bf16)
                dq = dq + jnp.where(hm[h], _dot(dsb, kb), 0.0)
                dkb = dkb + _dot_tn(dsb, qm)
                dvb = dvb + _dot_tn(p_h, dom)
                db_s[h] = db_s[h] + ds
            dz_ref[pl.ds(r0, ATT_PAIR), 0:MIXW] = (dq * scale).astype(bf16)
            dk_s[pl.ds(r0, ATT_BAND), :] = dk_s[pl.ds(r0, ATT_BAND), :] + dkb
            dv_s[pl.ds(r0, ATT_BAND), :] = dv_s[pl.ds(r0, ATT_BAND), :] + dvb
            return carry

        lax.fori_loop(0, SEQ // ATT_PAIR, pair, 0)
        dz_ref[:, MIXW:2 * MIXW] = dk_s[pl.ds(ATT_PAD, SEQ), :].astype(bf16)
        dz_ref[:, 2 * MIXW:3 * MIXW] = dv_s[pl.ds(ATT_PAD, SEQ), :].astype(bf16)
        _att_bias_grad(db_s, drb_ref)

    col = lambda j: pl.BlockSpec((SEQ, MIXW), lambda i: (0, j))
    return pl.pallas_call(
        body, name="attn_bwd", grid=(1,),
        in_specs=[col(0), col(1), col(2), pl.BlockSpec((8, REL_SIZE), lambda i: (0, 0)), pl.BlockSpec((SEQ, MIXW), lambda i: (0, 0))],
        out_specs=[pl.BlockSpec((SEQ, 3 * MIXW), lambda i: (0, 0)), pl.BlockSpec((8, REL_SIZE), lambda i: (0, 0))],
        out_shape=[_sds((SEQ, 3 * MIXW), bf16), _sds((8, REL_SIZE), f32)],
        scratch_shapes=[pltpu.VMEM((SEQ + ATT_PAD, MIXW), bf16), pltpu.VMEM((SEQ + ATT_PAD, MIXW), bf16),
                        pltpu.VMEM((NHEAD, ATT_PAIR, ATT_BAND), f32),
                        pltpu.VMEM((SEQ + ATT_PAD, MIXW), f32), pltpu.VMEM((SEQ + ATT_PAD, MIXW), f32),
                        pltpu.VMEM((NHEAD, ATT_PAIR, ATT_BAND), f32)],
        compiler_params=_params(("arbitrary",)),
    )(zm, zm, zm, rb8, do)


def _hgrn_out_bwd(zm3, ng, oraw3, do3):
    def body(g_ref, ng_ref, o_ref, do_ref, dor_ref, dg_ref, dng_ref):
        hm = _same_head(MIXW, HDIM, bf16)
        ngv = ng_ref[...]
        dng = jnp.zeros((1, MIXW), f32)
        for t in range(HG_T):
            o, g, d = o_ref[:, t, :], g_ref[:, t, :], do_ref[:, t, :]
            rs = lax.rsqrt(_dot_hl(o * o, hm) * (1.0 / HDIM) + EPS)
            y1 = o * rs
            dy2 = d * _silu(g)
            dg_ref[:, t, :] = (d * y1 * ngv * _dsilu(g)).astype(bf16)
            dng = dng + jnp.sum(dy2 * y1, axis=0, keepdims=True)
            dy1 = dy2 * ngv
            dor_ref[:, t, :] = rs * (dy1 - y1 * (_dot_hl(dy1 * y1, hm) * (1.0 / HDIM)))
        dng_ref[...] = jnp.broadcast_to(dng, (8, MIXW))

    blk = pl.BlockSpec((HG_N, HG_T, MIXW), lambda i: (0, 0, 0))
    return pl.pallas_call(
        body, name="hgrn_out_bwd", grid=(1,),
        in_specs=[pl.BlockSpec((HG_N, HG_T, MIXW), lambda i: (0, 0, 6)), pl.BlockSpec((1, MIXW), lambda i: (0, 0)), blk, blk],
        out_specs=[blk, blk, pl.BlockSpec((8, MIXW), lambda i: (0, 0))],
        out_shape=[_sds((HG_N, HG_T, MIXW), f32), _sds((HG_N, HG_T, MIXW), bf16), _sds((8, MIXW), f32)],
        compiler_params=_params(("arbitrary",)),
    )(zm3, ng, oraw3, do3)


def _hgrn_bwd(zm3, lb, dor3, states):
    def body(q_ref, f_ref, i_ref, lb_ref, dor_ref, st_s, dz_ref, dlb_ref,
             qf_s, kf_s, b_s, dq_s, dk_s, db_s, dv_s, w_s, x_s, cur_s):
        lb = lb_ref[...]
        hm = _same_head(MIXW, HDIM, bf16)
        hmf = _same_head(MIXW, HDIM, f32)
        b = None
        for t in range(HG_T):
            qf, kf, lf, _, _, _ = _hg_gates(q_ref[:, t, :], f_ref[:, t, :], lb)
            b = lf if b is None else b + lf
            qf_s[:, t, :] = qf
            kf_s[:, t, :] = kf
            b_s[:, t, :] = b

        def block_terms(n):
            bn = b_s[n]
            bl = bn[HG_T - 1:HG_T]
            eb = jnp.exp(bn)
            ek = jnp.exp(bl - bn)
            return qf_s[n] * eb, kf_s[n] * ek, jnp.exp(bl), eb, ek

        cur_s[...] = jnp.zeros((MIXW, MIXW), f32)
        last = lax.broadcasted_iota(jnp.int32, (HG_T, 1), 0) == HG_T - 1

        def bwd_step(j, carry):
            n = HG_N - 1 - j
            qd, kd, dec, eb, ek = block_terms(n)
            v, do_n = i_ref[n], dor_ref[n]
            dst = cur_s[...]
            st = st_s[n]
            dqd = _dot(do_n, st)
            dkd = _dot(v, dst)
            ddec = jnp.sum(dst * st.astype(f32), axis=0, keepdims=True)
            cur_s[...] = dst * dec + _dot_tn(do_n, qd) * hmf
            dq_s[n] = dqd * eb
            dk_s[n] = dkd * ek
            dv_s[n] = _dot_nt(kd, dst)
            dbl = jnp.sum(dkd * kd, axis=0, keepdims=True) + ddec * dec
            db_s[n] = dqd * qd - dkd * kd + jnp.where(last, dbl, 0.0)
            return carry

        lax.fori_loop(0, HG_N, bwd_step, 0, unroll=2)
        for t in range(HG_T):
            qt, bt, dot_t = qf_s[:, t, :], b_s[:, t, :], dor_ref[:, t, :]
            for s in range(t + 1):
                w = qt * kf_s[:, s, :]
                if s < t:
                    w = w * jnp.exp(bt - b_s[:, s, :])
                w_s[pl.ds(s * HG_N, HG_N), :] = w.astype(bf16)
                x_s[pl.ds(s * HG_N, HG_N), :] = (dot_t * i_ref[:, s, :]).astype(bf16)
            p = jnp.dot(w_s[pl.ds(0, (t + 1) * HG_N), :], hm, preferred_element_type=f32)
            dp = jnp.dot(x_s[pl.ds(0, (t + 1) * HG_N), :], hm, preferred_element_type=f32)
            dq_t = jnp.zeros((HG_N, MIXW), f32)
            db_t = jnp.zeros((HG_N, MIXW), f32)
            for s in range(t + 1):
                ps = p[s * HG_N:(s + 1) * HG_N]
                dps = dp[s * HG_N:(s + 1) * HG_N]
                ks = kf_s[:, s, :]
                dv_s[:, s, :] = dv_s[:, s, :] + ps * dot_t
                if s < t:
                    dec_ts = jnp.exp(bt - b_s[:, s, :])
                    g1 = dps * ks * dec_ts
                    dk_s[:, s, :] = dk_s[:, s, :] + dps * qt * dec_ts
                    gw = g1 * qt
                    db_t = db_t + gw
                    db_s[:, s, :] = db_s[:, s, :] - gw
                else:
                    g1 = dps * ks
                    dk_s[:, s, :] = dk_s[:, s, :] + dps * qt
                dq_t = dq_t + g1
            dq_s[:, t, :] = dq_s[:, t, :] + dq_t
            db_s[:, t, :] = db_s[:, t, :] + db_t
        run = jnp.zeros((HG_N, MIXW), f32)
        dlb = jnp.zeros((1, MIXW), f32)
        oml = 1.0 - lb
        for t in range(HG_T - 1, -1, -1):
            run = run + db_s[:, t, :]
            q = q_ref[:, t, :]
            _, _, _, sq, sg, f = _hg_gates(q, f_ref[:, t, :], lb)
            dkf = dk_s[:, t, :]
            df = jnp.where(f > LOG_FLOOR, run / f, 0.0)
            dsg = (df - dkf) * oml
            dlb = dlb + jnp.sum((df - dkf) * (1.0 - sg), axis=0, keepdims=True)
            dz_ref[:, t, 0:MIXW] = (dq_s[:, t, :] * sq * (1.0 + q * (1.0 - sq))).astype(bf16)
            dz_ref[:, t, MIXW:2 * MIXW] = (dsg * sg * (1.0 - sg)).astype(bf16)
            dz_ref[:, t, 2 * MIXW:3 * MIXW] = dv_s[:, t, :].astype(bf16)
        dlb_ref[...] = jnp.broadcast_to(dlb, (8, MIXW))

    one = pl.Buffered(1)
    col = lambda j: pl.BlockSpec((HG_N, HG_T, MIXW), lambda i: (0, 0, j), pipeline_mode=one)
    s3 = pltpu.VMEM((HG_N, HG_T, MIXW), f32)
    return pl.pallas_call(
        body, name="hgrn_bwd", grid=(1,),
        in_specs=[col(3), col(4), col(5), pl.BlockSpec((1, MIXW), lambda i: (0, 0)),
                  pl.BlockSpec((HG_N, HG_T, MIXW), lambda i: (0, 0, 0), pipeline_mode=one),
                  pl.BlockSpec((HG_N, MIXW, MIXW), lambda i: (0, 0, 0), pipeline_mode=one)],
        out_specs=[pl.BlockSpec((HG_N, HG_T, 3 * MIXW), lambda i: (0, 0, 0)), pl.BlockSpec((8, MIXW), lambda i: (0, 0))],
        out_shape=[_sds((HG_N, HG_T, 3 * MIXW), bf16), _sds((8, MIXW), f32)],
        scratch_shapes=[s3, s3, s3, s3, s3, s3, s3,
                        pltpu.VMEM((HG_T * HG_N, MIXW), bf16), pltpu.VMEM((HG_T * HG_N, MIXW), bf16),
                        pltpu.VMEM((MIXW, MIXW), f32)],
        compiler_params=_params(("arbitrary",)),
    )(zm3, zm3, zm3, lb, dor3, states)


def _gmlp_bwd(zm, ng, ws, bs8, do):
    def body(u_ref, v_ref, ng_ref, ws_ref, bs_ref, do_ref, dz_ref, dws_ref, dng_ref, dbs_ref, dm_s):
        hm = _head_masks()
        tril, wts = _gm_weights(ws_ref)
        bias = _gm_bias(bs_ref)
        ngv = ng_ref[...]
        dws_ref[...] = jnp.zeros_like(dws_ref)
        dm_s[...] = jnp.zeros_like(dm_s)

        def blk(n, dng):
            rows = pl.ds(pl.multiple_of(n * GM_T, GM_T), GM_T)
            cu, cv, d = u_ref[rows, :], v_ref[rows, :], do_ref[rows, :]
            v = _gelu(cv)
            r = lax.rsqrt(jnp.mean(v * v, axis=-1, keepdims=True) + EPS)
            vh = v * r
            vn = vh * ngv
            u = _gelu(cu)
            dm = d * u
            dmb, vnb = dm.astype(bf16), vn.astype(bf16)
            dvn = jnp.zeros((GM_T, MIXW), f32)
            for g in range(NHEAD):
                dws_ref[g] = dws_ref[g] + _dot_nt(jnp.where(hm[g], dm, 0.0), vnb)
                dvn = dvn + jnp.where(hm[g], _dot_tn(wts[g], dmb), 0.0)
            dm_s[...] = dm_s[...] + dm
            dvh = dvn * ngv
            dv = r * (dvh - vh * jnp.mean(dvh * vh, axis=-1, keepdims=True))
            dz_ref[rows, 0:MIXW] = (d * _gm_mixed(vn, wts, bias, hm) * _dgelu(cu)).astype(bf16)
            dz_ref[rows, MIXW:2 * MIXW] = (dv * _dgelu(cv)).astype(bf16)
            return dng + jnp.sum(dvn * vh, axis=0, keepdims=True)

        dng = lax.fori_loop(0, SEQ // GM_T, blk, jnp.zeros((1, MIXW), f32))
        dng_ref[...] = jnp.broadcast_to(dng, (8, MIXW))
        for g in range(NHEAD):
            dws_ref[g] = jnp.where(tril, dws_ref[g], 0.0)
        dbs_ref[...] = _dot_nt_hl(_gm_expand(), dm_s[...])

    col = lambda j: pl.BlockSpec((SEQ, MIXW), lambda i: (0, j))
    return pl.pallas_call(
        body, name="gmlp_bwd", grid=(1,),
        in_specs=[col(7), col(8), pl.BlockSpec((1, MIXW), lambda i: (0, 0)),
                  pl.BlockSpec((NHEAD, GM_T, GM_T), lambda i: (0, 0, 0)), pl.BlockSpec((8, GM_T), lambda i: (0, 0)),
                  pl.BlockSpec((SEQ, MIXW), lambda i: (0, 0))],
        out_specs=[pl.BlockSpec((SEQ, 2 * MIXW), lambda i: (0, 0)), pl.BlockSpec((NHEAD, GM_T, GM_T), lambda i: (0, 0, 0)),
                   pl.BlockSpec((8, MIXW), lambda i: (0, 0)), pl.BlockSpec((8, GM_T), lambda i: (0, 0))],
        out_shape=[_sds((SEQ, 2 * MIXW), bf16), _sds((NHEAD, GM_T, GM_T), f32), _sds((8, MIXW), f32), _sds((8, GM_T), f32)],
        scratch_shapes=[pltpu.VMEM((GM_T, MIXW), f32)],
        compiler_params=_params(("arbitrary",)),
    )(zm, zm, ng, ws, bs8, do)


def _lru_bwd(zm, cw8, cb, wa, ba, wx, bx, lam, hd, do):
    nchunk = SEQ // LRU_T

    def body(x_ref, g_ref, cw_ref, cb_ref, wa_ref, ba_ref, wx_ref, bx_ref, lam_ref, h_ref, do_ref,
             dz_ref, dwa_ref, dwx_ref, dcw_ref, dvec_ref, xp_s, xc_s, dxc_s):
        _lru_conv(x_ref, cw_ref, cb_ref, xp_s, xc_s)
        lam_v = lam_ref[...]
        sp = jax.nn.softplus(-lam_v)
        sgl = _sigmoid(-lam_v)
        wa_v, wx_v, ba_v, bx_v = wa_ref[...], wx_ref[...], ba_ref[...], bx_ref[...]
        dwa_ref[...] = jnp.zeros_like(dwa_ref)
        dwx_ref[...] = jnp.zeros_like(dwx_ref)
        dxc_s[pl.ds(SEQ, 8), :] = jnp.zeros((8, MIXW), f32)
        row = lax.broadcasted_iota(jnp.int32, (LRU_T, 1), 0)
        zero = jnp.zeros((1, MIXW), f32)

        def chunk(j, carry):
            dh_next, a_next, dba, dbx, dlam = carry
            c = nchunk - 1 - j
            rows = pl.ds(pl.multiple_of(c * LRU_T, LRU_T), LRU_T)
            prev = pl.ds(pl.multiple_of(jnp.maximum(c - 1, 0) * LRU_T, LRU_T), LRU_T)
            first = (row + c * LRU_T) == 0
            xc, gate, d, h = xc_s[rows, :], g_ref[rows, :], do_ref[rows, :], h_ref[rows, :]
            a, mult, r, ig, m2 = _lru_gates(xc, wa_v, ba_v, wx_v, bx_v, sp, first)
            h_last = jnp.where(c > 0, h_ref[prev, :][LRU_T - 1:LRU_T, :], 0.0)
            h_m1 = jnp.where(row == 0, h_last, pltpu.roll(h, 1, 0))
            a_up = jnp.where(row == LRU_T - 1, a_next, pltpu.roll(a, LRU_T - 1, 0))
            acum, dh_loc = _lru_scan(a_up, d * _gelu(gate), True)
            dh = dh_loc + acum * dh_next
            dmult = jnp.where(first, 0.0, dh * (ig * xc))
            msq = jnp.sqrt(jnp.maximum(m2, 0.0))
            dla = dh * h_m1 * a + jnp.where(m2 > 0.0, -dmult * (1.0 - m2) / msq, 0.0)
            dpr = dla * (-LRU_C) * sp * r * (1.0 - r)
            dpi = dh * mult * xc * ig * (1.0 - ig)
            dxc_s[rows, :] = dh * mult * ig + _dot_nt(dpr, wa_v) + _dot_nt(dpi, wx_v)
            dwa_ref[...] = dwa_ref[...] + _dot_tn(xc, dpr)
            dwx_ref[...] = dwx_ref[...] + _dot_tn(xc, dpi)
            dz_ref[rows, MIXW:2 * MIXW] = (d * h * _dgelu(gate)).astype(bf16)
            return (dh[0:1], a[0:1], dba + jnp.sum(dpr, axis=0, keepdims=True), dbx + jnp.sum(dpi, axis=0, keepdims=True),
                    dlam + jnp.sum(dla * r, axis=0, keepdims=True) * (LRU_C * sgl))

        _, _, dba, dbx, dlam = lax.fori_loop(0, nchunk, chunk, (zero, zero, zero, zero, zero))
        cw = cw_ref[...]
        dxc = dxc_s[pl.ds(0, SEQ), :]
        dx = dxc * cw[3:4]
        dcw = [None] * 4
        dcw[3] = jnp.sum(dxc * x_ref[...], axis=0, keepdims=True)
        for k in range(1, 4):
            dx = dx + dxc_s[pl.ds(k, SEQ), :] * cw[3 - k:4 - k]
            dcw[3 - k] = jnp.sum(dxc * xp_s[pl.ds(8 - k, SEQ), :], axis=0, keepdims=True)
        dz_ref[:, 0:MIXW] = dx.astype(bf16)
        dcw_ref[...] = jnp.concatenate(dcw + [jnp.zeros((4, MIXW), f32)], axis=0)
        dvec_ref[...] = jnp.concatenate([jnp.sum(dxc, axis=0, keepdims=True), dba, dbx, dlam, jnp.zeros((4, MIXW), f32)], axis=0)

    col = lambda j: pl.BlockSpec((SEQ, MIXW), lambda i: (0, j))
    vec = pl.BlockSpec((1, MIXW), lambda i: (0, 0))
    vec8 = pl.BlockSpec((8, MIXW), lambda i: (0, 0))
    mat = pl.BlockSpec((MIXW, MIXW), lambda i: (0, 0))
    full = pl.BlockSpec((SEQ, MIXW), lambda i: (0, 0))
    return pl.pallas_call(
        body, name="lru_bwd", grid=(1,),
        in_specs=[col(9), col(10), vec8, vec, mat, vec, mat, vec, vec, full, full],
        out_specs=[pl.BlockSpec((SEQ, 2 * MIXW), lambda i: (0, 0)), mat, mat, vec8, vec8],
        out_shape=[_sds((SEQ, 2 * MIXW), bf16), _sds((MIXW, MIXW), f32), _sds((MIXW, MIXW), f32),
                   _sds((8, MIXW), f32), _sds((8, MIXW), f32)],
        scratch_shapes=[pltpu.VMEM((SEQ + 8, MIXW), f32), pltpu.VMEM((SEQ, MIXW), f32), pltpu.VMEM((SEQ + 8, MIXW), f32)],
        compiler_params=_params(("arbitrary",)),
    )(zm, zm, cw8, cb, wa, ba, wx, bx, lam, hd, do)


def _matmul_tn(a, b, tm, tn, b_col0=0):
    m = a.shape[1]
    n = tn if b_col0 else b.shape[1]
    off = b_col0 // tn

    def body(a_ref, b_ref, o_ref):
        o_ref[...] = _dot_tn(a_ref[...], b_ref[...]).astype(bf16)

    return pl.pallas_call(
        body, name="matmul_tn", grid=(m // tm, n // tn),
        in_specs=[pl.BlockSpec((SEQ, tm), lambda i, j: (0, i)), pl.BlockSpec((SEQ, tn), lambda i, j: (0, j + off))],
        out_specs=pl.BlockSpec((tm, tn), lambda i, j: (i, j)),
        out_shape=_sds((m, n), bf16),
        compiler_params=_params(("parallel", "arbitrary")),
    )(a, b)


def _matmul_nt_norm(pairs, x, g, dres):
    tm = 1024
    steps = [a.shape[1] // t for a, _, t in pairs]
    starts = [sum(steps[:i]) for i in range(len(pairs))]
    total = sum(steps)
    npair = len(pairs)

    def body(*refs):
        a_refs, w_refs = refs[0:2 * npair:2], refs[1:2 * npair:2]
        x_ref, g_ref, dres_ref, dx_ref, dg_ref, acc_s = refs[2 * npair:]
        i, k = pl.program_id(0), pl.program_id(1)

        @pl.when(k == 0)
        def _():
            acc_s[...] = jnp.zeros_like(acc_s)

        @pl.when((i == 0) & (k == 0))
        def _():
            dg_ref[...] = jnp.zeros_like(dg_ref)

        for q in range(npair):
            @pl.when((k >= starts[q]) & (k < starts[q] + steps[q]))
            def _(q=q):
                acc_s[...] += _dot_nt(a_refs[q][...], w_refs[q][...])

        @pl.when(k == total - 1)
        def _():
            dx, dg = _rms_bwd(x_ref[...], g_ref[...], acc_s[...])
            dx_ref[...] = dres_ref[...] + dx
            dg_ref[...] += dg

    in_specs, args = [], []
    for q, (a, w, t) in enumerate(pairs):
        kmap = lambda k, q=q: jnp.clip(k - starts[q], 0, steps[q] - 1)
        in_specs += [pl.BlockSpec((tm, t), lambda i, k, kmap=kmap: (i, kmap(k))),
                     pl.BlockSpec((DM, t), lambda i, k, kmap=kmap: (0, kmap(k)))]
        args += [a, w]
    row = pl.BlockSpec((tm, DM), lambda i, k: (i, 0))
    vec = pl.BlockSpec((1, DM), lambda i, k: (0, 0))
    return pl.pallas_call(
        body, name="matmul_nt_norm", grid=(SEQ // tm, total),
        in_specs=in_specs + [row, vec, row], out_specs=[row, vec],
        out_shape=[_sds((SEQ, DM), f32), _sds((1, DM), f32)],
        scratch_shapes=[pltpu.VMEM((tm, DM), f32)],
        compiler_params=_params(("arbitrary", "arbitrary")),
    )(*args, x, g, dres)


def _merge_bwd(dx1, y, g2, outs, zg, wb, wo):
    def body(dx_ref, y_ref, g_ref, oa_ref, ob_ref, oc_ref, od_ref, zg_ref, wb_ref, wo_ref,
             da_ref, db_ref, dc_ref, dd_ref, dzg_ref, dpj_ref, dy_ref, dg_ref):
        @pl.when(pl.program_id(0) == 0)
        def _():
            dg_ref[...] = jnp.zeros_like(dg_ref)

        dy, dg = _rms_bwd(y_ref[...], g_ref[...], dx_ref[...])
        dg_ref[...] += dg
        dyb = dy.astype(bf16)
        dy_ref[...] = dyb
        dmerged = _dot_nt(dyb, wo_ref[...])
        for n, (o_ref, do_ref) in enumerate(((oa_ref, da_ref), (ob_ref, db_ref), (oc_ref, dc_ref), (od_ref, dd_ref))):
            cols = slice(n * DM, (n + 1) * DM)
            gate = _sigmoid(zg_ref[:, cols])
            proj = jnp.dot(o_ref[...], wb_ref[n], preferred_element_type=f32)
            dproj = (dmerged * gate).astype(bf16)
            dpj_ref[:, cols] = dproj
            dzg_ref[:, cols] = (dmerged * proj * gate * (1.0 - gate)).astype(bf16)
            do_ref[...] = _dot_nt(dproj, wb_ref[n])

    row = lambda w: pl.BlockSpec((ROW_TILE, w), lambda i: (i, 0))
    vec = pl.BlockSpec((1, DM), lambda i: (0, 0))
    return pl.pallas_call(
        body, name="merge_bwd", grid=(SEQ // ROW_TILE,),
        in_specs=[row(DM), row(DM), vec] + [row(MIXW)] * 4 + [row(NGATE), pl.BlockSpec((NHEAD, MIXW, DM), lambda i: (0, 0, 0)),
                                                              pl.BlockSpec((DM, DM), lambda i: (0, 0))],
        out_specs=[row(MIXW)] * 4 + [row(NGATE), row(NGATE), row(DM), vec],
        out_shape=[_sds((SEQ, MIXW), f32)] * 4 + [_sds((SEQ, NGATE), bf16), _sds((SEQ, NGATE), bf16), _sds((SEQ, DM), bf16),
                                                  _sds((1, DM), f32)],
        compiler_params=_params(("arbitrary",)),
    )(dx1, y, g2, *outs, zg, wb, wo)


def _ffn_bwd(dx2, f, g4, u, w2):
    def body(dx_ref, f_ref, g_ref, u_ref, w_ref, du_ref, a_ref, df_ref, dg_ref):
        @pl.when(pl.program_id(0) == 0)
        def _():
            dg_ref[...] = jnp.zeros_like(dg_ref)

        df, dg = _rms_bwd(f_ref[...], g_ref[...], dx_ref[...])
        dg_ref[...] += dg
        dfb = df.astype(bf16)
        df_ref[...] = dfb
        da = _dot_nt(dfb, w_ref[...])
        gt, up = u_ref[:, :FFH], u_ref[:, FFH:]
        a_ref[...] = (_silu(gt) * up).astype(bf16)
        du_ref[:, :FFH] = (da * up * _dsilu(gt)).astype(bf16)
        du_ref[:, FFH:] = (da * _silu(gt)).astype(bf16)

    row = lambda w: pl.BlockSpec((ROW_TILE, w), lambda i: (i, 0))
    vec = pl.BlockSpec((1, DM), lambda i: (0, 0))
    return pl.pallas_call(
        body, name="ffn_bwd", grid=(SEQ // ROW_TILE,),
        in_specs=[row(DM), row(DM), vec, row(2 * FFH), pl.BlockSpec((FFH, DM), lambda i: (0, 0))],
        out_specs=[row(2 * FFH), row(FFH), row(DM), vec],
        out_shape=[_sds((SEQ, 2 * FFH), bf16), _sds((SEQ, FFH), bf16), _sds((SEQ, DM), bf16), _sds((1, DM), f32)],
        compiler_params=_params(("arbitrary",)),
    )(dx2, f, g4, u, w2)


def _layer_bwd(dx2, p, sv, ffn_grads_ready=None, mix_grads_ready=None):
    du, act, df, dg4 = _ffn_bwd(dx2, sv["f"], p["g4"], sv["u"], p["w2"])
    dw2 = _matmul_tn(act, df, 1408, DM)
    dx1, dg3 = _matmul_nt_norm([(du, p["w1"], 1408)], sv["x1"], p["g3"], dx2)
    dw1 = _matmul_tn(sv["h2"], du, DM, 1408)
    g2 = p["g2"]
    if ffn_grads_ready is not None:
        g2 = g2 + ffn_grads_ready(dict(w_ffn_in=dw1, w_ffn_out=dw2), dx1)
    *dos, dzg, dproj, dy, dg2 = _merge_bwd(dx1, sv["y"], g2, sv["outs"], sv["zg"], p["wb"], p["wo"])
    dwo = _matmul_tn(sv["merged"], dy, DM, DM)
    dwb = jnp.stack([_matmul_tn(sv["outs"][n], dproj, MIXW, DM, b_col0=n * DM) if n else
                     _matmul_tn(sv["outs"][0], dproj[:, :DM], MIXW, DM) for n in range(NHEAD)])
    zm = sv["zm"]
    zm3 = zm.reshape(HG_N, HG_T, NMIX)
    dza, drb = _attn_bwd(zm, p["rb8"], dos[0])
    dor, dgb, dhng = _hgrn_out_bwd(zm3, p["hng"], sv["obraw3"], dos[1].reshape(HG_N, HG_T, MIXW))
    dzb, dlb = _hgrn_bwd(zm3, p["lb"], dor, sv["hstates"])
    dzc, dws, dgng, dbs = _gmlp_bwd(zm, p["gng"], p["gws"], p["gbs8"], dos[2])
    dzd, dwa, dwx, dcw, dvec = _lru_bwd(zm, p["cw8"], p["cb"], p["wa"], p["ba"], p["wx"], p["bx"], p["lam"], sv["hd"], dos[3])
    dzm = jnp.concatenate([dza, dzb.reshape(SEQ, 3 * MIXW), dgb.reshape(SEQ, MIXW), dzc, dzd], axis=1)
    dwin = jnp.concatenate([_matmul_tn(sv["h"], dzm, DM, 1408), _matmul_tn(sv["h"], dzg, DM, 1024)], axis=1)
    big = dict(w_in=dwin, w_branch=dwb, w_out=dwo, w_ffn_in=dw1, w_ffn_out=dw2)
    g1 = p["g1"]
    if mix_grads_ready is not None:
        g1 = g1 + mix_grads_ready(big)
    dx0, dg1 = _matmul_nt_norm([(dzm, p["wm"], 1408), (dzg, p["wgt"], 1024)], sv["x"], g1, dx1)
    small = dict(
        norm_mix_pre=dg1[0], norm_mix_post=dg2[0], norm_ffn_pre=dg3[0], norm_ffn_post=dg4[0],
        attn_rel_bias=drb[:NHEAD], lb=dlb[0], hgrn_norm_g=dhng[0], gmlp_norm_g=dgng[0], gmlp_ws=dws, gmlp_bs=dbs[:NHEAD],
        lru_conv_w=dcw[:NHEAD], lru_conv_b=dvec[0], lru_wa=_diag_blocks(dwa), lru_ba=dvec[1], lru_wx=_diag_blocks(dwx),
        lru_bx=dvec[2], lru_lambda=dvec[3],
    )
    return dx0, big, small


MIX_BIG = ("w_in", "w_branch", "w_out")
FFN_BIG = ("w_ffn_in", "w_ffn_out")
BIG = MIX_BIG + FFN_BIG
SMALL = ("norm_mix_pre", "norm_mix_post", "norm_ffn_pre", "norm_ffn_post", "attn_rel_bias", "hgrn_lb_logits", "hgrn_norm_g",
         "gmlp_norm_g", "gmlp_ws", "gmlp_bs", "lru_conv_w", "lru_conv_b", "lru_wa", "lru_ba", "lru_wx", "lru_bx", "lru_lambda")


def _local_step(x, tgt, full, small):
    lbs = _lb_fwd(small["hgrn_lb_logits"])
    params, saved = [], []
    for l in range(DEPTH):
        p = _layer_params(l, {k: full[k][l] for k in BIG}, small, lbs)
        x, sv = _layer_fwd(x, p)
        params.append(p)
        saved.append(sv)
    loss, dx = _loss_head(x, tgt)
    bigs, smalls = [None] * DEPTH, [None] * DEPTH
    for l in range(DEPTH - 1, -1, -1):
        dx, bigs[l], smalls[l] = _layer_bwd(dx, params[l], saved[l])
    gbig = {k: jnp.stack([bigs[l][k] for l in range(DEPTH)]) for k in BIG}
    gsmall = {k: jnp.stack([smalls[l][k] for l in range(DEPTH)]) for k in smalls[0]}
    gsmall["hgrn_lb_logits"] = _lb_bwd(small["hgrn_lb_logits"], gsmall.pop("lb"))
    return loss, dx, gbig, gsmall


HBM_ANY = pl.BlockSpec(memory_space=pl.ANY)


def _mesh_pos():
    return lax.axis_index("x"), lax.axis_index("y"), lax.axis_index("c")


def _all_gather(x, name):
    def body(x_ref, out_ref, send_sems, recv_sems, local_sem):
        ax, ay, ac = _mesh_pos()
        me, sibling = (ax, ay, ac), (ax, ay, 1 - ac)
        chips = [(1 - ax, ay), (ax, 1 - ay), (1 - ax, 1 - ay)]

        def slot(px, py, pc):
            return out_ref.at[4 * px + 2 * py + pc]

        def copy(k, block, to, src=None):
            return pltpu.make_async_remote_copy(
                src_ref=slot(*block) if src is None else src, dst_ref=slot(*block),
                send_sem=send_sems.at[k], recv_sem=recv_sems.at[k], device_id=to, device_id_type=MESH_ID)

        mine = pltpu.make_async_copy(x_ref, slot(*me), local_sem)
        mine.start()
        first = [copy(0, me, sibling, src=x_ref)]
        first += [copy(1 + j, me, (*chip, ac), src=x_ref) for j, chip in enumerate(chips)]
        for cp in first:
            cp.start()
        passed = [copy(4 + j, (*chip, ac), sibling) for j, chip in enumerate(chips)]
        for j, chip in enumerate(chips):
            copy(1 + j, (*chip, ac), me).wait_recv()
            passed[j].start()
        copy(0, sibling, me).wait_recv()
        for j, chip in enumerate(chips):
            copy(4 + j, (*chip, 1 - ac), me).wait_recv()
        for cp in first + passed:
            cp.wait_send()
        mine.wait()

    return pl.pallas_call(
        body, name=name, out_shape=_sds((NDEV,) + x.shape, x.dtype),
        in_specs=[HBM_ANY], out_specs=HBM_ANY,
        scratch_shapes=[pltpu.SemaphoreType.DMA((7,)), pltpu.SemaphoreType.DMA((7,)), pltpu.SemaphoreType.DMA],
    )(x)


def _exchange(g, name):
    def body(g_ref, out_ref, send_sems, recv_sems, local_sem):
        ax, ay, ac = _mesh_pos()
        me = 4 * ax + 2 * ay + ac
        mine = pltpu.make_async_copy(g_ref.at[me], out_ref.at[me], local_sem)
        mine.start()
        copies = []
        for k in range(1, NDEV):
            px = 1 - ax if k & 4 else ax
            py = 1 - ay if k & 2 else ay
            pc = 1 - ac if k & 1 else ac
            copies.append(pltpu.make_async_remote_copy(
                src_ref=g_ref.at[4 * px + 2 * py + pc], dst_ref=out_ref.at[me],
                send_sem=send_sems.at[k - 1], recv_sem=recv_sems.at[k - 1], device_id=(px, py, pc), device_id_type=MESH_ID))
        for cp in copies:
            cp.start()
        for cp in copies:
            cp.wait()
        mine.wait()

    return pl.pallas_call(
        body, name=name, out_shape=_sds(g.shape, g.dtype),
        in_specs=[HBM_ANY], out_specs=HBM_ANY,
        scratch_shapes=[pltpu.SemaphoreType.DMA((7,)), pltpu.SemaphoreType.DMA((7,)), pltpu.SemaphoreType.DMA],
    )(g)


def _peer(ax, ay, ac, k):
    return (1 - ax if k & 4 else ax, 1 - ay if k & 2 else ay, 1 - ac if k & 1 else ac)


def _handshake(peers):
    barrier = pltpu.get_barrier_semaphore()
    for peer in peers:
        pl.semaphore_signal(barrier, inc=1, device_id=peer, device_id_type=MESH_ID)
    pl.semaphore_wait(barrier, len(peers))


SEQUENCER = dict(axis_name="seq", num_cores=1)
GATHER_ID = 1
EXCHANGE_ID = 2


def _gather_sc(xs, name):
    n = len(xs)

    def body(*refs):
        srcs, outs = refs[:n], refs[n:2 * n]
        send_sems, recv_sems, local_sems = refs[2 * n:]
        ax, ay, ac = _mesh_pos()
        me, sibling = (ax, ay, ac), (ax, ay, 1 - ac)
        chips = [(1 - ax, ay), (ax, 1 - ay), (1 - ax, 1 - ay)]
        _handshake([sibling] + [(*chip, ac) for chip in chips])

        def slot(i, px, py, pc):
            return outs[i].at[4 * px + 2 * py + pc]

        def copy(i, k, block, to, src=None):
            return pltpu.make_async_remote_copy(
                src_ref=slot(i, *block) if src is None else src, dst_ref=slot(i, *block),
                send_sem=send_sems.at[7 * i + k], recv_sem=recv_sems.at[7 * i + k], device_id=to, device_id_type=MESH_ID)

        mine = [pltpu.make_async_copy(srcs[i], slot(i, *me), local_sems.at[i]) for i in range(n)]
        first = []
        for i in range(n):
            first += [copy(i, 1 + j, me, (*chip, ac), src=srcs[i]) for j, chip in enumerate(chips)]
        for i in range(n):
            first += [copy(i, 0, me, sibling, src=srcs[i])]
        for cp in first + mine:
            cp.start()
        passed = []
        for i in range(n):
            for j, chip in enumerate(chips):
                copy(i, 1 + j, (*chip, ac), me).wait_recv()
                passed.append(copy(i, 4 + j, (*chip, ac), sibling))
                passed[-1].start()
        for i in range(n):
            copy(i, 0, sibling, me).wait_recv()
            for j, chip in enumerate(chips):
                copy(i, 4 + j, (*chip, 1 - ac), me).wait_recv()
        for cp in first + passed:
            cp.wait_send()
        for cp in mine:
            cp.wait()

    return pl.kernel(
        body, name=name, out_type=[_sds((NDEV,) + x.shape, x.dtype) for x in xs],
        mesh=plsc.ScalarSubcoreMesh(**SEQUENCER),
        scratch_types=[pltpu.SemaphoreType.DMA((7 * n,)), pltpu.SemaphoreType.DMA((7 * n,)), pltpu.SemaphoreType.DMA((n,))],
        compiler_params=pltpu.CompilerParams(collective_id=GATHER_ID),
    )(*xs)


def _exchange_sc(gs, name):
    n = len(gs)

    def body(*refs):
        srcs, outs = refs[:n], refs[n:2 * n]
        send_sems, recv_sems, local_sems = refs[2 * n:]
        ax, ay, ac = _mesh_pos()
        me = 4 * ax + 2 * ay + ac
        peers = [_peer(ax, ay, ac, k) for k in range(1, NDEV)]
        _handshake(peers)
        mine = [pltpu.make_async_copy(srcs[i].at[me], outs[i].at[me], local_sems.at[i]) for i in range(n)]
        copies = []
        for i in range(n):
            for k, (px, py, pc) in enumerate(peers):
                copies.append(pltpu.make_async_remote_copy(
                    src_ref=srcs[i].at[4 * px + 2 * py + pc], dst_ref=outs[i].at[me],
                    send_sem=send_sems.at[7 * i + k], recv_sem=recv_sems.at[7 * i + k],
                    device_id=(px, py, pc), device_id_type=MESH_ID))
        for cp in copies + mine:
            cp.start()
        for cp in copies + mine:
            cp.wait()

    return pl.kernel(
        body, name=name, out_type=[_sds(g.shape, g.dtype) for g in gs],
        mesh=plsc.ScalarSubcoreMesh(**SEQUENCER),
        scratch_types=[pltpu.SemaphoreType.DMA((7 * n,)), pltpu.SemaphoreType.DMA((7 * n,)), pltpu.SemaphoreType.DMA((n,))],
        compiler_params=pltpu.CompilerParams(collective_id=EXCHANGE_ID),
    )(*gs)


HBM_SPEC = pl.BlockSpec(memory_space=pltpu.HBM)
SEM_SPEC = pl.BlockSpec(memory_space=pltpu.SEMAPHORE)
DATAFLOW = pltpu.SideEffectType.DATAFLOW_SIDE_EFFECTING


def _exchange_copies(srcs, lands, send_sems, recv_sems, local_sems):
    n = len(srcs)
    ax, ay, ac = _mesh_pos()
    me = 4 * ax + 2 * ay + ac
    copies = [pltpu.make_async_copy(srcs[i].at[me], lands[i].at[me], local_sems.at[i]) for i in range(n)]
    for i in range(n):
        for k in range(1, NDEV):
            px, py, pc = _peer(ax, ay, ac, k)
            copies.append(pltpu.make_async_remote_copy(
                src_ref=srcs[i].at[4 * px + 2 * py + pc], dst_ref=lands[i].at[me],
                send_sem=send_sems.at[7 * i + k - 1], recv_sem=recv_sems.at[7 * i + k - 1],
                device_id=(px, py, pc), device_id_type=MESH_ID))
    return copies


def _exchange_start(gs, name):
    n = len(gs)

    def body(*refs):
        srcs, lands = refs[:n], refs[n:2 * n]
        send_sems, recv_sems, local_sems = refs[2 * n:2 * n + 3]
        token = refs[-1]
        for cp in _exchange_copies(srcs, lands, send_sems, recv_sems, local_sems):
            cp.start()
        token[...] = jnp.zeros_like(token)

    hbm = [pltpu.HBM(g.shape, g.dtype) for g in gs]
    outs = pl.pallas_call(
        body, name=name,
        out_shape=(pltpu.SemaphoreType.DMA((7 * n,)), pltpu.SemaphoreType.DMA((7 * n,)), pltpu.SemaphoreType.DMA((n,)),
                   *hbm, *hbm, _sds((8, 128), f32)),
        in_specs=[HBM_SPEC] * (2 * n),
        out_specs=(SEM_SPEC, SEM_SPEC, SEM_SPEC, *[HBM_SPEC] * (2 * n), pl.BlockSpec(memory_space=pltpu.VMEM)),
        input_output_aliases={i: 3 + i for i in range(2 * n)},
        compiler_params=pltpu.CompilerParams(has_side_effects=DATAFLOW),
    )(*[pltpu.with_memory_space_constraint(g, pltpu.HBM) for g in gs],
      *[pltpu.with_memory_space_constraint(lax.empty(g.shape, g.dtype), pltpu.HBM) for g in gs])
    return outs[:-1], outs[-1]


def _exchange_wait(handles, after, name):
    n = (len(handles) - 3) // 2
    send_sems, recv_sems, local_sems = handles[:3]
    srcs, lands = handles[3:3 + n], handles[3 + n:]

    def body(*refs):
        srcs, lands = refs[:n], refs[n:2 * n]
        send_sems, recv_sems, local_sems = refs[2 * n:2 * n + 3]
        for cp in _exchange_copies(srcs, lands, send_sems, recv_sems, local_sems):
            cp.wait()

    hbm = [pltpu.HBM(g.shape, g.dtype) for g in srcs]
    outs = pl.pallas_call(
        body, name=name, out_shape=(*hbm, *hbm),
        in_specs=[HBM_SPEC] * (2 * n) + [SEM_SPEC] * 3 + [pl.BlockSpec(memory_space=pl.ANY)],
        out_specs=tuple([HBM_SPEC] * (2 * n)),
        input_output_aliases={i: i for i in range(2 * n)},
        compiler_params=pltpu.CompilerParams(has_side_effects=DATAFLOW),
    )(*srcs, *lands, send_sems, recv_sems, local_sems, after)
    return outs[n:]


def _row_tile(rows, cols):
    cap = max(8, (1 << 18) // cols)
    if rows <= cap:
        return rows
    best = None
    for t in range(8, cap + 1, 8):
        if rows % t == 0:
            best = t
    assert best is not None, (rows, cols)
    return best


def _sum_parts(parts, name):
    npart, rows, cols = parts.shape
    tr = _row_tile(rows, cols)

    def body(p_ref, o_ref):
        g = p_ref[0].astype(f32)
        for j in range(1, npart):
            g = g + p_ref[j].astype(f32)
        o_ref[...] = g

    return pl.pallas_call(
        body, name=name, grid=(rows // tr,),
        in_specs=[pl.BlockSpec((npart, tr, cols), lambda i: (0, i, 0))], out_specs=pl.BlockSpec((tr, cols), lambda i: (i, 0)),
        out_shape=_sds((rows, cols), f32), compiler_params=_params(("parallel",)),
    )(parts)


def _adamw(parts, w, m, v, name):
    npart, rows, cols = parts.shape
    tr = _row_tile(rows, cols)
    c1 = 1.0 / (1.0 - ADAM_B1 ** ADAM_STEP)
    c2 = 1.0 / (1.0 - ADAM_B2 ** ADAM_STEP)

    def body(p_ref, w_ref, m_ref, v_ref, g_ref, d_ref, mo_ref, vo_ref):
        g = p_ref[0].astype(f32)
        for j in range(1, npart):
            g = g + p_ref[j].astype(f32)
        mn = ADAM_B1 * m_ref[...] + (1.0 - ADAM_B1) * g
        vn = ADAM_B2 * v_ref[...] + (1.0 - ADAM_B2) * (g * g)
        g_ref[...] = g
        mo_ref[...] = mn
        vo_ref[...] = vn
        d_ref[...] = (-ADAM_LR) * ((mn * c1) / (jnp.sqrt(vn * c2) + ADAM_EPS) + ADAM_WD * w_ref[...])

    blk = pl.BlockSpec((tr, cols), lambda i: (i, 0))
    return pl.pallas_call(
        body, name=name, grid=(rows // tr,),
        in_specs=[pl.BlockSpec((npart, tr, cols), lambda i: (0, i, 0)), blk, blk, blk], out_specs=[blk] * 4,
        out_shape=[_sds((rows, cols), f32)] * 4, compiler_params=_params(("parallel",)),
    )(parts, w, m, v)


def _adamw_layer(parts, w, m, v, acc, l, name):
    npart, rows, cols = parts.shape
    tr = _row_tile(rows, cols)
    c1 = 1.0 / (1.0 - ADAM_B1 ** ADAM_STEP)
    c2 = 1.0 / (1.0 - ADAM_B2 ** ADAM_STEP)

    def body(p_ref, w_ref, m_ref, v_ref, *refs):
        g_ref, d_ref, mo_ref, vo_ref = refs[-4:]
        g = p_ref[0].astype(f32)
        for j in range(1, npart):
            g = g + p_ref[j].astype(f32)
        mn = ADAM_B1 * m_ref[...] + (1.0 - ADAM_B1) * g
        vn = ADAM_B2 * v_ref[...] + (1.0 - ADAM_B2) * (g * g)
        g_ref[...] = g
        mo_ref[...] = mn
        vo_ref[...] = vn
        d_ref[...] = (-ADAM_LR) * ((mn * c1) / (jnp.sqrt(vn * c2) + ADAM_EPS) + ADAM_WD * w_ref[...])

    blk = pl.BlockSpec((None, tr, cols), lambda i: (l, i, 0))
    prev = [] if acc is None else list(acc)
    return pl.pallas_call(
        body, name=name, grid=(rows // tr,),
        in_specs=[pl.BlockSpec((npart, tr, cols), lambda i: (0, i, 0)), blk, blk, blk] + [HBM_ANY] * len(prev),
        out_specs=[blk] * 4, out_shape=[_sds(w.shape, f32)] * 4,
        input_output_aliases={4 + j: j for j in range(len(prev))},
        compiler_params=_params(("parallel",)),
    )(parts, w, m, v, *prev)


def _pack(arrays):
    rows = []
    for a in arrays:
        flat = a.reshape(-1)
        pad = (-flat.shape[0]) % 1024
        rows.append(jnp.concatenate([flat, jnp.zeros((pad,), flat.dtype)]).reshape(-1, 128))
    return jnp.concatenate(rows, axis=0)


def _unpack(flat, shapes):
    out, r = [], 0
    for s in shapes:
        n = math.prod(s)
        nr = (n + 1023) // 1024 * 8
        out.append(flat[r:r + nr].reshape(-1)[:n].reshape(s))
        r += nr
    return out


BIG_SHARD_AXIS = dict(w_in=2, w_branch=3, w_out=1, w_ffn_in=2, w_ffn_out=1)
SHARDED_SMALL = ("attn_rel_bias", "lru_conv_w")


def _to_blocks(full, axis):
    s = full.shape
    cut = full.reshape(s[:axis] + (NDEV, s[axis] // NDEV) + s[axis + 1:])
    return jnp.moveaxis(cut, axis, 0)


def _from_blocks(blocks, axis):
    moved = jnp.moveaxis(blocks, 0, axis)
    s = moved.shape
    return moved.reshape(s[:axis] + (s[axis] * s[axis + 1],) + s[axis + 2:])


def _flat2(a):
    return a.reshape(-1, a.shape[-1])


def _my_slice(a, n):
    ax, ay, ac = _mesh_pos()
    return lax.dynamic_slice_in_dim(a, (4 * ax + 2 * ay + ac) * n, n, axis=a.ndim - 1)


_WEIGHTS = ("norm_mix_pre", "norm_mix_post", "norm_ffn_pre", "norm_ffn_post", "w_in", "attn_rel_bias", "hgrn_lb_logits",
            "hgrn_norm_g", "gmlp_norm_g", "gmlp_ws", "gmlp_bs", "lru_conv_w", "lru_conv_b", "lru_wa", "lru_ba", "lru_wx",
            "lru_bx", "lru_lambda", "w_branch", "w_out", "w_ffn_in", "w_ffn_out")


def _step(x, loss_target, w, m, v):
    gathered = []
    for l in range(DEPTH):
        shard = lambda k: w[k][l].astype(bf16)
        if l == 0:
            mix = (list(_gather_sc([shard("w_in")], "gather_in0"))
                   + list(_gather_sc([shard(k) for k in MIX_BIG[1:]], "gather_mix0")))
        else:
            mix = _gather_sc([shard(k) for k in MIX_BIG], "gather_mix%d" % l)
        gathered.append((mix, _gather_sc([shard(k) for k in FFN_BIG], "gather_ffn%d" % l)))
    cut = jnp.concatenate([w[k] for k in SHARDED_SMALL], axis=-1)
    parts = _all_gather(_pack([cut]), "gather_small").reshape(NDEV, -1)[:, :math.prod(cut.shape)].reshape((NDEV,) + cut.shape)
    small = {k: w[k] for k in SMALL if k not in SHARDED_SMALL}
    at = 0
    for k in SHARDED_SMALL:
        n = w[k].shape[-1]
        small[k] = _from_blocks(parts[..., at:at + n], 2)
        at += n
    loss, dx, layers = _step_forward(x, loss_target, gathered, small)
    flat3 = lambda a: a.reshape((DEPTH, -1, a.shape[-1]))
    acc = {k: None for k in BIG}
    smalls = [None] * DEPTH

    def send(grads, keys, name):
        handles, token = _exchange_start([_to_blocks(grads[k], BIG_SHARD_AXIS[k] - 1) for k in keys], "start_" + name)
        return (keys, handles, "wait_" + name), token[0:1, 0:1]

    def update(sent, l, after):
        keys, handles, name = sent
        got = dict(zip(keys, _exchange_wait(handles, after, name)))
        for k, g in got.items():
            w3 = flat3(w[k])
            acc[k] = _adamw_layer(g.reshape((NDEV,) + w3.shape[1:]), w3, flat3(m[k]), flat3(v[k]), acc[k], l,
                                  "adamw_%s_%d" % (k, l))

    waiting = []
    for l in range(DEPTH - 1, -1, -1):
        sent_ffn = []

        def ffn_grads_ready(grads, dx1, l=l, sent_ffn=sent_ffn):
            sent, zero = send(grads, FFN_BIG, "exchange_ffn%d" % l)
            sent_ffn.append(sent)
            while waiting:
                update(*waiting.pop(), dx1)
            return zero

        sent_mix = []

        def mix_grads_ready(grads, l=l, sent_mix=sent_mix):
            sent, zero = send(grads, MIX_BIG, "exchange_mix%d" % l)
            sent_mix.append(sent)
            return zero

        dx, _, smalls[l] = _step_backward(dx, layers[l], ffn_grads_ready, mix_grads_ready)
        update(sent_ffn[0], l, dx)
        waiting.append((sent_mix[0], l))
    grads, deltas, new_m, new_v = {}, {}, {}, {}
    gsmall ={k: jnp.stack([smalls[l][k] for l in range(DEPTH)]) for k in smalls[0]}
    gsmall["hgrn_lb_logits"] = _lb_bwd(small["hgrn_lb_logits"], gsmall.pop("lb"))
    shapes = [gsmall[k].shape for k in SMALL]
    pieces = _pack([gsmall[k] for k in SMALL])
    gathered_pieces = _gather_sc([pieces], "gather_small_grads")[0]
    update(*waiting.pop(), pieces)
    sums = _unpack(_sum_parts(gathered_pieces, "sum_small_grads"), shapes)
    gs = dict(zip(SMALL, sums))
    for k in SHARDED_SMALL:
        gs[k] = _my_slice(gs[k], w[k].shape[-1])
    packed = [_pack([d[k] for k in SMALL]) for d in (gs, w, m, v)]
    outs = _adamw(packed[0][None], packed[1], packed[2], packed[3], "adamw_small")
    shapes = [w[k].shape for k in SMALL]
    for d, o in zip((grads, deltas, new_m, new_v), outs):
        d.update(zip(SMALL, _unpack(o, shapes)))
    for k in BIG:
        grads[k], deltas[k], new_m[k], new_v[k] = (o.reshape(w[k].shape) for o in acc[k])
    total = lax.psum(loss[0, 0], ("x", "y", "c"))
    return total, dx[None], grads, deltas, new_m, new_v


def _step_forward(x, loss_target, gathered, small):
    lbs = _lb_fwd(small["hgrn_lb_logits"])
    x = x[0]
    layers = []

    def weights(blocks, keys, after):
        if after is not None:
            blocks, _ = lax.optimization_barrier((blocks, after))
        return {k: _from_blocks(g, BIG_SHARD_AXIS[k] - 1) for k, g in zip(keys, blocks)}

    for l in range(DEPTH):
        mix, ffn = gathered[l]
        p = _layer_params(l, weights(mix, MIX_BIG, x if l else None), small, lbs)
        x, sv = _layer_fwd(x, p, lambda x1, ffn=ffn: _ffn_weights(weights(ffn, FFN_BIG, x1)))
        layers.append((p, sv))
    loss, dx = _loss_head(x, loss_target[0])
    return loss, dx, layers


def _step_backward(dx, layer, ffn_grads_ready, mix_grads_ready):
    return _layer_bwd(dx, *layer, ffn_grads_ready, mix_grads_ready)


def kernel(x, norm_mix_pre, norm_mix_post, norm_ffn_pre, norm_ffn_post, w_in, attn_rel_bias, hgrn_lb_logits, hgrn_norm_g, gmlp_norm_g, gmlp_ws, gmlp_bs, lru_conv_w, lru_conv_b, lru_wa, lru_ba, lru_wx, lru_bx, lru_lambda, w_branch, w_out, w_ffn_in, w_ffn_out, loss_target, m_norm_mix_pre, m_norm_mix_post, m_norm_ffn_pre, m_norm_ffn_post, m_w_in, m_attn_rel_bias, m_hgrn_lb_logits, m_hgrn_norm_g, m_gmlp_norm_g, m_gmlp_ws, m_gmlp_bs, m_lru_conv_w, m_lru_conv_b, m_lru_wa, m_lru_ba, m_lru_wx, m_lru_bx, m_lru_lambda, m_w_branch, m_w_out, m_w_ffn_in, m_w_ffn_out, v_norm_mix_pre, v_norm_mix_post, v_norm_ffn_pre, v_norm_ffn_post, v_w_in, v_attn_rel_bias, v_hgrn_lb_logits, v_hgrn_norm_g, v_gmlp_norm_g, v_gmlp_ws, v_gmlp_bs, v_lru_conv_w, v_lru_conv_b, v_lru_wa, v_lru_ba, v_lru_wx, v_lru_bx, v_lru_lambda, v_w_branch, v_w_out, v_w_ffn_in, v_w_ffn_out):
    w = dict(zip(_WEIGHTS, (norm_mix_pre, norm_mix_post, norm_ffn_pre, norm_ffn_post, w_in, attn_rel_bias, hgrn_lb_logits, hgrn_norm_g, gmlp_norm_g, gmlp_ws, gmlp_bs, lru_conv_w, lru_conv_b, lru_wa, lru_ba, lru_wx, lru_bx, lru_lambda, w_branch, w_out, w_ffn_in, w_ffn_out)))
    m = dict(zip(_WEIGHTS, (m_norm_mix_pre, m_norm_mix_post, m_norm_ffn_pre, m_norm_ffn_post, m_w_in, m_attn_rel_bias, m_hgrn_lb_logits, m_hgrn_norm_g, m_gmlp_norm_g, m_gmlp_ws, m_gmlp_bs, m_lru_conv_w, m_lru_conv_b, m_lru_wa, m_lru_ba, m_lru_wx, m_lru_bx, m_lru_lambda, m_w_branch, m_w_out, m_w_ffn_in, m_w_ffn_out)))
    v = dict(zip(_WEIGHTS, (v_norm_mix_pre, v_norm_mix_post, v_norm_ffn_pre, v_norm_ffn_post, v_w_in, v_attn_rel_bias, v_hgrn_lb_logits, v_hgrn_norm_g, v_gmlp_norm_g, v_gmlp_ws, v_gmlp_bs, v_lru_conv_w, v_lru_conv_b, v_lru_wa, v_lru_ba, v_lru_wx, v_lru_bx, v_lru_lambda, v_w_branch, v_w_out, v_w_ffn_in, v_w_ffn_out)))
    loss, grad_x, grads, deltas, new_m, new_v = _step(x, loss_target, w, m, v)
    return (loss, grad_x, *[grads[k] for k in _WEIGHTS], *[deltas[k] for k in _WEIGHTS],
            *[new_m[k] for k in _WEIGHTS], *[new_v[k] for k in _WEIGHTS])
```
